```python
import jax, jax.numpy as jnp
from jax import lax
import numpy as np

D_MODEL = 1024
BATCH = 8
SEQ = 8192
DEPTH = 2

N_A_LAYERS = DEPTH // 2
N_B_LAYERS = DEPTH - N_A_LAYERS
FOX_HEADS = 16
FOX_HEAD_DIM = D_MODEL // FOX_HEADS
FOX_WIDTH = FOX_HEADS * FOX_HEAD_DIM
FOX_IN_COLS = 3 * FOX_WIDTH + FOX_HEADS
MLA_HEADS = 8
QK_NOPE_DIM = 128
QK_ROPE_DIM = 64
V_HEAD_DIM = 128
Q_LORA_RANK = 384
KV_LORA_RANK = 256
ROPE_BASE = 10000.0
D_FF = 4 * D_MODEL
Q_BLOCK = 128
EPS = 1e-6

kernel_name = "yoco_fox_mla_hybrid"


def rms_norm(x, g):
    xf = x.astype(jnp.float32)
    y = xf * lax.rsqrt(jnp.mean(xf * xf, axis=-1, keepdims=True) + EPS)
    return (y * g.astype(jnp.float32)).astype(x.dtype)


def sq_relu_mlp(h, w_up, w_down):
    return jnp.square(jax.nn.relu(h @ w_up)) @ w_down


def rope_tables(seq_len, dim):
    inv = 1.0 / (ROPE_BASE ** (jnp.arange(0, dim, 2, dtype=jnp.float32) / dim))
    ang = jnp.arange(seq_len, dtype=jnp.float32)[:, None] * inv[None, :]
    return jnp.cos(ang), jnp.sin(ang)


def apply_rope(t, cos, sin):
    cos = cos.astype(t.dtype)
    sin = sin.astype(t.dtype)
    half = t.shape[-1] // 2
    t1, t2 = t[..., :half], t[..., half:]
    return jnp.concatenate([t1 * cos - t2 * sin, t2 * cos + t1 * sin], axis=-1)


def causal_block_attention(logits_fn, q_parts, v):
    B, S = v.shape[0], v.shape[1]
    nb = S // Q_BLOCK
    blocks = tuple(jnp.moveaxis(t.reshape((B, nb, Q_BLOCK) + t.shape[2:]), 1, 0) for t in q_parts)
    kpos = jnp.arange(S)

    def one_block(args):
        i, qb = args
        logits = logits_fn(qb)
        qpos = i * Q_BLOCK + jnp.arange(Q_BLOCK)
        logits = jnp.where(kpos[None, :] <= qpos[:, None], logits, -jnp.inf)
        p = jax.nn.softmax(logits, axis=-1).astype(v.dtype)
        return jnp.einsum('bhqk,bkhd->bqhd', p, v)

    out = lax.map(one_block, (jnp.arange(nb), blocks))
    return jnp.moveaxis(out, 0, 1).reshape((B, S) + out.shape[3:])


def fox_mixer(h, w_in, b_f, w_out):
    B, S, _ = h.shape
    proj = h @ w_in
    q = proj[..., :FOX_WIDTH].reshape(B, S, FOX_HEADS, FOX_HEAD_DIM)
    k = proj[..., FOX_WIDTH:2 * FOX_WIDTH].reshape(B, S, FOX_HEADS, FOX_HEAD_DIM)
    v = proj[..., 2 * FOX_WIDTH:3 * FOX_WIDTH].reshape(B, S, FOX_HEADS, FOX_HEAD_DIM)
    f_logit = proj[..., 3 * FOX_WIDTH:].astype(jnp.float32) + b_f.astype(jnp.float32)
    cum = jnp.cumsum(jax.nn.log_sigmoid(f_logit), axis=1)
    c_keys = jnp.transpose(cum, (0, 2, 1))
    scale = FOX_HEAD_DIM ** -0.5

    def logits_fn(qb):
        q_blk, c_blk = qb
        s = jnp.einsum('bqhd,bkhd->bhqk', q_blk, k, preferred_element_type=jnp.float32) * scale
        return s + jnp.transpose(c_blk, (0, 2, 1))[..., None] - c_keys[:, :, None, :]

    ctx = causal_block_attention(logits_fn, (q, cum), v)
    return ctx.reshape(B, S, FOX_WIDTH) @ w_out


def mla_shared_kv(stream, kv_norm_g, w_kv_a, kv_a_norm_g, w_kv_b, cos, sin):
    B, S, _ = stream.shape
    src = rms_norm(stream, kv_norm_g)
    kv_a = src @ w_kv_a
    c_kv = rms_norm(kv_a[..., :KV_LORA_RANK], kv_a_norm_g)
    k_rope = apply_rope(kv_a[..., KV_LORA_RANK:], cos, sin)
    kv_b = (c_kv @ w_kv_b).reshape(B, S, MLA_HEADS, QK_NOPE_DIM + V_HEAD_DIM)
    k_nope = kv_b[..., :QK_NOPE_DIM]
    v = kv_b[..., QK_NOPE_DIM:]
    return k_nope, k_rope, v


def mla_mixer(h, w_q_a, q_a_norm_g, w_q_b, w_out, k_nope, k_rope, v, cos, sin):
    B, S, _ = h.shape
    c_q = rms_norm(h @ w_q_a, q_a_norm_g)
    q = (c_q @ w_q_b).reshape(B, S, MLA_HEADS, QK_NOPE_DIM + QK_ROPE_DIM)
    q_nope = q[..., :QK_NOPE_DIM]
    q_rope = apply_rope(q[..., QK_NOPE_DIM:], cos[:, None, :], sin[:, None, :])
    scale = (QK_NOPE_DIM + QK_ROPE_DIM) ** -0.5

    def logits_fn(qb):
        qn, qr = qb
        s = jnp.einsum('bqhd,bkhd->bhqk', qn, k_nope, preferred_element_type=jnp.float32)
        s = s + jnp.einsum('bqhr,bkr->bhqk', qr, k_rope, preferred_element_type=jnp.float32)
        return s * scale

    ctx = causal_block_attention(logits_fn, (q_nope, q_rope), v)
    return ctx.reshape(B, S, MLA_HEADS * V_HEAD_DIM) @ w_out


def _fwd_setup_inputs(seed: int = 0) -> dict:
    key = jax.random.key(seed)
    ks = jax.random.split(key, 24)

    def w(k, shape, fan_in):
        return jax.random.normal(k, shape, jnp.float32) * (fan_in ** -0.5)

    def gain(k, shape):
        return 1.0 + 0.02 * jax.random.normal(k, shape, jnp.float32)

    return {
        "x": jax.random.normal(ks[0], (BATCH, SEQ, D_MODEL), jnp.float32),
        "norm_mix_g": gain(ks[1], (DEPTH, D_MODEL)),
        "norm_ffn_g": gain(ks[2], (DEPTH, D_MODEL)),
        "fox_w_in": w(ks[3], (N_A_LAYERS, D_MODEL, FOX_IN_COLS), D_MODEL),
        "fox_b_f": 1.0 + 0.1 * jax.random.normal(ks[4], (N_A_LAYERS, FOX_HEADS), jnp.float32),
        "fox_w_out": w(ks[5], (N_A_LAYERS, FOX_WIDTH, D_MODEL), FOX_WIDTH),
        "kv_norm_g": gain(ks[6], (D_MODEL,)),
        "mla_w_kv_a": w(ks[7], (D_MODEL, KV_LORA_RANK + QK_ROPE_DIM), D_MODEL),
        "mla_kv_a_norm_g": gain(ks[8], (KV_LORA_RANK,)),
        "mla_w_kv_b": w(ks[9], (KV_LORA_RANK, MLA_HEADS * (QK_NOPE_DIM + V_HEAD_DIM)), KV_LORA_RANK),
        "mla_w_q_a": w(ks[10], (N_B_LAYERS, D_MODEL, Q_LORA_RANK), D_MODEL),
        "mla_q_a_norm_g": gain(ks[11], (N_B_LAYERS, Q_LORA_RANK)),
        "mla_w_q_b": w(ks[12], (N_B_LAYERS, Q_LORA_RANK, MLA_HEADS * (QK_NOPE_DIM + QK_ROPE_DIM)), Q_LORA_RANK),
        "mla_w_out": w(ks[13], (N_B_LAYERS, MLA_HEADS * V_HEAD_DIM, D_MODEL), MLA_HEADS * V_HEAD_DIM),
        "ffn_w_up": w(ks[14], (DEPTH, D_MODEL, D_FF), D_MODEL),
        "ffn_w_down": w(ks[15], (DEPTH, D_FF, D_MODEL), D_FF),
        "final_norm_g": gain(ks[16], (D_MODEL,)),
    }


def _fwd_reference(x, norm_mix_g, norm_ffn_g, fox_w_in, fox_b_f, fox_w_out, kv_norm_g,
              mla_w_kv_a, mla_kv_a_norm_g, mla_w_kv_b, mla_w_q_a, mla_q_a_norm_g,
              mla_w_q_b, mla_w_out, ffn_w_up, ffn_w_down, final_norm_g):
    S = x.shape[1]
    cos, sin = rope_tables(S, QK_ROPE_DIM)
    k_nope = k_rope = v_shared = None
    for layer in range(DEPTH):
        h = rms_norm(x, norm_mix_g[layer])
        if layer < N_A_LAYERS:
            x = x + fox_mixer(h, fox_w_in[layer], fox_b_f[layer], fox_w_out[layer])
        else:
            b = layer - N_A_LAYERS
            x = x + mla_mixer(h, mla_w_q_a[b], mla_q_a_norm_g[b], mla_w_q_b[b], mla_w_out[b],
                              k_nope, k_rope, v_shared, cos, sin)
        x = x + sq_relu_mlp(rms_norm(x, norm_ffn_g[layer]), ffn_w_up[layer], ffn_w_down[layer])
        if layer == N_A_LAYERS - 1:
            k_nope, k_rope, v_shared = mla_shared_kv(x, kv_norm_g, mla_w_kv_a, mla_kv_a_norm_g,
                                                     mla_w_kv_b, cos, sin)
    return rms_norm(x, final_norm_g)


import jax as _jax
import jax.numpy as _jnp

TWIN_FORMAT = 'train_step'
FWD_PARAMS = ['x', 'norm_mix_g', 'norm_ffn_g', 'fox_w_in', 'fox_b_f', 'fox_w_out', 'kv_norm_g', 'mla_w_kv_a', 'mla_kv_a_norm_g', 'mla_w_kv_b', 'mla_w_q_a', 'mla_q_a_norm_g', 'mla_w_q_b', 'mla_w_out', 'ffn_w_up', 'ffn_w_down', 'final_norm_g']
TWIN_WEIGHTS = ['norm_mix_g', 'norm_ffn_g', 'fox_w_in', 'fox_b_f', 'fox_w_out', 'kv_norm_g', 'mla_w_kv_a', 'mla_kv_a_norm_g', 'mla_w_kv_b', 'mla_w_q_a', 'mla_q_a_norm_g', 'mla_w_q_b', 'mla_w_out', 'ffn_w_up', 'ffn_w_down', 'final_norm_g']
TWIN_DIFF_INPUT = 'x'
TWIN_INPUTS = ['x', 'norm_mix_g', 'norm_ffn_g', 'fox_w_in', 'fox_b_f', 'fox_w_out', 'kv_norm_g', 'mla_w_kv_a', 'mla_kv_a_norm_g', 'mla_w_kv_b', 'mla_w_q_a', 'mla_q_a_norm_g', 'mla_w_q_b', 'mla_w_out', 'ffn_w_up', 'ffn_w_down', 'final_norm_g', 'loss_target', 'm_norm_mix_g', 'm_norm_ffn_g', 'm_fox_w_in', 'm_fox_b_f', 'm_fox_w_out', 'm_kv_norm_g', 'm_mla_w_kv_a', 'm_mla_kv_a_norm_g', 'm_mla_w_kv_b', 'm_mla_w_q_a', 'm_mla_q_a_norm_g', 'm_mla_w_q_b', 'm_mla_w_out', 'm_ffn_w_up', 'm_ffn_w_down', 'm_final_norm_g', 'v_norm_mix_g', 'v_norm_ffn_g', 'v_fox_w_in', 'v_fox_b_f', 'v_fox_w_out', 'v_kv_norm_g', 'v_mla_w_kv_a', 'v_mla_kv_a_norm_g', 'v_mla_w_kv_b', 'v_mla_w_q_a', 'v_mla_q_a_norm_g', 'v_mla_w_q_b', 'v_mla_w_out', 'v_ffn_w_up', 'v_ffn_w_down', 'v_final_norm_g']
TWIN_OUTPUTS = ['loss', 'grad_x', 'grad_norm_mix_g', 'grad_norm_ffn_g', 'grad_fox_w_in', 'grad_fox_b_f', 'grad_fox_w_out', 'grad_kv_norm_g', 'grad_mla_w_kv_a', 'grad_mla_kv_a_norm_g', 'grad_mla_w_kv_b', 'grad_mla_w_q_a', 'grad_mla_q_a_norm_g', 'grad_mla_w_q_b', 'grad_mla_w_out', 'grad_ffn_w_up', 'grad_ffn_w_down', 'grad_final_norm_g', 'delta_norm_mix_g', 'delta_norm_ffn_g', 'delta_fox_w_in', 'delta_fox_b_f', 'delta_fox_w_out', 'delta_kv_norm_g', 'delta_mla_w_kv_a', 'delta_mla_kv_a_norm_g', 'delta_mla_w_kv_b', 'delta_mla_w_q_a', 'delta_mla_q_a_norm_g', 'delta_mla_w_q_b', 'delta_mla_w_out', 'delta_ffn_w_up', 'delta_ffn_w_down', 'delta_final_norm_g', 'new_m_norm_mix_g', 'new_m_norm_ffn_g', 'new_m_fox_w_in', 'new_m_fox_b_f', 'new_m_fox_w_out', 'new_m_kv_norm_g', 'new_m_mla_w_kv_a', 'new_m_mla_kv_a_norm_g', 'new_m_mla_w_kv_b', 'new_m_mla_w_q_a', 'new_m_mla_q_a_norm_g', 'new_m_mla_w_q_b', 'new_m_mla_w_out', 'new_m_ffn_w_up', 'new_m_ffn_w_down', 'new_m_final_norm_g', 'new_v_norm_mix_g', 'new_v_norm_ffn_g', 'new_v_fox_w_in', 'new_v_fox_b_f', 'new_v_fox_w_out', 'new_v_kv_norm_g', 'new_v_mla_w_kv_a', 'new_v_mla_kv_a_norm_g', 'new_v_mla_w_kv_b', 'new_v_mla_w_q_a', 'new_v_mla_q_a_norm_g', 'new_v_mla_w_q_b', 'new_v_mla_w_out', 'new_v_ffn_w_up', 'new_v_ffn_w_down', 'new_v_final_norm_g']
TWIN_LEAF_KINDS = {'loss': 'loss', 'grad_x': 'grad_x', 'grad_norm_mix_g': 'grad_w', 'grad_norm_ffn_g': 'grad_w', 'grad_fox_w_in': 'grad_w', 'grad_fox_b_f': 'grad_w', 'grad_fox_w_out': 'grad_w', 'grad_kv_norm_g': 'grad_w', 'grad_mla_w_kv_a': 'grad_w', 'grad_mla_kv_a_norm_g': 'grad_w', 'grad_mla_w_kv_b': 'grad_w', 'grad_mla_w_q_a': 'grad_w', 'grad_mla_q_a_norm_g': 'grad_w', 'grad_mla_w_q_b': 'grad_w', 'grad_mla_w_out': 'grad_w', 'grad_ffn_w_up': 'grad_w', 'grad_ffn_w_down': 'grad_w', 'grad_final_norm_g': 'grad_w', 'delta_norm_mix_g': 'delta_w', 'delta_norm_ffn_g': 'delta_w', 'delta_fox_w_in': 'delta_w', 'delta_fox_b_f': 'delta_w', 'delta_fox_w_out': 'delta_w', 'delta_kv_norm_g': 'delta_w', 'delta_mla_w_kv_a': 'delta_w', 'delta_mla_kv_a_norm_g': 'delta_w', 'delta_mla_w_kv_b': 'delta_w', 'delta_mla_w_q_a': 'delta_w', 'delta_mla_q_a_norm_g': 'delta_w', 'delta_mla_w_q_b': 'delta_w', 'delta_mla_w_out': 'delta_w', 'delta_ffn_w_up': 'delta_w', 'delta_ffn_w_down': 'delta_w', 'delta_final_norm_g': 'delta_w', 'new_m_norm_mix_g': 'new_m', 'new_m_norm_ffn_g': 'new_m', 'new_m_fox_w_in': 'new_m', 'new_m_fox_b_f': 'new_m', 'new_m_fox_w_out': 'new_m', 'new_m_kv_norm_g': 'new_m', 'new_m_mla_w_kv_a': 'new_m', 'new_m_mla_kv_a_norm_g': 'new_m', 'new_m_mla_w_kv_b': 'new_m', 'new_m_mla_w_q_a': 'new_m', 'new_m_mla_q_a_norm_g': 'new_m', 'new_m_mla_w_q_b': 'new_m', 'new_m_mla_w_out': 'new_m', 'new_m_ffn_w_up': 'new_m', 'new_m_ffn_w_down': 'new_m', 'new_m_final_norm_g': 'new_m', 'new_v_norm_mix_g': 'new_v', 'new_v_norm_ffn_g': 'new_v', 'new_v_fox_w_in': 'new_v', 'new_v_fox_b_f': 'new_v', 'new_v_fox_w_out': 'new_v', 'new_v_kv_norm_g': 'new_v', 'new_v_mla_w_kv_a': 'new_v', 'new_v_mla_kv_a_norm_g': 'new_v', 'new_v_mla_w_kv_b': 'new_v', 'new_v_mla_w_q_a': 'new_v', 'new_v_mla_q_a_norm_g': 'new_v', 'new_v_mla_w_q_b': 'new_v', 'new_v_mla_w_out': 'new_v', 'new_v_ffn_w_up': 'new_v', 'new_v_ffn_w_down': 'new_v', 'new_v_final_norm_g': 'new_v'}


def _forward(args):
    return _fwd_reference(*[args[k] for k in FWD_PARAMS])


def _output_shape():
    def fwd():
        inp = _fwd_setup_inputs(0)
        return _fwd_reference(*[inp[k] for k in FWD_PARAMS])
    out = _jax.eval_shape(fwd)
    return out.shape, out.dtype

N_MICROBATCH = 1
ADAM_LR = 0.001
ADAM_B1 = 0.9
ADAM_B2 = 0.999
ADAM_EPS = 1e-08
ADAM_WD = 0.01
ADAM_STEP = 10
PER_EXAMPLE_BATCH_AXIS = {'x': 0, 'loss_target': 0}
SHARED_INPUTS = []
_WEIGHT_DTYPES = {'norm_mix_g': _jnp.float32, 'norm_ffn_g': _jnp.float32, 'fox_w_in': _jnp.float32, 'fox_b_f': _jnp.float32, 'fox_w_out': _jnp.float32, 'kv_norm_g': _jnp.float32, 'mla_w_kv_a': _jnp.float32, 'mla_kv_a_norm_g': _jnp.float32, 'mla_w_kv_b': _jnp.float32, 'mla_w_q_a': _jnp.float32, 'mla_q_a_norm_g': _jnp.float32, 'mla_w_q_b': _jnp.float32, 'mla_w_out': _jnp.float32, 'ffn_w_up': _jnp.float32, 'ffn_w_down': _jnp.float32, 'final_norm_g': _jnp.float32}
MOMENT_SCALE = {'norm_mix_g': 1.332826e-01, 'norm_ffn_g': 2.102518e-01, 'fox_w_in': 1.090281e-01, 'fox_b_f': 6.320335e-01, 'fox_w_out': 1.370262e-01, 'kv_norm_g': 1.029102e-01, 'mla_w_kv_a': 1.654937e-01, 'mla_kv_a_norm_g': 1.818086e-01, 'mla_w_kv_b': 7.010298e-02, 'mla_w_q_a': 3.970884e-02, 'mla_q_a_norm_g': 3.673377e-02, 'mla_w_q_b': 1.996564e-02, 'mla_w_out': 1.027097e-01, 'ffn_w_up': 1.025460e-01, 'ffn_w_down': 2.439615e-01, 'final_norm_g': 6.562862e+01}


def _to_microbatches(a, axis):
    t = _jnp.moveaxis(a, axis, 0)
    t = t.reshape((N_MICROBATCH, t.shape[0] // N_MICROBATCH) + t.shape[1:])
    return _jnp.moveaxis(t, 1, axis + 1)


def setup_inputs(seed: int = 0) -> dict:
    inp = _fwd_setup_inputs(seed)
    key = _jax.random.fold_in(_jax.random.key(seed), 7919)
    shape, _ = _output_shape()
    out = dict(inp)
    out["loss_target"] = _jax.random.normal(_jax.random.fold_in(key, 0), shape, _jnp.float32)
    for i, name in enumerate(TWIN_WEIGHTS):
        w = inp[name].astype(_jnp.float32)
        if MOMENT_SCALE is None:
            s = _jnp.sqrt(_jnp.mean(_jnp.square(w)) + 1e-30)
        else:
            s = MOMENT_SCALE[name]
        km, kv = _jax.random.split(_jax.random.fold_in(key, i + 1))
        out[name] = w
        out["m_" + name] = s * _jax.random.normal(km, w.shape, _jnp.float32)
        out["v_" + name] = (s * s) * _jax.random.uniform(kv, w.shape, _jnp.float32, 0.5, 1.5)
    if N_MICROBATCH > 1:
        for name, axis in PER_EXAMPLE_BATCH_AXIS.items():
            out[name] = _to_microbatches(out[name], axis)
    return {'x': out['x'], 'norm_mix_g': out['norm_mix_g'], 'norm_ffn_g': out['norm_ffn_g'], 'fox_w_in': out['fox_w_in'], 'fox_b_f': out['fox_b_f'], 'fox_w_out': out['fox_w_out'], 'kv_norm_g': out['kv_norm_g'], 'mla_w_kv_a': out['mla_w_kv_a'], 'mla_kv_a_norm_g': out['mla_kv_a_norm_g'], 'mla_w_kv_b': out['mla_w_kv_b'], 'mla_w_q_a': out['mla_w_q_a'], 'mla_q_a_norm_g': out['mla_q_a_norm_g'], 'mla_w_q_b': out['mla_w_q_b'], 'mla_w_out': out['mla_w_out'], 'ffn_w_up': out['ffn_w_up'], 'ffn_w_down': out['ffn_w_down'], 'final_norm_g': out['final_norm_g'], 'loss_target': out['loss_target'], 'm_norm_mix_g': out['m_norm_mix_g'], 'm_norm_ffn_g': out['m_norm_ffn_g'], 'm_fox_w_in': out['m_fox_w_in'], 'm_fox_b_f': out['m_fox_b_f'], 'm_fox_w_out': out['m_fox_w_out'], 'm_kv_norm_g': out['m_kv_norm_g'], 'm_mla_w_kv_a': out['m_mla_w_kv_a'], 'm_mla_kv_a_norm_g': out['m_mla_kv_a_norm_g'], 'm_mla_w_kv_b': out['m_mla_w_kv_b'], 'm_mla_w_q_a': out['m_mla_w_q_a'], 'm_mla_q_a_norm_g': out['m_mla_q_a_norm_g'], 'm_mla_w_q_b': out['m_mla_w_q_b'], 'm_mla_w_out': out['m_mla_w_out'], 'm_ffn_w_up': out['m_ffn_w_up'], 'm_ffn_w_down': out['m_ffn_w_down'], 'm_final_norm_g': out['m_final_norm_g'], 'v_norm_mix_g': out['v_norm_mix_g'], 'v_norm_ffn_g': out['v_norm_ffn_g'], 'v_fox_w_in': out['v_fox_w_in'], 'v_fox_b_f': out['v_fox_b_f'], 'v_fox_w_out': out['v_fox_w_out'], 'v_kv_norm_g': out['v_kv_norm_g'], 'v_mla_w_kv_a': out['v_mla_w_kv_a'], 'v_mla_kv_a_norm_g': out['v_mla_kv_a_norm_g'], 'v_mla_w_kv_b': out['v_mla_w_kv_b'], 'v_mla_w_q_a': out['v_mla_w_q_a'], 'v_mla_q_a_norm_g': out['v_mla_q_a_norm_g'], 'v_mla_w_q_b': out['v_mla_w_q_b'], 'v_mla_w_out': out['v_mla_w_out'], 'v_ffn_w_up': out['v_ffn_w_up'], 'v_ffn_w_down': out['v_ffn_w_down'], 'v_final_norm_g': out['v_final_norm_g']}


def _loss(weights, diff, rest, loss_target):
    with _jax.named_scope("forward"):
        args = {**rest, TWIN_DIFF_INPUT: diff, **{k: w.astype(_WEIGHT_DTYPES[k]) for k, w in weights.items()}}
        y = _forward(args)
    with _jax.named_scope("loss_head"):
        err = _jnp.square(y.astype(_jnp.float32) - loss_target)
        return 0.5 * _jnp.sum(_jnp.mean(err, axis=-1)) if err.ndim else 0.5 * err


def _adamw(w, g, m, v):
    m = ADAM_B1 * m + (1.0 - ADAM_B1) * g
    v = ADAM_B2 * v + (1.0 - ADAM_B2) * _jnp.square(g)
    m_hat = m / (1.0 - ADAM_B1 ** ADAM_STEP)
    v_hat = v / (1.0 - ADAM_B2 ** ADAM_STEP)
    delta = -ADAM_LR * (m_hat / (_jnp.sqrt(v_hat) + ADAM_EPS) + ADAM_WD * w)
    return delta, m, v


def reference(x, norm_mix_g, norm_ffn_g, fox_w_in, fox_b_f, fox_w_out, kv_norm_g, mla_w_kv_a, mla_kv_a_norm_g, mla_w_kv_b, mla_w_q_a, mla_q_a_norm_g, mla_w_q_b, mla_w_out, ffn_w_up, ffn_w_down, final_norm_g, loss_target, m_norm_mix_g, m_norm_ffn_g, m_fox_w_in, m_fox_b_f, m_fox_w_out, m_kv_norm_g, m_mla_w_kv_a, m_mla_kv_a_norm_g, m_mla_w_kv_b, m_mla_w_q_a, m_mla_q_a_norm_g, m_mla_w_q_b, m_mla_w_out, m_ffn_w_up, m_ffn_w_down, m_final_norm_g, v_norm_mix_g, v_norm_ffn_g, v_fox_w_in, v_fox_b_f, v_fox_w_out, v_kv_norm_g, v_mla_w_kv_a, v_mla_kv_a_norm_g, v_mla_w_kv_b, v_mla_w_q_a, v_mla_q_a_norm_g, v_mla_w_q_b, v_mla_w_out, v_ffn_w_up, v_ffn_w_down, v_final_norm_g):
    given = dict(x=x, norm_mix_g=norm_mix_g, norm_ffn_g=norm_ffn_g, fox_w_in=fox_w_in, fox_b_f=fox_b_f, fox_w_out=fox_w_out, kv_norm_g=kv_norm_g, mla_w_kv_a=mla_w_kv_a, mla_kv_a_norm_g=mla_kv_a_norm_g, mla_w_kv_b=mla_w_kv_b, mla_w_q_a=mla_w_q_a, mla_q_a_norm_g=mla_q_a_norm_g, mla_w_q_b=mla_w_q_b, mla_w_out=mla_w_out, ffn_w_up=ffn_w_up, ffn_w_down=ffn_w_down, final_norm_g=final_norm_g, loss_target=loss_target, m_norm_mix_g=m_norm_mix_g, m_norm_ffn_g=m_norm_ffn_g, m_fox_w_in=m_fox_w_in, m_fox_b_f=m_fox_b_f, m_fox_w_out=m_fox_w_out, m_kv_norm_g=m_kv_norm_g, m_mla_w_kv_a=m_mla_w_kv_a, m_mla_kv_a_norm_g=m_mla_kv_a_norm_g, m_mla_w_kv_b=m_mla_w_kv_b, m_mla_w_q_a=m_mla_w_q_a, m_mla_q_a_norm_g=m_mla_q_a_norm_g, m_mla_w_q_b=m_mla_w_q_b, m_mla_w_out=m_mla_w_out, m_ffn_w_up=m_ffn_w_up, m_ffn_w_down=m_ffn_w_down, m_final_norm_g=m_final_norm_g, v_norm_mix_g=v_norm_mix_g, v_norm_ffn_g=v_norm_ffn_g, v_fox_w_in=v_fox_w_in, v_fox_b_f=v_fox_b_f, v_fox_w_out=v_fox_w_out, v_kv_norm_g=v_kv_norm_g, v_mla_w_kv_a=v_mla_w_kv_a, v_mla_kv_a_norm_g=v_mla_kv_a_norm_g, v_mla_w_kv_b=v_mla_w_kv_b, v_mla_w_q_a=v_mla_w_q_a, v_mla_q_a_norm_g=v_mla_q_a_norm_g, v_mla_w_q_b=v_mla_w_q_b, v_mla_w_out=v_mla_w_out, v_ffn_w_up=v_ffn_w_up, v_ffn_w_down=v_ffn_w_down, v_final_norm_g=v_final_norm_g)
    weights = {n: given[n] for n in TWIN_WEIGHTS}
    shared = {n: given[n] for n in SHARED_INPUTS}
    per_example = {n: given[n] for n in ['x']}
    grad_fn = _jax.value_and_grad(_loss, argnums=(0, 1))

    def one_microbatch(ex, loss_target):
        ex = dict(ex)
        diff = ex.pop(TWIN_DIFF_INPUT)
        return grad_fn(weights, diff, {**shared, **ex}, loss_target)

    if N_MICROBATCH == 1:
        loss, (grad_w, grad_x) = one_microbatch(per_example, given["loss_target"])
    else:
        def body(carry, xs):
            loss_sum, grad_sum = carry
            l_k, (gw_k, gx_k) = one_microbatch(xs[0], xs[1])
            with _jax.named_scope("update"):
                return (loss_sum + l_k, _jax.tree.map(_jnp.add, grad_sum, gw_k)), gx_k

        init = (_jnp.zeros((), _jnp.float32), _jax.tree.map(_jnp.zeros_like, weights))
        (loss, grad_w), grad_x = _jax.lax.scan(body, init, (per_example, given["loss_target"]))
    with _jax.named_scope("update"):
        delta_w, new_m, new_v = {}, {}, {}
        for n in TWIN_WEIGHTS:
            delta_w[n], new_m[n], new_v[n] = _adamw(weights[n], grad_w[n], given["m_" + n], given["v_" + n])
    return (loss, grad_x, *[grad_w[n] for n in TWIN_WEIGHTS], *[delta_w[n] for n in TWIN_WEIGHTS],
            *[new_m[n] for n in TWIN_WEIGHTS], *[new_v[n] for n in TWIN_WEIGHTS])
```

```python
import functools
import math

import numpy as np
import jax
import jax.numpy as jnp
from jax import lax
from jax.experimental import pallas as pl
from jax.experimental.pallas import tpu as pltpu

F32 = jnp.float32
BF16 = jnp.bfloat16

FOX_HEADS = 16
MLA_HEADS = 8
QK_NOPE_DIM = 128
ROPE_BASE = 10000.0
EPS = 1e-6

ADAM_LR = 0.001
ADAM_B1 = 0.9
ADAM_B2 = 0.999
ADAM_EPS = 1e-08
ADAM_WD = 0.01
ADAM_STEP = 10

N_CHIPS = 4
N_DEV = 8
PACK_LANES = 1024
PACK_PART_ROWS = 16
SMALL_PART_ROWS = 8
PACK_ROWS_MULT = 1024
VMEM_LIMIT_BYTES = 48 * 1024 * 1024
NEG_BIG = -1e30
MESH = pl.DeviceIdType.MESH


def _round_up(n, m):
    return -(-n // m) * m


def _blk(dim, pref, mult=128):
    if dim <= pref:
        return dim
    b = (pref // mult) * mult
    while b >= mult:
        if dim % b == 0:
            return b
        b -= mult
    return dim


def _params(sem=None):
    return pltpu.CompilerParams(dimension_semantics=sem, vmem_limit_bytes=VMEM_LIMIT_BYTES)


_DIMS = {"nn": (((1,), (0,)), ((), ())), "nt": (((1,), (1,)), ((), ())), "tn": (((0,), (0,)), ((), ()))}


def _matmul(a, b, *, mode, out_dtypes, name, epilogue=None, extras=(), bm=1024, bn=1024, bk=512):
    if mode == "tn":
        kdim, m = a.shape
    else:
        m, kdim = a.shape
    n = b.shape[0] if mode == "nt" else b.shape[1]
    bm, bn, bk = _blk(m, bm), _blk(n, bn), _blk(kdim, bk)
    nk = kdim // bk
    n_extra, n_out = len(extras), len(out_dtypes)
    dims = _DIMS[mode]

    def body(a_ref, b_ref, *rest):
        extra_refs = rest[:n_extra]
        out_refs = rest[n_extra:n_extra + n_out]
        acc_ref = rest[n_extra + n_out]
        k = pl.program_id(2)

        @pl.when(k == 0)
        def _():
            acc_ref[...] = jnp.zeros_like(acc_ref)

        acc_ref[...] += lax.dot_general(a_ref[...].astype(BF16), b_ref[...].astype(BF16), dims,
                                        preferred_element_type=F32)

        @pl.when(k == nk - 1)
        def _():
            acc = acc_ref[...]
            res = (acc,) if epilogue is None else epilogue(acc, *[r[...] for r in extra_refs])
            for o_ref, r in zip(out_refs, res):
                o_ref[...] = r.astype(o_ref.dtype)

    if mode == "tn":
        a_spec = pl.BlockSpec((bk, bm), lambda i, j, k: (k, i))
    else:
        a_spec = pl.BlockSpec((bm, bk), lambda i, j, k: (i, k))
    if mode == "nt":
        b_spec = pl.BlockSpec((bn, bk), lambda i, j, k: (j, k))
    else:
        b_spec = pl.BlockSpec((bk, bn), lambda i, j, k: (k, j))
    tile = pl.BlockSpec((bm, bn), lambda i, j, k: (i, j))
    outs = pl.pallas_call(
        body, name=name,
        grid=(m // bm, n // bn, nk),
        in_specs=[a_spec, b_spec] + [tile] * n_extra,
        out_specs=[tile] * n_out,
        out_shape=[jax.ShapeDtypeStruct((m, n), dt) for dt in out_dtypes],
        scratch_shapes=[pltpu.VMEM((bm, bn), F32)],
        compiler_params=_params(("parallel", "parallel", "arbitrary")),
    )(a, b, *extras)
    return outs[0] if n_out == 1 else outs


def _rms_fwd(x, gains, name):
    s, w = x.shape
    g = gains.shape[0]
    bs = _blk(s, 512, 8)

    def body(x_ref, g_ref, *out_refs):
        xv = x_ref[...]
        y = xv * lax.rsqrt(jnp.mean(xv * xv, axis=-1, keepdims=True) + EPS)
        for i, o_ref in enumerate(out_refs):
            o_ref[...] = (y * g_ref[i:i + 1, :]).astype(o_ref.dtype)

    row = pl.BlockSpec((bs, w), lambda i: (i, 0))
    outs = pl.pallas_call(
        body, name=name, grid=(s // bs,),
        in_specs=[row, pl.BlockSpec((g, w), lambda i: (0, 0))],
        out_specs=[row] * g,
        out_shape=[jax.ShapeDtypeStruct((s, w), BF16)] * g,
        compiler_params=_params(("parallel",)),
    )(x, gains)
    return outs


def _rms_bwd(x, branches, resid, name):
    s, w = x.shape
    nb = len(branches)
    bs = _blk(s, 512, 8)
    has_resid = resid is not None

    def body(x_ref, *rest):
        g_refs = rest[:nb]
        dy_refs = rest[nb:2 * nb]
        pos = 2 * nb
        r_ref = rest[pos] if has_resid else None
        pos += int(has_resid)
        dx_ref = rest[pos]
        dg_refs = rest[pos + 1:pos + 1 + nb]
        i = pl.program_id(0)

        @pl.when(i == 0)
        def _():
            for dg_ref in dg_refs:
                dg_ref[...] = jnp.zeros_like(dg_ref)

        xv = x_ref[...]
        rstd = lax.rsqrt(jnp.mean(xv * xv, axis=-1, keepdims=True) + EPS)
        xhat = xv * rstd
        dx = r_ref[...] if has_resid else jnp.zeros_like(xv)
        for g_ref, dy_ref, dg_ref in zip(g_refs, dy_refs, dg_refs):
            dy = dy_ref[...].astype(F32)
            dyg = dy * g_ref[...]
            dx = dx + rstd * (dyg - xhat * jnp.mean(dyg * xhat, axis=-1, keepdims=True))
            dg_ref[...] += jnp.sum(dy * xhat, axis=0, keepdims=True)
        dx_ref[...] = dx

    row = pl.BlockSpec((bs, w), lambda i: (i, 0))
    vec = pl.BlockSpec((1, w), lambda i: (0, 0))
    args = [x] + [g for g, _ in branches] + [dy for _, dy in branches] + ([resid] if has_resid else [])
    outs = pl.pallas_call(
        body, name=name, grid=(s // bs,),
        in_specs=[row] + [vec] * nb + [row] * nb + ([row] if has_resid else []),
        out_specs=[row] + [vec] * nb,
        out_shape=[jax.ShapeDtypeStruct((s, w), F32)] + [jax.ShapeDtypeStruct((1, w), F32)] * nb,
        compiler_params=_params(("arbitrary",)),
    )(*args)
    return outs[0], list(outs[1:])


def _loss_head(x, g, target, name):
    s, w = x.shape
    bs = _blk(s, 512, 8)

    def body(x_ref, g_ref, t_ref, loss_ref, dx_ref, dg_ref):
        i = pl.program_id(0)

        @pl.when(i == 0)
        def _():
            loss_ref[...] = jnp.zeros_like(loss_ref)
            dg_ref[...] = jnp.zeros_like(dg_ref)

        xv = x_ref[...]
        gv = g_ref[...]
        rstd = lax.rsqrt(jnp.mean(xv * xv, axis=-1, keepdims=True) + EPS)
        xhat = xv * rstd
        err = xhat * gv - t_ref[...]
        loss_ref[...] += 0.5 * jnp.sum(jnp.mean(err * err, axis=-1, keepdims=True))
        dy = err * (1.0 / w)
        dyg = dy * gv
        dx_ref[...] = rstd * (dyg - xhat * jnp.mean(dyg * xhat, axis=-1, keepdims=True))
        dg_ref[...] += jnp.sum(dy * xhat, axis=0, keepdims=True)

    row = pl.BlockSpec((bs, w), lambda i: (i, 0))
    vec = pl.BlockSpec((1, w), lambda i: (0, 0))
    return pl.pallas_call(
        body, name=name, grid=(s // bs,),
        in_specs=[row, vec, row],
        out_specs=[pl.BlockSpec((8, 128), lambda i: (0, 0)), row, vec],
        out_shape=[jax.ShapeDtypeStruct((8, 128), F32), jax.ShapeDtypeStruct((s, w), F32),
                   jax.ShapeDtypeStruct((1, w), F32)],
        compiler_params=_params(("arbitrary",)),
    )(x, g, target)


def _rope(a, b, cos, sin, sign, name):
    g, s, w = a.shape
    bs = _blk(s, 1024, 8)

    def body(a_ref, b_ref, c_ref, s_ref, o1_ref, o2_ref):
        av = jnp.sum(a_ref[...].astype(F32), axis=0)
        bv = jnp.sum(b_ref[...].astype(F32), axis=0)
        cv, sv = c_ref[...], s_ref[...] * sign
        o1_ref[...] = av * cv - bv * sv
        o2_ref[...] = bv * cv + av * sv

    grp = pl.BlockSpec((g, bs, w), lambda i: (0, i, 0))
    row = pl.BlockSpec((bs, w), lambda i: (i, 0))
    return pl.pallas_call(
        body, name=name, grid=(s // bs,),
        in_specs=[grp, grp, row, row], out_specs=[row, row],
        out_shape=[jax.ShapeDtypeStruct((s, w), F32)] * 2,
        compiler_params=_params(("parallel",)),
    )(a, b, cos, sin)


def _causal_table(s, bq, bk, q_major):
    nq, nk = s // bq, s // bk
    rows = []
    if q_major:
        for qi in range(nq):
            kmax = (qi * bq + bq - 1) // bk
            for ki in range(kmax + 1):
                rows.append((qi, ki, int(ki * bk + bk - 1 > qi * bq), int(ki == 0), int(ki == kmax)))
    else:
        for ki in range(nk):
            qmin = (ki * bk) // bq
            for qi in range(qmin, nq):
                rows.append((qi, ki, int(ki * bk + bk - 1 > qi * bq), int(qi == qmin), int(qi == nq - 1)))
    return jnp.asarray(np.array(rows, np.int32).T)


def _causal_keep(qi, ki, bq, bk, transposed):
    if transposed:
        kpos = ki * bk + lax.broadcasted_iota(jnp.int32, (bk, bq), 0)
        qpos = qi * bq + lax.broadcasted_iota(jnp.int32, (bk, bq), 1)
    else:
        qpos = qi * bq + lax.broadcasted_iota(jnp.int32, (bq, bk), 0)
        kpos = ki * bk + lax.broadcasted_iota(jnp.int32, (bq, bk), 1)
    return kpos <= qpos


_NT = (((1,), (1,)), ((), ()))
_NN = (((1,), (0,)), ((), ()))


def _flash_fwd(q, k, v, scale, bias, name, bq=512, bk=512):
    h, s, dk = q.shape
    dv = v.shape[-1]
    bq, bk = _blk(s, bq), _blk(s, bk)
    tab = _causal_table(s, bq, bk, True)
    has_bias = bias is not None

    def body(tab_ref, q_ref, k_ref, v_ref, *rest):
        if has_bias:
            cq_ref, ck_ref = rest[:2]
            rest = rest[2:]
        o_ref, lse_ref, m_sc, l_sc, acc_sc = rest
        t = pl.program_id(1)
        qi, ki = tab_ref[0, t], tab_ref[1, t]

        @pl.when(tab_ref[3, t] == 1)
        def _():
            m_sc[...] = jnp.full_like(m_sc, NEG_BIG)
            l_sc[...] = jnp.zeros_like(l_sc)
            acc_sc[...] = jnp.zeros_like(acc_sc)

        def step(masked):
            sc = lax.dot_general(q_ref[...], k_ref[...], _NT, preferred_element_type=F32) * scale
            if has_bias:
                sc = sc + (cq_ref[...] - ck_ref[...])
            if masked:
                sc = jnp.where(_causal_keep(qi, ki, bq, bk, False), sc, NEG_BIG)
            m_prev = m_sc[...]
            m_new = jnp.maximum(m_prev, jnp.max(sc, axis=-1, keepdims=True))
            alpha = jnp.exp(m_prev - m_new)
            p = jnp.exp(sc - m_new)
            l_sc[...] = alpha * l_sc[...] + jnp.sum(p, axis=-1, keepdims=True)
            acc_sc[...] = alpha * acc_sc[...] + lax.dot_general(p.astype(BF16), v_ref[...], _NN,
                                                               preferred_element_type=F32)
            m_sc[...] = m_new

        @pl.when(tab_ref[2, t] == 1)
        def _():
            step(True)

        @pl.when(tab_ref[2, t] == 0)
        def _():
            step(False)

        @pl.when(tab_ref[4, t] == 1)
        def _():
            l = l_sc[...]
            o_ref[...] = (acc_sc[...] / l).astype(o_ref.dtype)
            lse_ref[...] = m_sc[...] + jnp.log(l)

    qspec = lambda d: pl.BlockSpec((None, bq, d), lambda hh, t, tb: (hh, tb[0, t], 0))
    kspec = lambda d: pl.BlockSpec((None, bk, d), lambda hh, t, tb: (hh, tb[1, t], 0))
    in_specs = [qspec(dk), kspec(dk), kspec(dv)]
    args = [q, k, v]
    if has_bias:
        in_specs += [qspec(1), pl.BlockSpec((None, 1, bk), lambda hh, t, tb: (hh, 0, tb[1, t]))]
        args += list(bias)
    return pl.pallas_call(
        body, name=name,
        grid_spec=pltpu.PrefetchScalarGridSpec(
            num_scalar_prefetch=1, grid=(h, tab.shape[1]), in_specs=in_specs,
            out_specs=[qspec(dv), qspec(1)],
            scratch_shapes=[pltpu.VMEM((bq, 1), F32), pltpu.VMEM((bq, 1), F32), pltpu.VMEM((bq, dv), F32)]),
        out_shape=[jax.ShapeDtypeStruct((h, s, dv), BF16), jax.ShapeDtypeStruct((h, s, 1), F32)],
        compiler_params=_params(("parallel", "arbitrary")),
    )(tab, *args)


def _attn_delta(do, o, name):
    h, s, dv = o.shape
    bs = _blk(s, 1024, 8)

    def body(do_ref, o_ref, d_ref):
        d_ref[...] = jnp.sum(do_ref[...].astype(F32) * o_ref[...].astype(F32), axis=-1, keepdims=True)

    blk = lambda d: pl.BlockSpec((None, bs, d), lambda hh, i: (hh, i, 0))
    return pl.pallas_call(
        body, name=name, grid=(h, s // bs), in_specs=[blk(dv), blk(dv)], out_specs=blk(1),
        out_shape=jax.ShapeDtypeStruct((h, s, 1), F32),
        compiler_params=_params(("parallel", "parallel")),
    )(do, o)


def _flash_bwd_dq(q, k, v, do, lse, delta, scale, bias, name, bq=512, bk=512):
    h, s, dk = q.shape
    dv = v.shape[-1]
    bq, bk = _blk(s, bq), _blk(s, bk)
    tab = _causal_table(s, bq, bk, True)
    has_bias = bias is not None

    def body(tab_ref, q_ref, k_ref, v_ref, do_ref, lse_ref, dl_ref, *rest):
        if has_bias:
            cq_ref, ck_ref = rest[:2]
            dq_ref, dc_ref, acc_sc, dc_sc = rest[2:]
        else:
            dq_ref, acc_sc = rest
        t = pl.program_id(1)
        qi, ki = tab_ref[0, t], tab_ref[1, t]

        @pl.when(tab_ref[3, t] == 1)
        def _():
            acc_sc[...] = jnp.zeros_like(acc_sc)
            if has_bias:
                dc_sc[...] = jnp.zeros_like(dc_sc)

        def step(masked):
            kv = k_ref[...]
            sc = lax.dot_general(q_ref[...], kv, _NT, preferred_element_type=F32) * scale
            if has_bias:
                sc = sc + (cq_ref[...] - ck_ref[...])
            if masked:
                sc = jnp.where(_causal_keep(qi, ki, bq, bk, False), sc, NEG_BIG)
            p = jnp.exp(sc - lse_ref[...])
            dp = lax.dot_general(do_ref[...], v_ref[...], _NT, preferred_element_type=F32)
            ds = p * (dp - dl_ref[...])
            acc_sc[...] += lax.dot_general(ds.astype(BF16), kv, _NN, preferred_element_type=F32)
            if has_bias:
                dc_sc[...] += jnp.sum(ds, axis=-1, keepdims=True)

        @pl.when(tab_ref[2, t] == 1)
        def _():
            step(True)

        @pl.when(tab_ref[2, t] == 0)
        def _():
            step(False)

        @pl.when(tab_ref[4, t] == 1)
        def _():
            dq_ref[...] = acc_sc[...] * scale
            if has_bias:
                dc_ref[...] = dc_sc[...]

    qspec = lambda d: pl.BlockSpec((None, bq, d), lambda hh, t, tb: (hh, tb[0, t], 0))
    kspec = lambda d: pl.BlockSpec((None, bk, d), lambda hh, t, tb: (hh, tb[1, t], 0))
    in_specs = [qspec(dk), kspec(dk), kspec(dv), qspec(dv), qspec(1), qspec(1)]
    args = [q, k, v, do, lse, delta]
    out_specs = [qspec(dk)]
    out_shape = [jax.ShapeDtypeStruct((h, s, dk), F32)]
    scratch = [pltpu.VMEM((bq, dk), F32)]
    if has_bias:
        in_specs += [qspec(1), pl.BlockSpec((None, 1, bk), lambda hh, t, tb: (hh, 0, tb[1, t]))]
        args += list(bias)
        out_specs.append(qspec(1))
        out_shape.append(jax.ShapeDtypeStruct((h, s, 1), F32))
        scratch.append(pltpu.VMEM((bq, 1), F32))
    outs = pl.pallas_call(
        body, name=name,
        grid_spec=pltpu.PrefetchScalarGridSpec(
            num_scalar_prefetch=1, grid=(h, tab.shape[1]), in_specs=in_specs,
            out_specs=out_specs, scratch_shapes=scratch),
        out_shape=out_shape,
        compiler_params=_params(("parallel", "arbitrary")),
    )(tab, *args)
    return outs if has_bias else outs[0]


def _flash_bwd_dkv(q, k, v, do, lse_row, delta_row, scale, bias, name, bq=512, bk=512):
    h, s, dk = q.shape
    dv = v.shape[-1]
    bq, bk = _blk(s, bq), _blk(s, bk)
    tab = _causal_table(s, bq, bk, False)
    has_bias = bias is not None

    def body(tab_ref, q_ref, k_ref, v_ref, do_ref, lse_ref, dl_ref, *rest):
        if has_bias:
            cq_ref, ck_ref = rest[:2]
            rest = rest[2:]
            dk_ref, dv_ref, dc_ref, dk_sc, dv_sc, dc_sc = rest
        else:
            dk_ref, dv_ref, dk_sc, dv_sc = rest
        t = pl.program_id(1)
        qi, ki = tab_ref[0, t], tab_ref[1, t]

        @pl.when(tab_ref[3, t] == 1)
        def _():
            dk_sc[...] = jnp.zeros_like(dk_sc)
            dv_sc[...] = jnp.zeros_like(dv_sc)
            if has_bias:
                dc_sc[...] = jnp.zeros_like(dc_sc)

        def step(masked):
            qv, dov = q_ref[...], do_ref[...]
            st = lax.dot_general(k_ref[...], qv, _NT, preferred_element_type=F32) * scale
            if has_bias:
                st = st + (cq_ref[...] - ck_ref[...])
            if masked:
                st = jnp.where(_causal_keep(qi, ki, bq, bk, True), st, NEG_BIG)
            pt = jnp.exp(st - lse_ref[...])
            dv_sc[...] += lax.dot_general(pt.astype(BF16), dov, _NN, preferred_element_type=F32)
            dpt = lax.dot_general(v_ref[...], dov, _NT, preferred_element_type=F32)
            dst = pt * (dpt - dl_ref[...])
            dk_sc[...] += lax.dot_general(dst.astype(BF16), qv, _NN, preferred_element_type=F32)
            if has_bias:
                dc_sc[...] += jnp.sum(dst, axis=-1, keepdims=True)

        @pl.when(tab_ref[2, t] == 1)
        def _():
            step(True)

        @pl.when(tab_ref[2, t] == 0)
        def _():
            step(False)

        @pl.when(tab_ref[4, t] == 1)
        def _():
            dk_ref[...] = dk_sc[...] * scale
            dv_ref[...] = dv_sc[...]
            if has_bias:
                dc_ref[...] = -dc_sc[...]

    qspec = lambda d: pl.BlockSpec((None, bq, d), lambda hh, t, tb: (hh, tb[0, t], 0))
    kspec = lambda d: pl.BlockSpec((None, bk, d), lambda hh, t, tb: (hh, tb[1, t], 0))
    qrow = pl.BlockSpec((None, 1, bq), lambda hh, t, tb: (hh, 0, tb[0, t]))
    in_specs = [qspec(dk), kspec(dk), kspec(dv), qspec(dv), qrow, qrow]
    args = [q, k, v, do, lse_row, delta_row]
    out_specs = [kspec(dk), kspec(dv)]
    out_shape = [jax.ShapeDtypeStruct((h, s, dk), F32), jax.ShapeDtypeStruct((h, s, dv), F32)]
    scratch = [pltpu.VMEM((bk, dk), F32), pltpu.VMEM((bk, dv), F32)]
    if has_bias:
        in_specs += [qrow, kspec(1)]
        args += list(bias)
        out_specs.append(kspec(1))
        out_shape.append(jax.ShapeDtypeStruct((h, s, 1), F32))
        scratch.append(pltpu.VMEM((bk, 1), F32))
    return pl.pallas_call(
        body, name=name,
        grid_spec=pltpu.PrefetchScalarGridSpec(
            num_scalar_prefetch=1, grid=(h, tab.shape[1]), in_specs=in_specs,
            out_specs=out_specs, scratch_shapes=scratch),
        out_shape=out_shape,
        compiler_params=_params(("parallel", "arbitrary")),
    )(tab, *args)


def _adamw(w, g, m, v, name):
    r, wd = w.shape
    br = _blk(r, 512, 8)

    def body(w_ref, g_ref, m_ref, v_ref, d_ref, nm_ref, nv_ref):
        gv = g_ref[...]
        mn = ADAM_B1 * m_ref[...] + (1.0 - ADAM_B1) * gv
        vn = ADAM_B2 * v_ref[...] + (1.0 - ADAM_B2) * (gv * gv)
        m_hat = mn / (1.0 - ADAM_B1 ** ADAM_STEP)
        v_hat = vn / (1.0 - ADAM_B2 ** ADAM_STEP)
        d_ref[...] = -ADAM_LR * (m_hat / (jnp.sqrt(v_hat) + ADAM_EPS) + ADAM_WD * w_ref[...])
        nm_ref[...] = mn
        nv_ref[...] = vn

    row = pl.BlockSpec((br, wd), lambda i: (i, 0))
    return pl.pallas_call(
        body, name=name, grid=(r // br,), in_specs=[row] * 4, out_specs=[row] * 3,
        out_shape=[jax.ShapeDtypeStruct((r, wd), F32)] * 3,
        compiler_params=_params(("parallel",)),
    )(w, g, m, v)


_ANY = pl.BlockSpec(memory_space=pl.ANY)


def _place():
    x, y, c = lax.axis_index("x"), lax.axis_index("y"), lax.axis_index("c")
    chips = [(1 - x, y), (x, 1 - y), (1 - x, 1 - y)]
    return x, y, c, chips


def _all_gather_shards(shard, name):
    r, w = shard.shape
    hr = r // 2

    def body(x_ref, out_ref, send_sems, recv_sems, local_sem):
        x, y, c, chips = _place()
        p = 2 * x + y
        sibling = (x, y, 1 - c)

        def rows(q, half):
            return out_ref.at[q, pl.ds(pl.multiple_of(half * hr, 16), hr), :]

        def copy(j, q, half, to, src=None):
            return pltpu.make_async_remote_copy(
                src_ref=rows(q, half) if src is None else src, dst_ref=rows(q, half),
                send_sem=send_sems.at[j], recv_sem=recv_sems.at[j], device_id=to, device_id_type=MESH)

        mine = pltpu.make_async_copy(x_ref, out_ref.at[p], local_sem)
        mine.start()
        my_half = x_ref.at[pl.ds(pl.multiple_of(c * hr, 16), hr), :]
        first = [copy(j, p, c, (cx, cy, c), src=my_half) for j, (cx, cy) in enumerate(chips)]
        for cp in first:
            cp.start()
        passed = []
        for j, (cx, cy) in enumerate(chips):
            q = 2 * cx + cy
            copy(j, q, c, (x, y, c)).wait_recv()
            fw = copy(3 + j, q, c, sibling)
            fw.start()
            passed.append(fw)
        for j, (cx, cy) in enumerate(chips):
            copy(3 + j, 2 * cx + cy, 1 - c, (x, y, c)).wait_recv()
        for cp in first + passed:
            cp.wait_send()
        mine.wait()

    return pl.pallas_call(
        body, name=name, in_specs=[_ANY], out_specs=_ANY,
        out_shape=jax.ShapeDtypeStruct((N_CHIPS, r, w), shard.dtype),
        scratch_shapes=[pltpu.SemaphoreType.DMA((6,)), pltpu.SemaphoreType.DMA((6,)), pltpu.SemaphoreType.DMA],
        compiler_params=pltpu.CompilerParams(vmem_limit_bytes=VMEM_LIMIT_BYTES),
    )(shard)


def _sibling_swap_halves(g, name):
    nq, r, w = g.shape
    hr = r // 2

    def body(g_ref, a_ref, send_sems, recv_sems):
        x, y, c, _ = _place()
        sibling = (x, y, 1 - c)
        cps = []
        for q in range(nq):
            cp = pltpu.make_async_remote_copy(
                src_ref=g_ref.at[q, pl.ds(pl.multiple_of((1 - c) * hr, 8), hr), :], dst_ref=a_ref.at[q],
                send_sem=send_sems.at[q], recv_sem=recv_sems.at[q], device_id=sibling, device_id_type=MESH)
            cp.start()
            cps.append(cp)
        for cp in cps:
            cp.wait()

    return pl.pallas_call(
        body, name=name, in_specs=[_ANY], out_specs=_ANY,
        out_shape=jax.ShapeDtypeStruct((nq, hr, w), g.dtype),
        scratch_shapes=[pltpu.SemaphoreType.DMA((nq,)), pltpu.SemaphoreType.DMA((nq,))],
        compiler_params=pltpu.CompilerParams(vmem_limit_bytes=VMEM_LIMIT_BYTES),
    )(g)


def _chip_sum(g, a, c_idx, name):
    nq, r, w = g.shape
    hr = r // 2
    br = _blk(hr, 512, 16)
    nb = hr // br

    def body(c_ref, g_ref, a_ref, o_ref):
        o_ref[...] = (g_ref[...] + a_ref[...]).astype(o_ref.dtype)

    return pl.pallas_call(
        body, name=name,
        grid_spec=pltpu.PrefetchScalarGridSpec(
            num_scalar_prefetch=1, grid=(nq, nb),
            in_specs=[pl.BlockSpec((None, br, w), lambda q, i, cr: (q, cr[0] * nb + i, 0)),
                      pl.BlockSpec((None, br, w), lambda q, i, cr: (q, i, 0))],
            out_specs=pl.BlockSpec((None, br, w), lambda q, i, cr: (q, i, 0))),
        out_shape=jax.ShapeDtypeStruct((nq, hr, w), BF16),
        compiler_params=_params(("parallel", "parallel")),
    )(c_idx, g, a)


def _chip_exchange(s4, name):
    nq, hr, w = s4.shape

    def body(s_ref, b_ref, send_sems, recv_sems, local_sem):
        x, y, c, chips = _place()
        p = 2 * x + y
        mine = pltpu.make_async_copy(s_ref.at[p], b_ref.at[p], local_sem)
        mine.start()
        cps = []
        for j, (cx, cy) in enumerate(chips):
            cp = pltpu.make_async_remote_copy(
                src_ref=s_ref.at[2 * cx + cy], dst_ref=b_ref.at[p],
                send_sem=send_sems.at[j], recv_sem=recv_sems.at[j], device_id=(cx, cy, c), device_id_type=MESH)
            cp.start()
            cps.append(cp)
        for j, (cx, cy) in enumerate(chips):
            pltpu.make_async_remote_copy(
                src_ref=s_ref.at[p], dst_ref=b_ref.at[2 * cx + cy],
                send_sem=send_sems.at[j], recv_sem=recv_sems.at[j], device_id=(cx, cy, c),
                device_id_type=MESH).wait_recv()
        for cp in cps:
            cp.wait_send()
        mine.wait()

    return pl.pallas_call(
        body, name=name, in_specs=[_ANY], out_specs=_ANY,
        out_shape=jax.ShapeDtypeStruct((nq, hr, w), s4.dtype),
        scratch_shapes=[pltpu.SemaphoreType.DMA((3,)), pltpu.SemaphoreType.DMA((3,)), pltpu.SemaphoreType.DMA],
        compiler_params=pltpu.CompilerParams(vmem_limit_bytes=VMEM_LIMIT_BYTES),
    )(s4)


def _sum_chips(b4, name):
    nq, hr, w = b4.shape
    br = _blk(hr, 512, 16)

    def body(b_ref, o_ref):
        acc = b_ref[0].astype(F32)
        for q in range(1, nq):
            acc = acc + b_ref[q].astype(F32)
        o_ref[...] = acc

    return pl.pallas_call(
        body, name=name, grid=(hr // br,),
        in_specs=[pl.BlockSpec((nq, br, w), lambda i: (0, i, 0))],
        out_specs=pl.BlockSpec((br, w), lambda i: (i, 0)),
        out_shape=jax.ShapeDtypeStruct((hr, w), F32),
        compiler_params=_params(("parallel",)),
    )(b4)


def _sibling_join_halves(t, name):
    hr, w = t.shape

    def body(t_ref, o_ref, send_sem, recv_sem, local_sem):
        x, y, c, _ = _place()

        def half(i):
            return o_ref.at[pl.ds(pl.multiple_of(i * hr, 8), hr), :]

        mine = pltpu.make_async_copy(t_ref, half(c), local_sem)
        mine.start()
        cp = pltpu.make_async_remote_copy(src_ref=t_ref, dst_ref=half(c), send_sem=send_sem, recv_sem=recv_sem,
                                          device_id=(x, y, 1 - c), device_id_type=MESH)
        cp.start()
        pltpu.make_async_remote_copy(src_ref=t_ref, dst_ref=half(1 - c), send_sem=send_sem, recv_sem=recv_sem,
                                     device_id=(x, y, 1 - c), device_id_type=MESH).wait_recv()
        cp.wait_send()
        mine.wait()

    return pl.pallas_call(
        body, name=name, in_specs=[_ANY], out_specs=_ANY,
        out_shape=jax.ShapeDtypeStruct((2 * hr, w), t.dtype),
        scratch_shapes=[pltpu.SemaphoreType.DMA, pltpu.SemaphoreType.DMA, pltpu.SemaphoreType.DMA],
        compiler_params=pltpu.CompilerParams(vmem_limit_bytes=VMEM_LIMIT_BYTES),
    )(t)


def _all_reduce_small(v, name):
    r, w = v.shape

    def body(v_ref, o_ref, slots, send_sems, recv_sems):
        x, y, c, _ = _place()
        me = 4 * x + 2 * y + c
        slots[me] = v_ref[...]
        cps = []
        for k in range(1, N_DEV):
            fx, fy, fc = (k >> 2) & 1, (k >> 1) & 1, k & 1
            to = (x ^ fx, y ^ fy, c ^ fc)
            cp = pltpu.make_async_remote_copy(
                src_ref=v_ref, dst_ref=slots.at[me], send_sem=send_sems.at[k - 1], recv_sem=recv_sems.at[k - 1],
                device_id=to, device_id_type=MESH)
            cp.start()
            cps.append(cp)
        for k in range(1, N_DEV):
            fx, fy, fc = (k >> 2) & 1, (k >> 1) & 1, k & 1
            src_dev = 4 * (x ^ fx) + 2 * (y ^ fy) + (c ^ fc)
            pltpu.make_async_remote_copy(
                src_ref=v_ref, dst_ref=slots.at[src_dev], send_sem=send_sems.at[k - 1],
                recv_sem=recv_sems.at[k - 1], device_id=(x, y, c), device_id_type=MESH).wait_recv()
        for cp in cps:
            cp.wait_send()
        acc = slots[0]
        for d in range(1, N_DEV):
            acc = acc + slots[d]
        o_ref[...] = acc

    return pl.pallas_call(
        body, name=name,
        in_specs=[pl.BlockSpec(memory_space=pltpu.VMEM)], out_specs=pl.BlockSpec(memory_space=pltpu.VMEM),
        out_shape=jax.ShapeDtypeStruct((r, w), F32),
        scratch_shapes=[pltpu.VMEM((N_DEV, r, w), F32), pltpu.SemaphoreType.DMA((N_DEV - 1,)),
                        pltpu.SemaphoreType.DMA((N_DEV - 1,))],
        compiler_params=pltpu.CompilerParams(vmem_limit_bytes=VMEM_LIMIT_BYTES),
    )(v)


def _part_rows(shape, part_rows=PACK_PART_ROWS):
    return _round_up(-(-math.prod(shape) // PACK_LANES), part_rows)


def _packed_rows(shapes):
    return _round_up(sum(_part_rows(s) for s in shapes), PACK_ROWS_MULT)


def _pack(arrs, total_rows, dtype, part_rows=PACK_PART_ROWS):
    parts = []
    for a in arrs:
        flat = a.reshape(-1).astype(dtype)
        rows = _part_rows(a.shape, part_rows)
        parts.append(jnp.pad(flat, (0, rows * PACK_LANES - flat.shape[0])).reshape(rows, PACK_LANES))
    used = sum(p.shape[0] for p in parts)
    if total_rows > used:
        parts.append(jnp.zeros((total_rows - used, PACK_LANES), dtype))
    return jnp.concatenate(parts, axis=0)


def _unpack(packed, shapes, part_rows=PACK_PART_ROWS):
    out, r0 = [], 0
    for s in shapes:
        rows = _part_rows(s, part_rows)
        out.append(packed[r0:r0 + rows].reshape(-1)[:math.prod(s)].reshape(s))
        r0 += rows
    return out


_BIG = (("fox_w_in", 2), ("fox_w_out", 1), ("mla_w_kv_a", 0), ("mla_w_kv_b", 1), ("mla_w_q_a", 1),
        ("mla_w_q_b", 2), ("mla_w_out", 1), ("ffn_w_up", 2), ("ffn_w_down", 1))
_SMALL = ("norm_mix_g", "norm_ffn_g", "fox_b_f", "kv_norm_g", "mla_kv_a_norm_g", "mla_q_a_norm_g", "final_norm_g")
_WEIGHTS = ("norm_mix_g", "norm_ffn_g", "fox_w_in", "fox_b_f", "fox_w_out", "kv_norm_g", "mla_w_kv_a",
            "mla_kv_a_norm_g", "mla_w_kv_b", "mla_w_q_a", "mla_q_a_norm_g", "mla_w_q_b", "mla_w_out",
            "ffn_w_up", "ffn_w_down", "final_norm_g")


def _heads_first(t, heads):
    s = t.shape[0]
    return jnp.transpose(t.reshape(s, heads, -1), (1, 0, 2))


def _heads_last(t):
    h, s, d = t.shape
    return jnp.transpose(t, (1, 0, 2)).reshape(s, h * d)


def _ffn_fwd(x, h, w_up, w_down, tag):
    def relu_sq(acc):
        r = jnp.maximum(acc, 0.0)
        return r, r * r

    r, a = _matmul(h, w_up, mode="nn", out_dtypes=(BF16, BF16), epilogue=relu_sq, name=f"{tag}_up")
    x_out = _matmul(a, w_down, mode="nn", out_dtypes=(F32,), epilogue=lambda acc, res: (acc + res,),
                    extras=(x,), name=f"{tag}_down")
    return x_out, r, a


def _ffn_bwd(dx_out, x_in, h, r, a, g_norm, w_up, w_down, tag):
    d_u = _matmul(dx_out, w_down, mode="nt", out_dtypes=(BF16,), epilogue=lambda acc, rr: (acc * (2.0 * rr.astype(F32)),),
                  extras=(r,), name=f"{tag}_d_act")
    d_w_down = _matmul(a, dx_out, mode="tn", out_dtypes=(F32,), name=f"{tag}_d_w_down")
    d_w_up = _matmul(h, d_u, mode="tn", out_dtypes=(F32,), name=f"{tag}_d_w_up")
    d_h = _matmul(d_u, w_up, mode="nt", out_dtypes=(F32,), name=f"{tag}_d_h")
    dx_in, (d_g,) = _rms_bwd(x_in, [(g_norm, d_h)], dx_out, name=f"{tag}_d_norm")
    return dx_in, d_w_up, d_w_down, d_g


def kernel(x, norm_mix_g, norm_ffn_g, fox_w_in, fox_b_f, fox_w_out, kv_norm_g, mla_w_kv_a, mla_kv_a_norm_g, mla_w_kv_b, mla_w_q_a, mla_q_a_norm_g, mla_w_q_b, mla_w_out, ffn_w_up, ffn_w_down, final_norm_g, loss_target, m_norm_mix_g, m_norm_ffn_g, m_fox_w_in, m_fox_b_f, m_fox_w_out, m_kv_norm_g, m_mla_w_kv_a, m_mla_kv_a_norm_g, m_mla_w_kv_b, m_mla_w_q_a, m_mla_q_a_norm_g, m_mla_w_q_b, m_mla_w_out, m_ffn_w_up, m_ffn_w_down, m_final_norm_g, v_norm_mix_g, v_norm_ffn_g, v_fox_w_in, v_fox_b_f, v_fox_w_out, v_kv_norm_g, v_mla_w_kv_a, v_mla_kv_a_norm_g, v_mla_w_kv_b, v_mla_w_q_a, v_mla_q_a_norm_g, v_mla_w_q_b, v_mla_w_out, v_ffn_w_up, v_ffn_w_down, v_final_norm_g):
    args = dict(locals())
    w_in = {n: args[n] for n in _WEIGHTS}
    m_in = {n: args["m_" + n] for n in _WEIGHTS}
    v_in = {n: args["v_" + n] for n in _WEIGHTS}

    xs = x[0]
    seq, d_model = xs.shape
    tgt = loss_target[0]
    fox_h, mla_h, nope = FOX_HEADS, MLA_HEADS, QK_NOPE_DIM
    kv_rank = mla_kv_a_norm_g.shape[0]
    rope = mla_w_kv_a.shape[1] - kv_rank
    half = rope // 2
    q_rank = mla_q_a_norm_g.shape[1]
    v_dim = mla_w_kv_b.shape[1] * N_CHIPS // mla_h - nope
    fox_w = fox_w_out.shape[1] * N_CHIPS
    fox_dh = fox_w // fox_h

    big_names = [n for n, _ in _BIG]
    shard_shapes = [w_in[n].shape for n in big_names]
    rows = _packed_rows(shard_shapes)
    gathered = _all_gather_shards(_pack([w_in[n] for n in big_names], rows, BF16), name="gather_weights")
    full = {}
    for q in range(N_CHIPS):
        for (n, ax), piece in zip(_BIG, _unpack(gathered[q], shard_shapes)):
            full.setdefault(n, []).append(piece)
    full = {n: jnp.concatenate(full[n], axis=ax) for n, ax in _BIG}

    w_fox_in = full["fox_w_in"][0]
    w_qkv, w_gate = w_fox_in[:, :3 * fox_w], w_fox_in[:, 3 * fox_w:]
    w_fox_out = full["fox_w_out"][0]
    w_kv_a, w_kv_b = full["mla_w_kv_a"], full["mla_w_kv_b"]
    w_q_a = full["mla_w_q_a"][0]
    w_q_b3 = full["mla_w_q_b"][0].reshape(q_rank, mla_h, nope + rope)
    w_q_b = jnp.concatenate([w_q_b3[:, :, :nope].reshape(q_rank, -1),
                             w_q_b3[:, :, nope:nope + half].reshape(q_rank, -1),
                             w_q_b3[:, :, nope + half:].reshape(q_rank, -1)], axis=1)
    w_mla_out = full["mla_w_out"][0]
    w_up, w_down = full["ffn_w_up"], full["ffn_w_down"]

    inv = 1.0 / (ROPE_BASE ** (jnp.arange(0, rope, 2, dtype=F32) / rope))
    ang = jnp.arange(seq, dtype=F32)[:, None] * inv[None, :]
    cos, sin = jnp.cos(ang), jnp.sin(ang)
    cos_q, sin_q = jnp.tile(cos, (1, mla_h)), jnp.tile(sin, (1, mla_h))

    (h0,) = _rms_fwd(xs, norm_mix_g[0:1], name="l0_norm_mix")
    qkv = _matmul(h0, w_qkv, mode="nn", out_dtypes=(BF16,), name="fox_qkv")
    gate = _matmul(h0, w_gate, mode="nn", out_dtypes=(F32,), name="fox_gate")
    z = gate + fox_b_f[0][None, :]
    cum = jnp.cumsum(jax.nn.log_sigmoid(z), axis=0)
    c_col = jnp.transpose(cum)[:, :, None]
    c_row = jnp.transpose(cum)[:, None, :]
    fq = _heads_first(qkv[:, :fox_w], fox_h)
    fk = _heads_first(qkv[:, fox_w:2 * fox_w], fox_h)
    fv = _heads_first(qkv[:, 2 * fox_w:], fox_h)
    fox_scale = fox_dh ** -0.5
    fo, f_lse = _flash_fwd(fq, fk, fv, fox_scale, (c_col, c_row), name="fox_attn")
    ctx0 = _heads_last(fo)
    add_res = lambda acc, res: (acc + res,)
    x1 = _matmul(ctx0, w_fox_out, mode="nn", out_dtypes=(F32,), epilogue=add_res, extras=(xs,), name="fox_out")
    (h1,) = _rms_fwd(x1, norm_ffn_g[0:1], name="l0_norm_ffn")
    x2, r0, a0 = _ffn_fwd(x1, h1, w_up[0], w_down[0], "ffn0")

    src, h2 = _rms_fwd(x2, jnp.stack([kv_norm_g, norm_mix_g[1]]), name="l1_norm_kv_mix")
    kv_a = _matmul(src, w_kv_a, mode="nn", out_dtypes=(F32,), name="mla_kv_a")
    ckv_pre = kv_a[:, :kv_rank]
    (c_kv,) = _rms_fwd(ckv_pre, mla_kv_a_norm_g[None, :], name="mla_norm_kv_a")
    kr1, kr2 = _rope(kv_a[None, :, kv_rank:kv_rank + half], kv_a[None, :, kv_rank + half:], cos, sin, 1.0,
                     name="mla_rope_k")
    kv_b = _matmul(c_kv, w_kv_b, mode="nn", out_dtypes=(BF16,), name="mla_kv_b").reshape(seq, mla_h, nope + v_dim)
    cq_pre = _matmul(h2, w_q_a, mode="nn", out_dtypes=(F32,), name="mla_q_a")
    (c_q,) = _rms_fwd(cq_pre, mla_q_a_norm_g, name="mla_norm_q_a")
    qf = _matmul(c_q, w_q_b, mode="nn", out_dtypes=(F32,), name="mla_q_b")
    n_nope = mla_h * nope
    n_half = mla_h * half
    qr1, qr2 = _rope(qf[None, :, n_nope:n_nope + n_half], qf[None, :, n_nope + n_half:], cos_q, sin_q, 1.0,
                     name="mla_rope_q")
    mq = jnp.transpose(jnp.concatenate([qf[:, :n_nope].reshape(seq, mla_h, nope), qr1.reshape(seq, mla_h, half),
                                        qr2.reshape(seq, mla_h, half)], axis=-1).astype(BF16), (1, 0, 2))
    k_rope = jnp.concatenate([kr1, kr2], axis=-1).astype(BF16)
    mk = jnp.concatenate([jnp.transpose(kv_b[:, :, :nope], (1, 0, 2)),
                          jnp.broadcast_to(k_rope[None], (mla_h, seq, rope))], axis=-1)
    mv = jnp.transpose(kv_b[:, :, nope:], (1, 0, 2))
    mla_scale = (nope + rope) ** -0.5
    mo, m_lse = _flash_fwd(mq, mk, mv, mla_scale, None, name="mla_attn")
    ctx1 = _heads_last(mo)
    x3 = _matmul(ctx1, w_mla_out, mode="nn", out_dtypes=(F32,), epilogue=add_res, extras=(x2,), name="mla_out")
    (h3,) = _rms_fwd(x3, norm_ffn_g[1:2], name="l1_norm_ffn")
    x4, r1, a1 = _ffn_fwd(x3, h3, w_up[1], w_down[1], "ffn1")

    loss_tile, dx4, d_final_g = _loss_head(x4, final_norm_g[None, :], tgt, name="loss_head")
    loss = lax.psum(loss_tile[0, 0], ("x", "y", "c"))

    gw = {}
    dx3, d_up1, d_down1, d_nf1 = _ffn_bwd(dx4, x3, h3, r1, a1, norm_ffn_g[1:2], w_up[1], w_down[1], "ffn1")

    d_ctx1 = _matmul(dx3, w_mla_out, mode="nt", out_dtypes=(BF16,), name="mla_d_ctx")
    gw["mla_w_out"] = _matmul(ctx1, dx3, mode="tn", out_dtypes=(F32,), name="mla_d_w_out")[None]
    d_mo = _heads_first(d_ctx1, mla_h)
    m_delta = _attn_delta(d_mo, mo, name="mla_attn_delta")
    row = lambda t: jnp.transpose(t, (0, 2, 1))
    d_mq = _flash_bwd_dq(mq, mk, mv, d_mo, m_lse, m_delta, mla_scale, None, name="mla_attn_dq")
    d_mk, d_mv = _flash_bwd_dkv(mq, mk, mv, d_mo, row(m_lse), row(m_delta), mla_scale, None, name="mla_attn_dkv")
    d_mq = jnp.transpose(d_mq, (1, 0, 2))
    d_qr1, d_qr2 = _rope(d_mq[None, :, :, nope:nope + half].reshape(1, seq, n_half),
                         d_mq[None, :, :, nope + half:].reshape(1, seq, n_half), cos_q, sin_q, -1.0,
                         name="mla_rope_dq")
    d_qf = jnp.concatenate([d_mq[:, :, :nope].reshape(seq, n_nope), d_qr1, d_qr2], axis=1)
    d_w_q_b = _matmul(c_q, d_qf, mode="tn", out_dtypes=(F32,), name="mla_d_w_q_b")
    gw["mla_w_q_b"] = jnp.concatenate([d_w_q_b[:, :n_nope].reshape(q_rank, mla_h, nope),
                                       d_w_q_b[:, n_nope:n_nope + n_half].reshape(q_rank, mla_h, half),
                                       d_w_q_b[:, n_nope + n_half:].reshape(q_rank, mla_h, half)],
                                      axis=-1).reshape(1, q_rank, mla_h * (nope + rope))
    d_c_q = _matmul(d_qf, w_q_b, mode="nt", out_dtypes=(F32,), name="mla_d_c_q")
    d_cq_pre, (d_q_a_g,) = _rms_bwd(cq_pre, [(mla_q_a_norm_g, d_c_q)], None, name="mla_d_norm_q_a")
    gw["mla_w_q_a"] = _matmul(h2, d_cq_pre, mode="tn", out_dtypes=(F32,), name="mla_d_w_q_a")[None]
    d_h2 = _matmul(d_cq_pre, w_q_a, mode="nt", out_dtypes=(F32,), name="mla_d_h")

    d_kv_b = jnp.concatenate([jnp.transpose(d_mk[:, :, :nope], (1, 0, 2)), jnp.transpose(d_mv, (1, 0, 2))],
                             axis=-1).reshape(seq, mla_h * (nope + v_dim))
    gw["mla_w_kv_b"] = _matmul(c_kv, d_kv_b, mode="tn", out_dtypes=(F32,), name="mla_d_w_kv_b")
    d_c_kv = _matmul(d_kv_b, w_kv_b, mode="nt", out_dtypes=(F32,), name="mla_d_c_kv")
    d_ckv_pre, (d_kv_a_g,) = _rms_bwd(ckv_pre, [(mla_kv_a_norm_g[None, :], d_c_kv)], None, name="mla_d_norm_kv_a")
    d_kr1, d_kr2 = _rope(d_mk[:, :, nope:nope + half], d_mk[:, :, nope + half:], cos, sin, -1.0, name="mla_rope_dk")
    d_kv_a = jnp.concatenate([d_ckv_pre, d_kr1, d_kr2], axis=1)
    gw["mla_w_kv_a"] = _matmul(src, d_kv_a, mode="tn", out_dtypes=(F32,), name="mla_d_w_kv_a")
    d_src = _matmul(d_kv_a, w_kv_a, mode="nt", out_dtypes=(F32,), name="mla_d_src")
    dx2, (d_kv_g, d_nm1) = _rms_bwd(x2, [(kv_norm_g[None, :], d_src), (norm_mix_g[1:2], d_h2)], dx3,
                                    name="l1_d_norm_kv_mix")

    dx1, d_up0, d_down0, d_nf0 = _ffn_bwd(dx2, x1, h1, r0, a0, norm_ffn_g[0:1], w_up[0], w_down[0], "ffn0")
    gw["ffn_w_up"] = jnp.stack([d_up0, d_up1])
    gw["ffn_w_down"] = jnp.stack([d_down0, d_down1])

    d_ctx0 = _matmul(dx1, w_fox_out, mode="nt", out_dtypes=(BF16,), name="fox_d_ctx")
    gw["fox_w_out"] = _matmul(ctx0, dx1, mode="tn", out_dtypes=(F32,), name="fox_d_w_out")[None]
    d_fo = _heads_first(d_ctx0, fox_h)
    f_delta = _attn_delta(d_fo, fo, name="fox_attn_delta")
    d_fq, d_cq = _flash_bwd_dq(fq, fk, fv, d_fo, f_lse, f_delta, fox_scale, (c_col, c_row), name="fox_attn_dq")
    d_fk, d_fv, d_ck = _flash_bwd_dkv(fq, fk, fv, d_fo, row(f_lse), row(f_delta), fox_scale, (c_row, c_col),
                                      name="fox_attn_dkv")
    d_qkv = jnp.concatenate([_heads_last(d_fq), _heads_last(d_fk), _heads_last(d_fv)], axis=1)
    d_cum = jnp.transpose(d_cq[:, :, 0] + d_ck[:, :, 0])
    d_z = jnp.flip(jnp.cumsum(jnp.flip(d_cum, 0), axis=0), 0) * jax.nn.sigmoid(-z)
    d_b_f = jnp.sum(d_z, axis=0)
    d_w_qkv = _matmul(h0, d_qkv, mode="tn", out_dtypes=(F32,), name="fox_d_w_qkv")
    d_w_gate = _matmul(h0, d_z, mode="tn", out_dtypes=(F32,), name="fox_d_w_gate")
    gw["fox_w_in"] = jnp.concatenate([d_w_qkv, d_w_gate], axis=1)[None]
    d_h0g = _matmul(d_z, w_gate, mode="nt", out_dtypes=(F32,), name="fox_d_h_gate")
    d_h0 = _matmul(d_qkv, w_qkv, mode="nt", out_dtypes=(F32,), epilogue=add_res, extras=(d_h0g,), name="fox_d_h")
    grad_x, (d_nm0,) = _rms_bwd(xs, [(norm_mix_g[0:1], d_h0)], dx1, name="l0_d_norm_mix")

    c_idx = lax.axis_index("c").astype(jnp.int32).reshape(1)
    per_chip = []
    for q in range(N_CHIPS):
        pieces = [jnp.split(gw[n], N_CHIPS, axis=ax)[q] for n, ax in _BIG]
        per_chip.append(_pack(pieces, rows, F32))
    g4 = jnp.stack(per_chip)
    a4 = _sibling_swap_halves(g4, name="grads_to_sibling")
    s4 = _chip_sum(g4, a4, c_idx, name="grads_chip_sum")
    b4 = _chip_exchange(s4, name="grads_between_chips")
    t_half = _sum_chips(b4, name="grads_sum_chips")
    g_big = _sibling_join_halves(t_half, name="grads_join_halves")

    small_local = {"norm_mix_g": jnp.concatenate([d_nm0, d_nm1], axis=0),
                   "norm_ffn_g": jnp.concatenate([d_nf0, d_nf1], axis=0),
                   "fox_b_f": d_b_f[None, :], "kv_norm_g": d_kv_g[0], "mla_kv_a_norm_g": d_kv_a_g[0],
                   "mla_q_a_norm_g": d_q_a_g, "final_norm_g": d_final_g[0]}
    small_shapes = [w_in[n].shape for n in _SMALL]
    small_rows = sum(_part_rows(s, SMALL_PART_ROWS) for s in small_shapes)
    pack_small = lambda arrs: _pack(arrs, small_rows, F32, SMALL_PART_ROWS)
    g_small = _all_reduce_small(pack_small([small_local[n] for n in _SMALL]), name="grads_small")

    d_big, nm_big, nv_big = _adamw(_pack([w_in[n] for n in big_names], rows, F32), g_big,
                                   _pack([m_in[n] for n in big_names], rows, F32),
                                   _pack([v_in[n] for n in big_names], rows, F32), name="adamw_big")
    d_sm, nm_sm, nv_sm = _adamw(pack_small([w_in[n] for n in _SMALL]), g_small,
                                pack_small([m_in[n] for n in _SMALL]),
                                pack_small([v_in[n] for n in _SMALL]), name="adamw_small")

    def spread(big, small):
        out = dict(zip(big_names, _unpack(big, shard_shapes)))
        out.update(zip(_SMALL, _unpack(small, small_shapes, SMALL_PART_ROWS)))
        return [out[n] for n in _WEIGHTS]

    return (loss, grad_x[None], *spread(g_big, g_small), *spread(d_big, d_sm), *spread(nm_big, nm_sm),
            *spread(nv_big, nv_sm))
```

```python
import math

import numpy as np
import jax
import jax.numpy as jnp
from jax import lax
from jax.experimental import pallas as pl
from jax.experimental.pallas import tpu as pltpu

F32 = jnp.float32
BF16 = jnp.bfloat16

FOX_HEADS = 16
MLA_HEADS = 8
QK_NOPE_DIM = 128
ROPE_BASE = 10000.0
EPS = 1e-6

ADAM_LR = 0.001
ADAM_B1 = 0.9
ADAM_B2 = 0.999
ADAM_EPS = 1e-08
ADAM_WD = 0.01
ADAM_STEP = 10

N_CHIPS = 4
N_DEV = 8
PACK_LANES = 1024
PACK_PART_ROWS = 16
SMALL_PART_ROWS = 8
PACK_ROWS_MULT = 1024
VMEM_LIMIT_BYTES = 48 * 1024 * 1024
LANE_TILE = 128
ATTN_BLOCK_Q = 1024
ATTN_BLOCK_K = 1024
NEG_BIG = -1e30
MESH = pl.DeviceIdType.MESH


def _round_up(n, m):
    return -(-n // m) * m


def _blk(dim, pref, mult=128):
    if dim <= pref:
        return dim
    b = (pref // mult) * mult
    while b >= mult:
        if dim % b == 0:
            return b
        b -= mult
    return dim


def _params(sem=None):
    return pltpu.CompilerParams(dimension_semantics=sem, vmem_limit_bytes=VMEM_LIMIT_BYTES)


_DIMS = {"nn": (((1,), (0,)), ((), ())), "nt": (((1,), (1,)), ((), ())), "tn": (((0,), (0,)), ((), ()))}


def _matmul(a, b, *, mode, out_dtypes, name, epilogue=None, extras=(), bm=1024, bn=1024, bk=512):
    if mode == "tn":
        kdim, m = a.shape
    else:
        m, kdim = a.shape
    n = b.shape[0] if mode == "nt" else b.shape[1]
    bm, bn, bk = _blk(m, bm), _blk(n, bn), _blk(kdim, bk)
    nk = kdim // bk
    n_extra, n_out = len(extras), len(out_dtypes)
    dims = _DIMS[mode]

    def body(a_ref, b_ref, *rest):
        extra_refs = rest[:n_extra]
        out_refs = rest[n_extra:n_extra + n_out]
        acc_ref = rest[n_extra + n_out]
        k = pl.program_id(2)

        @pl.when(k == 0)
        def _():
            acc_ref[...] = jnp.zeros_like(acc_ref)

        acc_ref[...] += lax.dot_general(a_ref[...].astype(BF16), b_ref[...].astype(BF16), dims,
                                        preferred_element_type=F32)

        @pl.when(k == nk - 1)
        def _():
            acc = acc_ref[...]
            res = (acc,) if epilogue is None else epilogue(acc, *[r[...] for r in extra_refs])
            for o_ref, r in zip(out_refs, res):
                o_ref[...] = r.astype(o_ref.dtype)

    if mode == "tn":
        a_spec = pl.BlockSpec((bk, bm), lambda i, j, k: (k, i))
    else:
        a_spec = pl.BlockSpec((bm, bk), lambda i, j, k: (i, k))
    if mode == "nt":
        b_spec = pl.BlockSpec((bn, bk), lambda i, j, k: (j, k))
    else:
        b_spec = pl.BlockSpec((bk, bn), lambda i, j, k: (k, j))
    tile = pl.BlockSpec((bm, bn), lambda i, j, k: (i, j))
    outs = pl.pallas_call(
        body, name=name,
        grid=(m // bm, n // bn, nk),
        in_specs=[a_spec, b_spec] + [tile] * n_extra,
        out_specs=[tile] * n_out,
        out_shape=[jax.ShapeDtypeStruct((m, n), dt) for dt in out_dtypes],
        scratch_shapes=[pltpu.VMEM((bm, bn), F32)],
        compiler_params=_params(("parallel", "parallel", "arbitrary")),
    )(a, b, *extras)
    return outs[0] if n_out == 1 else outs


def _rms_fwd(x, gains, name):
    s, w = x.shape
    g = gains.shape[0]
    bs = _blk(s, 512, 8)

    def body(x_ref, g_ref, *out_refs):
        xv = x_ref[...]
        y = xv * lax.rsqrt(jnp.mean(xv * xv, axis=-1, keepdims=True) + EPS)
        for i, o_ref in enumerate(out_refs):
            o_ref[...] = (y * g_ref[i:i + 1, :]).astype(o_ref.dtype)

    row = pl.BlockSpec((bs, w), lambda i: (i, 0))
    outs = pl.pallas_call(
        body, name=name, grid=(s // bs,),
        in_specs=[row, pl.BlockSpec((g, w), lambda i: (0, 0))],
        out_specs=[row] * g,
        out_shape=[jax.ShapeDtypeStruct((s, w), BF16)] * g,
        compiler_params=_params(("parallel",)),
    )(x, gains)
    return outs


def _rms_bwd(x, branches, resid, name):
    s, w = x.shape
    nb = len(branches)
    bs = _blk(s, 512, 8)
    has_resid = resid is not None

    def body(x_ref, *rest):
        g_refs = rest[:nb]
        dy_refs = rest[nb:2 * nb]
        pos = 2 * nb
        r_ref = rest[pos] if has_resid else None
        pos += int(has_resid)
        dx_ref = rest[pos]
        dg_refs = rest[pos + 1:pos + 1 + nb]
        i = pl.program_id(0)

        @pl.when(i == 0)
        def _():
            for dg_ref in dg_refs:
                dg_ref[...] = jnp.zeros_like(dg_ref)

        xv = x_ref[...]
        rstd = lax.rsqrt(jnp.mean(xv * xv, axis=-1, keepdims=True) + EPS)
        xhat = xv * rstd
        dx = r_ref[...] if has_resid else jnp.zeros_like(xv)
        for g_ref, dy_ref, dg_ref in zip(g_refs, dy_refs, dg_refs):
            dy = dy_ref[...].astype(F32)
            dyg = dy * g_ref[...]
            dx = dx + rstd * (dyg - xhat * jnp.mean(dyg * xhat, axis=-1, keepdims=True))
            dg_ref[...] += jnp.sum(dy * xhat, axis=0, keepdims=True)
        dx_ref[...] = dx

    row = pl.BlockSpec((bs, w), lambda i: (i, 0))
    vec = pl.BlockSpec((1, w), lambda i: (0, 0))
    args = [x] + [g for g, _ in branches] + [dy for _, dy in branches] + ([resid] if has_resid else [])
    outs = pl.pallas_call(
        body, name=name, grid=(s // bs,),
        in_specs=[row] + [vec] * nb + [row] * nb + ([row] if has_resid else []),
        out_specs=[row] + [vec] * nb,
        out_shape=[jax.ShapeDtypeStruct((s, w), F32)] + [jax.ShapeDtypeStruct((1, w), F32)] * nb,
        compiler_params=_params(("arbitrary",)),
    )(*args)
    return outs[0], list(outs[1:])


def _loss_head(x, g, target, name):
    s, w = x.shape
    bs = _blk(s, 512, 8)

    def body(x_ref, g_ref, t_ref, loss_ref, dx_ref, dg_ref):
        i = pl.program_id(0)

        @pl.when(i == 0)
        def _():
            loss_ref[...] = jnp.zeros_like(loss_ref)
            dg_ref[...] = jnp.zeros_like(dg_ref)

        xv = x_ref[...]
        gv = g_ref[...]
        rstd = lax.rsqrt(jnp.mean(xv * xv, axis=-1, keepdims=True) + EPS)
        xhat = xv * rstd
        err = xhat * gv - t_ref[...]
        loss_ref[...] += 0.5 * jnp.sum(jnp.mean(err * err, axis=-1, keepdims=True))
        dy = err * (1.0 / w)
        dyg = dy * gv
        dx_ref[...] = rstd * (dyg - xhat * jnp.mean(dyg * xhat, axis=-1, keepdims=True))
        dg_ref[...] += jnp.sum(dy * xhat, axis=0, keepdims=True)

    row = pl.BlockSpec((bs, w), lambda i: (i, 0))
    vec = pl.BlockSpec((1, w), lambda i: (0, 0))
    return pl.pallas_call(
        body, name=name, grid=(s // bs,),
        in_specs=[row, vec, row],
        out_specs=[pl.BlockSpec((8, 128), lambda i: (0, 0)), row, vec],
        out_shape=[jax.ShapeDtypeStruct((8, 128), F32), jax.ShapeDtypeStruct((s, w), F32),
                   jax.ShapeDtypeStruct((1, w), F32)],
        compiler_params=_params(("arbitrary",)),
    )(x, g, target)


def _rope(a, b, cos, sin, sign, name):
    g, s, w = a.shape
    bs = _blk(s, 1024, 8)

    def body(a_ref, b_ref, c_ref, s_ref, o1_ref, o2_ref):
        av = jnp.sum(a_ref[...].astype(F32), axis=0)
        bv = jnp.sum(b_ref[...].astype(F32), axis=0)
        cv, sv = c_ref[...], s_ref[...] * sign
        o1_ref[...] = av * cv - bv * sv
        o2_ref[...] = bv * cv + av * sv

    grp = pl.BlockSpec((g, bs, w), lambda i: (0, i, 0))
    row = pl.BlockSpec((bs, w), lambda i: (i, 0))
    return pl.pallas_call(
        body, name=name, grid=(s // bs,),
        in_specs=[grp, grp, row, row], out_specs=[row, row],
        out_shape=[jax.ShapeDtypeStruct((s, w), F32)] * 2,
        compiler_params=_params(("parallel",)),
    )(a, b, cos, sin)


def _causal_table(s, bq, bk, q_major):
    nq, nk = s // bq, s // bk
    rows = []
    if q_major:
        for qi in range(nq):
            kmax = (qi * bq + bq - 1) // bk
            for ki in range(kmax + 1):
                rows.append((qi, ki, int(ki * bk + bk - 1 > qi * bq), int(ki == 0), int(ki == kmax)))
    else:
        for ki in range(nk):
            qmin = (ki * bk) // bq
            for qi in range(qmin, nq):
                rows.append((qi, ki, int(ki * bk + bk - 1 > qi * bq), int(qi == qmin), int(qi == nq - 1)))
    return jnp.asarray(np.array(rows, np.int32).T)


def _causal_keep(qi, ki, bq, bk, transposed):
    if transposed:
        kpos = ki * bk + lax.broadcasted_iota(jnp.int32, (bk, bq), 0)
        qpos = qi * bq + lax.broadcasted_iota(jnp.int32, (bk, bq), 1)
    else:
        qpos = qi * bq + lax.broadcasted_iota(jnp.int32, (bq, bk), 0)
        kpos = ki * bk + lax.broadcasted_iota(jnp.int32, (bq, bk), 1)
    return kpos <= qpos


_NT = (((1,), (1,)), ((), ()))
_NN = (((1,), (0,)), ((), ()))


def _attn_specs(bq, bk):
    qspec = lambda d: pl.BlockSpec((None, bq, d), lambda hh, t, tb: (hh, tb[0, t], 0))
    kspec = lambda d: pl.BlockSpec((None, bk, d), lambda hh, t, tb: (hh, tb[1, t], 0))
    return qspec, kspec


def _flash_fwd(qa, ka, va, l_col, name):
    h, s, da = qa.shape
    dv = va.shape[-1]
    bq, bk = _blk(s, ATTN_BLOCK_Q), _blk(s, ATTN_BLOCK_K)
    tab = _causal_table(s, bq, bk, True)

    def body(tab_ref, q_ref, k_ref, v_ref, o_ref, lse_ref, m_sc, acc_sc):
        t = pl.program_id(1)
        qi, ki = tab_ref[0, t], tab_ref[1, t]

        @pl.when(tab_ref[3, t] == 1)
        def _():
            m_sc[...] = jnp.full_like(m_sc, NEG_BIG)
            acc_sc[...] = jnp.zeros_like(acc_sc)

        def step(masked):
            sc = lax.dot_general(q_ref[...], k_ref[...], _NT, preferred_element_type=F32)
            if masked:
                sc = jnp.where(_causal_keep(qi, ki, bq, bk, False), sc, NEG_BIG)
            m_prev = m_sc[...]
            m_new = jnp.maximum(m_prev, jnp.max(sc, axis=-1, keepdims=True))
            p = jnp.exp(sc - m_new).astype(BF16)
            acc_sc[...] = jnp.exp(m_prev - m_new) * acc_sc[...] + lax.dot_general(
                p, v_ref[...], _NN, preferred_element_type=F32)
            m_sc[...] = m_new

        @pl.when(tab_ref[2, t] == 1)
        def _():
            step(True)

        @pl.when(tab_ref[2, t] == 0)
        def _():
            step(False)

        @pl.when(tab_ref[4, t] == 1)
        def _():
            acc = acc_sc[...]
            lane = lax.broadcasted_iota(jnp.int32, acc.shape, 1)
            l = jnp.sum(jnp.where(lane == l_col, acc, 0.0), axis=-1, keepdims=True)
            o_ref[...] = (acc / l).astype(o_ref.dtype)
            lse_ref[...] = m_sc[...] + jnp.log(l)

    qspec, kspec = _attn_specs(bq, bk)
    return pl.pallas_call(
        body, name=name,
        grid_spec=pltpu.PrefetchScalarGridSpec(
            num_scalar_prefetch=1, grid=(h, tab.shape[1]),
            in_specs=[qspec(da), kspec(da), kspec(dv)],
            out_specs=[qspec(dv), qspec(1)],
            scratch_shapes=[pltpu.VMEM((bq, 1), F32), pltpu.VMEM((bq, dv), F32)]),
        out_shape=[jax.ShapeDtypeStruct((h, s, dv), BF16), jax.ShapeDtypeStruct((h, s, 1), F32)],
        compiler_params=_params(("parallel", "arbitrary")),
    )(tab, qa, ka, va)


def _attn_delta(do, o, name):
    h, s, dv = o.shape
    bs = _blk(s, 1024, 8)

    def body(do_ref, o_ref, d_ref):
        d_ref[...] = jnp.sum(do_ref[...].astype(F32) * o_ref[...].astype(F32), axis=-1, keepdims=True)

    blk = lambda d: pl.BlockSpec((None, bs, d), lambda hh, i: (hh, i, 0))
    return pl.pallas_call(
        body, name=name, grid=(h, s // bs), in_specs=[blk(dv), blk(dv)], out_specs=blk(1),
        out_shape=jax.ShapeDtypeStruct((h, s, 1), F32),
        compiler_params=_params(("parallel", "parallel")),
    )(do, o)


def _flash_bwd_dq(qa, ka, va, doa, name):
    h, s, da = qa.shape
    dv = va.shape[-1]
    bq, bk = _blk(s, ATTN_BLOCK_Q), _blk(s, ATTN_BLOCK_K)
    tab = _causal_table(s, bq, bk, True)

    def body(tab_ref, q_ref, k_ref, v_ref, do_ref, dq_ref, acc_sc):
        t = pl.program_id(1)
        qi, ki = tab_ref[0, t], tab_ref[1, t]

        @pl.when(tab_ref[3, t] == 1)
        def _():
            acc_sc[...] = jnp.zeros_like(acc_sc)

        def step(masked):
            kv = k_ref[...]
            sc = lax.dot_general(q_ref[...], kv, _NT, preferred_element_type=F32)
            if masked:
                sc = jnp.where(_causal_keep(qi, ki, bq, bk, False), sc, NEG_BIG)
            dp = lax.dot_general(do_ref[...], v_ref[...], _NT, preferred_element_type=F32)
            ds = (jnp.exp(sc) * dp).astype(BF16)
            acc_sc[...] += lax.dot_general(ds, kv, _NN, preferred_element_type=F32)

        @pl.when(tab_ref[2, t] == 1)
        def _():
            step(True)

        @pl.when(tab_ref[2, t] == 0)
        def _():
            step(False)

        @pl.when(tab_ref[4, t] == 1)
        def _():
            dq_ref[...] = acc_sc[...]

    qspec, kspec = _attn_specs(bq, bk)
    return pl.pallas_call(
        body, name=name,
        grid_spec=pltpu.PrefetchScalarGridSpec(
            num_scalar_prefetch=1, grid=(h, tab.shape[1]),
            in_specs=[qspec(da), kspec(da), kspec(dv), qspec(dv)],
            out_specs=qspec(da), scratch_shapes=[pltpu.VMEM((bq, da), F32)]),
        out_shape=jax.ShapeDtypeStruct((h, s, da), F32),
        compiler_params=_params(("parallel", "arbitrary")),
    )(tab, qa, ka, va, doa)


def _flash_bwd_dkv(qa, ka, va, doa, name):
    h, s, da = qa.shape
    dv = va.shape[-1]
    bq, bk = _blk(s, ATTN_BLOCK_Q), _blk(s, ATTN_BLOCK_K)
    tab = _causal_table(s, bq, bk, False)

    def body(tab_ref, q_ref, k_ref, v_ref, do_ref, dk_ref, dv_ref, dk_sc, dv_sc):
        t = pl.program_id(1)
        qi, ki = tab_ref[0, t], tab_ref[1, t]

        @pl.when(tab_ref[3, t] == 1)
        def _():
            dk_sc[...] = jnp.zeros_like(dk_sc)
            dv_sc[...] = jnp.zeros_like(dv_sc)

        def step(masked):
            qv, dov = q_ref[...], do_ref[...]
            st = lax.dot_general(k_ref[...], qv, _NT, preferred_element_type=F32)
            if masked:
                st = jnp.where(_causal_keep(qi, ki, bq, bk, True), st, NEG_BIG)
            pt = jnp.exp(st)
            dv_sc[...] += lax.dot_general(pt.astype(BF16), dov, _NN, preferred_element_type=F32)
            dpt = lax.dot_general(v_ref[...], dov, _NT, preferred_element_type=F32)
            dk_sc[...] += lax.dot_general((pt * dpt).astype(BF16), qv, _NN, preferred_element_type=F32)

        @pl.when(tab_ref[2, t] == 1)
        def _():
            step(True)

        @pl.when(tab_ref[2, t] == 0)
        def _():
            step(False)

        @pl.when(tab_ref[4, t] == 1)
        def _():
            dk_ref[...] = dk_sc[...]
            dv_ref[...] = dv_sc[...]

    qspec, kspec = _attn_specs(bq, bk)
    return pl.pallas_call(
        body, name=name,
        grid_spec=pltpu.PrefetchScalarGridSpec(
            num_scalar_prefetch=1, grid=(h, tab.shape[1]),
            in_specs=[qspec(da), kspec(da), kspec(dv), qspec(dv)],
            out_specs=[kspec(da), kspec(dv)],
            scratch_shapes=[pltpu.VMEM((bk, da), F32), pltpu.VMEM((bk, dv), F32)]),
        out_shape=[jax.ShapeDtypeStruct((h, s, da), F32), jax.ShapeDtypeStruct((h, s, dv), F32)],
        compiler_params=_params(("parallel", "arbitrary")),
    )(tab, qa, ka, va, doa)


def _split3(x):
    hi = lax.reduce_precision(x, 8, 7)
    rest = x - hi
    mid = lax.reduce_precision(rest, 8, 7)
    lo = lax.reduce_precision(rest - mid, 8, 7)
    return jnp.stack([hi, mid, lo], axis=-1).astype(BF16)


def _augment(parts, width):
    h, s = parts[0].shape[:2]
    used = sum(p.shape[-1] for p in parts)
    return jnp.concatenate(list(parts) + [jnp.zeros((h, s, width - used), BF16)], axis=-1)


def _adamw(w, g, m, v, name):
    r, wd = w.shape
    br = _blk(r, 512, 8)

    def body(w_ref, g_ref, m_ref, v_ref, d_ref, nm_ref, nv_ref):
        gv = g_ref[...]
        mn = ADAM_B1 * m_ref[...] + (1.0 - ADAM_B1) * gv
        vn = ADAM_B2 * v_ref[...] + (1.0 - ADAM_B2) * (gv * gv)
        m_hat = mn / (1.0 - ADAM_B1 ** ADAM_STEP)
        v_hat = vn / (1.0 - ADAM_B2 ** ADAM_STEP)
        d_ref[...] = -ADAM_LR * (m_hat / (jnp.sqrt(v_hat) + ADAM_EPS) + ADAM_WD * w_ref[...])
        nm_ref[...] = mn
        nv_ref[...] = vn

    row = pl.BlockSpec((br, wd), lambda i: (i, 0))
    return pl.pallas_call(
        body, name=name, grid=(r // br,), in_specs=[row] * 4, out_specs=[row] * 3,
        out_shape=[jax.ShapeDtypeStruct((r, wd), F32)] * 3,
        compiler_params=_params(("parallel",)),
    )(w, g, m, v)


_ANY = pl.BlockSpec(memory_space=pl.ANY)


def _place():
    x, y, c = lax.axis_index("x"), lax.axis_index("y"), lax.axis_index("c")
    chips = [(1 - x, y), (x, 1 - y), (1 - x, 1 - y)]
    return x, y, c, chips


def _all_gather_shards(shard, name):
    r, w = shard.shape
    hr = r // 2

    def body(x_ref, out_ref, send_sems, recv_sems, local_sem):
        x, y, c, chips = _place()
        p = 2 * x + y
        sibling = (x, y, 1 - c)

        def rows(q, half):
            return out_ref.at[q, pl.ds(pl.multiple_of(half * hr, 16), hr), :]

        def copy(j, q, half, to, src=None):
            return pltpu.make_async_remote_copy(
                src_ref=rows(q, half) if src is None else src, dst_ref=rows(q, half),
                send_sem=send_sems.at[j], recv_sem=recv_sems.at[j], device_id=to, device_id_type=MESH)

        mine = pltpu.make_async_copy(x_ref, out_ref.at[p], local_sem)
        mine.start()
        my_half = x_ref.at[pl.ds(pl.multiple_of(c * hr, 16), hr), :]
        first = [copy(j, p, c, (cx, cy, c), src=my_half) for j, (cx, cy) in enumerate(chips)]
        for cp in first:
            cp.start()
        passed = []
        for j, (cx, cy) in enumerate(chips):
            q = 2 * cx + cy
            copy(j, q, c, (x, y, c)).wait_recv()
            fw = copy(3 + j, q, c, sibling)
            fw.start()
            passed.append(fw)
        for j, (cx, cy) in enumerate(chips):
            copy(3 + j, 2 * cx + cy, 1 - c, (x, y, c)).wait_recv()
        for cp in first + passed:
            cp.wait_send()
        mine.wait()

    return pl.pallas_call(
        body, name=name, in_specs=[_ANY], out_specs=_ANY,
        out_shape=jax.ShapeDtypeStruct((N_CHIPS, r, w), shard.dtype),
        scratch_shapes=[pltpu.SemaphoreType.DMA((6,)), pltpu.SemaphoreType.DMA((6,)), pltpu.SemaphoreType.DMA],
        compiler_params=pltpu.CompilerParams(vmem_limit_bytes=VMEM_LIMIT_BYTES),
    )(shard)


def _sibling_swap_halves(g, name):
    nq, r, w = g.shape
    hr = r // 2

    def body(g_ref, a_ref, send_sems, recv_sems):
        x, y, c, _ = _place()
        sibling = (x, y, 1 - c)
        cps = []
        for q in range(nq):
            cp = pltpu.make_async_remote_copy(
                src_ref=g_ref.at[q, pl.ds(pl.multiple_of((1 - c) * hr, 8), hr), :], dst_ref=a_ref.at[q],
                send_sem=send_sems.at[q], recv_sem=recv_sems.at[q], device_id=sibling, device_id_type=MESH)
            cp.start()
            cps.append(cp)
        for cp in cps:
            cp.wait()

    return pl.pallas_call(
        body, name=name, in_specs=[_ANY], out_specs=_ANY,
        out_shape=jax.ShapeDtypeStruct((nq, hr, w), g.dtype),
        scratch_shapes=[pltpu.SemaphoreType.DMA((nq,)), pltpu.SemaphoreType.DMA((nq,))],
        compiler_params=pltpu.CompilerParams(vmem_limit_bytes=VMEM_LIMIT_BYTES),
    )(g)


def _chip_sum(g, a, c_idx, name):
    nq, r, w = g.shape
    hr = r // 2
    br = _blk(hr, 512, 16)
    nb = hr // br

    def body(c_ref, g_ref, a_ref, o_ref):
        o_ref[...] = (g_ref[...] + a_ref[...]).astype(o_ref.dtype)

    return pl.pallas_call(
        body, name=name,
        grid_spec=pltpu.PrefetchScalarGridSpec(
            num_scalar_prefetch=1, grid=(nq, nb),
            in_specs=[pl.BlockSpec((None, br, w), lambda q, i, cr: (q, cr[0] * nb + i, 0)),
                      pl.BlockSpec((None, br, w), lambda q, i, cr: (q, i, 0))],
            out_specs=pl.BlockSpec((None, br, w), lambda q, i, cr: (q, i, 0))),
        out_shape=jax.ShapeDtypeStruct((nq, hr, w), BF16),
        compiler_params=_params(("parallel", "parallel")),
    )(c_idx, g, a)


def _chip_exchange(s4, name):
    nq, hr, w = s4.shape

    def body(s_ref, b_ref, send_sems, recv_sems, local_sem):
        x, y, c, chips = _place()
        p = 2 * x + y
        mine = pltpu.make_async_copy(s_ref.at[p], b_ref.at[p], local_sem)
        mine.start()
        cps = []
        for j, (cx, cy) in enumerate(chips):
            cp = pltpu.make_async_remote_copy(
                src_ref=s_ref.at[2 * cx + cy], dst_ref=b_ref.at[p],
                send_sem=send_sems.at[j], recv_sem=recv_sems.at[j], device_id=(cx, cy, c), device_id_type=MESH)
            cp.start()
            cps.append(cp)
        for j, (cx, cy) in enumerate(chips):
            pltpu.make_async_remote_copy(
                src_ref=s_ref.at[p], dst_ref=b_ref.at[2 * cx + cy],
                send_sem=send_sems.at[j], recv_sem=recv_sems.at[j], device_id=(cx, cy, c),
                device_id_type=MESH).wait_recv()
        for cp in cps:
            cp.wait_send()
        mine.wait()

    return pl.pallas_call(
        body, name=name, in_specs=[_ANY], out_specs=_ANY,
        out_shape=jax.ShapeDtypeStruct((nq, hr, w), s4.dtype),
        scratch_shapes=[pltpu.SemaphoreType.DMA((3,)), pltpu.SemaphoreType.DMA((3,)), pltpu.SemaphoreType.DMA],
        compiler_params=pltpu.CompilerParams(vmem_limit_bytes=VMEM_LIMIT_BYTES),
    )(s4)


def _sum_chips(b4, name):
    nq, hr, w = b4.shape
    br = _blk(hr, 512, 16)

    def body(b_ref, o_ref):
        acc = b_ref[0].astype(F32)
        for q in range(1, nq):
            acc = acc + b_ref[q].astype(F32)
        o_ref[...] = acc

    return pl.pallas_call(
        body, name=name, grid=(hr // br,),
        in_specs=[pl.BlockSpec((nq, br, w), lambda i: (0, i, 0))],
        out_specs=pl.BlockSpec((br, w), lambda i: (i, 0)),
        out_shape=jax.ShapeDtypeStruct((hr, w), F32),
        compiler_params=_params(("parallel",)),
    )(b4)


def _sibling_join_halves(t, name):
    hr, w = t.shape

    def body(t_ref, o_ref, send_sem, recv_sem, local_sem):
        x, y, c, _ = _place()

        def half(i):
            return o_ref.at[pl.ds(pl.multiple_of(i * hr, 8), hr), :]

        mine = pltpu.make_async_copy(t_ref, half(c), local_sem)
        mine.start()
        cp = pltpu.make_async_remote_copy(src_ref=t_ref, dst_ref=half(c), send_sem=send_sem, recv_sem=recv_sem,
                                          device_id=(x, y, 1 - c), device_id_type=MESH)
        cp.start()
        pltpu.make_async_remote_copy(src_ref=t_ref, dst_ref=half(1 - c), send_sem=send_sem, recv_sem=recv_sem,
                                     device_id=(x, y, 1 - c), device_id_type=MESH).wait_recv()
        cp.wait_send()
        mine.wait()

    return pl.pallas_call(
        body, name=name, in_specs=[_ANY], out_specs=_ANY,
        out_shape=jax.ShapeDtypeStruct((2 * hr, w), t.dtype),
        scratch_shapes=[pltpu.SemaphoreType.DMA, pltpu.SemaphoreType.DMA, pltpu.SemaphoreType.DMA],
        compiler_params=pltpu.CompilerParams(vmem_limit_bytes=VMEM_LIMIT_BYTES),
    )(t)


def _all_reduce_small(v, name):
    r, w = v.shape

    def body(v_ref, o_ref, slots, send_sems, recv_sems):
        x, y, c, _ = _place()
        me = 4 * x + 2 * y + c
        slots[me] = v_ref[...]
        cps = []
        for k in range(1, N_DEV):
            fx, fy, fc = (k >> 2) & 1, (k >> 1) & 1, k & 1
            to = (x ^ fx, y ^ fy, c ^ fc)
            cp = pltpu.make_async_remote_copy(
                src_ref=v_ref, dst_ref=slots.at[me], send_sem=send_sems.at[k - 1], recv_sem=recv_sems.at[k - 1],
                device_id=to, device_id_type=MESH)
            cp.start()
            cps.append(cp)
        for k in range(1, N_DEV):
            fx, fy, fc = (k >> 2) & 1, (k >> 1) & 1, k & 1
            src_dev = 4 * (x ^ fx) + 2 * (y ^ fy) + (c ^ fc)
            pltpu.make_async_remote_copy(
                src_ref=v_ref, dst_ref=slots.at[src_dev], send_sem=send_sems.at[k - 1],
                recv_sem=recv_sems.at[k - 1], device_id=(x, y, c), device_id_type=MESH).wait_recv()
        for cp in cps:
            cp.wait_send()
        acc = slots[0]
        for d in range(1, N_DEV):
            acc = acc + slots[d]
        o_ref[...] = acc

    return pl.pallas_call(
        body, name=name,
        in_specs=[pl.BlockSpec(memory_space=pltpu.VMEM)], out_specs=pl.BlockSpec(memory_space=pltpu.VMEM),
        out_shape=jax.ShapeDtypeStruct((r, w), F32),
        scratch_shapes=[pltpu.VMEM((N_DEV, r, w), F32), pltpu.SemaphoreType.DMA((N_DEV - 1,)),
                        pltpu.SemaphoreType.DMA((N_DEV - 1,))],
        compiler_params=pltpu.CompilerParams(vmem_limit_bytes=VMEM_LIMIT_BYTES),
    )(v)


def _part_rows(shape, part_rows=PACK_PART_ROWS):
    return _round_up(-(-math.prod(shape) // PACK_LANES), part_rows)


def _packed_rows(shapes):
    return _round_up(sum(_part_rows(s) for s in shapes), PACK_ROWS_MULT)


def _pack(arrs, total_rows, dtype, part_rows=PACK_PART_ROWS):
    parts = []
    for a in arrs:
        flat = a.reshape(-1).astype(dtype)
        rows = _part_rows(a.shape, part_rows)
        parts.append(jnp.pad(flat, (0, rows * PACK_LANES - flat.shape[0])).reshape(rows, PACK_LANES))
    used = sum(p.shape[0] for p in parts)
    if total_rows > used:
        parts.append(jnp.zeros((total_rows - used, PACK_LANES), dtype))
    return jnp.concatenate(parts, axis=0)


def _unpack(packed, shapes, part_rows=PACK_PART_ROWS):
    out, r0 = [], 0
    for s in shapes:
        rows = _part_rows(s, part_rows)
        out.append(packed[r0:r0 + rows].reshape(-1)[:math.prod(s)].reshape(s))
        r0 += rows
    return out


_BIG = (("fox_w_in", 2), ("fox_w_out", 1), ("mla_w_kv_a", 0), ("mla_w_kv_b", 1), ("mla_w_q_a", 1),
        ("mla_w_q_b", 2), ("mla_w_out", 1), ("ffn_w_up", 2), ("ffn_w_down", 1))
_SMALL = ("norm_mix_g", "norm_ffn_g", "fox_b_f", "kv_norm_g", "mla_kv_a_norm_g", "mla_q_a_norm_g", "final_norm_g")
_WEIGHTS = ("norm_mix_g", "norm_ffn_g", "fox_w_in", "fox_b_f", "fox_w_out", "kv_norm_g", "mla_w_kv_a",
            "mla_kv_a_norm_g", "mla_w_kv_b", "mla_w_q_a", "mla_q_a_norm_g", "mla_w_q_b", "mla_w_out",
            "ffn_w_up", "ffn_w_down", "final_norm_g")


def _heads_first(t, heads):
    s = t.shape[0]
    return jnp.transpose(t.reshape(s, heads, -1), (1, 0, 2))


def _heads_last(t):
    h, s, d = t.shape
    return jnp.transpose(t, (1, 0, 2)).reshape(s, h * d)


def _ffn_fwd(x, h, w_up, w_down, tag):
    def relu_sq(acc):
        r = jnp.maximum(acc, 0.0)
        return r, r * r

    r, a = _matmul(h, w_up, mode="nn", out_dtypes=(BF16, BF16), epilogue=relu_sq, name=f"{tag}_up")
    x_out = _matmul(a, w_down, mode="nn", out_dtypes=(F32,), epilogue=lambda acc, res: (acc + res,),
                    extras=(x,), name=f"{tag}_down")
    return x_out, r, a


def _ffn_bwd(dx_out, x_in, h, r, a, g_norm, w_up, w_down, tag):
    d_u = _matmul(dx_out, w_down, mode="nt", out_dtypes=(BF16,), epilogue=lambda acc, rr: (acc * (2.0 * rr.astype(F32)),),
                  extras=(r,), name=f"{tag}_d_act")
    d_w_down = _matmul(a, dx_out, mode="tn", out_dtypes=(F32,), name=f"{tag}_d_w_down")
    d_w_up = _matmul(h, d_u, mode="tn", out_dtypes=(F32,), name=f"{tag}_d_w_up")
    d_h = _matmul(d_u, w_up, mode="nt", out_dtypes=(F32,), name=f"{tag}_d_h")
    dx_in, (d_g,) = _rms_bwd(x_in, [(g_norm, d_h)], dx_out, name=f"{tag}_d_norm")
    return dx_in, d_w_up, d_w_down, d_g


def kernel(x, norm_mix_g, norm_ffn_g, fox_w_in, fox_b_f, fox_w_out, kv_norm_g, mla_w_kv_a, mla_kv_a_norm_g, mla_w_kv_b, mla_w_q_a, mla_q_a_norm_g, mla_w_q_b, mla_w_out, ffn_w_up, ffn_w_down, final_norm_g, loss_target, m_norm_mix_g, m_norm_ffn_g, m_fox_w_in, m_fox_b_f, m_fox_w_out, m_kv_norm_g, m_mla_w_kv_a, m_mla_kv_a_norm_g, m_mla_w_kv_b, m_mla_w_q_a, m_mla_q_a_norm_g, m_mla_w_q_b, m_mla_w_out, m_ffn_w_up, m_ffn_w_down, m_final_norm_g, v_norm_mix_g, v_norm_ffn_g, v_fox_w_in, v_fox_b_f, v_fox_w_out, v_kv_norm_g, v_mla_w_kv_a, v_mla_kv_a_norm_g, v_mla_w_kv_b, v_mla_w_q_a, v_mla_q_a_norm_g, v_mla_w_q_b, v_mla_w_out, v_ffn_w_up, v_ffn_w_down, v_final_norm_g):
    args = dict(locals())
    w_in = {n: args[n] for n in _WEIGHTS}
    m_in = {n: args["m_" + n] for n in _WEIGHTS}
    v_in = {n: args["v_" + n] for n in _WEIGHTS}

    xs = x[0]
    seq, d_model = xs.shape
    tgt = loss_target[0]
    fox_h, mla_h, nope = FOX_HEADS, MLA_HEADS, QK_NOPE_DIM
    kv_rank = mla_kv_a_norm_g.shape[0]
    rope = mla_w_kv_a.shape[1] - kv_rank
    half = rope // 2
    q_rank = mla_q_a_norm_g.shape[1]
    v_dim = mla_w_kv_b.shape[1] * N_CHIPS // mla_h - nope
    fox_w = fox_w_out.shape[1] * N_CHIPS
    fox_dh = fox_w // fox_h

    big_names = [n for n, _ in _BIG]
    shard_shapes = [w_in[n].shape for n in big_names]
    rows = _packed_rows(shard_shapes)
    gathered = _all_gather_shards(_pack([w_in[n] for n in big_names], rows, BF16), name="gather_weights")
    full = {}
    for q in range(N_CHIPS):
        for (n, ax), piece in zip(_BIG, _unpack(gathered[q], shard_shapes)):
            full.setdefault(n, []).append(piece)
    full = {n: jnp.concatenate(full[n], axis=ax) for n, ax in _BIG}

    w_fox_in = full["fox_w_in"][0]
    w_qkv, w_gate = w_fox_in[:, :3 * fox_w], w_fox_in[:, 3 * fox_w:]
    w_fox_out = full["fox_w_out"][0]
    w_kv_a, w_kv_b = full["mla_w_kv_a"], full["mla_w_kv_b"]
    w_q_a = full["mla_w_q_a"][0]
    w_q_b3 = full["mla_w_q_b"][0].reshape(q_rank, mla_h, nope + rope)
    w_q_b = jnp.concatenate([w_q_b3[:, :, :nope].reshape(q_rank, -1),
                             w_q_b3[:, :, nope:nope + half].reshape(q_rank, -1),
                             w_q_b3[:, :, nope + half:].reshape(q_rank, -1)], axis=1)
    w_mla_out = full["mla_w_out"][0]
    w_up, w_down = full["ffn_w_up"], full["ffn_w_down"]

    inv = 1.0 / (ROPE_BASE ** (jnp.arange(0, rope, 2, dtype=F32) / rope))
    ang = jnp.arange(seq, dtype=F32)[:, None] * inv[None, :]
    cos, sin = jnp.cos(ang), jnp.sin(ang)
    cos_q, sin_q = jnp.tile(cos, (1, mla_h)), jnp.tile(sin, (1, mla_h))

    (h0,) = _rms_fwd(xs, norm_mix_g[0:1], name="l0_norm_mix")
    qkv = _matmul(h0, w_qkv, mode="nn", out_dtypes=(BF16,), name="fox_qkv")
    gate = _matmul(h0, w_gate, mode="nn", out_dtypes=(F32,), name="fox_gate")
    z = gate + fox_b_f[0][None, :]
    cum = jnp.cumsum(jax.nn.log_sigmoid(z), axis=0)
    fox_scale = fox_dh ** -0.5
    fox_da = _round_up(fox_dh + 9, LANE_TILE)
    fox_dva = _round_up(fox_dh + 4, LANE_TILE)
    c3 = _split3(jnp.transpose(cum))
    ones3 = jnp.ones((fox_h, seq, 3), BF16)
    fq = _heads_first(qkv[:, :fox_w], fox_h) * fox_scale
    fqa = _augment([fq, c3, ones3], fox_da)
    fka = _augment([_heads_first(qkv[:, fox_w:2 * fox_w], fox_h), ones3, -c3, ones3], fox_da)
    fva = _augment([_heads_first(qkv[:, 2 * fox_w:], fox_h), -ones3, ones3[:, :, :1]], fox_dva)
    foa, f_lse = _flash_fwd(fqa, fka, fva, fox_dh + 3, name="fox_attn")
    fo = foa[:, :, :fox_dh]
    ctx0 = _heads_last(fo)
    add_res = lambda acc, res: (acc + res,)
    x1 = _matmul(ctx0, w_fox_out, mode="nn", out_dtypes=(F32,), epilogue=add_res, extras=(xs,), name="fox_out")
    (h1,) = _rms_fwd(x1, norm_ffn_g[0:1], name="l0_norm_ffn")
    x2, r0, a0 = _ffn_fwd(x1, h1, w_up[0], w_down[0], "ffn0")

    src, h2 = _rms_fwd(x2, jnp.stack([kv_norm_g, norm_mix_g[1]]), name="l1_norm_kv_mix")
    kv_a = _matmul(src, w_kv_a, mode="nn", out_dtypes=(F32,), name="mla_kv_a")
    ckv_pre = kv_a[:, :kv_rank]
    (c_kv,) = _rms_fwd(ckv_pre, mla_kv_a_norm_g[None, :], name="mla_norm_kv_a")
    kr1, kr2 = _rope(kv_a[None, :, kv_rank:kv_rank + half], kv_a[None, :, kv_rank + half:], cos, sin, 1.0,
                     name="mla_rope_k")
    kv_b = _matmul(c_kv, w_kv_b, mode="nn", out_dtypes=(BF16,), name="mla_kv_b").reshape(seq, mla_h, nope + v_dim)
    cq_pre = _matmul(h2, w_q_a, mode="nn", out_dtypes=(F32,), name="mla_q_a")
    (c_q,) = _rms_fwd(cq_pre, mla_q_a_norm_g, name="mla_norm_q_a")
    qf = _matmul(c_q, w_q_b, mode="nn", out_dtypes=(F32,), name="mla_q_b")
    n_nope = mla_h * nope
    n_half = mla_h * half
    qr1, qr2 = _rope(qf[None, :, n_nope:n_nope + n_half], qf[None, :, n_nope + n_half:], cos_q, sin_q, 1.0,
                     name="mla_rope_q")
    mla_scale = (nope + rope) ** -0.5
    mla_dk = nope + rope
    mla_da = _round_up(mla_dk + 3, LANE_TILE)
    mla_dva = _round_up(v_dim + 4, LANE_TILE)
    mq = jnp.concatenate([qf[:, :n_nope].reshape(seq, mla_h, nope), qr1.reshape(seq, mla_h, half),
                          qr2.reshape(seq, mla_h, half)], axis=-1) * mla_scale
    mq = jnp.transpose(mq.astype(BF16), (1, 0, 2))
    k_rope = jnp.concatenate([kr1, kr2], axis=-1).astype(BF16)
    m_ones3 = jnp.ones((mla_h, seq, 3), BF16)
    mqa = _augment([mq], mla_da)
    mka = _augment([jnp.transpose(kv_b[:, :, :nope], (1, 0, 2)),
                    jnp.broadcast_to(k_rope[None], (mla_h, seq, rope)), m_ones3], mla_da)
    mva = _augment([jnp.transpose(kv_b[:, :, nope:], (1, 0, 2)), -m_ones3, m_ones3[:, :, :1]], mla_dva)
    moa, m_lse = _flash_fwd(mqa, mka, mva, v_dim + 3, name="mla_attn")
    mo = moa[:, :, :v_dim]
    ctx1 = _heads_last(mo)
    x3 = _matmul(ctx1, w_mla_out, mode="nn", out_dtypes=(F32,), epilogue=add_res, extras=(x2,), name="mla_out")
    (h3,) = _rms_fwd(x3, norm_ffn_g[1:2], name="l1_norm_ffn")
    x4, r1, a1 = _ffn_fwd(x3, h3, w_up[1], w_down[1], "ffn1")

    loss_tile, dx4, d_final_g = _loss_head(x4, final_norm_g[None, :], tgt, name="loss_head")
    loss = lax.psum(loss_tile[0, 0], ("x", "y", "c"))

    gw = {}
    dx3, d_up1, d_down1, d_nf1 = _ffn_bwd(dx4, x3, h3, r1, a1, norm_ffn_g[1:2], w_up[1], w_down[1], "ffn1")

    d_ctx1 = _matmul(dx3, w_mla_out, mode="nt", out_dtypes=(BF16,), name="mla_d_ctx")
    gw["mla_w_out"] = _matmul(ctx1, dx3, mode="tn", out_dtypes=(F32,), name="mla_d_w_out")[None]
    d_mo = _heads_first(d_ctx1, mla_h)
    m_delta = _attn_delta(d_mo, mo, name="mla_attn_delta")
    mqb = _augment([mq, -_split3(m_lse[:, :, 0])], mla_da)
    d_moa = _augment([d_mo, _split3(m_delta[:, :, 0])], mla_dva)
    d_mqa = _flash_bwd_dq(mqb, mka, mva, d_moa, name="mla_attn_dq")
    d_mka, d_mva = _flash_bwd_dkv(mqb, mka, mva, d_moa, name="mla_attn_dkv")
    d_mq = jnp.transpose(d_mqa[:, :, :mla_dk], (1, 0, 2)) * mla_scale
    d_mk, d_mv = d_mka[:, :, :mla_dk], d_mva[:, :, :v_dim]
    d_qr1, d_qr2 = _rope(d_mq[None, :, :, nope:nope + half].reshape(1, seq, n_half),
                         d_mq[None, :, :, nope + half:].reshape(1, seq, n_half), cos_q, sin_q, -1.0,
                         name="mla_rope_dq")
    d_qf = jnp.concatenate([d_mq[:, :, :nope].reshape(seq, n_nope), d_qr1, d_qr2], axis=1)
    d_w_q_b = _matmul(c_q, d_qf, mode="tn", out_dtypes=(F32,), name="mla_d_w_q_b")
    gw["mla_w_q_b"] = jnp.concatenate([d_w_q_b[:, :n_nope].reshape(q_rank, mla_h, nope),
                                       d_w_q_b[:, n_nope:n_nope + n_half].reshape(q_rank, mla_h, half),
                                       d_w_q_b[:, n_nope + n_half:].reshape(q_rank, mla_h, half)],
                                      axis=-1).reshape(1, q_rank, mla_h * (nope + rope))
    d_c_q = _matmul(d_qf, w_q_b, mode="nt", out_dtypes=(F32,), name="mla_d_c_q")
    d_cq_pre, (d_q_a_g,) = _rms_bwd(cq_pre, [(mla_q_a_norm_g, d_c_q)], None, name="mla_d_norm_q_a")
    gw["mla_w_q_a"] = _matmul(h2, d_cq_pre, mode="tn", out_dtypes=(F32,), name="mla_d_w_q_a")[None]
    d_h2 = _matmul(d_cq_pre, w_q_a, mode="nt", out_dtypes=(F32,), name="mla_d_h")

    d_kv_b = jnp.concatenate([jnp.transpose(d_mk[:, :, :nope], (1, 0, 2)), jnp.transpose(d_mv, (1, 0, 2))],
                             axis=-1).reshape(seq, mla_h * (nope + v_dim))
    gw["mla_w_kv_b"] = _matmul(c_kv, d_kv_b, mode="tn", out_dtypes=(F32,), name="mla_d_w_kv_b")
    d_c_kv = _matmul(d_kv_b, w_kv_b, mode="nt", out_dtypes=(F32,), name="mla_d_c_kv")
    d_ckv_pre, (d_kv_a_g,) = _rms_bwd(ckv_pre, [(mla_kv_a_norm_g[None, :], d_c_kv)], None, name="mla_d_norm_kv_a")
    d_kr1, d_kr2 = _rope(d_mk[:, :, nope:nope + half], d_mk[:, :, nope + half:], cos, sin, -1.0, name="mla_rope_dk")
    d_kv_a = jnp.concatenate([d_ckv_pre, d_kr1, d_kr2], axis=1)
    gw["mla_w_kv_a"] = _matmul(src, d_kv_a, mode="tn", out_dtypes=(F32,), name="mla_d_w_kv_a")
    d_src = _matmul(d_kv_a, w_kv_a, mode="nt", out_dtypes=(F32,), name="mla_d_src")
    dx2, (d_kv_g, d_nm1) = _rms_bwd(x2, [(kv_norm_g[None, :], d_src), (norm_mix_g[1:2], d_h2)], dx3,
                                    name="l1_d_norm_kv_mix")

    dx1, d_up0, d_down0, d_nf0 = _ffn_bwd(dx2, x1, h1, r0, a0, norm_ffn_g[0:1], w_up[0], w_down[0], "ffn0")
    gw["ffn_w_up"] = jnp.stack([d_up0, d_up1])
    gw["ffn_w_down"] = jnp.stack([d_down0, d_down1])

    d_ctx0 = _matmul(dx1, w_fox_out, mode="nt", out_dtypes=(BF16,), name="fox_d_ctx")
    gw["fox_w_out"] = _matmul(ctx0, dx1, mode="tn", out_dtypes=(F32,), name="fox_d_w_out")[None]
    d_fo = _heads_first(d_ctx0, fox_h)
    f_delta = _attn_delta(d_fo, fo, name="fox_attn_delta")
    fqb = _augment([fq, c3, ones3, -_split3(f_lse[:, :, 0])], fox_da)
    d_foa = _augment([d_fo, _split3(f_delta[:, :, 0])], fox_dva)
    d_fqa = _flash_bwd_dq(fqb, fka, fva, d_foa, name="fox_attn_dq")
    d_fka, d_fva = _flash_bwd_dkv(fqb, fka, fva, d_foa, name="fox_attn_dkv")
    d_qkv = jnp.concatenate([_heads_last(d_fqa[:, :, :fox_dh] * fox_scale), _heads_last(d_fka[:, :, :fox_dh]),
                             _heads_last(d_fva[:, :, :fox_dh])], axis=1)
    d_cum = jnp.transpose(d_fqa[:, :, fox_dh] - d_fka[:, :, fox_dh + 3])
    d_z = jnp.flip(jnp.cumsum(jnp.flip(d_cum, 0), axis=0), 0) * jax.nn.sigmoid(-z)
    d_b_f = jnp.sum(d_z, axis=0)
    d_w_qkv = _matmul(h0, d_qkv, mode="tn", out_dtypes=(F32,), name="fox_d_w_qkv")
    d_w_gate = _matmul(h0, d_z, mode="tn", out_dtypes=(F32,), name="fox_d_w_gate")
    gw["fox_w_in"] = jnp.concatenate([d_w_qkv, d_w_gate], axis=1)[None]
    d_h0g = _matmul(d_z, w_gate, mode="nt", out_dtypes=(F32,), name="fox_d_h_gate")
    d_h0 = _matmul(d_qkv, w_qkv, mode="nt", out_dtypes=(F32,), epilogue=add_res, extras=(d_h0g,), name="fox_d_h")
    grad_x, (d_nm0,) = _rms_bwd(xs, [(norm_mix_g[0:1], d_h0)], dx1, name="l0_d_norm_mix")

    c_idx = lax.axis_index("c").astype(jnp.int32).reshape(1)
    per_chip = []
    for q in range(N_CHIPS):
        pieces = [jnp.split(gw[n], N_CHIPS, axis=ax)[q] for n, ax in _BIG]
        per_chip.append(_pack(pieces, rows, F32))
    g4 = jnp.stack(per_chip)
    a4 = _sibling_swap_halves(g4, name="grads_to_sibling")
    s4 = _chip_sum(g4, a4, c_idx, name="grads_chip_sum")
    b4 = _chip_exchange(s4, name="grads_between_chips")
    t_half = _sum_chips(b4, name="grads_sum_chips")
    g_big = _sibling_join_halves(t_half, name="grads_join_halves")

    small_local = {"norm_mix_g": jnp.concatenate([d_nm0, d_nm1], axis=0),
                   "norm_ffn_g": jnp.concatenate([d_nf0, d_nf1], axis=0),
                   "fox_b_f": d_b_f[None, :], "kv_norm_g": d_kv_g[0], "mla_kv_a_norm_g": d_kv_a_g[0],
                   "mla_q_a_norm_g": d_q_a_g, "final_norm_g": d_final_g[0]}
    small_shapes = [w_in[n].shape for n in _SMALL]
    small_rows = sum(_part_rows(s, SMALL_PART_ROWS) for s in small_shapes)
    pack_small = lambda arrs: _pack(arrs, small_rows, F32, SMALL_PART_ROWS)
    g_small = _all_reduce_small(pack_small([small_local[n] for n in _SMALL]), name="grads_small")

    d_big, nm_big, nv_big = _adamw(_pack([w_in[n] for n in big_names], rows, F32), g_big,
                                   _pack([m_in[n] for n in big_names], rows, F32),
                                   _pack([v_in[n] for n in big_names], rows, F32), name="adamw_big")
    d_sm, nm_sm, nv_sm = _adamw(pack_small([w_in[n] for n in _SMALL]), g_small,
                                pack_small([m_in[n] for n in _SMALL]),
                                pack_small([v_in[n] for n in _SMALL]), name="adamw_small")

    def spread(big, small):
        out = dict(zip(big_names, _unpack(big, shard_shapes)))
        out.update(zip(_SMALL, _unpack(small, small_shapes, SMALL_PART_ROWS)))
        return [out[n] for n in _WEIGHTS]

    return (loss, grad_x[None], *spread(g_big, g_small), *spread(d_big, d_sm), *spread(nm_big, nm_sm),
            *spread(nv_big, nv_sm))
```

```python
import math

import numpy as np
import jax
import jax.numpy as jnp
from jax import lax
from jax.experimental import pallas as pl
from jax.experimental.pallas import tpu as pltpu

F32 = jnp.float32
BF16 = jnp.bfloat16

FOX_HEADS = 16
MLA_HEADS = 8
QK_NOPE_DIM = 128
ROPE_BASE = 10000.0
EPS = 1e-6

ADAM_LR = 0.001
ADAM_B1 = 0.9
ADAM_B2 = 0.999
ADAM_EPS = 1e-08
ADAM_WD = 0.01
ADAM_STEP = 10

N_CHIPS = 4
N_DEV = 8
PACK_LANES = 1024
PACK_PART_ROWS = 16
SMALL_PART_ROWS = 8
PACK_ROWS_MULT = 1024
VMEM_LIMIT_BYTES = 48 * 1024 * 1024
LANE_TILE = 128
MATMUL_BLOCK = 1024
ATTN_BLOCK_Q = 1024
ATTN_BLOCK_K = 1024
NEG_BIG = -1e30
MESH = pl.DeviceIdType.MESH


def _round_up(n, m):
    return -(-n // m) * m


def _blk(dim, pref, mult=128):
    if dim <= pref:
        return dim
    b = (pref // mult) * mult
    while b >= mult:
        if dim % b == 0:
            return b
        b -= mult
    return dim


def _params(sem=None):
    return pltpu.CompilerParams(dimension_semantics=sem, vmem_limit_bytes=VMEM_LIMIT_BYTES)


_DIMS = {"nn": (((1,), (0,)), ((), ())), "nt": (((1,), (1,)), ((), ())), "tn": (((0,), (0,)), ((), ()))}


def _matmul(a, b, *, mode, out_dtypes, name, epilogue=None, extras=()):
    if mode == "tn":
        kdim, m = a.shape
    else:
        m, kdim = a.shape
    n = b.shape[0] if mode == "nt" else b.shape[1]
    bm, bn, bk = _blk(m, MATMUL_BLOCK), _blk(n, MATMUL_BLOCK), _blk(kdim, MATMUL_BLOCK)
    nk = kdim // bk
    n_extra, n_out = len(extras), len(out_dtypes)
    dims = _DIMS[mode]

    def body(a_ref, b_ref, *rest):
        extra_refs = rest[:n_extra]
        out_refs = rest[n_extra:n_extra + n_out]

        def finish(acc):
            res = (acc,) if epilogue is None else epilogue(acc, *[r[...] for r in extra_refs])
            for o_ref, r in zip(out_refs, res):
                o_ref[...] = r.astype(o_ref.dtype)

        part = lax.dot_general(a_ref[...].astype(BF16), b_ref[...].astype(BF16), dims, preferred_element_type=F32)
        if nk == 1:
            finish(part)
            return
        acc_ref = rest[n_extra + n_out]
        k = pl.program_id(2)

        @pl.when(k == 0)
        def _():
            acc_ref[...] = part

        @pl.when((k > 0) & (k < nk - 1))
        def _():
            acc_ref[...] += part

        @pl.when(k == nk - 1)
        def _():
            finish(acc_ref[...] + part)

    if mode == "tn":
        a_spec = pl.BlockSpec((bk, bm), lambda i, j, k: (k, i))
    else:
        a_spec = pl.BlockSpec((bm, bk), lambda i, j, k: (i, k))
    if mode == "nt":
        b_spec = pl.BlockSpec((bn, bk), lambda i, j, k: (j, k))
    else:
        b_spec = pl.BlockSpec((bk, bn), lambda i, j, k: (k, j))
    tile = pl.BlockSpec((bm, bn), lambda i, j, k: (i, j))
    outs = pl.pallas_call(
        body, name=name,
        grid=(m // bm, n // bn, nk),
        in_specs=[a_spec, b_spec] + [tile] * n_extra,
        out_specs=[tile] * n_out,
        out_shape=[jax.ShapeDtypeStruct((m, n), dt) for dt in out_dtypes],
        scratch_shapes=[pltpu.VMEM((bm, bn), F32)] if nk > 1 else [],
        compiler_params=_params(("parallel", "parallel", "arbitrary")),
    )(a, b, *extras)
    return outs[0] if n_out == 1 else outs


def _rms_fwd(x, gains, name):
    s, w = x.shape
    g = gains.shape[0]
    bs = _blk(s, 512, 8)

    def body(x_ref, g_ref, *out_refs):
        xv = x_ref[...]
        y = xv * lax.rsqrt(jnp.mean(xv * xv, axis=-1, keepdims=True) + EPS)
        for i, o_ref in enumerate(out_refs):
            o_ref[...] = (y * g_ref[i:i + 1, :]).astype(o_ref.dtype)

    row = pl.BlockSpec((bs, w), lambda i: (i, 0))
    outs = pl.pallas_call(
        body, name=name, grid=(s // bs,),
        in_specs=[row, pl.BlockSpec((g, w), lambda i: (0, 0))],
        out_specs=[row] * g,
        out_shape=[jax.ShapeDtypeStruct((s, w), BF16)] * g,
        compiler_params=_params(("parallel",)),
    )(x, gains)
    return outs


def _rms_bwd(x, branches, resid, name):
    s, w = x.shape
    nb = len(branches)
    bs = _blk(s, 512, 8)
    has_resid = resid is not None

    def body(x_ref, *rest):
        g_refs = rest[:nb]
        dy_refs = rest[nb:2 * nb]
        pos = 2 * nb
        r_ref = rest[pos] if has_resid else None
        pos += int(has_resid)
        dx_ref = rest[pos]
        dg_refs = rest[pos + 1:pos + 1 + nb]
        i = pl.program_id(0)

        @pl.when(i == 0)
        def _():
            for dg_ref in dg_refs:
                dg_ref[...] = jnp.zeros_like(dg_ref)

        xv = x_ref[...]
        rstd = lax.rsqrt(jnp.mean(xv * xv, axis=-1, keepdims=True) + EPS)
        xhat = xv * rstd
        dx = r_ref[...] if has_resid else jnp.zeros_like(xv)
        for g_ref, dy_ref, dg_ref in zip(g_refs, dy_refs, dg_refs):
            dy = dy_ref[...].astype(F32)
            dyg = dy * g_ref[...]
            dx = dx + rstd * (dyg - xhat * jnp.mean(dyg * xhat, axis=-1, keepdims=True))
            dg_ref[...] += jnp.sum(dy * xhat, axis=0, keepdims=True)
        dx_ref[...] = dx

    row = pl.BlockSpec((bs, w), lambda i: (i, 0))
    vec = pl.BlockSpec((1, w), lambda i: (0, 0))
    args = [x] + [g for g, _ in branches] + [dy for _, dy in branches] + ([resid] if has_resid else [])
    outs = pl.pallas_call(
        body, name=name, grid=(s // bs,),
        in_specs=[row] + [vec] * nb + [row] * nb + ([row] if has_resid else []),
        out_specs=[row] + [vec] * nb,
        out_shape=[jax.ShapeDtypeStruct((s, w), F32)] + [jax.ShapeDtypeStruct((1, w), F32)] * nb,
        compiler_params=_params(("arbitrary",)),
    )(*args)
    return outs[0], list(outs[1:])


def _loss_head(x, g, target, name):
    s, w = x.shape
    bs = _blk(s, 512, 8)

    def body(x_ref, g_ref, t_ref, loss_ref, dx_ref, dg_ref):
        i = pl.program_id(0)

        @pl.when(i == 0)
        def _():
            loss_ref[...] = jnp.zeros_like(loss_ref)
            dg_ref[...] = jnp.zeros_like(dg_ref)

        xv = x_ref[...]
        gv = g_ref[...]
        rstd = lax.rsqrt(jnp.mean(xv * xv, axis=-1, keepdims=True) + EPS)
        xhat = xv * rstd
        err = xhat * gv - t_ref[...]
        loss_ref[...] += 0.5 * jnp.sum(jnp.mean(err * err, axis=-1, keepdims=True))
        dy = err * (1.0 / w)
        dyg = dy * gv
        dx_ref[...] = rstd * (dyg - xhat * jnp.mean(dyg * xhat, axis=-1, keepdims=True))
        dg_ref[...] += jnp.sum(dy * xhat, axis=0, keepdims=True)

    row = pl.BlockSpec((bs, w), lambda i: (i, 0))
    vec = pl.BlockSpec((1, w), lambda i: (0, 0))
    return pl.pallas_call(
        body, name=name, grid=(s // bs,),
        in_specs=[row, vec, row],
        out_specs=[pl.BlockSpec((8, 128), lambda i: (0, 0)), row, vec],
        out_shape=[jax.ShapeDtypeStruct((8, 128), F32), jax.ShapeDtypeStruct((s, w), F32),
                   jax.ShapeDtypeStruct((1, w), F32)],
        compiler_params=_params(("arbitrary",)),
    )(x, g, target)


def _rope(a, b, cos, sin, sign, name):
    g, s, w = a.shape
    bs = _blk(s, 1024, 8)

    def body(a_ref, b_ref, c_ref, s_ref, o1_ref, o2_ref):
        av = jnp.sum(a_ref[...].astype(F32), axis=0)
        bv = jnp.sum(b_ref[...].astype(F32), axis=0)
        cv, sv = c_ref[...], s_ref[...] * sign
        o1_ref[...] = av * cv - bv * sv
        o2_ref[...] = bv * cv + av * sv

    grp = pl.BlockSpec((g, bs, w), lambda i: (0, i, 0))
    row = pl.BlockSpec((bs, w), lambda i: (i, 0))
    return pl.pallas_call(
        body, name=name, grid=(s // bs,),
        in_specs=[grp, grp, row, row], out_specs=[row, row],
        out_shape=[jax.ShapeDtypeStruct((s, w), F32)] * 2,
        compiler_params=_params(("parallel",)),
    )(a, b, cos, sin)


def _causal_table(s, bq, bk, q_major):
    nq, nk = s // bq, s // bk
    rows = []
    if q_major:
        for qi in range(nq):
            kmax = (qi * bq + bq - 1) // bk
            for ki in range(kmax + 1):
                rows.append((qi, ki, int(ki * bk + bk - 1 > qi * bq), int(ki == 0), int(ki == kmax)))
    else:
        for ki in range(nk):
            qmin = (ki * bk) // bq
            for qi in range(qmin, nq):
                rows.append((qi, ki, int(ki * bk + bk - 1 > qi * bq), int(qi == qmin), int(qi == nq - 1)))
    return jnp.asarray(np.array(rows, np.int32).T)


def _causal_keep(qi, ki, bq, bk, transposed):
    if transposed:
        kpos = ki * bk + lax.broadcasted_iota(jnp.int32, (bk, bq), 0)
        qpos = qi * bq + lax.broadcasted_iota(jnp.int32, (bk, bq), 1)
    else:
        qpos = qi * bq + lax.broadcasted_iota(jnp.int32, (bq, bk), 0)
        kpos = ki * bk + lax.broadcasted_iota(jnp.int32, (bq, bk), 1)
    return kpos <= qpos


_NT = (((1,), (1,)), ((), ()))
_NN = (((1,), (0,)), ((), ()))


def _attn_specs(bq, bk):
    qspec = lambda d: pl.BlockSpec((None, bq, d), lambda hh, t, tb: (hh, tb[0, t], 0))
    kspec = lambda d: pl.BlockSpec((None, bk, d), lambda hh, t, tb: (hh, tb[1, t], 0))
    return qspec, kspec


def _flash_fwd(qa, ka, va, l_col, name):
    h, s, da = qa.shape
    dv = va.shape[-1]
    bq, bk = _blk(s, ATTN_BLOCK_Q), _blk(s, ATTN_BLOCK_K)
    tab = _causal_table(s, bq, bk, True)

    def body(tab_ref, q_ref, k_ref, v_ref, o_ref, lse_ref, m_sc, acc_sc):
        t = pl.program_id(1)
        qi, ki = tab_ref[0, t], tab_ref[1, t]

        @pl.when(tab_ref[3, t] == 1)
        def _():
            m_sc[...] = jnp.full_like(m_sc, NEG_BIG)
            acc_sc[...] = jnp.zeros_like(acc_sc)

        def step(masked):
            sc = lax.dot_general(q_ref[...], k_ref[...], _NT, preferred_element_type=F32)
            if masked:
                sc = jnp.where(_causal_keep(qi, ki, bq, bk, False), sc, NEG_BIG)
            m_prev = m_sc[...]
            m_new = jnp.maximum(m_prev, jnp.max(sc, axis=-1, keepdims=True))
            p = jnp.exp(sc - m_new).astype(BF16)
            acc_sc[...] = jnp.exp(m_prev - m_new) * acc_sc[...] + lax.dot_general(
                p, v_ref[...], _NN, preferred_element_type=F32)
            m_sc[...] = m_new

        @pl.when(tab_ref[2, t] == 1)
        def _():
            step(True)

        @pl.when(tab_ref[2, t] == 0)
        def _():
            step(False)

        @pl.when(tab_ref[4, t] == 1)
        def _():
            acc = acc_sc[...]
            lane = lax.broadcasted_iota(jnp.int32, acc.shape, 1)
            l = jnp.sum(jnp.where(lane == l_col, acc, 0.0), axis=-1, keepdims=True)
            o_ref[...] = (acc / l).astype(o_ref.dtype)
            lse_ref[...] = m_sc[...] + jnp.log(l)

    qspec, kspec = _attn_specs(bq, bk)
    return pl.pallas_call(
        body, name=name,
        grid_spec=pltpu.PrefetchScalarGridSpec(
            num_scalar_prefetch=1, grid=(h, tab.shape[1]),
            in_specs=[qspec(da), kspec(da), kspec(dv)],
            out_specs=[qspec(dv), qspec(1)],
            scratch_shapes=[pltpu.VMEM((bq, 1), F32), pltpu.VMEM((bq, dv), F32)]),
        out_shape=[jax.ShapeDtypeStruct((h, s, dv), BF16), jax.ShapeDtypeStruct((h, s, 1), F32)],
        compiler_params=_params(("parallel", "arbitrary")),
    )(tab, qa, ka, va)


def _attn_delta(do, o, name):
    h, s, dv = o.shape
    bs = _blk(s, 1024, 8)

    def body(do_ref, o_ref, d_ref):
        d_ref[...] = jnp.sum(do_ref[...].astype(F32) * o_ref[...].astype(F32), axis=-1, keepdims=True)

    blk = lambda d: pl.BlockSpec((None, bs, d), lambda hh, i: (hh, i, 0))
    return pl.pallas_call(
        body, name=name, grid=(h, s // bs), in_specs=[blk(dv), blk(dv)], out_specs=blk(1),
        out_shape=jax.ShapeDtypeStruct((h, s, 1), F32),
        compiler_params=_params(("parallel", "parallel")),
    )(do, o)


def _flash_bwd_dq(qa, ka, va, doa, name):
    h, s, da = qa.shape
    dv = va.shape[-1]
    bq, bk = _blk(s, ATTN_BLOCK_Q), _blk(s, ATTN_BLOCK_K)
    tab = _causal_table(s, bq, bk, True)

    def body(tab_ref, q_ref, k_ref, v_ref, do_ref, dq_ref, acc_sc):
        t = pl.program_id(1)
        qi, ki = tab_ref[0, t], tab_ref[1, t]

        @pl.when(tab_ref[3, t] == 1)
        def _():
            acc_sc[...] = jnp.zeros_like(acc_sc)

        def step(masked):
            kv = k_ref[...]
            sc = lax.dot_general(q_ref[...], kv, _NT, preferred_element_type=F32)
            if masked:
                sc = jnp.where(_causal_keep(qi, ki, bq, bk, False), sc, NEG_BIG)
            dp = lax.dot_general(do_ref[...], v_ref[...], _NT, preferred_element_type=F32)
            ds = (jnp.exp(sc) * dp).astype(BF16)
            acc_sc[...] += lax.dot_general(ds, kv, _NN, preferred_element_type=F32)

        @pl.when(tab_ref[2, t] == 1)
        def _():
            step(True)

        @pl.when(tab_ref[2, t] == 0)
        def _():
            step(False)

        @pl.when(tab_ref[4, t] == 1)
        def _():
            dq_ref[...] = acc_sc[...]

    qspec, kspec = _attn_specs(bq, bk)
    return pl.pallas_call(
        body, name=name,
        grid_spec=pltpu.PrefetchScalarGridSpec(
            num_scalar_prefetch=1, grid=(h, tab.shape[1]),
            in_specs=[qspec(da), kspec(da), kspec(dv), qspec(dv)],
            out_specs=qspec(da), scratch_shapes=[pltpu.VMEM((bq, da), F32)]),
        out_shape=jax.ShapeDtypeStruct((h, s, da), F32),
        compiler_params=_params(("parallel", "arbitrary")),
    )(tab, qa, ka, va, doa)


def _flash_bwd_dkv(qa, ka, va, doa, name):
    h, s, da = qa.shape
    dv = va.shape[-1]
    bq, bk = _blk(s, ATTN_BLOCK_Q), _blk(s, ATTN_BLOCK_K)
    tab = _causal_table(s, bq, bk, False)

    def body(tab_ref, q_ref, k_ref, v_ref, do_ref, dk_ref, dv_ref, dk_sc, dv_sc):
        t = pl.program_id(1)
        qi, ki = tab_ref[0, t], tab_ref[1, t]

        @pl.when(tab_ref[3, t] == 1)
        def _():
            dk_sc[...] = jnp.zeros_like(dk_sc)
            dv_sc[...] = jnp.zeros_like(dv_sc)

        def step(masked):
            qv, dov = q_ref[...], do_ref[...]
            st = lax.dot_general(k_ref[...], qv, _NT, preferred_element_type=F32)
            if masked:
                st = jnp.where(_causal_keep(qi, ki, bq, bk, True), st, NEG_BIG)
            pt = jnp.exp(st)
            dv_sc[...] += lax.dot_general(pt.astype(BF16), dov, _NN, preferred_element_type=F32)
            dpt = lax.dot_general(v_ref[...], dov, _NT, preferred_element_type=F32)
            dk_sc[...] += lax.dot_general((pt * dpt).astype(BF16), qv, _NN, preferred_element_type=F32)

        @pl.when(tab_ref[2, t] == 1)
        def _():
            step(True)

        @pl.when(tab_ref[2, t] == 0)
        def _():
            step(False)

        @pl.when(tab_ref[4, t] == 1)
        def _():
            dk_ref[...] = dk_sc[...]
            dv_ref[...] = dv_sc[...]

    qspec, kspec = _attn_specs(bq, bk)
    return pl.pallas_call(
        body, name=name,
        grid_spec=pltpu.PrefetchScalarGridSpec(
            num_scalar_prefetch=1, grid=(h, tab.shape[1]),
            in_specs=[qspec(da), kspec(da), kspec(dv), qspec(dv)],
            out_specs=[kspec(da), kspec(dv)],
            scratch_shapes=[pltpu.VMEM((bk, da), F32), pltpu.VMEM((bk, dv), F32)]),
        out_shape=[jax.ShapeDtypeStruct((h, s, da), F32), jax.ShapeDtypeStruct((h, s, dv), F32)],
        compiler_params=_params(("parallel", "arbitrary")),
    )(tab, qa, ka, va, doa)


def _split3(x):
    hi = lax.reduce_precision(x, 8, 7)
    rest = x - hi
    mid = lax.reduce_precision(rest, 8, 7)
    lo = lax.reduce_precision(rest - mid, 8, 7)
    return jnp.stack([hi, mid, lo], axis=-1).astype(BF16)


def _augment(parts, width):
    h, s = parts[0].shape[:2]
    used = sum(p.shape[-1] for p in parts)
    return jnp.concatenate(list(parts) + [jnp.zeros((h, s, width - used), BF16)], axis=-1)


def _adamw(w, g, m, v, name):
    r, wd = w.shape
    br = _blk(r, 512, 8)

    def body(w_ref, g_ref, m_ref, v_ref, d_ref, nm_ref, nv_ref):
        gv = g_ref[...]
        mn = ADAM_B1 * m_ref[...] + (1.0 - ADAM_B1) * gv
        vn = ADAM_B2 * v_ref[...] + (1.0 - ADAM_B2) * (gv * gv)
        m_hat = mn / (1.0 - ADAM_B1 ** ADAM_STEP)
        v_hat = vn / (1.0 - ADAM_B2 ** ADAM_STEP)
        d_ref[...] = -ADAM_LR * (m_hat / (jnp.sqrt(v_hat) + ADAM_EPS) + ADAM_WD * w_ref[...])
        nm_ref[...] = mn
        nv_ref[...] = vn

    row = pl.BlockSpec((br, wd), lambda i: (i, 0))
    return pl.pallas_call(
        body, name=name, grid=(r // br,), in_specs=[row] * 4, out_specs=[row] * 3,
        out_shape=[jax.ShapeDtypeStruct((r, wd), F32)] * 3,
        compiler_params=_params(("parallel",)),
    )(w, g, m, v)


_ANY = pl.BlockSpec(memory_space=pl.ANY)


def _place():
    x, y, c = lax.axis_index("x"), lax.axis_index("y"), lax.axis_index("c")
    chips = [(x, 1 - y), (1 - x, y), (1 - x, 1 - y)]
    return x, y, c, chips


def _all_gather_shards(shard, name):
    r, w = shard.shape
    hr = r // 2

    def body(x_ref, out_ref, send_sems, recv_sems):
        x, y, c, chips = _place()
        sibling = (x, y, 1 - c)

        def rows(j, half):
            return out_ref.at[j, pl.ds(pl.multiple_of(half * hr, 16), hr), :]

        def copy(sem, j, half, to, src=None):
            return pltpu.make_async_remote_copy(
                src_ref=rows(j, half) if src is None else src, dst_ref=rows(j, half),
                send_sem=send_sems.at[sem], recv_sem=recv_sems.at[sem], device_id=to, device_id_type=MESH)

        my_half = x_ref.at[pl.ds(pl.multiple_of(c * hr, 16), hr), :]
        first = [copy(j, j, c, (cx, cy, c), src=my_half) for j, (cx, cy) in enumerate(chips)]
        for cp in first:
            cp.start()
        passed = []
        for j in range(3):
            copy(j, j, c, (x, y, c)).wait_recv()
            fw = copy(3 + j, j, c, sibling)
            fw.start()
            passed.append(fw)
        for j in range(3):
            copy(3 + j, j, 1 - c, (x, y, c)).wait_recv()
        for cp in first + passed:
            cp.wait_send()

    return pl.pallas_call(
        body, name=name, in_specs=[_ANY], out_specs=_ANY,
        out_shape=jax.ShapeDtypeStruct((N_CHIPS - 1, r, w), shard.dtype),
        scratch_shapes=[pltpu.SemaphoreType.DMA((6,)), pltpu.SemaphoreType.DMA((6,))],
        compiler_params=pltpu.CompilerParams(vmem_limit_bytes=VMEM_LIMIT_BYTES),
    )(shard)


def _sibling_swap_halves(g, name):
    nq, r, w = g.shape
    hr = r // 2

    def body(g_ref, a_ref, send_sems, recv_sems):
        x, y, c, _ = _place()
        sibling = (x, y, 1 - c)
        cps = []
        for q in range(nq):
            cp = pltpu.make_async_remote_copy(
                src_ref=g_ref.at[q, pl.ds(pl.multiple_of((1 - c) * hr, 8), hr), :], dst_ref=a_ref.at[q],
                send_sem=send_sems.at[q], recv_sem=recv_sems.at[q], device_id=sibling, device_id_type=MESH)
            cp.start()
            cps.append(cp)
        for cp in cps:
            cp.wait()

    return pl.pallas_call(
        body, name=name, in_specs=[_ANY], out_specs=_ANY,
        out_shape=jax.ShapeDtypeStruct((nq, hr, w), g.dtype),
        scratch_shapes=[pltpu.SemaphoreType.DMA((nq,)), pltpu.SemaphoreType.DMA((nq,))],
        compiler_params=pltpu.CompilerParams(vmem_limit_bytes=VMEM_LIMIT_BYTES),
    )(g)


def _chip_sum(g, a, c_idx, name):
    nq, r, w = g.shape
    hr = r // 2
    br = _blk(hr, 512, 16)
    nb = hr // br

    def body(c_ref, g_ref, a_ref, o_ref):
        o_ref[...] = (g_ref[...] + a_ref[...]).astype(o_ref.dtype)

    return pl.pallas_call(
        body, name=name,
        grid_spec=pltpu.PrefetchScalarGridSpec(
            num_scalar_prefetch=1, grid=(nq, nb),
            in_specs=[pl.BlockSpec((None, br, w), lambda q, i, cr: (q, cr[0] * nb + i, 0)),
                      pl.BlockSpec((None, br, w), lambda q, i, cr: (q, i, 0))],
            out_specs=pl.BlockSpec((None, br, w), lambda q, i, cr: (q, i, 0))),
        out_shape=jax.ShapeDtypeStruct((nq, hr, w), BF16),
        compiler_params=_params(("parallel", "parallel")),
    )(c_idx, g, a)


def _chip_exchange(s4, name):
    nq, hr, w = s4.shape

    def body(s_ref, b_ref, send_sems, recv_sems):
        x, y, c, chips = _place()
        cps = []
        for j, (cx, cy) in enumerate(chips):
            cp = pltpu.make_async_remote_copy(
                src_ref=s_ref.at[2 * cx + cy], dst_ref=b_ref.at[j],
                send_sem=send_sems.at[j], recv_sem=recv_sems.at[j], device_id=(cx, cy, c), device_id_type=MESH)
            cp.start()
            cps.append(cp)
        for cp in cps:
            cp.wait()

    return pl.pallas_call(
        body, name=name, in_specs=[_ANY], out_specs=_ANY,
        out_shape=jax.ShapeDtypeStruct((nq - 1, hr, w), s4.dtype),
        scratch_shapes=[pltpu.SemaphoreType.DMA((3,)), pltpu.SemaphoreType.DMA((3,))],
        compiler_params=pltpu.CompilerParams(vmem_limit_bytes=VMEM_LIMIT_BYTES),
    )(s4)


def _sum_chips(s4, b3, p_idx, name):
    _, hr, w = s4.shape
    nb3 = b3.shape[0]
    br = _blk(hr, 512, 16)

    def body(p_ref, s_ref, b_ref, o_ref):
        acc = s_ref[...].astype(F32)
        for j in range(nb3):
            acc = acc + b_ref[j].astype(F32)
        o_ref[...] = acc

    return pl.pallas_call(
        body, name=name,
        grid_spec=pltpu.PrefetchScalarGridSpec(
            num_scalar_prefetch=1, grid=(hr // br,),
            in_specs=[pl.BlockSpec((None, br, w), lambda i, pr: (pr[0], i, 0)),
                      pl.BlockSpec((nb3, br, w), lambda i, pr: (0, i, 0))],
            out_specs=pl.BlockSpec((br, w), lambda i, pr: (i, 0))),
        out_shape=jax.ShapeDtypeStruct((hr, w), F32),
        compiler_params=_params(("parallel",)),
    )(p_idx, s4, b3)


def _sibling_swap(t, name):
    hr, w = t.shape

    def body(t_ref, o_ref, send_sem, recv_sem):
        x, y, c, _ = _place()
        cp = pltpu.make_async_remote_copy(src_ref=t_ref, dst_ref=o_ref, send_sem=send_sem, recv_sem=recv_sem,
                                          device_id=(x, y, 1 - c), device_id_type=MESH)
        cp.start()
        cp.wait()

    return pl.pallas_call(
        body, name=name, in_specs=[_ANY], out_specs=_ANY,
        out_shape=jax.ShapeDtypeStruct((hr, w), t.dtype),
        scratch_shapes=[pltpu.SemaphoreType.DMA, pltpu.SemaphoreType.DMA],
        compiler_params=pltpu.CompilerParams(vmem_limit_bytes=VMEM_LIMIT_BYTES),
    )(t)


def _all_reduce_small(v, name):
    r, w = v.shape

    def body(v_ref, o_ref, slots, send_sems, recv_sems):
        x, y, c, _ = _place()
        me = 4 * x + 2 * y + c
        slots[me] = v_ref[...]
        cps = []
        for k in range(1, N_DEV):
            fx, fy, fc = (k >> 2) & 1, (k >> 1) & 1, k & 1
            to = (x ^ fx, y ^ fy, c ^ fc)
            cp = pltpu.make_async_remote_copy(
                src_ref=v_ref, dst_ref=slots.at[me], send_sem=send_sems.at[k - 1], recv_sem=recv_sems.at[k - 1],
                device_id=to, device_id_type=MESH)
            cp.start()
            cps.append(cp)
        for k in range(1, N_DEV):
            fx, fy, fc = (k >> 2) & 1, (k >> 1) & 1, k & 1
            src_dev = 4 * (x ^ fx) + 2 * (y ^ fy) + (c ^ fc)
            pltpu.make_async_remote_copy(
                src_ref=v_ref, dst_ref=slots.at[src_dev], send_sem=send_sems.at[k - 1],
                recv_sem=recv_sems.at[k - 1], device_id=(x, y, c), device_id_type=MESH).wait_recv()
        for cp in cps:
            cp.wait_send()
        acc = slots[0]
        for d in range(1, N_DEV):
            acc = acc + slots[d]
        o_ref[...] = acc

    return pl.pallas_call(
        body, name=name,
        in_specs=[pl.BlockSpec(memory_space=pltpu.VMEM)], out_specs=pl.BlockSpec(memory_space=pltpu.VMEM),
        out_shape=jax.ShapeDtypeStruct((r, w), F32),
        scratch_shapes=[pltpu.VMEM((N_DEV, r, w), F32), pltpu.SemaphoreType.DMA((N_DEV - 1,)),
                        pltpu.SemaphoreType.DMA((N_DEV - 1,))],
        compiler_params=pltpu.CompilerParams(vmem_limit_bytes=VMEM_LIMIT_BYTES),
    )(v)


def _part_rows(shape, part_rows=PACK_PART_ROWS):
    assert shape[-1] <= PACK_LANES
    return _round_up(math.prod(shape[:-1]), part_rows)


def _packed_rows(shapes):
    return _round_up(sum(_part_rows(s) for s in shapes), PACK_ROWS_MULT)


def _pack(arrs, total_rows, dtype, part_rows=PACK_PART_ROWS):
    parts = []
    for a in arrs:
        a2 = a.reshape(-1, a.shape[-1]).astype(dtype)
        rows = _part_rows(a.shape, part_rows)
        parts.append(jnp.pad(a2, ((0, rows - a2.shape[0]), (0, PACK_LANES - a2.shape[1]))))
    used = sum(p.shape[0] for p in parts)
    if total_rows > used:
        parts.append(jnp.zeros((total_rows - used, PACK_LANES), dtype))
    return jnp.concatenate(parts, axis=0)


def _unpack(packed, shapes, part_rows=PACK_PART_ROWS):
    out, r0 = [], 0
    for s in shapes:
        out.append(packed[r0:r0 + math.prod(s[:-1]), :s[-1]].reshape(s))
        r0 += _part_rows(s, part_rows)
    return out


_BIG = (("fox_w_in", 2), ("fox_w_out", 1), ("mla_w_kv_a", 0), ("mla_w_kv_b", 1), ("mla_w_q_a", 1),
        ("mla_w_q_b", 2), ("mla_w_out", 1), ("ffn_w_up", 2), ("ffn_w_down", 1))
_SMALL = ("norm_mix_g", "norm_ffn_g", "fox_b_f", "kv_norm_g", "mla_kv_a_norm_g", "mla_q_a_norm_g", "final_norm_g")
_WEIGHTS = ("norm_mix_g", "norm_ffn_g", "fox_w_in", "fox_b_f", "fox_w_out", "kv_norm_g", "mla_w_kv_a",
            "mla_kv_a_norm_g", "mla_w_kv_b", "mla_w_q_a", "mla_q_a_norm_g", "mla_w_q_b", "mla_w_out",
            "ffn_w_up", "ffn_w_down", "final_norm_g")


def _heads_first(t, heads):
    s = t.shape[0]
    return jnp.transpose(t.reshape(s, heads, -1), (1, 0, 2))


def _heads_last(t):
    h, s, d = t.shape
    return jnp.transpose(t, (1, 0, 2)).reshape(s, h * d)


def _ffn_fwd(x, h, w_up, w_down, tag):
    def relu_sq(acc):
        r = jnp.maximum(acc, 0.0)
        return r, r * r

    r, a = _matmul(h, w_up, mode="nn", out_dtypes=(BF16, BF16), epilogue=relu_sq, name=f"{tag}_up")
    x_out = _matmul(a, w_down, mode="nn", out_dtypes=(F32,), epilogue=lambda acc, res: (acc + res,),
                    extras=(x,), name=f"{tag}_down")
    return x_out, r, a


def _ffn_bwd(dx_out, x_in, h, r, a, g_norm, w_up, w_down, tag):
    d_u = _matmul(dx_out, w_down, mode="nt", out_dtypes=(BF16,), epilogue=lambda acc, rr: (acc * (2.0 * rr.astype(F32)),),
                  extras=(r,), name=f"{tag}_d_act")
    d_w_down = _matmul(a, dx_out, mode="tn", out_dtypes=(F32,), name=f"{tag}_d_w_down")
    d_w_up = _matmul(h, d_u, mode="tn", out_dtypes=(F32,), name=f"{tag}_d_w_up")
    d_h = _matmul(d_u, w_up, mode="nt", out_dtypes=(F32,), name=f"{tag}_d_h")
    dx_in, (d_g,) = _rms_bwd(x_in, [(g_norm, d_h)], dx_out, name=f"{tag}_d_norm")
    return dx_in, d_w_up, d_w_down, d_g


def kernel(x, norm_mix_g, norm_ffn_g, fox_w_in, fox_b_f, fox_w_out, kv_norm_g, mla_w_kv_a, mla_kv_a_norm_g, mla_w_kv_b, mla_w_q_a, mla_q_a_norm_g, mla_w_q_b, mla_w_out, ffn_w_up, ffn_w_down, final_norm_g, loss_target, m_norm_mix_g, m_norm_ffn_g, m_fox_w_in, m_fox_b_f, m_fox_w_out, m_kv_norm_g, m_mla_w_kv_a, m_mla_kv_a_norm_g, m_mla_w_kv_b, m_mla_w_q_a, m_mla_q_a_norm_g, m_mla_w_q_b, m_mla_w_out, m_ffn_w_up, m_ffn_w_down, m_final_norm_g, v_norm_mix_g, v_norm_ffn_g, v_fox_w_in, v_fox_b_f, v_fox_w_out, v_kv_norm_g, v_mla_w_kv_a, v_mla_kv_a_norm_g, v_mla_w_kv_b, v_mla_w_q_a, v_mla_q_a_norm_g, v_mla_w_q_b, v_mla_w_out, v_ffn_w_up, v_ffn_w_down, v_final_norm_g):
    args = dict(locals())
    w_in = {n: args[n] for n in _WEIGHTS}
    m_in = {n: args["m_" + n] for n in _WEIGHTS}
    v_in = {n: args["v_" + n] for n in _WEIGHTS}

    xs = x[0]
    seq, d_model = xs.shape
    tgt = loss_target[0]
    fox_h, mla_h, nope = FOX_HEADS, MLA_HEADS, QK_NOPE_DIM
    kv_rank = mla_kv_a_norm_g.shape[0]
    rope = mla_w_kv_a.shape[1] - kv_rank
    half = rope // 2
    q_rank = mla_q_a_norm_g.shape[1]
    v_dim = mla_w_kv_b.shape[1] * N_CHIPS // mla_h - nope
    fox_w = fox_w_out.shape[1] * N_CHIPS
    fox_dh = fox_w // fox_h

    big_names = [n for n, _ in _BIG]
    shard_shapes = [w_in[n].shape for n in big_names]
    rows = _packed_rows(shard_shapes)
    my_shard = _pack([w_in[n] for n in big_names], rows, BF16)
    others = _all_gather_shards(my_shard, name="gather_weights")
    by_relation = jnp.concatenate([my_shard[None], others], axis=0)
    p_chip = 2 * lax.axis_index("x") + lax.axis_index("y")
    full = {}
    for q in range(N_CHIPS):
        shard_q = lax.dynamic_index_in_dim(by_relation, p_chip ^ q, axis=0, keepdims=False)
        for (n, ax), piece in zip(_BIG, _unpack(shard_q, shard_shapes)):
            full.setdefault(n, []).append(piece)
    full = {n: jnp.concatenate(full[n], axis=ax) for n, ax in _BIG}

    w_fox_in = full["fox_w_in"][0]
    w_qkv, w_gate = w_fox_in[:, :3 * fox_w], w_fox_in[:, 3 * fox_w:]
    w_fox_out = full["fox_w_out"][0]
    w_kv_a, w_kv_b = full["mla_w_kv_a"], full["mla_w_kv_b"]
    w_q_a = full["mla_w_q_a"][0]
    w_q_b3 = full["mla_w_q_b"][0].reshape(q_rank, mla_h, nope + rope)
    w_q_b = jnp.concatenate([w_q_b3[:, :, :nope].reshape(q_rank, -1),
                             w_q_b3[:, :, nope:nope + half].reshape(q_rank, -1),
                             w_q_b3[:, :, nope + half:].reshape(q_rank, -1)], axis=1)
    w_mla_out = full["mla_w_out"][0]
    w_up, w_down = full["ffn_w_up"], full["ffn_w_down"]

    inv = 1.0 / (ROPE_BASE ** (jnp.arange(0, rope, 2, dtype=F32) / rope))
    ang = jnp.arange(seq, dtype=F32)[:, None] * inv[None, :]
    cos, sin = jnp.cos(ang), jnp.sin(ang)
    cos_q, sin_q = jnp.tile(cos, (1, mla_h)), jnp.tile(sin, (1, mla_h))

    (h0,) = _rms_fwd(xs, norm_mix_g[0:1], name="l0_norm_mix")
    qkv = _matmul(h0, w_qkv, mode="nn", out_dtypes=(BF16,), name="fox_qkv")
    gate = _matmul(h0, w_gate, mode="nn", out_dtypes=(F32,), name="fox_gate")
    z = gate + fox_b_f[0][None, :]
    cum = jnp.cumsum(jax.nn.log_sigmoid(z), axis=0)
    fox_scale = fox_dh ** -0.5
    fox_da = _round_up(fox_dh + 9, LANE_TILE)
    fox_dva = _round_up(fox_dh + 4, LANE_TILE)
    c3 = _split3(jnp.transpose(cum))
    ones3 = jnp.ones((fox_h, seq, 3), BF16)
    fq = _heads_first(qkv[:, :fox_w], fox_h) * fox_scale
    fqa = _augment([fq, c3, ones3], fox_da)
    fka = _augment([_heads_first(qkv[:, fox_w:2 * fox_w], fox_h), ones3, -c3, ones3], fox_da)
    fva = _augment([_heads_first(qkv[:, 2 * fox_w:], fox_h), -ones3, ones3[:, :, :1]], fox_dva)
    foa, f_lse = _flash_fwd(fqa, fka, fva, fox_dh + 3, name="fox_attn")
    fo = foa[:, :, :fox_dh]
    ctx0 = _heads_last(fo)
    add_res = lambda acc, res: (acc + res,)
    x1 = _matmul(ctx0, w_fox_out, mode="nn", out_dtypes=(F32,), epilogue=add_res, extras=(xs,), name="fox_out")
    (h1,) = _rms_fwd(x1, norm_ffn_g[0:1], name="l0_norm_ffn")
    x2, r0, a0 = _ffn_fwd(x1, h1, w_up[0], w_down[0], "ffn0")

    src, h2 = _rms_fwd(x2, jnp.stack([kv_norm_g, norm_mix_g[1]]), name="l1_norm_kv_mix")
    kv_a = _matmul(src, w_kv_a, mode="nn", out_dtypes=(F32,), name="mla_kv_a")
    ckv_pre = kv_a[:, :kv_rank]
    (c_kv,) = _rms_fwd(ckv_pre, mla_kv_a_norm_g[None, :], name="mla_norm_kv_a")
    kr1, kr2 = _rope(kv_a[None, :, kv_rank:kv_rank + half], kv_a[None, :, kv_rank + half:], cos, sin, 1.0,
                     name="mla_rope_k")
    kv_b = _matmul(c_kv, w_kv_b, mode="nn", out_dtypes=(BF16,), name="mla_kv_b").reshape(seq, mla_h, nope + v_dim)
    cq_pre = _matmul(h2, w_q_a, mode="nn", out_dtypes=(F32,), name="mla_q_a")
    (c_q,) = _rms_fwd(cq_pre, mla_q_a_norm_g, name="mla_norm_q_a")
    qf = _matmul(c_q, w_q_b, mode="nn", out_dtypes=(F32,), name="mla_q_b")
    n_nope = mla_h * nope
    n_half = mla_h * half
    qr1, qr2 = _rope(qf[None, :, n_nope:n_nope + n_half], qf[None, :, n_nope + n_half:], cos_q, sin_q, 1.0,
                     name="mla_rope_q")
    mla_scale = (nope + rope) ** -0.5
    mla_dk = nope + rope
    mla_da = _round_up(mla_dk + 3, LANE_TILE)
    mla_dva = _round_up(v_dim + 4, LANE_TILE)
    mq = jnp.concatenate([qf[:, :n_nope].reshape(seq, mla_h, nope), qr1.reshape(seq, mla_h, half),
                          qr2.reshape(seq, mla_h, half)], axis=-1) * mla_scale
    mq = jnp.transpose(mq.astype(BF16), (1, 0, 2))
    k_rope = jnp.concatenate([kr1, kr2], axis=-1).astype(BF16)
    m_ones3 = jnp.ones((mla_h, seq, 3), BF16)
    mqa = _augment([mq], mla_da)
    mka = _augment([jnp.transpose(kv_b[:, :, :nope], (1, 0, 2)),
                    jnp.broadcast_to(k_rope[None], (mla_h, seq, rope)), m_ones3], mla_da)
    mva = _augment([jnp.transpose(kv_b[:, :, nope:], (1, 0, 2)), -m_ones3, m_ones3[:, :, :1]], mla_dva)
    moa, m_lse = _flash_fwd(mqa, mka, mva, v_dim + 3, name="mla_attn")
    mo = moa[:, :, :v_dim]
    ctx1 = _heads_last(mo)
    x3 = _matmul(ctx1, w_mla_out, mode="nn", out_dtypes=(F32,), epilogue=add_res, extras=(x2,), name="mla_out")
    (h3,) = _rms_fwd(x3, norm_ffn_g[1:2], name="l1_norm_ffn")
    x4, r1, a1 = _ffn_fwd(x3, h3, w_up[1], w_down[1], "ffn1")

    loss_tile, dx4, d_final_g = _loss_head(x4, final_norm_g[None, :], tgt, name="loss_head")
    loss = lax.psum(loss_tile[0, 0], ("x", "y", "c"))

    gw = {}
    dx3, d_up1, d_down1, d_nf1 = _ffn_bwd(dx4, x3, h3, r1, a1, norm_ffn_g[1:2], w_up[1], w_down[1], "ffn1")

    d_ctx1 = _matmul(dx3, w_mla_out, mode="nt", out_dtypes=(BF16,), name="mla_d_ctx")
    gw["mla_w_out"] = _matmul(ctx1, dx3, mode="tn", out_dtypes=(F32,), name="mla_d_w_out")[None]
    d_mo = _heads_first(d_ctx1, mla_h)
    m_delta = _attn_delta(d_mo, mo, name="mla_attn_delta")
    mqb = _augment([mq, -_split3(m_lse[:, :, 0])], mla_da)
    d_moa = _augment([d_mo, _split3(m_delta[:, :, 0])], mla_dva)
    d_mqa = _flash_bwd_dq(mqb, mka, mva, d_moa, name="mla_attn_dq")
    d_mka, d_mva = _flash_bwd_dkv(mqb, mka, mva, d_moa, name="mla_attn_dkv")
    d_mq = jnp.transpose(d_mqa[:, :, :mla_dk], (1, 0, 2)) * mla_scale
    d_mk, d_mv = d_mka[:, :, :mla_dk], d_mva[:, :, :v_dim]
    d_qr1, d_qr2 = _rope(d_mq[None, :, :, nope:nope + half].reshape(1, seq, n_half),
                         d_mq[None, :, :, nope + half:].reshape(1, seq, n_half), cos_q, sin_q, -1.0,
                         name="mla_rope_dq")
    d_qf = jnp.concatenate([d_mq[:, :, :nope].reshape(seq, n_nope), d_qr1, d_qr2], axis=1)
    d_w_q_b = _matmul(c_q, d_qf, mode="tn", out_dtypes=(F32,), name="mla_d_w_q_b")
    gw["mla_w_q_b"] = jnp.concatenate([d_w_q_b[:, :n_nope].reshape(q_rank, mla_h, nope),
                                       d_w_q_b[:, n_nope:n_nope + n_half].reshape(q_rank, mla_h, half),
                                       d_w_q_b[:, n_nope + n_half:].reshape(q_rank, mla_h, half)],
                                      axis=-1).reshape(1, q_rank, mla_h * (nope + rope))
    d_c_q = _matmul(d_qf, w_q_b, mode="nt", out_dtypes=(F32,), name="mla_d_c_q")
    d_cq_pre, (d_q_a_g,) = _rms_bwd(cq_pre, [(mla_q_a_norm_g, d_c_q)], None, name="mla_d_norm_q_a")
    gw["mla_w_q_a"] = _matmul(h2, d_cq_pre, mode="tn", out_dtypes=(F32,), name="mla_d_w_q_a")[None]
    d_h2 = _matmul(d_cq_pre, w_q_a, mode="nt", out_dtypes=(F32,), name="mla_d_h")

    d_kv_b = jnp.concatenate([jnp.transpose(d_mk[:, :, :nope], (1, 0, 2)), jnp.transpose(d_mv, (1, 0, 2))],
                             axis=-1).reshape(seq, mla_h * (nope + v_dim))
    gw["mla_w_kv_b"] = _matmul(c_kv, d_kv_b, mode="tn", out_dtypes=(F32,), name="mla_d_w_kv_b")
    d_c_kv = _matmul(d_kv_b, w_kv_b, mode="nt", out_dtypes=(F32,), name="mla_d_c_kv")
    d_ckv_pre, (d_kv_a_g,) = _rms_bwd(ckv_pre, [(mla_kv_a_norm_g[None, :], d_c_kv)], None, name="mla_d_norm_kv_a")
    d_kr1, d_kr2 = _rope(d_mk[:, :, nope:nope + half], d_mk[:, :, nope + half:], cos, sin, -1.0, name="mla_rope_dk")
    d_kv_a = jnp.concatenate([d_ckv_pre, d_kr1, d_kr2], axis=1)
    gw["mla_w_kv_a"] = _matmul(src, d_kv_a, mode="tn", out_dtypes=(F32,), name="mla_d_w_kv_a")
    d_src = _matmul(d_kv_a, w_kv_a, mode="nt", out_dtypes=(F32,), name="mla_d_src")
    dx2, (d_kv_g, d_nm1) = _rms_bwd(x2, [(kv_norm_g[None, :], d_src), (norm_mix_g[1:2], d_h2)], dx3,
                                    name="l1_d_norm_kv_mix")

    dx1, d_up0, d_down0, d_nf0 = _ffn_bwd(dx2, x1, h1, r0, a0, norm_ffn_g[0:1], w_up[0], w_down[0], "ffn0")
    gw["ffn_w_up"] = jnp.stack([d_up0, d_up1])
    gw["ffn_w_down"] = jnp.stack([d_down0, d_down1])

    d_ctx0 = _matmul(dx1, w_fox_out, mode="nt", out_dtypes=(BF16,), name="fox_d_ctx")
    gw["fox_w_out"] = _matmul(ctx0, dx1, mode="tn", out_dtypes=(F32,), name="fox_d_w_out")[None]
    d_fo = _heads_first(d_ctx0, fox_h)
    f_delta = _attn_delta(d_fo, fo, name="fox_attn_delta")
    fqb = _augment([fq, c3, ones3, -_split3(f_lse[:, :, 0])], fox_da)
    d_foa = _augment([d_fo, _split3(f_delta[:, :, 0])], fox_dva)
    d_fqa = _flash_bwd_dq(fqb, fka, fva, d_foa, name="fox_attn_dq")
    d_fka, d_fva = _flash_bwd_dkv(fqb, fka, fva, d_foa, name="fox_attn_dkv")
    d_qkv = jnp.concatenate([_heads_last(d_fqa[:, :, :fox_dh] * fox_scale), _heads_last(d_fka[:, :, :fox_dh]),
                             _heads_last(d_fva[:, :, :fox_dh])], axis=1)
    d_cum = jnp.transpose(d_fqa[:, :, fox_dh] - d_fka[:, :, fox_dh + 3])
    d_z = jnp.flip(jnp.cumsum(jnp.flip(d_cum, 0), axis=0), 0) * jax.nn.sigmoid(-z)
    d_b_f = jnp.sum(d_z, axis=0)
    d_w_qkv = _matmul(h0, d_qkv, mode="tn", out_dtypes=(F32,), name="fox_d_w_qkv")
    d_w_gate = _matmul(h0, d_z, mode="tn", out_dtypes=(F32,), name="fox_d_w_gate")
    gw["fox_w_in"] = jnp.concatenate([d_w_qkv, d_w_gate], axis=1)[None]
    d_h0g = _matmul(d_z, w_gate, mode="nt", out_dtypes=(F32,), name="fox_d_h_gate")
    d_h0 = _matmul(d_qkv, w_qkv, mode="nt", out_dtypes=(F32,), epilogue=add_res, extras=(d_h0g,), name="fox_d_h")
    grad_x, (d_nm0,) = _rms_bwd(xs, [(norm_mix_g[0:1], d_h0)], dx1, name="l0_d_norm_mix")

    c_idx = lax.axis_index("c").astype(jnp.int32).reshape(1)
    per_chip = []
    for q in range(N_CHIPS):
        pieces = [jnp.split(gw[n], N_CHIPS, axis=ax)[q] for n, ax in _BIG]
        per_chip.append(_pack(pieces, rows, F32))
    g4 = jnp.stack(per_chip)
    a4 = _sibling_swap_halves(g4, name="grads_to_sibling")
    s4 = _chip_sum(g4, a4, c_idx, name="grads_chip_sum")
    b3 = _chip_exchange(s4, name="grads_between_chips")
    t_mine = _sum_chips(s4, b3, p_chip.astype(jnp.int32).reshape(1), name="grads_sum_chips")
    t_theirs = _sibling_swap(t_mine, name="grads_join_halves")
    is_south = lax.axis_index("c") == 0
    g_big = jnp.concatenate([jnp.where(is_south, t_mine, t_theirs), jnp.where(is_south, t_theirs, t_mine)],
                            axis=0)

    small_local = {"norm_mix_g": jnp.concatenate([d_nm0, d_nm1], axis=0),
                   "norm_ffn_g": jnp.concatenate([d_nf0, d_nf1], axis=0),
                   "fox_b_f": d_b_f[None, :], "kv_norm_g": d_kv_g[0], "mla_kv_a_norm_g": d_kv_a_g[0],
                   "mla_q_a_norm_g": d_q_a_g, "final_norm_g": d_final_g[0]}
    small_shapes = [w_in[n].shape for n in _SMALL]
    small_rows = sum(_part_rows(s, SMALL_PART_ROWS) for s in small_shapes)
    pack_small = lambda arrs: _pack(arrs, small_rows, F32, SMALL_PART_ROWS)
    g_small = _all_reduce_small(pack_small([small_local[n] for n in _SMALL]), name="grads_small")

    d_big, nm_big, nv_big = _adamw(_pack([w_in[n] for n in big_names], rows, F32), g_big,
                                   _pack([m_in[n] for n in big_names], rows, F32),
                                   _pack([v_in[n] for n in big_names], rows, F32), name="adamw_big")
    d_sm, nm_sm, nv_sm = _adamw(pack_small([w_in[n] for n in _SMALL]), g_small,
                                pack_small([m_in[n] for n in _SMALL]),
                                pack_small([v_in[n] for n in _SMALL]), name="adamw_small")

    def spread(big, small):
        out = dict(zip(big_names, _unpack(big, shard_shapes)))
        out.update(zip(_SMALL, _unpack(small, small_shapes, SMALL_PART_ROWS)))
        return [out[n] for n in _WEIGHTS]

    return (loss, grad_x[None], *spread(g_big, g_small), *spread(d_big, d_sm), *spread(nm_big, nm_sm),
            *spread(nv_big, nv_sm))
```

```python
import math

import numpy as np
import jax
import jax.numpy as jnp
from jax import lax
from jax.experimental import pallas as pl
from jax.experimental.pallas import tpu as pltpu

F32 = jnp.float32
BF16 = jnp.bfloat16

FOX_HEADS = 16
MLA_HEADS = 8
QK_NOPE_DIM = 128
ROPE_BASE = 10000.0
EPS = 1e-6

ADAM_LR = 0.001
ADAM_B1 = 0.9
ADAM_B2 = 0.999
ADAM_EPS = 1e-08
ADAM_WD = 0.01
ADAM_STEP = 10

N_CHIPS = 4
N_DEV = 8
PACK_LANES = 1024
PACK_PART_ROWS = 16
SMALL_PART_ROWS = 8
PACK_ROWS_MULT = 1024
VMEM_LIMIT_BYTES = 48 * 1024 * 1024
LANE_TILE = 128
MATMUL_BLOCK = 1024
ATTN_BLOCK_Q = 1024
ATTN_BLOCK_K = 1024
NEG_BIG = -1e30
MESH = pl.DeviceIdType.MESH


def _round_up(n, m):
    return -(-n // m) * m


def _blk(dim, pref, mult=128):
    if dim <= pref:
        return dim
    b = (pref // mult) * mult
    while b >= mult:
        if dim % b == 0:
            return b
        b -= mult
    return dim


def _params(sem=None):
    return pltpu.CompilerParams(dimension_semantics=sem, vmem_limit_bytes=VMEM_LIMIT_BYTES)


_DIMS = {"nn": (((1,), (0,)), ((), ())), "nt": (((1,), (1,)), ((), ())), "tn": (((0,), (0,)), ((), ()))}


def _matmul(a, b, *, mode, out_dtypes, name, epilogue=None, extras=(), placed=None):
    if mode == "tn":
        kdim, m = a.shape
    else:
        m, kdim = a.shape
    n = b.shape[0] if mode == "nt" else b.shape[1]
    bm, bn, bk = _blk(m, MATMUL_BLOCK), _blk(n, MATMUL_BLOCK), _blk(kdim, MATMUL_BLOCK)
    nk = kdim // bk
    n_extra, n_out = len(extras), len(out_dtypes)
    n_placed = 0 if placed is None else 2
    dims = _DIMS[mode]

    def body(a_ref, b_ref, *rest):
        placed_refs = rest[:n_placed]
        rest = rest[n_placed:]
        extra_refs = rest[:n_extra]
        out_refs = rest[n_extra:n_extra + n_out]

        def finish(acc):
            if n_placed:
                acc = acc + lax.dot_general(placed_refs[0][...], placed_refs[1][...], _DIMS["nn"],
                                            preferred_element_type=F32)
            res = (acc,) if epilogue is None else epilogue(acc, *[r[...] for r in extra_refs])
            for o_ref, r in zip(out_refs, res):
                o_ref[...] = r.astype(o_ref.dtype)

        part = lax.dot_general(a_ref[...].astype(BF16), b_ref[...].astype(BF16), dims, preferred_element_type=F32)
        if nk == 1:
            finish(part)
            return
        acc_ref = rest[n_extra + n_out]
        k = pl.program_id(2)

        @pl.when(k == 0)
        def _():
            acc_ref[...] = part

        @pl.when((k > 0) & (k < nk - 1))
        def _():
            acc_ref[...] += part

        @pl.when(k == nk - 1)
        def _():
            finish(acc_ref[...] + part)

    if mode == "tn":
        a_spec = pl.BlockSpec((bk, bm), lambda i, j, k: (k, i))
    else:
        a_spec = pl.BlockSpec((bm, bk), lambda i, j, k: (i, k))
    if mode == "nt":
        b_spec = pl.BlockSpec((bn, bk), lambda i, j, k: (j, k))
    else:
        b_spec = pl.BlockSpec((bk, bn), lambda i, j, k: (k, j))
    tile = pl.BlockSpec((bm, bn), lambda i, j, k: (i, j))
    placed_specs = []
    if n_placed:
        k2 = placed[0].shape[1]
        placed_specs = [pl.BlockSpec((bm, k2), lambda i, j, k: (i, 0)), pl.BlockSpec((k2, bn), lambda i, j, k: (0, j))]
    outs = pl.pallas_call(
        body, name=name,
        grid=(m // bm, n // bn, nk),
        in_specs=[a_spec, b_spec] + placed_specs + [tile] * n_extra,
        out_specs=[tile] * n_out,
        out_shape=[jax.ShapeDtypeStruct((m, n), dt) for dt in out_dtypes],
        scratch_shapes=[pltpu.VMEM((bm, bn), F32)] if nk > 1 else [],
        compiler_params=_params(("parallel", "parallel", "arbitrary")),
    )(a, b, *(placed or ()), *extras)
    return outs[0] if n_out == 1 else outs


def _rms_fwd(x, gains, name):
    s = x.shape[0]
    g, w = gains.shape
    bs = _blk(s, 512, 8)

    def body(x_ref, g_ref, *out_refs):
        xv = x_ref[...]
        y = xv * lax.rsqrt(jnp.mean(xv * xv, axis=-1, keepdims=True) + EPS)
        for i, o_ref in enumerate(out_refs):
            o_ref[...] = (y * g_ref[i:i + 1, :]).astype(o_ref.dtype)

    row = pl.BlockSpec((bs, w), lambda i: (i, 0))
    outs = pl.pallas_call(
        body, name=name, grid=(s // bs,),
        in_specs=[row, pl.BlockSpec((g, w), lambda i: (0, 0))],
        out_specs=[row] * g,
        out_shape=[jax.ShapeDtypeStruct((s, w), BF16)] * g,
        compiler_params=_params(("parallel",)),
    )(x, gains)
    return outs


def _rms_bwd(x, branches, resid, name):
    s = x.shape[0]
    w = branches[0][0].shape[1]
    nb = len(branches)
    bs = _blk(s, 512, 8)
    has_resid = resid is not None

    def body(x_ref, *rest):
        g_refs = rest[:nb]
        dy_refs = rest[nb:2 * nb]
        pos = 2 * nb
        r_ref = rest[pos] if has_resid else None
        pos += int(has_resid)
        dx_ref = rest[pos]
        dg_refs = rest[pos + 1:pos + 1 + nb]
        i = pl.program_id(0)

        @pl.when(i == 0)
        def _():
            for dg_ref in dg_refs:
                dg_ref[...] = jnp.zeros_like(dg_ref)

        xv = x_ref[...]
        rstd = lax.rsqrt(jnp.mean(xv * xv, axis=-1, keepdims=True) + EPS)
        xhat = xv * rstd
        dx = r_ref[...] if has_resid else jnp.zeros_like(xv)
        for g_ref, dy_ref, dg_ref in zip(g_refs, dy_refs, dg_refs):
            dy = dy_ref[...].astype(F32)
            dyg = dy * g_ref[...]
            dx = dx + rstd * (dyg - xhat * jnp.mean(dyg * xhat, axis=-1, keepdims=True))
            dg_ref[...] += jnp.sum(dy * xhat, axis=0, keepdims=True)
        dx_ref[...] = dx

    row = pl.BlockSpec((bs, w), lambda i: (i, 0))
    vec = pl.BlockSpec((1, w), lambda i: (0, 0))
    args = [x] + [g for g, _ in branches] + [dy for _, dy in branches] + ([resid] if has_resid else [])
    outs = pl.pallas_call(
        body, name=name, grid=(s // bs,),
        in_specs=[row] + [vec] * nb + [row] * nb + ([row] if has_resid else []),
        out_specs=[row] + [vec] * nb,
        out_shape=[jax.ShapeDtypeStruct((s, w), F32)] + [jax.ShapeDtypeStruct((1, w), F32)] * nb,
        compiler_params=_params(("arbitrary",)),
    )(*args)
    return outs[0], list(outs[1:])


def _loss_head(x, g, target, name):
    s, w = x.shape
    bs = _blk(s, 512, 8)

    def body(x_ref, g_ref, t_ref, loss_ref, dx_ref, dg_ref):
        i = pl.program_id(0)

        @pl.when(i == 0)
        def _():
            loss_ref[...] = jnp.zeros_like(loss_ref)
            dg_ref[...] = jnp.zeros_like(dg_ref)

        xv = x_ref[...]
        gv = g_ref[...]
        rstd = lax.rsqrt(jnp.mean(xv * xv, axis=-1, keepdims=True) + EPS)
        xhat = xv * rstd
        err = xhat * gv - t_ref[...]
        loss_ref[...] += 0.5 * jnp.sum(jnp.mean(err * err, axis=-1, keepdims=True))
        dy = err * (1.0 / w)
        dyg = dy * gv
        dx_ref[...] = rstd * (dyg - xhat * jnp.mean(dyg * xhat, axis=-1, keepdims=True))
        dg_ref[...] += jnp.sum(dy * xhat, axis=0, keepdims=True)

    row = pl.BlockSpec((bs, w), lambda i: (i, 0))
    vec = pl.BlockSpec((1, w), lambda i: (0, 0))
    return pl.pallas_call(
        body, name=name, grid=(s // bs,),
        in_specs=[row, vec, row],
        out_specs=[pl.BlockSpec((8, 128), lambda i: (0, 0)), row, vec],
        out_shape=[jax.ShapeDtypeStruct((8, 128), F32), jax.ShapeDtypeStruct((s, w), F32),
                   jax.ShapeDtypeStruct((1, w), F32)],
        compiler_params=_params(("arbitrary",)),
    )(x, g, target)


def _rope(a, b, cos, sin, sign, name):
    g, s, w = a.shape
    bs = _blk(s, 1024, 8)

    def body(a_ref, b_ref, c_ref, s_ref, o1_ref, o2_ref):
        av = jnp.sum(a_ref[...].astype(F32), axis=0)
        bv = jnp.sum(b_ref[...].astype(F32), axis=0)
        cv, sv = c_ref[...], s_ref[...] * sign
        o1_ref[...] = av * cv - bv * sv
        o2_ref[...] = bv * cv + av * sv

    grp = pl.BlockSpec((g, bs, w), lambda i: (0, i, 0))
    row = pl.BlockSpec((bs, w), lambda i: (i, 0))
    return pl.pallas_call(
        body, name=name, grid=(s // bs,),
        in_specs=[grp, grp, row, row], out_specs=[row, row],
        out_shape=[jax.ShapeDtypeStruct((s, w), F32)] * 2,
        compiler_params=_params(("parallel",)),
    )(a, b, cos, sin)


def _causal_table(s, bq, bk, q_major):
    nq, nk = s // bq, s // bk
    rows = []
    if q_major:
        for qi in range(nq):
            kmax = (qi * bq + bq - 1) // bk
            for ki in range(kmax + 1):
                rows.append((qi, ki, int(ki * bk + bk - 1 > qi * bq), int(ki == 0), int(ki == kmax)))
    else:
        for ki in range(nk):
            qmin = (ki * bk) // bq
            for qi in range(qmin, nq):
                rows.append((qi, ki, int(ki * bk + bk - 1 > qi * bq), int(qi == qmin), int(qi == nq - 1)))
    return jnp.asarray(np.array(rows, np.int32).T)


def _causal_keep(qi, ki, bq, bk, transposed):
    if transposed:
        kpos = ki * bk + lax.broadcasted_iota(jnp.int32, (bk, bq), 0)
        qpos = qi * bq + lax.broadcasted_iota(jnp.int32, (bk, bq), 1)
    else:
        qpos = qi * bq + lax.broadcasted_iota(jnp.int32, (bq, bk), 0)
        kpos = ki * bk + lax.broadcasted_iota(jnp.int32, (bq, bk), 1)
    return kpos <= qpos


_NT = (((1,), (1,)), ((), ()))
_NN = (((1,), (0,)), ((), ()))


def _attn_specs(bq, bk):
    qspec = lambda d: pl.BlockSpec((bq, d), lambda hh, t, tb: (tb[0, t], hh))
    kspec = lambda d: pl.BlockSpec((bk, d), lambda hh, t, tb: (tb[1, t], hh))
    return qspec, kspec


def _split3_cols(x):
    hi = x.astype(BF16).astype(F32)
    rest = x - hi
    mid = rest.astype(BF16).astype(F32)
    lo = (rest - mid).astype(BF16).astype(F32)
    return hi, mid, lo


def _place3(base, col, pieces, sign):
    lane = lax.broadcasted_iota(jnp.int32, base.shape, 1)
    out = base.astype(F32)
    for i, piece in enumerate(pieces):
        out = jnp.where(lane == col + i, sign * piece, out)
    return out.astype(BF16)


def _flash_fwd(qa, ka, va, heads, l_col, lse_col, name):
    s = qa.shape[0]
    da, dv = qa.shape[1] // heads, va.shape[1] // heads
    h = heads
    bq, bk = _blk(s, ATTN_BLOCK_Q), _blk(s, ATTN_BLOCK_K)
    tab = _causal_table(s, bq, bk, True)

    def body(tab_ref, q_ref, k_ref, v_ref, o_ref, qb_ref, m_sc, acc_sc):
        t = pl.program_id(1)
        qi, ki = tab_ref[0, t], tab_ref[1, t]

        @pl.when(tab_ref[3, t] == 1)
        def _():
            m_sc[...] = jnp.full_like(m_sc, NEG_BIG)
            acc_sc[...] = jnp.zeros_like(acc_sc)

        def step(masked):
            sc = lax.dot_general(q_ref[...], k_ref[...], _NT, preferred_element_type=F32)
            if masked:
                sc = jnp.where(_causal_keep(qi, ki, bq, bk, False), sc, NEG_BIG)
            m_prev = m_sc[...]
            m_new = jnp.maximum(m_prev, jnp.max(sc, axis=-1, keepdims=True))
            p = jnp.exp(sc - m_new).astype(BF16)
            acc_sc[...] = jnp.exp(m_prev - m_new) * acc_sc[...] + lax.dot_general(
                p, v_ref[...], _NN, preferred_element_type=F32)
            m_sc[...] = m_new

        @pl.when(tab_ref[2, t] == 1)
        def _():
            step(True)

        @pl.when(tab_ref[2, t] == 0)
        def _():
            step(False)

        @pl.when(tab_ref[4, t] == 1)
        def _():
            acc = acc_sc[...]
            lane = lax.broadcasted_iota(jnp.int32, acc.shape, 1)
            l = jnp.sum(jnp.where(lane == l_col, acc, 0.0), axis=-1, keepdims=True)
            o_ref[...] = (acc / l).astype(o_ref.dtype)
            lse = m_sc[...] + jnp.log(l)
            qb_ref[...] = _place3(q_ref[...], lse_col, _split3_cols(lse), -1.0)

    qspec, kspec = _attn_specs(bq, bk)
    return pl.pallas_call(
        body, name=name,
        grid_spec=pltpu.PrefetchScalarGridSpec(
            num_scalar_prefetch=1, grid=(h, tab.shape[1]),
            in_specs=[qspec(da), kspec(da), kspec(dv)],
            out_specs=[qspec(dv), qspec(da)],
            scratch_shapes=[pltpu.VMEM((bq, 1), F32), pltpu.VMEM((bq, dv), F32)]),
        out_shape=[jax.ShapeDtypeStruct((s, h * dv), BF16), jax.ShapeDtypeStruct((s, h * da), BF16)],
        compiler_params=_params(("parallel", "arbitrary")),
    )(tab, qa, ka, va)


def _delta_place(do, o, heads, delta_col, name):
    s = o.shape[0]
    dv = o.shape[1] // heads
    bs = _blk(s, 1024, 8)

    def body(do_ref, o_ref, out_ref):
        dov = do_ref[...]
        delta = jnp.sum(dov.astype(F32) * o_ref[...].astype(F32), axis=-1, keepdims=True)
        out_ref[...] = _place3(dov, delta_col, _split3_cols(delta), 1.0)

    blk = pl.BlockSpec((bs, dv), lambda i, hh: (i, hh))
    return pl.pallas_call(
        body, name=name, grid=(s // bs, heads), in_specs=[blk, blk], out_specs=blk,
        out_shape=jax.ShapeDtypeStruct(do.shape, BF16),
        compiler_params=_params(("parallel", "parallel")),
    )(do, o)


def _flash_bwd_dq(qa, ka, va, doa, heads, name):
    s = qa.shape[0]
    h, da, dv = heads, qa.shape[1] // heads, va.shape[1] // heads
    bq, bk = _blk(s, ATTN_BLOCK_Q), _blk(s, ATTN_BLOCK_K)
    tab = _causal_table(s, bq, bk, True)

    def body(tab_ref, q_ref, k_ref, v_ref, do_ref, dq_ref, acc_sc):
        t = pl.program_id(1)
        qi, ki = tab_ref[0, t], tab_ref[1, t]

        @pl.when(tab_ref[3, t] == 1)
        def _():
            acc_sc[...] = jnp.zeros_like(acc_sc)

        def step(masked):
            kv = k_ref[...]
            sc = lax.dot_general(q_ref[...], kv, _NT, preferred_element_type=F32)
            if masked:
                sc = jnp.where(_causal_keep(qi, ki, bq, bk, False), sc, NEG_BIG)
            dp = lax.dot_general(do_ref[...], v_ref[...], _NT, preferred_element_type=F32)
            ds = (jnp.exp(sc) * dp).astype(BF16)
            acc_sc[...] += lax.dot_general(ds, kv, _NN, preferred_element_type=F32)

        @pl.when(tab_ref[2, t] == 1)
        def _():
            step(True)

        @pl.when(tab_ref[2, t] == 0)
        def _():
            step(False)

        @pl.when(tab_ref[4, t] == 1)
        def _():
            dq_ref[...] = acc_sc[...]

    qspec, kspec = _attn_specs(bq, bk)
    return pl.pallas_call(
        body, name=name,
        grid_spec=pltpu.PrefetchScalarGridSpec(
            num_scalar_prefetch=1, grid=(h, tab.shape[1]),
            in_specs=[qspec(da), kspec(da), kspec(dv), qspec(dv)],
            out_specs=qspec(da), scratch_shapes=[pltpu.VMEM((bq, da), F32)]),
        out_shape=jax.ShapeDtypeStruct((s, h * da), F32),
        compiler_params=_params(("parallel", "arbitrary")),
    )(tab, qa, ka, va, doa)


def _flash_bwd_dkv(qa, ka, va, doa, heads, name):
    s = qa.shape[0]
    h, da, dv = heads, qa.shape[1] // heads, va.shape[1] // heads
    bq, bk = _blk(s, ATTN_BLOCK_Q), _blk(s, ATTN_BLOCK_K)
    tab = _causal_table(s, bq, bk, False)

    def body(tab_ref, q_ref, k_ref, v_ref, do_ref, dk_ref, dv_ref, dk_sc, dv_sc):
        t = pl.program_id(1)
        qi, ki = tab_ref[0, t], tab_ref[1, t]

        @pl.when(tab_ref[3, t] == 1)
        def _():
            dk_sc[...] = jnp.zeros_like(dk_sc)
            dv_sc[...] = jnp.zeros_like(dv_sc)

        def step(masked):
            qv, dov = q_ref[...], do_ref[...]
            st = lax.dot_general(k_ref[...], qv, _NT, preferred_element_type=F32)
            if masked:
                st = jnp.where(_causal_keep(qi, ki, bq, bk, True), st, NEG_BIG)
            pt = jnp.exp(st)
            dv_sc[...] += lax.dot_general(pt.astype(BF16), dov, _NN, preferred_element_type=F32)
            dpt = lax.dot_general(v_ref[...], dov, _NT, preferred_element_type=F32)
            dk_sc[...] += lax.dot_general((pt * dpt).astype(BF16), qv, _NN, preferred_element_type=F32)

        @pl.when(tab_ref[2, t] == 1)
        def _():
            step(True)

        @pl.when(tab_ref[2, t] == 0)
        def _():
            step(False)

        @pl.when(tab_ref[4, t] == 1)
        def _():
            dk_ref[...] = dk_sc[...]
            dv_ref[...] = dv_sc[...]

    qspec, kspec = _attn_specs(bq, bk)
    return pl.pallas_call(
        body, name=name,
        grid_spec=pltpu.PrefetchScalarGridSpec(
            num_scalar_prefetch=1, grid=(h, tab.shape[1]),
            in_specs=[qspec(da), kspec(da), kspec(dv), qspec(dv)],
            out_specs=[kspec(da), kspec(dv)],
            scratch_shapes=[pltpu.VMEM((bk, da), F32), pltpu.VMEM((bk, dv), F32)]),
        out_shape=[jax.ShapeDtypeStruct((s, h * da), F32), jax.ShapeDtypeStruct((s, h * dv), F32)],
        compiler_params=_params(("parallel", "arbitrary")),
    )(tab, qa, ka, va, doa)


def _split3(x):
    hi = lax.reduce_precision(x, 8, 7)
    rest = x - hi
    mid = lax.reduce_precision(rest, 8, 7)
    lo = lax.reduce_precision(rest - mid, 8, 7)
    return jnp.stack([hi, mid, lo], axis=-1).astype(BF16)


def _pad_heads(w, heads, width, axis):
    shape = list(w.shape)
    d = shape[axis] // heads
    w = w.reshape(shape[:axis] + [heads, d] + shape[axis + 1:])
    pad = [(0, 0)] * w.ndim
    pad[axis + 1] = (0, width - d)
    return jnp.pad(w, pad).reshape(shape[:axis] + [heads * width] + shape[axis + 1:])


def _unpad_heads(w, heads, d, axis):
    shape = list(w.shape)
    width = shape[axis] // heads
    w = w.reshape(shape[:axis] + [heads, width] + shape[axis + 1:])
    w = lax.slice_in_dim(w, 0, d, axis=axis + 1)
    return w.reshape(shape[:axis] + [heads * d] + shape[axis + 1:])


def _placement(rows, heads, width, entries):
    e = np.zeros((rows, heads * width), np.float32)
    for row, col, val in entries:
        for hh in range(heads):
            e[row(hh) if callable(row) else row, hh * width + col] = val
    return jnp.asarray(e, BF16)


def _rope_mix(a, b, cos_t, sin_t, scale, heads, name):
    s = a.shape[0]
    d = a.shape[1] // heads
    bs = _blk(s, 1024, 8)

    def body(a_ref, b_ref, c_ref, s_ref, o_ref):
        o_ref[...] = ((a_ref[...] * c_ref[...] + b_ref[...] * s_ref[...]) * scale).astype(o_ref.dtype)

    blk = pl.BlockSpec((bs, d), lambda i, hh: (i, hh))
    tbl = pl.BlockSpec((bs, d), lambda i, hh: (i, 0))
    return pl.pallas_call(
        body, name=name, grid=(s // bs, heads), in_specs=[blk, blk, tbl, tbl], out_specs=blk,
        out_shape=jax.ShapeDtypeStruct(a.shape, BF16),
        compiler_params=_params(("parallel", "parallel")),
    )(a, b, cos_t, sin_t)


def _rope_unmix(g, cos_t, sin_t, scale, heads, name):
    s = g.shape[0]
    d = g.shape[1] // heads
    bs = _blk(s, 1024, 8)

    def body(g_ref, c_ref, s_ref, da_ref, db_ref):
        gv = g_ref[...] * scale
        da_ref[...] = (gv * c_ref[...]).astype(da_ref.dtype)
        db_ref[...] = (gv * s_ref[...]).astype(db_ref.dtype)

    blk = pl.BlockSpec((bs, d), lambda i, hh: (i, hh))
    tbl = pl.BlockSpec((bs, d), lambda i, hh: (i, 0))
    return pl.pallas_call(
        body, name=name, grid=(s // bs, heads), in_specs=[blk, tbl, tbl], out_specs=[blk, blk],
        out_shape=[jax.ShapeDtypeStruct(g.shape, BF16)] * 2,
        compiler_params=_params(("parallel", "parallel")),
    )(g, cos_t, sin_t)


def _adamw(w, g, m, v, name):
    r, wd = w.shape
    br = _blk(r, 512, 8)

    def body(w_ref, g_ref, m_ref, v_ref, d_ref, nm_ref, nv_ref):
        gv = g_ref[...]
        mn = ADAM_B1 * m_ref[...] + (1.0 - ADAM_B1) * gv
        vn = ADAM_B2 * v_ref[...] + (1.0 - ADAM_B2) * (gv * gv)
        m_hat = mn / (1.0 - ADAM_B1 ** ADAM_STEP)
        v_hat = vn / (1.0 - ADAM_B2 ** ADAM_STEP)
        d_ref[...] = -ADAM_LR * (m_hat / (jnp.sqrt(v_hat) + ADAM_EPS) + ADAM_WD * w_ref[...])
        nm_ref[...] = mn
        nv_ref[...] = vn

    row = pl.BlockSpec((br, wd), lambda i: (i, 0))
    return pl.pallas_call(
        body, name=name, grid=(r // br,), in_specs=[row] * 4, out_specs=[row] * 3,
        out_shape=[jax.ShapeDtypeStruct((r, wd), F32)] * 3,
        compiler_params=_params(("parallel",)),
    )(w, g, m, v)


_ANY = pl.BlockSpec(memory_space=pl.ANY)


def _place():
    x, y, c = lax.axis_index("x"), lax.axis_index("y"), lax.axis_index("c")
    chips = [(x, 1 - y), (1 - x, y), (1 - x, 1 - y)]
    return x, y, c, chips


def _all_gather_shards(shard, name):
    r, w = shard.shape
    hr = r // 2

    def body(x_ref, out_ref, send_sems, recv_sems):
        x, y, c, chips = _place()
        sibling = (x, y, 1 - c)

        def rows(j, half):
            return out_ref.at[j, pl.ds(pl.multiple_of(half * hr, 16), hr), :]

        def copy(sem, j, half, to, src=None):
            return pltpu.make_async_remote_copy(
                src_ref=rows(j, half) if src is None else src, dst_ref=rows(j, half),
                send_sem=send_sems.at[sem], recv_sem=recv_sems.at[sem], device_id=to, device_id_type=MESH)

        my_half = x_ref.at[pl.ds(pl.multiple_of(c * hr, 16), hr), :]
        first = [copy(j, j, c, (cx, cy, c), src=my_half) for j, (cx, cy) in enumerate(chips)]
        for cp in first:
            cp.start()
        passed = []
        for j in range(3):
            copy(j, j, c, (x, y, c)).wait_recv()
            fw = copy(3 + j, j, c, sibling)
            fw.start()
            passed.append(fw)
        for j in range(3):
            copy(3 + j, j, 1 - c, (x, y, c)).wait_recv()
        for cp in first + passed:
            cp.wait_send()

    return pl.pallas_call(
        body, name=name, in_specs=[_ANY], out_specs=_ANY,
        out_shape=jax.ShapeDtypeStruct((N_CHIPS - 1, r, w), shard.dtype),
        scratch_shapes=[pltpu.SemaphoreType.DMA((6,)), pltpu.SemaphoreType.DMA((6,))],
        compiler_params=pltpu.CompilerParams(vmem_limit_bytes=VMEM_LIMIT_BYTES),
    )(shard)


def _sibling_swap_halves(g, name):
    nq, r, w = g.shape
    hr = r // 2

    def body(g_ref, a_ref, send_sems, recv_sems):
        x, y, c, _ = _place()
        sibling = (x, y, 1 - c)
        cps = []
        for q in range(nq):
            cp = pltpu.make_async_remote_copy(
                src_ref=g_ref.at[q, pl.ds(pl.multiple_of((1 - c) * hr, 8), hr), :], dst_ref=a_ref.at[q],
                send_sem=send_sems.at[q], recv_sem=recv_sems.at[q], device_id=sibling, device_id_type=MESH)
            cp.start()
            cps.append(cp)
        for cp in cps:
            cp.wait()

    return pl.pallas_call(
        body, name=name, in_specs=[_ANY], out_specs=_ANY,
        out_shape=jax.ShapeDtypeStruct((nq, hr, w), g.dtype),
        scratch_shapes=[pltpu.SemaphoreType.DMA((nq,)), pltpu.SemaphoreType.DMA((nq,))],
        compiler_params=pltpu.CompilerParams(vmem_limit_bytes=VMEM_LIMIT_BYTES),
    )(g)


def _chip_sum(g, a, c_idx, name):
    nq, r, w = g.shape
    hr = r // 2
    br = _blk(hr, 512, 16)
    nb = hr // br

    def body(c_ref, g_ref, a_ref, o_ref):
        o_ref[...] = (g_ref[...] + a_ref[...]).astype(o_ref.dtype)

    return pl.pallas_call(
        body, name=name,
        grid_spec=pltpu.PrefetchScalarGridSpec(
            num_scalar_prefetch=1, grid=(nq, nb),
            in_specs=[pl.BlockSpec((None, br, w), lambda q, i, cr: (q, cr[0] * nb + i, 0)),
                      pl.BlockSpec((None, br, w), lambda q, i, cr: (q, i, 0))],
            out_specs=pl.BlockSpec((None, br, w), lambda q, i, cr: (q, i, 0))),
        out_shape=jax.ShapeDtypeStruct((nq, hr, w), BF16),
        compiler_params=_params(("parallel", "parallel")),
    )(c_idx, g, a)


def _chip_exchange(s4, name):
    nq, hr, w = s4.shape

    def body(s_ref, b_ref, send_sems, recv_sems):
        x, y, c, chips = _place()
        cps = []
        for j, (cx, cy) in enumerate(chips):
            cp = pltpu.make_async_remote_copy(
                src_ref=s_ref.at[2 * cx + cy], dst_ref=b_ref.at[j],
                send_sem=send_sems.at[j], recv_sem=recv_sems.at[j], device_id=(cx, cy, c), device_id_type=MESH)
            cp.start()
            cps.append(cp)
        for cp in cps:
            cp.wait()

    return pl.pallas_call(
        body, name=name, in_specs=[_ANY], out_specs=_ANY,
        out_shape=jax.ShapeDtypeStruct((nq - 1, hr, w), s4.dtype),
        scratch_shapes=[pltpu.SemaphoreType.DMA((3,)), pltpu.SemaphoreType.DMA((3,))],
        compiler_params=pltpu.CompilerParams(vmem_limit_bytes=VMEM_LIMIT_BYTES),
    )(s4)


def _sum_chips(s4, b3, p_idx, name):
    _, hr, w = s4.shape
    nb3 = b3.shape[0]
    br = _blk(hr, 512, 16)

    def body(p_ref, s_ref, b_ref, o_ref):
        acc = s_ref[...].astype(F32)
        for j in range(nb3):
            acc = acc + b_ref[j].astype(F32)
        o_ref[...] = acc

    return pl.pallas_call(
        body, name=name,
        grid_spec=pltpu.PrefetchScalarGridSpec(
            num_scalar_prefetch=1, grid=(hr // br,),
            in_specs=[pl.BlockSpec((None, br, w), lambda i, pr: (pr[0], i, 0)),
                      pl.BlockSpec((nb3, br, w), lambda i, pr: (0, i, 0))],
            out_specs=pl.BlockSpec((br, w), lambda i, pr: (i, 0))),
        out_shape=jax.ShapeDtypeStruct((hr, w), F32),
        compiler_params=_params(("parallel",)),
    )(p_idx, s4, b3)


def _sibling_swap(t, name):
    hr, w = t.shape

    def body(t_ref, o_ref, send_sem, recv_sem):
        x, y, c, _ = _place()
        cp = pltpu.make_async_remote_copy(src_ref=t_ref, dst_ref=o_ref, send_sem=send_sem, recv_sem=recv_sem,
                                          device_id=(x, y, 1 - c), device_id_type=MESH)
        cp.start()
        cp.wait()

    return pl.pallas_call(
        body, name=name, in_specs=[_ANY], out_specs=_ANY,
        out_shape=jax.ShapeDtypeStruct((hr, w), t.dtype),
        scratch_shapes=[pltpu.SemaphoreType.DMA, pltpu.SemaphoreType.DMA],
        compiler_params=pltpu.CompilerParams(vmem_limit_bytes=VMEM_LIMIT_BYTES),
    )(t)


def _all_reduce_small(v, name):
    r, w = v.shape

    def body(v_ref, o_ref, slots, send_sems, recv_sems):
        x, y, c, _ = _place()
        me = 4 * x + 2 * y + c
        slots[me] = v_ref[...]
        cps = []
        for k in range(1, N_DEV):
            fx, fy, fc = (k >> 2) & 1, (k >> 1) & 1, k & 1
            to = (x ^ fx, y ^ fy, c ^ fc)
            cp = pltpu.make_async_remote_copy(
                src_ref=v_ref, dst_ref=slots.at[me], send_sem=send_sems.at[k - 1], recv_sem=recv_sems.at[k - 1],
                device_id=to, device_id_type=MESH)
            cp.start()
            cps.append(cp)
        for k in range(1, N_DEV):
            fx, fy, fc = (k >> 2) & 1, (k >> 1) & 1, k & 1
            src_dev = 4 * (x ^ fx) + 2 * (y ^ fy) + (c ^ fc)
            pltpu.make_async_remote_copy(
                src_ref=v_ref, dst_ref=slots.at[src_dev], send_sem=send_sems.at[k - 1],
                recv_sem=recv_sems.at[k - 1], device_id=(x, y, c), device_id_type=MESH).wait_recv()
        for cp in cps:
            cp.wait_send()
        acc = slots[0]
        for d in range(1, N_DEV):
            acc = acc + slots[d]
        o_ref[...] = acc

    return pl.pallas_call(
        body, name=name,
        in_specs=[pl.BlockSpec(memory_space=pltpu.VMEM)], out_specs=pl.BlockSpec(memory_space=pltpu.VMEM),
        out_shape=jax.ShapeDtypeStruct((r, w), F32),
        scratch_shapes=[pltpu.VMEM((N_DEV, r, w), F32), pltpu.SemaphoreType.DMA((N_DEV - 1,)),
                        pltpu.SemaphoreType.DMA((N_DEV - 1,))],
        compiler_params=pltpu.CompilerParams(vmem_limit_bytes=VMEM_LIMIT_BYTES),
    )(v)


def _part_rows(shape, part_rows=PACK_PART_ROWS):
    assert shape[-1] <= PACK_LANES
    return _round_up(math.prod(shape[:-1]), part_rows)


def _packed_rows(shapes):
    return _round_up(sum(_part_rows(s) for s in shapes), PACK_ROWS_MULT)


def _pack(arrs, total_rows, dtype, part_rows=PACK_PART_ROWS):
    parts = []
    for a in arrs:
        a2 = a.reshape(-1, a.shape[-1]).astype(dtype)
        rows = _part_rows(a.shape, part_rows)
        parts.append(jnp.pad(a2, ((0, rows - a2.shape[0]), (0, PACK_LANES - a2.shape[1]))))
    used = sum(p.shape[0] for p in parts)
    if total_rows > used:
        parts.append(jnp.zeros((total_rows - used, PACK_LANES), dtype))
    return jnp.concatenate(parts, axis=0)


def _unpack(packed, shapes, part_rows=PACK_PART_ROWS):
    out, r0 = [], 0
    for s in shapes:
        out.append(packed[r0:r0 + math.prod(s[:-1]), :s[-1]].reshape(s))
        r0 += _part_rows(s, part_rows)
    return out


_BIG = (("fox_w_in", 2), ("fox_w_out", 1), ("mla_w_kv_a", 0), ("mla_w_kv_b", 1), ("mla_w_q_a", 1),
        ("mla_w_q_b", 2), ("mla_w_out", 1), ("ffn_w_up", 2), ("ffn_w_down", 1))
_SMALL = ("norm_mix_g", "norm_ffn_g", "fox_b_f", "kv_norm_g", "mla_kv_a_norm_g", "mla_q_a_norm_g", "final_norm_g")
_WEIGHTS = ("norm_mix_g", "norm_ffn_g", "fox_w_in", "fox_b_f", "fox_w_out", "kv_norm_g", "mla_w_kv_a",
            "mla_kv_a_norm_g", "mla_w_kv_b", "mla_w_q_a", "mla_q_a_norm_g", "mla_w_q_b", "mla_w_out",
            "ffn_w_up", "ffn_w_down", "final_norm_g")


def _ffn_fwd(x, h, w_up, w_down, tag):
    def relu_sq(acc):
        r = jnp.maximum(acc, 0.0)
        return r, r * r

    r, a = _matmul(h, w_up, mode="nn", out_dtypes=(BF16, BF16), epilogue=relu_sq, name=f"{tag}_up")
    x_out = _matmul(a, w_down, mode="nn", out_dtypes=(F32,), epilogue=lambda acc, res: (acc + res,),
                    extras=(x,), name=f"{tag}_down")
    return x_out, r, a


def _ffn_bwd(dx_out, x_in, h, r, a, g_norm, w_up, w_down, tag):
    d_u = _matmul(dx_out, w_down, mode="nt", out_dtypes=(BF16,), epilogue=lambda acc, rr: (acc * (2.0 * rr.astype(F32)),),
                  extras=(r,), name=f"{tag}_d_act")
    d_w_down = _matmul(a, dx_out, mode="tn", out_dtypes=(F32,), name=f"{tag}_d_w_down")
    d_w_up = _matmul(h, d_u, mode="tn", out_dtypes=(F32,), name=f"{tag}_d_w_up")
    d_h = _matmul(d_u, w_up, mode="nt", out_dtypes=(F32,), name=f"{tag}_d_h")
    dx_in, (d_g,) = _rms_bwd(x_in, [(g_norm, d_h)], dx_out, name=f"{tag}_d_norm")
    return dx_in, d_w_up, d_w_down, d_g


def kernel(x, norm_mix_g, norm_ffn_g, fox_w_in, fox_b_f, fox_w_out, kv_norm_g, mla_w_kv_a, mla_kv_a_norm_g, mla_w_kv_b, mla_w_q_a, mla_q_a_norm_g, mla_w_q_b, mla_w_out, ffn_w_up, ffn_w_down, final_norm_g, loss_target, m_norm_mix_g, m_norm_ffn_g, m_fox_w_in, m_fox_b_f, m_fox_w_out, m_kv_norm_g, m_mla_w_kv_a, m_mla_kv_a_norm_g, m_mla_w_kv_b, m_mla_w_q_a, m_mla_q_a_norm_g, m_mla_w_q_b, m_mla_w_out, m_ffn_w_up, m_ffn_w_down, m_final_norm_g, v_norm_mix_g, v_norm_ffn_g, v_fox_w_in, v_fox_b_f, v_fox_w_out, v_kv_norm_g, v_mla_w_kv_a, v_mla_kv_a_norm_g, v_mla_w_kv_b, v_mla_w_q_a, v_mla_q_a_norm_g, v_mla_w_q_b, v_mla_w_out, v_ffn_w_up, v_ffn_w_down, v_final_norm_g):
    args = dict(locals())
    w_in = {n: args[n] for n in _WEIGHTS}
    m_in = {n: args["m_" + n] for n in _WEIGHTS}
    v_in = {n: args["v_" + n] for n in _WEIGHTS}

    xs = x[0]
    seq, d_model = xs.shape
    tgt = loss_target[0]
    fox_h, mla_h, nope = FOX_HEADS, MLA_HEADS, QK_NOPE_DIM
    kv_rank = mla_kv_a_norm_g.shape[0]
    rope = mla_w_kv_a.shape[1] - kv_rank
    half = rope // 2
    q_rank = mla_q_a_norm_g.shape[1]
    v_dim = mla_w_kv_b.shape[1] * N_CHIPS // mla_h - nope
    fox_w = fox_w_out.shape[1] * N_CHIPS
    fox_dh = fox_w // fox_h

    big_names = [n for n, _ in _BIG]
    shard_shapes = [w_in[n].shape for n in big_names]
    rows = _packed_rows(shard_shapes)
    my_shard = _pack([w_in[n] for n in big_names], rows, BF16)
    others = _all_gather_shards(my_shard, name="gather_weights")
    by_relation = jnp.concatenate([my_shard[None], others], axis=0)
    p_chip = 2 * lax.axis_index("x") + lax.axis_index("y")
    full = {}
    for q in range(N_CHIPS):
        shard_q = lax.dynamic_index_in_dim(by_relation, p_chip ^ q, axis=0, keepdims=False)
        for (n, ax), piece in zip(_BIG, _unpack(shard_q, shard_shapes)):
            full.setdefault(n, []).append(piece)
    full = {n: jnp.concatenate(full[n], axis=ax) for n, ax in _BIG}

    fox_scale = fox_dh ** -0.5
    fox_wd = _round_up(fox_dh + 9, LANE_TILE)
    fox_vwd = _round_up(fox_dh + 4, LANE_TILE)
    w_fox_in = full["fox_w_in"][0]
    w_fq = _pad_heads(w_fox_in[:, :fox_w] * fox_scale, fox_h, fox_wd, 1)
    w_fk = _pad_heads(w_fox_in[:, fox_w:2 * fox_w], fox_h, fox_wd, 1)
    w_fv = _pad_heads(w_fox_in[:, 2 * fox_w:3 * fox_w], fox_h, fox_vwd, 1)
    w_gate = w_fox_in[:, 3 * fox_w:]
    w_fox_out = _pad_heads(full["fox_w_out"][0], fox_h, fox_vwd, 0)
    n_cx = _round_up(3 * fox_h + 1, LANE_TILE)
    c_piece = lambda i: (lambda hh: 3 * hh + i)
    one_col = 3 * fox_h
    e_fq = _placement(n_cx, fox_h, fox_wd, [(c_piece(i), fox_dh + i, 1.0) for i in range(3)]
                      + [(one_col, fox_dh + 3 + i, 1.0) for i in range(3)])
    e_fk = _placement(n_cx, fox_h, fox_wd, [(one_col, fox_dh + i, 1.0) for i in range(3)]
                      + [(c_piece(i), fox_dh + 3 + i, -1.0) for i in range(3)]
                      + [(one_col, fox_dh + 6 + i, 1.0) for i in range(3)])
    e_fv = _placement(n_cx, fox_h, fox_vwd, [(one_col, fox_dh + i, -1.0) for i in range(3)]
                      + [(one_col, fox_dh + 3, 1.0)])

    mla_scale = (nope + rope) ** -0.5
    mla_dk = nope + rope
    mla_wd = _round_up(mla_dk + 3, LANE_TILE)
    mla_vwd = _round_up(v_dim + 4, LANE_TILE)
    w_kv_a = full["mla_w_kv_a"]
    w_kv_b3 = full["mla_w_kv_b"].reshape(kv_rank, mla_h, nope + v_dim)
    w_kn = _pad_heads(w_kv_b3[:, :, :nope].reshape(kv_rank, -1), mla_h, mla_wd, 1)
    w_mv = _pad_heads(w_kv_b3[:, :, nope:].reshape(kv_rank, -1), mla_h, mla_vwd, 1)
    w_q_a = full["mla_w_q_a"][0]
    w_q_b3 = full["mla_w_q_b"][0].reshape(q_rank, mla_h, nope + rope)
    w_qa_ = _pad_heads(w_q_b3.reshape(q_rank, -1), mla_h, mla_wd, 1)
    w_qb_ = _pad_heads(jnp.concatenate([jnp.zeros_like(w_q_b3[:, :, :nope]), -w_q_b3[:, :, nope + half:],
                                        w_q_b3[:, :, nope:nope + half]], axis=-1).reshape(q_rank, -1),
                       mla_h, mla_wd, 1)
    w_mla_out = _pad_heads(full["mla_w_out"][0], mla_h, mla_vwd, 0)
    w_up, w_down = full["ffn_w_up"], full["ffn_w_down"]
    n_kx = _round_up(rope + 1, LANE_TILE)
    e_mk = _placement(n_kx, mla_h, mla_wd, [(j, nope + j, 1.0) for j in range(rope)]
                      + [(rope, mla_dk + i, 1.0) for i in range(3)])
    e_mv = _placement(n_kx, mla_h, mla_vwd, [(rope, v_dim + i, -1.0) for i in range(3)] + [(rope, v_dim + 3, 1.0)])

    inv = 1.0 / (ROPE_BASE ** (jnp.arange(0, rope, 2, dtype=F32) / rope))
    ang = jnp.arange(seq, dtype=F32)[:, None] * inv[None, :]
    cos, sin = jnp.cos(ang), jnp.sin(ang)
    pad_t = jnp.zeros((seq, mla_wd - mla_dk), F32)
    cos_t = jnp.concatenate([jnp.ones((seq, nope), F32), cos, cos, pad_t], axis=1)
    sin_t = jnp.concatenate([jnp.zeros((seq, nope), F32), sin, sin, pad_t], axis=1)

    (h0,) = _rms_fwd(xs, norm_mix_g[0:1], name="l0_norm_mix")
    gate = _matmul(h0, w_gate, mode="nn", out_dtypes=(F32,), name="fox_gate")
    z = gate + fox_b_f[0][None, :]
    cum = jnp.cumsum(jax.nn.log_sigmoid(z), axis=0)
    cx = jnp.concatenate([_split3(cum).reshape(seq, 3 * fox_h), jnp.ones((seq, 1), BF16),
                          jnp.zeros((seq, n_cx - 3 * fox_h - 1), BF16)], axis=1)
    fqa = _matmul(h0, w_fq, mode="nn", out_dtypes=(BF16,), placed=(cx, e_fq), name="fox_q")
    fka = _matmul(h0, w_fk, mode="nn", out_dtypes=(BF16,), placed=(cx, e_fk), name="fox_k")
    fva = _matmul(h0, w_fv, mode="nn", out_dtypes=(BF16,), placed=(cx, e_fv), name="fox_v")
    foa, fqb = _flash_fwd(fqa, fka, fva, fox_h, fox_dh + 3, fox_dh + 6, name="fox_attn")
    add_res = lambda acc, res: (acc + res,)
    x1 = _matmul(foa, w_fox_out, mode="nn", out_dtypes=(F32,), epilogue=add_res, extras=(xs,), name="fox_out")
    (h1,) = _rms_fwd(x1, norm_ffn_g[0:1], name="l0_norm_ffn")
    x2, r0, a0 = _ffn_fwd(x1, h1, w_up[0], w_down[0], "ffn0")

    src, h2 = _rms_fwd(x2, jnp.stack([kv_norm_g, norm_mix_g[1]]), name="l1_norm_kv_mix")
    kv_a = _matmul(src, w_kv_a, mode="nn", out_dtypes=(F32,), name="mla_kv_a")
    (c_kv,) = _rms_fwd(kv_a, mla_kv_a_norm_g[None, :], name="mla_norm_kv_a")
    kr1, kr2 = _rope(kv_a[None, :, kv_rank:kv_rank + half], kv_a[None, :, kv_rank + half:], cos, sin, 1.0,
                     name="mla_rope_k")
    krx = jnp.concatenate([kr1.astype(BF16), kr2.astype(BF16), jnp.ones((seq, 1), BF16),
                           jnp.zeros((seq, n_kx - rope - 1), BF16)], axis=1)
    mka = _matmul(c_kv, w_kn, mode="nn", out_dtypes=(BF16,), placed=(krx, e_mk), name="mla_k")
    mva = _matmul(c_kv, w_mv, mode="nn", out_dtypes=(BF16,), placed=(krx, e_mv), name="mla_v")
    cq_pre = _matmul(h2, w_q_a, mode="nn", out_dtypes=(F32,), name="mla_q_a")
    (c_q,) = _rms_fwd(cq_pre, mla_q_a_norm_g, name="mla_norm_q_a")
    q_a_part = _matmul(c_q, w_qa_, mode="nn", out_dtypes=(F32,), name="mla_q_b_cos")
    q_b_part = _matmul(c_q, w_qb_, mode="nn", out_dtypes=(F32,), name="mla_q_b_sin")
    mqa = _rope_mix(q_a_part, q_b_part, cos_t, sin_t, mla_scale, mla_h, name="mla_rope_q")
    moa, mqb = _flash_fwd(mqa, mka, mva, mla_h, v_dim + 3, mla_dk, name="mla_attn")
    x3 = _matmul(moa, w_mla_out, mode="nn", out_dtypes=(F32,), epilogue=add_res, extras=(x2,), name="mla_out")
    (h3,) = _rms_fwd(x3, norm_ffn_g[1:2], name="l1_norm_ffn")
    x4, r1, a1 = _ffn_fwd(x3, h3, w_up[1], w_down[1], "ffn1")

    loss_tile, dx4, d_final_g = _loss_head(x4, final_norm_g[None, :], tgt, name="loss_head")
    loss = lax.psum(loss_tile[0, 0], ("x", "y", "c"))

    gw = {}
    dx3, d_up1, d_down1, d_nf1 = _ffn_bwd(dx4, x3, h3, r1, a1, norm_ffn_g[1:2], w_up[1], w_down[1], "ffn1")

    d_mo = _matmul(dx3, w_mla_out, mode="nt", out_dtypes=(BF16,), name="mla_d_ctx")
    gw["mla_w_out"] = _unpad_heads(_matmul(moa, dx3, mode="tn", out_dtypes=(F32,), name="mla_d_w_out"),
                                   mla_h, v_dim, 0)[None]
    d_moa = _delta_place(d_mo, moa, mla_h, v_dim, name="mla_attn_delta")
    d_mqa = _flash_bwd_dq(mqb, mka, mva, d_moa, mla_h, name="mla_attn_dq")
    d_mka, d_mva = _flash_bwd_dkv(mqb, mka, mva, d_moa, mla_h, name="mla_attn_dkv")
    d_qa_part, d_qb_part = _rope_unmix(d_mqa, cos_t, sin_t, mla_scale, mla_h, name="mla_rope_dq")
    d_w_qa_ = _unpad_heads(_matmul(c_q, d_qa_part, mode="tn", out_dtypes=(F32,), name="mla_d_w_q_b_cos"),
                           mla_h, mla_dk, 1).reshape(q_rank, mla_h, mla_dk)
    d_w_qb_ = _unpad_heads(_matmul(c_q, d_qb_part, mode="tn", out_dtypes=(F32,), name="mla_d_w_q_b_sin"),
                           mla_h, mla_dk, 1).reshape(q_rank, mla_h, mla_dk)
    gw["mla_w_q_b"] = jnp.concatenate(
        [d_w_qa_[:, :, :nope], d_w_qa_[:, :, nope:nope + half] + d_w_qb_[:, :, nope + half:],
         d_w_qa_[:, :, nope + half:] - d_w_qb_[:, :, nope:nope + half]], axis=-1).reshape(1, q_rank, mla_h * mla_dk)
    d_c_q_sin = _matmul(d_qb_part, w_qb_, mode="nt", out_dtypes=(F32,), name="mla_d_c_q_sin")
    d_c_q = _matmul(d_qa_part, w_qa_, mode="nt", out_dtypes=(F32,), epilogue=add_res, extras=(d_c_q_sin,),
                    name="mla_d_c_q")
    d_cq_pre, (d_q_a_g,) = _rms_bwd(cq_pre, [(mla_q_a_norm_g, d_c_q)], None, name="mla_d_norm_q_a")
    gw["mla_w_q_a"] = _matmul(h2, d_cq_pre, mode="tn", out_dtypes=(F32,), name="mla_d_w_q_a")[None]
    d_h2 = _matmul(d_cq_pre, w_q_a, mode="nt", out_dtypes=(F32,), name="mla_d_h")

    d_w_kn = _unpad_heads(_matmul(c_kv, d_mka, mode="tn", out_dtypes=(F32,), name="mla_d_w_k"), mla_h, nope, 1)
    d_w_mv = _unpad_heads(_matmul(c_kv, d_mva, mode="tn", out_dtypes=(F32,), name="mla_d_w_v"), mla_h, v_dim, 1)
    gw["mla_w_kv_b"] = jnp.concatenate([d_w_kn.reshape(kv_rank, mla_h, nope), d_w_mv.reshape(kv_rank, mla_h, v_dim)],
                                       axis=-1).reshape(kv_rank, mla_h * (nope + v_dim))
    d_c_kv_v = _matmul(d_mva, w_mv, mode="nt", out_dtypes=(F32,), name="mla_d_c_kv_v")
    d_c_kv = _matmul(d_mka, w_kn, mode="nt", out_dtypes=(F32,), epilogue=add_res, extras=(d_c_kv_v,),
                     name="mla_d_c_kv")
    d_ckv_pre, (d_kv_a_g,) = _rms_bwd(kv_a, [(mla_kv_a_norm_g[None, :], d_c_kv)], None, name="mla_d_norm_kv_a")
    d_mk_rope = jnp.transpose(d_mka.reshape(seq, mla_h, mla_wd)[:, :, nope:mla_dk], (1, 0, 2))
    d_kr1, d_kr2 = _rope(d_mk_rope[:, :, :half], d_mk_rope[:, :, half:], cos, sin, -1.0, name="mla_rope_dk")
    d_kv_a = jnp.concatenate([d_ckv_pre, d_kr1, d_kr2], axis=1)
    gw["mla_w_kv_a"] = _matmul(src, d_kv_a, mode="tn", out_dtypes=(F32,), name="mla_d_w_kv_a")
    d_src = _matmul(d_kv_a, w_kv_a, mode="nt", out_dtypes=(F32,), name="mla_d_src")
    dx2, (d_kv_g, d_nm1) = _rms_bwd(x2, [(kv_norm_g[None, :], d_src), (norm_mix_g[1:2], d_h2)], dx3,
                                    name="l1_d_norm_kv_mix")

    dx1, d_up0, d_down0, d_nf0 = _ffn_bwd(dx2, x1, h1, r0, a0, norm_ffn_g[0:1], w_up[0], w_down[0], "ffn0")
    gw["ffn_w_up"] = jnp.stack([d_up0, d_up1])
    gw["ffn_w_down"] = jnp.stack([d_down0, d_down1])

    d_fo = _matmul(dx1, w_fox_out, mode="nt", out_dtypes=(BF16,), name="fox_d_ctx")
    gw["fox_w_out"] = _unpad_heads(_matmul(foa, dx1, mode="tn", out_dtypes=(F32,), name="fox_d_w_out"),
                                   fox_h, fox_dh, 0)[None]
    d_foa = _delta_place(d_fo, foa, fox_h, fox_dh, name="fox_attn_delta")
    d_fqa = _flash_bwd_dq(fqb, fka, fva, d_foa, fox_h, name="fox_attn_dq")
    d_fka, d_fva = _flash_bwd_dkv(fqb, fka, fva, d_foa, fox_h, name="fox_attn_dkv")
    d_cum = (d_fqa.reshape(seq, fox_h, fox_wd)[:, :, fox_dh]
             - d_fka.reshape(seq, fox_h, fox_wd)[:, :, fox_dh + 3])
    d_z = jnp.flip(jnp.cumsum(jnp.flip(d_cum, 0), axis=0), 0) * jax.nn.sigmoid(-z)
    d_b_f = jnp.sum(d_z, axis=0)
    d_w_in = [_unpad_heads(_matmul(h0, g, mode="tn", out_dtypes=(F32,), name=f"fox_d_w_{tag}"), fox_h, fox_dh, 1)
              for tag, g in (("q", d_fqa), ("k", d_fka), ("v", d_fva))]
    d_w_gate = _matmul(h0, d_z, mode="tn", out_dtypes=(F32,), name="fox_d_w_gate")
    gw["fox_w_in"] = jnp.concatenate([d_w_in[0] * fox_scale, d_w_in[1], d_w_in[2], d_w_gate], axis=1)[None]
    d_h0 = _matmul(d_z, w_gate, mode="nt", out_dtypes=(F32,), name="fox_d_h_gate")
    for tag, g, w in (("q", d_fqa, w_fq), ("k", d_fka, w_fk), ("v", d_fva, w_fv)):
        d_h0 = _matmul(g, w, mode="nt", out_dtypes=(F32,), epilogue=add_res, extras=(d_h0,), name=f"fox_d_h_{tag}")
    grad_x, (d_nm0,) = _rms_bwd(xs, [(norm_mix_g[0:1], d_h0)], dx1, name="l0_d_norm_mix")

    c_idx = lax.axis_index("c").astype(jnp.int32).reshape(1)
    per_chip = []
    for q in range(N_CHIPS):
        pieces = [jnp.split(gw[n], N_CHIPS, axis=ax)[q] for n, ax in _BIG]
        per_chip.append(_pack(pieces, rows, F32))
    g4 = jnp.stack(per_chip)
    a4 = _sibling_swap_halves(g4, name="grads_to_sibling")
    s4 = _chip_sum(g4, a4, c_idx, name="grads_chip_sum")
    b3 = _chip_exchange(s4, name="grads_between_chips")
    t_mine = _sum_chips(s4, b3, p_chip.astype(jnp.int32).reshape(1), name="grads_sum_chips")
    t_theirs = _sibling_swap(t_mine, name="grads_join_halves")
    is_south = lax.axis_index("c") == 0
    g_big = jnp.concatenate([jnp.where(is_south, t_mine, t_theirs), jnp.where(is_south, t_theirs, t_mine)],
                            axis=0)

    small_local = {"norm_mix_g": jnp.concatenate([d_nm0, d_nm1], axis=0),
                   "norm_ffn_g": jnp.concatenate([d_nf0, d_nf1], axis=0),
                   "fox_b_f": d_b_f[None, :], "kv_norm_g": d_kv_g[0], "mla_kv_a_norm_g": d_kv_a_g[0],
                   "mla_q_a_norm_g": d_q_a_g, "final_norm_g": d_final_g[0]}
    small_shapes = [w_in[n].shape for n in _SMALL]
    small_rows = sum(_part_rows(s, SMALL_PART_ROWS) for s in small_shapes)
    pack_small = lambda arrs: _pack(arrs, small_rows, F32, SMALL_PART_ROWS)
    g_small = _all_reduce_small(pack_small([small_local[n] for n in _SMALL]), name="grads_small")

    d_big, nm_big, nv_big = _adamw(_pack([w_in[n] for n in big_names], rows, F32), g_big,
                                   _pack([m_in[n] for n in big_names], rows, F32),
                                   _pack([v_in[n] for n in big_names], rows, F32), name="adamw_big")
    d_sm, nm_sm, nv_sm = _adamw(pack_small([w_in[n] for n in _SMALL]), g_small,
                                pack_small([m_in[n] for n in _SMALL]),
                                pack_small([v_in[n] for n in _SMALL]), name="adamw_small")

    def spread(big, small):
        out = dict(zip(big_names, _unpack(big, shard_shapes)))
        out.update(zip(_SMALL, _unpack(small, small_shapes, SMALL_PART_ROWS)))
        return [out[n] for n in _WEIGHTS]

    return (loss, grad_x[None], *spread(g_big, g_small), *spread(d_big, d_sm), *spread(nm_big, nm_sm),
            *spread(nv_big, nv_sm))
```

```python
import math

import numpy as np
import jax
import jax.numpy as jnp
from jax import lax
from jax.experimental import pallas as pl
from jax.experimental.pallas import tpu as pltpu

F32 = jnp.float32
BF16 = jnp.bfloat16

FOX_HEADS = 16
MLA_HEADS = 8
QK_NOPE_DIM = 128
ROPE_BASE = 10000.0
EPS = 1e-6

ADAM_LR = 0.001
ADAM_B1 = 0.9
ADAM_B2 = 0.999
ADAM_EPS = 1e-08
ADAM_WD = 0.01
ADAM_STEP = 10

N_CHIPS = 4
N_DEV = 8
PACK_LANES = 1024
PACK_PART_ROWS = 16
SMALL_PART_ROWS = 8
PACK_ROWS_MULT = 1024
VMEM_LIMIT_BYTES = 48 * 1024 * 1024
LANE_TILE = 128
MATMUL_BLOCK = 1024
ATTN_BLOCK_Q = 1024
ATTN_BLOCK_K = 1024
ATTN_SUB_ROWS = 256
FOX_FWD_SUB_ROWS = (1024, 512)
MLA_FWD_SUB_ROWS = (256, 256)
NEG_BIG = -1e30
MESH = pl.DeviceIdType.MESH


def _round_up(n, m):
    return -(-n // m) * m


def _blk(dim, pref, mult=128):
    if dim <= pref:
        return dim
    b = (pref // mult) * mult
    while b >= mult:
        if dim % b == 0:
            return b
        b -= mult
    return dim


def _params(sem=None):
    return pltpu.CompilerParams(dimension_semantics=sem, vmem_limit_bytes=VMEM_LIMIT_BYTES)


_DIMS = {"nn": (((1,), (0,)), ((), ())), "nt": (((1,), (1,)), ((), ())), "tn": (((0,), (0,)), ((), ()))}


def _matmul(a, b, *, mode, out_dtypes, name, epilogue=None, extras=(), placed=None):
    if mode == "tn":
        kdim, m = a.shape
    else:
        m, kdim = a.shape
    n = b.shape[0] if mode == "nt" else b.shape[1]
    bm, bn, bk = _blk(m, MATMUL_BLOCK), _blk(n, MATMUL_BLOCK), _blk(kdim, MATMUL_BLOCK)
    nk = kdim // bk
    n_extra, n_out = len(extras), len(out_dtypes)
    n_placed = 0 if placed is None else 2
    dims = _DIMS[mode]

    def body(a_ref, b_ref, *rest):
        placed_refs = rest[:n_placed]
        rest = rest[n_placed:]
        extra_refs = rest[:n_extra]
        out_refs = rest[n_extra:n_extra + n_out]

        def finish(acc):
            if n_placed:
                acc = acc + lax.dot_general(placed_refs[0][...], placed_refs[1][...], _DIMS["nn"],
                                            preferred_element_type=F32)
            res = (acc,) if epilogue is None else epilogue(acc, *[r[...] for r in extra_refs])
            for o_ref, r in zip(out_refs, res):
                o_ref[...] = r.astype(o_ref.dtype)

        part = lax.dot_general(a_ref[...].astype(BF16), b_ref[...].astype(BF16), dims, preferred_element_type=F32)
        if nk == 1:
            finish(part)
            return
        acc_ref = rest[n_extra + n_out]
        k = pl.program_id(2)

        @pl.when(k == 0)
        def _():
            acc_ref[...] = part

        @pl.when((k > 0) & (k < nk - 1))
        def _():
            acc_ref[...] += part

        @pl.when(k == nk - 1)
        def _():
            finish(acc_ref[...] + part)

    if mode == "tn":
        a_spec = pl.BlockSpec((bk, bm), lambda i, j, k: (k, i))
    else:
        a_spec = pl.BlockSpec((bm, bk), lambda i, j, k: (i, k))
    if mode == "nt":
        b_spec = pl.BlockSpec((bn, bk), lambda i, j, k: (j, k))
    else:
        b_spec = pl.BlockSpec((bk, bn), lambda i, j, k: (k, j))
    tile = pl.BlockSpec((bm, bn), lambda i, j, k: (i, j))
    placed_specs = []
    if n_placed:
        k2 = placed[0].shape[1]
        placed_specs = [pl.BlockSpec((bm, k2), lambda i, j, k: (i, 0)), pl.BlockSpec((k2, bn), lambda i, j, k: (0, j))]
    outs = pl.pallas_call(
        body, name=name,
        grid=(m // bm, n // bn, nk),
        in_specs=[a_spec, b_spec] + placed_specs + [tile] * n_extra,
        out_specs=[tile] * n_out,
        out_shape=[jax.ShapeDtypeStruct((m, n), dt) for dt in out_dtypes],
        scratch_shapes=[pltpu.VMEM((bm, bn), F32)] if nk > 1 else [],
        compiler_params=_params(("parallel", "parallel", "arbitrary")),
    )(a, b, *(placed or ()), *extras)
    return outs[0] if n_out == 1 else outs


def _rms_fwd(x, gains, name):
    s = x.shape[0]
    g, w = gains.shape
    bs = _blk(s, 512, 8)

    def body(x_ref, g_ref, *out_refs):
        xv = x_ref[...]
        y = xv * lax.rsqrt(jnp.mean(xv * xv, axis=-1, keepdims=True) + EPS)
        for i, o_ref in enumerate(out_refs):
            o_ref[...] = (y * g_ref[i:i + 1, :]).astype(o_ref.dtype)

    row = pl.BlockSpec((bs, w), lambda i: (i, 0))
    outs = pl.pallas_call(
        body, name=name, grid=(s // bs,),
        in_specs=[row, pl.BlockSpec((g, w), lambda i: (0, 0))],
        out_specs=[row] * g,
        out_shape=[jax.ShapeDtypeStruct((s, w), BF16)] * g,
        compiler_params=_params(("parallel",)),
    )(x, gains)
    return outs


def _rms_bwd(x, branches, resid, name):
    s = x.shape[0]
    w = branches[0][0].shape[1]
    nb = len(branches)
    bs = _blk(s, 512, 8)
    has_resid = resid is not None

    def body(x_ref, *rest):
        g_refs = rest[:nb]
        dy_refs = rest[nb:2 * nb]
        pos = 2 * nb
        r_ref = rest[pos] if has_resid else None
        pos += int(has_resid)
        dx_ref = rest[pos]
        dg_refs = rest[pos + 1:pos + 1 + nb]
        i = pl.program_id(0)

        @pl.when(i == 0)
        def _():
            for dg_ref in dg_refs:
                dg_ref[...] = jnp.zeros_like(dg_ref)

        xv = x_ref[...]
        rstd = lax.rsqrt(jnp.mean(xv * xv, axis=-1, keepdims=True) + EPS)
        xhat = xv * rstd
        dx = r_ref[...] if has_resid else jnp.zeros_like(xv)
        for g_ref, dy_ref, dg_ref in zip(g_refs, dy_refs, dg_refs):
            dy = dy_ref[...].astype(F32)
            dyg = dy * g_ref[...]
            dx = dx + rstd * (dyg - xhat * jnp.mean(dyg * xhat, axis=-1, keepdims=True))
            dg_ref[...] += jnp.sum(dy * xhat, axis=0, keepdims=True)
        dx_ref[...] = dx

    row = pl.BlockSpec((bs, w), lambda i: (i, 0))
    vec = pl.BlockSpec((1, w), lambda i: (0, 0))
    args = [x] + [g for g, _ in branches] + [dy for _, dy in branches] + ([resid] if has_resid else [])
    outs = pl.pallas_call(
        body, name=name, grid=(s // bs,),
        in_specs=[row] + [vec] * nb + [row] * nb + ([row] if has_resid else []),
        out_specs=[row] + [vec] * nb,
        out_shape=[jax.ShapeDtypeStruct((s, w), F32)] + [jax.ShapeDtypeStruct((1, w), F32)] * nb,
        compiler_params=_params(("arbitrary",)),
    )(*args)
    return outs[0], list(outs[1:])


def _loss_head(x, g, target, name):
    s, w = x.shape
    bs = _blk(s, 512, 8)

    def body(x_ref, g_ref, t_ref, loss_ref, dx_ref, dg_ref):
        i = pl.program_id(0)

        @pl.when(i == 0)
        def _():
            loss_ref[...] = jnp.zeros_like(loss_ref)
            dg_ref[...] = jnp.zeros_like(dg_ref)

        xv = x_ref[...]
        gv = g_ref[...]
        rstd = lax.rsqrt(jnp.mean(xv * xv, axis=-1, keepdims=True) + EPS)
        xhat = xv * rstd
        err = xhat * gv - t_ref[...]
        loss_ref[...] += 0.5 * jnp.sum(jnp.mean(err * err, axis=-1, keepdims=True))
        dy = err * (1.0 / w)
        dyg = dy * gv
        dx_ref[...] = rstd * (dyg - xhat * jnp.mean(dyg * xhat, axis=-1, keepdims=True))
        dg_ref[...] += jnp.sum(dy * xhat, axis=0, keepdims=True)

    row = pl.BlockSpec((bs, w), lambda i: (i, 0))
    vec = pl.BlockSpec((1, w), lambda i: (0, 0))
    return pl.pallas_call(
        body, name=name, grid=(s // bs,),
        in_specs=[row, vec, row],
        out_specs=[pl.BlockSpec((8, 128), lambda i: (0, 0)), row, vec],
        out_shape=[jax.ShapeDtypeStruct((8, 128), F32), jax.ShapeDtypeStruct((s, w), F32),
                   jax.ShapeDtypeStruct((1, w), F32)],
        compiler_params=_params(("arbitrary",)),
    )(x, g, target)


def _rope(a, b, cos, sin, sign, name):
    g, s, w = a.shape
    bs = _blk(s, 1024, 8)

    def body(a_ref, b_ref, c_ref, s_ref, o1_ref, o2_ref):
        av = jnp.sum(a_ref[...].astype(F32), axis=0)
        bv = jnp.sum(b_ref[...].astype(F32), axis=0)
        cv, sv = c_ref[...], s_ref[...] * sign
        o1_ref[...] = av * cv - bv * sv
        o2_ref[...] = bv * cv + av * sv

    grp = pl.BlockSpec((g, bs, w), lambda i: (0, i, 0))
    row = pl.BlockSpec((bs, w), lambda i: (i, 0))
    return pl.pallas_call(
        body, name=name, grid=(s // bs,),
        in_specs=[grp, grp, row, row], out_specs=[row, row],
        out_shape=[jax.ShapeDtypeStruct((s, w), F32)] * 2,
        compiler_params=_params(("parallel",)),
    )(a, b, cos, sin)


def _causal_table(s, bq, bk, q_major):
    nq, nk = s // bq, s // bk
    rows = []
    if q_major:
        for qi in range(nq):
            kmax = (qi * bq + bq - 1) // bk
            for ki in range(kmax + 1):
                rows.append((qi, ki, int(ki * bk + bk - 1 > qi * bq), int(ki == 0), int(ki == kmax)))
    else:
        for ki in range(nk):
            qmin = (ki * bk) // bq
            for qi in range(qmin, nq):
                rows.append((qi, ki, int(ki * bk + bk - 1 > qi * bq), int(qi == qmin), int(qi == nq - 1)))
    return jnp.asarray(np.array(rows, np.int32).T)


def _causal_keep(q0, k0, nq, nk, transposed):
    if transposed:
        kpos = k0 + lax.broadcasted_iota(jnp.int32, (nk, nq), 0)
        qpos = q0 + lax.broadcasted_iota(jnp.int32, (nk, nq), 1)
    else:
        qpos = q0 + lax.broadcasted_iota(jnp.int32, (nq, nk), 0)
        kpos = k0 + lax.broadcasted_iota(jnp.int32, (nq, nk), 1)
    return kpos <= qpos


def _sub_tiles(n_rows, n_cols, masked, square, rows_are_keys, sub_rows):
    sub = min(sub_rows, n_rows)
    out = []
    for r0 in range(0, n_rows, sub):
        if masked and square:
            c0, nc = (r0, n_cols - r0) if rows_are_keys else (0, r0 + sub)
        else:
            c0, nc = 0, n_cols
        out.append((r0, sub, c0, nc))
    return out


_NT = (((1,), (1,)), ((), ()))
_NN = (((1,), (0,)), ((), ()))


def _attn_specs(bq, bk):
    qspec = lambda d: pl.BlockSpec((bq, d), lambda hh, t, tb: (tb[0, t], hh))
    kspec = lambda d: pl.BlockSpec((bk, d), lambda hh, t, tb: (tb[1, t], hh))
    return qspec, kspec


def _split3_cols(x):
    hi = x.astype(BF16).astype(F32)
    rest = x - hi
    mid = rest.astype(BF16).astype(F32)
    lo = (rest - mid).astype(BF16).astype(F32)
    return hi, mid, lo


def _place3(base, col, pieces, sign):
    lane = lax.broadcasted_iota(jnp.int32, base.shape, 1)
    out = base.astype(F32)
    for i, piece in enumerate(pieces):
        out = jnp.where(lane == col + i, sign * piece, out)
    return out.astype(BF16)


def _flash_fwd(qa, ka, va, heads, l_col, lse_col, sub_rows, name):
    s = qa.shape[0]
    da, dv = qa.shape[1] // heads, va.shape[1] // heads
    h = heads
    bq, bk = _blk(s, ATTN_BLOCK_Q), _blk(s, ATTN_BLOCK_K)
    tab = _causal_table(s, bq, bk, True)

    def body(tab_ref, q_ref, k_ref, v_ref, o_ref, qb_ref, m_sc, acc_sc):
        t = pl.program_id(1)
        qi, ki = tab_ref[0, t], tab_ref[1, t]

        @pl.when(tab_ref[3, t] == 1)
        def _():
            m_sc[...] = jnp.full_like(m_sc, NEG_BIG)
            acc_sc[...] = jnp.zeros_like(acc_sc)

        def step(masked):
            for r0, nr, c0, nc in _sub_tiles(bq, bk, masked, bq == bk, False, sub_rows[int(masked)]):
                sc = lax.dot_general(q_ref[r0:r0 + nr, :], k_ref[c0:c0 + nc, :], _NT, preferred_element_type=F32)
                if masked:
                    sc = jnp.where(_causal_keep(qi * bq + r0, ki * bk + c0, nr, nc, False), sc, NEG_BIG)
                m_prev = m_sc[r0:r0 + nr, :]
                m_new = jnp.maximum(m_prev, jnp.max(sc, axis=-1, keepdims=True))
                p = jnp.exp(sc - m_new).astype(BF16)
                acc_sc[r0:r0 + nr, :] = jnp.exp(m_prev - m_new) * acc_sc[r0:r0 + nr, :] + lax.dot_general(
                    p, v_ref[c0:c0 + nc, :], _NN, preferred_element_type=F32)
                m_sc[r0:r0 + nr, :] = m_new

        @pl.when(tab_ref[2, t] == 1)
        def _():
            step(True)

        @pl.when(tab_ref[2, t] == 0)
        def _():
            step(False)

        @pl.when(tab_ref[4, t] == 1)
        def _():
            acc = acc_sc[...]
            lane = lax.broadcasted_iota(jnp.int32, acc.shape, 1)
            l = jnp.sum(jnp.where(lane == l_col, acc, 0.0), axis=-1, keepdims=True)
            o_ref[...] = (acc / l).astype(o_ref.dtype)
            lse = m_sc[...] + jnp.log(l)
            qb_ref[...] = _place3(q_ref[...], lse_col, _split3_cols(lse), -1.0)

    qspec, kspec = _attn_specs(bq, bk)
    return pl.pallas_call(
        body, name=name,
        grid_spec=pltpu.PrefetchScalarGridSpec(
            num_scalar_prefetch=1, grid=(h, tab.shape[1]),
            in_specs=[qspec(da), kspec(da), kspec(dv)],
            out_specs=[qspec(dv), qspec(da)],
            scratch_shapes=[pltpu.VMEM((bq, 1), F32), pltpu.VMEM((bq, dv), F32)]),
        out_shape=[jax.ShapeDtypeStruct((s, h * dv), BF16), jax.ShapeDtypeStruct((s, h * da), BF16)],
        compiler_params=_params(("parallel", "arbitrary")),
    )(tab, qa, ka, va)


def _delta_place(do, o, heads, delta_col, name):
    s = o.shape[0]
    dv = o.shape[1] // heads
    bs = _blk(s, 1024, 8)

    def body(do_ref, o_ref, out_ref):
        dov = do_ref[...]
        delta = jnp.sum(dov.astype(F32) * o_ref[...].astype(F32), axis=-1, keepdims=True)
        out_ref[...] = _place3(dov, delta_col, _split3_cols(delta), 1.0)

    blk = pl.BlockSpec((bs, dv), lambda i, hh: (i, hh))
    return pl.pallas_call(
        body, name=name, grid=(s // bs, heads), in_specs=[blk, blk], out_specs=blk,
        out_shape=jax.ShapeDtypeStruct(do.shape, BF16),
        compiler_params=_params(("parallel", "parallel")),
    )(do, o)


_TN =(((0,), (0,)), ((), ()))


def _flash_bwd(qa, ka, va, doa, heads, name):
    s = qa.shape[0]
    h, da, dv = heads, qa.shape[1] // heads, va.shape[1] // heads
    bq, bk = _blk(s, ATTN_BLOCK_Q), _blk(s, ATTN_BLOCK_K)
    tab = _causal_table(s, bq, bk, False)
    n_tiles = tab.shape[1]

    def body(tab_ref, q_ref, k_ref, v_ref, do_ref, dq_ref, dk_ref, dv_ref, dk_sc, dv_sc):
        t = pl.program_id(1)
        qi, ki = tab_ref[0, t], tab_ref[1, t]

        @pl.when(t == 0)
        def _():
            dq_ref[...] = jnp.zeros_like(dq_ref)

        @pl.when(tab_ref[3, t] == 1)
        def _():
            dk_sc[...] = jnp.zeros_like(dk_sc)
            dv_sc[...] = jnp.zeros_like(dv_sc)

        def step(masked):
            for r0, nr, c0, nc in _sub_tiles(bk, bq, masked, bq == bk, True, ATTN_SUB_ROWS):
                qv, dov, kv = q_ref[c0:c0 + nc, :], do_ref[c0:c0 + nc, :], k_ref[r0:r0 + nr, :]
                st = lax.dot_general(kv, qv, _NT, preferred_element_type=F32)
                if masked:
                    st = jnp.where(_causal_keep(qi * bq + c0, ki * bk + r0, nc, nr, True), st, NEG_BIG)
                pt = jnp.exp(st)
                dv_sc[r0:r0 + nr, :] += lax.dot_general(pt.astype(BF16), dov, _NN, preferred_element_type=F32)
                dpt = lax.dot_general(v_ref[r0:r0 + nr, :], dov, _NT, preferred_element_type=F32)
                dst = (pt * dpt).astype(BF16)
                dk_sc[r0:r0 + nr, :] += lax.dot_general(dst, qv, _NN, preferred_element_type=F32)
                q_rows = pl.ds(pl.multiple_of(qi * bq + c0, ATTN_SUB_ROWS), nc)
                dq_ref[q_rows, :] += lax.dot_general(dst, kv, _TN, preferred_element_type=F32)

        @pl.when(tab_ref[2, t] == 1)
        def _():
            step(True)

        @pl.when(tab_ref[2, t] == 0)
        def _():
            step(False)

        @pl.when(tab_ref[4, t] == 1)
        def _():
            dk_ref[...] = dk_sc[...]
            dv_ref[...] = dv_sc[...]

    qspec, kspec = _attn_specs(bq, bk)
    return pl.pallas_call(
        body, name=name,
        grid_spec=pltpu.PrefetchScalarGridSpec(
            num_scalar_prefetch=1, grid=(h, n_tiles),
            in_specs=[qspec(da), kspec(da), kspec(dv), qspec(dv)],
            out_specs=[pl.BlockSpec((s, da), lambda hh, t, tb: (0, hh)), kspec(da), kspec(dv)],
            scratch_shapes=[pltpu.VMEM((bk, da), F32), pltpu.VMEM((bk, dv), F32)]),
        out_shape=[jax.ShapeDtypeStruct((s, h * da), F32), jax.ShapeDtypeStruct((s, h * da), F32),
                   jax.ShapeDtypeStruct((s, h * dv), F32)],
        compiler_params=_params(("parallel", "arbitrary")),
    )(tab, qa, ka, va, doa)


def _split3(x):
    hi = lax.reduce_precision(x, 8, 7)
    rest = x - hi
    mid = lax.reduce_precision(rest, 8, 7)
    lo = lax.reduce_precision(rest - mid, 8, 7)
    return jnp.stack([hi, mid, lo], axis=-1).astype(BF16)


def _pad_heads(w, heads, width, axis):
    shape = list(w.shape)
    d = shape[axis] // heads
    w = w.reshape(shape[:axis] + [heads, d] + shape[axis + 1:])
    pad = [(0, 0)] * w.ndim
    pad[axis + 1] = (0, width - d)
    return jnp.pad(w, pad).reshape(shape[:axis] + [heads * width] + shape[axis + 1:])


def _unpad_heads(w, heads, d, axis):
    shape = list(w.shape)
    width = shape[axis] // heads
    w = w.reshape(shape[:axis] + [heads, width] + shape[axis + 1:])
    w = lax.slice_in_dim(w, 0, d, axis=axis + 1)
    return w.reshape(shape[:axis] + [heads * d] + shape[axis + 1:])


def _placement(rows, heads, width, entries):
    e = np.zeros((rows, heads * width), np.float32)
    for row, col, val in entries:
        for hh in range(heads):
            e[row(hh) if callable(row) else row, hh * width + col] = val
    return jnp.asarray(e, BF16)


def _rope_mix(a, b, cos_t, sin_t, scale, heads, name):
    s = a.shape[0]
    d = a.shape[1] // heads
    bs = _blk(s, 1024, 8)

    def body(a_ref, b_ref, c_ref, s_ref, o_ref):
        o_ref[...] = ((a_ref[...] * c_ref[...] + b_ref[...] * s_ref[...]) * scale).astype(o_ref.dtype)

    blk = pl.BlockSpec((bs, d), lambda i, hh: (i, hh))
    tbl = pl.BlockSpec((bs, d), lambda i, hh: (i, 0))
    return pl.pallas_call(
        body, name=name, grid=(s // bs, heads), in_specs=[blk, blk, tbl, tbl], out_specs=blk,
        out_shape=jax.ShapeDtypeStruct(a.shape, BF16),
        compiler_params=_params(("parallel", "parallel")),
    )(a, b, cos_t, sin_t)


def _rope_unmix(g, cos_t, sin_t, scale, heads, name):
    s = g.shape[0]
    d = g.shape[1] // heads
    bs = _blk(s, 1024, 8)

    def body(g_ref, c_ref, s_ref, da_ref, db_ref):
        gv = g_ref[...] * scale
        da_ref[...] = (gv * c_ref[...]).astype(da_ref.dtype)
        db_ref[...] = (gv * s_ref[...]).astype(db_ref.dtype)

    blk = pl.BlockSpec((bs, d), lambda i, hh: (i, hh))
    tbl = pl.BlockSpec((bs, d), lambda i, hh: (i, 0))
    return pl.pallas_call(
        body, name=name, grid=(s // bs, heads), in_specs=[blk, tbl, tbl], out_specs=[blk, blk],
        out_shape=[jax.ShapeDtypeStruct(g.shape, BF16)] * 2,
        compiler_params=_params(("parallel", "parallel")),
    )(g, cos_t, sin_t)


def _adamw(w, g, m, v, name):
    r, wd = w.shape
    br = _blk(r, 512, 8)

    def body(w_ref, g_ref, m_ref, v_ref, d_ref, nm_ref, nv_ref):
        gv = g_ref[...]
        mn = ADAM_B1 * m_ref[...] + (1.0 - ADAM_B1) * gv
        vn = ADAM_B2 * v_ref[...] + (1.0 - ADAM_B2) * (gv * gv)
        m_hat = mn / (1.0 - ADAM_B1 ** ADAM_STEP)
        v_hat = vn / (1.0 - ADAM_B2 ** ADAM_STEP)
        d_ref[...] = -ADAM_LR * (m_hat / (jnp.sqrt(v_hat) + ADAM_EPS) + ADAM_WD * w_ref[...])
        nm_ref[...] = mn
        nv_ref[...] = vn

    row = pl.BlockSpec((br, wd), lambda i: (i, 0))
    return pl.pallas_call(
        body, name=name, grid=(r // br,), in_specs=[row] * 4, out_specs=[row] * 3,
        out_shape=[jax.ShapeDtypeStruct((r, wd), F32)] * 3,
        compiler_params=_params(("parallel",)),
    )(w, g, m, v)


_ANY = pl.BlockSpec(memory_space=pl.ANY)


def _place():
    x, y, c = lax.axis_index("x"), lax.axis_index("y"), lax.axis_index("c")
    chips = [(x, 1 - y), (1 - x, y), (1 - x, 1 - y)]
    return x, y, c, chips


def _all_gather_shards(shard, name):
    r, w = shard.shape
    hr = r // 2

    def body(x_ref, out_ref, send_sems, recv_sems):
        x, y, c, chips = _place()
        sibling = (x, y, 1 - c)

        def rows(j, half):
            return out_ref.at[j, pl.ds(pl.multiple_of(half * hr, 16), hr), :]

        def copy(sem, j, half, to, src=None):
            return pltpu.make_async_remote_copy(
                src_ref=rows(j, half) if src is None else src, dst_ref=rows(j, half),
                send_sem=send_sems.at[sem], recv_sem=recv_sems.at[sem], device_id=to, device_id_type=MESH)

        my_half = x_ref.at[pl.ds(pl.multiple_of(c * hr, 16), hr), :]
        first = [copy(j, j, c, (cx, cy, c), src=my_half) for j, (cx, cy) in enumerate(chips)]
        for cp in first:
            cp.start()
        passed = []
        for j in range(3):
            copy(j, j, c, (x, y, c)).wait_recv()
            fw = copy(3 + j, j, c, sibling)
            fw.start()
            passed.append(fw)
        for j in range(3):
            copy(3 + j, j, 1 - c, (x, y, c)).wait_recv()
        for cp in first + passed:
            cp.wait_send()

    return pl.pallas_call(
        body, name=name, in_specs=[_ANY], out_specs=_ANY,
        out_shape=jax.ShapeDtypeStruct((N_CHIPS - 1, r, w), shard.dtype),
        scratch_shapes=[pltpu.SemaphoreType.DMA((6,)), pltpu.SemaphoreType.DMA((6,))],
        compiler_params=pltpu.CompilerParams(vmem_limit_bytes=VMEM_LIMIT_BYTES),
    )(shard)


def _sibling_swap_halves(g, name):
    nq, r, w = g.shape
    hr = r // 2

    def body(g_ref, a_ref, send_sems, recv_sems):
        x, y, c, _ = _place()
        sibling = (x, y, 1 - c)
        cps = []
        for q in range(nq):
            cp = pltpu.make_async_remote_copy(
                src_ref=g_ref.at[q, pl.ds(pl.multiple_of((1 - c) * hr, 8), hr), :], dst_ref=a_ref.at[q],
                send_sem=send_sems.at[q], recv_sem=recv_sems.at[q], device_id=sibling, device_id_type=MESH)
            cp.start()
            cps.append(cp)
        for cp in cps:
            cp.wait()

    return pl.pallas_call(
        body, name=name, in_specs=[_ANY], out_specs=_ANY,
        out_shape=jax.ShapeDtypeStruct((nq, hr, w), g.dtype),
        scratch_shapes=[pltpu.SemaphoreType.DMA((nq,)), pltpu.SemaphoreType.DMA((nq,))],
        compiler_params=pltpu.CompilerParams(vmem_limit_bytes=VMEM_LIMIT_BYTES),
    )(g)


def _chip_sum(g, a, c_idx, name):
    nq, r, w = g.shape
    hr = r // 2
    br = _blk(hr, 512, 16)
    nb = hr // br

    def body(c_ref, g_ref, a_ref, o_ref):
        o_ref[...] = (g_ref[...] + a_ref[...]).astype(o_ref.dtype)

    return pl.pallas_call(
        body, name=name,
        grid_spec=pltpu.PrefetchScalarGridSpec(
            num_scalar_prefetch=1, grid=(nq, nb),
            in_specs=[pl.BlockSpec((None, br, w), lambda q, i, cr: (q, cr[0] * nb + i, 0)),
                      pl.BlockSpec((None, br, w), lambda q, i, cr: (q, i, 0))],
            out_specs=pl.BlockSpec((None, br, w), lambda q, i, cr: (q, i, 0))),
        out_shape=jax.ShapeDtypeStruct((nq, hr, w), BF16),
        compiler_params=_params(("parallel", "parallel")),
    )(c_idx, g, a)


def _chip_exchange(s4, name):
    nq, hr, w = s4.shape

    def body(s_ref, b_ref, send_sems, recv_sems):
        x, y, c, chips = _place()
        cps = []
        for j, (cx, cy) in enumerate(chips):
            cp = pltpu.make_async_remote_copy(
                src_ref=s_ref.at[2 * cx + cy], dst_ref=b_ref.at[j],
                send_sem=send_sems.at[j], recv_sem=recv_sems.at[j], device_id=(cx, cy, c), device_id_type=MESH)
            cp.start()
            cps.append(cp)
        for cp in cps:
            cp.wait()

    return pl.pallas_call(
        body, name=name, in_specs=[_ANY], out_specs=_ANY,
        out_shape=jax.ShapeDtypeStruct((nq - 1, hr, w), s4.dtype),
        scratch_shapes=[pltpu.SemaphoreType.DMA((3,)), pltpu.SemaphoreType.DMA((3,))],
        compiler_params=pltpu.CompilerParams(vmem_limit_bytes=VMEM_LIMIT_BYTES),
    )(s4)


def _sum_chips(s4, b3, p_idx, name):
    _, hr, w = s4.shape
    nb3 = b3.shape[0]
    br = _blk(hr, 512, 16)

    def body(p_ref, s_ref, b_ref, o_ref):
        acc = s_ref[...].astype(F32)
        for j in range(nb3):
            acc = acc + b_ref[j].astype(F32)
        o_ref[...] = acc

    return pl.pallas_call(
        body, name=name,
        grid_spec=pltpu.PrefetchScalarGridSpec(
            num_scalar_prefetch=1, grid=(hr // br,),
            in_specs=[pl.BlockSpec((None, br, w), lambda i, pr: (pr[0], i, 0)),
                      pl.BlockSpec((nb3, br, w), lambda i, pr: (0, i, 0))],
            out_specs=pl.BlockSpec((br, w), lambda i, pr: (i, 0))),
        out_shape=jax.ShapeDtypeStruct((hr, w), F32),
        compiler_params=_params(("parallel",)),
    )(p_idx, s4, b3)


def _sibling_swap(t, name):
    hr, w = t.shape

    def body(t_ref, o_ref, send_sem, recv_sem):
        x, y, c, _ = _place()
        cp = pltpu.make_async_remote_copy(src_ref=t_ref, dst_ref=o_ref, send_sem=send_sem, recv_sem=recv_sem,
                                          device_id=(x, y, 1 - c), device_id_type=MESH)
        cp.start()
        cp.wait()

    return pl.pallas_call(
        body, name=name, in_specs=[_ANY], out_specs=_ANY,
        out_shape=jax.ShapeDtypeStruct((hr, w), t.dtype),
        scratch_shapes=[pltpu.SemaphoreType.DMA, pltpu.SemaphoreType.DMA],
        compiler_params=pltpu.CompilerParams(vmem_limit_bytes=VMEM_LIMIT_BYTES),
    )(t)


def _all_reduce_small(v, name):
    r, w = v.shape

    def body(v_ref, o_ref, slots, send_sems, recv_sems):
        x, y, c, _ = _place()
        me = 4 * x + 2 * y + c
        slots[me] = v_ref[...]
        cps = []
        for k in range(1, N_DEV):
            fx, fy, fc = (k >> 2) & 1, (k >> 1) & 1, k & 1
            to = (x ^ fx, y ^ fy, c ^ fc)
            cp = pltpu.make_async_remote_copy(
                src_ref=v_ref, dst_ref=slots.at[me], send_sem=send_sems.at[k - 1], recv_sem=recv_sems.at[k - 1],
                device_id=to, device_id_type=MESH)
            cp.start()
            cps.append(cp)
        for k in range(1, N_DEV):
            fx, fy, fc = (k >> 2) & 1, (k >> 1) & 1, k & 1
            src_dev = 4 * (x ^ fx) + 2 * (y ^ fy) + (c ^ fc)
            pltpu.make_async_remote_copy(
                src_ref=v_ref, dst_ref=slots.at[src_dev], send_sem=send_sems.at[k - 1],
                recv_sem=recv_sems.at[k - 1], device_id=(x, y, c), device_id_type=MESH).wait_recv()
        for cp in cps:
            cp.wait_send()
        acc = slots[0]
        for d in range(1, N_DEV):
            acc = acc + slots[d]
        o_ref[...] = acc

    return pl.pallas_call(
        body, name=name,
        in_specs=[pl.BlockSpec(memory_space=pltpu.VMEM)], out_specs=pl.BlockSpec(memory_space=pltpu.VMEM),
        out_shape=jax.ShapeDtypeStruct((r, w), F32),
        scratch_shapes=[pltpu.VMEM((N_DEV, r, w), F32), pltpu.SemaphoreType.DMA((N_DEV - 1,)),
                        pltpu.SemaphoreType.DMA((N_DEV - 1,))],
        compiler_params=pltpu.CompilerParams(vmem_limit_bytes=VMEM_LIMIT_BYTES),
    )(v)


def _part_rows(shape, part_rows=PACK_PART_ROWS):
    assert shape[-1] <= PACK_LANES
    return _round_up(math.prod(shape[:-1]), part_rows)


def _packed_rows(shapes):
    return _round_up(sum(_part_rows(s) for s in shapes), PACK_ROWS_MULT)


def _pack(arrs, total_rows, dtype, part_rows=PACK_PART_ROWS):
    parts = []
    for a in arrs:
        a2 = a.reshape(-1, a.shape[-1]).astype(dtype)
        rows = _part_rows(a.shape, part_rows)
        parts.append(jnp.pad(a2, ((0, rows - a2.shape[0]), (0, PACK_LANES - a2.shape[1]))))
    used = sum(p.shape[0] for p in parts)
    if total_rows > used:
        parts.append(jnp.zeros((total_rows - used, PACK_LANES), dtype))
    return jnp.concatenate(parts, axis=0)


def _unpack(packed, shapes, part_rows=PACK_PART_ROWS):
    out, r0 = [], 0
    for s in shapes:
        out.append(packed[r0:r0 + math.prod(s[:-1]), :s[-1]].reshape(s))
        r0 += _part_rows(s, part_rows)
    return out


_BIG = (("fox_w_in", 2), ("fox_w_out", 1), ("mla_w_kv_a", 0), ("mla_w_kv_b", 1), ("mla_w_q_a", 1),
        ("mla_w_q_b", 2), ("mla_w_out", 1), ("ffn_w_up", 2), ("ffn_w_down", 1))
_SMALL = ("norm_mix_g", "norm_ffn_g", "fox_b_f", "kv_norm_g", "mla_kv_a_norm_g", "mla_q_a_norm_g", "final_norm_g")
_WEIGHTS = ("norm_mix_g", "norm_ffn_g", "fox_w_in", "fox_b_f", "fox_w_out", "kv_norm_g", "mla_w_kv_a",
            "mla_kv_a_norm_g", "mla_w_kv_b", "mla_w_q_a", "mla_q_a_norm_g", "mla_w_q_b", "mla_w_out",
            "ffn_w_up", "ffn_w_down", "final_norm_g")


def _ffn_fwd(x, h, w_up, w_down, tag):
    def relu_sq(acc):
        r = jnp.maximum(acc, 0.0)
        return r, r * r

    r, a = _matmul(h, w_up, mode="nn", out_dtypes=(BF16, BF16), epilogue=relu_sq, name=f"{tag}_up")
    x_out = _matmul(a, w_down, mode="nn", out_dtypes=(F32,), epilogue=lambda acc, res: (acc + res,),
                    extras=(x,), name=f"{tag}_down")
    return x_out, r, a


def _ffn_bwd(dx_out, x_in, h, r, a, g_norm, w_up, w_down, tag):
    d_u = _matmul(dx_out, w_down, mode="nt", out_dtypes=(BF16,), epilogue=lambda acc, rr: (acc * (2.0 * rr.astype(F32)),),
                  extras=(r,), name=f"{tag}_d_act")
    d_w_down = _matmul(a, dx_out, mode="tn", out_dtypes=(F32,), name=f"{tag}_d_w_down")
    d_w_up = _matmul(h, d_u, mode="tn", out_dtypes=(F32,), name=f"{tag}_d_w_up")
    d_h = _matmul(d_u, w_up, mode="nt", out_dtypes=(F32,), name=f"{tag}_d_h")
    dx_in, (d_g,) = _rms_bwd(x_in, [(g_norm, d_h)], dx_out, name=f"{tag}_d_norm")
    return dx_in, d_w_up, d_w_down, d_g


def kernel(x, norm_mix_g, norm_ffn_g, fox_w_in, fox_b_f, fox_w_out, kv_norm_g, mla_w_kv_a, mla_kv_a_norm_g, mla_w_kv_b, mla_w_q_a, mla_q_a_norm_g, mla_w_q_b, mla_w_out, ffn_w_up, ffn_w_down, final_norm_g, loss_target, m_norm_mix_g, m_norm_ffn_g, m_fox_w_in, m_fox_b_f, m_fox_w_out, m_kv_norm_g, m_mla_w_kv_a, m_mla_kv_a_norm_g, m_mla_w_kv_b, m_mla_w_q_a, m_mla_q_a_norm_g, m_mla_w_q_b, m_mla_w_out, m_ffn_w_up, m_ffn_w_down, m_final_norm_g, v_norm_mix_g, v_norm_ffn_g, v_fox_w_in, v_fox_b_f, v_fox_w_out, v_kv_norm_g, v_mla_w_kv_a, v_mla_kv_a_norm_g, v_mla_w_kv_b, v_mla_w_q_a, v_mla_q_a_norm_g, v_mla_w_q_b, v_mla_w_out, v_ffn_w_up, v_ffn_w_down, v_final_norm_g):
    args = dict(locals())
    w_in = {n: args[n] for n in _WEIGHTS}
    m_in = {n: args["m_" + n] for n in _WEIGHTS}
    v_in = {n: args["v_" + n] for n in _WEIGHTS}

    xs = x[0]
    seq, d_model = xs.shape
    tgt = loss_target[0]
    fox_h, mla_h, nope = FOX_HEADS, MLA_HEADS, QK_NOPE_DIM
    kv_rank = mla_kv_a_norm_g.shape[0]
    rope = mla_w_kv_a.shape[1] - kv_rank
    half = rope // 2
    q_rank = mla_q_a_norm_g.shape[1]
    v_dim = mla_w_kv_b.shape[1] * N_CHIPS // mla_h - nope
    fox_w = fox_w_out.shape[1] * N_CHIPS
    fox_dh = fox_w // fox_h

    big_names = [n for n, _ in _BIG]
    shard_shapes = [w_in[n].shape for n in big_names]
    rows = _packed_rows(shard_shapes)
    my_shard = _pack([w_in[n] for n in big_names], rows, BF16)
    others = _all_gather_shards(my_shard, name="gather_weights")
    by_relation = jnp.concatenate([my_shard[None], others], axis=0)
    p_chip = 2 * lax.axis_index("x") + lax.axis_index("y")
    full = {}
    for q in range(N_CHIPS):
        shard_q = lax.dynamic_index_in_dim(by_relation, p_chip ^ q, axis=0, keepdims=False)
        for (n, ax), piece in zip(_BIG, _unpack(shard_q, shard_shapes)):
            full.setdefault(n, []).append(piece)
    full = {n: jnp.concatenate(full[n], axis=ax) for n, ax in _BIG}

    fox_scale = fox_dh ** -0.5
    fox_wd = _round_up(fox_dh + 9, LANE_TILE)
    fox_vwd = _round_up(fox_dh + 4, LANE_TILE)
    w_fox_in = full["fox_w_in"][0]
    w_fq = _pad_heads(w_fox_in[:, :fox_w] * fox_scale, fox_h, fox_wd, 1)
    w_fk = _pad_heads(w_fox_in[:, fox_w:2 * fox_w], fox_h, fox_wd, 1)
    w_fv = _pad_heads(w_fox_in[:, 2 * fox_w:3 * fox_w], fox_h, fox_vwd, 1)
    w_gate = w_fox_in[:, 3 * fox_w:]
    w_fox_out = _pad_heads(full["fox_w_out"][0], fox_h, fox_vwd, 0)
    n_cx = _round_up(3 * fox_h + 1, LANE_TILE)
    c_piece = lambda i: (lambda hh: 3 * hh + i)
    one_col = 3 * fox_h
    e_fq = _placement(n_cx, fox_h, fox_wd, [(c_piece(i), fox_dh + i, 1.0) for i in range(3)]
                      + [(one_col, fox_dh + 3 + i, 1.0) for i in range(3)])
    e_fk = _placement(n_cx, fox_h, fox_wd, [(one_col, fox_dh + i, 1.0) for i in range(3)]
                      + [(c_piece(i), fox_dh + 3 + i, -1.0) for i in range(3)]
                      + [(one_col, fox_dh + 6 + i, 1.0) for i in range(3)])
    e_fv = _placement(n_cx, fox_h, fox_vwd, [(one_col, fox_dh + i, -1.0) for i in range(3)]
                      + [(one_col, fox_dh + 3, 1.0)])

    mla_scale = (nope + rope) ** -0.5
    mla_dk = nope + rope
    mla_wd = _round_up(mla_dk + 3, LANE_TILE)
    mla_vwd = _round_up(v_dim + 4, LANE_TILE)
    w_kv_a = full["mla_w_kv_a"]
    w_kv_b3 = full["mla_w_kv_b"].reshape(kv_rank, mla_h, nope + v_dim)
    w_kn = _pad_heads(w_kv_b3[:, :, :nope].reshape(kv_rank, -1), mla_h, mla_wd, 1)
    w_mv = _pad_heads(w_kv_b3[:, :, nope:].reshape(kv_rank, -1), mla_h, mla_vwd, 1)
    w_q_a = full["mla_w_q_a"][0]
    w_q_b3 = full["mla_w_q_b"][0].reshape(q_rank, mla_h, nope + rope)
    w_qa_ = _pad_heads(w_q_b3.reshape(q_rank, -1), mla_h, mla_wd, 1)
    w_qb_ = _pad_heads(jnp.concatenate([jnp.zeros_like(w_q_b3[:, :, :nope]), -w_q_b3[:, :, nope + half:],
                                        w_q_b3[:, :, nope:nope + half]], axis=-1).reshape(q_rank, -1),
                       mla_h, mla_wd, 1)
    w_mla_out = _pad_heads(full["mla_w_out"][0], mla_h, mla_vwd, 0)
    w_up, w_down = full["ffn_w_up"], full["ffn_w_down"]
    n_kx = _round_up(rope + 1, LANE_TILE)
    e_mk = _placement(n_kx, mla_h, mla_wd, [(j, nope + j, 1.0) for j in range(rope)]
                      + [(rope, mla_dk + i, 1.0) for i in range(3)])
    e_mv = _placement(n_kx, mla_h, mla_vwd, [(rope, v_dim + i, -1.0) for i in range(3)] + [(rope, v_dim + 3, 1.0)])

    inv = 1.0 / (ROPE_BASE ** (jnp.arange(0, rope, 2, dtype=F32) / rope))
    ang = jnp.arange(seq, dtype=F32)[:, None] * inv[None, :]
    cos, sin = jnp.cos(ang), jnp.sin(ang)
    pad_t = jnp.zeros((seq, mla_wd - mla_dk), F32)
    cos_t = jnp.concatenate([jnp.ones((seq, nope), F32), cos, cos, pad_t], axis=1)
    sin_t = jnp.concatenate([jnp.zeros((seq, nope), F32), sin, sin, pad_t], axis=1)

    (h0,) = _rms_fwd(xs, norm_mix_g[0:1], name="l0_norm_mix")
    gate = _matmul(h0, w_gate, mode="nn", out_dtypes=(F32,), name="fox_gate")
    z = gate + fox_b_f[0][None, :]
    cum = jnp.cumsum(jax.nn.log_sigmoid(z), axis=0)
    cx = jnp.concatenate([_split3(cum).reshape(seq, 3 * fox_h), jnp.ones((seq, 1), BF16),
                          jnp.zeros((seq, n_cx - 3 * fox_h - 1), BF16)], axis=1)
    fqa = _matmul(h0, w_fq, mode="nn", out_dtypes=(BF16,), placed=(cx, e_fq), name="fox_q")
    fka = _matmul(h0, w_fk, mode="nn", out_dtypes=(BF16,), placed=(cx, e_fk), name="fox_k")
    fva = _matmul(h0, w_fv, mode="nn", out_dtypes=(BF16,), placed=(cx, e_fv), name="fox_v")
    foa, fqb = _flash_fwd(fqa, fka, fva, fox_h, fox_dh + 3, fox_dh + 6, FOX_FWD_SUB_ROWS, name="fox_attn")
    add_res = lambda acc, res: (acc + res,)
    x1 = _matmul(foa, w_fox_out, mode="nn", out_dtypes=(F32,), epilogue=add_res, extras=(xs,), name="fox_out")
    (h1,) = _rms_fwd(x1, norm_ffn_g[0:1], name="l0_norm_ffn")
    x2, r0, a0 = _ffn_fwd(x1, h1, w_up[0], w_down[0], "ffn0")

    src, h2 = _rms_fwd(x2, jnp.stack([kv_norm_g, norm_mix_g[1]]), name="l1_norm_kv_mix")
    kv_a = _matmul(src, w_kv_a, mode="nn", out_dtypes=(F32,), name="mla_kv_a")
    (c_kv,) = _rms_fwd(kv_a, mla_kv_a_norm_g[None, :], name="mla_norm_kv_a")
    kr1, kr2 = _rope(kv_a[None, :, kv_rank:kv_rank + half], kv_a[None, :, kv_rank + half:], cos, sin, 1.0,
                     name="mla_rope_k")
    krx = jnp.concatenate([kr1.astype(BF16), kr2.astype(BF16), jnp.ones((seq, 1), BF16),
                           jnp.zeros((seq, n_kx - rope - 1), BF16)], axis=1)
    mka = _matmul(c_kv, w_kn, mode="nn", out_dtypes=(BF16,), placed=(krx, e_mk), name="mla_k")
    mva = _matmul(c_kv, w_mv, mode="nn", out_dtypes=(BF16,), placed=(krx, e_mv), name="mla_v")
    cq_pre = _matmul(h2, w_q_a, mode="nn", out_dtypes=(F32,), name="mla_q_a")
    (c_q,) = _rms_fwd(cq_pre, mla_q_a_norm_g, name="mla_norm_q_a")
    q_a_part = _matmul(c_q, w_qa_, mode="nn", out_dtypes=(F32,), name="mla_q_b_cos")
    q_b_part = _matmul(c_q, w_qb_, mode="nn", out_dtypes=(F32,), name="mla_q_b_sin")
    mqa = _rope_mix(q_a_part, q_b_part, cos_t, sin_t, mla_scale, mla_h, name="mla_rope_q")
    moa, mqb = _flash_fwd(mqa, mka, mva, mla_h, v_dim + 3, mla_dk, MLA_FWD_SUB_ROWS, name="mla_attn")
    x3 = _matmul(moa, w_mla_out, mode="nn", out_dtypes=(F32,), epilogue=add_res, extras=(x2,), name="mla_out")
    (h3,) = _rms_fwd(x3, norm_ffn_g[1:2], name="l1_norm_ffn")
    x4, r1, a1 = _ffn_fwd(x3, h3, w_up[1], w_down[1], "ffn1")

    loss_tile, dx4, d_final_g = _loss_head(x4, final_norm_g[None, :], tgt, name="loss_head")
    loss = lax.psum(loss_tile[0, 0], ("x", "y", "c"))

    gw = {}
    dx3, d_up1, d_down1, d_nf1 = _ffn_bwd(dx4, x3, h3, r1, a1, norm_ffn_g[1:2], w_up[1], w_down[1], "ffn1")

    d_mo = _matmul(dx3, w_mla_out, mode="nt", out_dtypes=(BF16,), name="mla_d_ctx")
    gw["mla_w_out"] = _unpad_heads(_matmul(moa, dx3, mode="tn", out_dtypes=(F32,), name="mla_d_w_out"),
                                   mla_h, v_dim, 0)[None]
    d_moa = _delta_place(d_mo, moa, mla_h, v_dim, name="mla_attn_delta")
    d_mqa, d_mka, d_mva = _flash_bwd(mqb, mka, mva, d_moa, mla_h, name="mla_attn_bwd")
    d_qa_part, d_qb_part = _rope_unmix(d_mqa, cos_t, sin_t, mla_scale, mla_h, name="mla_rope_dq")
    d_w_qa_ = _unpad_heads(_matmul(c_q, d_qa_part, mode="tn", out_dtypes=(F32,), name="mla_d_w_q_b_cos"),
                           mla_h, mla_dk, 1).reshape(q_rank, mla_h, mla_dk)
    d_w_qb_ = _unpad_heads(_matmul(c_q, d_qb_part, mode="tn", out_dtypes=(F32,), name="mla_d_w_q_b_sin"),
                           mla_h, mla_dk, 1).reshape(q_rank, mla_h, mla_dk)
    gw["mla_w_q_b"] = jnp.concatenate(
        [d_w_qa_[:, :, :nope], d_w_qa_[:, :, nope:nope + half] + d_w_qb_[:, :, nope + half:],
         d_w_qa_[:, :, nope + half:] - d_w_qb_[:, :, nope:nope + half]], axis=-1).reshape(1, q_rank, mla_h * mla_dk)
    d_c_q_sin = _matmul(d_qb_part, w_qb_, mode="nt", out_dtypes=(F32,), name="mla_d_c_q_sin")
    d_c_q = _matmul(d_qa_part, w_qa_, mode="nt", out_dtypes=(F32,), epilogue=add_res, extras=(d_c_q_sin,),
                    name="mla_d_c_q")
    d_cq_pre, (d_q_a_g,) = _rms_bwd(cq_pre, [(mla_q_a_norm_g, d_c_q)], None, name="mla_d_norm_q_a")
    gw["mla_w_q_a"] = _matmul(h2, d_cq_pre, mode="tn", out_dtypes=(F32,), name="mla_d_w_q_a")[None]
    d_h2 = _matmul(d_cq_pre, w_q_a, mode="nt", out_dtypes=(F32,), name="mla_d_h")

    d_w_kn = _unpad_heads(_matmul(c_kv, d_mka, mode="tn", out_dtypes=(F32,), name="mla_d_w_k"), mla_h, nope, 1)
    d_w_mv = _unpad_heads(_matmul(c_kv, d_mva, mode="tn", out_dtypes=(F32,), name="mla_d_w_v"), mla_h, v_dim, 1)
    gw["mla_w_kv_b"] = jnp.concatenate([d_w_kn.reshape(kv_rank, mla_h, nope), d_w_mv.reshape(kv_rank, mla_h, v_dim)],
                                       axis=-1).reshape(kv_rank, mla_h * (nope + v_dim))
    d_c_kv_v = _matmul(d_mva, w_mv, mode="nt", out_dtypes=(F32,), name="mla_d_c_kv_v")
    d_c_kv = _matmul(d_mka, w_kn, mode="nt", out_dtypes=(F32,), epilogue=add_res, extras=(d_c_kv_v,),
                     name="mla_d_c_kv")
    d_ckv_pre, (d_kv_a_g,) = _rms_bwd(kv_a, [(mla_kv_a_norm_g[None, :], d_c_kv)], None, name="mla_d_norm_kv_a")
    d_mk_rope = jnp.transpose(d_mka.reshape(seq, mla_h, mla_wd)[:, :, nope:mla_dk], (1, 0, 2))
    d_kr1, d_kr2 = _rope(d_mk_rope[:, :, :half], d_mk_rope[:, :, half:], cos, sin, -1.0, name="mla_rope_dk")
    d_kv_a = jnp.concatenate([d_ckv_pre, d_kr1, d_kr2], axis=1)
    gw["mla_w_kv_a"] = _matmul(src, d_kv_a, mode="tn", out_dtypes=(F32,), name="mla_d_w_kv_a")
    d_src = _matmul(d_kv_a, w_kv_a, mode="nt", out_dtypes=(F32,), name="mla_d_src")
    dx2, (d_kv_g, d_nm1) = _rms_bwd(x2, [(kv_norm_g[None, :], d_src), (norm_mix_g[1:2], d_h2)], dx3,
                                    name="l1_d_norm_kv_mix")

    dx1, d_up0, d_down0, d_nf0 = _ffn_bwd(dx2, x1, h1, r0, a0, norm_ffn_g[0:1], w_up[0], w_down[0], "ffn0")
    gw["ffn_w_up"] = jnp.stack([d_up0, d_up1])
    gw["ffn_w_down"] = jnp.stack([d_down0, d_down1])

    d_fo = _matmul(dx1, w_fox_out, mode="nt", out_dtypes=(BF16,), name="fox_d_ctx")
    gw["fox_w_out"] = _unpad_heads(_matmul(foa, dx1, mode="tn", out_dtypes=(F32,), name="fox_d_w_out"),
                                   fox_h, fox_dh, 0)[None]
    d_foa = _delta_place(d_fo, foa, fox_h, fox_dh, name="fox_attn_delta")
    d_fqa, d_fka, d_fva = _flash_bwd(fqb, fka, fva, d_foa, fox_h, name="fox_attn_bwd")
    d_cum = (d_fqa.reshape(seq, fox_h, fox_wd)[:, :, fox_dh]
             - d_fka.reshape(seq, fox_h, fox_wd)[:, :, fox_dh + 3])
    d_z = jnp.flip(jnp.cumsum(jnp.flip(d_cum, 0), axis=0), 0) * jax.nn.sigmoid(-z)
    d_b_f = jnp.sum(d_z, axis=0)
    d_w_in = [_unpad_heads(_matmul(h0, g, mode="tn", out_dtypes=(F32,), name=f"fox_d_w_{tag}"), fox_h, fox_dh, 1)
              for tag, g in (("q", d_fqa), ("k", d_fka), ("v", d_fva))]
    d_w_gate = _matmul(h0, d_z, mode="tn", out_dtypes=(F32,), name="fox_d_w_gate")
    gw["fox_w_in"] = jnp.concatenate([d_w_in[0] * fox_scale, d_w_in[1], d_w_in[2], d_w_gate], axis=1)[None]
    d_h0 = _matmul(d_z, w_gate, mode="nt", out_dtypes=(F32,), name="fox_d_h_gate")
    for tag, g, w in (("q", d_fqa, w_fq), ("k", d_fka, w_fk), ("v", d_fva, w_fv)):
        d_h0 = _matmul(g, w, mode="nt", out_dtypes=(F32,), epilogue=add_res, extras=(d_h0,), name=f"fox_d_h_{tag}")
    grad_x, (d_nm0,) = _rms_bwd(xs, [(norm_mix_g[0:1], d_h0)], dx1, name="l0_d_norm_mix")

    c_idx = lax.axis_index("c").astype(jnp.int32).reshape(1)
    per_chip = []
    for q in range(N_CHIPS):
        pieces = [jnp.split(gw[n], N_CHIPS, axis=ax)[q] for n, ax in _BIG]
        per_chip.append(_pack(pieces, rows, F32))
    g4 = jnp.stack(per_chip)
    a4 = _sibling_swap_halves(g4, name="grads_to_sibling")
    s4 = _chip_sum(g4, a4, c_idx, name="grads_chip_sum")
    b3 = _chip_exchange(s4, name="grads_between_chips")
    t_mine = _sum_chips(s4, b3, p_chip.astype(jnp.int32).reshape(1), name="grads_sum_chips")
    t_theirs = _sibling_swap(t_mine, name="grads_join_halves")
    is_south = lax.axis_index("c") == 0
    g_big = jnp.concatenate([jnp.where(is_south, t_mine, t_theirs), jnp.where(is_south, t_theirs, t_mine)],
                            axis=0)

    small_local = {"norm_mix_g": jnp.concatenate([d_nm0, d_nm1], axis=0),
                   "norm_ffn_g": jnp.concatenate([d_nf0, d_nf1], axis=0),
                   "fox_b_f": d_b_f[None, :], "kv_norm_g": d_kv_g[0], "mla_kv_a_norm_g": d_kv_a_g[0],
                   "mla_q_a_norm_g": d_q_a_g, "final_norm_g": d_final_g[0]}
    small_shapes = [w_in[n].shape for n in _SMALL]
    small_rows = sum(_part_rows(s, SMALL_PART_ROWS) for s in small_shapes)
    pack_small = lambda arrs: _pack(arrs, small_rows, F32, SMALL_PART_ROWS)
    g_small = _all_reduce_small(pack_small([small_local[n] for n in _SMALL]), name="grads_small")

    d_big, nm_big, nv_big = _adamw(_pack([w_in[n] for n in big_names], rows, F32), g_big,
                                   _pack([m_in[n] for n in big_names], rows, F32),
                                   _pack([v_in[n] for n in big_names], rows, F32), name="adamw_big")
    d_sm, nm_sm, nv_sm = _adamw(pack_small([w_in[n] for n in _SMALL]), g_small,
                                pack_small([m_in[n] for n in _SMALL]),
                                pack_small([v_in[n] for n in _SMALL]), name="adamw_small")

    def spread(big, small):
        out = dict(zip(big_names, _unpack(big, shard_shapes)))
        out.update(zip(_SMALL, _unpack(small, small_shapes, SMALL_PART_ROWS)))
        return [out[n] for n in _WEIGHTS]

    return (loss, grad_x[None], *spread(g_big, g_small), *spread(d_big, d_sm), *spread(nm_big, nm_sm),
            *spread(nv_big, nv_sm))
```

```python
import math

import numpy as np
import jax
import jax.numpy as jnp
from jax import lax
from jax.experimental import pallas as pl
from jax.experimental.pallas import tpu as pltpu

F32 = jnp.float32
BF16 = jnp.bfloat16

FOX_HEADS = 16
MLA_HEADS = 8
QK_NOPE_DIM = 128
ROPE_BASE = 10000.0
EPS = 1e-6

ADAM_LR = 0.001
ADAM_B1 = 0.9
ADAM_B2 = 0.999
ADAM_EPS = 1e-08
ADAM_WD = 0.01
ADAM_STEP = 10

N_CHIPS = 4
N_DEV = 8
PACK_LANES = 1024
PACK_PART_ROWS = 16
SMALL_PART_ROWS = 8
PACK_ROWS_MULT = 1024
VMEM_LIMIT_BYTES = 48 * 1024 * 1024
LANE_TILE = 128
MATMUL_BLOCK = 1024
ATTN_BLOCK_Q = 1024
ATTN_BLOCK_K = 1024
ATTN_FWD_HEADS_PER_STEP = 2
FOX_BWD_HEADS_PER_STEP = 2
MLA_BWD_HEADS_PER_STEP = 1
ATTN_SUB_ROWS = 256
FOX_FWD_SUB_ROWS = (1024, 512)
MLA_FWD_SUB_ROWS = (256, 256)
NEG_BIG = -1e30
MESH = pl.DeviceIdType.MESH


def _round_up(n, m):
    return -(-n // m) * m


def _blk(dim, pref, mult=128):
    if dim <= pref:
        return dim
    b = (pref // mult) * mult
    while b >= mult:
        if dim % b == 0:
            return b
        b -= mult
    return dim


def _params(sem=None):
    return pltpu.CompilerParams(dimension_semantics=sem, vmem_limit_bytes=VMEM_LIMIT_BYTES)


_DIMS = {"nn": (((1,), (0,)), ((), ())), "nt": (((1,), (1,)), ((), ())), "tn": (((0,), (0,)), ((), ()))}


def _matmul(a, b, *, mode, out_dtypes, name, epilogue=None, extras=(), placed=None):
    if mode == "tn":
        kdim, m = a.shape
    else:
        m, kdim = a.shape
    n = b.shape[0] if mode == "nt" else b.shape[1]
    bm, bn, bk = _blk(m, MATMUL_BLOCK), _blk(n, MATMUL_BLOCK), _blk(kdim, MATMUL_BLOCK)
    nk = kdim // bk
    n_extra, n_out = len(extras), len(out_dtypes)
    n_placed = 0 if placed is None else 2
    dims = _DIMS[mode]

    def body(a_ref, b_ref, *rest):
        placed_refs = rest[:n_placed]
        rest = rest[n_placed:]
        extra_refs = rest[:n_extra]
        out_refs = rest[n_extra:n_extra + n_out]

        def finish(acc):
            if n_placed:
                acc = acc + lax.dot_general(placed_refs[0][...], placed_refs[1][...], _DIMS["nn"],
                                            preferred_element_type=F32)
            res = (acc,) if epilogue is None else epilogue(acc, *[r[...] for r in extra_refs])
            for o_ref, r in zip(out_refs, res):
                o_ref[...] = r.astype(o_ref.dtype)

        part = lax.dot_general(a_ref[...].astype(BF16), b_ref[...].astype(BF16), dims, preferred_element_type=F32)
        if nk == 1:
            finish(part)
            return
        acc_ref = rest[n_extra + n_out]
        k = pl.program_id(2)

        @pl.when(k == 0)
        def _():
            acc_ref[...] = part

        @pl.when((k > 0) & (k < nk - 1))
        def _():
            acc_ref[...] += part

        @pl.when(k == nk - 1)
        def _():
            finish(acc_ref[...] + part)

    if mode == "tn":
        a_spec = pl.BlockSpec((bk, bm), lambda i, j, k: (k, i))
    else:
        a_spec = pl.BlockSpec((bm, bk), lambda i, j, k: (i, k))
    if mode == "nt":
        b_spec = pl.BlockSpec((bn, bk), lambda i, j, k: (j, k))
    else:
        b_spec = pl.BlockSpec((bk, bn), lambda i, j, k: (k, j))
    tile = pl.BlockSpec((bm, bn), lambda i, j, k: (i, j))
    placed_specs = []
    if n_placed:
        k2 = placed[0].shape[1]
        placed_specs = [pl.BlockSpec((bm, k2), lambda i, j, k: (i, 0)), pl.BlockSpec((k2, bn), lambda i, j, k: (0, j))]
    outs = pl.pallas_call(
        body, name=name,
        grid=(m // bm, n // bn, nk),
        in_specs=[a_spec, b_spec] + placed_specs + [tile] * n_extra,
        out_specs=[tile] * n_out,
        out_shape=[jax.ShapeDtypeStruct((m, n), dt) for dt in out_dtypes],
        scratch_shapes=[pltpu.VMEM((bm, bn), F32)] if nk > 1 else [],
        compiler_params=_params(("parallel", "parallel", "arbitrary")),
    )(a, b, *(placed or ()), *extras)
    return outs[0] if n_out == 1 else outs


def _rms_fwd(x, gains, name):
    s = x.shape[0]
    g, w = gains.shape
    bs = _blk(s, 512, 8)

    def body(x_ref, g_ref, *out_refs):
        xv = x_ref[...]
        y = xv * lax.rsqrt(jnp.mean(xv * xv, axis=-1, keepdims=True) + EPS)
        for i, o_ref in enumerate(out_refs):
            o_ref[...] = (y * g_ref[i:i + 1, :]).astype(o_ref.dtype)

    row = pl.BlockSpec((bs, w), lambda i: (i, 0))
    outs = pl.pallas_call(
        body, name=name, grid=(s // bs,),
        in_specs=[row, pl.BlockSpec((g, w), lambda i: (0, 0))],
        out_specs=[row] * g,
        out_shape=[jax.ShapeDtypeStruct((s, w), BF16)] * g,
        compiler_params=_params(("parallel",)),
    )(x, gains)
    return outs


def _rms_bwd(x, branches, resid, name):
    s = x.shape[0]
    w = branches[0][0].shape[1]
    nb = len(branches)
    bs = _blk(s, 512, 8)
    has_resid = resid is not None

    def body(x_ref, *rest):
        g_refs = rest[:nb]
        dy_refs = rest[nb:2 * nb]
        pos = 2 * nb
        r_ref = rest[pos] if has_resid else None
        pos += int(has_resid)
        dx_ref = rest[pos]
        dg_refs = rest[pos + 1:pos + 1 + nb]
        i = pl.program_id(0)

        @pl.when(i == 0)
        def _():
            for dg_ref in dg_refs:
                dg_ref[...] = jnp.zeros_like(dg_ref)

        xv = x_ref[...]
        rstd = lax.rsqrt(jnp.mean(xv * xv, axis=-1, keepdims=True) + EPS)
        xhat = xv * rstd
        dx = r_ref[...] if has_resid else jnp.zeros_like(xv)
        for g_ref, dy_ref, dg_ref in zip(g_refs, dy_refs, dg_refs):
            dy = dy_ref[...].astype(F32)
            dyg = dy * g_ref[...]
            dx = dx + rstd * (dyg - xhat * jnp.mean(dyg * xhat, axis=-1, keepdims=True))
            dg_ref[...] += jnp.sum(dy * xhat, axis=0, keepdims=True)
        dx_ref[...] = dx

    row = pl.BlockSpec((bs, w), lambda i: (i, 0))
    vec = pl.BlockSpec((1, w), lambda i: (0, 0))
    args = [x] + [g for g, _ in branches] + [dy for _, dy in branches] + ([resid] if has_resid else [])
    outs = pl.pallas_call(
        body, name=name, grid=(s // bs,),
        in_specs=[row] + [vec] * nb + [row] * nb + ([row] if has_resid else []),
        out_specs=[row] + [vec] * nb,
        out_shape=[jax.ShapeDtypeStruct((s, w), F32)] + [jax.ShapeDtypeStruct((1, w), F32)] * nb,
        compiler_params=_params(("arbitrary",)),
    )(*args)
    return outs[0], list(outs[1:])


def _loss_head(x, g, target, name):
    s, w = x.shape
    bs = _blk(s, 512, 8)

    def body(x_ref, g_ref, t_ref, loss_ref, dx_ref, dg_ref):
        i = pl.program_id(0)

        @pl.when(i == 0)
        def _():
            loss_ref[...] = jnp.zeros_like(loss_ref)
            dg_ref[...] = jnp.zeros_like(dg_ref)

        xv = x_ref[...]
        gv = g_ref[...]
        rstd = lax.rsqrt(jnp.mean(xv * xv, axis=-1, keepdims=True) + EPS)
        xhat = xv * rstd
        err = xhat * gv - t_ref[...]
        loss_ref[...] += 0.5 * jnp.sum(jnp.mean(err * err, axis=-1, keepdims=True))
        dy = err * (1.0 / w)
        dyg = dy * gv
        dx_ref[...] = rstd * (dyg - xhat * jnp.mean(dyg * xhat, axis=-1, keepdims=True))
        dg_ref[...] += jnp.sum(dy * xhat, axis=0, keepdims=True)

    row = pl.BlockSpec((bs, w), lambda i: (i, 0))
    vec = pl.BlockSpec((1, w), lambda i: (0, 0))
    return pl.pallas_call(
        body, name=name, grid=(s // bs,),
        in_specs=[row, vec, row],
        out_specs=[pl.BlockSpec((8, 128), lambda i: (0, 0)), row, vec],
        out_shape=[jax.ShapeDtypeStruct((8, 128), F32), jax.ShapeDtypeStruct((s, w), F32),
                   jax.ShapeDtypeStruct((1, w), F32)],
        compiler_params=_params(("arbitrary",)),
    )(x, g, target)


def _rope(a, b, cos, sin, sign, name):
    g, s, w = a.shape
    bs = _blk(s, 1024, 8)

    def body(a_ref, b_ref, c_ref, s_ref, o1_ref, o2_ref):
        av = jnp.sum(a_ref[...].astype(F32), axis=0)
        bv = jnp.sum(b_ref[...].astype(F32), axis=0)
        cv, sv = c_ref[...], s_ref[...] * sign
        o1_ref[...] = av * cv - bv * sv
        o2_ref[...] = bv * cv + av * sv

    grp = pl.BlockSpec((g, bs, w), lambda i: (0, i, 0))
    row = pl.BlockSpec((bs, w), lambda i: (i, 0))
    return pl.pallas_call(
        body, name=name, grid=(s // bs,),
        in_specs=[grp, grp, row, row], out_specs=[row, row],
        out_shape=[jax.ShapeDtypeStruct((s, w), F32)] * 2,
        compiler_params=_params(("parallel",)),
    )(a, b, cos, sin)


def _causal_table(s, bq, bk, q_major):
    nq, nk = s // bq, s // bk
    rows = []
    if q_major:
        for qi in range(nq):
            kmax = (qi * bq + bq - 1) // bk
            for ki in range(kmax + 1):
                rows.append((qi, ki, int(ki * bk + bk - 1 > qi * bq), int(ki == 0), int(ki == kmax)))
    else:
        for ki in range(nk):
            qmin = (ki * bk) // bq
            for qi in range(qmin, nq):
                rows.append((qi, ki, int(ki * bk + bk - 1 > qi * bq), int(qi == qmin), int(qi == nq - 1)))
    return jnp.asarray(np.array(rows, np.int32).T)


def _causal_keep(q0, k0, nq, nk, transposed):
    if transposed:
        kpos = k0 + lax.broadcasted_iota(jnp.int32, (nk, nq), 0)
        qpos = q0 + lax.broadcasted_iota(jnp.int32, (nk, nq), 1)
    else:
        qpos = q0 + lax.broadcasted_iota(jnp.int32, (nq, nk), 0)
        kpos = k0 + lax.broadcasted_iota(jnp.int32, (nq, nk), 1)
    return kpos <= qpos


def _sub_tiles(n_rows, n_cols, masked, square, rows_are_keys, sub_rows):
    sub = min(sub_rows, n_rows)
    out = []
    for r0 in range(0, n_rows, sub):
        if masked and square:
            c0, nc = (r0, n_cols - r0) if rows_are_keys else (0, r0 + sub)
        else:
            c0, nc = 0, n_cols
        out.append((r0, sub, c0, nc))
    return out


_NT = (((1,), (1,)), ((), ()))
_NN = (((1,), (0,)), ((), ()))


def _attn_specs(bq, bk):
    qspec = lambda d: pl.BlockSpec((bq, d), lambda hh, t, tb: (tb[0, t], hh))
    kspec = lambda d: pl.BlockSpec((bk, d), lambda hh, t, tb: (tb[1, t], hh))
    return qspec, kspec


def _split3_cols(x):
    hi = x.astype(BF16).astype(F32)
    rest = x - hi
    mid = rest.astype(BF16).astype(F32)
    lo = (rest - mid).astype(BF16).astype(F32)
    return hi, mid, lo


def _place3(base, col, pieces, sign):
    lane = lax.broadcasted_iota(jnp.int32, base.shape, 1)
    out = base.astype(F32)
    for i, piece in enumerate(pieces):
        out = jnp.where(lane == col + i, sign * piece, out)
    return out.astype(BF16)


def _flash_fwd(qa, ka, va, heads, l_col, lse_col, sub_rows, name):
    s = qa.shape[0]
    da, dv = qa.shape[1] // heads, va.shape[1] // heads
    hps = ATTN_FWD_HEADS_PER_STEP if heads % ATTN_FWD_HEADS_PER_STEP == 0 else 1
    bq, bk = _blk(s, ATTN_BLOCK_Q), _blk(s, ATTN_BLOCK_K)
    tab = _causal_table(s, bq, bk, True)

    def body(tab_ref, q_ref, k_ref, v_ref, o_ref, qb_ref, m_sc, acc_sc):
        t = pl.program_id(1)
        qi, ki = tab_ref[0, t], tab_ref[1, t]

        @pl.when(tab_ref[3, t] == 1)
        def _():
            m_sc[...] = jnp.full_like(m_sc, NEG_BIG)
            acc_sc[...] = jnp.zeros_like(acc_sc)

        def step(masked):
            for hh in range(hps):
                qc, vc = slice(hh * da, (hh + 1) * da), slice(hh * dv, (hh + 1) * dv)
                for r0, nr, c0, nc in _sub_tiles(bq, bk, masked, bq == bk, False, sub_rows[int(masked)]):
                    sc = lax.dot_general(q_ref[r0:r0 + nr, qc], k_ref[c0:c0 + nc, qc], _NT,
                                         preferred_element_type=F32)
                    if masked:
                        sc = jnp.where(_causal_keep(qi * bq + r0, ki * bk + c0, nr, nc, False), sc, NEG_BIG)
                    m_prev = m_sc[hh, r0:r0 + nr, :]
                    m_new = jnp.maximum(m_prev, jnp.max(sc, axis=-1, keepdims=True))
                    p = jnp.exp(sc - m_new).astype(BF16)
                    acc_sc[r0:r0 + nr, vc] = jnp.exp(m_prev - m_new) * acc_sc[r0:r0 + nr, vc] + lax.dot_general(
                        p, v_ref[c0:c0 + nc, vc], _NN, preferred_element_type=F32)
                    m_sc[hh, r0:r0 + nr, :] = m_new

        @pl.when(tab_ref[2, t] == 1)
        def _():
            step(True)

        @pl.when(tab_ref[2, t] == 0)
        def _():
            step(False)

        @pl.when(tab_ref[4, t] == 1)
        def _():
            for hh in range(hps):
                qc, vc = slice(hh * da, (hh + 1) * da), slice(hh * dv, (hh + 1) * dv)
                acc = acc_sc[:, vc]
                lane = lax.broadcasted_iota(jnp.int32, acc.shape, 1)
                l = jnp.sum(jnp.where(lane == l_col, acc, 0.0), axis=-1, keepdims=True)
                o_ref[:, vc] = (acc / l).astype(o_ref.dtype)
                lse = m_sc[hh] + jnp.log(l)
                qb_ref[:, qc] = _place3(q_ref[:, qc], lse_col, _split3_cols(lse), -1.0)

    qspec, kspec = _attn_specs(bq, bk)
    return pl.pallas_call(
        body, name=name,
        grid_spec=pltpu.PrefetchScalarGridSpec(
            num_scalar_prefetch=1, grid=(heads // hps, tab.shape[1]),
            in_specs=[qspec(hps * da), kspec(hps * da), kspec(hps * dv)],
            out_specs=[qspec(hps * dv), qspec(hps * da)],
            scratch_shapes=[pltpu.VMEM((hps, bq, 1), F32), pltpu.VMEM((bq, hps * dv), F32)]),
        out_shape=[jax.ShapeDtypeStruct((s, heads * dv), BF16), jax.ShapeDtypeStruct((s, heads * da), BF16)],
        compiler_params=_params(("parallel", "arbitrary")),
    )(tab, qa, ka, va)


def _delta_place(do, o, heads, delta_col, name):
    s = o.shape[0]
    dv = o.shape[1] // heads
    bs = _blk(s, 512, 8)
    hpb = max(1, min(heads, 1024 // dv))
    while heads % hpb:
        hpb -= 1

    def body(do_ref, o_ref, out_ref):
        for hh in range(hpb):
            vc = slice(hh * dv, (hh + 1) * dv)
            dov = do_ref[:, vc]
            delta = jnp.sum(dov.astype(F32) * o_ref[:, vc].astype(F32), axis=-1, keepdims=True)
            out_ref[:, vc] = _place3(dov, delta_col, _split3_cols(delta), 1.0)

    blk = pl.BlockSpec((bs, hpb * dv), lambda i, hh: (i, hh))
    return pl.pallas_call(
        body, name=name, grid=(s // bs, heads // hpb), in_specs=[blk, blk], out_specs=blk,
        out_shape=jax.ShapeDtypeStruct(do.shape, BF16),
        compiler_params=_params(("parallel", "parallel")),
    )(do, o)


_TN =(((0,), (0,)), ((), ()))


def _flash_bwd(qa, ka, va, doa, heads, hps, name):
    s = qa.shape[0]
    da, dv = qa.shape[1] // heads, va.shape[1] // heads
    h = heads // hps
    bq, bk = _blk(s, ATTN_BLOCK_Q), _blk(s, ATTN_BLOCK_K)
    tab = _causal_table(s, bq, bk, False)
    n_tiles = tab.shape[1]

    def body(tab_ref, q_ref, k_ref, v_ref, do_ref, dq_ref, dk_ref, dv_ref, dk_sc, dv_sc):
        t = pl.program_id(1)
        qi, ki = tab_ref[0, t], tab_ref[1, t]

        @pl.when(t == 0)
        def _():
            dq_ref[...] = jnp.zeros_like(dq_ref)

        @pl.when(tab_ref[3, t] == 1)
        def _():
            dk_sc[...] = jnp.zeros_like(dk_sc)
            dv_sc[...] = jnp.zeros_like(dv_sc)

        def step(masked):
            for hh in range(hps):
                qc, vc = slice(hh * da, (hh + 1) * da), slice(hh * dv, (hh + 1) * dv)
                for r0, nr, c0, nc in _sub_tiles(bk, bq, masked, bq == bk, True, ATTN_SUB_ROWS):
                    qv, dov, kv = q_ref[c0:c0 + nc, qc], do_ref[c0:c0 + nc, vc], k_ref[r0:r0 + nr, qc]
                    st = lax.dot_general(kv, qv, _NT, preferred_element_type=F32)
                    if masked:
                        st = jnp.where(_causal_keep(qi * bq + c0, ki * bk + r0, nc, nr, True), st, NEG_BIG)
                    pt = jnp.exp(st)
                    dv_sc[r0:r0 + nr, vc] += lax.dot_general(pt.astype(BF16), dov, _NN, preferred_element_type=F32)
                    dpt = lax.dot_general(v_ref[r0:r0 + nr, vc], dov, _NT, preferred_element_type=F32)
                    dst = (pt * dpt).astype(BF16)
                    dk_sc[r0:r0 + nr, qc] += lax.dot_general(dst, qv, _NN, preferred_element_type=F32)
                    q_rows = pl.ds(pl.multiple_of(qi * bq + c0, ATTN_SUB_ROWS), nc)
                    dq_ref[q_rows, qc] += lax.dot_general(dst, kv, _TN, preferred_element_type=F32)

        @pl.when(tab_ref[2, t] == 1)
        def _():
            step(True)

        @pl.when(tab_ref[2, t] == 0)
        def _():
            step(False)

        @pl.when(tab_ref[4, t] == 1)
        def _():
            dk_ref[...] = dk_sc[...]
            dv_ref[...] = dv_sc[...]

    qspec, kspec = _attn_specs(bq, bk)
    return pl.pallas_call(
        body, name=name,
        grid_spec=pltpu.PrefetchScalarGridSpec(
            num_scalar_prefetch=1, grid=(h, n_tiles),
            in_specs=[qspec(hps * da), kspec(hps * da), kspec(hps * dv), qspec(hps * dv)],
            out_specs=[pl.BlockSpec((s, hps * da), lambda hh, t, tb: (0, hh)), kspec(hps * da), kspec(hps * dv)],
            scratch_shapes=[pltpu.VMEM((bk, hps * da), F32), pltpu.VMEM((bk, hps * dv), F32)]),
        out_shape=[jax.ShapeDtypeStruct((s, heads * da), F32), jax.ShapeDtypeStruct((s, heads * da), F32),
                   jax.ShapeDtypeStruct((s, heads * dv), F32)],
        compiler_params=_params(("parallel", "arbitrary")),
    )(tab, qa, ka, va, doa)


def _split3(x):
    hi = lax.reduce_precision(x, 8, 7)
    rest = x - hi
    mid = lax.reduce_precision(rest, 8, 7)
    lo = lax.reduce_precision(rest - mid, 8, 7)
    return jnp.stack([hi, mid, lo], axis=-1).astype(BF16)


def _pad_heads(w, heads, width, axis):
    shape = list(w.shape)
    d = shape[axis] // heads
    w = w.reshape(shape[:axis] + [heads, d] + shape[axis + 1:])
    pad = [(0, 0)] * w.ndim
    pad[axis + 1] = (0, width - d)
    return jnp.pad(w, pad).reshape(shape[:axis] + [heads * width] + shape[axis + 1:])


def _unpad_heads(w, heads, d, axis):
    shape = list(w.shape)
    width = shape[axis] // heads
    w = w.reshape(shape[:axis] + [heads, width] + shape[axis + 1:])
    w = lax.slice_in_dim(w, 0, d, axis=axis + 1)
    return w.reshape(shape[:axis] + [heads * d] + shape[axis + 1:])


def _placement(rows, heads, width, entries):
    e = np.zeros((rows, heads * width), np.float32)
    for row, col, val in entries:
        for hh in range(heads):
            e[row(hh) if callable(row) else row, hh * width + col] = val
    return jnp.asarray(e, BF16)


def _rope_mix(a, b, cos_t, sin_t, scale, heads, name):
    s = a.shape[0]
    d = a.shape[1] // heads
    bs = _blk(s, 1024, 8)

    def body(a_ref, b_ref, c_ref, s_ref, o_ref):
        o_ref[...] = ((a_ref[...] * c_ref[...] + b_ref[...] * s_ref[...]) * scale).astype(o_ref.dtype)

    blk = pl.BlockSpec((bs, d), lambda i, hh: (i, hh))
    tbl = pl.BlockSpec((bs, d), lambda i, hh: (i, 0))
    return pl.pallas_call(
        body, name=name, grid=(s // bs, heads), in_specs=[blk, blk, tbl, tbl], out_specs=blk,
        out_shape=jax.ShapeDtypeStruct(a.shape, BF16),
        compiler_params=_params(("parallel", "parallel")),
    )(a, b, cos_t, sin_t)


def _rope_unmix(g, cos_t, sin_t, scale, heads, name):
    s = g.shape[0]
    d = g.shape[1] // heads
    bs = _blk(s, 1024, 8)

    def body(g_ref, c_ref, s_ref, da_ref, db_ref):
        gv = g_ref[...] * scale
        da_ref[...] = (gv * c_ref[...]).astype(da_ref.dtype)
        db_ref[...] = (gv * s_ref[...]).astype(db_ref.dtype)

    blk = pl.BlockSpec((bs, d), lambda i, hh: (i, hh))
    tbl = pl.BlockSpec((bs, d), lambda i, hh: (i, 0))
    return pl.pallas_call(
        body, name=name, grid=(s // bs, heads), in_specs=[blk, tbl, tbl], out_specs=[blk, blk],
        out_shape=[jax.ShapeDtypeStruct(g.shape, BF16)] * 2,
        compiler_params=_params(("parallel", "parallel")),
    )(g, cos_t, sin_t)


def _adamw(w, g, m, v, name):
    r, wd = w.shape
    br = _blk(r, 512, 8)

    def body(w_ref, g_ref, m_ref, v_ref, d_ref, nm_ref, nv_ref):
        gv = g_ref[...]
        mn = ADAM_B1 * m_ref[...] + (1.0 - ADAM_B1) * gv
        vn = ADAM_B2 * v_ref[...] + (1.0 - ADAM_B2) * (gv * gv)
        m_hat = mn / (1.0 - ADAM_B1 ** ADAM_STEP)
        v_hat = vn / (1.0 - ADAM_B2 ** ADAM_STEP)
        d_ref[...] = -ADAM_LR * (m_hat / (jnp.sqrt(v_hat) + ADAM_EPS) + ADAM_WD * w_ref[...])
        nm_ref[...] = mn
        nv_ref[...] = vn

    row = pl.BlockSpec((br, wd), lambda i: (i, 0))
    return pl.pallas_call(
        body, name=name, grid=(r // br,), in_specs=[row] * 4, out_specs=[row] * 3,
        out_shape=[jax.ShapeDtypeStruct((r, wd), F32)] * 3,
        compiler_params=_params(("parallel",)),
    )(w, g, m, v)


_ANY = pl.BlockSpec(memory_space=pl.ANY)


def _place():
    x, y, c = lax.axis_index("x"), lax.axis_index("y"), lax.axis_index("c")
    chips = [(x, 1 - y), (1 - x, y), (1 - x, 1 - y)]
    return x, y, c, chips


def _all_gather_shards(shard, name):
    r, w = shard.shape
    hr = r // 2

    def body(x_ref, out_ref, send_sems, recv_sems):
        x, y, c, chips = _place()
        sibling = (x, y, 1 - c)

        def rows(j, half):
            return out_ref.at[j, pl.ds(pl.multiple_of(half * hr, 16), hr), :]

        def copy(sem, j, half, to, src=None):
            return pltpu.make_async_remote_copy(
                src_ref=rows(j, half) if src is None else src, dst_ref=rows(j, half),
                send_sem=send_sems.at[sem], recv_sem=recv_sems.at[sem], device_id=to, device_id_type=MESH)

        my_half = x_ref.at[pl.ds(pl.multiple_of(c * hr, 16), hr), :]
        first = [copy(j, j, c, (cx, cy, c), src=my_half) for j, (cx, cy) in enumerate(chips)]
        for cp in first:
            cp.start()
        passed = []
        for j in range(3):
            copy(j, j, c, (x, y, c)).wait_recv()
            fw = copy(3 + j, j, c, sibling)
            fw.start()
            passed.append(fw)
        for j in range(3):
            copy(3 + j, j, 1 - c, (x, y, c)).wait_recv()
        for cp in first + passed:
            cp.wait_send()

    return pl.pallas_call(
        body, name=name, in_specs=[_ANY], out_specs=_ANY,
        out_shape=jax.ShapeDtypeStruct((N_CHIPS - 1, r, w), shard.dtype),
        scratch_shapes=[pltpu.SemaphoreType.DMA((6,)), pltpu.SemaphoreType.DMA((6,))],
        compiler_params=pltpu.CompilerParams(vmem_limit_bytes=VMEM_LIMIT_BYTES),
    )(shard)


def _sibling_swap_halves(g, name):
    nq, r, w = g.shape
    hr = r // 2

    def body(g_ref, a_ref, send_sems, recv_sems):
        x, y, c, _ = _place()
        sibling = (x, y, 1 - c)
        cps = []
        for q in range(nq):
            cp = pltpu.make_async_remote_copy(
                src_ref=g_ref.at[q, pl.ds(pl.multiple_of((1 - c) * hr, 8), hr), :], dst_ref=a_ref.at[q],
                send_sem=send_sems.at[q], recv_sem=recv_sems.at[q], device_id=sibling, device_id_type=MESH)
            cp.start()
            cps.append(cp)
        for cp in cps:
            cp.wait()

    return pl.pallas_call(
        body, name=name, in_specs=[_ANY], out_specs=_ANY,
        out_shape=jax.ShapeDtypeStruct((nq, hr, w), g.dtype),
        scratch_shapes=[pltpu.SemaphoreType.DMA((nq,)), pltpu.SemaphoreType.DMA((nq,))],
        compiler_params=pltpu.CompilerParams(vmem_limit_bytes=VMEM_LIMIT_BYTES),
    )(g)


def _chip_sum(g, a, c_idx, name):
    nq, r, w = g.shape
    hr = r // 2
    br = _blk(hr, 512, 16)
    nb = hr // br

    def body(c_ref, g_ref, a_ref, o_ref):
        o_ref[...] = (g_ref[...] + a_ref[...]).astype(o_ref.dtype)

    return pl.pallas_call(
        body, name=name,
        grid_spec=pltpu.PrefetchScalarGridSpec(
            num_scalar_prefetch=1, grid=(nq, nb),
            in_specs=[pl.BlockSpec((None, br, w), lambda q, i, cr: (q, cr[0] * nb + i, 0)),
                      pl.BlockSpec((None, br, w), lambda q, i, cr: (q, i, 0))],
            out_specs=pl.BlockSpec((None, br, w), lambda q, i, cr: (q, i, 0))),
        out_shape=jax.ShapeDtypeStruct((nq, hr, w), BF16),
        compiler_params=_params(("parallel", "parallel")),
    )(c_idx, g, a)


def _chip_exchange(s4, name):
    nq, hr, w = s4.shape

    def body(s_ref, b_ref, send_sems, recv_sems):
        x, y, c, chips = _place()
        cps = []
        for j, (cx, cy) in enumerate(chips):
            cp = pltpu.make_async_remote_copy(
                src_ref=s_ref.at[2 * cx + cy], dst_ref=b_ref.at[j],
                send_sem=send_sems.at[j], recv_sem=recv_sems.at[j], device_id=(cx, cy, c), device_id_type=MESH)
            cp.start()
            cps.append(cp)
        for cp in cps:
            cp.wait()

    return pl.pallas_call(
        body, name=name, in_specs=[_ANY], out_specs=_ANY,
        out_shape=jax.ShapeDtypeStruct((nq - 1, hr, w), s4.dtype),
        scratch_shapes=[pltpu.SemaphoreType.DMA((3,)), pltpu.SemaphoreType.DMA((3,))],
        compiler_params=pltpu.CompilerParams(vmem_limit_bytes=VMEM_LIMIT_BYTES),
    )(s4)


def _sum_chips(s4, b3, p_idx, name):
    _, hr, w = s4.shape
    nb3 = b3.shape[0]
    br = _blk(hr, 512, 16)

    def body(p_ref, s_ref, b_ref, o_ref):
        acc = s_ref[...].astype(F32)
        for j in range(nb3):
            acc = acc + b_ref[j].astype(F32)
        o_ref[...] = acc

    return pl.pallas_call(
        body, name=name,
        grid_spec=pltpu.PrefetchScalarGridSpec(
            num_scalar_prefetch=1, grid=(hr // br,),
            in_specs=[pl.BlockSpec((None, br, w), lambda i, pr: (pr[0], i, 0)),
                      pl.BlockSpec((nb3, br, w), lambda i, pr: (0, i, 0))],
            out_specs=pl.BlockSpec((br, w), lambda i, pr: (i, 0))),
        out_shape=jax.ShapeDtypeStruct((hr, w), F32),
        compiler_params=_params(("parallel",)),
    )(p_idx, s4, b3)


def _sibling_swap(t, name):
    hr, w = t.shape

    def body(t_ref, o_ref, send_sem, recv_sem):
        x, y, c, _ = _place()
        cp = pltpu.make_async_remote_copy(src_ref=t_ref, dst_ref=o_ref, send_sem=send_sem, recv_sem=recv_sem,
                                          device_id=(x, y, 1 - c), device_id_type=MESH)
        cp.start()
        cp.wait()

    return pl.pallas_call(
        body, name=name, in_specs=[_ANY], out_specs=_ANY,
        out_shape=jax.ShapeDtypeStruct((hr, w), t.dtype),
        scratch_shapes=[pltpu.SemaphoreType.DMA, pltpu.SemaphoreType.DMA],
        compiler_params=pltpu.CompilerParams(vmem_limit_bytes=VMEM_LIMIT_BYTES),
    )(t)


def _all_reduce_small(v, name):
    r, w = v.shape

    def body(v_ref, o_ref, slots, send_sems, recv_sems):
        x, y, c, _ = _place()
        me = 4 * x + 2 * y + c
        slots[me] = v_ref[...]
        cps = []
        for k in range(1, N_DEV):
            fx, fy, fc = (k >> 2) & 1, (k >> 1) & 1, k & 1
            to = (x ^ fx, y ^ fy, c ^ fc)
            cp = pltpu.make_async_remote_copy(
                src_ref=v_ref, dst_ref=slots.at[me], send_sem=send_sems.at[k - 1], recv_sem=recv_sems.at[k - 1],
                device_id=to, device_id_type=MESH)
            cp.start()
            cps.append(cp)
        for k in range(1, N_DEV):
            fx, fy, fc = (k >> 2) & 1, (k >> 1) & 1, k & 1
            src_dev = 4 * (x ^ fx) + 2 * (y ^ fy) + (c ^ fc)
            pltpu.make_async_remote_copy(
                src_ref=v_ref, dst_ref=slots.at[src_dev], send_sem=send_sems.at[k - 1],
                recv_sem=recv_sems.at[k - 1], device_id=(x, y, c), device_id_type=MESH).wait_recv()
        for cp in cps:
            cp.wait_send()
        acc = slots[0]
        for d in range(1, N_DEV):
            acc = acc + slots[d]
        o_ref[...] = acc

    return pl.pallas_call(
        body, name=name,
        in_specs=[pl.BlockSpec(memory_space=pltpu.VMEM)], out_specs=pl.BlockSpec(memory_space=pltpu.VMEM),
        out_shape=jax.ShapeDtypeStruct((r, w), F32),
        scratch_shapes=[pltpu.VMEM((N_DEV, r, w), F32), pltpu.SemaphoreType.DMA((N_DEV - 1,)),
                        pltpu.SemaphoreType.DMA((N_DEV - 1,))],
        compiler_params=pltpu.CompilerParams(vmem_limit_bytes=VMEM_LIMIT_BYTES),
    )(v)


def _part_rows(shape, part_rows=PACK_PART_ROWS):
    assert shape[-1] <= PACK_LANES
    return _round_up(math.prod(shape[:-1]), part_rows)


def _packed_rows(shapes):
    return _round_up(sum(_part_rows(s) for s in shapes), PACK_ROWS_MULT)


def _pack(arrs, total_rows, dtype, part_rows=PACK_PART_ROWS):
    parts = []
    for a in arrs:
        a2 = a.reshape(-1, a.shape[-1]).astype(dtype)
        rows = _part_rows(a.shape, part_rows)
        parts.append(jnp.pad(a2, ((0, rows - a2.shape[0]), (0, PACK_LANES - a2.shape[1]))))
    used = sum(p.shape[0] for p in parts)
    if total_rows > used:
        parts.append(jnp.zeros((total_rows - used, PACK_LANES), dtype))
    return jnp.concatenate(parts, axis=0)


def _unpack(packed, shapes, part_rows=PACK_PART_ROWS):
    out, r0 = [], 0
    for s in shapes:
        out.append(packed[r0:r0 + math.prod(s[:-1]), :s[-1]].reshape(s))
        r0 += _part_rows(s, part_rows)
    return out


_BIG = (("fox_w_in", 2), ("fox_w_out", 1), ("mla_w_kv_a", 0), ("mla_w_kv_b", 1), ("mla_w_q_a", 1),
        ("mla_w_q_b", 2), ("mla_w_out", 1), ("ffn_w_up", 2), ("ffn_w_down", 1))
_SMALL = ("norm_mix_g", "norm_ffn_g", "fox_b_f", "kv_norm_g", "mla_kv_a_norm_g", "mla_q_a_norm_g", "final_norm_g")
_WEIGHTS = ("norm_mix_g", "norm_ffn_g", "fox_w_in", "fox_b_f", "fox_w_out", "kv_norm_g", "mla_w_kv_a",
            "mla_kv_a_norm_g", "mla_w_kv_b", "mla_w_q_a", "mla_q_a_norm_g", "mla_w_q_b", "mla_w_out",
            "ffn_w_up", "ffn_w_down", "final_norm_g")


def _ffn_fwd(x, h, w_up, w_down, tag):
    def relu_sq(acc):
        r = jnp.maximum(acc, 0.0)
        return r, r * r

    r, a = _matmul(h, w_up, mode="nn", out_dtypes=(BF16, BF16), epilogue=relu_sq, name=f"{tag}_up")
    x_out = _matmul(a, w_down, mode="nn", out_dtypes=(F32,), epilogue=lambda acc, res: (acc + res,),
                    extras=(x,), name=f"{tag}_down")
    return x_out, r, a


def _ffn_bwd(dx_out, x_in, h, r, a, g_norm, w_up, w_down, tag):
    d_u = _matmul(dx_out, w_down, mode="nt", out_dtypes=(BF16,), epilogue=lambda acc, rr: (acc * (2.0 * rr.astype(F32)),),
                  extras=(r,), name=f"{tag}_d_act")
    d_w_down = _matmul(a, dx_out, mode="tn", out_dtypes=(F32,), name=f"{tag}_d_w_down")
    d_w_up = _matmul(h, d_u, mode="tn", out_dtypes=(F32,), name=f"{tag}_d_w_up")
    d_h = _matmul(d_u, w_up, mode="nt", out_dtypes=(F32,), name=f"{tag}_d_h")
    dx_in, (d_g,) = _rms_bwd(x_in, [(g_norm, d_h)], dx_out, name=f"{tag}_d_norm")
    return dx_in, d_w_up, d_w_down, d_g


def kernel(x, norm_mix_g, norm_ffn_g, fox_w_in, fox_b_f, fox_w_out, kv_norm_g, mla_w_kv_a, mla_kv_a_norm_g, mla_w_kv_b, mla_w_q_a, mla_q_a_norm_g, mla_w_q_b, mla_w_out, ffn_w_up, ffn_w_down, final_norm_g, loss_target, m_norm_mix_g, m_norm_ffn_g, m_fox_w_in, m_fox_b_f, m_fox_w_out, m_kv_norm_g, m_mla_w_kv_a, m_mla_kv_a_norm_g, m_mla_w_kv_b, m_mla_w_q_a, m_mla_q_a_norm_g, m_mla_w_q_b, m_mla_w_out, m_ffn_w_up, m_ffn_w_down, m_final_norm_g, v_norm_mix_g, v_norm_ffn_g, v_fox_w_in, v_fox_b_f, v_fox_w_out, v_kv_norm_g, v_mla_w_kv_a, v_mla_kv_a_norm_g, v_mla_w_kv_b, v_mla_w_q_a, v_mla_q_a_norm_g, v_mla_w_q_b, v_mla_w_out, v_ffn_w_up, v_ffn_w_down, v_final_norm_g):
    args = dict(locals())
    w_in = {n: args[n] for n in _WEIGHTS}
    m_in = {n: args["m_" + n] for n in _WEIGHTS}
    v_in = {n: args["v_" + n] for n in _WEIGHTS}

    xs = x[0]
    seq, d_model = xs.shape
    tgt = loss_target[0]
    fox_h, mla_h, nope = FOX_HEADS, MLA_HEADS, QK_NOPE_DIM
    kv_rank = mla_kv_a_norm_g.shape[0]
    rope = mla_w_kv_a.shape[1] - kv_rank
    half = rope // 2
    q_rank = mla_q_a_norm_g.shape[1]
    v_dim = mla_w_kv_b.shape[1] * N_CHIPS // mla_h - nope
    fox_w = fox_w_out.shape[1] * N_CHIPS
    fox_dh = fox_w // fox_h

    big_names = [n for n, _ in _BIG]
    shard_shapes = [w_in[n].shape for n in big_names]
    rows = _packed_rows(shard_shapes)
    my_shard = _pack([w_in[n] for n in big_names], rows, BF16)
    others = _all_gather_shards(my_shard, name="gather_weights")
    by_relation = jnp.concatenate([my_shard[None], others], axis=0)
    p_chip = 2 * lax.axis_index("x") + lax.axis_index("y")
    full = {}
    for q in range(N_CHIPS):
        shard_q = lax.dynamic_index_in_dim(by_relation, p_chip ^ q, axis=0, keepdims=False)
        for (n, ax), piece in zip(_BIG, _unpack(shard_q, shard_shapes)):
            full.setdefault(n, []).append(piece)
    full = {n: jnp.concatenate(full[n], axis=ax) for n, ax in _BIG}

    fox_scale = fox_dh ** -0.5
    fox_wd = _round_up(fox_dh + 9, LANE_TILE)
    fox_vwd = _round_up(fox_dh + 4, LANE_TILE)
    w_fox_in = full["fox_w_in"][0]
    w_fq = _pad_heads(w_fox_in[:, :fox_w] * fox_scale, fox_h, fox_wd, 1)
    w_fk = _pad_heads(w_fox_in[:, fox_w:2 * fox_w], fox_h, fox_wd, 1)
    w_fv = _pad_heads(w_fox_in[:, 2 * fox_w:3 * fox_w], fox_h, fox_vwd, 1)
    w_gate = w_fox_in[:, 3 * fox_w:]
    w_fox_out = _pad_heads(full["fox_w_out"][0], fox_h, fox_vwd, 0)
    n_cx = _round_up(3 * fox_h + 1, LANE_TILE)
    c_piece = lambda i: (lambda hh: 3 * hh + i)
    one_col = 3 * fox_h
    e_fq = _placement(n_cx, fox_h, fox_wd, [(c_piece(i), fox_dh + i, 1.0) for i in range(3)]
                      + [(one_col, fox_dh + 3 + i, 1.0) for i in range(3)])
    e_fk = _placement(n_cx, fox_h, fox_wd, [(one_col, fox_dh + i, 1.0) for i in range(3)]
                      + [(c_piece(i), fox_dh + 3 + i, -1.0) for i in range(3)]
                      + [(one_col, fox_dh + 6 + i, 1.0) for i in range(3)])
    e_fv = _placement(n_cx, fox_h, fox_vwd, [(one_col, fox_dh + i, -1.0) for i in range(3)]
                      + [(one_col, fox_dh + 3, 1.0)])

    mla_scale = (nope + rope) ** -0.5
    mla_dk = nope + rope
    mla_wd = _round_up(mla_dk + 3, LANE_TILE)
    mla_vwd = _round_up(v_dim + 4, LANE_TILE)
    w_kv_a = full["mla_w_kv_a"]
    w_kv_b3 = full["mla_w_kv_b"].reshape(kv_rank, mla_h, nope + v_dim)
    w_kn = _pad_heads(w_kv_b3[:, :, :nope].reshape(kv_rank, -1), mla_h, mla_wd, 1)
    w_mv = _pad_heads(w_kv_b3[:, :, nope:].reshape(kv_rank, -1), mla_h, mla_vwd, 1)
    w_q_a = full["mla_w_q_a"][0]
    w_q_b3 = full["mla_w_q_b"][0].reshape(q_rank, mla_h, nope + rope)
    w_qa_ = _pad_heads(w_q_b3.reshape(q_rank, -1), mla_h, mla_wd, 1)
    w_qb_ = _pad_heads(jnp.concatenate([jnp.zeros_like(w_q_b3[:, :, :nope]), -w_q_b3[:, :, nope + half:],
                                        w_q_b3[:, :, nope:nope + half]], axis=-1).reshape(q_rank, -1),
                       mla_h, mla_wd, 1)
    w_mla_out = _pad_heads(full["mla_w_out"][0], mla_h, mla_vwd, 0)
    w_up, w_down = full["ffn_w_up"], full["ffn_w_down"]
    n_kx = _round_up(rope + 1, LANE_TILE)
    e_mk = _placement(n_kx, mla_h, mla_wd, [(j, nope + j, 1.0) for j in range(rope)]
                      + [(rope, mla_dk + i, 1.0) for i in range(3)])
    e_mv = _placement(n_kx, mla_h, mla_vwd, [(rope, v_dim + i, -1.0) for i in range(3)] + [(rope, v_dim + 3, 1.0)])

    inv = 1.0 / (ROPE_BASE ** (jnp.arange(0, rope, 2, dtype=F32) / rope))
    ang = jnp.arange(seq, dtype=F32)[:, None] * inv[None, :]
    cos, sin = jnp.cos(ang), jnp.sin(ang)
    pad_t = jnp.zeros((seq, mla_wd - mla_dk), F32)
    cos_t = jnp.concatenate([jnp.ones((seq, nope), F32), cos, cos, pad_t], axis=1)
    sin_t = jnp.concatenate([jnp.zeros((seq, nope), F32), sin, sin, pad_t], axis=1)

    (h0,) = _rms_fwd(xs, norm_mix_g[0:1], name="l0_norm_mix")
    gate = _matmul(h0, w_gate, mode="nn", out_dtypes=(F32,), name="fox_gate")
    z = gate + fox_b_f[0][None, :]
    cum = jnp.cumsum(jax.nn.log_sigmoid(z), axis=0)
    cx = jnp.concatenate([_split3(cum).reshape(seq, 3 * fox_h), jnp.ones((seq, 1), BF16),
                          jnp.zeros((seq, n_cx - 3 * fox_h - 1), BF16)], axis=1)
    fqa = _matmul(h0, w_fq, mode="nn", out_dtypes=(BF16,), placed=(cx, e_fq), name="fox_q")
    fka = _matmul(h0, w_fk, mode="nn", out_dtypes=(BF16,), placed=(cx, e_fk), name="fox_k")
    fva = _matmul(h0, w_fv, mode="nn", out_dtypes=(BF16,), placed=(cx, e_fv), name="fox_v")
    foa, fqb = _flash_fwd(fqa, fka, fva, fox_h, fox_dh + 3, fox_dh + 6, FOX_FWD_SUB_ROWS, name="fox_attn")
    add_res = lambda acc, res: (acc + res,)
    x1 = _matmul(foa, w_fox_out, mode="nn", out_dtypes=(F32,), epilogue=add_res, extras=(xs,), name="fox_out")
    (h1,) = _rms_fwd(x1, norm_ffn_g[0:1], name="l0_norm_ffn")
    x2, r0, a0 = _ffn_fwd(x1, h1, w_up[0], w_down[0], "ffn0")

    src, h2 = _rms_fwd(x2, jnp.stack([kv_norm_g, norm_mix_g[1]]), name="l1_norm_kv_mix")
    kv_a = _matmul(src, w_kv_a, mode="nn", out_dtypes=(F32,), name="mla_kv_a")
    (c_kv,) = _rms_fwd(kv_a, mla_kv_a_norm_g[None, :], name="mla_norm_kv_a")
    kr1, kr2 = _rope(kv_a[None, :, kv_rank:kv_rank + half], kv_a[None, :, kv_rank + half:], cos, sin, 1.0,
                     name="mla_rope_k")
    krx = jnp.concatenate([kr1.astype(BF16), kr2.astype(BF16), jnp.ones((seq, 1), BF16),
                           jnp.zeros((seq, n_kx - rope - 1), BF16)], axis=1)
    mka = _matmul(c_kv, w_kn, mode="nn", out_dtypes=(BF16,), placed=(krx, e_mk), name="mla_k")
    mva = _matmul(c_kv, w_mv, mode="nn", out_dtypes=(BF16,), placed=(krx, e_mv), name="mla_v")
    cq_pre = _matmul(h2, w_q_a, mode="nn", out_dtypes=(F32,), name="mla_q_a")
    (c_q,) = _rms_fwd(cq_pre, mla_q_a_norm_g, name="mla_norm_q_a")
    q_a_part = _matmul(c_q, w_qa_, mode="nn", out_dtypes=(F32,), name="mla_q_b_cos")
    q_b_part = _matmul(c_q, w_qb_, mode="nn", out_dtypes=(F32,), name="mla_q_b_sin")
    mqa = _rope_mix(q_a_part, q_b_part, cos_t, sin_t, mla_scale, mla_h, name="mla_rope_q")
    moa, mqb = _flash_fwd(mqa, mka, mva, mla_h, v_dim + 3, mla_dk, MLA_FWD_SUB_ROWS, name="mla_attn")
    x3 = _matmul(moa, w_mla_out, mode="nn", out_dtypes=(F32,), epilogue=add_res, extras=(x2,), name="mla_out")
    (h3,) = _rms_fwd(x3, norm_ffn_g[1:2], name="l1_norm_ffn")
    x4, r1, a1 = _ffn_fwd(x3, h3, w_up[1], w_down[1], "ffn1")

    loss_tile, dx4, d_final_g = _loss_head(x4, final_norm_g[None, :], tgt, name="loss_head")
    loss = lax.psum(loss_tile[0, 0], ("x", "y", "c"))

    gw = {}
    dx3, d_up1, d_down1, d_nf1 = _ffn_bwd(dx4, x3, h3, r1, a1, norm_ffn_g[1:2], w_up[1], w_down[1], "ffn1")

    d_mo = _matmul(dx3, w_mla_out, mode="nt", out_dtypes=(BF16,), name="mla_d_ctx")
    gw["mla_w_out"] = _unpad_heads(_matmul(moa, dx3, mode="tn", out_dtypes=(F32,), name="mla_d_w_out"),
                                   mla_h, v_dim, 0)[None]
    d_moa = _delta_place(d_mo, moa, mla_h, v_dim, name="mla_attn_delta")
    d_mqa, d_mka, d_mva = _flash_bwd(mqb, mka, mva, d_moa, mla_h, MLA_BWD_HEADS_PER_STEP,
                                     name="mla_attn_bwd")
    d_qa_part, d_qb_part = _rope_unmix(d_mqa, cos_t, sin_t, mla_scale, mla_h, name="mla_rope_dq")
    d_w_qa_ = _unpad_heads(_matmul(c_q, d_qa_part, mode="tn", out_dtypes=(F32,), name="mla_d_w_q_b_cos"),
                           mla_h, mla_dk, 1).reshape(q_rank, mla_h, mla_dk)
    d_w_qb_ = _unpad_heads(_matmul(c_q, d_qb_part, mode="tn", out_dtypes=(F32,), name="mla_d_w_q_b_sin"),
                           mla_h, mla_dk, 1).reshape(q_rank, mla_h, mla_dk)
    gw["mla_w_q_b"] = jnp.concatenate(
        [d_w_qa_[:, :, :nope], d_w_qa_[:, :, nope:nope + half] + d_w_qb_[:, :, nope + half:],
         d_w_qa_[:, :, nope + half:] - d_w_qb_[:, :, nope:nope + half]], axis=-1).reshape(1, q_rank, mla_h * mla_dk)
    d_c_q_sin = _matmul(d_qb_part, w_qb_, mode="nt", out_dtypes=(F32,), name="mla_d_c_q_sin")
    d_c_q = _matmul(d_qa_part, w_qa_, mode="nt", out_dtypes=(F32,), epilogue=add_res, extras=(d_c_q_sin,),
                    name="mla_d_c_q")
    d_cq_pre, (d_q_a_g,) = _rms_bwd(cq_pre, [(mla_q_a_norm_g, d_c_q)], None, name="mla_d_norm_q_a")
    gw["mla_w_q_a"] = _matmul(h2, d_cq_pre, mode="tn", out_dtypes=(F32,), name="mla_d_w_q_a")[None]
    d_h2 = _matmul(d_cq_pre, w_q_a, mode="nt", out_dtypes=(F32,), name="mla_d_h")

    d_w_kn = _unpad_heads(_matmul(c_kv, d_mka, mode="tn", out_dtypes=(F32,), name="mla_d_w_k"), mla_h, nope, 1)
    d_w_mv = _unpad_heads(_matmul(c_kv, d_mva, mode="tn", out_dtypes=(F32,), name="mla_d_w_v"), mla_h, v_dim, 1)
    gw["mla_w_kv_b"] = jnp.concatenate([d_w_kn.reshape(kv_rank, mla_h, nope), d_w_mv.reshape(kv_rank, mla_h, v_dim)],
                                       axis=-1).reshape(kv_rank, mla_h * (nope + v_dim))
    d_c_kv_v = _matmul(d_mva, w_mv, mode="nt", out_dtypes=(F32,), name="mla_d_c_kv_v")
    d_c_kv = _matmul(d_mka, w_kn, mode="nt", out_dtypes=(F32,), epilogue=add_res, extras=(d_c_kv_v,),
                     name="mla_d_c_kv")
    d_ckv_pre, (d_kv_a_g,) = _rms_bwd(kv_a, [(mla_kv_a_norm_g[None, :], d_c_kv)], None, name="mla_d_norm_kv_a")
    d_mk_rope = jnp.transpose(d_mka.reshape(seq, mla_h, mla_wd)[:, :, nope:mla_dk], (1, 0, 2))
    d_kr1, d_kr2 = _rope(d_mk_rope[:, :, :half], d_mk_rope[:, :, half:], cos, sin, -1.0, name="mla_rope_dk")
    d_kv_a = jnp.concatenate([d_ckv_pre, d_kr1, d_kr2], axis=1)
    gw["mla_w_kv_a"] = _matmul(src, d_kv_a, mode="tn", out_dtypes=(F32,), name="mla_d_w_kv_a")
    d_src = _matmul(d_kv_a, w_kv_a, mode="nt", out_dtypes=(F32,), name="mla_d_src")
    dx2, (d_kv_g, d_nm1) = _rms_bwd(x2, [(kv_norm_g[None, :], d_src), (norm_mix_g[1:2], d_h2)], dx3,
                                    name="l1_d_norm_kv_mix")

    dx1, d_up0, d_down0, d_nf0 = _ffn_bwd(dx2, x1, h1, r0, a0, norm_ffn_g[0:1], w_up[0], w_down[0], "ffn0")
    gw["ffn_w_up"] = jnp.stack([d_up0, d_up1])
    gw["ffn_w_down"] = jnp.stack([d_down0, d_down1])

    d_fo = _matmul(dx1, w_fox_out, mode="nt", out_dtypes=(BF16,), name="fox_d_ctx")
    gw["fox_w_out"] = _unpad_heads(_matmul(foa, dx1, mode="tn", out_dtypes=(F32,), name="fox_d_w_out"),
                                   fox_h, fox_dh, 0)[None]
    d_foa = _delta_place(d_fo, foa, fox_h, fox_dh, name="fox_attn_delta")
    d_fqa, d_fka, d_fva = _flash_bwd(fqb, fka, fva, d_foa, fox_h, FOX_BWD_HEADS_PER_STEP, name="fox_attn_bwd")
    d_cum = (d_fqa.reshape(seq, fox_h, fox_wd)[:, :, fox_dh]
             - d_fka.reshape(seq, fox_h, fox_wd)[:, :, fox_dh + 3])
    d_z = jnp.flip(jnp.cumsum(jnp.flip(d_cum, 0), axis=0), 0) * jax.nn.sigmoid(-z)
    d_b_f = jnp.sum(d_z, axis=0)
    d_w_in = [_unpad_heads(_matmul(h0, g, mode="tn", out_dtypes=(F32,), name=f"fox_d_w_{tag}"), fox_h, fox_dh, 1)
              for tag, g in (("q", d_fqa), ("k", d_fka), ("v", d_fva))]
    d_w_gate = _matmul(h0, d_z, mode="tn", out_dtypes=(F32,), name="fox_d_w_gate")
    gw["fox_w_in"] = jnp.concatenate([d_w_in[0] * fox_scale, d_w_in[1], d_w_in[2], d_w_gate], axis=1)[None]
    d_h0 = _matmul(d_z, w_gate, mode="nt", out_dtypes=(F32,), name="fox_d_h_gate")
    for tag, g, w in (("q", d_fqa, w_fq), ("k", d_fka, w_fk), ("v", d_fva, w_fv)):
        d_h0 = _matmul(g, w, mode="nt", out_dtypes=(F32,), epilogue=add_res, extras=(d_h0,), name=f"fox_d_h_{tag}")
    grad_x, (d_nm0,) = _rms_bwd(xs, [(norm_mix_g[0:1], d_h0)], dx1, name="l0_d_norm_mix")

    c_idx = lax.axis_index("c").astype(jnp.int32).reshape(1)
    per_chip = []
    for q in range(N_CHIPS):
        pieces = [jnp.split(gw[n], N_CHIPS, axis=ax)[q] for n, ax in _BIG]
        per_chip.append(_pack(pieces, rows, F32))
    g4 = jnp.stack(per_chip)
    a4 = _sibling_swap_halves(g4, name="grads_to_sibling")
    s4 = _chip_sum(g4, a4, c_idx, name="grads_chip_sum")
    b3 = _chip_exchange(s4, name="grads_between_chips")
    t_mine = _sum_chips(s4, b3, p_chip.astype(jnp.int32).reshape(1), name="grads_sum_chips")
    t_theirs = _sibling_swap(t_mine, name="grads_join_halves")
    is_south = lax.axis_index("c") == 0
    g_big = jnp.concatenate([jnp.where(is_south, t_mine, t_theirs), jnp.where(is_south, t_theirs, t_mine)],
                            axis=0)

    small_local = {"norm_mix_g": jnp.concatenate([d_nm0, d_nm1], axis=0),
                   "norm_ffn_g": jnp.concatenate([d_nf0, d_nf1], axis=0),
                   "fox_b_f": d_b_f[None, :], "kv_norm_g": d_kv_g[0], "mla_kv_a_norm_g": d_kv_a_g[0],
                   "mla_q_a_norm_g": d_q_a_g, "final_norm_g": d_final_g[0]}
    small_shapes = [w_in[n].shape for n in _SMALL]
    small_rows = sum(_part_rows(s, SMALL_PART_ROWS) for s in small_shapes)
    pack_small = lambda arrs: _pack(arrs, small_rows, F32, SMALL_PART_ROWS)
    g_small = _all_reduce_small(pack_small([small_local[n] for n in _SMALL]), name="grads_small")

    d_big, nm_big, nv_big = _adamw(_pack([w_in[n] for n in big_names], rows, F32), g_big,
                                   _pack([m_in[n] for n in big_names], rows, F32),
                                   _pack([v_in[n] for n in big_names], rows, F32), name="adamw_big")
    d_sm, nm_sm, nv_sm = _adamw(pack_small([w_in[n] for n in _SMALL]), g_small,
                                pack_small([m_in[n] for n in _SMALL]),
                                pack_small([v_in[n] for n in _SMALL]), name="adamw_small")

    def spread(big, small):
        out = dict(zip(big_names, _unpack(big, shard_shapes)))
        out.update(zip(_SMALL, _unpack(small, small_shapes, SMALL_PART_ROWS)))
        return [out[n] for n in _WEIGHTS]

    return (loss, grad_x[None], *spread(g_big, g_small), *spread(d_big, d_sm), *spread(nm_big, nm_sm),
            *spread(nv_big, nv_sm))
```

```python
import math

import numpy as np
import jax
import jax.numpy as jnp
from jax import lax
from jax.experimental import pallas as pl
from jax.experimental.pallas import tpu as pltpu

F32 = jnp.float32
BF16 = jnp.bfloat16

FOX_HEADS = 16
MLA_HEADS = 8
QK_NOPE_DIM = 128
ROPE_BASE = 10000.0
EPS = 1e-6

ADAM_LR = 0.001
ADAM_B1 = 0.9
ADAM_B2 = 0.999
ADAM_EPS = 1e-08
ADAM_WD = 0.01
ADAM_STEP = 10

N_CHIPS = 4
N_DEV = 8
PACK_LANES = 1024
PACK_PART_ROWS = 16
SMALL_PART_ROWS = 8
PACK_ROWS_MULT = 1024
VMEM_LIMIT_BYTES = 48 * 1024 * 1024
LANE_TILE = 128
MATMUL_BLOCK = 1024
ATTN_BLOCK_Q = 1024
ATTN_BLOCK_K = 1024
ATTN_FWD_LANES = 1024
FOX_BWD_HEADS_PER_STEP = 2
MLA_BWD_HEADS_PER_STEP = 1
ATTN_SUB_ROWS = 256
FOX_FWD_SUB_ROWS = (1024, 512)
MLA_FWD_SUB_ROWS = (256, 256)
NEG_BIG = -1e30
MESH = pl.DeviceIdType.MESH


def _round_up(n, m):
    return -(-n // m) * m


def _blk(dim, pref, mult=128):
    if dim <= pref:
        return dim
    b = (pref // mult) * mult
    while b >= mult:
        if dim % b == 0:
            return b
        b -= mult
    return dim


def _params(sem=None):
    return pltpu.CompilerParams(dimension_semantics=sem, vmem_limit_bytes=VMEM_LIMIT_BYTES)


_DIMS = {"nn": (((1,), (0,)), ((), ())), "nt": (((1,), (1,)), ((), ())), "tn": (((0,), (0,)), ((), ()))}


def _matmul(a, b, *, mode, out_dtypes, name, epilogue=None, extras=(), placed=None):
    if mode == "tn":
        kdim, m = a.shape
    else:
        m, kdim = a.shape
    n = b.shape[0] if mode == "nt" else b.shape[1]
    bm, bn, bk = _blk(m, MATMUL_BLOCK), _blk(n, MATMUL_BLOCK), _blk(kdim, MATMUL_BLOCK)
    nk = kdim // bk
    n_extra, n_out = len(extras), len(out_dtypes)
    n_placed = 0 if placed is None else 2
    dims = _DIMS[mode]

    def body(a_ref, b_ref, *rest):
        placed_refs = rest[:n_placed]
        rest = rest[n_placed:]
        extra_refs = rest[:n_extra]
        out_refs = rest[n_extra:n_extra + n_out]

        def finish(acc):
            if n_placed:
                acc = acc + lax.dot_general(placed_refs[0][...], placed_refs[1][...], _DIMS["nn"],
                                            preferred_element_type=F32)
            res = (acc,) if epilogue is None else epilogue(acc, *[r[...] for r in extra_refs])
            for o_ref, r in zip(out_refs, res):
                o_ref[...] = r.astype(o_ref.dtype)

        part = lax.dot_general(a_ref[...].astype(BF16), b_ref[...].astype(BF16), dims, preferred_element_type=F32)
        if nk == 1:
            finish(part)
            return
        acc_ref = rest[n_extra + n_out]
        k = pl.program_id(2)

        @pl.when(k == 0)
        def _():
            acc_ref[...] = part

        @pl.when((k > 0) & (k < nk - 1))
        def _():
            acc_ref[...] += part

        @pl.when(k == nk - 1)
        def _():
            finish(acc_ref[...] + part)

    if mode == "tn":
        a_spec = pl.BlockSpec((bk, bm), lambda i, j, k: (k, i))
    else:
        a_spec = pl.BlockSpec((bm, bk), lambda i, j, k: (i, k))
    if mode == "nt":
        b_spec = pl.BlockSpec((bn, bk), lambda i, j, k: (j, k))
    else:
        b_spec = pl.BlockSpec((bk, bn), lambda i, j, k: (k, j))
    tile = pl.BlockSpec((bm, bn), lambda i, j, k: (i, j))
    placed_specs = []
    if n_placed:
        k2 = placed[0].shape[1]
        placed_specs = [pl.BlockSpec((bm, k2), lambda i, j, k: (i, 0)), pl.BlockSpec((k2, bn), lambda i, j, k: (0, j))]
    outs = pl.pallas_call(
        body, name=name,
        grid=(m // bm, n // bn, nk),
        in_specs=[a_spec, b_spec] + placed_specs + [tile] * n_extra,
        out_specs=[tile] * n_out,
        out_shape=[jax.ShapeDtypeStruct((m, n), dt) for dt in out_dtypes],
        scratch_shapes=[pltpu.VMEM((bm, bn), F32)] if nk > 1 else [],
        compiler_params=_params(("parallel", "parallel", "arbitrary")),
    )(a, b, *(placed or ()), *extras)
    return outs[0] if n_out == 1 else outs


def _rms_fwd(x, gains, name):
    s = x.shape[0]
    g, w = gains.shape
    bs = _blk(s, 512, 8)

    def body(x_ref, g_ref, *out_refs):
        xv = x_ref[...]
        y = xv * lax.rsqrt(jnp.mean(xv * xv, axis=-1, keepdims=True) + EPS)
        for i, o_ref in enumerate(out_refs):
            o_ref[...] = (y * g_ref[i:i + 1, :]).astype(o_ref.dtype)

    row = pl.BlockSpec((bs, w), lambda i: (i, 0))
    outs = pl.pallas_call(
        body, name=name, grid=(s // bs,),
        in_specs=[row, pl.BlockSpec((g, w), lambda i: (0, 0))],
        out_specs=[row] * g,
        out_shape=[jax.ShapeDtypeStruct((s, w), BF16)] * g,
        compiler_params=_params(("parallel",)),
    )(x, gains)
    return outs


def _rms_bwd(x, branches, resid, name):
    s = x.shape[0]
    w = branches[0][0].shape[1]
    nb = len(branches)
    bs = _blk(s, 512, 8)
    has_resid = resid is not None

    def body(x_ref, *rest):
        g_refs = rest[:nb]
        dy_refs = rest[nb:2 * nb]
        pos = 2 * nb
        r_ref = rest[pos] if has_resid else None
        pos += int(has_resid)
        dx_ref = rest[pos]
        dg_refs = rest[pos + 1:pos + 1 + nb]
        i = pl.program_id(0)

        @pl.when(i == 0)
        def _():
            for dg_ref in dg_refs:
                dg_ref[...] = jnp.zeros_like(dg_ref)

        xv = x_ref[...]
        rstd = lax.rsqrt(jnp.mean(xv * xv, axis=-1, keepdims=True) + EPS)
        xhat = xv * rstd
        dx = r_ref[...] if has_resid else jnp.zeros_like(xv)
        for g_ref, dy_ref, dg_ref in zip(g_refs, dy_refs, dg_refs):
            dy = dy_ref[...].astype(F32)
            dyg = dy * g_ref[...]
            dx = dx + rstd * (dyg - xhat * jnp.mean(dyg * xhat, axis=-1, keepdims=True))
            dg_ref[...] += jnp.sum(dy * xhat, axis=0, keepdims=True)
        dx_ref[...] = dx

    row = pl.BlockSpec((bs, w), lambda i: (i, 0))
    vec = pl.BlockSpec((1, w), lambda i: (0, 0))
    args = [x] + [g for g, _ in branches] + [dy for _, dy in branches] + ([resid] if has_resid else [])
    outs = pl.pallas_call(
        body, name=name, grid=(s // bs,),
        in_specs=[row] + [vec] * nb + [row] * nb + ([row] if has_resid else []),
        out_specs=[row] + [vec] * nb,
        out_shape=[jax.ShapeDtypeStruct((s, w), F32)] + [jax.ShapeDtypeStruct((1, w), F32)] * nb,
        compiler_params=_params(("arbitrary",)),
    )(*args)
    return outs[0], list(outs[1:])


def _loss_head(x, g, target, name):
    s, w = x.shape
    bs = _blk(s, 512, 8)

    def body(x_ref, g_ref, t_ref, loss_ref, dx_ref, dg_ref):
        i = pl.program_id(0)

        @pl.when(i == 0)
        def _():
            loss_ref[...] = jnp.zeros_like(loss_ref)
            dg_ref[...] = jnp.zeros_like(dg_ref)

        xv = x_ref[...]
        gv = g_ref[...]
        rstd = lax.rsqrt(jnp.mean(xv * xv, axis=-1, keepdims=True) + EPS)
        xhat = xv * rstd
        err = xhat * gv - t_ref[...]
        loss_ref[...] += 0.5 * jnp.sum(jnp.mean(err * err, axis=-1, keepdims=True))
        dy = err * (1.0 / w)
        dyg = dy * gv
        dx_ref[...] = rstd * (dyg - xhat * jnp.mean(dyg * xhat, axis=-1, keepdims=True))
        dg_ref[...] += jnp.sum(dy * xhat, axis=0, keepdims=True)

    row = pl.BlockSpec((bs, w), lambda i: (i, 0))
    vec = pl.BlockSpec((1, w), lambda i: (0, 0))
    return pl.pallas_call(
        body, name=name, grid=(s // bs,),
        in_specs=[row, vec, row],
        out_specs=[pl.BlockSpec((8, 128), lambda i: (0, 0)), row, vec],
        out_shape=[jax.ShapeDtypeStruct((8, 128), F32), jax.ShapeDtypeStruct((s, w), F32),
                   jax.ShapeDtypeStruct((1, w), F32)],
        compiler_params=_params(("arbitrary",)),
    )(x, g, target)


def _rope(a, b, cos, sin, sign, name):
    g, s, w = a.shape
    bs = _blk(s, 1024, 8)

    def body(a_ref, b_ref, c_ref, s_ref, o1_ref, o2_ref):
        av = jnp.sum(a_ref[...].astype(F32), axis=0)
        bv = jnp.sum(b_ref[...].astype(F32), axis=0)
        cv, sv = c_ref[...], s_ref[...] * sign
        o1_ref[...] = av * cv - bv * sv
        o2_ref[...] = bv * cv + av * sv

    grp = pl.BlockSpec((g, bs, w), lambda i: (0, i, 0))
    row = pl.BlockSpec((bs, w), lambda i: (i, 0))
    return pl.pallas_call(
        body, name=name, grid=(s // bs,),
        in_specs=[grp, grp, row, row], out_specs=[row, row],
        out_shape=[jax.ShapeDtypeStruct((s, w), F32)] * 2,
        compiler_params=_params(("parallel",)),
    )(a, b, cos, sin)


def _causal_table(s, bq, bk, q_major):
    nq, nk = s // bq, s // bk
    rows = []
    if q_major:
        for qi in range(nq):
            kmax = (qi * bq + bq - 1) // bk
            for ki in range(kmax + 1):
                rows.append((qi, ki, int(ki * bk + bk - 1 > qi * bq), int(ki == 0), int(ki == kmax)))
    else:
        for ki in range(nk):
            qmin = (ki * bk) // bq
            for qi in range(qmin, nq):
                rows.append((qi, ki, int(ki * bk + bk - 1 > qi * bq), int(qi == qmin), int(qi == nq - 1)))
    return jnp.asarray(np.array(rows, np.int32).T)


def _causal_keep(q0, k0, nq, nk, transposed):
    if transposed:
        kpos = k0 + lax.broadcasted_iota(jnp.int32, (nk, nq), 0)
        qpos = q0 + lax.broadcasted_iota(jnp.int32, (nk, nq), 1)
    else:
        qpos = q0 + lax.broadcasted_iota(jnp.int32, (nq, nk), 0)
        kpos = k0 + lax.broadcasted_iota(jnp.int32, (nq, nk), 1)
    return kpos <= qpos


def _sub_tiles(n_rows, n_cols, masked, square, rows_are_keys, sub_rows):
    sub = min(sub_rows, n_rows)
    out = []
    for r0 in range(0, n_rows, sub):
        if masked and square:
            c0, nc = (r0, n_cols - r0) if rows_are_keys else (0, r0 + sub)
        else:
            c0, nc = 0, n_cols
        out.append((r0, sub, c0, nc))
    return out


_NT = (((1,), (1,)), ((), ()))
_NN = (((1,), (0,)), ((), ()))


def _attn_specs(bq, bk):
    qspec = lambda d: pl.BlockSpec((bq, d), lambda hh, t, tb: (tb[0, t], hh))
    kspec = lambda d: pl.BlockSpec((bk, d), lambda hh, t, tb: (tb[1, t], hh))
    return qspec, kspec


def _split3_cols(x):
    hi = x.astype(BF16).astype(F32)
    rest = x - hi
    mid = rest.astype(BF16).astype(F32)
    lo = (rest - mid).astype(BF16).astype(F32)
    return hi, mid, lo


def _place3(base, col, pieces, sign):
    lane = lax.broadcasted_iota(jnp.int32, base.shape, 1)
    out = base.astype(F32)
    for i, piece in enumerate(pieces):
        out = jnp.where(lane == col + i, sign * piece, out)
    return out.astype(BF16)


def _flash_fwd(qa, ka, va, heads, l_col, lse_col, sub_rows, name):
    s = qa.shape[0]
    da, dv = qa.shape[1] // heads, va.shape[1] // heads
    hps = max(n for n in range(1, ATTN_FWD_LANES // max(da, dv) + 1) if heads % n == 0)
    bq, bk = _blk(s, ATTN_BLOCK_Q), _blk(s, ATTN_BLOCK_K)
    tab = _causal_table(s, bq, bk, True)

    def body(tab_ref, q_ref, k_ref, v_ref, o_ref, qb_ref, m_sc, acc_sc):
        t = pl.program_id(1)
        qi, ki = tab_ref[0, t], tab_ref[1, t]

        @pl.when(tab_ref[3, t] == 1)
        def _():
            m_sc[...] = jnp.full_like(m_sc, NEG_BIG)
            acc_sc[...] = jnp.zeros_like(acc_sc)

        def step(masked):
            for hh in range(hps):
                qc, vc = slice(hh * da, (hh + 1) * da), slice(hh * dv, (hh + 1) * dv)
                for r0, nr, c0, nc in _sub_tiles(bq, bk, masked, bq == bk, False, sub_rows[int(masked)]):
                    sc = lax.dot_general(q_ref[r0:r0 + nr, qc], k_ref[c0:c0 + nc, qc], _NT,
                                         preferred_element_type=F32)
                    if masked:
                        sc = jnp.where(_causal_keep(qi * bq + r0, ki * bk + c0, nr, nc, False), sc, NEG_BIG)
                    m_prev = m_sc[hh, r0:r0 + nr, :]
                    m_new = jnp.maximum(m_prev, jnp.max(sc, axis=-1, keepdims=True))
                    p = jnp.exp(sc - m_new).astype(BF16)
                    acc_sc[r0:r0 + nr, vc] = jnp.exp(m_prev - m_new) * acc_sc[r0:r0 + nr, vc] + lax.dot_general(
                        p, v_ref[c0:c0 + nc, vc], _NN, preferred_element_type=F32)
                    m_sc[hh, r0:r0 + nr, :] = m_new

        @pl.when(tab_ref[2, t] == 1)
        def _():
            step(True)

        @pl.when(tab_ref[2, t] == 0)
        def _():
            step(False)

        @pl.when(tab_ref[4, t] == 1)
        def _():
            for hh in range(hps):
                qc, vc = slice(hh * da, (hh + 1) * da), slice(hh * dv, (hh + 1) * dv)
                acc = acc_sc[:, vc]
                lane = lax.broadcasted_iota(jnp.int32, acc.shape, 1)
                l = jnp.sum(jnp.where(lane == l_col, acc, 0.0), axis=-1, keepdims=True)
                o_ref[:, vc] = (acc / l).astype(o_ref.dtype)
                lse = m_sc[hh] + jnp.log(l)
                qb_ref[:, qc] = _place3(q_ref[:, qc], lse_col, _split3_cols(lse), -1.0)

    qspec, kspec = _attn_specs(bq, bk)
    return pl.pallas_call(
        body, name=name,
        grid_spec=pltpu.PrefetchScalarGridSpec(
            num_scalar_prefetch=1, grid=(heads // hps, tab.shape[1]),
            in_specs=[qspec(hps * da), kspec(hps * da), kspec(hps * dv)],
            out_specs=[qspec(hps * dv), qspec(hps * da)],
            scratch_shapes=[pltpu.VMEM((hps, bq, 1), F32), pltpu.VMEM((bq, hps * dv), F32)]),
        out_shape=[jax.ShapeDtypeStruct((s, heads * dv), BF16), jax.ShapeDtypeStruct((s, heads * da), BF16)],
        compiler_params=_params(("parallel", "arbitrary")),
    )(tab, qa, ka, va)


def _delta_place(do, o, heads, delta_col, name):
    s = o.shape[0]
    dv = o.shape[1] // heads
    bs = _blk(s, 512, 8)
    hpb = max(1, min(heads, 1024 // dv))
    while heads % hpb:
        hpb -= 1

    def body(do_ref, o_ref, out_ref):
        for hh in range(hpb):
            vc = slice(hh * dv, (hh + 1) * dv)
            dov = do_ref[:, vc]
            delta = jnp.sum(dov.astype(F32) * o_ref[:, vc].astype(F32), axis=-1, keepdims=True)
            out_ref[:, vc] = _place3(dov, delta_col, _split3_cols(delta), 1.0)

    blk = pl.BlockSpec((bs, hpb * dv), lambda i, hh: (i, hh))
    return pl.pallas_call(
        body, name=name, grid=(s // bs, heads // hpb), in_specs=[blk, blk], out_specs=blk,
        out_shape=jax.ShapeDtypeStruct(do.shape, BF16),
        compiler_params=_params(("parallel", "parallel")),
    )(do, o)


_TN =(((0,), (0,)), ((), ()))


def _flash_bwd(qa, ka, va, doa, heads, hps, name):
    s = qa.shape[0]
    da, dv = qa.shape[1] // heads, va.shape[1] // heads
    h = heads // hps
    bq, bk = _blk(s, ATTN_BLOCK_Q), _blk(s, ATTN_BLOCK_K)
    tab = _causal_table(s, bq, bk, False)
    n_tiles = tab.shape[1]

    def body(tab_ref, q_ref, k_ref, v_ref, do_ref, dq_ref, dk_ref, dv_ref, dk_sc, dv_sc):
        t = pl.program_id(1)
        qi, ki = tab_ref[0, t], tab_ref[1, t]

        @pl.when(t == 0)
        def _():
            dq_ref[...] = jnp.zeros_like(dq_ref)

        @pl.when(tab_ref[3, t] == 1)
        def _():
            dk_sc[...] = jnp.zeros_like(dk_sc)
            dv_sc[...] = jnp.zeros_like(dv_sc)

        def step(masked):
            for hh in range(hps):
                qc, vc = slice(hh * da, (hh + 1) * da), slice(hh * dv, (hh + 1) * dv)
                for r0, nr, c0, nc in _sub_tiles(bk, bq, masked, bq == bk, True, ATTN_SUB_ROWS):
                    qv, dov, kv = q_ref[c0:c0 + nc, qc], do_ref[c0:c0 + nc, vc], k_ref[r0:r0 + nr, qc]
                    st = lax.dot_general(kv, qv, _NT, preferred_element_type=F32)
                    if masked:
                        st = jnp.where(_causal_keep(qi * bq + c0, ki * bk + r0, nc, nr, True), st, NEG_BIG)
                    pt = jnp.exp(st)
                    dv_sc[r0:r0 + nr, vc] += lax.dot_general(pt.astype(BF16), dov, _NN, preferred_element_type=F32)
                    dpt = lax.dot_general(v_ref[r0:r0 + nr, vc], dov, _NT, preferred_element_type=F32)
                    dst = (pt * dpt).astype(BF16)
                    dk_sc[r0:r0 + nr, qc] += lax.dot_general(dst, qv, _NN, preferred_element_type=F32)
                    q_rows = pl.ds(pl.multiple_of(qi * bq + c0, ATTN_SUB_ROWS), nc)
                    dq_ref[q_rows, qc] += lax.dot_general(dst, kv, _TN, preferred_element_type=F32)

        @pl.when(tab_ref[2, t] == 1)
        def _():
            step(True)

        @pl.when(tab_ref[2, t] == 0)
        def _():
            step(False)

        @pl.when(tab_ref[4, t] == 1)
        def _():
            dk_ref[...] = dk_sc[...]
            dv_ref[...] = dv_sc[...]

    qspec, kspec = _attn_specs(bq, bk)
    return pl.pallas_call(
        body, name=name,
        grid_spec=pltpu.PrefetchScalarGridSpec(
            num_scalar_prefetch=1, grid=(h, n_tiles),
            in_specs=[qspec(hps * da), kspec(hps * da), kspec(hps * dv), qspec(hps * dv)],
            out_specs=[pl.BlockSpec((s, hps * da), lambda hh, t, tb: (0, hh)), kspec(hps * da), kspec(hps * dv)],
            scratch_shapes=[pltpu.VMEM((bk, hps * da), F32), pltpu.VMEM((bk, hps * dv), F32)]),
        out_shape=[jax.ShapeDtypeStruct((s, heads * da), F32), jax.ShapeDtypeStruct((s, heads * da), F32),
                   jax.ShapeDtypeStruct((s, heads * dv), F32)],
        compiler_params=_params(("parallel", "arbitrary")),
    )(tab, qa, ka, va, doa)


def _split3(x):
    hi = lax.reduce_precision(x, 8, 7)
    rest = x - hi
    mid = lax.reduce_precision(rest, 8, 7)
    lo = lax.reduce_precision(rest - mid, 8, 7)
    return jnp.stack([hi, mid, lo], axis=-1).astype(BF16)


def _pad_heads(w, heads, width, axis):
    shape = list(w.shape)
    d = shape[axis] // heads
    w = w.reshape(shape[:axis] + [heads, d] + shape[axis + 1:])
    pad = [(0, 0)] * w.ndim
    pad[axis + 1] = (0, width - d)
    return jnp.pad(w, pad).reshape(shape[:axis] + [heads * width] + shape[axis + 1:])


def _unpad_heads(w, heads, d, axis):
    shape = list(w.shape)
    width = shape[axis] // heads
    w = w.reshape(shape[:axis] + [heads, width] + shape[axis + 1:])
    w = lax.slice_in_dim(w, 0, d, axis=axis + 1)
    return w.reshape(shape[:axis] + [heads * d] + shape[axis + 1:])


def _placement(rows, heads, width, entries):
    e = np.zeros((rows, heads * width), np.float32)
    for row, col, val in entries:
        for hh in range(heads):
            e[row(hh) if callable(row) else row, hh * width + col] = val
    return jnp.asarray(e, BF16)


def _rope_mix(a, b, cos_t, sin_t, scale, heads, name):
    s = a.shape[0]
    d = a.shape[1] // heads
    bs = _blk(s, 1024, 8)

    def body(a_ref, b_ref, c_ref, s_ref, o_ref):
        o_ref[...] = ((a_ref[...] * c_ref[...] + b_ref[...] * s_ref[...]) * scale).astype(o_ref.dtype)

    blk = pl.BlockSpec((bs, d), lambda i, hh: (i, hh))
    tbl = pl.BlockSpec((bs, d), lambda i, hh: (i, 0))
    return pl.pallas_call(
        body, name=name, grid=(s // bs, heads), in_specs=[blk, blk, tbl, tbl], out_specs=blk,
        out_shape=jax.ShapeDtypeStruct(a.shape, BF16),
        compiler_params=_params(("parallel", "parallel")),
    )(a, b, cos_t, sin_t)


def _rope_unmix(g, cos_t, sin_t, scale, heads, name):
    s = g.shape[0]
    d = g.shape[1] // heads
    bs = _blk(s, 1024, 8)

    def body(g_ref, c_ref, s_ref, da_ref, db_ref):
        gv = g_ref[...] * scale
        da_ref[...] = (gv * c_ref[...]).astype(da_ref.dtype)
        db_ref[...] = (gv * s_ref[...]).astype(db_ref.dtype)

    blk = pl.BlockSpec((bs, d), lambda i, hh: (i, hh))
    tbl = pl.BlockSpec((bs, d), lambda i, hh: (i, 0))
    return pl.pallas_call(
        body, name=name, grid=(s // bs, heads), in_specs=[blk, tbl, tbl], out_specs=[blk, blk],
        out_shape=[jax.ShapeDtypeStruct(g.shape, BF16)] * 2,
        compiler_params=_params(("parallel", "parallel")),
    )(g, cos_t, sin_t)


def _adamw(w, g, m, v, name):
    r, wd = w.shape
    br = _blk(r, 512, 8)

    def body(w_ref, g_ref, m_ref, v_ref, d_ref, nm_ref, nv_ref):
        gv = g_ref[...]
        mn = ADAM_B1 * m_ref[...] + (1.0 - ADAM_B1) * gv
        vn = ADAM_B2 * v_ref[...] + (1.0 - ADAM_B2) * (gv * gv)
        m_hat = mn / (1.0 - ADAM_B1 ** ADAM_STEP)
        v_hat = vn / (1.0 - ADAM_B2 ** ADAM_STEP)
        d_ref[...] = -ADAM_LR * (m_hat / (jnp.sqrt(v_hat) + ADAM_EPS) + ADAM_WD * w_ref[...])
        nm_ref[...] = mn
        nv_ref[...] = vn

    row = pl.BlockSpec((br, wd), lambda i: (i, 0))
    return pl.pallas_call(
        body, name=name, grid=(r // br,), in_specs=[row] * 4, out_specs=[row] * 3,
        out_shape=[jax.ShapeDtypeStruct((r, wd), F32)] * 3,
        compiler_params=_params(("parallel",)),
    )(w, g, m, v)


_ANY = pl.BlockSpec(memory_space=pl.ANY)


def _place():
    x, y, c = lax.axis_index("x"), lax.axis_index("y"), lax.axis_index("c")
    chips = [(x, 1 - y), (1 - x, y), (1 - x, 1 - y)]
    return x, y, c, chips


def _all_gather_shards(shard, name):
    r, w = shard.shape
    hr = r // 2

    def body(x_ref, out_ref, send_sems, recv_sems):
        x, y, c, chips = _place()
        sibling = (x, y, 1 - c)

        def rows(j, half):
            return out_ref.at[j, pl.ds(pl.multiple_of(half * hr, 16), hr), :]

        def copy(sem, j, half, to, src=None):
            return pltpu.make_async_remote_copy(
                src_ref=rows(j, half) if src is None else src, dst_ref=rows(j, half),
                send_sem=send_sems.at[sem], recv_sem=recv_sems.at[sem], device_id=to, device_id_type=MESH)

        my_half = x_ref.at[pl.ds(pl.multiple_of(c * hr, 16), hr), :]
        first = [copy(j, j, c, (cx, cy, c), src=my_half) for j, (cx, cy) in enumerate(chips)]
        for cp in first:
            cp.start()
        passed = []
        for j in range(3):
            copy(j, j, c, (x, y, c)).wait_recv()
            fw = copy(3 + j, j, c, sibling)
            fw.start()
            passed.append(fw)
        for j in range(3):
            copy(3 + j, j, 1 - c, (x, y, c)).wait_recv()
        for cp in first + passed:
            cp.wait_send()

    return pl.pallas_call(
        body, name=name, in_specs=[_ANY], out_specs=_ANY,
        out_shape=jax.ShapeDtypeStruct((N_CHIPS - 1, r, w), shard.dtype),
        scratch_shapes=[pltpu.SemaphoreType.DMA((6,)), pltpu.SemaphoreType.DMA((6,))],
        compiler_params=pltpu.CompilerParams(vmem_limit_bytes=VMEM_LIMIT_BYTES),
    )(shard)


def _sibling_swap_halves(g, name):
    nq, r, w = g.shape
    hr = r // 2

    def body(g_ref, a_ref, send_sems, recv_sems):
        x, y, c, _ = _place()
        sibling = (x, y, 1 - c)
        cps = []
        for q in range(nq):
            cp = pltpu.make_async_remote_copy(
                src_ref=g_ref.at[q, pl.ds(pl.multiple_of((1 - c) * hr, 8), hr), :], dst_ref=a_ref.at[q],
                send_sem=send_sems.at[q], recv_sem=recv_sems.at[q], device_id=sibling, device_id_type=MESH)
            cp.start()
            cps.append(cp)
        for cp in cps:
            cp.wait()

    return pl.pallas_call(
        body, name=name, in_specs=[_ANY], out_specs=_ANY,
        out_shape=jax.ShapeDtypeStruct((nq, hr, w), g.dtype),
        scratch_shapes=[pltpu.SemaphoreType.DMA((nq,)), pltpu.SemaphoreType.DMA((nq,))],
        compiler_params=pltpu.CompilerParams(vmem_limit_bytes=VMEM_LIMIT_BYTES),
    )(g)


def _chip_sum(g, a, c_idx, name):
    nq, r, w = g.shape
    hr = r // 2
    br = _blk(hr, 512, 16)
    nb = hr // br

    def body(c_ref, g_ref, a_ref, o_ref):
        o_ref[...] = (g_ref[...] + a_ref[...]).astype(o_ref.dtype)

    return pl.pallas_call(
        body, name=name,
        grid_spec=pltpu.PrefetchScalarGridSpec(
            num_scalar_prefetch=1, grid=(nq, nb),
            in_specs=[pl.BlockSpec((None, br, w), lambda q, i, cr: (q, cr[0] * nb + i, 0)),
                      pl.BlockSpec((None, br, w), lambda q, i, cr: (q, i, 0))],
            out_specs=pl.BlockSpec((None, br, w), lambda q, i, cr: (q, i, 0))),
        out_shape=jax.ShapeDtypeStruct((nq, hr, w), BF16),
        compiler_params=_params(("parallel", "parallel")),
    )(c_idx, g, a)


def _chip_exchange(s4, name):
    nq, hr, w = s4.shape

    def body(s_ref, b_ref, send_sems, recv_sems):
        x, y, c, chips = _place()
        cps = []
        for j, (cx, cy) in enumerate(chips):
            cp = pltpu.make_async_remote_copy(
                src_ref=s_ref.at[2 * cx + cy], dst_ref=b_ref.at[j],
                send_sem=send_sems.at[j], recv_sem=recv_sems.at[j], device_id=(cx, cy, c), device_id_type=MESH)
            cp.start()
            cps.append(cp)
        for cp in cps:
            cp.wait()

    return pl.pallas_call(
        body, name=name, in_specs=[_ANY], out_specs=_ANY,
        out_shape=jax.ShapeDtypeStruct((nq - 1, hr, w), s4.dtype),
        scratch_shapes=[pltpu.SemaphoreType.DMA((3,)), pltpu.SemaphoreType.DMA((3,))],
        compiler_params=pltpu.CompilerParams(vmem_limit_bytes=VMEM_LIMIT_BYTES),
    )(s4)


def _sum_chips(s4, b3, p_idx, name):
    _, hr, w = s4.shape
    nb3 = b3.shape[0]
    br = _blk(hr, 512, 16)

    def body(p_ref, s_ref, b_ref, o_ref):
        acc = s_ref[...].astype(F32)
        for j in range(nb3):
            acc = acc + b_ref[j].astype(F32)
        o_ref[...] = acc

    return pl.pallas_call(
        body, name=name,
        grid_spec=pltpu.PrefetchScalarGridSpec(
            num_scalar_prefetch=1, grid=(hr // br,),
            in_specs=[pl.BlockSpec((None, br, w), lambda i, pr: (pr[0], i, 0)),
                      pl.BlockSpec((nb3, br, w), lambda i, pr: (0, i, 0))],
            out_specs=pl.BlockSpec((br, w), lambda i, pr: (i, 0))),
        out_shape=jax.ShapeDtypeStruct((hr, w), F32),
        compiler_params=_params(("parallel",)),
    )(p_idx, s4, b3)


def _sibling_swap(t, name):
    hr, w = t.shape

    def body(t_ref, o_ref, send_sem, recv_sem):
        x, y, c, _ = _place()
        cp = pltpu.make_async_remote_copy(src_ref=t_ref, dst_ref=o_ref, send_sem=send_sem, recv_sem=recv_sem,
                                          device_id=(x, y, 1 - c), device_id_type=MESH)
        cp.start()
        cp.wait()

    return pl.pallas_call(
        body, name=name, in_specs=[_ANY], out_specs=_ANY,
        out_shape=jax.ShapeDtypeStruct((hr, w), t.dtype),
        scratch_shapes=[pltpu.SemaphoreType.DMA, pltpu.SemaphoreType.DMA],
        compiler_params=pltpu.CompilerParams(vmem_limit_bytes=VMEM_LIMIT_BYTES),
    )(t)


def _all_reduce_small(v, name):
    r, w = v.shape

    def body(v_ref, o_ref, slots, send_sems, recv_sems):
        x, y, c, _ = _place()
        me = 4 * x + 2 * y + c
        slots[me] = v_ref[...]
        cps = []
        for k in range(1, N_DEV):
            fx, fy, fc = (k >> 2) & 1, (k >> 1) & 1, k & 1
            to = (x ^ fx, y ^ fy, c ^ fc)
            cp = pltpu.make_async_remote_copy(
                src_ref=v_ref, dst_ref=slots.at[me], send_sem=send_sems.at[k - 1], recv_sem=recv_sems.at[k - 1],
                device_id=to, device_id_type=MESH)
            cp.start()
            cps.append(cp)
        for k in range(1, N_DEV):
            fx, fy, fc = (k >> 2) & 1, (k >> 1) & 1, k & 1
            src_dev = 4 * (x ^ fx) + 2 * (y ^ fy) + (c ^ fc)
            pltpu.make_async_remote_copy(
                src_ref=v_ref, dst_ref=slots.at[src_dev], send_sem=send_sems.at[k - 1],
                recv_sem=recv_sems.at[k - 1], device_id=(x, y, c), device_id_type=MESH).wait_recv()
        for cp in cps:
            cp.wait_send()
        acc = slots[0]
        for d in range(1, N_DEV):
            acc = acc + slots[d]
        o_ref[...] = acc

    return pl.pallas_call(
        body, name=name,
        in_specs=[pl.BlockSpec(memory_space=pltpu.VMEM)], out_specs=pl.BlockSpec(memory_space=pltpu.VMEM),
        out_shape=jax.ShapeDtypeStruct((r, w), F32),
        scratch_shapes=[pltpu.VMEM((N_DEV, r, w), F32), pltpu.SemaphoreType.DMA((N_DEV - 1,)),
                        pltpu.SemaphoreType.DMA((N_DEV - 1,))],
        compiler_params=pltpu.CompilerParams(vmem_limit_bytes=VMEM_LIMIT_BYTES),
    )(v)


def _part_rows(shape, part_rows=PACK_PART_ROWS):
    assert shape[-1] <= PACK_LANES
    return _round_up(math.prod(shape[:-1]), part_rows)


def _packed_rows(shapes):
    return _round_up(sum(_part_rows(s) for s in shapes), PACK_ROWS_MULT)


def _pack(arrs, total_rows, dtype, part_rows=PACK_PART_ROWS):
    parts = []
    for a in arrs:
        a2 = a.reshape(-1, a.shape[-1]).astype(dtype)
        rows = _part_rows(a.shape, part_rows)
        parts.append(jnp.pad(a2, ((0, rows - a2.shape[0]), (0, PACK_LANES - a2.shape[1]))))
    used = sum(p.shape[0] for p in parts)
    if total_rows > used:
        parts.append(jnp.zeros((total_rows - used, PACK_LANES), dtype))
    return jnp.concatenate(parts, axis=0)


def _unpack(packed, shapes, part_rows=PACK_PART_ROWS):
    out, r0 = [], 0
    for s in shapes:
        out.append(packed[r0:r0 + math.prod(s[:-1]), :s[-1]].reshape(s))
        r0 += _part_rows(s, part_rows)
    return out


_BIG = (("fox_w_in", 2), ("fox_w_out", 1), ("mla_w_kv_a", 0), ("mla_w_kv_b", 1), ("mla_w_q_a", 1),
        ("mla_w_q_b", 2), ("mla_w_out", 1), ("ffn_w_up", 2), ("ffn_w_down", 1))
_SMALL = ("norm_mix_g", "norm_ffn_g", "fox_b_f", "kv_norm_g", "mla_kv_a_norm_g", "mla_q_a_norm_g", "final_norm_g")
_WEIGHTS = ("norm_mix_g", "norm_ffn_g", "fox_w_in", "fox_b_f", "fox_w_out", "kv_norm_g", "mla_w_kv_a",
            "mla_kv_a_norm_g", "mla_w_kv_b", "mla_w_q_a", "mla_q_a_norm_g", "mla_w_q_b", "mla_w_out",
            "ffn_w_up", "ffn_w_down", "final_norm_g")


def _ffn_fwd(x, h, w_up, w_down, tag):
    def relu_sq(acc):
        r = jnp.maximum(acc, 0.0)
        return r, r * r

    r, a = _matmul(h, w_up, mode="nn", out_dtypes=(BF16, BF16), epilogue=relu_sq, name=f"{tag}_up")
    x_out = _matmul(a, w_down, mode="nn", out_dtypes=(F32,), epilogue=lambda acc, res: (acc + res,),
                    extras=(x,), name=f"{tag}_down")
    return x_out, r, a


def _ffn_bwd(dx_out, x_in, h, r, a, g_norm, w_up, w_down, tag):
    d_u = _matmul(dx_out, w_down, mode="nt", out_dtypes=(BF16,), epilogue=lambda acc, rr: (acc * (2.0 * rr.astype(F32)),),
                  extras=(r,), name=f"{tag}_d_act")
    d_w_down = _matmul(a, dx_out, mode="tn", out_dtypes=(F32,), name=f"{tag}_d_w_down")
    d_w_up = _matmul(h, d_u, mode="tn", out_dtypes=(F32,), name=f"{tag}_d_w_up")
    d_h = _matmul(d_u, w_up, mode="nt", out_dtypes=(F32,), name=f"{tag}_d_h")
    dx_in, (d_g,) = _rms_bwd(x_in, [(g_norm, d_h)], dx_out, name=f"{tag}_d_norm")
    return dx_in, d_w_up, d_w_down, d_g


def kernel(x, norm_mix_g, norm_ffn_g, fox_w_in, fox_b_f, fox_w_out, kv_norm_g, mla_w_kv_a, mla_kv_a_norm_g, mla_w_kv_b, mla_w_q_a, mla_q_a_norm_g, mla_w_q_b, mla_w_out, ffn_w_up, ffn_w_down, final_norm_g, loss_target, m_norm_mix_g, m_norm_ffn_g, m_fox_w_in, m_fox_b_f, m_fox_w_out, m_kv_norm_g, m_mla_w_kv_a, m_mla_kv_a_norm_g, m_mla_w_kv_b, m_mla_w_q_a, m_mla_q_a_norm_g, m_mla_w_q_b, m_mla_w_out, m_ffn_w_up, m_ffn_w_down, m_final_norm_g, v_norm_mix_g, v_norm_ffn_g, v_fox_w_in, v_fox_b_f, v_fox_w_out, v_kv_norm_g, v_mla_w_kv_a, v_mla_kv_a_norm_g, v_mla_w_kv_b, v_mla_w_q_a, v_mla_q_a_norm_g, v_mla_w_q_b, v_mla_w_out, v_ffn_w_up, v_ffn_w_down, v_final_norm_g):
    args = dict(locals())
    w_in = {n: args[n] for n in _WEIGHTS}
    m_in = {n: args["m_" + n] for n in _WEIGHTS}
    v_in = {n: args["v_" + n] for n in _WEIGHTS}

    xs = x[0]
    seq, d_model = xs.shape
    tgt = loss_target[0]
    fox_h, mla_h, nope = FOX_HEADS, MLA_HEADS, QK_NOPE_DIM
    kv_rank = mla_kv_a_norm_g.shape[0]
    rope = mla_w_kv_a.shape[1] - kv_rank
    half = rope // 2
    q_rank = mla_q_a_norm_g.shape[1]
    v_dim = mla_w_kv_b.shape[1] * N_CHIPS // mla_h - nope
    fox_w = fox_w_out.shape[1] * N_CHIPS
    fox_dh = fox_w // fox_h

    big_names = [n for n, _ in _BIG]
    shard_shapes = [w_in[n].shape for n in big_names]
    rows = _packed_rows(shard_shapes)
    my_shard = _pack([w_in[n] for n in big_names], rows, BF16)
    others = _all_gather_shards(my_shard, name="gather_weights")
    by_relation = jnp.concatenate([my_shard[None], others], axis=0)
    p_chip = 2 * lax.axis_index("x") + lax.axis_index("y")
    full = {}
    for q in range(N_CHIPS):
        shard_q = lax.dynamic_index_in_dim(by_relation, p_chip ^ q, axis=0, keepdims=False)
        for (n, ax), piece in zip(_BIG, _unpack(shard_q, shard_shapes)):
            full.setdefault(n, []).append(piece)
    full = {n: jnp.concatenate(full[n], axis=ax) for n, ax in _BIG}

    fox_scale = fox_dh ** -0.5
    fox_wd = _round_up(fox_dh + 9, LANE_TILE)
    fox_vwd = _round_up(fox_dh + 4, LANE_TILE)
    w_fox_in = full["fox_w_in"][0]
    w_fq = _pad_heads(w_fox_in[:, :fox_w] * fox_scale, fox_h, fox_wd, 1)
    w_fk = _pad_heads(w_fox_in[:, fox_w:2 * fox_w], fox_h, fox_wd, 1)
    w_fv = _pad_heads(w_fox_in[:, 2 * fox_w:3 * fox_w], fox_h, fox_vwd, 1)
    w_gate = w_fox_in[:, 3 * fox_w:]
    w_fox_out = _pad_heads(full["fox_w_out"][0], fox_h, fox_vwd, 0)
    n_cx = _round_up(3 * fox_h + 1, LANE_TILE)
    c_piece = lambda i: (lambda hh: 3 * hh + i)
    one_col = 3 * fox_h
    e_fq = _placement(n_cx, fox_h, fox_wd, [(c_piece(i), fox_dh + i, 1.0) for i in range(3)]
                      + [(one_col, fox_dh + 3 + i, 1.0) for i in range(3)])
    e_fk = _placement(n_cx, fox_h, fox_wd, [(one_col, fox_dh + i, 1.0) for i in range(3)]
                      + [(c_piece(i), fox_dh + 3 + i, -1.0) for i in range(3)]
                      + [(one_col, fox_dh + 6 + i, 1.0) for i in range(3)])
    e_fv = _placement(n_cx, fox_h, fox_vwd, [(one_col, fox_dh + i, -1.0) for i in range(3)]
                      + [(one_col, fox_dh + 3, 1.0)])

    mla_scale = (nope + rope) ** -0.5
    mla_dk = nope + rope
    mla_wd = _round_up(mla_dk + 3, LANE_TILE)
    mla_vwd = _round_up(v_dim + 4, LANE_TILE)
    w_kv_a = full["mla_w_kv_a"]
    w_kv_b3 = full["mla_w_kv_b"].reshape(kv_rank, mla_h, nope + v_dim)
    w_kn = _pad_heads(w_kv_b3[:, :, :nope].reshape(kv_rank, -1), mla_h, mla_wd, 1)
    w_mv = _pad_heads(w_kv_b3[:, :, nope:].reshape(kv_rank, -1), mla_h, mla_vwd, 1)
    w_q_a = full["mla_w_q_a"][0]
    w_q_b3 = full["mla_w_q_b"][0].reshape(q_rank, mla_h, nope + rope)
    w_qa_ = _pad_heads(w_q_b3.reshape(q_rank, -1), mla_h, mla_wd, 1)
    w_qb_ = _pad_heads(jnp.concatenate([jnp.zeros_like(w_q_b3[:, :, :nope]), -w_q_b3[:, :, nope + half:],
                                        w_q_b3[:, :, nope:nope + half]], axis=-1).reshape(q_rank, -1),
                       mla_h, mla_wd, 1)
    w_mla_out = _pad_heads(full["mla_w_out"][0], mla_h, mla_vwd, 0)
    w_up, w_down = full["ffn_w_up"], full["ffn_w_down"]
    n_kx = _round_up(rope + 1, LANE_TILE)
    e_mk = _placement(n_kx, mla_h, mla_wd, [(j, nope + j, 1.0) for j in range(rope)]
                      + [(rope, mla_dk + i, 1.0) for i in range(3)])
    e_mv = _placement(n_kx, mla_h, mla_vwd, [(rope, v_dim + i, -1.0) for i in range(3)] + [(rope, v_dim + 3, 1.0)])
    e_kr_u = _placement(n_kx, mla_h, mla_wd, [(j, nope + j, 1.0) for j in range(rope)]).T
    e_kr_v = _placement(n_kx, mla_h, mla_wd, [(j, nope + half + j, 1.0) for j in range(half)]
                        + [(half + j, nope + j, -1.0) for j in range(half)]).T

    inv = 1.0 / (ROPE_BASE ** (jnp.arange(0, rope, 2, dtype=F32) / rope))
    ang = jnp.arange(seq, dtype=F32)[:, None] * inv[None, :]
    cos, sin = jnp.cos(ang), jnp.sin(ang)
    pad_t = jnp.zeros((seq, mla_wd - mla_dk), F32)
    cos_t = jnp.concatenate([jnp.ones((seq, nope), F32), cos, cos, pad_t], axis=1)
    sin_t = jnp.concatenate([jnp.zeros((seq, nope), F32), sin, sin, pad_t], axis=1)
    pad_k = jnp.zeros((seq, n_kx - rope), F32)
    cos_k, sin_k = jnp.concatenate([cos, cos, pad_k], axis=1), jnp.concatenate([sin, sin, pad_k], axis=1)

    (h0,) = _rms_fwd(xs, norm_mix_g[0:1], name="l0_norm_mix")
    gate = _matmul(h0, w_gate, mode="nn", out_dtypes=(F32,), name="fox_gate")
    z = gate + fox_b_f[0][None, :]
    cum = jnp.cumsum(jax.nn.log_sigmoid(z), axis=0)
    cx = jnp.concatenate([_split3(cum).reshape(seq, 3 * fox_h), jnp.ones((seq, 1), BF16),
                          jnp.zeros((seq, n_cx - 3 * fox_h - 1), BF16)], axis=1)
    fqa = _matmul(h0, w_fq, mode="nn", out_dtypes=(BF16,), placed=(cx, e_fq), name="fox_q")
    fka = _matmul(h0, w_fk, mode="nn", out_dtypes=(BF16,), placed=(cx, e_fk), name="fox_k")
    fva = _matmul(h0, w_fv, mode="nn", out_dtypes=(BF16,), placed=(cx, e_fv), name="fox_v")
    foa, fqb = _flash_fwd(fqa, fka, fva, fox_h, fox_dh + 3, fox_dh + 6, FOX_FWD_SUB_ROWS, name="fox_attn")
    add_res = lambda acc, res: (acc + res,)
    x1 = _matmul(foa, w_fox_out, mode="nn", out_dtypes=(F32,), epilogue=add_res, extras=(xs,), name="fox_out")
    (h1,) = _rms_fwd(x1, norm_ffn_g[0:1], name="l0_norm_ffn")
    x2, r0, a0 = _ffn_fwd(x1, h1, w_up[0], w_down[0], "ffn0")

    src, h2 = _rms_fwd(x2, jnp.stack([kv_norm_g, norm_mix_g[1]]), name="l1_norm_kv_mix")
    kv_a = _matmul(src, w_kv_a, mode="nn", out_dtypes=(F32,), name="mla_kv_a")
    (c_kv,) = _rms_fwd(kv_a, mla_kv_a_norm_g[None, :], name="mla_norm_kv_a")
    kr1, kr2 = _rope(kv_a[None, :, kv_rank:kv_rank + half], kv_a[None, :, kv_rank + half:], cos, sin, 1.0,
                     name="mla_rope_k")
    krx = jnp.concatenate([kr1.astype(BF16), kr2.astype(BF16), jnp.ones((seq, 1), BF16),
                           jnp.zeros((seq, n_kx - rope - 1), BF16)], axis=1)
    mka = _matmul(c_kv, w_kn, mode="nn", out_dtypes=(BF16,), placed=(krx, e_mk), name="mla_k")
    mva = _matmul(c_kv, w_mv, mode="nn", out_dtypes=(BF16,), placed=(krx, e_mv), name="mla_v")
    cq_pre = _matmul(h2, w_q_a, mode="nn", out_dtypes=(F32,), name="mla_q_a")
    (c_q,) = _rms_fwd(cq_pre, mla_q_a_norm_g, name="mla_norm_q_a")
    q_a_part = _matmul(c_q, w_qa_, mode="nn", out_dtypes=(F32,), name="mla_q_b_cos")
    q_b_part = _matmul(c_q, w_qb_, mode="nn", out_dtypes=(F32,), name="mla_q_b_sin")
    mqa = _rope_mix(q_a_part, q_b_part, cos_t, sin_t, mla_scale, mla_h, name="mla_rope_q")
    moa, mqb = _flash_fwd(mqa, mka, mva, mla_h, v_dim + 3, mla_dk, MLA_FWD_SUB_ROWS, name="mla_attn")
    x3 = _matmul(moa, w_mla_out, mode="nn", out_dtypes=(F32,), epilogue=add_res, extras=(x2,), name="mla_out")
    (h3,) = _rms_fwd(x3, norm_ffn_g[1:2], name="l1_norm_ffn")
    x4, r1, a1 = _ffn_fwd(x3, h3, w_up[1], w_down[1], "ffn1")

    loss_tile, dx4, d_final_g = _loss_head(x4, final_norm_g[None, :], tgt, name="loss_head")
    loss = lax.psum(loss_tile[0, 0], ("x", "y", "c"))

    gw = {}
    dx3, d_up1, d_down1, d_nf1 = _ffn_bwd(dx4, x3, h3, r1, a1, norm_ffn_g[1:2], w_up[1], w_down[1], "ffn1")

    d_mo = _matmul(dx3, w_mla_out, mode="nt", out_dtypes=(BF16,), name="mla_d_ctx")
    gw["mla_w_out"] = _unpad_heads(_matmul(moa, dx3, mode="tn", out_dtypes=(F32,), name="mla_d_w_out"),
                                   mla_h, v_dim, 0)[None]
    d_moa = _delta_place(d_mo, moa, mla_h, v_dim, name="mla_attn_delta")
    d_mqa, d_mka, d_mva = _flash_bwd(mqb, mka, mva, d_moa, mla_h, MLA_BWD_HEADS_PER_STEP,
                                     name="mla_attn_bwd")
    d_qa_part, d_qb_part = _rope_unmix(d_mqa, cos_t, sin_t, mla_scale, mla_h, name="mla_rope_dq")
    d_w_qa_ = _unpad_heads(_matmul(c_q, d_qa_part, mode="tn", out_dtypes=(F32,), name="mla_d_w_q_b_cos"),
                           mla_h, mla_dk, 1).reshape(q_rank, mla_h, mla_dk)
    d_w_qb_ = _unpad_heads(_matmul(c_q, d_qb_part, mode="tn", out_dtypes=(F32,), name="mla_d_w_q_b_sin"),
                           mla_h, mla_dk, 1).reshape(q_rank, mla_h, mla_dk)
    gw["mla_w_q_b"] = jnp.concatenate(
        [d_w_qa_[:, :, :nope], d_w_qa_[:, :, nope:nope + half] + d_w_qb_[:, :, nope + half:],
         d_w_qa_[:, :, nope + half:] - d_w_qb_[:, :, nope:nope + half]], axis=-1).reshape(1, q_rank, mla_h * mla_dk)
    d_c_q_sin = _matmul(d_qb_part, w_qb_, mode="nt", out_dtypes=(F32,), name="mla_d_c_q_sin")
    d_c_q = _matmul(d_qa_part, w_qa_, mode="nt", out_dtypes=(F32,), epilogue=add_res, extras=(d_c_q_sin,),
                    name="mla_d_c_q")
    d_cq_pre, (d_q_a_g,) = _rms_bwd(cq_pre, [(mla_q_a_norm_g, d_c_q)], None, name="mla_d_norm_q_a")
    gw["mla_w_q_a"] = _matmul(h2, d_cq_pre, mode="tn", out_dtypes=(F32,), name="mla_d_w_q_a")[None]
    d_h2 = _matmul(d_cq_pre, w_q_a, mode="nt", out_dtypes=(F32,), name="mla_d_h")

    d_w_kn = _unpad_heads(_matmul(c_kv, d_mka, mode="tn", out_dtypes=(F32,), name="mla_d_w_k"), mla_h, nope, 1)
    d_w_mv = _unpad_heads(_matmul(c_kv, d_mva, mode="tn", out_dtypes=(F32,), name="mla_d_w_v"), mla_h, v_dim, 1)
    gw["mla_w_kv_b"] = jnp.concatenate([d_w_kn.reshape(kv_rank, mla_h, nope), d_w_mv.reshape(kv_rank, mla_h, v_dim)],
                                       axis=-1).reshape(kv_rank, mla_h * (nope + v_dim))
    d_c_kv_v = _matmul(d_mva, w_mv, mode="nt", out_dtypes=(F32,), name="mla_d_c_kv_v")
    d_c_kv = _matmul(d_mka, w_kn, mode="nt", out_dtypes=(F32,), epilogue=add_res, extras=(d_c_kv_v,),
                     name="mla_d_c_kv")
    d_ckv_pre, (d_kv_a_g,) = _rms_bwd(kv_a, [(mla_kv_a_norm_g[None, :], d_c_kv)], None, name="mla_d_norm_kv_a")
    d_kr_u = _matmul(d_mka, e_kr_u, mode="nn", out_dtypes=(F32,), name="mla_d_k_rope_u")
    d_kr_v = _matmul(d_mka, e_kr_v, mode="nn", out_dtypes=(F32,), name="mla_d_k_rope_v")
    d_kr = _rope_mix(d_kr_u, d_kr_v, cos_k, sin_k, 1.0, 1, name="mla_rope_dk")
    d_kv_a = jnp.concatenate([d_ckv_pre, d_kr[:, :rope].astype(F32)], axis=1)
    gw["mla_w_kv_a"] = _matmul(src, d_kv_a, mode="tn", out_dtypes=(F32,), name="mla_d_w_kv_a")
    d_src = _matmul(d_kv_a, w_kv_a, mode="nt", out_dtypes=(F32,), name="mla_d_src")
    dx2, (d_kv_g, d_nm1) = _rms_bwd(x2, [(kv_norm_g[None, :], d_src), (norm_mix_g[1:2], d_h2)], dx3,
                                    name="l1_d_norm_kv_mix")

    dx1, d_up0, d_down0, d_nf0 = _ffn_bwd(dx2, x1, h1, r0, a0, norm_ffn_g[0:1], w_up[0], w_down[0], "ffn0")
    gw["ffn_w_up"] = jnp.stack([d_up0, d_up1])
    gw["ffn_w_down"] = jnp.stack([d_down0, d_down1])

    d_fo = _matmul(dx1, w_fox_out, mode="nt", out_dtypes=(BF16,), name="fox_d_ctx")
    gw["fox_w_out"] = _unpad_heads(_matmul(foa, dx1, mode="tn", out_dtypes=(F32,), name="fox_d_w_out"),
                                   fox_h, fox_dh, 0)[None]
    d_foa = _delta_place(d_fo, foa, fox_h, fox_dh, name="fox_attn_delta")
    d_fqa, d_fka, d_fva = _flash_bwd(fqb, fka, fva, d_foa, fox_h, FOX_BWD_HEADS_PER_STEP, name="fox_attn_bwd")
    d_cum = (d_fqa.reshape(seq, fox_h, fox_wd)[:, :, fox_dh]
             - d_fka.reshape(seq, fox_h, fox_wd)[:, :, fox_dh + 3])
    d_z = lax.cumsum(d_cum, axis=0, reverse=True) * jax.nn.sigmoid(-z)
    d_b_f = jnp.sum(d_z, axis=0)
    d_w_in = [_unpad_heads(_matmul(h0, g, mode="tn", out_dtypes=(F32,), name=f"fox_d_w_{tag}"), fox_h, fox_dh, 1)
              for tag, g in (("q", d_fqa), ("k", d_fka), ("v", d_fva))]
    d_w_gate = _matmul(h0, d_z, mode="tn", out_dtypes=(F32,), name="fox_d_w_gate")
    gw["fox_w_in"] = jnp.concatenate([d_w_in[0] * fox_scale, d_w_in[1], d_w_in[2], d_w_gate], axis=1)[None]
    d_h0 = _matmul(d_z, w_gate, mode="nt", out_dtypes=(F32,), name="fox_d_h_gate")
    for tag, g, w in (("q", d_fqa, w_fq), ("k", d_fka, w_fk), ("v", d_fva, w_fv)):
        d_h0 = _matmul(g, w, mode="nt", out_dtypes=(F32,), epilogue=add_res, extras=(d_h0,), name=f"fox_d_h_{tag}")
    grad_x, (d_nm0,) = _rms_bwd(xs, [(norm_mix_g[0:1], d_h0)], dx1, name="l0_d_norm_mix")

    c_idx = lax.axis_index("c").astype(jnp.int32).reshape(1)
    parts = []
    for (n, ax), shape in zip(_BIG, shard_shapes):
        g = gw[n]
        g = jnp.moveaxis(g.reshape(g.shape[:ax] + (N_CHIPS, shape[ax]) + g.shape[ax + 1:]), ax, 0)
        g = g.reshape(N_CHIPS, -1, shape[-1])
        parts.append(jnp.pad(g, ((0, 0), (0, _part_rows(shape) - g.shape[1]), (0, PACK_LANES - shape[-1]))))
    parts.append(jnp.zeros((N_CHIPS, rows - sum(p.shape[1] for p in parts), PACK_LANES), F32))
    g4 = jnp.concatenate(parts, axis=1)
    a4 = _sibling_swap_halves(g4, name="grads_to_sibling")
    s4 = _chip_sum(g4, a4, c_idx, name="grads_chip_sum")
    b3 = _chip_exchange(s4, name="grads_between_chips")
    t_mine = _sum_chips(s4, b3, p_chip.astype(jnp.int32).reshape(1), name="grads_sum_chips")
    t_theirs = _sibling_swap(t_mine, name="grads_join_halves")
    is_south = lax.axis_index("c") == 0
    g_big = jnp.concatenate([jnp.where(is_south, t_mine, t_theirs), jnp.where(is_south, t_theirs, t_mine)],
                            axis=0)

    small_local = {"norm_mix_g": jnp.concatenate([d_nm0, d_nm1], axis=0),
                   "norm_ffn_g": jnp.concatenate([d_nf0, d_nf1], axis=0),
                   "fox_b_f": d_b_f[None, :], "kv_norm_g": d_kv_g[0], "mla_kv_a_norm_g": d_kv_a_g[0],
                   "mla_q_a_norm_g": d_q_a_g, "final_norm_g": d_final_g[0]}
    small_shapes = [w_in[n].shape for n in _SMALL]
    small_rows = sum(_part_rows(s, SMALL_PART_ROWS) for s in small_shapes)
    pack_small = lambda arrs: _pack(arrs, small_rows, F32, SMALL_PART_ROWS)
    g_small = _all_reduce_small(pack_small([small_local[n] for n in _SMALL]), name="grads_small")

    d_big, nm_big, nv_big = _adamw(_pack([w_in[n] for n in big_names], rows, F32), g_big,
                                   _pack([m_in[n] for n in big_names], rows, F32),
                                   _pack([v_in[n] for n in big_names], rows, F32), name="adamw_big")
    d_sm, nm_sm, nv_sm = _adamw(pack_small([w_in[n] for n in _SMALL]), g_small,
                                pack_small([m_in[n] for n in _SMALL]),
                                pack_small([v_in[n] for n in _SMALL]), name="adamw_small")

    def spread(big, small):
        out = dict(zip(big_names, _unpack(big, shard_shapes)))
        out.update(zip(_SMALL, _unpack(small, small_shapes, SMALL_PART_ROWS)))
        return [out[n] for n in _WEIGHTS]

    return (loss, grad_x[None], *spread(g_big, g_small), *spread(d_big, d_sm), *spread(nm_big, nm_sm),
            *spread(nv_big, nv_sm))
```

```python
import math

import numpy as np
import jax
import jax.numpy as jnp
from jax import lax
from jax.experimental import pallas as pl
from jax.experimental.pallas import tpu as pltpu

F32 = jnp.float32
BF16 = jnp.bfloat16

FOX_HEADS = 16
MLA_HEADS = 8
QK_NOPE_DIM = 128
ROPE_BASE = 10000.0
EPS = 1e-6

ADAM_LR = 0.001
ADAM_B1 = 0.9
ADAM_B2 = 0.999
ADAM_EPS = 1e-08
ADAM_WD = 0.01
ADAM_STEP = 10

N_CHIPS = 4
N_DEV = 8
PACK_LANES = 1024
PACK_PART_ROWS = 16
SMALL_PART_ROWS = 8
PACK_ROWS_MULT = 1024
VMEM_LIMIT_BYTES = 48 * 1024 * 1024
LANE_TILE = 128
MATMUL_BLOCK = 1024
ATTN_BLOCK_Q = 1024
ATTN_BLOCK_K = 1024
ATTN_FWD_LANES = 1024
FOX_BWD_HEADS_PER_STEP = 4
MLA_BWD_HEADS_PER_STEP = 2
ATTN_SUB_ROWS = 256
FOX_FWD_SUB_ROWS = (1024, 512)
MLA_FWD_SUB_ROWS = (256, 256)
NEG_BIG = -1e30
MESH = pl.DeviceIdType.MESH


def _round_up(n, m):
    return -(-n // m) * m


def _blk(dim, pref, mult=128):
    if dim <= pref:
        return dim
    b = (pref // mult) * mult
    while b >= mult:
        if dim % b == 0:
            return b
        b -= mult
    return dim


def _params(sem=None):
    return pltpu.CompilerParams(dimension_semantics=sem, vmem_limit_bytes=VMEM_LIMIT_BYTES)


_DIMS = {"nn": (((1,), (0,)), ((), ())), "nt": (((1,), (1,)), ((), ())), "tn": (((0,), (0,)), ((), ()))}


def _matmul(a, b, *, mode, out_dtypes, name, epilogue=None, extras=(), placed=None):
    if mode == "tn":
        kdim, m = a.shape
    else:
        m, kdim = a.shape
    n = b.shape[0] if mode == "nt" else b.shape[1]
    bm, bn, bk = _blk(m, MATMUL_BLOCK), _blk(n, MATMUL_BLOCK), _blk(kdim, MATMUL_BLOCK)
    nk = kdim // bk
    n_extra, n_out = len(extras), len(out_dtypes)
    n_placed = 0 if placed is None else 2
    dims = _DIMS[mode]

    def body(a_ref, b_ref, *rest):
        placed_refs = rest[:n_placed]
        rest = rest[n_placed:]
        extra_refs = rest[:n_extra]
        out_refs = rest[n_extra:n_extra + n_out]

        def finish(acc):
            if n_placed:
                acc = acc + lax.dot_general(placed_refs[0][...], placed_refs[1][...], _DIMS["nn"],
                                            preferred_element_type=F32)
            res = (acc,) if epilogue is None else epilogue(acc, *[r[...] for r in extra_refs])
            for o_ref, r in zip(out_refs, res):
                o_ref[...] = r.astype(o_ref.dtype)

        part = lax.dot_general(a_ref[...].astype(BF16), b_ref[...].astype(BF16), dims, preferred_element_type=F32)
        if nk == 1:
            finish(part)
            return
        acc_ref = rest[n_extra + n_out]
        k = pl.program_id(2)

        @pl.when(k == 0)
        def _():
            acc_ref[...] = part

        @pl.when((k > 0) & (k < nk - 1))
        def _():
            acc_ref[...] += part

        @pl.when(k == nk - 1)
        def _():
            finish(acc_ref[...] + part)

    if mode == "tn":
        a_spec = pl.BlockSpec((bk, bm), lambda i, j, k: (k, i))
    else:
        a_spec = pl.BlockSpec((bm, bk), lambda i, j, k: (i, k))
    if mode == "nt":
        b_spec = pl.BlockSpec((bn, bk), lambda i, j, k: (j, k))
    else:
        b_spec = pl.BlockSpec((bk, bn), lambda i, j, k: (k, j))
    tile = pl.BlockSpec((bm, bn), lambda i, j, k: (i, j))
    placed_specs = []
    if n_placed:
        k2 = placed[0].shape[1]
        placed_specs = [pl.BlockSpec((bm, k2), lambda i, j, k: (i, 0)), pl.BlockSpec((k2, bn), lambda i, j, k: (0, j))]
    outs = pl.pallas_call(
        body, name=name,
        grid=(m // bm, n // bn, nk),
        in_specs=[a_spec, b_spec] + placed_specs + [tile] * n_extra,
        out_specs=[tile] * n_out,
        out_shape=[jax.ShapeDtypeStruct((m, n), dt) for dt in out_dtypes],
        scratch_shapes=[pltpu.VMEM((bm, bn), F32)] if nk > 1 else [],
        compiler_params=_params(("parallel", "parallel", "arbitrary")),
    )(a, b, *(placed or ()), *extras)
    return outs[0] if n_out == 1 else outs


def _rms_fwd(x, gains, name):
    s = x.shape[0]
    g, w = gains.shape
    bs = _blk(s, 512, 8)

    def body(x_ref, g_ref, *out_refs):
        xv = x_ref[...]
        y = xv * lax.rsqrt(jnp.mean(xv * xv, axis=-1, keepdims=True) + EPS)
        for i, o_ref in enumerate(out_refs):
            o_ref[...] = (y * g_ref[i:i + 1, :]).astype(o_ref.dtype)

    row = pl.BlockSpec((bs, w), lambda i: (i, 0))
    outs = pl.pallas_call(
        body, name=name, grid=(s // bs,),
        in_specs=[row, pl.BlockSpec((g, w), lambda i: (0, 0))],
        out_specs=[row] * g,
        out_shape=[jax.ShapeDtypeStruct((s, w), BF16)] * g,
        compiler_params=_params(("parallel",)),
    )(x, gains)
    return outs


def _rms_bwd(x, branches, resid, name):
    s = x.shape[0]
    w = branches[0][0].shape[1]
    nb = len(branches)
    bs = _blk(s, 512, 8)
    has_resid = resid is not None

    def body(x_ref, *rest):
        g_refs = rest[:nb]
        dy_refs = rest[nb:2 * nb]
        pos = 2 * nb
        r_ref = rest[pos] if has_resid else None
        pos += int(has_resid)
        dx_ref = rest[pos]
        dg_refs = rest[pos + 1:pos + 1 + nb]
        i = pl.program_id(0)

        @pl.when(i == 0)
        def _():
            for dg_ref in dg_refs:
                dg_ref[...] = jnp.zeros_like(dg_ref)

        xv = x_ref[...]
        rstd = lax.rsqrt(jnp.mean(xv * xv, axis=-1, keepdims=True) + EPS)
        xhat = xv * rstd
        dx = r_ref[...] if has_resid else jnp.zeros_like(xv)
        for g_ref, dy_ref, dg_ref in zip(g_refs, dy_refs, dg_refs):
            dy = dy_ref[...].astype(F32)
            dyg = dy * g_ref[...]
            dx = dx + rstd * (dyg - xhat * jnp.mean(dyg * xhat, axis=-1, keepdims=True))
            dg_ref[...] += jnp.sum(dy * xhat, axis=0, keepdims=True)
        dx_ref[...] = dx

    row = pl.BlockSpec((bs, w), lambda i: (i, 0))
    vec = pl.BlockSpec((1, w), lambda i: (0, 0))
    args = [x] + [g for g, _ in branches] + [dy for _, dy in branches] + ([resid] if has_resid else [])
    outs = pl.pallas_call(
        body, name=name, grid=(s // bs,),
        in_specs=[row] + [vec] * nb + [row] * nb + ([row] if has_resid else []),
        out_specs=[row] + [vec] * nb,
        out_shape=[jax.ShapeDtypeStruct((s, w), F32)] + [jax.ShapeDtypeStruct((1, w), F32)] * nb,
        compiler_params=_params(("arbitrary",)),
    )(*args)
    return outs[0], list(outs[1:])


def _loss_head(x, g, target, name):
    s, w = x.shape
    bs = _blk(s, 512, 8)

    def body(x_ref, g_ref, t_ref, loss_ref, dx_ref, dg_ref):
        i = pl.program_id(0)

        @pl.when(i == 0)
        def _():
            loss_ref[...] = jnp.zeros_like(loss_ref)
            dg_ref[...] = jnp.zeros_like(dg_ref)

        xv = x_ref[...]
        gv = g_ref[...]
        rstd = lax.rsqrt(jnp.mean(xv * xv, axis=-1, keepdims=True) + EPS)
        xhat = xv * rstd
        err = xhat * gv - t_ref[...]
        loss_ref[...] += 0.5 * jnp.sum(jnp.mean(err * err, axis=-1, keepdims=True))
        dy = err * (1.0 / w)
        dyg = dy * gv
        dx_ref[...] = rstd * (dyg - xhat * jnp.mean(dyg * xhat, axis=-1, keepdims=True))
        dg_ref[...] += jnp.sum(dy * xhat, axis=0, keepdims=True)

    row = pl.BlockSpec((bs, w), lambda i: (i, 0))
    vec = pl.BlockSpec((1, w), lambda i: (0, 0))
    return pl.pallas_call(
        body, name=name, grid=(s // bs,),
        in_specs=[row, vec, row],
        out_specs=[pl.BlockSpec((8, 128), lambda i: (0, 0)), row, vec],
        out_shape=[jax.ShapeDtypeStruct((8, 128), F32), jax.ShapeDtypeStruct((s, w), F32),
                   jax.ShapeDtypeStruct((1, w), F32)],
        compiler_params=_params(("arbitrary",)),
    )(x, g, target)


def _rope(a, b, cos, sin, sign, name):
    g, s, w = a.shape
    bs = _blk(s, 1024, 8)

    def body(a_ref, b_ref, c_ref, s_ref, o1_ref, o2_ref):
        av = jnp.sum(a_ref[...].astype(F32), axis=0)
        bv = jnp.sum(b_ref[...].astype(F32), axis=0)
        cv, sv = c_ref[...], s_ref[...] * sign
        o1_ref[...] = av * cv - bv * sv
        o2_ref[...] = bv * cv + av * sv

    grp = pl.BlockSpec((g, bs, w), lambda i: (0, i, 0))
    row = pl.BlockSpec((bs, w), lambda i: (i, 0))
    return pl.pallas_call(
        body, name=name, grid=(s // bs,),
        in_specs=[grp, grp, row, row], out_specs=[row, row],
        out_shape=[jax.ShapeDtypeStruct((s, w), F32)] * 2,
        compiler_params=_params(("parallel",)),
    )(a, b, cos, sin)


def _causal_table(s, bq, bk, q_major):
    nq, nk = s // bq, s // bk
    rows = []
    if q_major:
        for qi in range(nq):
            kmax = (qi * bq + bq - 1) // bk
            for ki in range(kmax + 1):
                rows.append((qi, ki, int(ki * bk + bk - 1 > qi * bq), int(ki == 0), int(ki == kmax)))
    else:
        for ki in range(nk):
            qmin = (ki * bk) // bq
            for qi in range(qmin, nq):
                rows.append((qi, ki, int(ki * bk + bk - 1 > qi * bq), int(qi == qmin), int(qi == nq - 1)))
    return jnp.asarray(np.array(rows, np.int32).T)


def _causal_keep(q0, k0, nq, nk, transposed):
    if transposed:
        kpos = k0 + lax.broadcasted_iota(jnp.int32, (nk, nq), 0)
        qpos = q0 + lax.broadcasted_iota(jnp.int32, (nk, nq), 1)
    else:
        qpos = q0 + lax.broadcasted_iota(jnp.int32, (nq, nk), 0)
        kpos = k0 + lax.broadcasted_iota(jnp.int32, (nq, nk), 1)
    return kpos <= qpos


def _sub_tiles(n_rows, n_cols, masked, square, rows_are_keys, sub_rows):
    sub = min(sub_rows, n_rows)
    out = []
    for r0 in range(0, n_rows, sub):
        if masked and square:
            c0, nc = (r0, n_cols - r0) if rows_are_keys else (0, r0 + sub)
        else:
            c0, nc = 0, n_cols
        out.append((r0, sub, c0, nc))
    return out


_NT = (((1,), (1,)), ((), ()))
_NN = (((1,), (0,)), ((), ()))


def _attn_specs(bq, bk):
    qspec = lambda d: pl.BlockSpec((bq, d), lambda hh, t, tb: (tb[0, t], hh))
    kspec = lambda d: pl.BlockSpec((bk, d), lambda hh, t, tb: (tb[1, t], hh))
    return qspec, kspec


def _split3_cols(x):
    hi = x.astype(BF16).astype(F32)
    rest = x - hi
    mid = rest.astype(BF16).astype(F32)
    lo = (rest - mid).astype(BF16).astype(F32)
    return hi, mid, lo


def _place3(base, col, pieces, sign):
    lane = lax.broadcasted_iota(jnp.int32, base.shape, 1)
    out = base.astype(F32)
    for i, piece in enumerate(pieces):
        out = jnp.where(lane == col + i, sign * piece, out)
    return out.astype(BF16)


def _flash_fwd(qa, ka, va, heads, l_col, lse_col, sub_rows, name):
    s = qa.shape[0]
    da, dv = qa.shape[1] // heads, va.shape[1] // heads
    hps = max(n for n in range(1, ATTN_FWD_LANES // max(da, dv) + 1) if heads % n == 0)
    bq, bk = _blk(s, ATTN_BLOCK_Q), _blk(s, ATTN_BLOCK_K)
    tab = _causal_table(s, bq, bk, True)

    def body(tab_ref, q_ref, k_ref, v_ref, o_ref, qb_ref, m_sc, acc_sc):
        t = pl.program_id(1)
        qi, ki = tab_ref[0, t], tab_ref[1, t]

        @pl.when(tab_ref[3, t] == 1)
        def _():
            m_sc[...] = jnp.full_like(m_sc, NEG_BIG)
            acc_sc[...] = jnp.zeros_like(acc_sc)

        def step(masked):
            for hh in range(hps):
                qc, vc = slice(hh * da, (hh + 1) * da), slice(hh * dv, (hh + 1) * dv)
                for r0, nr, c0, nc in _sub_tiles(bq, bk, masked, bq == bk, False, sub_rows[int(masked)]):
                    sc = lax.dot_general(q_ref[r0:r0 + nr, qc], k_ref[c0:c0 + nc, qc], _NT,
                                         preferred_element_type=F32)
                    if masked:
                        sc = jnp.where(_causal_keep(qi * bq + r0, ki * bk + c0, nr, nc, False), sc, NEG_BIG)
                    m_prev = m_sc[hh, r0:r0 + nr, :]
                    m_new = jnp.maximum(m_prev, jnp.max(sc, axis=-1, keepdims=True))
                    p = jnp.exp(sc - m_new).astype(BF16)
                    acc_sc[r0:r0 + nr, vc] = jnp.exp(m_prev - m_new) * acc_sc[r0:r0 + nr, vc] + lax.dot_general(
                        p, v_ref[c0:c0 + nc, vc], _NN, preferred_element_type=F32)
                    m_sc[hh, r0:r0 + nr, :] = m_new

        @pl.when(tab_ref[2, t] == 1)
        def _():
            step(True)

        @pl.when(tab_ref[2, t] == 0)
        def _():
            step(False)

        @pl.when(tab_ref[4, t] == 1)
        def _():
            for hh in range(hps):
                qc, vc = slice(hh * da, (hh + 1) * da), slice(hh * dv, (hh + 1) * dv)
                acc = acc_sc[:, vc]
                lane = lax.broadcasted_iota(jnp.int32, acc.shape, 1)
                l = jnp.sum(jnp.where(lane == l_col, acc, 0.0), axis=-1, keepdims=True)
                o_ref[:, vc] = (acc / l).astype(o_ref.dtype)
                lse = m_sc[hh] + jnp.log(l)
                qb_ref[:, qc] = _place3(q_ref[:, qc], lse_col, _split3_cols(lse), -1.0)

    qspec, kspec = _attn_specs(bq, bk)
    return pl.pallas_call(
        body, name=name,
        grid_spec=pltpu.PrefetchScalarGridSpec(
            num_scalar_prefetch=1, grid=(heads // hps, tab.shape[1]),
            in_specs=[qspec(hps * da), kspec(hps * da), kspec(hps * dv)],
            out_specs=[qspec(hps * dv), qspec(hps * da)],
            scratch_shapes=[pltpu.VMEM((hps, bq, 1), F32), pltpu.VMEM((bq, hps * dv), F32)]),
        out_shape=[jax.ShapeDtypeStruct((s, heads * dv), BF16), jax.ShapeDtypeStruct((s, heads * da), BF16)],
        compiler_params=_params(("parallel", "arbitrary")),
    )(tab, qa, ka, va)


def _delta_place(do, o, heads, delta_col, name):
    s = o.shape[0]
    dv = o.shape[1] // heads
    bs = _blk(s, 512, 8)
    hpb = max(1, min(heads, 1024 // dv))
    while heads % hpb:
        hpb -= 1

    def body(do_ref, o_ref, out_ref):
        for hh in range(hpb):
            vc = slice(hh * dv, (hh + 1) * dv)
            dov = do_ref[:, vc]
            delta = jnp.sum(dov.astype(F32) * o_ref[:, vc].astype(F32), axis=-1, keepdims=True)
            out_ref[:, vc] = _place3(dov, delta_col, _split3_cols(delta), 1.0)

    blk = pl.BlockSpec((bs, hpb * dv), lambda i, hh: (i, hh))
    return pl.pallas_call(
        body, name=name, grid=(s // bs, heads // hpb), in_specs=[blk, blk], out_specs=blk,
        out_shape=jax.ShapeDtypeStruct(do.shape, BF16),
        compiler_params=_params(("parallel", "parallel")),
    )(do, o)


_TN =(((0,), (0,)), ((), ()))


def _flash_bwd(qa, ka, va, doa, heads, hps, name):
    s = qa.shape[0]
    da, dv = qa.shape[1] // heads, va.shape[1] // heads
    h = heads // hps
    bq, bk = _blk(s, ATTN_BLOCK_Q), _blk(s, ATTN_BLOCK_K)
    tab = _causal_table(s, bq, bk, False)
    n_tiles = tab.shape[1]

    def body(tab_ref, q_ref, k_ref, v_ref, do_ref, dq_ref, dk_ref, dv_ref, dk_sc, dv_sc):
        t = pl.program_id(1)
        qi, ki = tab_ref[0, t], tab_ref[1, t]

        @pl.when(t == 0)
        def _():
            dq_ref[...] = jnp.zeros_like(dq_ref)

        @pl.when(tab_ref[3, t] == 1)
        def _():
            dk_sc[...] = jnp.zeros_like(dk_sc)
            dv_sc[...] = jnp.zeros_like(dv_sc)

        def step(masked):
            for hh in range(hps):
                qc, vc = slice(hh * da, (hh + 1) * da), slice(hh * dv, (hh + 1) * dv)
                for r0, nr, c0, nc in _sub_tiles(bk, bq, masked, bq == bk, True, ATTN_SUB_ROWS):
                    qv, dov, kv = q_ref[c0:c0 + nc, qc], do_ref[c0:c0 + nc, vc], k_ref[r0:r0 + nr, qc]
                    st = lax.dot_general(kv, qv, _NT, preferred_element_type=F32)
                    if masked:
                        st = jnp.where(_causal_keep(qi * bq + c0, ki * bk + r0, nc, nr, True), st, NEG_BIG)
                    pt = jnp.exp(st)
                    dv_sc[r0:r0 + nr, vc] += lax.dot_general(pt.astype(BF16), dov, _NN, preferred_element_type=F32)
                    dpt = lax.dot_general(v_ref[r0:r0 + nr, vc], dov, _NT, preferred_element_type=F32)
                    dst = (pt * dpt).astype(BF16)
                    dk_sc[r0:r0 + nr, qc] += lax.dot_general(dst, qv, _NN, preferred_element_type=F32)
                    q_rows = pl.ds(pl.multiple_of(qi * bq + c0, ATTN_SUB_ROWS), nc)
                    dq_ref[q_rows, qc] += lax.dot_general(dst, kv, _TN, preferred_element_type=F32)

        @pl.when(tab_ref[2, t] == 1)
        def _():
            step(True)

        @pl.when(tab_ref[2, t] == 0)
        def _():
            step(False)

        @pl.when(tab_ref[4, t] == 1)
        def _():
            dk_ref[...] = dk_sc[...]
            dv_ref[...] = dv_sc[...]

    qspec, kspec = _attn_specs(bq, bk)
    return pl.pallas_call(
        body, name=name,
        grid_spec=pltpu.PrefetchScalarGridSpec(
            num_scalar_prefetch=1, grid=(h, n_tiles),
            in_specs=[qspec(hps * da), kspec(hps * da), kspec(hps * dv), qspec(hps * dv)],
            out_specs=[pl.BlockSpec((s, hps * da), lambda hh, t, tb: (0, hh), pipeline_mode=pl.Buffered(1)),
                       kspec(hps * da), kspec(hps * dv)],
            scratch_shapes=[pltpu.VMEM((bk, hps * da), F32), pltpu.VMEM((bk, hps * dv), F32)]),
        out_shape=[jax.ShapeDtypeStruct((s, heads * da), F32), jax.ShapeDtypeStruct((s, heads * da), F32),
                   jax.ShapeDtypeStruct((s, heads * dv), F32)],
        compiler_params=_params(("parallel", "arbitrary")),
    )(tab, qa, ka, va, doa)


def _split3(x):
    hi = lax.reduce_precision(x, 8, 7)
    rest = x - hi
    mid = lax.reduce_precision(rest, 8, 7)
    lo = lax.reduce_precision(rest - mid, 8, 7)
    return jnp.stack([hi, mid, lo], axis=-1).astype(BF16)


def _pad_heads(w, heads, width, axis):
    shape = list(w.shape)
    d = shape[axis] // heads
    w = w.reshape(shape[:axis] + [heads, d] + shape[axis + 1:])
    pad = [(0, 0)] * w.ndim
    pad[axis + 1] = (0, width - d)
    return jnp.pad(w, pad).reshape(shape[:axis] + [heads * width] + shape[axis + 1:])


def _unpad_heads(w, heads, d, axis):
    shape = list(w.shape)
    width = shape[axis] // heads
    w = w.reshape(shape[:axis] + [heads, width] + shape[axis + 1:])
    w = lax.slice_in_dim(w, 0, d, axis=axis + 1)
    return w.reshape(shape[:axis] + [heads * d] + shape[axis + 1:])


def _placement(rows, heads, width, entries):
    e = np.zeros((rows, heads * width), np.float32)
    for row, col, val in entries:
        for hh in range(heads):
            e[row(hh) if callable(row) else row, hh * width + col] = val
    return jnp.asarray(e, BF16)


def _rope_mix(a, b, cos_t, sin_t, scale, heads, name):
    s = a.shape[0]
    d = a.shape[1] // heads
    bs = _blk(s, 1024, 8)

    def body(a_ref, b_ref, c_ref, s_ref, o_ref):
        o_ref[...] = ((a_ref[...] * c_ref[...] + b_ref[...] * s_ref[...]) * scale).astype(o_ref.dtype)

    blk = pl.BlockSpec((bs, d), lambda i, hh: (i, hh))
    tbl = pl.BlockSpec((bs, d), lambda i, hh: (i, 0))
    return pl.pallas_call(
        body, name=name, grid=(s // bs, heads), in_specs=[blk, blk, tbl, tbl], out_specs=blk,
        out_shape=jax.ShapeDtypeStruct(a.shape, BF16),
        compiler_params=_params(("parallel", "parallel")),
    )(a, b, cos_t, sin_t)


def _rope_unmix(g, cos_t, sin_t, scale, heads, name):
    s = g.shape[0]
    d = g.shape[1] // heads
    bs = _blk(s, 1024, 8)

    def body(g_ref, c_ref, s_ref, da_ref, db_ref):
        gv = g_ref[...] * scale
        da_ref[...] = (gv * c_ref[...]).astype(da_ref.dtype)
        db_ref[...] = (gv * s_ref[...]).astype(db_ref.dtype)

    blk = pl.BlockSpec((bs, d), lambda i, hh: (i, hh))
    tbl = pl.BlockSpec((bs, d), lambda i, hh: (i, 0))
    return pl.pallas_call(
        body, name=name, grid=(s // bs, heads), in_specs=[blk, tbl, tbl], out_specs=[blk, blk],
        out_shape=[jax.ShapeDtypeStruct(g.shape, BF16)] * 2,
        compiler_params=_params(("parallel", "parallel")),
    )(g, cos_t, sin_t)


def _adamw(w, g, m, v, name):
    r, wd = w.shape
    br = _blk(r, 512, 8)

    def body(w_ref, g_ref, m_ref, v_ref, d_ref, nm_ref, nv_ref):
        gv = g_ref[...]
        mn = ADAM_B1 * m_ref[...] + (1.0 - ADAM_B1) * gv
        vn = ADAM_B2 * v_ref[...] + (1.0 - ADAM_B2) * (gv * gv)
        m_hat = mn / (1.0 - ADAM_B1 ** ADAM_STEP)
        v_hat = vn / (1.0 - ADAM_B2 ** ADAM_STEP)
        d_ref[...] = -ADAM_LR * (m_hat / (jnp.sqrt(v_hat) + ADAM_EPS) + ADAM_WD * w_ref[...])
        nm_ref[...] = mn
        nv_ref[...] = vn

    row = pl.BlockSpec((br, wd), lambda i: (i, 0))
    return pl.pallas_call(
        body, name=name, grid=(r // br,), in_specs=[row] * 4, out_specs=[row] * 3,
        out_shape=[jax.ShapeDtypeStruct((r, wd), F32)] * 3,
        compiler_params=_params(("parallel",)),
    )(w, g, m, v)


_ANY = pl.BlockSpec(memory_space=pl.ANY)


def _place():
    x, y, c = lax.axis_index("x"), lax.axis_index("y"), lax.axis_index("c")
    chips = [(x, 1 - y), (1 - x, y), (1 - x, 1 - y)]
    return x, y, c, chips


def _all_gather_shards(shard, name):
    r, w = shard.shape
    hr = r // 2
    qr = hr // 2

    def body(x_ref, out_ref, send_sems, recv_sems):
        x, y, c, _ = _place()
        me, sibling, y_nbr, x_nbr = (x, y, c), (x, y, 1 - c), (x, 1 - y, c), (1 - x, y, c)

        def rows(j, half, piece=None):
            if piece is None:
                return out_ref.at[j, pl.ds(pl.multiple_of(half * hr, 16), hr), :]
            return out_ref.at[j, pl.ds(pl.multiple_of(half * hr + piece * qr, 16), qr), :]

        def mine(piece):
            return x_ref.at[pl.ds(pl.multiple_of(c * hr + piece * qr, 16), qr), :]

        def copy(sem, src, dst, to):
            return pltpu.make_async_remote_copy(src_ref=src, dst_ref=dst, send_sem=send_sems.at[sem],
                                                recv_sem=recv_sems.at[sem], device_id=to, device_id_type=MESH)

        sent = [copy(0, mine(0), rows(0, c, 0), y_nbr), copy(1, mine(1), rows(0, c, 1), y_nbr),
                copy(2, mine(0), rows(1, c, 0), x_nbr), copy(3, mine(1), rows(1, c, 1), x_nbr)]
        for cp in sent:
            cp.start()

        def landed(sem, ref):
            copy(sem, ref, ref, me).wait_recv()

        def pass_on(sem, src, dst, to):
            cp = copy(sem, src, dst, to)
            cp.start()
            sent.append(cp)

        landed(2, rows(1, c, 0))
        pass_on(4, rows(1, c, 0), rows(2, c, 0), y_nbr)
        landed(1, rows(0, c, 1))
        pass_on(5, rows(0, c, 1), rows(2, c, 1), x_nbr)
        landed(0, rows(0, c, 0))
        pass_on(6, rows(0, c), rows(0, c), sibling)
        landed(3, rows(1, c, 1))
        pass_on(7, rows(1, c), rows(1, c), sibling)
        landed(4, rows(2, c, 0))
        landed(5, rows(2, c, 1))
        pass_on(8, rows(2, c), rows(2, c), sibling)
        for j in range(3):
            landed(6 + j, rows(j, 1 - c))
        for cp in sent:
            cp.wait_send()

    return pl.pallas_call(
        body, name=name, in_specs=[_ANY], out_specs=_ANY,
        out_shape=jax.ShapeDtypeStruct((N_CHIPS - 1, r, w), shard.dtype),
        scratch_shapes=[pltpu.SemaphoreType.DMA((9,)), pltpu.SemaphoreType.DMA((9,))],
        compiler_params=pltpu.CompilerParams(vmem_limit_bytes=VMEM_LIMIT_BYTES),
    )(shard)


def _sibling_swap_halves(g, name):
    nq, r, w = g.shape
    hr = r // 2

    def body(g_ref, a_ref, send_sems, recv_sems):
        x, y, c, _ = _place()
        sibling = (x, y, 1 - c)
        cps = []
        for q in range(nq):
            cp = pltpu.make_async_remote_copy(
                src_ref=g_ref.at[q, pl.ds(pl.multiple_of((1 - c) * hr, 8), hr), :], dst_ref=a_ref.at[q],
                send_sem=send_sems.at[q], recv_sem=recv_sems.at[q], device_id=sibling, device_id_type=MESH)
            cp.start()
            cps.append(cp)
        for cp in cps:
            cp.wait()

    return pl.pallas_call(
        body, name=name, in_specs=[_ANY], out_specs=_ANY,
        out_shape=jax.ShapeDtypeStruct((nq, hr, w), g.dtype),
        scratch_shapes=[pltpu.SemaphoreType.DMA((nq,)), pltpu.SemaphoreType.DMA((nq,))],
        compiler_params=pltpu.CompilerParams(vmem_limit_bytes=VMEM_LIMIT_BYTES),
    )(g)


def _chip_sum(g, a, c_idx, name):
    nq, r, w = g.shape
    hr = r // 2
    br = _blk(hr, 512, 16)
    nb = hr // br

    def body(c_ref, g_ref, a_ref, o_ref):
        o_ref[...] = (g_ref[...] + a_ref[...]).astype(o_ref.dtype)

    return pl.pallas_call(
        body, name=name,
        grid_spec=pltpu.PrefetchScalarGridSpec(
            num_scalar_prefetch=1, grid=(nq, nb),
            in_specs=[pl.BlockSpec((None, br, w), lambda q, i, cr: (q, cr[0] * nb + i, 0)),
                      pl.BlockSpec((None, br, w), lambda q, i, cr: (q, i, 0))],
            out_specs=pl.BlockSpec((None, br, w), lambda q, i, cr: (q, i, 0))),
        out_shape=jax.ShapeDtypeStruct((nq, hr, w), BF16),
        compiler_params=_params(("parallel", "parallel")),
    )(c_idx, g, a)


def _chip_exchange(s4, name):
    nq, hr, w = s4.shape

    def body(s_ref, b_ref, send_sems, recv_sems):
        x, y, c, chips = _place()
        cps = []
        for j, (cx, cy) in enumerate(chips):
            cp = pltpu.make_async_remote_copy(
                src_ref=s_ref.at[2 * cx + cy], dst_ref=b_ref.at[j],
                send_sem=send_sems.at[j], recv_sem=recv_sems.at[j], device_id=(cx, cy, c), device_id_type=MESH)
            cp.start()
            cps.append(cp)
        for cp in cps:
            cp.wait()

    return pl.pallas_call(
        body, name=name, in_specs=[_ANY], out_specs=_ANY,
        out_shape=jax.ShapeDtypeStruct((nq - 1, hr, w), s4.dtype),
        scratch_shapes=[pltpu.SemaphoreType.DMA((3,)), pltpu.SemaphoreType.DMA((3,))],
        compiler_params=pltpu.CompilerParams(vmem_limit_bytes=VMEM_LIMIT_BYTES),
    )(s4)


def _sum_chips(s4, b3, p_idx, name):
    _, hr, w = s4.shape
    nb3 = b3.shape[0]
    br = _blk(hr, 512, 16)

    def body(p_ref, s_ref, b_ref, o_ref):
        acc = s_ref[...].astype(F32)
        for j in range(nb3):
            acc = acc + b_ref[j].astype(F32)
        o_ref[...] = acc

    return pl.pallas_call(
        body, name=name,
        grid_spec=pltpu.PrefetchScalarGridSpec(
            num_scalar_prefetch=1, grid=(hr // br,),
            in_specs=[pl.BlockSpec((None, br, w), lambda i, pr: (pr[0], i, 0)),
                      pl.BlockSpec((nb3, br, w), lambda i, pr: (0, i, 0))],
            out_specs=pl.BlockSpec((br, w), lambda i, pr: (i, 0))),
        out_shape=jax.ShapeDtypeStruct((hr, w), F32),
        compiler_params=_params(("parallel",)),
    )(p_idx, s4, b3)


def _sibling_swap(t, name):
    hr, w = t.shape

    def body(t_ref, o_ref, send_sem, recv_sem):
        x, y, c, _ = _place()
        cp = pltpu.make_async_remote_copy(src_ref=t_ref, dst_ref=o_ref, send_sem=send_sem, recv_sem=recv_sem,
                                          device_id=(x, y, 1 - c), device_id_type=MESH)
        cp.start()
        cp.wait()

    return pl.pallas_call(
        body, name=name, in_specs=[_ANY], out_specs=_ANY,
        out_shape=jax.ShapeDtypeStruct((hr, w), t.dtype),
        scratch_shapes=[pltpu.SemaphoreType.DMA, pltpu.SemaphoreType.DMA],
        compiler_params=pltpu.CompilerParams(vmem_limit_bytes=VMEM_LIMIT_BYTES),
    )(t)


def _all_reduce_small(v, name):
    r, w = v.shape

    def body(v_ref, o_ref, slots, send_sems, recv_sems):
        x, y, c, _ = _place()
        me = 4 * x + 2 * y + c
        slots[me] = v_ref[...]
        cps = []
        for k in range(1, N_DEV):
            fx, fy, fc = (k >> 2) & 1, (k >> 1) & 1, k & 1
            to = (x ^ fx, y ^ fy, c ^ fc)
            cp = pltpu.make_async_remote_copy(
                src_ref=v_ref, dst_ref=slots.at[me], send_sem=send_sems.at[k - 1], recv_sem=recv_sems.at[k - 1],
                device_id=to, device_id_type=MESH)
            cp.start()
            cps.append(cp)
        for k in range(1, N_DEV):
            fx, fy, fc = (k >> 2) & 1, (k >> 1) & 1, k & 1
            src_dev = 4 * (x ^ fx) + 2 * (y ^ fy) + (c ^ fc)
            pltpu.make_async_remote_copy(
                src_ref=v_ref, dst_ref=slots.at[src_dev], send_sem=send_sems.at[k - 1],
                recv_sem=recv_sems.at[k - 1], device_id=(x, y, c), device_id_type=MESH).wait_recv()
        for cp in cps:
            cp.wait_send()
        acc = slots[0]
        for d in range(1, N_DEV):
            acc = acc + slots[d]
        o_ref[...] = acc

    return pl.pallas_call(
        body, name=name,
        in_specs=[pl.BlockSpec(memory_space=pltpu.VMEM)], out_specs=pl.BlockSpec(memory_space=pltpu.VMEM),
        out_shape=jax.ShapeDtypeStruct((r, w), F32),
        scratch_shapes=[pltpu.VMEM((N_DEV, r, w), F32), pltpu.SemaphoreType.DMA((N_DEV - 1,)),
                        pltpu.SemaphoreType.DMA((N_DEV - 1,))],
        compiler_params=pltpu.CompilerParams(vmem_limit_bytes=VMEM_LIMIT_BYTES),
    )(v)


def _part_rows(shape, part_rows=PACK_PART_ROWS):
    assert shape[-1] <= PACK_LANES
    return _round_up(math.prod(shape[:-1]), part_rows)


def _packed_rows(shapes):
    return _round_up(sum(_part_rows(s) for s in shapes), PACK_ROWS_MULT)


def _pack(arrs, total_rows, dtype, part_rows=PACK_PART_ROWS):
    parts = []
    for a in arrs:
        a2 = a.reshape(-1, a.shape[-1]).astype(dtype)
        rows = _part_rows(a.shape, part_rows)
        parts.append(jnp.pad(a2, ((0, rows - a2.shape[0]), (0, PACK_LANES - a2.shape[1]))))
    used = sum(p.shape[0] for p in parts)
    if total_rows > used:
        parts.append(jnp.zeros((total_rows - used, PACK_LANES), dtype))
    return jnp.concatenate(parts, axis=0)


def _unpack(packed, shapes, part_rows=PACK_PART_ROWS):
    out, r0 = [], 0
    for s in shapes:
        out.append(packed[r0:r0 + math.prod(s[:-1]), :s[-1]].reshape(s))
        r0 += _part_rows(s, part_rows)
    return out


_BIG = (("fox_w_in", 2), ("fox_w_out", 1), ("mla_w_kv_a", 0), ("mla_w_kv_b", 1), ("mla_w_q_a", 1),
        ("mla_w_q_b", 2), ("mla_w_out", 1), ("ffn_w_up", 2), ("ffn_w_down", 1))
_SMALL = ("norm_mix_g", "norm_ffn_g", "fox_b_f", "kv_norm_g", "mla_kv_a_norm_g", "mla_q_a_norm_g", "final_norm_g")
_WEIGHTS = ("norm_mix_g", "norm_ffn_g", "fox_w_in", "fox_b_f", "fox_w_out", "kv_norm_g", "mla_w_kv_a",
            "mla_kv_a_norm_g", "mla_w_kv_b", "mla_w_q_a", "mla_q_a_norm_g", "mla_w_q_b", "mla_w_out",
            "ffn_w_up", "ffn_w_down", "final_norm_g")


def _ffn_fwd(x, h, w_up, w_down, tag):
    def relu_sq(acc):
        r = jnp.maximum(acc, 0.0)
        return r, r * r

    r, a = _matmul(h, w_up, mode="nn", out_dtypes=(BF16, BF16), epilogue=relu_sq, name=f"{tag}_up")
    x_out = _matmul(a, w_down, mode="nn", out_dtypes=(F32,), epilogue=lambda acc, res: (acc + res,),
                    extras=(x,), name=f"{tag}_down")
    return x_out, r, a


def _ffn_bwd(dx_out, x_in, h, r, a, g_norm, w_up, w_down, tag):
    d_u = _matmul(dx_out, w_down, mode="nt", out_dtypes=(BF16,), epilogue=lambda acc, rr: (acc * (2.0 * rr.astype(F32)),),
                  extras=(r,), name=f"{tag}_d_act")
    d_w_down = _matmul(a, dx_out, mode="tn", out_dtypes=(F32,), name=f"{tag}_d_w_down")
    d_w_up = _matmul(h, d_u, mode="tn", out_dtypes=(F32,), name=f"{tag}_d_w_up")
    d_h = _matmul(d_u, w_up, mode="nt", out_dtypes=(F32,), name=f"{tag}_d_h")
    dx_in, (d_g,) = _rms_bwd(x_in, [(g_norm, d_h)], dx_out, name=f"{tag}_d_norm")
    return dx_in, d_w_up, d_w_down, d_g


def kernel(x, norm_mix_g, norm_ffn_g, fox_w_in, fox_b_f, fox_w_out, kv_norm_g, mla_w_kv_a, mla_kv_a_norm_g, mla_w_kv_b, mla_w_q_a, mla_q_a_norm_g, mla_w_q_b, mla_w_out, ffn_w_up, ffn_w_down, final_norm_g, loss_target, m_norm_mix_g, m_norm_ffn_g, m_fox_w_in, m_fox_b_f, m_fox_w_out, m_kv_norm_g, m_mla_w_kv_a, m_mla_kv_a_norm_g, m_mla_w_kv_b, m_mla_w_q_a, m_mla_q_a_norm_g, m_mla_w_q_b, m_mla_w_out, m_ffn_w_up, m_ffn_w_down, m_final_norm_g, v_norm_mix_g, v_norm_ffn_g, v_fox_w_in, v_fox_b_f, v_fox_w_out, v_kv_norm_g, v_mla_w_kv_a, v_mla_kv_a_norm_g, v_mla_w_kv_b, v_mla_w_q_a, v_mla_q_a_norm_g, v_mla_w_q_b, v_mla_w_out, v_ffn_w_up, v_ffn_w_down, v_final_norm_g):
    args = dict(locals())
    w_in = {n: args[n] for n in _WEIGHTS}
    m_in = {n: args["m_" + n] for n in _WEIGHTS}
    v_in = {n: args["v_" + n] for n in _WEIGHTS}

    xs = x[0]
    seq, d_model = xs.shape
    tgt = loss_target[0]
    fox_h, mla_h, nope = FOX_HEADS, MLA_HEADS, QK_NOPE_DIM
    kv_rank = mla_kv_a_norm_g.shape[0]
    rope = mla_w_kv_a.shape[1] - kv_rank
    half = rope // 2
    q_rank = mla_q_a_norm_g.shape[1]
    v_dim = mla_w_kv_b.shape[1] * N_CHIPS // mla_h - nope
    fox_w = fox_w_out.shape[1] * N_CHIPS
    fox_dh = fox_w // fox_h

    big_names = [n for n, _ in _BIG]
    shard_shapes = [w_in[n].shape for n in big_names]
    rows = _packed_rows(shard_shapes)
    my_shard = _pack([w_in[n] for n in big_names], rows, BF16)
    others = _all_gather_shards(my_shard, name="gather_weights")
    by_relation = jnp.concatenate([my_shard[None], others], axis=0)
    p_chip = 2 * lax.axis_index("x") + lax.axis_index("y")
    full = {}
    for q in range(N_CHIPS):
        shard_q = lax.dynamic_index_in_dim(by_relation, p_chip ^ q, axis=0, keepdims=False)
        for (n, ax), piece in zip(_BIG, _unpack(shard_q, shard_shapes)):
            full.setdefault(n, []).append(piece)
    full = {n: jnp.concatenate(full[n], axis=ax) for n, ax in _BIG}

    fox_scale = fox_dh ** -0.5
    fox_wd = _round_up(fox_dh + 9, LANE_TILE)
    fox_vwd = _round_up(fox_dh + 4, LANE_TILE)
    w_fox_in = full["fox_w_in"][0]
    w_fq = _pad_heads(w_fox_in[:, :fox_w] * fox_scale, fox_h, fox_wd, 1)
    w_fk = _pad_heads(w_fox_in[:, fox_w:2 * fox_w], fox_h, fox_wd, 1)
    w_fv = _pad_heads(w_fox_in[:, 2 * fox_w:3 * fox_w], fox_h, fox_vwd, 1)
    w_gate = w_fox_in[:, 3 * fox_w:]
    w_fox_out = _pad_heads(full["fox_w_out"][0], fox_h, fox_vwd, 0)
    n_cx = _round_up(3 * fox_h + 1, LANE_TILE)
    c_piece = lambda i: (lambda hh: 3 * hh + i)
    one_col = 3 * fox_h
    e_fq = _placement(n_cx, fox_h, fox_wd, [(c_piece(i), fox_dh + i, 1.0) for i in range(3)]
                      + [(one_col, fox_dh + 3 + i, 1.0) for i in range(3)])
    e_fk = _placement(n_cx, fox_h, fox_wd, [(one_col, fox_dh + i, 1.0) for i in range(3)]
                      + [(c_piece(i), fox_dh + 3 + i, -1.0) for i in range(3)]
                      + [(one_col, fox_dh + 6 + i, 1.0) for i in range(3)])
    e_fv = _placement(n_cx, fox_h, fox_vwd, [(one_col, fox_dh + i, -1.0) for i in range(3)]
                      + [(one_col, fox_dh + 3, 1.0)])

    mla_scale = (nope + rope) ** -0.5
    mla_dk = nope + rope
    mla_wd = _round_up(mla_dk + 3, LANE_TILE)
    mla_vwd = _round_up(v_dim + 4, LANE_TILE)
    w_kv_a = full["mla_w_kv_a"]
    w_kv_b3 = full["mla_w_kv_b"].reshape(kv_rank, mla_h, nope + v_dim)
    w_kn = _pad_heads(w_kv_b3[:, :, :nope].reshape(kv_rank, -1), mla_h, mla_wd, 1)
    w_mv = _pad_heads(w_kv_b3[:, :, nope:].reshape(kv_rank, -1), mla_h, mla_vwd, 1)
    w_q_a = full["mla_w_q_a"][0]
    w_q_b3 = full["mla_w_q_b"][0].reshape(q_rank, mla_h, nope + rope)
    w_qa_ = _pad_heads(w_q_b3.reshape(q_rank, -1), mla_h, mla_wd, 1)
    w_qb_ = _pad_heads(jnp.concatenate([jnp.zeros_like(w_q_b3[:, :, :nope]), -w_q_b3[:, :, nope + half:],
                                        w_q_b3[:, :, nope:nope + half]], axis=-1).reshape(q_rank, -1),
                       mla_h, mla_wd, 1)
    w_mla_out = _pad_heads(full["mla_w_out"][0], mla_h, mla_vwd, 0)
    w_up, w_down = full["ffn_w_up"], full["ffn_w_down"]
    n_kx = _round_up(rope + 1, LANE_TILE)
    e_mk = _placement(n_kx, mla_h, mla_wd, [(j, nope + j, 1.0) for j in range(rope)]
                      + [(rope, mla_dk + i, 1.0) for i in range(3)])
    e_mv = _placement(n_kx, mla_h, mla_vwd, [(rope, v_dim + i, -1.0) for i in range(3)] + [(rope, v_dim + 3, 1.0)])
    e_kr_u = _placement(n_kx, mla_h, mla_wd, [(j, nope + j, 1.0) for j in range(rope)]).T
    e_kr_v = _placement(n_kx, mla_h, mla_wd, [(j, nope + half + j, 1.0) for j in range(half)]
                        + [(half + j, nope + j, -1.0) for j in range(half)]).T

    inv = 1.0 / (ROPE_BASE ** (jnp.arange(0, rope, 2, dtype=F32) / rope))
    ang = jnp.arange(seq, dtype=F32)[:, None] * inv[None, :]
    cos, sin = jnp.cos(ang), jnp.sin(ang)
    pad_t = jnp.zeros((seq, mla_wd - mla_dk), F32)
    cos_t = jnp.concatenate([jnp.ones((seq, nope), F32), cos, cos, pad_t], axis=1)
    sin_t = jnp.concatenate([jnp.zeros((seq, nope), F32), sin, sin, pad_t], axis=1)
    pad_k = jnp.zeros((seq, n_kx - rope), F32)
    cos_k, sin_k = jnp.concatenate([cos, cos, pad_k], axis=1), jnp.concatenate([sin, sin, pad_k], axis=1)

    (h0,) = _rms_fwd(xs, norm_mix_g[0:1], name="l0_norm_mix")
    gate = _matmul(h0, w_gate, mode="nn", out_dtypes=(F32,), name="fox_gate")
    z = gate + fox_b_f[0][None, :]
    cum = jnp.cumsum(jax.nn.log_sigmoid(z), axis=0)
    cx = jnp.concatenate([_split3(cum).reshape(seq, 3 * fox_h), jnp.ones((seq, 1), BF16),
                          jnp.zeros((seq, n_cx - 3 * fox_h - 1), BF16)], axis=1)
    fqa = _matmul(h0, w_fq, mode="nn", out_dtypes=(BF16,), placed=(cx, e_fq), name="fox_q")
    fka = _matmul(h0, w_fk, mode="nn", out_dtypes=(BF16,), placed=(cx, e_fk), name="fox_k")
    fva = _matmul(h0, w_fv, mode="nn", out_dtypes=(BF16,), placed=(cx, e_fv), name="fox_v")
    foa, fqb = _flash_fwd(fqa, fka, fva, fox_h, fox_dh + 3, fox_dh + 6, FOX_FWD_SUB_ROWS, name="fox_attn")
    add_res = lambda acc, res: (acc + res,)
    x1 = _matmul(foa, w_fox_out, mode="nn", out_dtypes=(F32,), epilogue=add_res, extras=(xs,), name="fox_out")
    (h1,) = _rms_fwd(x1, norm_ffn_g[0:1], name="l0_norm_ffn")
    x2, r0, a0 = _ffn_fwd(x1, h1, w_up[0], w_down[0], "ffn0")

    src, h2 = _rms_fwd(x2, jnp.stack([kv_norm_g, norm_mix_g[1]]), name="l1_norm_kv_mix")
    kv_a = _matmul(src, w_kv_a, mode="nn", out_dtypes=(F32,), name="mla_kv_a")
    (c_kv,) = _rms_fwd(kv_a, mla_kv_a_norm_g[None, :], name="mla_norm_kv_a")
    kr1, kr2 = _rope(kv_a[None, :, kv_rank:kv_rank + half], kv_a[None, :, kv_rank + half:], cos, sin, 1.0,
                     name="mla_rope_k")
    krx = jnp.concatenate([kr1.astype(BF16), kr2.astype(BF16), jnp.ones((seq, 1), BF16),
                           jnp.zeros((seq, n_kx - rope - 1), BF16)], axis=1)
    mka = _matmul(c_kv, w_kn, mode="nn", out_dtypes=(BF16,), placed=(krx, e_mk), name="mla_k")
    mva = _matmul(c_kv, w_mv, mode="nn", out_dtypes=(BF16,), placed=(krx, e_mv), name="mla_v")
    cq_pre = _matmul(h2, w_q_a, mode="nn", out_dtypes=(F32,), name="mla_q_a")
    (c_q,) = _rms_fwd(cq_pre, mla_q_a_norm_g, name="mla_norm_q_a")
    q_a_part = _matmul(c_q, w_qa_, mode="nn", out_dtypes=(F32,), name="mla_q_b_cos")
    q_b_part = _matmul(c_q, w_qb_, mode="nn", out_dtypes=(F32,), name="mla_q_b_sin")
    mqa = _rope_mix(q_a_part, q_b_part, cos_t, sin_t, mla_scale, mla_h, name="mla_rope_q")
    moa, mqb = _flash_fwd(mqa, mka, mva, mla_h, v_dim + 3, mla_dk, MLA_FWD_SUB_ROWS, name="mla_attn")
    x3 = _matmul(moa, w_mla_out, mode="nn", out_dtypes=(F32,), epilogue=add_res, extras=(x2,), name="mla_out")
    (h3,) = _rms_fwd(x3, norm_ffn_g[1:2], name="l1_norm_ffn")
    x4, r1, a1 = _ffn_fwd(x3, h3, w_up[1], w_down[1], "ffn1")

    loss_tile, dx4, d_final_g = _loss_head(x4, final_norm_g[None, :], tgt, name="loss_head")
    loss = lax.psum(loss_tile[0, 0], ("x", "y", "c"))

    gw = {}
    dx3, d_up1, d_down1, d_nf1 = _ffn_bwd(dx4, x3, h3, r1, a1, norm_ffn_g[1:2], w_up[1], w_down[1], "ffn1")

    d_mo = _matmul(dx3, w_mla_out, mode="nt", out_dtypes=(BF16,), name="mla_d_ctx")
    gw["mla_w_out"] = _unpad_heads(_matmul(moa, dx3, mode="tn", out_dtypes=(F32,), name="mla_d_w_out"),
                                   mla_h, v_dim, 0)[None]
    d_moa = _delta_place(d_mo, moa, mla_h, v_dim, name="mla_attn_delta")
    d_mqa, d_mka, d_mva = _flash_bwd(mqb, mka, mva, d_moa, mla_h, MLA_BWD_HEADS_PER_STEP,
                                     name="mla_attn_bwd")
    d_qa_part, d_qb_part = _rope_unmix(d_mqa, cos_t, sin_t, mla_scale, mla_h, name="mla_rope_dq")
    d_w_qa_ = _unpad_heads(_matmul(c_q, d_qa_part, mode="tn", out_dtypes=(F32,), name="mla_d_w_q_b_cos"),
                           mla_h, mla_dk, 1).reshape(q_rank, mla_h, mla_dk)
    d_w_qb_ = _unpad_heads(_matmul(c_q, d_qb_part, mode="tn", out_dtypes=(F32,), name="mla_d_w_q_b_sin"),
                           mla_h, mla_dk, 1).reshape(q_rank, mla_h, mla_dk)
    gw["mla_w_q_b"] = jnp.concatenate(
        [d_w_qa_[:, :, :nope], d_w_qa_[:, :, nope:nope + half] + d_w_qb_[:, :, nope + half:],
         d_w_qa_[:, :, nope + half:] - d_w_qb_[:, :, nope:nope + half]], axis=-1).reshape(1, q_rank, mla_h * mla_dk)
    d_c_q_sin = _matmul(d_qb_part, w_qb_, mode="nt", out_dtypes=(F32,), name="mla_d_c_q_sin")
    d_c_q = _matmul(d_qa_part, w_qa_, mode="nt", out_dtypes=(F32,), epilogue=add_res, extras=(d_c_q_sin,),
                    name="mla_d_c_q")
    d_cq_pre, (d_q_a_g,) = _rms_bwd(cq_pre, [(mla_q_a_norm_g, d_c_q)], None, name="mla_d_norm_q_a")
    gw["mla_w_q_a"] = _matmul(h2, d_cq_pre, mode="tn", out_dtypes=(F32,), name="mla_d_w_q_a")[None]
    d_h2 = _matmul(d_cq_pre, w_q_a, mode="nt", out_dtypes=(F32,), name="mla_d_h")

    d_w_kn = _unpad_heads(_matmul(c_kv, d_mka, mode="tn", out_dtypes=(F32,), name="mla_d_w_k"), mla_h, nope, 1)
    d_w_mv = _unpad_heads(_matmul(c_kv, d_mva, mode="tn", out_dtypes=(F32,), name="mla_d_w_v"), mla_h, v_dim, 1)
    gw["mla_w_kv_b"] = jnp.concatenate([d_w_kn.reshape(kv_rank, mla_h, nope), d_w_mv.reshape(kv_rank, mla_h, v_dim)],
                                       axis=-1).reshape(kv_rank, mla_h * (nope + v_dim))
    d_c_kv_v = _matmul(d_mva, w_mv, mode="nt", out_dtypes=(F32,), name="mla_d_c_kv_v")
    d_c_kv = _matmul(d_mka, w_kn, mode="nt", out_dtypes=(F32,), epilogue=add_res, extras=(d_c_kv_v,),
                     name="mla_d_c_kv")
    d_ckv_pre, (d_kv_a_g,) = _rms_bwd(kv_a, [(mla_kv_a_norm_g[None, :], d_c_kv)], None, name="mla_d_norm_kv_a")
    d_kr_u = _matmul(d_mka, e_kr_u, mode="nn", out_dtypes=(F32,), name="mla_d_k_rope_u")
    d_kr_v = _matmul(d_mka, e_kr_v, mode="nn", out_dtypes=(F32,), name="mla_d_k_rope_v")
    d_kr = _rope_mix(d_kr_u, d_kr_v, cos_k, sin_k, 1.0, 1, name="mla_rope_dk")
    d_kv_a = jnp.concatenate([d_ckv_pre, d_kr[:, :rope].astype(F32)], axis=1)
    gw["mla_w_kv_a"] = _matmul(src, d_kv_a, mode="tn", out_dtypes=(F32,), name="mla_d_w_kv_a")
    d_src = _matmul(d_kv_a, w_kv_a, mode="nt", out_dtypes=(F32,), name="mla_d_src")
    dx2, (d_kv_g, d_nm1) = _rms_bwd(x2, [(kv_norm_g[None, :], d_src), (norm_mix_g[1:2], d_h2)], dx3,
                                    name="l1_d_norm_kv_mix")

    dx1, d_up0, d_down0, d_nf0 = _ffn_bwd(dx2, x1, h1, r0, a0, norm_ffn_g[0:1], w_up[0], w_down[0], "ffn0")
    gw["ffn_w_up"] = jnp.stack([d_up0, d_up1])
    gw["ffn_w_down"] = jnp.stack([d_down0, d_down1])

    d_fo = _matmul(dx1, w_fox_out, mode="nt", out_dtypes=(BF16,), name="fox_d_ctx")
    gw["fox_w_out"] = _unpad_heads(_matmul(foa, dx1, mode="tn", out_dtypes=(F32,), name="fox_d_w_out"),
                                   fox_h, fox_dh, 0)[None]
    d_foa = _delta_place(d_fo, foa, fox_h, fox_dh, name="fox_attn_delta")
    d_fqa, d_fka, d_fva = _flash_bwd(fqb, fka, fva, d_foa, fox_h, FOX_BWD_HEADS_PER_STEP, name="fox_attn_bwd")
    d_cum = (d_fqa.reshape(seq, fox_h, fox_wd)[:, :, fox_dh]
             - d_fka.reshape(seq, fox_h, fox_wd)[:, :, fox_dh + 3])
    d_z = lax.cumsum(d_cum, axis=0, reverse=True) * jax.nn.sigmoid(-z)
    d_b_f = jnp.sum(d_z, axis=0)
    d_w_in = [_unpad_heads(_matmul(h0, g, mode="tn", out_dtypes=(F32,), name=f"fox_d_w_{tag}"), fox_h, fox_dh, 1)
              for tag, g in (("q", d_fqa), ("k", d_fka), ("v", d_fva))]
    d_w_gate = _matmul(h0, d_z, mode="tn", out_dtypes=(F32,), name="fox_d_w_gate")
    gw["fox_w_in"] = jnp.concatenate([d_w_in[0] * fox_scale, d_w_in[1], d_w_in[2], d_w_gate], axis=1)[None]
    d_h0 = _matmul(d_z, w_gate, mode="nt", out_dtypes=(F32,), name="fox_d_h_gate")
    for tag, g, w in (("q", d_fqa, w_fq), ("k", d_fka, w_fk), ("v", d_fva, w_fv)):
        d_h0 = _matmul(g, w, mode="nt", out_dtypes=(F32,), epilogue=add_res, extras=(d_h0,), name=f"fox_d_h_{tag}")
    grad_x, (d_nm0,) = _rms_bwd(xs, [(norm_mix_g[0:1], d_h0)], dx1, name="l0_d_norm_mix")

    c_idx = lax.axis_index("c").astype(jnp.int32).reshape(1)
    parts = []
    for (n, ax), shape in zip(_BIG, shard_shapes):
        g = gw[n]
        g = jnp.moveaxis(g.reshape(g.shape[:ax] + (N_CHIPS, shape[ax]) + g.shape[ax + 1:]), ax, 0)
        g = g.reshape(N_CHIPS, -1, shape[-1])
        parts.append(jnp.pad(g, ((0, 0), (0, _part_rows(shape) - g.shape[1]), (0, PACK_LANES - shape[-1]))))
    parts.append(jnp.zeros((N_CHIPS, rows - sum(p.shape[1] for p in parts), PACK_LANES), F32))
    g4 = jnp.concatenate(parts, axis=1)
    a4 = _sibling_swap_halves(g4, name="grads_to_sibling")
    s4 = _chip_sum(g4, a4, c_idx, name="grads_chip_sum")
    b3 = _chip_exchange(s4, name="grads_between_chips")
    t_mine = _sum_chips(s4, b3, p_chip.astype(jnp.int32).reshape(1), name="grads_sum_chips")
    t_theirs = _sibling_swap(t_mine, name="grads_join_halves")
    is_south = lax.axis_index("c") == 0
    g_big = jnp.concatenate([jnp.where(is_south, t_mine, t_theirs), jnp.where(is_south, t_theirs, t_mine)],
                            axis=0)

    small_local = {"norm_mix_g": jnp.concatenate([d_nm0, d_nm1], axis=0),
                   "norm_ffn_g": jnp.concatenate([d_nf0, d_nf1], axis=0),
                   "fox_b_f": d_b_f[None, :], "kv_norm_g": d_kv_g[0], "mla_kv_a_norm_g": d_kv_a_g[0],
                   "mla_q_a_norm_g": d_q_a_g, "final_norm_g": d_final_g[0]}
    small_shapes = [w_in[n].shape for n in _SMALL]
    small_rows = sum(_part_rows(s, SMALL_PART_ROWS) for s in small_shapes)
    pack_small = lambda arrs: _pack(arrs, small_rows, F32, SMALL_PART_ROWS)
    g_small = _all_reduce_small(pack_small([small_local[n] for n in _SMALL]), name="grads_small")

    d_big, nm_big, nv_big = _adamw(_pack([w_in[n] for n in big_names], rows, F32), g_big,
                                   _pack([m_in[n] for n in big_names], rows, F32),
                                   _pack([v_in[n] for n in big_names], rows, F32), name="adamw_big")
    d_sm, nm_sm, nv_sm = _adamw(pack_small([w_in[n] for n in _SMALL]), g_small,
                                pack_small([m_in[n] for n in _SMALL]),
                                pack_small([v_in[n] for n in _SMALL]), name="adamw_small")

    def spread(big, small):
        out = dict(zip(big_names, _unpack(big, shard_shapes)))
        out.update(zip(_SMALL, _unpack(small, small_shapes, SMALL_PART_ROWS)))
        return [out[n] for n in _WEIGHTS]

    return (loss, grad_x[None], *spread(g_big, g_small), *spread(d_big, d_sm), *spread(nm_big, nm_sm),
            *spread(nv_big, nv_sm))
```

```python
import math

import numpy as np
import jax
import jax.numpy as jnp
from jax import lax
from jax.experimental import pallas as pl
from jax.experimental.pallas import tpu as pltpu

F32 = jnp.float32
BF16 = jnp.bfloat16

FOX_HEADS = 16
MLA_HEADS = 8
QK_NOPE_DIM = 128
ROPE_BASE = 10000.0
EPS = 1e-6

ADAM_LR = 0.001
ADAM_B1 = 0.9
ADAM_B2 = 0.999
ADAM_EPS = 1e-08
ADAM_WD = 0.01
ADAM_STEP = 10

N_CHIPS = 4
N_DEV = 8
PACK_LANES = 1024
PACK_PART_ROWS = 16
SMALL_PART_ROWS = 8
PACK_ROWS_MULT = 1024
VMEM_LIMIT_BYTES = 48 * 1024 * 1024
LANE_TILE = 128
MATMUL_BLOCK = 1024
MATMUL_TN_DEPTH = 2
ATTN_BLOCK_Q = 1024
ATTN_BLOCK_K = 1024
ATTN_FWD_LANES = 1024
FOX_BWD_HEADS_PER_STEP = 4
MLA_BWD_HEADS_PER_STEP = 2
ATTN_SUB_ROWS = 256
FOX_FWD_SUB_ROWS = (1024, 512)
MLA_FWD_SUB_ROWS = (256, 256)
NEG_BIG = -1e30
MESH = pl.DeviceIdType.MESH


def _round_up(n, m):
    return -(-n // m) * m


def _blk(dim, pref, mult=128):
    if dim <= pref:
        return dim
    b = (pref // mult) * mult
    while b >= mult:
        if dim % b == 0:
            return b
        b -= mult
    return dim


def _params(sem=None):
    return pltpu.CompilerParams(dimension_semantics=sem, vmem_limit_bytes=VMEM_LIMIT_BYTES)


_DIMS = {"nn": (((1,), (0,)), ((), ())), "nt": (((1,), (1,)), ((), ())), "tn": (((0,), (0,)), ((), ()))}


def _matmul(a, b, *, mode, out_dtypes, name, epilogue=None, extras=(), placed=None):
    if mode == "tn":
        kdim, m = a.shape
    else:
        m, kdim = a.shape
    n = b.shape[0] if mode == "nt" else b.shape[1]
    bm, bn = _blk(m, MATMUL_BLOCK), _blk(n, MATMUL_BLOCK)
    bk = _blk(kdim, MATMUL_BLOCK * (MATMUL_TN_DEPTH if mode == "tn" else 1))
    nk = kdim // bk
    n_extra, n_out = len(extras), len(out_dtypes)
    n_placed = 0 if placed is None else 2
    dims = _DIMS[mode]

    def body(a_ref, b_ref, *rest):
        placed_refs = rest[:n_placed]
        rest = rest[n_placed:]
        extra_refs = rest[:n_extra]
        out_refs = rest[n_extra:n_extra + n_out]

        def finish(acc):
            if n_placed:
                acc = acc + lax.dot_general(placed_refs[0][...], placed_refs[1][...], _DIMS["nn"],
                                            preferred_element_type=F32)
            res = (acc,) if epilogue is None else epilogue(acc, *[r[...] for r in extra_refs])
            for o_ref, r in zip(out_refs, res):
                o_ref[...] = r.astype(o_ref.dtype)

        part = lax.dot_general(a_ref[...].astype(BF16), b_ref[...].astype(BF16), dims, preferred_element_type=F32)
        if nk == 1:
            finish(part)
            return
        acc_ref = rest[n_extra + n_out]
        k = pl.program_id(2)

        @pl.when(k == 0)
        def _():
            acc_ref[...] = part

        @pl.when((k > 0) & (k < nk - 1))
        def _():
            acc_ref[...] += part

        @pl.when(k == nk - 1)
        def _():
            finish(acc_ref[...] + part)

    if mode == "tn":
        a_spec = pl.BlockSpec((bk, bm), lambda i, j, k: (k, i))
    else:
        a_spec = pl.BlockSpec((bm, bk), lambda i, j, k: (i, k))
    if mode == "nt":
        b_spec = pl.BlockSpec((bn, bk), lambda i, j, k: (j, k))
    else:
        b_spec = pl.BlockSpec((bk, bn), lambda i, j, k: (k, j))
    tile = pl.BlockSpec((bm, bn), lambda i, j, k: (i, j))
    placed_specs = []
    if n_placed:
        k2 = placed[0].shape[1]
        placed_specs = [pl.BlockSpec((bm, k2), lambda i, j, k: (i, 0)), pl.BlockSpec((k2, bn), lambda i, j, k: (0, j))]
    outs = pl.pallas_call(
        body, name=name,
        grid=(m // bm, n // bn, nk),
        in_specs=[a_spec, b_spec] + placed_specs + [tile] * n_extra,
        out_specs=[tile] * n_out,
        out_shape=[jax.ShapeDtypeStruct((m, n), dt) for dt in out_dtypes],
        scratch_shapes=[pltpu.VMEM((bm, bn), F32)] if nk > 1 else [],
        compiler_params=_params(("parallel", "parallel", "arbitrary")),
    )(a, b, *(placed or ()), *extras)
    return outs[0] if n_out == 1 else outs


def _rms_fwd(x, gains, name):
    s = x.shape[0]
    g, w = gains.shape
    bs = _blk(s, 512, 8)

    def body(x_ref, g_ref, *out_refs):
        xv = x_ref[...]
        y = xv * lax.rsqrt(jnp.mean(xv * xv, axis=-1, keepdims=True) + EPS)
        for i, o_ref in enumerate(out_refs):
            o_ref[...] = (y * g_ref[i:i + 1, :]).astype(o_ref.dtype)

    row = pl.BlockSpec((bs, w), lambda i: (i, 0))
    outs = pl.pallas_call(
        body, name=name, grid=(s // bs,),
        in_specs=[row, pl.BlockSpec((g, w), lambda i: (0, 0))],
        out_specs=[row] * g,
        out_shape=[jax.ShapeDtypeStruct((s, w), BF16)] * g,
        compiler_params=_params(("parallel",)),
    )(x, gains)
    return outs


def _rms_bwd(x, branches, resid, name):
    s = x.shape[0]
    w = branches[0][0].shape[1]
    nb = len(branches)
    bs = _blk(s, 512, 8)
    has_resid = resid is not None

    def body(x_ref, *rest):
        g_refs = rest[:nb]
        dy_refs = rest[nb:2 * nb]
        pos = 2 * nb
        r_ref = rest[pos] if has_resid else None
        pos += int(has_resid)
        dx_ref = rest[pos]
        dg_refs = rest[pos + 1:pos + 1 + nb]
        i = pl.program_id(0)

        @pl.when(i == 0)
        def _():
            for dg_ref in dg_refs:
                dg_ref[...] = jnp.zeros_like(dg_ref)

        xv = x_ref[...]
        rstd = lax.rsqrt(jnp.mean(xv * xv, axis=-1, keepdims=True) + EPS)
        xhat = xv * rstd
        dx = r_ref[...] if has_resid else jnp.zeros_like(xv)
        for g_ref, dy_ref, dg_ref in zip(g_refs, dy_refs, dg_refs):
            dy = dy_ref[...].astype(F32)
            dyg = dy * g_ref[...]
            dx = dx + rstd * (dyg - xhat * jnp.mean(dyg * xhat, axis=-1, keepdims=True))
            dg_ref[...] += jnp.sum(dy * xhat, axis=0, keepdims=True)
        dx_ref[...] = dx

    row = pl.BlockSpec((bs, w), lambda i: (i, 0))
    vec = pl.BlockSpec((1, w), lambda i: (0, 0))
    args = [x] + [g for g, _ in branches] + [dy for _, dy in branches] + ([resid] if has_resid else [])
    outs = pl.pallas_call(
        body, name=name, grid=(s // bs,),
        in_specs=[row] + [vec] * nb + [row] * nb + ([row] if has_resid else []),
        out_specs=[row] + [vec] * nb,
        out_shape=[jax.ShapeDtypeStruct((s, w), F32)] + [jax.ShapeDtypeStruct((1, w), F32)] * nb,
        compiler_params=_params(("arbitrary",)),
    )(*args)
    return outs[0], list(outs[1:])


def _loss_head(x, g, target, name):
    s, w = x.shape
    bs = _blk(s, 512, 8)

    def body(x_ref, g_ref, t_ref, loss_ref, dx_ref, dg_ref):
        i = pl.program_id(0)

        @pl.when(i == 0)
        def _():
            loss_ref[...] = jnp.zeros_like(loss_ref)
            dg_ref[...] = jnp.zeros_like(dg_ref)

        xv = x_ref[...]
        gv = g_ref[...]
        rstd = lax.rsqrt(jnp.mean(xv * xv, axis=-1, keepdims=True) + EPS)
        xhat = xv * rstd
        err = xhat * gv - t_ref[...]
        loss_ref[...] += 0.5 * jnp.sum(jnp.mean(err * err, axis=-1, keepdims=True))
        dy = err * (1.0 / w)
        dyg = dy * gv
        dx_ref[...] = rstd * (dyg - xhat * jnp.mean(dyg * xhat, axis=-1, keepdims=True))
        dg_ref[...] += jnp.sum(dy * xhat, axis=0, keepdims=True)

    row = pl.BlockSpec((bs, w), lambda i: (i, 0))
    vec = pl.BlockSpec((1, w), lambda i: (0, 0))
    return pl.pallas_call(
        body, name=name, grid=(s // bs,),
        in_specs=[row, vec, row],
        out_specs=[pl.BlockSpec((8, 128), lambda i: (0, 0)), row, vec],
        out_shape=[jax.ShapeDtypeStruct((8, 128), F32), jax.ShapeDtypeStruct((s, w), F32),
                   jax.ShapeDtypeStruct((1, w), F32)],
        compiler_params=_params(("arbitrary",)),
    )(x, g, target)


def _rope(a, b, cos, sin, sign, name):
    g, s, w = a.shape
    bs = _blk(s, 1024, 8)

    def body(a_ref, b_ref, c_ref, s_ref, o1_ref, o2_ref):
        av = jnp.sum(a_ref[...].astype(F32), axis=0)
        bv = jnp.sum(b_ref[...].astype(F32), axis=0)
        cv, sv = c_ref[...], s_ref[...] * sign
        o1_ref[...] = av * cv - bv * sv
        o2_ref[...] = bv * cv + av * sv

    grp = pl.BlockSpec((g, bs, w), lambda i: (0, i, 0))
    row = pl.BlockSpec((bs, w), lambda i: (i, 0))
    return pl.pallas_call(
        body, name=name, grid=(s // bs,),
        in_specs=[grp, grp, row, row], out_specs=[row, row],
        out_shape=[jax.ShapeDtypeStruct((s, w), F32)] * 2,
        compiler_params=_params(("parallel",)),
    )(a, b, cos, sin)


def _causal_table(s, bq, bk, q_major):
    nq, nk = s // bq, s // bk
    rows = []
    if q_major:
        for qi in range(nq):
            kmax = (qi * bq + bq - 1) // bk
            for ki in range(kmax + 1):
                rows.append((qi, ki, int(ki * bk + bk - 1 > qi * bq), int(ki == 0), int(ki == kmax)))
    else:
        for ki in range(nk):
            qmin = (ki * bk) // bq
            for qi in range(qmin, nq):
                rows.append((qi, ki, int(ki * bk + bk - 1 > qi * bq), int(qi == qmin), int(qi == nq - 1)))
    return jnp.asarray(np.array(rows, np.int32).T)


def _causal_keep(q0, k0, nq, nk, transposed):
    if transposed:
        kpos = k0 + lax.broadcasted_iota(jnp.int32, (nk, nq), 0)
        qpos = q0 + lax.broadcasted_iota(jnp.int32, (nk, nq), 1)
    else:
        qpos = q0 + lax.broadcasted_iota(jnp.int32, (nq, nk), 0)
        kpos = k0 + lax.broadcasted_iota(jnp.int32, (nq, nk), 1)
    return kpos <= qpos


def _sub_tiles(n_rows, n_cols, masked, square, rows_are_keys, sub_rows):
    sub = min(sub_rows, n_rows)
    out = []
    for r0 in range(0, n_rows, sub):
        if masked and square:
            c0, nc = (r0, n_cols - r0) if rows_are_keys else (0, r0 + sub)
        else:
            c0, nc = 0, n_cols
        out.append((r0, sub, c0, nc))
    return out


_NT = (((1,), (1,)), ((), ()))
_NN = (((1,), (0,)), ((), ()))


def _attn_specs(bq, bk):
    qspec = lambda d: pl.BlockSpec((bq, d), lambda hh, t, tb: (tb[0, t], hh))
    kspec = lambda d: pl.BlockSpec((bk, d), lambda hh, t, tb: (tb[1, t], hh))
    return qspec, kspec


def _split3_cols(x):
    hi = x.astype(BF16).astype(F32)
    rest = x - hi
    mid = rest.astype(BF16).astype(F32)
    lo = (rest - mid).astype(BF16).astype(F32)
    return hi, mid, lo


def _place3(base, col, pieces, sign):
    lane = lax.broadcasted_iota(jnp.int32, base.shape, 1)
    out = base.astype(F32)
    for i, piece in enumerate(pieces):
        out = jnp.where(lane == col + i, sign * piece, out)
    return out.astype(BF16)


def _flash_fwd(qa, ka, va, heads, l_col, lse_col, sub_rows, name):
    s = qa.shape[0]
    da, dv = qa.shape[1] // heads, va.shape[1] // heads
    hps = max(n for n in range(1, ATTN_FWD_LANES // max(da, dv) + 1) if heads % n == 0)
    bq, bk = _blk(s, ATTN_BLOCK_Q), _blk(s, ATTN_BLOCK_K)
    tab = _causal_table(s, bq, bk, True)

    def body(tab_ref, q_ref, k_ref, v_ref, o_ref, qb_ref, m_sc, acc_sc):
        t = pl.program_id(1)
        qi, ki = tab_ref[0, t], tab_ref[1, t]

        @pl.when(tab_ref[3, t] == 1)
        def _():
            m_sc[...] = jnp.full_like(m_sc, NEG_BIG)
            acc_sc[...] = jnp.zeros_like(acc_sc)

        def step(masked):
            for hh in range(hps):
                qc, vc = slice(hh * da, (hh + 1) * da), slice(hh * dv, (hh + 1) * dv)
                for r0, nr, c0, nc in _sub_tiles(bq, bk, masked, bq == bk, False, sub_rows[int(masked)]):
                    sc = lax.dot_general(q_ref[r0:r0 + nr, qc], k_ref[c0:c0 + nc, qc], _NT,
                                         preferred_element_type=F32)
                    if masked:
                        sc = jnp.where(_causal_keep(qi * bq + r0, ki * bk + c0, nr, nc, False), sc, NEG_BIG)
                    m_prev = m_sc[hh, r0:r0 + nr, :]
                    m_new = jnp.maximum(m_prev, jnp.max(sc, axis=-1, keepdims=True))
                    p = jnp.exp(sc - m_new).astype(BF16)
                    acc_sc[r0:r0 + nr, vc] = jnp.exp(m_prev - m_new) * acc_sc[r0:r0 + nr, vc] + lax.dot_general(
                        p, v_ref[c0:c0 + nc, vc], _NN, preferred_element_type=F32)
                    m_sc[hh, r0:r0 + nr, :] = m_new

        @pl.when(tab_ref[2, t] == 1)
        def _():
            step(True)

        @pl.when(tab_ref[2, t] == 0)
        def _():
            step(False)

        @pl.when(tab_ref[4, t] == 1)
        def _():
            for hh in range(hps):
                qc, vc = slice(hh * da, (hh + 1) * da), slice(hh * dv, (hh + 1) * dv)
                acc = acc_sc[:, vc]
                lane = lax.broadcasted_iota(jnp.int32, acc.shape, 1)
                l = jnp.sum(jnp.where(lane == l_col, acc, 0.0), axis=-1, keepdims=True)
                o_ref[:, vc] = (acc / l).astype(o_ref.dtype)
                lse = m_sc[hh] + jnp.log(l)
                qb_ref[:, qc] = _place3(q_ref[:, qc], lse_col, _split3_cols(lse), -1.0)

    qspec, kspec = _attn_specs(bq, bk)
    return pl.pallas_call(
        body, name=name,
        grid_spec=pltpu.PrefetchScalarGridSpec(
            num_scalar_prefetch=1, grid=(heads // hps, tab.shape[1]),
            in_specs=[qspec(hps * da), kspec(hps * da), kspec(hps * dv)],
            out_specs=[qspec(hps * dv), qspec(hps * da)],
            scratch_shapes=[pltpu.VMEM((hps, bq, 1), F32), pltpu.VMEM((bq, hps * dv), F32)]),
        out_shape=[jax.ShapeDtypeStruct((s, heads * dv), BF16), jax.ShapeDtypeStruct((s, heads * da), BF16)],
        compiler_params=_params(("parallel", "arbitrary")),
    )(tab, qa, ka, va)


def _delta_place(do, o, heads, delta_col, name):
    s = o.shape[0]
    dv = o.shape[1] // heads
    bs = _blk(s, 512, 8)
    hpb = max(1, min(heads, 1024 // dv))
    while heads % hpb:
        hpb -= 1

    def body(do_ref, o_ref, out_ref):
        for hh in range(hpb):
            vc = slice(hh * dv, (hh + 1) * dv)
            dov = do_ref[:, vc]
            delta = jnp.sum(dov.astype(F32) * o_ref[:, vc].astype(F32), axis=-1, keepdims=True)
            out_ref[:, vc] = _place3(dov, delta_col, _split3_cols(delta), 1.0)

    blk = pl.BlockSpec((bs, hpb * dv), lambda i, hh: (i, hh))
    return pl.pallas_call(
        body, name=name, grid=(s // bs, heads // hpb), in_specs=[blk, blk], out_specs=blk,
        out_shape=jax.ShapeDtypeStruct(do.shape, BF16),
        compiler_params=_params(("parallel", "parallel")),
    )(do, o)


_TN =(((0,), (0,)), ((), ()))


def _flash_bwd(qa, ka, va, doa, heads, hps, name, sum_cols=None):
    s = qa.shape[0]
    da, dv = qa.shape[1] // heads, va.shape[1] // heads
    h = heads // hps
    bq, bk = _blk(s, ATTN_BLOCK_Q), _blk(s, ATTN_BLOCK_K)
    tab = _causal_table(s, bq, bk, False)
    n_tiles = tab.shape[1]
    n_sum = 0 if sum_cols is None else 2

    def head_column(acc, col):
        out = jnp.zeros((acc.shape[0], hps), F32)
        lane = lax.broadcasted_iota(jnp.int32, (acc.shape[0], da), 1)
        pick = lax.broadcasted_iota(jnp.int32, out.shape, 1)
        for hh in range(hps):
            val = jnp.sum(jnp.where(lane == col, acc[:, hh * da:(hh + 1) * da], 0.0), axis=-1, keepdims=True)
            out = jnp.where(pick == hh, val, out)
        return out

    def body(tab_ref, q_ref, k_ref, v_ref, do_ref, dq_ref, dk_ref, dv_ref, *rest):
        sum_refs, (dk_sc, dv_sc) = rest[:n_sum], rest[n_sum:]
        t = pl.program_id(1)
        qi, ki = tab_ref[0, t], tab_ref[1, t]

        @pl.when(t == 0)
        def _():
            dq_ref[...] = jnp.zeros_like(dq_ref)

        @pl.when(tab_ref[3, t] == 1)
        def _():
            dk_sc[...] = jnp.zeros_like(dk_sc)
            dv_sc[...] = jnp.zeros_like(dv_sc)

        def step(masked):
            for hh in range(hps):
                qc, vc = slice(hh * da, (hh + 1) * da), slice(hh * dv, (hh + 1) * dv)
                for r0, nr, c0, nc in _sub_tiles(bk, bq, masked, bq == bk, True, ATTN_SUB_ROWS):
                    qv, dov, kv = q_ref[c0:c0 + nc, qc], do_ref[c0:c0 + nc, vc], k_ref[r0:r0 + nr, qc]
                    st = lax.dot_general(kv, qv, _NT, preferred_element_type=F32)
                    if masked:
                        st = jnp.where(_causal_keep(qi * bq + c0, ki * bk + r0, nc, nr, True), st, NEG_BIG)
                    pt = jnp.exp(st)
                    dv_sc[r0:r0 + nr, vc] += lax.dot_general(pt.astype(BF16), dov, _NN, preferred_element_type=F32)
                    dpt = lax.dot_general(v_ref[r0:r0 + nr, vc], dov, _NT, preferred_element_type=F32)
                    dst = (pt * dpt).astype(BF16)
                    dk_sc[r0:r0 + nr, qc] += lax.dot_general(dst, qv, _NN, preferred_element_type=F32)
                    q_rows = pl.ds(pl.multiple_of(qi * bq + c0, ATTN_SUB_ROWS), nc)
                    dq_ref[q_rows, qc] += lax.dot_general(dst, kv, _TN, preferred_element_type=F32)

        @pl.when(tab_ref[2, t] == 1)
        def _():
            step(True)

        @pl.when(tab_ref[2, t] == 0)
        def _():
            step(False)

        @pl.when(tab_ref[4, t] == 1)
        def _():
            dk_ref[...] = dk_sc[...]
            dv_ref[...] = dv_sc[...]
            if n_sum:
                sum_refs[1][...] = head_column(dk_sc[...], sum_cols[1])

        if n_sum:
            @pl.when(t == n_tiles - 1)
            def _():
                sum_refs[0][...] = head_column(dq_ref[...], sum_cols[0])

    qspec, kspec = _attn_specs(bq, bk)
    out_specs = [pl.BlockSpec((s, hps * da), lambda hh, t, tb: (0, hh), pipeline_mode=pl.Buffered(1)),
                 kspec(hps * da), kspec(hps * dv)]
    out_shape = [jax.ShapeDtypeStruct((s, heads * da), F32), jax.ShapeDtypeStruct((s, heads * da), F32),
                 jax.ShapeDtypeStruct((s, heads * dv), F32)]
    if n_sum:
        out_specs += [pl.BlockSpec((None, s, hps), lambda hh, t, tb: (hh, 0, 0), pipeline_mode=pl.Buffered(1)),
                      pl.BlockSpec((None, bk, hps), lambda hh, t, tb: (hh, tb[1, t], 0))]
        out_shape += [jax.ShapeDtypeStruct((h, s, hps), F32)] * 2
    return pl.pallas_call(
        body, name=name,
        grid_spec=pltpu.PrefetchScalarGridSpec(
            num_scalar_prefetch=1, grid=(h, n_tiles),
            in_specs=[qspec(hps * da), kspec(hps * da), kspec(hps * dv), qspec(hps * dv)],
            out_specs=out_specs,
            scratch_shapes=[pltpu.VMEM((bk, hps * da), F32), pltpu.VMEM((bk, hps * dv), F32)]),
        out_shape=out_shape,
        compiler_params=_params(("parallel", "arbitrary")),
    )(tab, qa, ka, va, doa)


def _split3(x):
    hi = lax.reduce_precision(x, 8, 7)
    rest = x - hi
    mid = lax.reduce_precision(rest, 8, 7)
    lo = lax.reduce_precision(rest - mid, 8, 7)
    return jnp.stack([hi, mid, lo], axis=-1).astype(BF16)


def _pad_heads(w, heads, width, axis):
    shape = list(w.shape)
    d = shape[axis] // heads
    w = w.reshape(shape[:axis] + [heads, d] + shape[axis + 1:])
    pad = [(0, 0)] * w.ndim
    pad[axis + 1] = (0, width - d)
    return jnp.pad(w, pad).reshape(shape[:axis] + [heads * width] + shape[axis + 1:])


def _unpad_heads(w, heads, d, axis):
    shape = list(w.shape)
    width = shape[axis] // heads
    w = w.reshape(shape[:axis] + [heads, width] + shape[axis + 1:])
    w = lax.slice_in_dim(w, 0, d, axis=axis + 1)
    return w.reshape(shape[:axis] + [heads * d] + shape[axis + 1:])


def _placement(rows, heads, width, entries):
    e = np.zeros((rows, heads * width), np.float32)
    for row, col, val in entries:
        for hh in range(heads):
            e[row(hh) if callable(row) else row, hh * width + col] = val
    return jnp.asarray(e, BF16)


def _rope_mix(a, b, cos_t, sin_t, scale, heads, name):
    s = a.shape[0]
    d = a.shape[1] // heads
    bs = _blk(s, 1024, 8)

    def body(a_ref, b_ref, c_ref, s_ref, o_ref):
        o_ref[...] = ((a_ref[...] * c_ref[...] + b_ref[...] * s_ref[...]) * scale).astype(o_ref.dtype)

    blk = pl.BlockSpec((bs, d), lambda i, hh: (i, hh))
    tbl = pl.BlockSpec((bs, d), lambda i, hh: (i, 0))
    return pl.pallas_call(
        body, name=name, grid=(s // bs, heads), in_specs=[blk, blk, tbl, tbl], out_specs=blk,
        out_shape=jax.ShapeDtypeStruct(a.shape, BF16),
        compiler_params=_params(("parallel", "parallel")),
    )(a, b, cos_t, sin_t)


def _rope_unmix(g, cos_t, sin_t, scale, heads, name):
    s = g.shape[0]
    d = g.shape[1] // heads
    bs = _blk(s, 1024, 8)

    def body(g_ref, c_ref, s_ref, da_ref, db_ref):
        gv = g_ref[...] * scale
        da_ref[...] = (gv * c_ref[...]).astype(da_ref.dtype)
        db_ref[...] = (gv * s_ref[...]).astype(db_ref.dtype)

    blk = pl.BlockSpec((bs, d), lambda i, hh: (i, hh))
    tbl = pl.BlockSpec((bs, d), lambda i, hh: (i, 0))
    return pl.pallas_call(
        body, name=name, grid=(s // bs, heads), in_specs=[blk, tbl, tbl], out_specs=[blk, blk],
        out_shape=[jax.ShapeDtypeStruct(g.shape, BF16)] * 2,
        compiler_params=_params(("parallel", "parallel")),
    )(g, cos_t, sin_t)


def _adamw(w, g, m, v, name):
    r, wd = w.shape
    br = _blk(r, 512, 8)

    def body(w_ref, g_ref, m_ref, v_ref, d_ref, nm_ref, nv_ref):
        gv = g_ref[...]
        mn = ADAM_B1 * m_ref[...] + (1.0 - ADAM_B1) * gv
        vn = ADAM_B2 * v_ref[...] + (1.0 - ADAM_B2) * (gv * gv)
        m_hat = mn / (1.0 - ADAM_B1 ** ADAM_STEP)
        v_hat = vn / (1.0 - ADAM_B2 ** ADAM_STEP)
        d_ref[...] = -ADAM_LR * (m_hat / (jnp.sqrt(v_hat) + ADAM_EPS) + ADAM_WD * w_ref[...])
        nm_ref[...] = mn
        nv_ref[...] = vn

    row = pl.BlockSpec((br, wd), lambda i: (i, 0))
    return pl.pallas_call(
        body, name=name, grid=(r // br,), in_specs=[row] * 4, out_specs=[row] * 3,
        out_shape=[jax.ShapeDtypeStruct((r, wd), F32)] * 3,
        compiler_params=_params(("parallel",)),
    )(w, g, m, v)


_ANY = pl.BlockSpec(memory_space=pl.ANY)


def _place():
    x, y, c = lax.axis_index("x"), lax.axis_index("y"), lax.axis_index("c")
    chips = [(x, 1 - y), (1 - x, y), (1 - x, 1 - y)]
    return x, y, c, chips


def _all_gather_shards(shard, name):
    r, w = shard.shape
    hr = r // 2
    qr = hr // 2

    def body(x_ref, out_ref, send_sems, recv_sems):
        x, y, c, _ = _place()
        me, sibling, y_nbr, x_nbr = (x, y, c), (x, y, 1 - c), (x, 1 - y, c), (1 - x, y, c)

        def rows(j, half, piece=None):
            if piece is None:
                return out_ref.at[j, pl.ds(pl.multiple_of(half * hr, 16), hr), :]
            return out_ref.at[j, pl.ds(pl.multiple_of(half * hr + piece * qr, 16), qr), :]

        def mine(piece):
            return x_ref.at[pl.ds(pl.multiple_of(c * hr + piece * qr, 16), qr), :]

        def copy(sem, src, dst, to):
            return pltpu.make_async_remote_copy(src_ref=src, dst_ref=dst, send_sem=send_sems.at[sem],
                                                recv_sem=recv_sems.at[sem], device_id=to, device_id_type=MESH)

        sent = [copy(0, mine(0), rows(0, c, 0), y_nbr), copy(1, mine(1), rows(0, c, 1), y_nbr),
                copy(2, mine(0), rows(1, c, 0), x_nbr), copy(3, mine(1), rows(1, c, 1), x_nbr)]
        for cp in sent:
            cp.start()

        def landed(sem, ref):
            copy(sem, ref, ref, me).wait_recv()

        def pass_on(sem, src, dst, to):
            cp = copy(sem, src, dst, to)
            cp.start()
            sent.append(cp)

        landed(2, rows(1, c, 0))
        pass_on(4, rows(1, c, 0), rows(2, c, 0), y_nbr)
        landed(1, rows(0, c, 1))
        pass_on(5, rows(0, c, 1), rows(2, c, 1), x_nbr)
        landed(0, rows(0, c, 0))
        pass_on(6, rows(0, c), rows(0, c), sibling)
        landed(3, rows(1, c, 1))
        pass_on(7, rows(1, c), rows(1, c), sibling)
        landed(4, rows(2, c, 0))
        landed(5, rows(2, c, 1))
        pass_on(8, rows(2, c), rows(2, c), sibling)
        for j in range(3):
            landed(6 + j, rows(j, 1 - c))
        for cp in sent:
            cp.wait_send()

    return pl.pallas_call(
        body, name=name, in_specs=[_ANY], out_specs=_ANY,
        out_shape=jax.ShapeDtypeStruct((N_CHIPS - 1, r, w), shard.dtype),
        scratch_shapes=[pltpu.SemaphoreType.DMA((9,)), pltpu.SemaphoreType.DMA((9,))],
        compiler_params=pltpu.CompilerParams(vmem_limit_bytes=VMEM_LIMIT_BYTES),
    )(shard)


def _sibling_swap_halves(g, name):
    nq, r, w = g.shape
    hr = r // 2

    def body(g_ref, a_ref, send_sems, recv_sems):
        x, y, c, _ = _place()
        sibling = (x, y, 1 - c)
        cps = []
        for q in range(nq):
            cp = pltpu.make_async_remote_copy(
                src_ref=g_ref.at[q, pl.ds(pl.multiple_of((1 - c) * hr, 8), hr), :], dst_ref=a_ref.at[q],
                send_sem=send_sems.at[q], recv_sem=recv_sems.at[q], device_id=sibling, device_id_type=MESH)
            cp.start()
            cps.append(cp)
        for cp in cps:
            cp.wait()

    return pl.pallas_call(
        body, name=name, in_specs=[_ANY], out_specs=_ANY,
        out_shape=jax.ShapeDtypeStruct((nq, hr, w), g.dtype),
        scratch_shapes=[pltpu.SemaphoreType.DMA((nq,)), pltpu.SemaphoreType.DMA((nq,))],
        compiler_params=pltpu.CompilerParams(vmem_limit_bytes=VMEM_LIMIT_BYTES),
    )(g)


def _chip_sum(g, a, c_idx, name):
    nq, r, w = g.shape
    hr = r // 2
    br = _blk(hr, 512, 16)
    nb = hr // br

    def body(c_ref, g_ref, a_ref, o_ref):
        o_ref[...] = (g_ref[...] + a_ref[...]).astype(o_ref.dtype)

    return pl.pallas_call(
        body, name=name,
        grid_spec=pltpu.PrefetchScalarGridSpec(
            num_scalar_prefetch=1, grid=(nq, nb),
            in_specs=[pl.BlockSpec((None, br, w), lambda q, i, cr: (q, cr[0] * nb + i, 0)),
                      pl.BlockSpec((None, br, w), lambda q, i, cr: (q, i, 0))],
            out_specs=pl.BlockSpec((None, br, w), lambda q, i, cr: (q, i, 0))),
        out_shape=jax.ShapeDtypeStruct((nq, hr, w), BF16),
        compiler_params=_params(("parallel", "parallel")),
    )(c_idx, g, a)


def _chip_exchange(s4, name):
    nq, hr, w = s4.shape

    def body(s_ref, b_ref, send_sems, recv_sems):
        x, y, c, chips = _place()
        cps = []
        for j, (cx, cy) in enumerate(chips):
            cp = pltpu.make_async_remote_copy(
                src_ref=s_ref.at[2 * cx + cy], dst_ref=b_ref.at[j],
                send_sem=send_sems.at[j], recv_sem=recv_sems.at[j], device_id=(cx, cy, c), device_id_type=MESH)
            cp.start()
            cps.append(cp)
        for cp in cps:
            cp.wait()

    return pl.pallas_call(
        body, name=name, in_specs=[_ANY], out_specs=_ANY,
        out_shape=jax.ShapeDtypeStruct((nq - 1, hr, w), s4.dtype),
        scratch_shapes=[pltpu.SemaphoreType.DMA((3,)), pltpu.SemaphoreType.DMA((3,))],
        compiler_params=pltpu.CompilerParams(vmem_limit_bytes=VMEM_LIMIT_BYTES),
    )(s4)


def _sum_chips(s4, b3, p_idx, name):
    _, hr, w = s4.shape
    nb3 = b3.shape[0]
    br = _blk(hr, 512, 16)

    def body(p_ref, s_ref, b_ref, o_ref):
        acc = s_ref[...].astype(F32)
        for j in range(nb3):
            acc = acc + b_ref[j].astype(F32)
        o_ref[...] = acc

    return pl.pallas_call(
        body, name=name,
        grid_spec=pltpu.PrefetchScalarGridSpec(
            num_scalar_prefetch=1, grid=(hr // br,),
            in_specs=[pl.BlockSpec((None, br, w), lambda i, pr: (pr[0], i, 0)),
                      pl.BlockSpec((nb3, br, w), lambda i, pr: (0, i, 0))],
            out_specs=pl.BlockSpec((br, w), lambda i, pr: (i, 0))),
        out_shape=jax.ShapeDtypeStruct((hr, w), F32),
        compiler_params=_params(("parallel",)),
    )(p_idx, s4, b3)


def _sibling_swap(t, name):
    hr, w = t.shape

    def body(t_ref, o_ref, send_sem, recv_sem):
        x, y, c, _ = _place()
        cp = pltpu.make_async_remote_copy(src_ref=t_ref, dst_ref=o_ref, send_sem=send_sem, recv_sem=recv_sem,
                                          device_id=(x, y, 1 - c), device_id_type=MESH)
        cp.start()
        cp.wait()

    return pl.pallas_call(
        body, name=name, in_specs=[_ANY], out_specs=_ANY,
        out_shape=jax.ShapeDtypeStruct((hr, w), t.dtype),
        scratch_shapes=[pltpu.SemaphoreType.DMA, pltpu.SemaphoreType.DMA],
        compiler_params=pltpu.CompilerParams(vmem_limit_bytes=VMEM_LIMIT_BYTES),
    )(t)


def _all_reduce_small(v, name):
    r, w = v.shape

    def body(v_ref, o_ref, slots, send_sems, recv_sems):
        x, y, c, _ = _place()
        me = 4 * x + 2 * y + c
        slots[me] = v_ref[...]
        cps = []
        for k in range(1, N_DEV):
            fx, fy, fc = (k >> 2) & 1, (k >> 1) & 1, k & 1
            to = (x ^ fx, y ^ fy, c ^ fc)
            cp = pltpu.make_async_remote_copy(
                src_ref=v_ref, dst_ref=slots.at[me], send_sem=send_sems.at[k - 1], recv_sem=recv_sems.at[k - 1],
                device_id=to, device_id_type=MESH)
            cp.start()
            cps.append(cp)
        for k in range(1, N_DEV):
            fx, fy, fc = (k >> 2) & 1, (k >> 1) & 1, k & 1
            src_dev = 4 * (x ^ fx) + 2 * (y ^ fy) + (c ^ fc)
            pltpu.make_async_remote_copy(
                src_ref=v_ref, dst_ref=slots.at[src_dev], send_sem=send_sems.at[k - 1],
                recv_sem=recv_sems.at[k - 1], device_id=(x, y, c), device_id_type=MESH).wait_recv()
        for cp in cps:
            cp.wait_send()
        acc = slots[0]
        for d in range(1, N_DEV):
            acc = acc + slots[d]
        o_ref[...] = acc

    return pl.pallas_call(
        body, name=name,
        in_specs=[pl.BlockSpec(memory_space=pltpu.VMEM)], out_specs=pl.BlockSpec(memory_space=pltpu.VMEM),
        out_shape=jax.ShapeDtypeStruct((r, w), F32),
        scratch_shapes=[pltpu.VMEM((N_DEV, r, w), F32), pltpu.SemaphoreType.DMA((N_DEV - 1,)),
                        pltpu.SemaphoreType.DMA((N_DEV - 1,))],
        compiler_params=pltpu.CompilerParams(vmem_limit_bytes=VMEM_LIMIT_BYTES),
    )(v)


def _part_rows(shape, part_rows=PACK_PART_ROWS):
    assert shape[-1] <= PACK_LANES
    return _round_up(math.prod(shape[:-1]), part_rows)


def _packed_rows(shapes):
    return _round_up(sum(_part_rows(s) for s in shapes), PACK_ROWS_MULT)


def _pack(arrs, total_rows, dtype, part_rows=PACK_PART_ROWS):
    parts = []
    for a in arrs:
        a2 = a.reshape(-1, a.shape[-1]).astype(dtype)
        rows = _part_rows(a.shape, part_rows)
        parts.append(jnp.pad(a2, ((0, rows - a2.shape[0]), (0, PACK_LANES - a2.shape[1]))))
    used = sum(p.shape[0] for p in parts)
    if total_rows > used:
        parts.append(jnp.zeros((total_rows - used, PACK_LANES), dtype))
    return jnp.concatenate(parts, axis=0)


def _unpack(packed, shapes, part_rows=PACK_PART_ROWS):
    out, r0 = [], 0
    for s in shapes:
        out.append(packed[r0:r0 + math.prod(s[:-1]), :s[-1]].reshape(s))
        r0 += _part_rows(s, part_rows)
    return out


_BIG = (("fox_w_in", 2), ("fox_w_out", 1), ("mla_w_kv_a", 0), ("mla_w_kv_b", 1), ("mla_w_q_a", 1),
        ("mla_w_q_b", 2), ("mla_w_out", 1), ("ffn_w_up", 2), ("ffn_w_down", 1))
_SMALL = ("norm_mix_g", "norm_ffn_g", "fox_b_f", "kv_norm_g", "mla_kv_a_norm_g", "mla_q_a_norm_g", "final_norm_g")
_WEIGHTS = ("norm_mix_g", "norm_ffn_g", "fox_w_in", "fox_b_f", "fox_w_out", "kv_norm_g", "mla_w_kv_a",
            "mla_kv_a_norm_g", "mla_w_kv_b", "mla_w_q_a", "mla_q_a_norm_g", "mla_w_q_b", "mla_w_out",
            "ffn_w_up", "ffn_w_down", "final_norm_g")


def _ffn_fwd(x, h, w_up, w_down, tag):
    def relu_sq(acc):
        r = jnp.maximum(acc, 0.0)
        return r, r * r

    r, a = _matmul(h, w_up, mode="nn", out_dtypes=(BF16, BF16), epilogue=relu_sq, name=f"{tag}_up")
    x_out = _matmul(a, w_down, mode="nn", out_dtypes=(F32,), epilogue=lambda acc, res: (acc + res,),
                    extras=(x,), name=f"{tag}_down")
    return x_out, r, a


def _ffn_bwd(dx_out, x_in, h, r, a, g_norm, w_up, w_down, tag):
    d_u = _matmul(dx_out, w_down, mode="nt", out_dtypes=(BF16,), epilogue=lambda acc, rr: (acc * (2.0 * rr.astype(F32)),),
                  extras=(r,), name=f"{tag}_d_act")
    d_w_down = _matmul(a, dx_out, mode="tn", out_dtypes=(F32,), name=f"{tag}_d_w_down")
    d_w_up = _matmul(h, d_u, mode="tn", out_dtypes=(F32,), name=f"{tag}_d_w_up")
    d_h = _matmul(d_u, w_up, mode="nt", out_dtypes=(F32,), name=f"{tag}_d_h")
    dx_in, (d_g,) = _rms_bwd(x_in, [(g_norm, d_h)], dx_out, name=f"{tag}_d_norm")
    return dx_in, d_w_up, d_w_down, d_g


def kernel(x, norm_mix_g, norm_ffn_g, fox_w_in, fox_b_f, fox_w_out, kv_norm_g, mla_w_kv_a, mla_kv_a_norm_g, mla_w_kv_b, mla_w_q_a, mla_q_a_norm_g, mla_w_q_b, mla_w_out, ffn_w_up, ffn_w_down, final_norm_g, loss_target, m_norm_mix_g, m_norm_ffn_g, m_fox_w_in, m_fox_b_f, m_fox_w_out, m_kv_norm_g, m_mla_w_kv_a, m_mla_kv_a_norm_g, m_mla_w_kv_b, m_mla_w_q_a, m_mla_q_a_norm_g, m_mla_w_q_b, m_mla_w_out, m_ffn_w_up, m_ffn_w_down, m_final_norm_g, v_norm_mix_g, v_norm_ffn_g, v_fox_w_in, v_fox_b_f, v_fox_w_out, v_kv_norm_g, v_mla_w_kv_a, v_mla_kv_a_norm_g, v_mla_w_kv_b, v_mla_w_q_a, v_mla_q_a_norm_g, v_mla_w_q_b, v_mla_w_out, v_ffn_w_up, v_ffn_w_down, v_final_norm_g):
    args = dict(locals())
    w_in = {n: args[n] for n in _WEIGHTS}
    m_in = {n: args["m_" + n] for n in _WEIGHTS}
    v_in = {n: args["v_" + n] for n in _WEIGHTS}

    xs = x[0]
    seq, d_model = xs.shape
    tgt = loss_target[0]
    fox_h, mla_h, nope = FOX_HEADS, MLA_HEADS, QK_NOPE_DIM
    kv_rank = mla_kv_a_norm_g.shape[0]
    rope = mla_w_kv_a.shape[1] - kv_rank
    half = rope // 2
    q_rank = mla_q_a_norm_g.shape[1]
    v_dim = mla_w_kv_b.shape[1] * N_CHIPS // mla_h - nope
    fox_w = fox_w_out.shape[1] * N_CHIPS
    fox_dh = fox_w // fox_h

    big_names = [n for n, _ in _BIG]
    shard_shapes = [w_in[n].shape for n in big_names]
    rows = _packed_rows(shard_shapes)
    my_shard = _pack([w_in[n] for n in big_names], rows, BF16)
    others = _all_gather_shards(my_shard, name="gather_weights")
    by_relation = jnp.concatenate([my_shard[None], others], axis=0)
    p_chip = 2 * lax.axis_index("x") + lax.axis_index("y")
    full = {}
    for q in range(N_CHIPS):
        shard_q = lax.dynamic_index_in_dim(by_relation, p_chip ^ q, axis=0, keepdims=False)
        for (n, ax), piece in zip(_BIG, _unpack(shard_q, shard_shapes)):
            full.setdefault(n, []).append(piece)
    full = {n: jnp.concatenate(full[n], axis=ax) for n, ax in _BIG}

    fox_scale = fox_dh ** -0.5
    fox_wd = _round_up(fox_dh + 9, LANE_TILE)
    fox_vwd = _round_up(fox_dh + 4, LANE_TILE)
    w_fox_in = full["fox_w_in"][0]
    w_fq = _pad_heads(w_fox_in[:, :fox_w] * fox_scale, fox_h, fox_wd, 1)
    w_fk = _pad_heads(w_fox_in[:, fox_w:2 * fox_w], fox_h, fox_wd, 1)
    w_fv = _pad_heads(w_fox_in[:, 2 * fox_w:3 * fox_w], fox_h, fox_vwd, 1)
    w_gate = w_fox_in[:, 3 * fox_w:]
    w_fox_out = _pad_heads(full["fox_w_out"][0], fox_h, fox_vwd, 0)
    n_cx = _round_up(3 * fox_h + 1, LANE_TILE)
    c_piece = lambda i: (lambda hh: 3 * hh + i)
    one_col = 3 * fox_h
    e_fq = _placement(n_cx, fox_h, fox_wd, [(c_piece(i), fox_dh + i, 1.0) for i in range(3)]
                      + [(one_col, fox_dh + 3 + i, 1.0) for i in range(3)])
    e_fk = _placement(n_cx, fox_h, fox_wd, [(one_col, fox_dh + i, 1.0) for i in range(3)]
                      + [(c_piece(i), fox_dh + 3 + i, -1.0) for i in range(3)]
                      + [(one_col, fox_dh + 6 + i, 1.0) for i in range(3)])
    e_fv = _placement(n_cx, fox_h, fox_vwd, [(one_col, fox_dh + i, -1.0) for i in range(3)]
                      + [(one_col, fox_dh + 3, 1.0)])

    mla_scale = (nope + rope) ** -0.5
    mla_dk = nope + rope
    mla_wd = _round_up(mla_dk + 3, LANE_TILE)
    mla_vwd = _round_up(v_dim + 4, LANE_TILE)
    w_kv_a = full["mla_w_kv_a"]
    w_kv_b3 = full["mla_w_kv_b"].reshape(kv_rank, mla_h, nope + v_dim)
    w_kn = _pad_heads(w_kv_b3[:, :, :nope].reshape(kv_rank, -1), mla_h, mla_wd, 1)
    w_mv = _pad_heads(w_kv_b3[:, :, nope:].reshape(kv_rank, -1), mla_h, mla_vwd, 1)
    w_q_a = full["mla_w_q_a"][0]
    w_q_b3 = full["mla_w_q_b"][0].reshape(q_rank, mla_h, nope + rope)
    w_qa_ = _pad_heads(w_q_b3.reshape(q_rank, -1), mla_h, mla_wd, 1)
    w_qb_ = _pad_heads(jnp.concatenate([jnp.zeros_like(w_q_b3[:, :, :nope]), -w_q_b3[:, :, nope + half:],
                                        w_q_b3[:, :, nope:nope + half]], axis=-1).reshape(q_rank, -1),
                       mla_h, mla_wd, 1)
    w_mla_out = _pad_heads(full["mla_w_out"][0], mla_h, mla_vwd, 0)
    w_up, w_down = full["ffn_w_up"], full["ffn_w_down"]
    n_kx = _round_up(rope + 1, LANE_TILE)
    e_mk = _placement(n_kx, mla_h, mla_wd, [(j, nope + j, 1.0) for j in range(rope)]
                      + [(rope, mla_dk + i, 1.0) for i in range(3)])
    e_mv = _placement(n_kx, mla_h, mla_vwd, [(rope, v_dim + i, -1.0) for i in range(3)] + [(rope, v_dim + 3, 1.0)])
    e_kr_u = _placement(n_kx, mla_h, mla_wd, [(j, nope + j, 1.0) for j in range(rope)]).T
    e_kr_v = _placement(n_kx, mla_h, mla_wd, [(j, nope + half + j, 1.0) for j in range(half)]
                        + [(half + j, nope + j, -1.0) for j in range(half)]).T

    inv = 1.0 / (ROPE_BASE ** (jnp.arange(0, rope, 2, dtype=F32) / rope))
    ang = jnp.arange(seq, dtype=F32)[:, None] * inv[None, :]
    cos, sin = jnp.cos(ang), jnp.sin(ang)
    pad_t = jnp.zeros((seq, mla_wd - mla_dk), F32)
    cos_t = jnp.concatenate([jnp.ones((seq, nope), F32), cos, cos, pad_t], axis=1)
    sin_t = jnp.concatenate([jnp.zeros((seq, nope), F32), sin, sin, pad_t], axis=1)
    pad_k = jnp.zeros((seq, n_kx - rope), F32)
    cos_k, sin_k = jnp.concatenate([cos, cos, pad_k], axis=1), jnp.concatenate([sin, sin, pad_k], axis=1)

    (h0,) = _rms_fwd(xs, norm_mix_g[0:1], name="l0_norm_mix")
    gate = _matmul(h0, w_gate, mode="nn", out_dtypes=(F32,), name="fox_gate")
    z = gate + fox_b_f[0][None, :]
    cum = jnp.cumsum(jax.nn.log_sigmoid(z), axis=0)
    cx = jnp.concatenate([_split3(cum).reshape(seq, 3 * fox_h), jnp.ones((seq, 1), BF16),
                          jnp.zeros((seq, n_cx - 3 * fox_h - 1), BF16)], axis=1)
    fqa = _matmul(h0, w_fq, mode="nn", out_dtypes=(BF16,), placed=(cx, e_fq), name="fox_q")
    fka = _matmul(h0, w_fk, mode="nn", out_dtypes=(BF16,), placed=(cx, e_fk), name="fox_k")
    fva = _matmul(h0, w_fv, mode="nn", out_dtypes=(BF16,), placed=(cx, e_fv), name="fox_v")
    foa, fqb = _flash_fwd(fqa, fka, fva, fox_h, fox_dh + 3, fox_dh + 6, FOX_FWD_SUB_ROWS, name="fox_attn")
    add_res = lambda acc, res: (acc + res,)
    x1 = _matmul(foa, w_fox_out, mode="nn", out_dtypes=(F32,), epilogue=add_res, extras=(xs,), name="fox_out")
    (h1,) = _rms_fwd(x1, norm_ffn_g[0:1], name="l0_norm_ffn")
    x2, r0, a0 = _ffn_fwd(x1, h1, w_up[0], w_down[0], "ffn0")

    src, h2 = _rms_fwd(x2, jnp.stack([kv_norm_g, norm_mix_g[1]]), name="l1_norm_kv_mix")
    kv_a = _matmul(src, w_kv_a, mode="nn", out_dtypes=(F32,), name="mla_kv_a")
    (c_kv,) = _rms_fwd(kv_a, mla_kv_a_norm_g[None, :], name="mla_norm_kv_a")
    kr1, kr2 = _rope(kv_a[None, :, kv_rank:kv_rank + half], kv_a[None, :, kv_rank + half:], cos, sin, 1.0,
                     name="mla_rope_k")
    krx = jnp.concatenate([kr1.astype(BF16), kr2.astype(BF16), jnp.ones((seq, 1), BF16),
                           jnp.zeros((seq, n_kx - rope - 1), BF16)], axis=1)
    mka = _matmul(c_kv, w_kn, mode="nn", out_dtypes=(BF16,), placed=(krx, e_mk), name="mla_k")
    mva = _matmul(c_kv, w_mv, mode="nn", out_dtypes=(BF16,), placed=(krx, e_mv), name="mla_v")
    cq_pre = _matmul(h2, w_q_a, mode="nn", out_dtypes=(F32,), name="mla_q_a")
    (c_q,) = _rms_fwd(cq_pre, mla_q_a_norm_g, name="mla_norm_q_a")
    q_a_part = _matmul(c_q, w_qa_, mode="nn", out_dtypes=(F32,), name="mla_q_b_cos")
    q_b_part = _matmul(c_q, w_qb_, mode="nn", out_dtypes=(F32,), name="mla_q_b_sin")
    mqa = _rope_mix(q_a_part, q_b_part, cos_t, sin_t, mla_scale, mla_h, name="mla_rope_q")
    moa, mqb = _flash_fwd(mqa, mka, mva, mla_h, v_dim + 3, mla_dk, MLA_FWD_SUB_ROWS, name="mla_attn")
    x3 = _matmul(moa, w_mla_out, mode="nn", out_dtypes=(F32,), epilogue=add_res, extras=(x2,), name="mla_out")
    (h3,) = _rms_fwd(x3, norm_ffn_g[1:2], name="l1_norm_ffn")
    x4, r1, a1 = _ffn_fwd(x3, h3, w_up[1], w_down[1], "ffn1")

    loss_tile, dx4, d_final_g = _loss_head(x4, final_norm_g[None, :], tgt, name="loss_head")
    loss = lax.psum(loss_tile[0, 0], ("x", "y", "c"))

    gw = {}
    dx3, d_up1, d_down1, d_nf1 = _ffn_bwd(dx4, x3, h3, r1, a1, norm_ffn_g[1:2], w_up[1], w_down[1], "ffn1")

    d_mo = _matmul(dx3, w_mla_out, mode="nt", out_dtypes=(BF16,), name="mla_d_ctx")
    gw["mla_w_out"] = _unpad_heads(_matmul(moa, dx3, mode="tn", out_dtypes=(F32,), name="mla_d_w_out"),
                                   mla_h, v_dim, 0)[None]
    d_moa = _delta_place(d_mo, moa, mla_h, v_dim, name="mla_attn_delta")
    d_mqa, d_mka, d_mva = _flash_bwd(mqb, mka, mva, d_moa, mla_h, MLA_BWD_HEADS_PER_STEP,
                                     name="mla_attn_bwd")
    d_qa_part, d_qb_part = _rope_unmix(d_mqa, cos_t, sin_t, mla_scale, mla_h, name="mla_rope_dq")
    d_w_qa_ = _unpad_heads(_matmul(c_q, d_qa_part, mode="tn", out_dtypes=(F32,), name="mla_d_w_q_b_cos"),
                           mla_h, mla_dk, 1).reshape(q_rank, mla_h, mla_dk)
    d_w_qb_ = _unpad_heads(_matmul(c_q, d_qb_part, mode="tn", out_dtypes=(F32,), name="mla_d_w_q_b_sin"),
                           mla_h, mla_dk, 1).reshape(q_rank, mla_h, mla_dk)
    gw["mla_w_q_b"] = jnp.concatenate(
        [d_w_qa_[:, :, :nope], d_w_qa_[:, :, nope:nope + half] + d_w_qb_[:, :, nope + half:],
         d_w_qa_[:, :, nope + half:] - d_w_qb_[:, :, nope:nope + half]], axis=-1).reshape(1, q_rank, mla_h * mla_dk)
    d_c_q_sin = _matmul(d_qb_part, w_qb_, mode="nt", out_dtypes=(F32,), name="mla_d_c_q_sin")
    d_c_q = _matmul(d_qa_part, w_qa_, mode="nt", out_dtypes=(F32,), epilogue=add_res, extras=(d_c_q_sin,),
                    name="mla_d_c_q")
    d_cq_pre, (d_q_a_g,) = _rms_bwd(cq_pre, [(mla_q_a_norm_g, d_c_q)], None, name="mla_d_norm_q_a")
    gw["mla_w_q_a"] = _matmul(h2, d_cq_pre, mode="tn", out_dtypes=(F32,), name="mla_d_w_q_a")[None]
    d_h2 = _matmul(d_cq_pre, w_q_a, mode="nt", out_dtypes=(F32,), name="mla_d_h")

    d_w_kn = _unpad_heads(_matmul(c_kv, d_mka, mode="tn", out_dtypes=(F32,), name="mla_d_w_k"), mla_h, nope, 1)
    d_w_mv = _unpad_heads(_matmul(c_kv, d_mva, mode="tn", out_dtypes=(F32,), name="mla_d_w_v"), mla_h, v_dim, 1)
    gw["mla_w_kv_b"] = jnp.concatenate([d_w_kn.reshape(kv_rank, mla_h, nope), d_w_mv.reshape(kv_rank, mla_h, v_dim)],
                                       axis=-1).reshape(kv_rank, mla_h * (nope + v_dim))
    d_c_kv_v = _matmul(d_mva, w_mv, mode="nt", out_dtypes=(F32,), name="mla_d_c_kv_v")
    d_c_kv = _matmul(d_mka, w_kn, mode="nt", out_dtypes=(F32,), epilogue=add_res, extras=(d_c_kv_v,),
                     name="mla_d_c_kv")
    d_ckv_pre, (d_kv_a_g,) = _rms_bwd(kv_a, [(mla_kv_a_norm_g[None, :], d_c_kv)], None, name="mla_d_norm_kv_a")
    d_kr_u = _matmul(d_mka, e_kr_u, mode="nn", out_dtypes=(F32,), name="mla_d_k_rope_u")
    d_kr_v = _matmul(d_mka, e_kr_v, mode="nn", out_dtypes=(F32,), name="mla_d_k_rope_v")
    d_kr = _rope_mix(d_kr_u, d_kr_v, cos_k, sin_k, 1.0, 1, name="mla_rope_dk")
    d_kv_a = jnp.concatenate([d_ckv_pre, d_kr[:, :rope].astype(F32)], axis=1)
    gw["mla_w_kv_a"] = _matmul(src, d_kv_a, mode="tn", out_dtypes=(F32,), name="mla_d_w_kv_a")
    d_src = _matmul(d_kv_a, w_kv_a, mode="nt", out_dtypes=(F32,), name="mla_d_src")
    dx2, (d_kv_g, d_nm1) = _rms_bwd(x2, [(kv_norm_g[None, :], d_src), (norm_mix_g[1:2], d_h2)], dx3,
                                    name="l1_d_norm_kv_mix")

    dx1, d_up0, d_down0, d_nf0 = _ffn_bwd(dx2, x1, h1, r0, a0, norm_ffn_g[0:1], w_up[0], w_down[0], "ffn0")
    gw["ffn_w_up"] = jnp.stack([d_up0, d_up1])
    gw["ffn_w_down"] = jnp.stack([d_down0, d_down1])

    d_fo = _matmul(dx1, w_fox_out, mode="nt", out_dtypes=(BF16,), name="fox_d_ctx")
    gw["fox_w_out"] = _unpad_heads(_matmul(foa, dx1, mode="tn", out_dtypes=(F32,), name="fox_d_w_out"),
                                   fox_h, fox_dh, 0)[None]
    d_foa = _delta_place(d_fo, foa, fox_h, fox_dh, name="fox_attn_delta")
    fox_hps = FOX_BWD_HEADS_PER_STEP if fox_h % FOX_BWD_HEADS_PER_STEP == 0 else 1
    d_fqa, d_fka, d_fva, ds_rows, ds_cols = _flash_bwd(fqb, fka, fva, d_foa, fox_h, fox_hps, name="fox_attn_bwd",
                                                       sum_cols=(fox_dh, fox_dh + 3))
    d_cum = jnp.transpose(ds_rows - ds_cols, (1, 0, 2)).reshape(seq, fox_h)
    d_z = lax.cumsum(d_cum, axis=0, reverse=True) * jax.nn.sigmoid(-z)
    d_b_f = jnp.sum(d_z, axis=0)
    d_w_in = [_unpad_heads(_matmul(h0, g, mode="tn", out_dtypes=(F32,), name=f"fox_d_w_{tag}"), fox_h, fox_dh, 1)
              for tag, g in (("q", d_fqa), ("k", d_fka), ("v", d_fva))]
    d_w_gate = _matmul(h0, d_z, mode="tn", out_dtypes=(F32,), name="fox_d_w_gate")
    gw["fox_w_in"] = jnp.concatenate([d_w_in[0] * fox_scale, d_w_in[1], d_w_in[2], d_w_gate], axis=1)[None]
    d_h0 = _matmul(d_z, w_gate, mode="nt", out_dtypes=(F32,), name="fox_d_h_gate")
    for tag, g, w in (("q", d_fqa, w_fq), ("k", d_fka, w_fk), ("v", d_fva, w_fv)):
        d_h0 = _matmul(g, w, mode="nt", out_dtypes=(F32,), epilogue=add_res, extras=(d_h0,), name=f"fox_d_h_{tag}")
    grad_x, (d_nm0,) = _rms_bwd(xs, [(norm_mix_g[0:1], d_h0)], dx1, name="l0_d_norm_mix")

    c_idx = lax.axis_index("c").astype(jnp.int32).reshape(1)
    parts = []
    for (n, ax), shape in zip(_BIG, shard_shapes):
        g = gw[n]
        g = jnp.moveaxis(g.reshape(g.shape[:ax] + (N_CHIPS, shape[ax]) + g.shape[ax + 1:]), ax, 0)
        g = g.reshape(N_CHIPS, -1, shape[-1])
        parts.append(jnp.pad(g, ((0, 0), (0, _part_rows(shape) - g.shape[1]), (0, PACK_LANES - shape[-1]))))
    parts.append(jnp.zeros((N_CHIPS, rows - sum(p.shape[1] for p in parts), PACK_LANES), F32))
    g4 = jnp.concatenate(parts, axis=1)
    a4 = _sibling_swap_halves(g4, name="grads_to_sibling")
    s4 = _chip_sum(g4, a4, c_idx, name="grads_chip_sum")
    b3 = _chip_exchange(s4, name="grads_between_chips")
    t_mine = _sum_chips(s4, b3, p_chip.astype(jnp.int32).reshape(1), name="grads_sum_chips")
    t_theirs = _sibling_swap(t_mine, name="grads_join_halves")
    is_south = lax.axis_index("c") == 0
    g_big = jnp.concatenate([jnp.where(is_south, t_mine, t_theirs), jnp.where(is_south, t_theirs, t_mine)],
                            axis=0)

    small_local = {"norm_mix_g": jnp.concatenate([d_nm0, d_nm1], axis=0),
                   "norm_ffn_g": jnp.concatenate([d_nf0, d_nf1], axis=0),
                   "fox_b_f": d_b_f[None, :], "kv_norm_g": d_kv_g[0], "mla_kv_a_norm_g": d_kv_a_g[0],
                   "mla_q_a_norm_g": d_q_a_g, "final_norm_g": d_final_g[0]}
    small_shapes = [w_in[n].shape for n in _SMALL]
    small_rows = sum(_part_rows(s, SMALL_PART_ROWS) for s in small_shapes)
    pack_small = lambda arrs: _pack(arrs, small_rows, F32, SMALL_PART_ROWS)
    g_small = _all_reduce_small(pack_small([small_local[n] for n in _SMALL]), name="grads_small")

    grads = dict(zip(big_names, _unpack(g_big, shard_shapes)))
    delta, new_m, new_v = {}, {}, {}
    for n, shape in zip(big_names, shard_shapes):
        flat = lambda a: a.reshape(-1, shape[-1])
        outs = _adamw(flat(w_in[n]), flat(grads[n]), flat(m_in[n]), flat(v_in[n]), name=f"adamw_{n}")
        delta[n], new_m[n], new_v[n] = (o.reshape(shape) for o in outs)
    sm_outs = _adamw(pack_small([w_in[n] for n in _SMALL]), g_small, pack_small([m_in[n] for n in _SMALL]),
                     pack_small([v_in[n] for n in _SMALL]), name="adamw_small")
    for res, packed in zip((grads, delta, new_m, new_v), (g_small,) + tuple(sm_outs)):
        res.update(zip(_SMALL, _unpack(packed, small_shapes, SMALL_PART_ROWS)))

    return (loss, grad_x[None], *[grads[n] for n in _WEIGHTS], *[delta[n] for n in _WEIGHTS],
            *[new_m[n] for n in _WEIGHTS], *[new_v[n] for n in _WEIGHTS])
```

```python
import math

import numpy as np
import jax
import jax.numpy as jnp
from jax import lax
from jax.experimental import pallas as pl
from jax.experimental.pallas import tpu as pltpu

F32 = jnp.float32
BF16 = jnp.bfloat16

FOX_HEADS = 16
MLA_HEADS = 8
QK_NOPE_DIM = 128
ROPE_BASE = 10000.0
EPS = 1e-6

ADAM_LR = 0.001
ADAM_B1 = 0.9
ADAM_B2 = 0.999
ADAM_EPS = 1e-08
ADAM_WD = 0.01
ADAM_STEP = 10

N_CHIPS = 4
N_DEV = 8
PACK_LANES = 1024
PACK_PART_ROWS = 16
SMALL_PART_ROWS = 8
PACK_ROWS_MULT = 1024
VMEM_LIMIT_BYTES = 48 * 1024 * 1024
LANE_TILE = 128
MATMUL_BLOCK = 1024
MATMUL_WIDE_BLOCK = 2048
MATMUL_DEPTH = 2048
ATTN_BLOCK_Q = 1024
ATTN_BLOCK_K = 1024
ATTN_FWD_LANES = 1024
FOX_BWD_HEADS_PER_STEP = 4
MLA_BWD_HEADS_PER_STEP = 2
ATTN_SUB_ROWS = 256
FOX_FWD_SUB_ROWS = (1024, 512)
MLA_FWD_SUB_ROWS = (256, 256)
NEG_BIG = -1e30
MESH = pl.DeviceIdType.MESH


def _round_up(n, m):
    return -(-n // m) * m


def _blk(dim, pref, mult=128):
    if dim <= pref:
        return dim
    b = (pref // mult) * mult
    while b >= mult:
        if dim % b == 0:
            return b
        b -= mult
    return dim


def _params(sem=None):
    return pltpu.CompilerParams(dimension_semantics=sem, vmem_limit_bytes=VMEM_LIMIT_BYTES)


_DIMS = {"nn": (((1,), (0,)), ((), ())), "nt": (((1,), (1,)), ((), ())), "tn": (((0,), (0,)), ((), ()))}


def _matmul(a, b, *, mode, out_dtypes, name, epilogue=None, extras=(), placed=None):
    if mode == "tn":
        kdim, m = a.shape
    else:
        m, kdim = a.shape
    n = b.shape[0] if mode == "nt" else b.shape[1]
    bm, bk = _blk(m, MATMUL_BLOCK), _blk(kdim, MATMUL_DEPTH)
    bn = _blk(n, MATMUL_WIDE_BLOCK if (mode != "tn" and kdim <= MATMUL_BLOCK) else MATMUL_BLOCK)
    nk = kdim // bk
    n_extra, n_out = len(extras), len(out_dtypes)
    n_placed = 0 if placed is None else 2
    dims = _DIMS[mode]

    def body(a_ref, b_ref, *rest):
        placed_refs = rest[:n_placed]
        rest = rest[n_placed:]
        extra_refs = rest[:n_extra]
        out_refs = rest[n_extra:n_extra + n_out]

        def finish(acc):
            if n_placed:
                acc = acc + lax.dot_general(placed_refs[0][...], placed_refs[1][...], _DIMS["nn"],
                                            preferred_element_type=F32)
            res = (acc,) if epilogue is None else epilogue(acc, *[r[...] for r in extra_refs])
            for o_ref, r in zip(out_refs, res):
                o_ref[...] = r.astype(o_ref.dtype)

        part = lax.dot_general(a_ref[...].astype(BF16), b_ref[...].astype(BF16), dims, preferred_element_type=F32)
        if nk == 1:
            finish(part)
            return
        acc_ref = rest[n_extra + n_out]
        k = pl.program_id(2)

        @pl.when(k == 0)
        def _():
            acc_ref[...] = part

        @pl.when((k > 0) & (k < nk - 1))
        def _():
            acc_ref[...] += part

        @pl.when(k == nk - 1)
        def _():
            finish(acc_ref[...] + part)

    if mode == "tn":
        a_spec = pl.BlockSpec((bk, bm), lambda i, j, k: (k, i))
    else:
        a_spec = pl.BlockSpec((bm, bk), lambda i, j, k: (i, k))
    if mode == "nt":
        b_spec = pl.BlockSpec((bn, bk), lambda i, j, k: (j, k))
    else:
        b_spec = pl.BlockSpec((bk, bn), lambda i, j, k: (k, j))
    tile = pl.BlockSpec((bm, bn), lambda i, j, k: (i, j))
    placed_specs = []
    if n_placed:
        k2 = placed[0].shape[1]
        placed_specs = [pl.BlockSpec((bm, k2), lambda i, j, k: (i, 0)), pl.BlockSpec((k2, bn), lambda i, j, k: (0, j))]
    outs = pl.pallas_call(
        body, name=name,
        grid=(m // bm, n // bn, nk),
        in_specs=[a_spec, b_spec] + placed_specs + [tile] * n_extra,
        out_specs=[tile] * n_out,
        out_shape=[jax.ShapeDtypeStruct((m, n), dt) for dt in out_dtypes],
        scratch_shapes=[pltpu.VMEM((bm, bn), F32)] if nk > 1 else [],
        compiler_params=_params(("parallel", "parallel", "arbitrary")),
    )(a, b, *(placed or ()), *extras)
    return outs[0] if n_out == 1 else outs


def _rms_fwd(x, gains, name):
    s = x.shape[0]
    g, w = gains.shape
    bs = _blk(s, 512, 8)

    def body(x_ref, g_ref, *out_refs):
        xv = x_ref[...]
        y = xv * lax.rsqrt(jnp.mean(xv * xv, axis=-1, keepdims=True) + EPS)
        for i, o_ref in enumerate(out_refs):
            o_ref[...] = (y * g_ref[i:i + 1, :]).astype(o_ref.dtype)

    row = pl.BlockSpec((bs, w), lambda i: (i, 0))
    outs = pl.pallas_call(
        body, name=name, grid=(s // bs,),
        in_specs=[row, pl.BlockSpec((g, w), lambda i: (0, 0))],
        out_specs=[row] * g,
        out_shape=[jax.ShapeDtypeStruct((s, w), BF16)] * g,
        compiler_params=_params(("parallel",)),
    )(x, gains)
    return outs


def _rms_bwd(x, branches, resid, name):
    s = x.shape[0]
    w = branches[0][0].shape[1]
    nb = len(branches)
    bs = _blk(s, 512, 8)
    has_resid = resid is not None

    def body(x_ref, *rest):
        g_refs = rest[:nb]
        dy_refs = rest[nb:2 * nb]
        pos = 2 * nb
        r_ref = rest[pos] if has_resid else None
        pos += int(has_resid)
        dx_ref = rest[pos]
        dg_refs = rest[pos + 1:pos + 1 + nb]
        i = pl.program_id(0)

        @pl.when(i == 0)
        def _():
            for dg_ref in dg_refs:
                dg_ref[...] = jnp.zeros_like(dg_ref)

        xv = x_ref[...]
        rstd = lax.rsqrt(jnp.mean(xv * xv, axis=-1, keepdims=True) + EPS)
        xhat = xv * rstd
        dx = r_ref[...] if has_resid else jnp.zeros_like(xv)
        for g_ref, dy_ref, dg_ref in zip(g_refs, dy_refs, dg_refs):
            dy = dy_ref[...].astype(F32)
            dyg = dy * g_ref[...]
            dx = dx + rstd * (dyg - xhat * jnp.mean(dyg * xhat, axis=-1, keepdims=True))
            dg_ref[...] += jnp.sum(dy * xhat, axis=0, keepdims=True)
        dx_ref[...] = dx

    row = pl.BlockSpec((bs, w), lambda i: (i, 0))
    vec = pl.BlockSpec((1, w), lambda i: (0, 0))
    args = [x] + [g for g, _ in branches] + [dy for _, dy in branches] + ([resid] if has_resid else [])
    outs = pl.pallas_call(
        body, name=name, grid=(s // bs,),
        in_specs=[row] + [vec] * nb + [row] * nb + ([row] if has_resid else []),
        out_specs=[row] + [vec] * nb,
        out_shape=[jax.ShapeDtypeStruct((s, w), F32)] + [jax.ShapeDtypeStruct((1, w), F32)] * nb,
        compiler_params=_params(("arbitrary",)),
    )(*args)
    return outs[0], list(outs[1:])


def _loss_head(x, g, target, name):
    s, w = x.shape
    bs = _blk(s, 512, 8)

    def body(x_ref, g_ref, t_ref, loss_ref, dx_ref, dg_ref):
        i = pl.program_id(0)

        @pl.when(i == 0)
        def _():
            loss_ref[...] = jnp.zeros_like(loss_ref)
            dg_ref[...] = jnp.zeros_like(dg_ref)

        xv = x_ref[...]
        gv = g_ref[...]
        rstd = lax.rsqrt(jnp.mean(xv * xv, axis=-1, keepdims=True) + EPS)
        xhat = xv * rstd
        err = xhat * gv - t_ref[...]
        loss_ref[...] += 0.5 * jnp.sum(jnp.mean(err * err, axis=-1, keepdims=True))
        dy = err * (1.0 / w)
        dyg = dy * gv
        dx_ref[...] = rstd * (dyg - xhat * jnp.mean(dyg * xhat, axis=-1, keepdims=True))
        dg_ref[...] += jnp.sum(dy * xhat, axis=0, keepdims=True)

    row = pl.BlockSpec((bs, w), lambda i: (i, 0))
    vec = pl.BlockSpec((1, w), lambda i: (0, 0))
    return pl.pallas_call(
        body, name=name, grid=(s // bs,),
        in_specs=[row, vec, row],
        out_specs=[pl.BlockSpec((8, 128), lambda i: (0, 0)), row, vec],
        out_shape=[jax.ShapeDtypeStruct((8, 128), F32), jax.ShapeDtypeStruct((s, w), F32),
                   jax.ShapeDtypeStruct((1, w), F32)],
        compiler_params=_params(("arbitrary",)),
    )(x, g, target)


def _rope(a, b, cos, sin, sign, name):
    g, s, w = a.shape
    bs = _blk(s, 1024, 8)

    def body(a_ref, b_ref, c_ref, s_ref, o1_ref, o2_ref):
        av = jnp.sum(a_ref[...].astype(F32), axis=0)
        bv = jnp.sum(b_ref[...].astype(F32), axis=0)
        cv, sv = c_ref[...], s_ref[...] * sign
        o1_ref[...] = av * cv - bv * sv
        o2_ref[...] = bv * cv + av * sv

    grp = pl.BlockSpec((g, bs, w), lambda i: (0, i, 0))
    row = pl.BlockSpec((bs, w), lambda i: (i, 0))
    return pl.pallas_call(
        body, name=name, grid=(s // bs,),
        in_specs=[grp, grp, row, row], out_specs=[row, row],
        out_shape=[jax.ShapeDtypeStruct((s, w), F32)] * 2,
        compiler_params=_params(("parallel",)),
    )(a, b, cos, sin)


def _causal_table(s, bq, bk, q_major):
    nq, nk = s // bq, s // bk
    rows = []
    if q_major:
        for qi in range(nq):
            kmax = (qi * bq + bq - 1) // bk
            for ki in range(kmax + 1):
                rows.append((qi, ki, int(ki * bk + bk - 1 > qi * bq), int(ki == 0), int(ki == kmax)))
    else:
        for ki in range(nk):
            qmin = (ki * bk) // bq
            for qi in range(qmin, nq):
                rows.append((qi, ki, int(ki * bk + bk - 1 > qi * bq), int(qi == qmin), int(qi == nq - 1)))
    return jnp.asarray(np.array(rows, np.int32).T)


def _causal_keep(q0, k0, nq, nk, transposed):
    if transposed:
        kpos = k0 + lax.broadcasted_iota(jnp.int32, (nk, nq), 0)
        qpos = q0 + lax.broadcasted_iota(jnp.int32, (nk, nq), 1)
    else:
        qpos = q0 + lax.broadcasted_iota(jnp.int32, (nq, nk), 0)
        kpos = k0 + lax.broadcasted_iota(jnp.int32, (nq, nk), 1)
    return kpos <= qpos


def _sub_tiles(n_rows, n_cols, masked, square, rows_are_keys, sub_rows):
    sub = min(sub_rows, n_rows)
    out = []
    for r0 in range(0, n_rows, sub):
        if masked and square:
            c0, nc = (r0, n_cols - r0) if rows_are_keys else (0, r0 + sub)
        else:
            c0, nc = 0, n_cols
        out.append((r0, sub, c0, nc))
    return out


_NT = (((1,), (1,)), ((), ()))
_NN = (((1,), (0,)), ((), ()))


def _attn_specs(bq, bk):
    qspec = lambda d: pl.BlockSpec((bq, d), lambda hh, t, tb: (tb[0, t], hh))
    kspec = lambda d: pl.BlockSpec((bk, d), lambda hh, t, tb: (tb[1, t], hh))
    return qspec, kspec


def _split3_cols(x):
    hi = x.astype(BF16).astype(F32)
    rest = x - hi
    mid = rest.astype(BF16).astype(F32)
    lo = (rest - mid).astype(BF16).astype(F32)
    return hi, mid, lo


def _place3(base, col, pieces, sign):
    lane = lax.broadcasted_iota(jnp.int32, base.shape, 1)
    out = base.astype(F32)
    for i, piece in enumerate(pieces):
        out = jnp.where(lane == col + i, sign * piece, out)
    return out.astype(BF16)


def _flash_fwd(qa, ka, va, heads, l_col, lse_col, sub_rows, name):
    s = qa.shape[0]
    da, dv = qa.shape[1] // heads, va.shape[1] // heads
    hps = max(n for n in range(1, ATTN_FWD_LANES // max(da, dv) + 1) if heads % n == 0)
    bq, bk = _blk(s, ATTN_BLOCK_Q), _blk(s, ATTN_BLOCK_K)
    tab = _causal_table(s, bq, bk, True)

    def body(tab_ref, q_ref, k_ref, v_ref, o_ref, qb_ref, m_sc, acc_sc):
        t = pl.program_id(1)
        qi, ki = tab_ref[0, t], tab_ref[1, t]

        @pl.when(tab_ref[3, t] == 1)
        def _():
            m_sc[...] = jnp.full_like(m_sc, NEG_BIG)
            acc_sc[...] = jnp.zeros_like(acc_sc)

        def step(masked):
            for hh in range(hps):
                qc, vc = slice(hh * da, (hh + 1) * da), slice(hh * dv, (hh + 1) * dv)
                for r0, nr, c0, nc in _sub_tiles(bq, bk, masked, bq == bk, False, sub_rows[int(masked)]):
                    sc = lax.dot_general(q_ref[r0:r0 + nr, qc], k_ref[c0:c0 + nc, qc], _NT,
                                         preferred_element_type=F32)
                    if masked:
                        sc = jnp.where(_causal_keep(qi * bq + r0, ki * bk + c0, nr, nc, False), sc, NEG_BIG)
                    m_prev = m_sc[hh, r0:r0 + nr, :]
                    m_new = jnp.maximum(m_prev, jnp.max(sc, axis=-1, keepdims=True))
                    p = jnp.exp(sc - m_new).astype(BF16)
                    acc_sc[r0:r0 + nr, vc] = jnp.exp(m_prev - m_new) * acc_sc[r0:r0 + nr, vc] + lax.dot_general(
                        p, v_ref[c0:c0 + nc, vc], _NN, preferred_element_type=F32)
                    m_sc[hh, r0:r0 + nr, :] = m_new

        @pl.when(tab_ref[2, t] == 1)
        def _():
            step(True)

        @pl.when(tab_ref[2, t] == 0)
        def _():
            step(False)

        @pl.when(tab_ref[4, t] == 1)
        def _():
            for hh in range(hps):
                qc, vc = slice(hh * da, (hh + 1) * da), slice(hh * dv, (hh + 1) * dv)
                acc = acc_sc[:, vc]
                lane = lax.broadcasted_iota(jnp.int32, acc.shape, 1)
                l = jnp.sum(jnp.where(lane == l_col, acc, 0.0), axis=-1, keepdims=True)
                o_ref[:, vc] = (acc / l).astype(o_ref.dtype)
                lse = m_sc[hh] + jnp.log(l)
                qb_ref[:, qc] = _place3(q_ref[:, qc], lse_col, _split3_cols(lse), -1.0)

    qspec, kspec = _attn_specs(bq, bk)
    return pl.pallas_call(
        body, name=name,
        grid_spec=pltpu.PrefetchScalarGridSpec(
            num_scalar_prefetch=1, grid=(heads // hps, tab.shape[1]),
            in_specs=[qspec(hps * da), kspec(hps * da), kspec(hps * dv)],
            out_specs=[qspec(hps * dv), qspec(hps * da)],
            scratch_shapes=[pltpu.VMEM((hps, bq, 1), F32), pltpu.VMEM((bq, hps * dv), F32)]),
        out_shape=[jax.ShapeDtypeStruct((s, heads * dv), BF16), jax.ShapeDtypeStruct((s, heads * da), BF16)],
        compiler_params=_params(("parallel", "arbitrary")),
    )(tab, qa, ka, va)


def _delta_epilogue(dv, delta_col):
    def epilogue(acc, o_tile):
        heads_out = []
        for hh in range(acc.shape[1] // dv):
            vc = slice(hh * dv, (hh + 1) * dv)
            dov = acc[:, vc].astype(BF16)
            delta = jnp.sum(dov.astype(F32) * o_tile[:, vc].astype(F32), axis=-1, keepdims=True)
            heads_out.append(_place3(dov, delta_col, _split3_cols(delta), 1.0))
        return (jnp.concatenate(heads_out, axis=1),)
    return epilogue


_TN =(((0,), (0,)), ((), ()))


def _flash_bwd(qa, ka, va, doa, heads, hps, name, sum_cols=None):
    s = qa.shape[0]
    da, dv = qa.shape[1] // heads, va.shape[1] // heads
    h = heads // hps
    bq, bk = _blk(s, ATTN_BLOCK_Q), _blk(s, ATTN_BLOCK_K)
    tab = _causal_table(s, bq, bk, False)
    n_tiles = tab.shape[1]
    n_sum = 0 if sum_cols is None else 2

    def head_column(acc, col):
        out = jnp.zeros((acc.shape[0], hps), F32)
        lane = lax.broadcasted_iota(jnp.int32, (acc.shape[0], da), 1)
        pick = lax.broadcasted_iota(jnp.int32, out.shape, 1)
        for hh in range(hps):
            val = jnp.sum(jnp.where(lane == col, acc[:, hh * da:(hh + 1) * da], 0.0), axis=-1, keepdims=True)
            out = jnp.where(pick == hh, val, out)
        return out

    def body(tab_ref, q_ref, k_ref, v_ref, do_ref, dq_ref, dk_ref, dv_ref, *rest):
        sum_refs, (dk_sc, dv_sc) = rest[:n_sum], rest[n_sum:]
        t = pl.program_id(1)
        qi, ki = tab_ref[0, t], tab_ref[1, t]

        @pl.when(t == 0)
        def _():
            dq_ref[...] = jnp.zeros_like(dq_ref)

        @pl.when(tab_ref[3, t] == 1)
        def _():
            dk_sc[...] = jnp.zeros_like(dk_sc)
            dv_sc[...] = jnp.zeros_like(dv_sc)

        def step(masked):
            for hh in range(hps):
                qc, vc = slice(hh * da, (hh + 1) * da), slice(hh * dv, (hh + 1) * dv)
                for r0, nr, c0, nc in _sub_tiles(bk, bq, masked, bq == bk, True, ATTN_SUB_ROWS):
                    qv, dov, kv = q_ref[c0:c0 + nc, qc], do_ref[c0:c0 + nc, vc], k_ref[r0:r0 + nr, qc]
                    st = lax.dot_general(kv, qv, _NT, preferred_element_type=F32)
                    if masked:
                        st = jnp.where(_causal_keep(qi * bq + c0, ki * bk + r0, nc, nr, True), st, NEG_BIG)
                    pt = jnp.exp(st)
                    dv_sc[r0:r0 + nr, vc] += lax.dot_general(pt.astype(BF16), dov, _NN, preferred_element_type=F32)
                    dpt = lax.dot_general(v_ref[r0:r0 + nr, vc], dov, _NT, preferred_element_type=F32)
                    dst = (pt * dpt).astype(BF16)
                    dk_sc[r0:r0 + nr, qc] += lax.dot_general(dst, qv, _NN, preferred_element_type=F32)
                    q_rows = pl.ds(pl.multiple_of(qi * bq + c0, ATTN_SUB_ROWS), nc)
                    dq_ref[q_rows, qc] += lax.dot_general(dst, kv, _TN, preferred_element_type=F32)

        @pl.when(tab_ref[2, t] == 1)
        def _():
            step(True)

        @pl.when(tab_ref[2, t] == 0)
        def _():
            step(False)

        @pl.when(tab_ref[4, t] == 1)
        def _():
            dk_ref[...] = dk_sc[...]
            dv_ref[...] = dv_sc[...]
            if n_sum:
                sum_refs[1][...] = head_column(dk_sc[...], sum_cols[1])

        if n_sum:
            @pl.when(t == n_tiles - 1)
            def _():
                sum_refs[0][...] = head_column(dq_ref[...], sum_cols[0])

    qspec, kspec = _attn_specs(bq, bk)
    out_specs = [pl.BlockSpec((s, hps * da), lambda hh, t, tb: (0, hh), pipeline_mode=pl.Buffered(1)),
                 kspec(hps * da), kspec(hps * dv)]
    out_shape = [jax.ShapeDtypeStruct((s, heads * da), F32), jax.ShapeDtypeStruct((s, heads * da), F32),
                 jax.ShapeDtypeStruct((s, heads * dv), F32)]
    if n_sum:
        out_specs += [pl.BlockSpec((None, s, hps), lambda hh, t, tb: (hh, 0, 0), pipeline_mode=pl.Buffered(1)),
                      pl.BlockSpec((None, bk, hps), lambda hh, t, tb: (hh, tb[1, t], 0))]
        out_shape += [jax.ShapeDtypeStruct((h, s, hps), F32)] * 2
    return pl.pallas_call(
        body, name=name,
        grid_spec=pltpu.PrefetchScalarGridSpec(
            num_scalar_prefetch=1, grid=(h, n_tiles),
            in_specs=[qspec(hps * da), kspec(hps * da), kspec(hps * dv), qspec(hps * dv)],
            out_specs=out_specs,
            scratch_shapes=[pltpu.VMEM((bk, hps * da), F32), pltpu.VMEM((bk, hps * dv), F32)]),
        out_shape=out_shape,
        compiler_params=_params(("parallel", "arbitrary")),
    )(tab, qa, ka, va, doa)


def _split3(x):
    hi = lax.reduce_precision(x, 8, 7)
    rest = x - hi
    mid = lax.reduce_precision(rest, 8, 7)
    lo = lax.reduce_precision(rest - mid, 8, 7)
    return jnp.stack([hi, mid, lo], axis=-1).astype(BF16)


def _pad_heads(w, heads, width, axis):
    shape = list(w.shape)
    d = shape[axis] // heads
    w = w.reshape(shape[:axis] + [heads, d] + shape[axis + 1:])
    pad = [(0, 0)] * w.ndim
    pad[axis + 1] = (0, width - d)
    return jnp.pad(w, pad).reshape(shape[:axis] + [heads * width] + shape[axis + 1:])


def _unpad_heads(w, heads, d, axis):
    shape = list(w.shape)
    width = shape[axis] // heads
    w = w.reshape(shape[:axis] + [heads, width] + shape[axis + 1:])
    w = lax.slice_in_dim(w, 0, d, axis=axis + 1)
    return w.reshape(shape[:axis] + [heads * d] + shape[axis + 1:])


def _placement(rows, heads, width, entries):
    e = np.zeros((rows, heads * width), np.float32)
    for row, col, val in entries:
        for hh in range(heads):
            e[row(hh) if callable(row) else row, hh * width + col] = val
    return jnp.asarray(e, BF16)


def _rope_mix(a, b, cos_t, sin_t, scale, heads, name):
    s = a.shape[0]
    d = a.shape[1] // heads
    bs = _blk(s, 1024, 8)

    def body(a_ref, b_ref, c_ref, s_ref, o_ref):
        o_ref[...] = ((a_ref[...] * c_ref[...] + b_ref[...] * s_ref[...]) * scale).astype(o_ref.dtype)

    blk = pl.BlockSpec((bs, d), lambda i, hh: (i, hh))
    tbl = pl.BlockSpec((bs, d), lambda i, hh: (i, 0))
    return pl.pallas_call(
        body, name=name, grid=(s // bs, heads), in_specs=[blk, blk, tbl, tbl], out_specs=blk,
        out_shape=jax.ShapeDtypeStruct(a.shape, BF16),
        compiler_params=_params(("parallel", "parallel")),
    )(a, b, cos_t, sin_t)


def _rope_unmix(g, cos_t, sin_t, scale, heads, name):
    s = g.shape[0]
    d = g.shape[1] // heads
    bs = _blk(s, 1024, 8)

    def body(g_ref, c_ref, s_ref, da_ref, db_ref):
        gv = g_ref[...] * scale
        da_ref[...] = (gv * c_ref[...]).astype(da_ref.dtype)
        db_ref[...] = (gv * s_ref[...]).astype(db_ref.dtype)

    blk = pl.BlockSpec((bs, d), lambda i, hh: (i, hh))
    tbl = pl.BlockSpec((bs, d), lambda i, hh: (i, 0))
    return pl.pallas_call(
        body, name=name, grid=(s // bs, heads), in_specs=[blk, tbl, tbl], out_specs=[blk, blk],
        out_shape=[jax.ShapeDtypeStruct(g.shape, BF16)] * 2,
        compiler_params=_params(("parallel", "parallel")),
    )(g, cos_t, sin_t)


def _adamw(w, g, m, v, name):
    r, wd = w.shape
    br = _blk(r, 512, 8)

    def body(w_ref, g_ref, m_ref, v_ref, d_ref, nm_ref, nv_ref):
        gv = g_ref[...]
        mn = ADAM_B1 * m_ref[...] + (1.0 - ADAM_B1) * gv
        vn = ADAM_B2 * v_ref[...] + (1.0 - ADAM_B2) * (gv * gv)
        m_hat = mn / (1.0 - ADAM_B1 ** ADAM_STEP)
        v_hat = vn / (1.0 - ADAM_B2 ** ADAM_STEP)
        d_ref[...] = -ADAM_LR * (m_hat / (jnp.sqrt(v_hat) + ADAM_EPS) + ADAM_WD * w_ref[...])
        nm_ref[...] = mn
        nv_ref[...] = vn

    row = pl.BlockSpec((br, wd), lambda i: (i, 0))
    return pl.pallas_call(
        body, name=name, grid=(r // br,), in_specs=[row] * 4, out_specs=[row] * 3,
        out_shape=[jax.ShapeDtypeStruct((r, wd), F32)] * 3,
        compiler_params=_params(("parallel",)),
    )(w, g, m, v)


_ANY = pl.BlockSpec(memory_space=pl.ANY)


def _place():
    x, y, c = lax.axis_index("x"), lax.axis_index("y"), lax.axis_index("c")
    chips = [(x, 1 - y), (1 - x, y), (1 - x, 1 - y)]
    return x, y, c, chips


def _all_gather_shards(shard, name):
    r, w = shard.shape
    hr = r // 2
    qr = hr // 2

    def body(x_ref, out_ref, send_sems, recv_sems):
        x, y, c, _ = _place()
        me, sibling, y_nbr, x_nbr = (x, y, c), (x, y, 1 - c), (x, 1 - y, c), (1 - x, y, c)

        def rows(j, half, piece=None):
            if piece is None:
                return out_ref.at[j, pl.ds(pl.multiple_of(half * hr, 16), hr), :]
            return out_ref.at[j, pl.ds(pl.multiple_of(half * hr + piece * qr, 16), qr), :]

        def mine(piece):
            return x_ref.at[pl.ds(pl.multiple_of(c * hr + piece * qr, 16), qr), :]

        def copy(sem, src, dst, to):
            return pltpu.make_async_remote_copy(src_ref=src, dst_ref=dst, send_sem=send_sems.at[sem],
                                                recv_sem=recv_sems.at[sem], device_id=to, device_id_type=MESH)

        sent = [copy(0, mine(0), rows(0, c, 0), y_nbr), copy(1, mine(1), rows(0, c, 1), y_nbr),
                copy(2, mine(0), rows(1, c, 0), x_nbr), copy(3, mine(1), rows(1, c, 1), x_nbr)]
        for cp in sent:
            cp.start()

        def landed(sem, ref):
            copy(sem, ref, ref, me).wait_recv()

        def pass_on(sem, src, dst, to):
            cp = copy(sem, src, dst, to)
            cp.start()
            sent.append(cp)

        landed(2, rows(1, c, 0))
        pass_on(4, rows(1, c, 0), rows(2, c, 0), y_nbr)
        landed(1, rows(0, c, 1))
        pass_on(5, rows(0, c, 1), rows(2, c, 1), x_nbr)
        landed(0, rows(0, c, 0))
        pass_on(6, rows(0, c), rows(0, c), sibling)
        landed(3, rows(1, c, 1))
        pass_on(7, rows(1, c), rows(1, c), sibling)
        landed(4, rows(2, c, 0))
        landed(5, rows(2, c, 1))
        pass_on(8, rows(2, c), rows(2, c), sibling)
        for j in range(3):
            landed(6 + j, rows(j, 1 - c))
        for cp in sent:
            cp.wait_send()

    return pl.pallas_call(
        body, name=name, in_specs=[_ANY], out_specs=_ANY,
        out_shape=jax.ShapeDtypeStruct((N_CHIPS - 1, r, w), shard.dtype),
        scratch_shapes=[pltpu.SemaphoreType.DMA((9,)), pltpu.SemaphoreType.DMA((9,))],
        compiler_params=pltpu.CompilerParams(vmem_limit_bytes=VMEM_LIMIT_BYTES),
    )(shard)


def _sibling_swap_halves(g, name):
    nq, r, w = g.shape
    hr = r // 2

    def body(g_ref, a_ref, send_sems, recv_sems):
        x, y, c, _ = _place()
        sibling = (x, y, 1 - c)
        cps = []
        for q in range(nq):
            cp = pltpu.make_async_remote_copy(
                src_ref=g_ref.at[q, pl.ds(pl.multiple_of((1 - c) * hr, 8), hr), :], dst_ref=a_ref.at[q],
                send_sem=send_sems.at[q], recv_sem=recv_sems.at[q], device_id=sibling, device_id_type=MESH)
            cp.start()
            cps.append(cp)
        for cp in cps:
            cp.wait()

    return pl.pallas_call(
        body, name=name, in_specs=[_ANY], out_specs=_ANY,
        out_shape=jax.ShapeDtypeStruct((nq, hr, w), g.dtype),
        scratch_shapes=[pltpu.SemaphoreType.DMA((nq,)), pltpu.SemaphoreType.DMA((nq,))],
        compiler_params=pltpu.CompilerParams(vmem_limit_bytes=VMEM_LIMIT_BYTES),
    )(g)


def _chip_sum(g, a, c_idx, name):
    nq, r, w = g.shape
    hr = r // 2
    br = _blk(hr, 512, 16)
    nb = hr // br

    def body(c_ref, g_ref, a_ref, o_ref):
        o_ref[...] = (g_ref[...] + a_ref[...]).astype(o_ref.dtype)

    return pl.pallas_call(
        body, name=name,
        grid_spec=pltpu.PrefetchScalarGridSpec(
            num_scalar_prefetch=1, grid=(nq, nb),
            in_specs=[pl.BlockSpec((None, br, w), lambda q, i, cr: (q, cr[0] * nb + i, 0)),
                      pl.BlockSpec((None, br, w), lambda q, i, cr: (q, i, 0))],
            out_specs=pl.BlockSpec((None, br, w), lambda q, i, cr: (q, i, 0))),
        out_shape=jax.ShapeDtypeStruct((nq, hr, w), BF16),
        compiler_params=_params(("parallel", "parallel")),
    )(c_idx, g, a)


def _chip_exchange(s4, name):
    nq, hr, w = s4.shape

    def body(s_ref, b_ref, send_sems, recv_sems):
        x, y, c, chips = _place()
        cps = []
        for j, (cx, cy) in enumerate(chips):
            cp = pltpu.make_async_remote_copy(
                src_ref=s_ref.at[2 * cx + cy], dst_ref=b_ref.at[j],
                send_sem=send_sems.at[j], recv_sem=recv_sems.at[j], device_id=(cx, cy, c), device_id_type=MESH)
            cp.start()
            cps.append(cp)
        for cp in cps:
            cp.wait()

    return pl.pallas_call(
        body, name=name, in_specs=[_ANY], out_specs=_ANY,
        out_shape=jax.ShapeDtypeStruct((nq - 1, hr, w), s4.dtype),
        scratch_shapes=[pltpu.SemaphoreType.DMA((3,)), pltpu.SemaphoreType.DMA((3,))],
        compiler_params=pltpu.CompilerParams(vmem_limit_bytes=VMEM_LIMIT_BYTES),
    )(s4)


def _sum_chips(s4, b3, p_idx, name):
    _, hr, w = s4.shape
    nb3 = b3.shape[0]
    br = _blk(hr, 512, 16)

    def body(p_ref, s_ref, b_ref, o_ref):
        acc = s_ref[...].astype(F32)
        for j in range(nb3):
            acc = acc + b_ref[j].astype(F32)
        o_ref[...] = acc

    return pl.pallas_call(
        body, name=name,
        grid_spec=pltpu.PrefetchScalarGridSpec(
            num_scalar_prefetch=1, grid=(hr // br,),
            in_specs=[pl.BlockSpec((None, br, w), lambda i, pr: (pr[0], i, 0)),
                      pl.BlockSpec((nb3, br, w), lambda i, pr: (0, i, 0))],
            out_specs=pl.BlockSpec((br, w), lambda i, pr: (i, 0))),
        out_shape=jax.ShapeDtypeStruct((hr, w), F32),
        compiler_params=_params(("parallel",)),
    )(p_idx, s4, b3)


def _sibling_swap(t, name):
    hr, w = t.shape

    def body(t_ref, o_ref, send_sem, recv_sem):
        x, y, c, _ = _place()
        cp = pltpu.make_async_remote_copy(src_ref=t_ref, dst_ref=o_ref, send_sem=send_sem, recv_sem=recv_sem,
                                          device_id=(x, y, 1 - c), device_id_type=MESH)
        cp.start()
        cp.wait()

    return pl.pallas_call(
        body, name=name, in_specs=[_ANY], out_specs=_ANY,
        out_shape=jax.ShapeDtypeStruct((hr, w), t.dtype),
        scratch_shapes=[pltpu.SemaphoreType.DMA, pltpu.SemaphoreType.DMA],
        compiler_params=pltpu.CompilerParams(vmem_limit_bytes=VMEM_LIMIT_BYTES),
    )(t)


def _all_reduce_small(v, name):
    r, w = v.shape

    def body(v_ref, o_ref, slots, send_sems, recv_sems):
        x, y, c, _ = _place()
        me = 4 * x + 2 * y + c
        slots[me] = v_ref[...]
        cps = []
        for k in range(1, N_DEV):
            fx, fy, fc = (k >> 2) & 1, (k >> 1) & 1, k & 1
            to = (x ^ fx, y ^ fy, c ^ fc)
            cp = pltpu.make_async_remote_copy(
                src_ref=v_ref, dst_ref=slots.at[me], send_sem=send_sems.at[k - 1], recv_sem=recv_sems.at[k - 1],
                device_id=to, device_id_type=MESH)
            cp.start()
            cps.append(cp)
        for k in range(1, N_DEV):
            fx, fy, fc = (k >> 2) & 1, (k >> 1) & 1, k & 1
            src_dev = 4 * (x ^ fx) + 2 * (y ^ fy) + (c ^ fc)
            pltpu.make_async_remote_copy(
                src_ref=v_ref, dst_ref=slots.at[src_dev], send_sem=send_sems.at[k - 1],
                recv_sem=recv_sems.at[k - 1], device_id=(x, y, c), device_id_type=MESH).wait_recv()
        for cp in cps:
            cp.wait_send()
        acc = slots[0]
        for d in range(1, N_DEV):
            acc = acc + slots[d]
        o_ref[...] = acc

    return pl.pallas_call(
        body, name=name,
        in_specs=[pl.BlockSpec(memory_space=pltpu.VMEM)], out_specs=pl.BlockSpec(memory_space=pltpu.VMEM),
        out_shape=jax.ShapeDtypeStruct((r, w), F32),
        scratch_shapes=[pltpu.VMEM((N_DEV, r, w), F32), pltpu.SemaphoreType.DMA((N_DEV - 1,)),
                        pltpu.SemaphoreType.DMA((N_DEV - 1,))],
        compiler_params=pltpu.CompilerParams(vmem_limit_bytes=VMEM_LIMIT_BYTES),
    )(v)


def _part_rows(shape, part_rows=PACK_PART_ROWS):
    assert shape[-1] <= PACK_LANES
    return _round_up(math.prod(shape[:-1]), part_rows)


def _packed_rows(shapes):
    return _round_up(sum(_part_rows(s) for s in shapes), PACK_ROWS_MULT)


def _pack(arrs, total_rows, dtype, part_rows=PACK_PART_ROWS):
    parts = []
    for a in arrs:
        a2 = a.reshape(-1, a.shape[-1]).astype(dtype)
        rows = _part_rows(a.shape, part_rows)
        parts.append(jnp.pad(a2, ((0, rows - a2.shape[0]), (0, PACK_LANES - a2.shape[1]))))
    used = sum(p.shape[0] for p in parts)
    if total_rows > used:
        parts.append(jnp.zeros((total_rows - used, PACK_LANES), dtype))
    return jnp.concatenate(parts, axis=0)


def _unpack(packed, shapes, part_rows=PACK_PART_ROWS):
    out, r0 = [], 0
    for s in shapes:
        out.append(packed[r0:r0 + math.prod(s[:-1]), :s[-1]].reshape(s))
        r0 += _part_rows(s, part_rows)
    return out


_BIG = (("fox_w_in", 2), ("fox_w_out", 1), ("mla_w_kv_a", 0), ("mla_w_kv_b", 1), ("mla_w_q_a", 1),
        ("mla_w_q_b", 2), ("mla_w_out", 1), ("ffn_w_up", 2), ("ffn_w_down", 1))
_SMALL = ("norm_mix_g", "norm_ffn_g", "fox_b_f", "kv_norm_g", "mla_kv_a_norm_g", "mla_q_a_norm_g", "final_norm_g")
_WEIGHTS = ("norm_mix_g", "norm_ffn_g", "fox_w_in", "fox_b_f", "fox_w_out", "kv_norm_g", "mla_w_kv_a",
            "mla_kv_a_norm_g", "mla_w_kv_b", "mla_w_q_a", "mla_q_a_norm_g", "mla_w_q_b", "mla_w_out",
            "ffn_w_up", "ffn_w_down", "final_norm_g")


def _ffn_fwd(x, h, w_up, w_down, tag):
    def relu_sq(acc):
        r = jnp.maximum(acc, 0.0)
        return r, r * r

    r, a = _matmul(h, w_up, mode="nn", out_dtypes=(BF16, BF16), epilogue=relu_sq, name=f"{tag}_up")
    x_out = _matmul(a, w_down, mode="nn", out_dtypes=(F32,), epilogue=lambda acc, res: (acc + res,),
                    extras=(x,), name=f"{tag}_down")
    return x_out, r, a


def _ffn_bwd(dx_out, x_in, h, r, a, g_norm, w_up, w_down, tag):
    d_u = _matmul(dx_out, w_down, mode="nt", out_dtypes=(BF16,), epilogue=lambda acc, rr: (acc * (2.0 * rr.astype(F32)),),
                  extras=(r,), name=f"{tag}_d_act")
    d_w_down = _matmul(a, dx_out, mode="tn", out_dtypes=(F32,), name=f"{tag}_d_w_down")
    d_w_up = _matmul(h, d_u, mode="tn", out_dtypes=(F32,), name=f"{tag}_d_w_up")
    d_h = _matmul(d_u, w_up, mode="nt", out_dtypes=(F32,), name=f"{tag}_d_h")
    dx_in, (d_g,) = _rms_bwd(x_in, [(g_norm, d_h)], dx_out, name=f"{tag}_d_norm")
    return dx_in, d_w_up, d_w_down, d_g


def kernel(x, norm_mix_g, norm_ffn_g, fox_w_in, fox_b_f, fox_w_out, kv_norm_g, mla_w_kv_a, mla_kv_a_norm_g, mla_w_kv_b, mla_w_q_a, mla_q_a_norm_g, mla_w_q_b, mla_w_out, ffn_w_up, ffn_w_down, final_norm_g, loss_target, m_norm_mix_g, m_norm_ffn_g, m_fox_w_in, m_fox_b_f, m_fox_w_out, m_kv_norm_g, m_mla_w_kv_a, m_mla_kv_a_norm_g, m_mla_w_kv_b, m_mla_w_q_a, m_mla_q_a_norm_g, m_mla_w_q_b, m_mla_w_out, m_ffn_w_up, m_ffn_w_down, m_final_norm_g, v_norm_mix_g, v_norm_ffn_g, v_fox_w_in, v_fox_b_f, v_fox_w_out, v_kv_norm_g, v_mla_w_kv_a, v_mla_kv_a_norm_g, v_mla_w_kv_b, v_mla_w_q_a, v_mla_q_a_norm_g, v_mla_w_q_b, v_mla_w_out, v_ffn_w_up, v_ffn_w_down, v_final_norm_g):
    args = dict(locals())
    w_in = {n: args[n] for n in _WEIGHTS}
    m_in = {n: args["m_" + n] for n in _WEIGHTS}
    v_in = {n: args["v_" + n] for n in _WEIGHTS}

    xs = x[0]
    seq, d_model = xs.shape
    tgt = loss_target[0]
    fox_h, mla_h, nope = FOX_HEADS, MLA_HEADS, QK_NOPE_DIM
    kv_rank = mla_kv_a_norm_g.shape[0]
    rope = mla_w_kv_a.shape[1] - kv_rank
    half = rope // 2
    q_rank = mla_q_a_norm_g.shape[1]
    v_dim = mla_w_kv_b.shape[1] * N_CHIPS // mla_h - nope
    fox_w = fox_w_out.shape[1] * N_CHIPS
    fox_dh = fox_w // fox_h

    big_names = [n for n, _ in _BIG]
    shard_shapes = [w_in[n].shape for n in big_names]
    rows = _packed_rows(shard_shapes)
    my_shard = _pack([w_in[n] for n in big_names], rows, BF16)
    others = _all_gather_shards(my_shard, name="gather_weights")
    by_relation = jnp.concatenate([my_shard[None], others], axis=0)
    p_chip = 2 * lax.axis_index("x") + lax.axis_index("y")
    full = {}
    for q in range(N_CHIPS):
        shard_q = lax.dynamic_index_in_dim(by_relation, p_chip ^ q, axis=0, keepdims=False)
        for (n, ax), piece in zip(_BIG, _unpack(shard_q, shard_shapes)):
            full.setdefault(n, []).append(piece)
    full = {n: jnp.concatenate(full[n], axis=ax) for n, ax in _BIG}

    fox_scale = fox_dh ** -0.5
    fox_wd = _round_up(fox_dh + 9, LANE_TILE)
    fox_vwd = _round_up(fox_dh + 4, LANE_TILE)
    w_fox_in = full["fox_w_in"][0]
    w_fq = _pad_heads(w_fox_in[:, :fox_w] * fox_scale, fox_h, fox_wd, 1)
    w_fk = _pad_heads(w_fox_in[:, fox_w:2 * fox_w], fox_h, fox_wd, 1)
    w_fv = _pad_heads(w_fox_in[:, 2 * fox_w:3 * fox_w], fox_h, fox_vwd, 1)
    w_gate = w_fox_in[:, 3 * fox_w:]
    w_fox_out = _pad_heads(full["fox_w_out"][0], fox_h, fox_vwd, 0)
    n_cx = _round_up(3 * fox_h + 1, LANE_TILE)
    c_piece = lambda i: (lambda hh: 3 * hh + i)
    one_col = 3 * fox_h
    e_fq = _placement(n_cx, fox_h, fox_wd, [(c_piece(i), fox_dh + i, 1.0) for i in range(3)]
                      + [(one_col, fox_dh + 3 + i, 1.0) for i in range(3)])
    e_fk = _placement(n_cx, fox_h, fox_wd, [(one_col, fox_dh + i, 1.0) for i in range(3)]
                      + [(c_piece(i), fox_dh + 3 + i, -1.0) for i in range(3)]
                      + [(one_col, fox_dh + 6 + i, 1.0) for i in range(3)])
    e_fv = _placement(n_cx, fox_h, fox_vwd, [(one_col, fox_dh + i, -1.0) for i in range(3)]
                      + [(one_col, fox_dh + 3, 1.0)])

    mla_scale = (nope + rope) ** -0.5
    mla_dk = nope + rope
    mla_wd = _round_up(mla_dk + 3, LANE_TILE)
    mla_vwd = _round_up(v_dim + 4, LANE_TILE)
    w_kv_a = full["mla_w_kv_a"]
    w_kv_b3 = full["mla_w_kv_b"].reshape(kv_rank, mla_h, nope + v_dim)
    w_kn = _pad_heads(w_kv_b3[:, :, :nope].reshape(kv_rank, -1), mla_h, mla_wd, 1)
    w_mv = _pad_heads(w_kv_b3[:, :, nope:].reshape(kv_rank, -1), mla_h, mla_vwd, 1)
    w_q_a = full["mla_w_q_a"][0]
    w_q_b3 = full["mla_w_q_b"][0].reshape(q_rank, mla_h, nope + rope)
    w_qa_ = _pad_heads(w_q_b3.reshape(q_rank, -1), mla_h, mla_wd, 1)
    w_qb_ = _pad_heads(jnp.concatenate([jnp.zeros_like(w_q_b3[:, :, :nope]), -w_q_b3[:, :, nope + half:],
                                        w_q_b3[:, :, nope:nope + half]], axis=-1).reshape(q_rank, -1),
                       mla_h, mla_wd, 1)
    w_mla_out = _pad_heads(full["mla_w_out"][0], mla_h, mla_vwd, 0)
    w_up, w_down = full["ffn_w_up"], full["ffn_w_down"]
    n_kx = _round_up(rope + 1, LANE_TILE)
    e_mk = _placement(n_kx, mla_h, mla_wd, [(j, nope + j, 1.0) for j in range(rope)]
                      + [(rope, mla_dk + i, 1.0) for i in range(3)])
    e_mv = _placement(n_kx, mla_h, mla_vwd, [(rope, v_dim + i, -1.0) for i in range(3)] + [(rope, v_dim + 3, 1.0)])
    e_kr_u = _placement(n_kx, mla_h, mla_wd, [(j, nope + j, 1.0) for j in range(rope)]).T
    e_kr_v = _placement(n_kx, mla_h, mla_wd, [(j, nope + half + j, 1.0) for j in range(half)]
                        + [(half + j, nope + j, -1.0) for j in range(half)]).T

    inv = 1.0 / (ROPE_BASE ** (jnp.arange(0, rope, 2, dtype=F32) / rope))
    ang = jnp.arange(seq, dtype=F32)[:, None] * inv[None, :]
    cos, sin = jnp.cos(ang), jnp.sin(ang)
    pad_t = jnp.zeros((seq, mla_wd - mla_dk), F32)
    cos_t = jnp.concatenate([jnp.ones((seq, nope), F32), cos, cos, pad_t], axis=1)
    sin_t = jnp.concatenate([jnp.zeros((seq, nope), F32), sin, sin, pad_t], axis=1)
    pad_k = jnp.zeros((seq, n_kx - rope), F32)
    cos_k, sin_k = jnp.concatenate([cos, cos, pad_k], axis=1), jnp.concatenate([sin, sin, pad_k], axis=1)

    (h0,) = _rms_fwd(xs, norm_mix_g[0:1], name="l0_norm_mix")
    gate = _matmul(h0, w_gate, mode="nn", out_dtypes=(F32,), name="fox_gate")
    z = gate + fox_b_f[0][None, :]
    cum = jnp.cumsum(jax.nn.log_sigmoid(z), axis=0)
    cx = jnp.concatenate([_split3(cum).reshape(seq, 3 * fox_h), jnp.ones((seq, 1), BF16),
                          jnp.zeros((seq, n_cx - 3 * fox_h - 1), BF16)], axis=1)
    fqa = _matmul(h0, w_fq, mode="nn", out_dtypes=(BF16,), placed=(cx, e_fq), name="fox_q")
    fka = _matmul(h0, w_fk, mode="nn", out_dtypes=(BF16,), placed=(cx, e_fk), name="fox_k")
    fva = _matmul(h0, w_fv, mode="nn", out_dtypes=(BF16,), placed=(cx, e_fv), name="fox_v")
    foa, fqb = _flash_fwd(fqa, fka, fva, fox_h, fox_dh + 3, fox_dh + 6, FOX_FWD_SUB_ROWS, name="fox_attn")
    add_res = lambda acc, res: (acc + res,)
    x1 = _matmul(foa, w_fox_out, mode="nn", out_dtypes=(F32,), epilogue=add_res, extras=(xs,), name="fox_out")
    (h1,) = _rms_fwd(x1, norm_ffn_g[0:1], name="l0_norm_ffn")
    x2, r0, a0 = _ffn_fwd(x1, h1, w_up[0], w_down[0], "ffn0")

    src, h2 = _rms_fwd(x2, jnp.stack([kv_norm_g, norm_mix_g[1]]), name="l1_norm_kv_mix")
    kv_a = _matmul(src, w_kv_a, mode="nn", out_dtypes=(F32,), name="mla_kv_a")
    (c_kv,) = _rms_fwd(kv_a, mla_kv_a_norm_g[None, :], name="mla_norm_kv_a")
    kr1, kr2 = _rope(kv_a[None, :, kv_rank:kv_rank + half], kv_a[None, :, kv_rank + half:], cos, sin, 1.0,
                     name="mla_rope_k")
    krx = jnp.concatenate([kr1.astype(BF16), kr2.astype(BF16), jnp.ones((seq, 1), BF16),
                           jnp.zeros((seq, n_kx - rope - 1), BF16)], axis=1)
    mka = _matmul(c_kv, w_kn, mode="nn", out_dtypes=(BF16,), placed=(krx, e_mk), name="mla_k")
    mva = _matmul(c_kv, w_mv, mode="nn", out_dtypes=(BF16,), placed=(krx, e_mv), name="mla_v")
    cq_pre = _matmul(h2, w_q_a, mode="nn", out_dtypes=(F32,), name="mla_q_a")
    (c_q,) = _rms_fwd(cq_pre, mla_q_a_norm_g, name="mla_norm_q_a")
    q_a_part = _matmul(c_q, w_qa_, mode="nn", out_dtypes=(F32,), name="mla_q_b_cos")
    q_b_part = _matmul(c_q, w_qb_, mode="nn", out_dtypes=(F32,), name="mla_q_b_sin")
    mqa = _rope_mix(q_a_part, q_b_part, cos_t, sin_t, mla_scale, mla_h, name="mla_rope_q")
    moa, mqb = _flash_fwd(mqa, mka, mva, mla_h, v_dim + 3, mla_dk, MLA_FWD_SUB_ROWS, name="mla_attn")
    x3 = _matmul(moa, w_mla_out, mode="nn", out_dtypes=(F32,), epilogue=add_res, extras=(x2,), name="mla_out")
    (h3,) = _rms_fwd(x3, norm_ffn_g[1:2], name="l1_norm_ffn")
    x4, r1, a1 = _ffn_fwd(x3, h3, w_up[1], w_down[1], "ffn1")

    loss_tile, dx4, d_final_g = _loss_head(x4, final_norm_g[None, :], tgt, name="loss_head")
    loss = lax.psum(loss_tile[0, 0], ("x", "y", "c"))

    gw = {}
    dx3, d_up1, d_down1, d_nf1 = _ffn_bwd(dx4, x3, h3, r1, a1, norm_ffn_g[1:2], w_up[1], w_down[1], "ffn1")

    d_moa = _matmul(dx3, w_mla_out, mode="nt", out_dtypes=(BF16,), epilogue=_delta_epilogue(mla_vwd, v_dim),
                    extras=(moa,), name="mla_d_ctx")
    gw["mla_w_out"] = _unpad_heads(_matmul(moa, dx3, mode="tn", out_dtypes=(F32,), name="mla_d_w_out"),
                                   mla_h, v_dim, 0)[None]
    d_mqa, d_mka, d_mva = _flash_bwd(mqb, mka, mva, d_moa, mla_h, MLA_BWD_HEADS_PER_STEP,
                                     name="mla_attn_bwd")
    d_qa_part, d_qb_part = _rope_unmix(d_mqa, cos_t, sin_t, mla_scale, mla_h, name="mla_rope_dq")
    d_w_qa_ = _unpad_heads(_matmul(c_q, d_qa_part, mode="tn", out_dtypes=(F32,), name="mla_d_w_q_b_cos"),
                           mla_h, mla_dk, 1).reshape(q_rank, mla_h, mla_dk)
    d_w_qb_ = _unpad_heads(_matmul(c_q, d_qb_part, mode="tn", out_dtypes=(F32,), name="mla_d_w_q_b_sin"),
                           mla_h, mla_dk, 1).reshape(q_rank, mla_h, mla_dk)
    gw["mla_w_q_b"] = jnp.concatenate(
        [d_w_qa_[:, :, :nope], d_w_qa_[:, :, nope:nope + half] + d_w_qb_[:, :, nope + half:],
         d_w_qa_[:, :, nope + half:] - d_w_qb_[:, :, nope:nope + half]], axis=-1).reshape(1, q_rank, mla_h * mla_dk)
    d_c_q_sin = _matmul(d_qb_part, w_qb_, mode="nt", out_dtypes=(F32,), name="mla_d_c_q_sin")
    d_c_q = _matmul(d_qa_part, w_qa_, mode="nt", out_dtypes=(F32,), epilogue=add_res, extras=(d_c_q_sin,),
                    name="mla_d_c_q")
    d_cq_pre, (d_q_a_g,) = _rms_bwd(cq_pre, [(mla_q_a_norm_g, d_c_q)], None, name="mla_d_norm_q_a")
    gw["mla_w_q_a"] = _matmul(h2, d_cq_pre, mode="tn", out_dtypes=(F32,), name="mla_d_w_q_a")[None]
    d_h2 = _matmul(d_cq_pre, w_q_a, mode="nt", out_dtypes=(F32,), name="mla_d_h")

    d_w_kn = _unpad_heads(_matmul(c_kv, d_mka, mode="tn", out_dtypes=(F32,), name="mla_d_w_k"), mla_h, nope, 1)
    d_w_mv = _unpad_heads(_matmul(c_kv, d_mva, mode="tn", out_dtypes=(F32,), name="mla_d_w_v"), mla_h, v_dim, 1)
    gw["mla_w_kv_b"] = jnp.concatenate([d_w_kn.reshape(kv_rank, mla_h, nope), d_w_mv.reshape(kv_rank, mla_h, v_dim)],
                                       axis=-1).reshape(kv_rank, mla_h * (nope + v_dim))
    d_c_kv_v = _matmul(d_mva, w_mv, mode="nt", out_dtypes=(F32,), name="mla_d_c_kv_v")
    d_c_kv = _matmul(d_mka, w_kn, mode="nt", out_dtypes=(F32,), epilogue=add_res, extras=(d_c_kv_v,),
                     name="mla_d_c_kv")
    d_ckv_pre, (d_kv_a_g,) = _rms_bwd(kv_a, [(mla_kv_a_norm_g[None, :], d_c_kv)], None, name="mla_d_norm_kv_a")
    d_kr_u = _matmul(d_mka, e_kr_u, mode="nn", out_dtypes=(F32,), name="mla_d_k_rope_u")
    d_kr_v = _matmul(d_mka, e_kr_v, mode="nn", out_dtypes=(F32,), name="mla_d_k_rope_v")
    d_kr = _rope_mix(d_kr_u, d_kr_v, cos_k, sin_k, 1.0, 1, name="mla_rope_dk")
    d_kv_a = jnp.concatenate([d_ckv_pre, d_kr[:, :rope].astype(F32)], axis=1)
    gw["mla_w_kv_a"] = _matmul(src, d_kv_a, mode="tn", out_dtypes=(F32,), name="mla_d_w_kv_a")
    d_src = _matmul(d_kv_a, w_kv_a, mode="nt", out_dtypes=(F32,), name="mla_d_src")
    dx2, (d_kv_g, d_nm1) = _rms_bwd(x2, [(kv_norm_g[None, :], d_src), (norm_mix_g[1:2], d_h2)], dx3,
                                    name="l1_d_norm_kv_mix")

    dx1, d_up0, d_down0, d_nf0 = _ffn_bwd(dx2, x1, h1, r0, a0, norm_ffn_g[0:1], w_up[0], w_down[0], "ffn0")
    gw["ffn_w_up"] = jnp.stack([d_up0, d_up1])
    gw["ffn_w_down"] = jnp.stack([d_down0, d_down1])

    d_foa = _matmul(dx1, w_fox_out, mode="nt", out_dtypes=(BF16,), epilogue=_delta_epilogue(fox_vwd, fox_dh),
                    extras=(foa,), name="fox_d_ctx")
    gw["fox_w_out"] = _unpad_heads(_matmul(foa, dx1, mode="tn", out_dtypes=(F32,), name="fox_d_w_out"),
                                   fox_h, fox_dh, 0)[None]
    fox_hps = FOX_BWD_HEADS_PER_STEP if fox_h % FOX_BWD_HEADS_PER_STEP == 0 else 1
    d_fqa, d_fka, d_fva, ds_rows, ds_cols = _flash_bwd(fqb, fka, fva, d_foa, fox_h, fox_hps, name="fox_attn_bwd",
                                                       sum_cols=(fox_dh, fox_dh + 3))
    d_cum = jnp.transpose(ds_rows - ds_cols, (1, 0, 2)).reshape(seq, fox_h)
    d_z = lax.cumsum(d_cum, axis=0, reverse=True) * jax.nn.sigmoid(-z)
    d_b_f = jnp.sum(d_z, axis=0)
    d_w_in = [_unpad_heads(_matmul(h0, g, mode="tn", out_dtypes=(F32,), name=f"fox_d_w_{tag}"), fox_h, fox_dh, 1)
              for tag, g in (("q", d_fqa), ("k", d_fka), ("v", d_fva))]
    d_w_gate = _matmul(h0, d_z, mode="tn", out_dtypes=(F32,), name="fox_d_w_gate")
    gw["fox_w_in"] = jnp.concatenate([d_w_in[0] * fox_scale, d_w_in[1], d_w_in[2], d_w_gate], axis=1)[None]
    d_h0 = _matmul(d_z, w_gate, mode="nt", out_dtypes=(F32,), name="fox_d_h_gate")
    for tag, g, w in (("q", d_fqa, w_fq), ("k", d_fka, w_fk), ("v", d_fva, w_fv)):
        d_h0 = _matmul(g, w, mode="nt", out_dtypes=(F32,), epilogue=add_res, extras=(d_h0,), name=f"fox_d_h_{tag}")
    grad_x, (d_nm0,) = _rms_bwd(xs, [(norm_mix_g[0:1], d_h0)], dx1, name="l0_d_norm_mix")

    c_idx = lax.axis_index("c").astype(jnp.int32).reshape(1)
    parts = []
    for (n, ax), shape in zip(_BIG, shard_shapes):
        g = gw[n]
        g = jnp.moveaxis(g.reshape(g.shape[:ax] + (N_CHIPS, shape[ax]) + g.shape[ax + 1:]), ax, 0)
        g = g.reshape(N_CHIPS, -1, shape[-1])
        parts.append(jnp.pad(g, ((0, 0), (0, _part_rows(shape) - g.shape[1]), (0, PACK_LANES - shape[-1]))))
    parts.append(jnp.zeros((N_CHIPS, rows - sum(p.shape[1] for p in parts), PACK_LANES), F32))
    g4 = jnp.concatenate(parts, axis=1)
    a4 = _sibling_swap_halves(g4, name="grads_to_sibling")
    s4 = _chip_sum(g4, a4, c_idx, name="grads_chip_sum")
    b3 = _chip_exchange(s4, name="grads_between_chips")
    t_mine = _sum_chips(s4, b3, p_chip.astype(jnp.int32).reshape(1), name="grads_sum_chips")
    t_theirs = _sibling_swap(t_mine, name="grads_join_halves")
    is_south = lax.axis_index("c") == 0
    g_big = jnp.concatenate([jnp.where(is_south, t_mine, t_theirs), jnp.where(is_south, t_theirs, t_mine)],
                            axis=0)

    small_local = {"norm_mix_g": jnp.concatenate([d_nm0, d_nm1], axis=0),
                   "norm_ffn_g": jnp.concatenate([d_nf0, d_nf1], axis=0),
                   "fox_b_f": d_b_f[None, :], "kv_norm_g": d_kv_g[0], "mla_kv_a_norm_g": d_kv_a_g[0],
                   "mla_q_a_norm_g": d_q_a_g, "final_norm_g": d_final_g[0]}
    small_shapes = [w_in[n].shape for n in _SMALL]
    small_rows = sum(_part_rows(s, SMALL_PART_ROWS) for s in small_shapes)
    pack_small = lambda arrs: _pack(arrs, small_rows, F32, SMALL_PART_ROWS)
    g_small = _all_reduce_small(pack_small([small_local[n] for n in _SMALL]), name="grads_small")

    grads = dict(zip(big_names, _unpack(g_big, shard_shapes)))
    delta, new_m, new_v = {}, {}, {}
    for n, shape in zip(big_names, shard_shapes):
        flat = lambda a: a.reshape(-1, shape[-1])
        outs = _adamw(flat(w_in[n]), flat(grads[n]), flat(m_in[n]), flat(v_in[n]), name=f"adamw_{n}")
        delta[n], new_m[n], new_v[n] = (o.reshape(shape) for o in outs)
    sm_outs = _adamw(pack_small([w_in[n] for n in _SMALL]), g_small, pack_small([m_in[n] for n in _SMALL]),
                     pack_small([v_in[n] for n in _SMALL]), name="adamw_small")
    for res, packed in zip((grads, delta, new_m, new_v), (g_small,) + tuple(sm_outs)):
        res.update(zip(_SMALL, _unpack(packed, small_shapes, SMALL_PART_ROWS)))

    return (loss, grad_x[None], *[grads[n] for n in _WEIGHTS], *[delta[n] for n in _WEIGHTS],
            *[new_m[n] for n in _WEIGHTS], *[new_v[n] for n in _WEIGHTS])
```

```python
import math

import numpy as np
import jax
import jax.numpy as jnp
from jax import lax
from jax.experimental import pallas as pl
from jax.experimental.pallas import tpu as pltpu

F32 = jnp.float32
BF16 = jnp.bfloat16

FOX_HEADS = 16
MLA_HEADS = 8
QK_NOPE_DIM = 128
ROPE_BASE = 10000.0
EPS = 1e-6

ADAM_LR = 0.001
ADAM_B1 = 0.9
ADAM_B2 = 0.999
ADAM_EPS = 1e-08
ADAM_WD = 0.01
ADAM_STEP = 10

N_CHIPS = 4
N_DEV = 8
PACK_LANES = 1024
PACK_PART_ROWS = 16
SMALL_PART_ROWS = 8
PACK_ROWS_MULT = 1024
VMEM_LIMIT_BYTES = 48 * 1024 * 1024
LANE_TILE = 128
MATMUL_BLOCK = 1024
MATMUL_WIDE_BLOCK = 2048
MATMUL_DEPTH = 2048
ATTN_BLOCK_Q = 1024
ATTN_BLOCK_K = 1024
ATTN_FWD_LANES = 1024
FOX_BWD_HEADS_PER_STEP = 4
MLA_BWD_HEADS_PER_STEP = 2
ATTN_SUB_ROWS = 256
FOX_FWD_SUB_ROWS = (1024, 512)
MLA_FWD_SUB_ROWS = (256, 256)
NEG_BIG = -1e30
MESH = pl.DeviceIdType.MESH


def _round_up(n, m):
    return -(-n // m) * m


def _blk(dim, pref, mult=128):
    if dim <= pref:
        return dim
    b = (pref // mult) * mult
    while b >= mult:
        if dim % b == 0:
            return b
        b -= mult
    return dim


def _params(sem=None):
    return pltpu.CompilerParams(dimension_semantics=sem, vmem_limit_bytes=VMEM_LIMIT_BYTES)


_DIMS = {"nn": (((1,), (0,)), ((), ())), "nt": (((1,), (1,)), ((), ())), "tn": (((0,), (0,)), ((), ()))}


def _matmul(a, b, *, mode, out_dtypes, name, epilogue=None, extras=(), placed=None, by_chip=False):
    if mode == "tn":
        kdim, m = a.shape
    else:
        m, kdim = a.shape
    n = b.shape[0] if mode == "nt" else b.shape[1]
    bm, bk = _blk(m, MATMUL_BLOCK), _blk(kdim, MATMUL_DEPTH)
    bn = _blk(n, MATMUL_WIDE_BLOCK if (mode != "tn" and kdim <= MATMUL_BLOCK) else MATMUL_BLOCK)
    if by_chip:
        bn = _blk(n // N_CHIPS, bn)
    nk = kdim // bk
    n_extra, n_out = len(extras), len(out_dtypes)
    n_placed = 0 if placed is None else 2
    dims = _DIMS[mode]

    def body(a_ref, b_ref, *rest):
        placed_refs = rest[:n_placed]
        rest = rest[n_placed:]
        extra_refs = rest[:n_extra]
        out_refs = rest[n_extra:n_extra + n_out]

        def finish(acc):
            if n_placed:
                acc = acc + lax.dot_general(placed_refs[0][...], placed_refs[1][...], _DIMS["nn"],
                                            preferred_element_type=F32)
            res = (acc,) if epilogue is None else epilogue(acc, *[r[...] for r in extra_refs])
            for o_ref, r in zip(out_refs, res):
                o_ref[...] = r.astype(o_ref.dtype)

        part = lax.dot_general(a_ref[...].astype(BF16), b_ref[...].astype(BF16), dims, preferred_element_type=F32)
        if nk == 1:
            finish(part)
            return
        acc_ref = rest[n_extra + n_out]
        k = pl.program_id(2)

        @pl.when(k == 0)
        def _():
            acc_ref[...] = part

        @pl.when((k > 0) & (k < nk - 1))
        def _():
            acc_ref[...] += part

        @pl.when(k == nk - 1)
        def _():
            finish(acc_ref[...] + part)

    if mode == "tn":
        a_spec = pl.BlockSpec((bk, bm), lambda i, j, k: (k, i))
    else:
        a_spec = pl.BlockSpec((bm, bk), lambda i, j, k: (i, k))
    if mode == "nt":
        b_spec = pl.BlockSpec((bn, bk), lambda i, j, k: (j, k))
    else:
        b_spec = pl.BlockSpec((bk, bn), lambda i, j, k: (k, j))
    tile = pl.BlockSpec((bm, bn), lambda i, j, k: (i, j))
    placed_specs = []
    if n_placed:
        k2 = placed[0].shape[1]
        placed_specs = [pl.BlockSpec((bm, k2), lambda i, j, k: (i, 0)), pl.BlockSpec((k2, bn), lambda i, j, k: (0, j))]
    out_tile, out_dims = tile, (m, n)
    if by_chip:
        per_chip = n // N_CHIPS // bn
        out_tile = pl.BlockSpec((None, bm, bn), lambda i, j, k: (j // per_chip, i, j % per_chip))
        out_dims = (N_CHIPS, m, n // N_CHIPS)
    outs = pl.pallas_call(
        body, name=name,
        grid=(m // bm, n // bn, nk),
        in_specs=[a_spec, b_spec] + placed_specs + [tile] * n_extra,
        out_specs=[out_tile] * n_out,
        out_shape=[jax.ShapeDtypeStruct(out_dims, dt) for dt in out_dtypes],
        scratch_shapes=[pltpu.VMEM((bm, bn), F32)] if nk > 1 else [],
        compiler_params=_params(("parallel", "parallel", "arbitrary")),
    )(a, b, *(placed or ()), *extras)
    return outs[0] if n_out == 1 else outs


def _rms_fwd(x, gains, name):
    s = x.shape[0]
    g, w = gains.shape
    bs = _blk(s, 512, 8)

    def body(x_ref, g_ref, *out_refs):
        xv = x_ref[...]
        y = xv * lax.rsqrt(jnp.mean(xv * xv, axis=-1, keepdims=True) + EPS)
        for i, o_ref in enumerate(out_refs):
            o_ref[...] = (y * g_ref[i:i + 1, :]).astype(o_ref.dtype)

    row = pl.BlockSpec((bs, w), lambda i: (i, 0))
    outs = pl.pallas_call(
        body, name=name, grid=(s // bs,),
        in_specs=[row, pl.BlockSpec((g, w), lambda i: (0, 0))],
        out_specs=[row] * g,
        out_shape=[jax.ShapeDtypeStruct((s, w), BF16)] * g,
        compiler_params=_params(("parallel",)),
    )(x, gains)
    return outs


def _rms_bwd(x, branches, resid, name):
    s = x.shape[0]
    w = branches[0][0].shape[1]
    nb = len(branches)
    bs = _blk(s, 512, 8)
    has_resid = resid is not None

    def body(x_ref, *rest):
        g_refs = rest[:nb]
        dy_refs = rest[nb:2 * nb]
        pos = 2 * nb
        r_ref = rest[pos] if has_resid else None
        pos += int(has_resid)
        dx_ref = rest[pos]
        dg_refs = rest[pos + 1:pos + 1 + nb]
        i = pl.program_id(0)

        @pl.when(i == 0)
        def _():
            for dg_ref in dg_refs:
                dg_ref[...] = jnp.zeros_like(dg_ref)

        xv = x_ref[...]
        rstd = lax.rsqrt(jnp.mean(xv * xv, axis=-1, keepdims=True) + EPS)
        xhat = xv * rstd
        dx = r_ref[...] if has_resid else jnp.zeros_like(xv)
        for g_ref, dy_ref, dg_ref in zip(g_refs, dy_refs, dg_refs):
            dy = dy_ref[...].astype(F32)
            dyg = dy * g_ref[...]
            dx = dx + rstd * (dyg - xhat * jnp.mean(dyg * xhat, axis=-1, keepdims=True))
            dg_ref[...] += jnp.sum(dy * xhat, axis=0, keepdims=True)
        dx_ref[...] = dx

    row = pl.BlockSpec((bs, w), lambda i: (i, 0))
    vec = pl.BlockSpec((1, w), lambda i: (0, 0))
    args = [x] + [g for g, _ in branches] + [dy for _, dy in branches] + ([resid] if has_resid else [])
    outs = pl.pallas_call(
        body, name=name, grid=(s // bs,),
        in_specs=[row] + [vec] * nb + [row] * nb + ([row] if has_resid else []),
        out_specs=[row] + [vec] * nb,
        out_shape=[jax.ShapeDtypeStruct((s, w), F32)] + [jax.ShapeDtypeStruct((1, w), F32)] * nb,
        compiler_params=_params(("arbitrary",)),
    )(*args)
    return outs[0], list(outs[1:])


def _loss_head(x, g, target, name):
    s, w = x.shape
    bs = _blk(s, 512, 8)

    def body(x_ref, g_ref, t_ref, loss_ref, dx_ref, dg_ref):
        i = pl.program_id(0)

        @pl.when(i == 0)
        def _():
            loss_ref[...] = jnp.zeros_like(loss_ref)
            dg_ref[...] = jnp.zeros_like(dg_ref)

        xv = x_ref[...]
        gv = g_ref[...]
        rstd = lax.rsqrt(jnp.mean(xv * xv, axis=-1, keepdims=True) + EPS)
        xhat = xv * rstd
        err = xhat * gv - t_ref[...]
        loss_ref[...] += 0.5 * jnp.sum(jnp.mean(err * err, axis=-1, keepdims=True))
        dy = err * (1.0 / w)
        dyg = dy * gv
        dx_ref[...] = rstd * (dyg - xhat * jnp.mean(dyg * xhat, axis=-1, keepdims=True))
        dg_ref[...] += jnp.sum(dy * xhat, axis=0, keepdims=True)

    row = pl.BlockSpec((bs, w), lambda i: (i, 0))
    vec = pl.BlockSpec((1, w), lambda i: (0, 0))
    return pl.pallas_call(
        body, name=name, grid=(s // bs,),
        in_specs=[row, vec, row],
        out_specs=[pl.BlockSpec((8, 128), lambda i: (0, 0)), row, vec],
        out_shape=[jax.ShapeDtypeStruct((8, 128), F32), jax.ShapeDtypeStruct((s, w), F32),
                   jax.ShapeDtypeStruct((1, w), F32)],
        compiler_params=_params(("arbitrary",)),
    )(x, g, target)


def _rope(a, b, cos, sin, sign, name):
    g, s, w = a.shape
    bs = _blk(s, 1024, 8)

    def body(a_ref, b_ref, c_ref, s_ref, o1_ref, o2_ref):
        av = jnp.sum(a_ref[...].astype(F32), axis=0)
        bv = jnp.sum(b_ref[...].astype(F32), axis=0)
        cv, sv = c_ref[...], s_ref[...] * sign
        o1_ref[...] = av * cv - bv * sv
        o2_ref[...] = bv * cv + av * sv

    grp = pl.BlockSpec((g, bs, w), lambda i: (0, i, 0))
    row = pl.BlockSpec((bs, w), lambda i: (i, 0))
    return pl.pallas_call(
        body, name=name, grid=(s // bs,),
        in_specs=[grp, grp, row, row], out_specs=[row, row],
        out_shape=[jax.ShapeDtypeStruct((s, w), F32)] * 2,
        compiler_params=_params(("parallel",)),
    )(a, b, cos, sin)


def _causal_table(s, bq, bk, q_major):
    nq, nk = s // bq, s // bk
    rows = []
    if q_major:
        for qi in range(nq):
            kmax = (qi * bq + bq - 1) // bk
            for ki in range(kmax + 1):
                rows.append((qi, ki, int(ki * bk + bk - 1 > qi * bq), int(ki == 0), int(ki == kmax)))
    else:
        for ki in range(nk):
            qmin = (ki * bk) // bq
            for qi in range(qmin, nq):
                rows.append((qi, ki, int(ki * bk + bk - 1 > qi * bq), int(qi == qmin), int(qi == nq - 1)))
    return jnp.asarray(np.array(rows, np.int32).T)


def _causal_keep(q0, k0, nq, nk, transposed):
    if transposed:
        kpos = k0 + lax.broadcasted_iota(jnp.int32, (nk, nq), 0)
        qpos = q0 + lax.broadcasted_iota(jnp.int32, (nk, nq), 1)
    else:
        qpos = q0 + lax.broadcasted_iota(jnp.int32, (nq, nk), 0)
        kpos = k0 + lax.broadcasted_iota(jnp.int32, (nq, nk), 1)
    return kpos <= qpos


def _sub_tiles(n_rows, n_cols, masked, square, rows_are_keys, sub_rows):
    sub = min(sub_rows, n_rows)
    out = []
    for r0 in range(0, n_rows, sub):
        if masked and square:
            c0, nc = (r0, n_cols - r0) if rows_are_keys else (0, r0 + sub)
        else:
            c0, nc = 0, n_cols
        out.append((r0, sub, c0, nc))
    return out


_NT = (((1,), (1,)), ((), ()))
_NN = (((1,), (0,)), ((), ()))


def _attn_specs(bq, bk):
    qspec = lambda d: pl.BlockSpec((bq, d), lambda hh, t, tb: (tb[0, t], hh))
    kspec = lambda d: pl.BlockSpec((bk, d), lambda hh, t, tb: (tb[1, t], hh))
    return qspec, kspec


def _split3_cols(x):
    hi = x.astype(BF16).astype(F32)
    rest = x - hi
    mid = rest.astype(BF16).astype(F32)
    lo = (rest - mid).astype(BF16).astype(F32)
    return hi, mid, lo


def _place3(base, col, pieces, sign):
    lane = lax.broadcasted_iota(jnp.int32, base.shape, 1)
    out = base.astype(F32)
    for i, piece in enumerate(pieces):
        out = jnp.where(lane == col + i, sign * piece, out)
    return out.astype(BF16)


def _flash_fwd(qa, ka, va, heads, l_col, lse_col, sub_rows, name):
    s = qa.shape[0]
    da, dv = qa.shape[1] // heads, va.shape[1] // heads
    hps = max(n for n in range(1, ATTN_FWD_LANES // max(da, dv) + 1) if heads % n == 0)
    bq, bk = _blk(s, ATTN_BLOCK_Q), _blk(s, ATTN_BLOCK_K)
    tab = _causal_table(s, bq, bk, True)

    def body(tab_ref, q_ref, k_ref, v_ref, o_ref, qb_ref, m_sc, acc_sc):
        t = pl.program_id(1)
        qi, ki = tab_ref[0, t], tab_ref[1, t]

        @pl.when(tab_ref[3, t] == 1)
        def _():
            m_sc[...] = jnp.full_like(m_sc, NEG_BIG)
            acc_sc[...] = jnp.zeros_like(acc_sc)

        def step(masked):
            for hh in range(hps):
                qc, vc = slice(hh * da, (hh + 1) * da), slice(hh * dv, (hh + 1) * dv)
                for r0, nr, c0, nc in _sub_tiles(bq, bk, masked, bq == bk, False, sub_rows[int(masked)]):
                    sc = lax.dot_general(q_ref[r0:r0 + nr, qc], k_ref[c0:c0 + nc, qc], _NT,
                                         preferred_element_type=F32)
                    if masked:
                        sc = jnp.where(_causal_keep(qi * bq + r0, ki * bk + c0, nr, nc, False), sc, NEG_BIG)
                    m_prev = m_sc[hh, r0:r0 + nr, :]
                    m_new = jnp.maximum(m_prev, jnp.max(sc, axis=-1, keepdims=True))
                    p = jnp.exp(sc - m_new).astype(BF16)
                    acc_sc[r0:r0 + nr, vc] = jnp.exp(m_prev - m_new) * acc_sc[r0:r0 + nr, vc] + lax.dot_general(
                        p, v_ref[c0:c0 + nc, vc], _NN, preferred_element_type=F32)
                    m_sc[hh, r0:r0 + nr, :] = m_new

        @pl.when(tab_ref[2, t] == 1)
        def _():
            step(True)

        @pl.when(tab_ref[2, t] == 0)
        def _():
            step(False)

        @pl.when(tab_ref[4, t] == 1)
        def _():
            for hh in range(hps):
                qc, vc = slice(hh * da, (hh + 1) * da), slice(hh * dv, (hh + 1) * dv)
                acc = acc_sc[:, vc]
                lane = lax.broadcasted_iota(jnp.int32, acc.shape, 1)
                l = jnp.sum(jnp.where(lane == l_col, acc, 0.0), axis=-1, keepdims=True)
                o_ref[:, vc] = (acc / l).astype(o_ref.dtype)
                lse = m_sc[hh] + jnp.log(l)
                qb_ref[:, qc] = _place3(q_ref[:, qc], lse_col, _split3_cols(lse), -1.0)

    qspec, kspec = _attn_specs(bq, bk)
    return pl.pallas_call(
        body, name=name,
        grid_spec=pltpu.PrefetchScalarGridSpec(
            num_scalar_prefetch=1, grid=(heads // hps, tab.shape[1]),
            in_specs=[qspec(hps * da), kspec(hps * da), kspec(hps * dv)],
            out_specs=[qspec(hps * dv), qspec(hps * da)],
            scratch_shapes=[pltpu.VMEM((hps, bq, 1), F32), pltpu.VMEM((bq, hps * dv), F32)]),
        out_shape=[jax.ShapeDtypeStruct((s, heads * dv), BF16), jax.ShapeDtypeStruct((s, heads * da), BF16)],
        compiler_params=_params(("parallel", "arbitrary")),
    )(tab, qa, ka, va)


def _delta_epilogue(dv, delta_col):
    def epilogue(acc, o_tile):
        heads_out = []
        for hh in range(acc.shape[1] // dv):
            vc = slice(hh * dv, (hh + 1) * dv)
            dov = acc[:, vc].astype(BF16)
            delta = jnp.sum(dov.astype(F32) * o_tile[:, vc].astype(F32), axis=-1, keepdims=True)
            heads_out.append(_place3(dov, delta_col, _split3_cols(delta), 1.0))
        return (jnp.concatenate(heads_out, axis=1),)
    return epilogue


_TN =(((0,), (0,)), ((), ()))


def _flash_bwd(qa, ka, va, doa, heads, hps, name, sum_cols=None):
    s = qa.shape[0]
    da, dv = qa.shape[1] // heads, va.shape[1] // heads
    h = heads // hps
    bq, bk = _blk(s, ATTN_BLOCK_Q), _blk(s, ATTN_BLOCK_K)
    tab = _causal_table(s, bq, bk, False)
    n_tiles = tab.shape[1]
    n_sum = 0 if sum_cols is None else 2

    def head_column(acc, col):
        out = jnp.zeros((acc.shape[0], hps), F32)
        lane = lax.broadcasted_iota(jnp.int32, (acc.shape[0], da), 1)
        pick = lax.broadcasted_iota(jnp.int32, out.shape, 1)
        for hh in range(hps):
            val = jnp.sum(jnp.where(lane == col, acc[:, hh * da:(hh + 1) * da], 0.0), axis=-1, keepdims=True)
            out = jnp.where(pick == hh, val, out)
        return out

    def body(tab_ref, q_ref, k_ref, v_ref, do_ref, dq_ref, dk_ref, dv_ref, *rest):
        sum_refs, (dk_sc, dv_sc) = rest[:n_sum], rest[n_sum:]
        t = pl.program_id(1)
        qi, ki = tab_ref[0, t], tab_ref[1, t]

        @pl.when(t == 0)
        def _():
            dq_ref[...] = jnp.zeros_like(dq_ref)

        @pl.when(tab_ref[3, t] == 1)
        def _():
            dk_sc[...] = jnp.zeros_like(dk_sc)
            dv_sc[...] = jnp.zeros_like(dv_sc)

        def step(masked):
            for hh in range(hps):
                qc, vc = slice(hh * da, (hh + 1) * da), slice(hh * dv, (hh + 1) * dv)
                for r0, nr, c0, nc in _sub_tiles(bk, bq, masked, bq == bk, True, ATTN_SUB_ROWS):
                    qv, dov, kv = q_ref[c0:c0 + nc, qc], do_ref[c0:c0 + nc, vc], k_ref[r0:r0 + nr, qc]
                    st = lax.dot_general(kv, qv, _NT, preferred_element_type=F32)
                    if masked:
                        st = jnp.where(_causal_keep(qi * bq + c0, ki * bk + r0, nc, nr, True), st, NEG_BIG)
                    pt = jnp.exp(st)
                    dv_sc[r0:r0 + nr, vc] += lax.dot_general(pt.astype(BF16), dov, _NN, preferred_element_type=F32)
                    dpt = lax.dot_general(v_ref[r0:r0 + nr, vc], dov, _NT, preferred_element_type=F32)
                    dst = (pt * dpt).astype(BF16)
                    dk_sc[r0:r0 + nr, qc] += lax.dot_general(dst, qv, _NN, preferred_element_type=F32)
                    q_rows = pl.ds(pl.multiple_of(qi * bq + c0, ATTN_SUB_ROWS), nc)
                    dq_ref[q_rows, qc] += lax.dot_general(dst, kv, _TN, preferred_element_type=F32)

        @pl.when(tab_ref[2, t] == 1)
        def _():
            step(True)

        @pl.when(tab_ref[2, t] == 0)
        def _():
            step(False)

        @pl.when(tab_ref[4, t] == 1)
        def _():
            dk_ref[...] = dk_sc[...]
            dv_ref[...] = dv_sc[...]
            if n_sum:
                sum_refs[1][...] = head_column(dk_sc[...], sum_cols[1])

        if n_sum:
            @pl.when(t == n_tiles - 1)
            def _():
                sum_refs[0][...] = head_column(dq_ref[...], sum_cols[0])

    qspec, kspec = _attn_specs(bq, bk)
    out_specs = [pl.BlockSpec((s, hps * da), lambda hh, t, tb: (0, hh), pipeline_mode=pl.Buffered(1)),
                 kspec(hps * da), kspec(hps * dv)]
    out_shape = [jax.ShapeDtypeStruct((s, heads * da), F32), jax.ShapeDtypeStruct((s, heads * da), F32),
                 jax.ShapeDtypeStruct((s, heads * dv), F32)]
    if n_sum:
        out_specs += [pl.BlockSpec((None, s, hps), lambda hh, t, tb: (hh, 0, 0), pipeline_mode=pl.Buffered(1)),
                      pl.BlockSpec((None, bk, hps), lambda hh, t, tb: (hh, tb[1, t], 0))]
        out_shape += [jax.ShapeDtypeStruct((h, s, hps), F32)] * 2
    return pl.pallas_call(
        body, name=name,
        grid_spec=pltpu.PrefetchScalarGridSpec(
            num_scalar_prefetch=1, grid=(h, n_tiles),
            in_specs=[qspec(hps * da), kspec(hps * da), kspec(hps * dv), qspec(hps * dv)],
            out_specs=out_specs,
            scratch_shapes=[pltpu.VMEM((bk, hps * da), F32), pltpu.VMEM((bk, hps * dv), F32)]),
        out_shape=out_shape,
        compiler_params=_params(("parallel", "arbitrary")),
    )(tab, qa, ka, va, doa)


def _split3(x):
    hi = lax.reduce_precision(x, 8, 7)
    rest = x - hi
    mid = lax.reduce_precision(rest, 8, 7)
    lo = lax.reduce_precision(rest - mid, 8, 7)
    return jnp.stack([hi, mid, lo], axis=-1).astype(BF16)


def _pad_heads(w, heads, width, axis):
    shape = list(w.shape)
    d = shape[axis] // heads
    w = w.reshape(shape[:axis] + [heads, d] + shape[axis + 1:])
    pad = [(0, 0)] * w.ndim
    pad[axis + 1] = (0, width - d)
    return jnp.pad(w, pad).reshape(shape[:axis] + [heads * width] + shape[axis + 1:])


def _unpad_heads(w, heads, d, axis):
    shape = list(w.shape)
    width = shape[axis] // heads
    w = w.reshape(shape[:axis] + [heads, width] + shape[axis + 1:])
    w = lax.slice_in_dim(w, 0, d, axis=axis + 1)
    return w.reshape(shape[:axis] + [heads * d] + shape[axis + 1:])


def _placement(rows, heads, width, entries):
    e = np.zeros((rows, heads * width), np.float32)
    for row, col, val in entries:
        for hh in range(heads):
            e[row(hh) if callable(row) else row, hh * width + col] = val
    return jnp.asarray(e, BF16)


def _rope_mix(a, b, cos_t, sin_t, scale, heads, name):
    s = a.shape[0]
    d = a.shape[1] // heads
    bs = _blk(s, 1024, 8)

    def body(a_ref, b_ref, c_ref, s_ref, o_ref):
        o_ref[...] = ((a_ref[...] * c_ref[...] + b_ref[...] * s_ref[...]) * scale).astype(o_ref.dtype)

    blk = pl.BlockSpec((bs, d), lambda i, hh: (i, hh))
    tbl = pl.BlockSpec((bs, d), lambda i, hh: (i, 0))
    return pl.pallas_call(
        body, name=name, grid=(s // bs, heads), in_specs=[blk, blk, tbl, tbl], out_specs=blk,
        out_shape=jax.ShapeDtypeStruct(a.shape, BF16),
        compiler_params=_params(("parallel", "parallel")),
    )(a, b, cos_t, sin_t)


def _rope_proj(x, w_a, w_b, cos_t, sin_t, scale, heads, name):
    s, kdim = x.shape
    d = w_a.shape[1] // heads
    hpt = max(n for n in range(1, max(1, MATMUL_BLOCK // d) + 1) if heads % n == 0)
    bm = _blk(s, MATMUL_BLOCK)
    cos_w, sin_w = jnp.tile(cos_t, (1, hpt)), jnp.tile(sin_t, (1, hpt))

    def body(x_ref, wa_ref, wb_ref, c_ref, s_ref, o_ref):
        xv = x_ref[...]
        a = lax.dot_general(xv, wa_ref[...], _NN, preferred_element_type=F32)
        b = lax.dot_general(xv, wb_ref[...], _NN, preferred_element_type=F32)
        o_ref[...] = ((a * c_ref[...] + b * s_ref[...]) * scale).astype(o_ref.dtype)

    wide = pl.BlockSpec((bm, hpt * d), lambda i, j: (i, j))
    tbl = pl.BlockSpec((bm, hpt * d), lambda i, j: (i, 0))
    wgt = pl.BlockSpec((kdim, hpt * d), lambda i, j: (0, j))
    return pl.pallas_call(
        body, name=name, grid=(s // bm, heads // hpt),
        in_specs=[pl.BlockSpec((bm, kdim), lambda i, j: (i, 0)), wgt, wgt, tbl, tbl], out_specs=wide,
        out_shape=jax.ShapeDtypeStruct((s, heads * d), BF16),
        compiler_params=_params(("parallel", "parallel")),
    )(x, w_a, w_b, cos_w, sin_w)


def _rope_unmix(g, cos_t, sin_t, scale, heads, name):
    s = g.shape[0]
    d = g.shape[1] // heads
    bs = _blk(s, 1024, 8)

    def body(g_ref, c_ref, s_ref, da_ref, db_ref):
        gv = g_ref[...] * scale
        da_ref[...] = (gv * c_ref[...]).astype(da_ref.dtype)
        db_ref[...] = (gv * s_ref[...]).astype(db_ref.dtype)

    blk = pl.BlockSpec((bs, d), lambda i, hh: (i, hh))
    tbl = pl.BlockSpec((bs, d), lambda i, hh: (i, 0))
    return pl.pallas_call(
        body, name=name, grid=(s // bs, heads), in_specs=[blk, tbl, tbl], out_specs=[blk, blk],
        out_shape=[jax.ShapeDtypeStruct(g.shape, BF16)] * 2,
        compiler_params=_params(("parallel", "parallel")),
    )(g, cos_t, sin_t)


def _adamw(w, g, m, v, name):
    r, wd = w.shape
    br = _blk(r, 512, 8)

    def body(w_ref, g_ref, m_ref, v_ref, d_ref, nm_ref, nv_ref):
        gv = g_ref[...]
        mn = ADAM_B1 * m_ref[...] + (1.0 - ADAM_B1) * gv
        vn = ADAM_B2 * v_ref[...] + (1.0 - ADAM_B2) * (gv * gv)
        m_hat = mn / (1.0 - ADAM_B1 ** ADAM_STEP)
        v_hat = vn / (1.0 - ADAM_B2 ** ADAM_STEP)
        d_ref[...] = -ADAM_LR * (m_hat / (jnp.sqrt(v_hat) + ADAM_EPS) + ADAM_WD * w_ref[...])
        nm_ref[...] = mn
        nv_ref[...] = vn

    row = pl.BlockSpec((br, wd), lambda i: (i, 0))
    return pl.pallas_call(
        body, name=name, grid=(r // br,), in_specs=[row] * 4, out_specs=[row] * 3,
        out_shape=[jax.ShapeDtypeStruct((r, wd), F32)] * 3,
        compiler_params=_params(("parallel",)),
    )(w, g, m, v)


_ANY = pl.BlockSpec(memory_space=pl.ANY)


def _place():
    x, y, c = lax.axis_index("x"), lax.axis_index("y"), lax.axis_index("c")
    chips = [(x, 1 - y), (1 - x, y), (1 - x, 1 - y)]
    return x, y, c, chips


def _all_gather_shards(shard, name):
    r, w = shard.shape
    hr = r // 2
    qr = hr // 2

    def body(x_ref, out_ref, send_sems, recv_sems):
        x, y, c, _ = _place()
        me, sibling, y_nbr, x_nbr = (x, y, c), (x, y, 1 - c), (x, 1 - y, c), (1 - x, y, c)

        def rows(j, half, piece=None):
            if piece is None:
                return out_ref.at[j, pl.ds(pl.multiple_of(half * hr, 16), hr), :]
            return out_ref.at[j, pl.ds(pl.multiple_of(half * hr + piece * qr, 16), qr), :]

        def mine(piece):
            return x_ref.at[pl.ds(pl.multiple_of(c * hr + piece * qr, 16), qr), :]

        def copy(sem, src, dst, to):
            return pltpu.make_async_remote_copy(src_ref=src, dst_ref=dst, send_sem=send_sems.at[sem],
                                                recv_sem=recv_sems.at[sem], device_id=to, device_id_type=MESH)

        sent = [copy(0, mine(0), rows(0, c, 0), y_nbr), copy(1, mine(1), rows(0, c, 1), y_nbr),
                copy(2, mine(0), rows(1, c, 0), x_nbr), copy(3, mine(1), rows(1, c, 1), x_nbr)]
        for cp in sent:
            cp.start()

        def landed(sem, ref):
            copy(sem, ref, ref, me).wait_recv()

        def pass_on(sem, src, dst, to):
            cp = copy(sem, src, dst, to)
            cp.start()
            sent.append(cp)

        landed(2, rows(1, c, 0))
        pass_on(4, rows(1, c, 0), rows(2, c, 0), y_nbr)
        landed(1, rows(0, c, 1))
        pass_on(5, rows(0, c, 1), rows(2, c, 1), x_nbr)
        landed(0, rows(0, c, 0))
        pass_on(6, rows(0, c), rows(0, c), sibling)
        landed(3, rows(1, c, 1))
        pass_on(7, rows(1, c), rows(1, c), sibling)
        landed(4, rows(2, c, 0))
        landed(5, rows(2, c, 1))
        pass_on(8, rows(2, c), rows(2, c), sibling)
        for j in range(3):
            landed(6 + j, rows(j, 1 - c))
        for cp in sent:
            cp.wait_send()

    return pl.pallas_call(
        body, name=name, in_specs=[_ANY], out_specs=_ANY,
        out_shape=jax.ShapeDtypeStruct((N_CHIPS - 1, r, w), shard.dtype),
        scratch_shapes=[pltpu.SemaphoreType.DMA((9,)), pltpu.SemaphoreType.DMA((9,))],
        compiler_params=pltpu.CompilerParams(vmem_limit_bytes=VMEM_LIMIT_BYTES),
    )(shard)


def _sibling_swap_halves(g, name):
    nq, r, w = g.shape
    hr = r // 2

    def body(g_ref, a_ref, send_sems, recv_sems):
        x, y, c, _ = _place()
        sibling = (x, y, 1 - c)
        cps = []
        for q in range(nq):
            cp = pltpu.make_async_remote_copy(
                src_ref=g_ref.at[q, pl.ds(pl.multiple_of((1 - c) * hr, 8), hr), :], dst_ref=a_ref.at[q],
                send_sem=send_sems.at[q], recv_sem=recv_sems.at[q], device_id=sibling, device_id_type=MESH)
            cp.start()
            cps.append(cp)
        for cp in cps:
            cp.wait()

    return pl.pallas_call(
        body, name=name, in_specs=[_ANY], out_specs=_ANY,
        out_shape=jax.ShapeDtypeStruct((nq, hr, w), g.dtype),
        scratch_shapes=[pltpu.SemaphoreType.DMA((nq,)), pltpu.SemaphoreType.DMA((nq,))],
        compiler_params=pltpu.CompilerParams(vmem_limit_bytes=VMEM_LIMIT_BYTES),
    )(g)


def _chip_sum(g, a, c_idx, name):
    nq, r, w = g.shape
    hr = r // 2
    br = _blk(hr, 512, 16)
    nb = hr // br

    def body(c_ref, g_ref, a_ref, o_ref):
        o_ref[...] = (g_ref[...] + a_ref[...]).astype(o_ref.dtype)

    return pl.pallas_call(
        body, name=name,
        grid_spec=pltpu.PrefetchScalarGridSpec(
            num_scalar_prefetch=1, grid=(nq, nb),
            in_specs=[pl.BlockSpec((None, br, w), lambda q, i, cr: (q, cr[0] * nb + i, 0)),
                      pl.BlockSpec((None, br, w), lambda q, i, cr: (q, i, 0))],
            out_specs=pl.BlockSpec((None, br, w), lambda q, i, cr: (q, i, 0))),
        out_shape=jax.ShapeDtypeStruct((nq, hr, w), BF16),
        compiler_params=_params(("parallel", "parallel")),
    )(c_idx, g, a)


def _chip_exchange(s4, name):
    nq, hr, w = s4.shape

    def body(s_ref, b_ref, send_sems, recv_sems):
        x, y, c, chips = _place()
        cps = []
        for j, (cx, cy) in enumerate(chips):
            cp = pltpu.make_async_remote_copy(
                src_ref=s_ref.at[2 * cx + cy], dst_ref=b_ref.at[j],
                send_sem=send_sems.at[j], recv_sem=recv_sems.at[j], device_id=(cx, cy, c), device_id_type=MESH)
            cp.start()
            cps.append(cp)
        for cp in cps:
            cp.wait()

    return pl.pallas_call(
        body, name=name, in_specs=[_ANY], out_specs=_ANY,
        out_shape=jax.ShapeDtypeStruct((nq - 1, hr, w), s4.dtype),
        scratch_shapes=[pltpu.SemaphoreType.DMA((3,)), pltpu.SemaphoreType.DMA((3,))],
        compiler_params=pltpu.CompilerParams(vmem_limit_bytes=VMEM_LIMIT_BYTES),
    )(s4)


def _sum_chips(s4, b3, p_idx, name):
    _, hr, w = s4.shape
    nb3 = b3.shape[0]
    br = _blk(hr, 512, 16)

    def body(p_ref, s_ref, b_ref, o_ref):
        acc = s_ref[...].astype(F32)
        for j in range(nb3):
            acc = acc + b_ref[j].astype(F32)
        o_ref[...] = acc

    return pl.pallas_call(
        body, name=name,
        grid_spec=pltpu.PrefetchScalarGridSpec(
            num_scalar_prefetch=1, grid=(hr // br,),
            in_specs=[pl.BlockSpec((None, br, w), lambda i, pr: (pr[0], i, 0)),
                      pl.BlockSpec((nb3, br, w), lambda i, pr: (0, i, 0))],
            out_specs=pl.BlockSpec((br, w), lambda i, pr: (i, 0))),
        out_shape=jax.ShapeDtypeStruct((hr, w), F32),
        compiler_params=_params(("parallel",)),
    )(p_idx, s4, b3)


def _sibling_swap(t, name):
    hr, w = t.shape

    def body(t_ref, o_ref, send_sem, recv_sem):
        x, y, c, _ = _place()
        cp = pltpu.make_async_remote_copy(src_ref=t_ref, dst_ref=o_ref, send_sem=send_sem, recv_sem=recv_sem,
                                          device_id=(x, y, 1 - c), device_id_type=MESH)
        cp.start()
        cp.wait()

    return pl.pallas_call(
        body, name=name, in_specs=[_ANY], out_specs=_ANY,
        out_shape=jax.ShapeDtypeStruct((hr, w), t.dtype),
        scratch_shapes=[pltpu.SemaphoreType.DMA, pltpu.SemaphoreType.DMA],
        compiler_params=pltpu.CompilerParams(vmem_limit_bytes=VMEM_LIMIT_BYTES),
    )(t)


def _all_reduce_small(v, name):
    r, w = v.shape

    def body(v_ref, o_ref, slots, send_sems, recv_sems):
        x, y, c, _ = _place()
        me = 4 * x + 2 * y + c
        slots[me] = v_ref[...]
        cps = []
        for k in range(1, N_DEV):
            fx, fy, fc = (k >> 2) & 1, (k >> 1) & 1, k & 1
            to = (x ^ fx, y ^ fy, c ^ fc)
            cp = pltpu.make_async_remote_copy(
                src_ref=v_ref, dst_ref=slots.at[me], send_sem=send_sems.at[k - 1], recv_sem=recv_sems.at[k - 1],
                device_id=to, device_id_type=MESH)
            cp.start()
            cps.append(cp)
        for k in range(1, N_DEV):
            fx, fy, fc = (k >> 2) & 1, (k >> 1) & 1, k & 1
            src_dev = 4 * (x ^ fx) + 2 * (y ^ fy) + (c ^ fc)
            pltpu.make_async_remote_copy(
                src_ref=v_ref, dst_ref=slots.at[src_dev], send_sem=send_sems.at[k - 1],
                recv_sem=recv_sems.at[k - 1], device_id=(x, y, c), device_id_type=MESH).wait_recv()
        for cp in cps:
            cp.wait_send()
        acc = slots[0]
        for d in range(1, N_DEV):
            acc = acc + slots[d]
        o_ref[...] = acc

    return pl.pallas_call(
        body, name=name,
        in_specs=[pl.BlockSpec(memory_space=pltpu.VMEM)], out_specs=pl.BlockSpec(memory_space=pltpu.VMEM),
        out_shape=jax.ShapeDtypeStruct((r, w), F32),
        scratch_shapes=[pltpu.VMEM((N_DEV, r, w), F32), pltpu.SemaphoreType.DMA((N_DEV - 1,)),
                        pltpu.SemaphoreType.DMA((N_DEV - 1,))],
        compiler_params=pltpu.CompilerParams(vmem_limit_bytes=VMEM_LIMIT_BYTES),
    )(v)


def _part_rows(shape, part_rows=PACK_PART_ROWS):
    assert shape[-1] <= PACK_LANES
    return _round_up(math.prod(shape[:-1]), part_rows)


def _packed_rows(shapes):
    return _round_up(sum(_part_rows(s) for s in shapes), PACK_ROWS_MULT)


def _pack(arrs, total_rows, dtype, part_rows=PACK_PART_ROWS):
    parts = []
    for a in arrs:
        a2 = a.reshape(-1, a.shape[-1]).astype(dtype)
        rows = _part_rows(a.shape, part_rows)
        parts.append(jnp.pad(a2, ((0, rows - a2.shape[0]), (0, PACK_LANES - a2.shape[1]))))
    used = sum(p.shape[0] for p in parts)
    if total_rows > used:
        parts.append(jnp.zeros((total_rows - used, PACK_LANES), dtype))
    return jnp.concatenate(parts, axis=0)


def _unpack(packed, shapes, part_rows=PACK_PART_ROWS):
    out, r0 = [], 0
    for s in shapes:
        out.append(packed[r0:r0 + math.prod(s[:-1]), :s[-1]].reshape(s))
        r0 += _part_rows(s, part_rows)
    return out


_BIG = (("fox_w_in", 2), ("fox_w_out", 1), ("mla_w_kv_a", 0), ("mla_w_kv_b", 1), ("mla_w_q_a", 1),
        ("mla_w_q_b", 2), ("mla_w_out", 1), ("ffn_w_up", 2), ("ffn_w_down", 1))
_SMALL = ("norm_mix_g", "norm_ffn_g", "fox_b_f", "kv_norm_g", "mla_kv_a_norm_g", "mla_q_a_norm_g", "final_norm_g")
_WEIGHTS = ("norm_mix_g", "norm_ffn_g", "fox_w_in", "fox_b_f", "fox_w_out", "kv_norm_g", "mla_w_kv_a",
            "mla_kv_a_norm_g", "mla_w_kv_b", "mla_w_q_a", "mla_q_a_norm_g", "mla_w_q_b", "mla_w_out",
            "ffn_w_up", "ffn_w_down", "final_norm_g")


def _ffn_fwd(x, h, w_up, w_down, tag):
    def relu_sq(acc):
        r = jnp.maximum(acc, 0.0)
        return r, r * r

    r, a = _matmul(h, w_up, mode="nn", out_dtypes=(BF16, BF16), epilogue=relu_sq, name=f"{tag}_up")
    x_out = _matmul(a, w_down, mode="nn", out_dtypes=(F32,), epilogue=lambda acc, res: (acc + res,),
                    extras=(x,), name=f"{tag}_down")
    return x_out, r, a


def _ffn_bwd(dx_out, x_in, h, r, a, g_norm, w_up, w_down, tag):
    d_u = _matmul(dx_out, w_down, mode="nt", out_dtypes=(BF16,), epilogue=lambda acc, rr: (acc * (2.0 * rr.astype(F32)),),
                  extras=(r,), name=f"{tag}_d_act")
    d_w_down = _matmul(a, dx_out, mode="tn", out_dtypes=(F32,), name=f"{tag}_d_w_down")
    d_w_up = _matmul(h, d_u, mode="tn", out_dtypes=(F32,), by_chip=True, name=f"{tag}_d_w_up")
    d_h = _matmul(d_u, w_up, mode="nt", out_dtypes=(F32,), name=f"{tag}_d_h")
    dx_in, (d_g,) = _rms_bwd(x_in, [(g_norm, d_h)], dx_out, name=f"{tag}_d_norm")
    return dx_in, d_w_up, d_w_down, d_g


def kernel(x, norm_mix_g, norm_ffn_g, fox_w_in, fox_b_f, fox_w_out, kv_norm_g, mla_w_kv_a, mla_kv_a_norm_g, mla_w_kv_b, mla_w_q_a, mla_q_a_norm_g, mla_w_q_b, mla_w_out, ffn_w_up, ffn_w_down, final_norm_g, loss_target, m_norm_mix_g, m_norm_ffn_g, m_fox_w_in, m_fox_b_f, m_fox_w_out, m_kv_norm_g, m_mla_w_kv_a, m_mla_kv_a_norm_g, m_mla_w_kv_b, m_mla_w_q_a, m_mla_q_a_norm_g, m_mla_w_q_b, m_mla_w_out, m_ffn_w_up, m_ffn_w_down, m_final_norm_g, v_norm_mix_g, v_norm_ffn_g, v_fox_w_in, v_fox_b_f, v_fox_w_out, v_kv_norm_g, v_mla_w_kv_a, v_mla_kv_a_norm_g, v_mla_w_kv_b, v_mla_w_q_a, v_mla_q_a_norm_g, v_mla_w_q_b, v_mla_w_out, v_ffn_w_up, v_ffn_w_down, v_final_norm_g):
    args = dict(locals())
    w_in = {n: args[n] for n in _WEIGHTS}
    m_in = {n: args["m_" + n] for n in _WEIGHTS}
    v_in = {n: args["v_" + n] for n in _WEIGHTS}

    xs = x[0]
    seq, d_model = xs.shape
    tgt = loss_target[0]
    fox_h, mla_h, nope = FOX_HEADS, MLA_HEADS, QK_NOPE_DIM
    kv_rank = mla_kv_a_norm_g.shape[0]
    rope = mla_w_kv_a.shape[1] - kv_rank
    half = rope // 2
    q_rank = mla_q_a_norm_g.shape[1]
    v_dim = mla_w_kv_b.shape[1] * N_CHIPS // mla_h - nope
    fox_w = fox_w_out.shape[1] * N_CHIPS
    fox_dh = fox_w // fox_h

    big_names = [n for n, _ in _BIG]
    shard_shapes = [w_in[n].shape for n in big_names]
    rows = _packed_rows(shard_shapes)
    my_shard = _pack([w_in[n] for n in big_names], rows, BF16)
    others = _all_gather_shards(my_shard, name="gather_weights")
    by_relation = jnp.concatenate([my_shard[None], others], axis=0)
    p_chip = 2 * lax.axis_index("x") + lax.axis_index("y")
    full = {}
    for q in range(N_CHIPS):
        shard_q = lax.dynamic_index_in_dim(by_relation, p_chip ^ q, axis=0, keepdims=False)
        for (n, ax), piece in zip(_BIG, _unpack(shard_q, shard_shapes)):
            full.setdefault(n, []).append(piece)
    full = {n: jnp.concatenate(full[n], axis=ax) for n, ax in _BIG}

    fox_scale = fox_dh ** -0.5
    fox_wd = _round_up(fox_dh + 9, LANE_TILE)
    fox_vwd = _round_up(fox_dh + 4, LANE_TILE)
    w_fox_in = full["fox_w_in"][0]
    w_fq = _pad_heads(w_fox_in[:, :fox_w] * fox_scale, fox_h, fox_wd, 1)
    w_fk = _pad_heads(w_fox_in[:, fox_w:2 * fox_w], fox_h, fox_wd, 1)
    w_fv = _pad_heads(w_fox_in[:, 2 * fox_w:3 * fox_w], fox_h, fox_vwd, 1)
    w_gate = w_fox_in[:, 3 * fox_w:]
    w_fox_out = _pad_heads(full["fox_w_out"][0], fox_h, fox_vwd, 0)
    n_cx = _round_up(3 * fox_h + 1, LANE_TILE)
    c_piece = lambda i: (lambda hh: 3 * hh + i)
    one_col = 3 * fox_h
    e_fq = _placement(n_cx, fox_h, fox_wd, [(c_piece(i), fox_dh + i, 1.0) for i in range(3)]
                      + [(one_col, fox_dh + 3 + i, 1.0) for i in range(3)])
    e_fk = _placement(n_cx, fox_h, fox_wd, [(one_col, fox_dh + i, 1.0) for i in range(3)]
                      + [(c_piece(i), fox_dh + 3 + i, -1.0) for i in range(3)]
                      + [(one_col, fox_dh + 6 + i, 1.0) for i in range(3)])
    e_fv = _placement(n_cx, fox_h, fox_vwd, [(one_col, fox_dh + i, -1.0) for i in range(3)]
                      + [(one_col, fox_dh + 3, 1.0)])

    mla_scale = (nope + rope) ** -0.5
    mla_dk = nope + rope
    mla_wd = _round_up(mla_dk + 3, LANE_TILE)
    mla_vwd = _round_up(v_dim + 4, LANE_TILE)
    w_kv_a = full["mla_w_kv_a"]
    w_kv_b3 = full["mla_w_kv_b"].reshape(kv_rank, mla_h, nope + v_dim)
    w_kn = _pad_heads(w_kv_b3[:, :, :nope].reshape(kv_rank, -1), mla_h, mla_wd, 1)
    w_mv = _pad_heads(w_kv_b3[:, :, nope:].reshape(kv_rank, -1), mla_h, mla_vwd, 1)
    w_q_a = full["mla_w_q_a"][0]
    w_q_b3 = full["mla_w_q_b"][0].reshape(q_rank, mla_h, nope + rope)
    w_qa_ = _pad_heads(w_q_b3.reshape(q_rank, -1), mla_h, mla_wd, 1)
    w_qb_ = _pad_heads(jnp.concatenate([jnp.zeros_like(w_q_b3[:, :, :nope]), -w_q_b3[:, :, nope + half:],
                                        w_q_b3[:, :, nope:nope + half]], axis=-1).reshape(q_rank, -1),
                       mla_h, mla_wd, 1)
    w_mla_out = _pad_heads(full["mla_w_out"][0], mla_h, mla_vwd, 0)
    w_up, w_down = full["ffn_w_up"], full["ffn_w_down"]
    n_kx = _round_up(rope + 1, LANE_TILE)
    e_mk = _placement(n_kx, mla_h, mla_wd, [(j, nope + j, 1.0) for j in range(rope)]
                      + [(rope, mla_dk + i, 1.0) for i in range(3)])
    e_mv = _placement(n_kx, mla_h, mla_vwd, [(rope, v_dim + i, -1.0) for i in range(3)] + [(rope, v_dim + 3, 1.0)])
    e_kr_u = _placement(n_kx, mla_h, mla_wd, [(j, nope + j, 1.0) for j in range(rope)]).T
    e_kr_v = _placement(n_kx, mla_h, mla_wd, [(j, nope + half + j, 1.0) for j in range(half)]
                        + [(half + j, nope + j, -1.0) for j in range(half)]).T

    inv = 1.0 / (ROPE_BASE ** (jnp.arange(0, rope, 2, dtype=F32) / rope))
    ang = jnp.arange(seq, dtype=F32)[:, None] * inv[None, :]
    cos, sin = jnp.cos(ang), jnp.sin(ang)
    pad_t = jnp.zeros((seq, mla_wd - mla_dk), F32)
    cos_t = jnp.concatenate([jnp.ones((seq, nope), F32), cos, cos, pad_t], axis=1)
    sin_t = jnp.concatenate([jnp.zeros((seq, nope), F32), sin, sin, pad_t], axis=1)
    pad_k = jnp.zeros((seq, n_kx - rope), F32)
    cos_k, sin_k = jnp.concatenate([cos, cos, pad_k], axis=1), jnp.concatenate([sin, sin, pad_k], axis=1)

    (h0,) = _rms_fwd(xs, norm_mix_g[0:1], name="l0_norm_mix")
    gate = _matmul(h0, w_gate, mode="nn", out_dtypes=(F32,), name="fox_gate")
    z = gate + fox_b_f[0][None, :]
    cum = jnp.cumsum(jax.nn.log_sigmoid(z), axis=0)
    cx = jnp.concatenate([_split3(cum).reshape(seq, 3 * fox_h), jnp.ones((seq, 1), BF16),
                          jnp.zeros((seq, n_cx - 3 * fox_h - 1), BF16)], axis=1)
    fqa = _matmul(h0, w_fq, mode="nn", out_dtypes=(BF16,), placed=(cx, e_fq), name="fox_q")
    fka = _matmul(h0, w_fk, mode="nn", out_dtypes=(BF16,), placed=(cx, e_fk), name="fox_k")
    fva = _matmul(h0, w_fv, mode="nn", out_dtypes=(BF16,), placed=(cx, e_fv), name="fox_v")
    foa, fqb = _flash_fwd(fqa, fka, fva, fox_h, fox_dh + 3, fox_dh + 6, FOX_FWD_SUB_ROWS, name="fox_attn")
    add_res = lambda acc, res: (acc + res,)
    x1 = _matmul(foa, w_fox_out, mode="nn", out_dtypes=(F32,), epilogue=add_res, extras=(xs,), name="fox_out")
    (h1,) = _rms_fwd(x1, norm_ffn_g[0:1], name="l0_norm_ffn")
    x2, r0, a0 = _ffn_fwd(x1, h1, w_up[0], w_down[0], "ffn0")

    src, h2 = _rms_fwd(x2, jnp.stack([kv_norm_g, norm_mix_g[1]]), name="l1_norm_kv_mix")
    kv_a = _matmul(src, w_kv_a, mode="nn", out_dtypes=(F32,), name="mla_kv_a")
    (c_kv,) = _rms_fwd(kv_a, mla_kv_a_norm_g[None, :], name="mla_norm_kv_a")
    kr1, kr2 = _rope(kv_a[None, :, kv_rank:kv_rank + half], kv_a[None, :, kv_rank + half:], cos, sin, 1.0,
                     name="mla_rope_k")
    krx = jnp.concatenate([kr1.astype(BF16), kr2.astype(BF16), jnp.ones((seq, 1), BF16),
                           jnp.zeros((seq, n_kx - rope - 1), BF16)], axis=1)
    mka = _matmul(c_kv, w_kn, mode="nn", out_dtypes=(BF16,), placed=(krx, e_mk), name="mla_k")
    mva = _matmul(c_kv, w_mv, mode="nn", out_dtypes=(BF16,), placed=(krx, e_mv), name="mla_v")
    cq_pre = _matmul(h2, w_q_a, mode="nn", out_dtypes=(F32,), name="mla_q_a")
    (c_q,) = _rms_fwd(cq_pre, mla_q_a_norm_g, name="mla_norm_q_a")
    mqa = _rope_proj(c_q, w_qa_, w_qb_, cos_t, sin_t, mla_scale, mla_h, name="mla_q_b_rope")
    moa, mqb = _flash_fwd(mqa, mka, mva, mla_h, v_dim + 3, mla_dk, MLA_FWD_SUB_ROWS, name="mla_attn")
    x3 = _matmul(moa, w_mla_out, mode="nn", out_dtypes=(F32,), epilogue=add_res, extras=(x2,), name="mla_out")
    (h3,) = _rms_fwd(x3, norm_ffn_g[1:2], name="l1_norm_ffn")
    x4, r1, a1 = _ffn_fwd(x3, h3, w_up[1], w_down[1], "ffn1")

    loss_tile, dx4, d_final_g = _loss_head(x4, final_norm_g[None, :], tgt, name="loss_head")
    loss = lax.psum(loss_tile[0, 0], ("x", "y", "c"))

    gw = {}
    dx3, d_up1, d_down1, d_nf1 = _ffn_bwd(dx4, x3, h3, r1, a1, norm_ffn_g[1:2], w_up[1], w_down[1], "ffn1")

    d_moa = _matmul(dx3, w_mla_out, mode="nt", out_dtypes=(BF16,), epilogue=_delta_epilogue(mla_vwd, v_dim),
                    extras=(moa,), name="mla_d_ctx")
    gw["mla_w_out"] = _unpad_heads(_matmul(moa, dx3, mode="tn", out_dtypes=(F32,), name="mla_d_w_out"),
                                   mla_h, v_dim, 0)[None]
    d_mqa, d_mka, d_mva = _flash_bwd(mqb, mka, mva, d_moa, mla_h, MLA_BWD_HEADS_PER_STEP,
                                     name="mla_attn_bwd")
    d_qa_part, d_qb_part = _rope_unmix(d_mqa, cos_t, sin_t, mla_scale, mla_h, name="mla_rope_dq")
    d_w_qa_ = _unpad_heads(_matmul(c_q, d_qa_part, mode="tn", out_dtypes=(F32,), name="mla_d_w_q_b_cos"),
                           mla_h, mla_dk, 1).reshape(q_rank, mla_h, mla_dk)
    d_w_qb_ = _unpad_heads(_matmul(c_q, d_qb_part, mode="tn", out_dtypes=(F32,), name="mla_d_w_q_b_sin"),
                           mla_h, mla_dk, 1).reshape(q_rank, mla_h, mla_dk)
    gw["mla_w_q_b"] = jnp.concatenate(
        [d_w_qa_[:, :, :nope], d_w_qa_[:, :, nope:nope + half] + d_w_qb_[:, :, nope + half:],
         d_w_qa_[:, :, nope + half:] - d_w_qb_[:, :, nope:nope + half]], axis=-1).reshape(1, q_rank, mla_h * mla_dk)
    d_c_q_sin = _matmul(d_qb_part, w_qb_, mode="nt", out_dtypes=(F32,), name="mla_d_c_q_sin")
    d_c_q = _matmul(d_qa_part, w_qa_, mode="nt", out_dtypes=(F32,), epilogue=add_res, extras=(d_c_q_sin,),
                    name="mla_d_c_q")
    d_cq_pre, (d_q_a_g,) = _rms_bwd(cq_pre, [(mla_q_a_norm_g, d_c_q)], None, name="mla_d_norm_q_a")
    gw["mla_w_q_a"] = _matmul(h2, d_cq_pre, mode="tn", out_dtypes=(F32,), name="mla_d_w_q_a")[None]
    d_h2 = _matmul(d_cq_pre, w_q_a, mode="nt", out_dtypes=(F32,), name="mla_d_h")

    d_w_kn = _unpad_heads(_matmul(c_kv, d_mka, mode="tn", out_dtypes=(F32,), name="mla_d_w_k"), mla_h, nope, 1)
    d_w_mv = _unpad_heads(_matmul(c_kv, d_mva, mode="tn", out_dtypes=(F32,), name="mla_d_w_v"), mla_h, v_dim, 1)
    gw["mla_w_kv_b"] = jnp.concatenate([d_w_kn.reshape(kv_rank, mla_h, nope), d_w_mv.reshape(kv_rank, mla_h, v_dim)],
                                       axis=-1).reshape(kv_rank, mla_h * (nope + v_dim))
    d_c_kv_v = _matmul(d_mva, w_mv, mode="nt", out_dtypes=(F32,), name="mla_d_c_kv_v")
    d_c_kv = _matmul(d_mka, w_kn, mode="nt", out_dtypes=(F32,), epilogue=add_res, extras=(d_c_kv_v,),
                     name="mla_d_c_kv")
    d_ckv_pre, (d_kv_a_g,) = _rms_bwd(kv_a, [(mla_kv_a_norm_g[None, :], d_c_kv)], None, name="mla_d_norm_kv_a")
    d_kr_u = _matmul(d_mka, e_kr_u, mode="nn", out_dtypes=(F32,), name="mla_d_k_rope_u")
    d_kr_v = _matmul(d_mka, e_kr_v, mode="nn", out_dtypes=(F32,), name="mla_d_k_rope_v")
    d_kr = _rope_mix(d_kr_u, d_kr_v, cos_k, sin_k, 1.0, 1, name="mla_rope_dk")
    d_kv_a = jnp.concatenate([d_ckv_pre, d_kr[:, :rope].astype(F32)], axis=1)
    gw["mla_w_kv_a"] = _matmul(src, d_kv_a, mode="tn", out_dtypes=(F32,), name="mla_d_w_kv_a")
    d_src = _matmul(d_kv_a, w_kv_a, mode="nt", out_dtypes=(F32,), name="mla_d_src")
    dx2, (d_kv_g, d_nm1) = _rms_bwd(x2, [(kv_norm_g[None, :], d_src), (norm_mix_g[1:2], d_h2)], dx3,
                                    name="l1_d_norm_kv_mix")

    dx1, d_up0, d_down0, d_nf0 = _ffn_bwd(dx2, x1, h1, r0, a0, norm_ffn_g[0:1], w_up[0], w_down[0], "ffn0")
    by_rows = lambda g: g.reshape(N_CHIPS, g.shape[0] // N_CHIPS, g.shape[1])
    gw_by_chip = {"ffn_w_up": jnp.concatenate([d_up0, d_up1], axis=1),
                  "ffn_w_down": jnp.concatenate([by_rows(d_down0), by_rows(d_down1)], axis=1)}

    d_foa = _matmul(dx1, w_fox_out, mode="nt", out_dtypes=(BF16,), epilogue=_delta_epilogue(fox_vwd, fox_dh),
                    extras=(foa,), name="fox_d_ctx")
    gw["fox_w_out"] = _unpad_heads(_matmul(foa, dx1, mode="tn", out_dtypes=(F32,), name="fox_d_w_out"),
                                   fox_h, fox_dh, 0)[None]
    fox_hps = FOX_BWD_HEADS_PER_STEP if fox_h % FOX_BWD_HEADS_PER_STEP == 0 else 1
    d_fqa, d_fka, d_fva, ds_rows, ds_cols = _flash_bwd(fqb, fka, fva, d_foa, fox_h, fox_hps, name="fox_attn_bwd",
                                                       sum_cols=(fox_dh, fox_dh + 3))
    d_cum = jnp.transpose(ds_rows - ds_cols, (1, 0, 2)).reshape(seq, fox_h)
    d_z = lax.cumsum(d_cum, axis=0, reverse=True) * jax.nn.sigmoid(-z)
    d_b_f = jnp.sum(d_z, axis=0)
    d_w_in = [_unpad_heads(_matmul(h0, g, mode="tn", out_dtypes=(F32,), name=f"fox_d_w_{tag}"), fox_h, fox_dh, 1)
              for tag, g in (("q", d_fqa), ("k", d_fka), ("v", d_fva))]
    d_w_gate = _matmul(h0, d_z, mode="tn", out_dtypes=(F32,), name="fox_d_w_gate")
    gw["fox_w_in"] = jnp.concatenate([d_w_in[0] * fox_scale, d_w_in[1], d_w_in[2], d_w_gate], axis=1)[None]
    d_h0 = _matmul(d_z, w_gate, mode="nt", out_dtypes=(F32,), name="fox_d_h_gate")
    for tag, g, w in (("q", d_fqa, w_fq), ("k", d_fka, w_fk), ("v", d_fva, w_fv)):
        d_h0 = _matmul(g, w, mode="nt", out_dtypes=(F32,), epilogue=add_res, extras=(d_h0,), name=f"fox_d_h_{tag}")
    grad_x, (d_nm0,) = _rms_bwd(xs, [(norm_mix_g[0:1], d_h0)], dx1, name="l0_d_norm_mix")

    c_idx = lax.axis_index("c").astype(jnp.int32).reshape(1)
    parts = []
    for (n, ax), shape in zip(_BIG, shard_shapes):
        if n in gw_by_chip:
            g = gw_by_chip[n]
        else:
            g = gw[n]
            g = jnp.moveaxis(g.reshape(g.shape[:ax] + (N_CHIPS, shape[ax]) + g.shape[ax + 1:]), ax, 0)
            g = g.reshape(N_CHIPS, -1, shape[-1])
        parts.append(jnp.pad(g, ((0, 0), (0, _part_rows(shape) - g.shape[1]), (0, PACK_LANES - shape[-1]))))
    parts.append(jnp.zeros((N_CHIPS, rows - sum(p.shape[1] for p in parts), PACK_LANES), F32))
    g4 = jnp.concatenate(parts, axis=1)
    a4 = _sibling_swap_halves(g4, name="grads_to_sibling")
    s4 = _chip_sum(g4, a4, c_idx, name="grads_chip_sum")
    b3 = _chip_exchange(s4, name="grads_between_chips")
    t_mine = _sum_chips(s4, b3, p_chip.astype(jnp.int32).reshape(1), name="grads_sum_chips")
    t_theirs = _sibling_swap(t_mine, name="grads_join_halves")
    is_south = lax.axis_index("c") == 0
    g_big = jnp.concatenate([jnp.where(is_south, t_mine, t_theirs), jnp.where(is_south, t_theirs, t_mine)],
                            axis=0)

    small_local = {"norm_mix_g": jnp.concatenate([d_nm0, d_nm1], axis=0),
                   "norm_ffn_g": jnp.concatenate([d_nf0, d_nf1], axis=0),
                   "fox_b_f": d_b_f[None, :], "kv_norm_g": d_kv_g[0], "mla_kv_a_norm_g": d_kv_a_g[0],
                   "mla_q_a_norm_g": d_q_a_g, "final_norm_g": d_final_g[0]}
    small_shapes = [w_in[n].shape for n in _SMALL]
    small_rows = sum(_part_rows(s, SMALL_PART_ROWS) for s in small_shapes)
    pack_small = lambda arrs: _pack(arrs, small_rows, F32, SMALL_PART_ROWS)
    g_small = _all_reduce_small(pack_small([small_local[n] for n in _SMALL]), name="grads_small")

    grads = dict(zip(big_names, _unpack(g_big, shard_shapes)))
    delta, new_m, new_v = {}, {}, {}
    for n, shape in zip(big_names, shard_shapes):
        flat = lambda a: a.reshape(-1, shape[-1])
        outs = _adamw(flat(w_in[n]), flat(grads[n]), flat(m_in[n]), flat(v_in[n]), name=f"adamw_{n}")
        delta[n], new_m[n], new_v[n] = (o.reshape(shape) for o in outs)
    sm_outs = _adamw(pack_small([w_in[n] for n in _SMALL]), g_small, pack_small([m_in[n] for n in _SMALL]),
                     pack_small([v_in[n] for n in _SMALL]), name="adamw_small")
    for res, packed in zip((grads, delta, new_m, new_v), (g_small,) + tuple(sm_outs)):
        res.update(zip(_SMALL, _unpack(packed, small_shapes, SMALL_PART_ROWS)))

    return (loss, grad_x[None], *[grads[n] for n in _WEIGHTS], *[delta[n] for n in _WEIGHTS],
            *[new_m[n] for n in _WEIGHTS], *[new_v[n] for n in _WEIGHTS])
```

```python
import math

import numpy as np
import jax
import jax.numpy as jnp
from jax import lax
from jax.experimental import pallas as pl
from jax.experimental.pallas import tpu as pltpu

F32 = jnp.float32
BF16 = jnp.bfloat16

FOX_HEADS = 16
MLA_HEADS = 8
QK_NOPE_DIM = 128
ROPE_BASE = 10000.0
EPS = 1e-6

ADAM_LR = 0.001
ADAM_B1 = 0.9
ADAM_B2 = 0.999
ADAM_EPS = 1e-08
ADAM_WD = 0.01
ADAM_STEP = 10

N_CHIPS = 4
N_DEV = 8
PACK_LANES = 1024
PACK_PART_ROWS = 16
SMALL_PART_ROWS = 8
PACK_ROWS_MULT = 1024
VMEM_LIMIT_BYTES = 48 * 1024 * 1024
LANE_TILE = 128
MATMUL_BLOCK = 1024
MATMUL_WIDE_BLOCK = 2048
MATMUL_DEPTH = 2048
ATTN_BLOCK_Q = 1024
ATTN_BLOCK_K = 1024
ATTN_FWD_LANES = 1024
FOX_BWD_HEADS_PER_STEP = 4
MLA_BWD_HEADS_PER_STEP = 2
ATTN_SUB_ROWS = 256
FOX_FWD_SUB_ROWS = (1024, 512)
MLA_FWD_SUB_ROWS = (256, 256)
NEG_BIG = -1e30
MESH = pl.DeviceIdType.MESH


def _round_up(n, m):
    return -(-n // m) * m


def _blk(dim, pref, mult=128):
    if dim <= pref:
        return dim
    b = (pref // mult) * mult
    while b >= mult:
        if dim % b == 0:
            return b
        b -= mult
    return dim


def _params(sem=None):
    return pltpu.CompilerParams(dimension_semantics=sem, vmem_limit_bytes=VMEM_LIMIT_BYTES)


_DIMS = {"nn": (((1,), (0,)), ((), ())), "nt": (((1,), (1,)), ((), ())), "tn": (((0,), (0,)), ((), ()))}


def _matmul(a, b, *, mode, out_dtypes, name, epilogue=None, extras=(), placed=None, by_chip=False):
    if mode == "tn":
        kdim, m = a.shape
    else:
        m, kdim = a.shape
    n = b.shape[0] if mode == "nt" else b.shape[1]
    bm, bk = _blk(m, MATMUL_BLOCK), _blk(kdim, MATMUL_DEPTH)
    bn = _blk(n, MATMUL_WIDE_BLOCK if (mode != "tn" and kdim <= MATMUL_BLOCK) else MATMUL_BLOCK)
    if by_chip:
        bn = _blk(n // N_CHIPS, bn)
    nk = kdim // bk
    n_extra, n_out = len(extras), len(out_dtypes)
    n_placed = 0 if placed is None else 2
    dims = _DIMS[mode]

    def body(a_ref, b_ref, *rest):
        placed_refs = rest[:n_placed]
        rest = rest[n_placed:]
        extra_refs = rest[:n_extra]
        out_refs = rest[n_extra:n_extra + n_out]

        def finish(acc):
            if n_placed:
                acc = acc + lax.dot_general(placed_refs[0][...], placed_refs[1][...], _DIMS["nn"],
                                            preferred_element_type=F32)
            res = (acc,) if epilogue is None else epilogue(acc, *[r[...] for r in extra_refs])
            for o_ref, r in zip(out_refs, res):
                o_ref[...] = r.astype(o_ref.dtype)

        part = lax.dot_general(a_ref[...].astype(BF16), b_ref[...].astype(BF16), dims, preferred_element_type=F32)
        if nk == 1:
            finish(part)
            return
        acc_ref = rest[n_extra + n_out]
        k = pl.program_id(2)

        @pl.when(k == 0)
        def _():
            acc_ref[...] = part

        @pl.when((k > 0) & (k < nk - 1))
        def _():
            acc_ref[...] += part

        @pl.when(k == nk - 1)
        def _():
            finish(acc_ref[...] + part)

    if mode == "tn":
        a_spec = pl.BlockSpec((bk, bm), lambda i, j, k: (k, i))
    else:
        a_spec = pl.BlockSpec((bm, bk), lambda i, j, k: (i, k))
    if mode == "nt":
        b_spec = pl.BlockSpec((bn, bk), lambda i, j, k: (j, k))
    else:
        b_spec = pl.BlockSpec((bk, bn), lambda i, j, k: (k, j))
    tile = pl.BlockSpec((bm, bn), lambda i, j, k: (i, j))
    placed_specs = []
    if n_placed:
        k2 = placed[0].shape[1]
        placed_specs = [pl.BlockSpec((bm, k2), lambda i, j, k: (i, 0)), pl.BlockSpec((k2, bn), lambda i, j, k: (0, j))]
    out_tile, out_dims = tile, (m, n)
    if by_chip:
        per_chip = n // N_CHIPS // bn
        out_tile = pl.BlockSpec((None, bm, bn), lambda i, j, k: (j // per_chip, i, j % per_chip))
        out_dims = (N_CHIPS, m, n // N_CHIPS)
    outs = pl.pallas_call(
        body, name=name,
        grid=(m // bm, n // bn, nk),
        in_specs=[a_spec, b_spec] + placed_specs + [tile] * n_extra,
        out_specs=[out_tile] * n_out,
        out_shape=[jax.ShapeDtypeStruct(out_dims, dt) for dt in out_dtypes],
        scratch_shapes=[pltpu.VMEM((bm, bn), F32)] if nk > 1 else [],
        compiler_params=_params(("parallel", "parallel", "arbitrary")),
    )(a, b, *(placed or ()), *extras)
    return outs[0] if n_out == 1 else outs


def _rms_fwd(x, gains, name):
    s = x.shape[0]
    g, w = gains.shape
    bs = _blk(s, 512, 8)

    def body(x_ref, g_ref, *out_refs):
        xv = x_ref[...]
        y = xv * lax.rsqrt(jnp.mean(xv * xv, axis=-1, keepdims=True) + EPS)
        for i, o_ref in enumerate(out_refs):
            o_ref[...] = (y * g_ref[i:i + 1, :]).astype(o_ref.dtype)

    row = pl.BlockSpec((bs, w), lambda i: (i, 0))
    outs = pl.pallas_call(
        body, name=name, grid=(s // bs,),
        in_specs=[row, pl.BlockSpec((g, w), lambda i: (0, 0))],
        out_specs=[row] * g,
        out_shape=[jax.ShapeDtypeStruct((s, w), BF16)] * g,
        compiler_params=_params(("parallel",)),
    )(x, gains)
    return outs


def _rms_bwd(x, branches, resid, name):
    s = x.shape[0]
    w = branches[0][0].shape[1]
    nb = len(branches)
    bs = _blk(s, 512, 8)
    has_resid = resid is not None

    def body(x_ref, *rest):
        g_refs = rest[:nb]
        dy_refs = rest[nb:2 * nb]
        pos = 2 * nb
        r_ref = rest[pos] if has_resid else None
        pos += int(has_resid)
        dx_ref = rest[pos]
        dg_refs = rest[pos + 1:pos + 1 + nb]
        i = pl.program_id(0)

        @pl.when(i == 0)
        def _():
            for dg_ref in dg_refs:
                dg_ref[...] = jnp.zeros_like(dg_ref)

        xv = x_ref[...]
        rstd = lax.rsqrt(jnp.mean(xv * xv, axis=-1, keepdims=True) + EPS)
        xhat = xv * rstd
        dx = r_ref[...] if has_resid else jnp.zeros_like(xv)
        for g_ref, dy_ref, dg_ref in zip(g_refs, dy_refs, dg_refs):
            dy = dy_ref[...].astype(F32)
            dyg = dy * g_ref[...]
            dx = dx + rstd * (dyg - xhat * jnp.mean(dyg * xhat, axis=-1, keepdims=True))
            dg_ref[...] += jnp.sum(dy * xhat, axis=0, keepdims=True)
        dx_ref[...] = dx

    row = pl.BlockSpec((bs, w), lambda i: (i, 0))
    vec = pl.BlockSpec((1, w), lambda i: (0, 0))
    args = [x] + [g for g, _ in branches] + [dy for _, dy in branches] + ([resid] if has_resid else [])
    outs = pl.pallas_call(
        body, name=name, grid=(s // bs,),
        in_specs=[row] + [vec] * nb + [row] * nb + ([row] if has_resid else []),
        out_specs=[row] + [vec] * nb,
        out_shape=[jax.ShapeDtypeStruct((s, w), F32)] + [jax.ShapeDtypeStruct((1, w), F32)] * nb,
        compiler_params=_params(("arbitrary",)),
    )(*args)
    return outs[0], list(outs[1:])


def _loss_head(x, g, target, name):
    s, w = x.shape
    bs = _blk(s, 512, 8)

    def body(x_ref, g_ref, t_ref, loss_ref, dx_ref, dg_ref):
        i = pl.program_id(0)

        @pl.when(i == 0)
        def _():
            loss_ref[...] = jnp.zeros_like(loss_ref)
            dg_ref[...] = jnp.zeros_like(dg_ref)

        xv = x_ref[...]
        gv = g_ref[...]
        rstd = lax.rsqrt(jnp.mean(xv * xv, axis=-1, keepdims=True) + EPS)
        xhat = xv * rstd
        err = xhat * gv - t_ref[...]
        loss_ref[...] += 0.5 * jnp.sum(jnp.mean(err * err, axis=-1, keepdims=True))
        dy = err * (1.0 / w)
        dyg = dy * gv
        dx_ref[...] = rstd * (dyg - xhat * jnp.mean(dyg * xhat, axis=-1, keepdims=True))
        dg_ref[...] += jnp.sum(dy * xhat, axis=0, keepdims=True)

    row = pl.BlockSpec((bs, w), lambda i: (i, 0))
    vec = pl.BlockSpec((1, w), lambda i: (0, 0))
    return pl.pallas_call(
        body, name=name, grid=(s // bs,),
        in_specs=[row, vec, row],
        out_specs=[pl.BlockSpec((8, 128), lambda i: (0, 0)), row, vec],
        out_shape=[jax.ShapeDtypeStruct((8, 128), F32), jax.ShapeDtypeStruct((s, w), F32),
                   jax.ShapeDtypeStruct((1, w), F32)],
        compiler_params=_params(("arbitrary",)),
    )(x, g, target)


def _rope(a, b, cos, sin, sign, name):
    g, s, w = a.shape
    bs = _blk(s, 1024, 8)

    def body(a_ref, b_ref, c_ref, s_ref, o1_ref, o2_ref):
        av = jnp.sum(a_ref[...].astype(F32), axis=0)
        bv = jnp.sum(b_ref[...].astype(F32), axis=0)
        cv, sv = c_ref[...], s_ref[...] * sign
        o1_ref[...] = av * cv - bv * sv
        o2_ref[...] = bv * cv + av * sv

    grp = pl.BlockSpec((g, bs, w), lambda i: (0, i, 0))
    row = pl.BlockSpec((bs, w), lambda i: (i, 0))
    return pl.pallas_call(
        body, name=name, grid=(s // bs,),
        in_specs=[grp, grp, row, row], out_specs=[row, row],
        out_shape=[jax.ShapeDtypeStruct((s, w), F32)] * 2,
        compiler_params=_params(("parallel",)),
    )(a, b, cos, sin)


def _causal_table(s, bq, bk, q_major):
    nq, nk = s // bq, s // bk
    rows = []
    if q_major:
        for qi in range(nq):
            kmax = (qi * bq + bq - 1) // bk
            for ki in range(kmax + 1):
                rows.append((qi, ki, int(ki * bk + bk - 1 > qi * bq), int(ki == 0), int(ki == kmax)))
    else:
        for ki in range(nk):
            qmin = (ki * bk) // bq
            for qi in range(qmin, nq):
                rows.append((qi, ki, int(ki * bk + bk - 1 > qi * bq), int(qi == qmin), int(qi == nq - 1)))
    return jnp.asarray(np.array(rows, np.int32).T)


def _causal_keep(q0, k0, nq, nk, transposed):
    if transposed:
        kpos = k0 + lax.broadcasted_iota(jnp.int32, (nk, nq), 0)
        qpos = q0 + lax.broadcasted_iota(jnp.int32, (nk, nq), 1)
    else:
        qpos = q0 + lax.broadcasted_iota(jnp.int32, (nq, nk), 0)
        kpos = k0 + lax.broadcasted_iota(jnp.int32, (nq, nk), 1)
    return kpos <= qpos


def _sub_tiles(n_rows, n_cols, masked, square, rows_are_keys, sub_rows):
    sub = min(sub_rows, n_rows)
    out = []
    for r0 in range(0, n_rows, sub):
        if masked and square:
            c0, nc = (r0, n_cols - r0) if rows_are_keys else (0, r0 + sub)
        else:
            c0, nc = 0, n_cols
        out.append((r0, sub, c0, nc))
    return out


_NT = (((1,), (1,)), ((), ()))
_NN = (((1,), (0,)), ((), ()))


def _attn_specs(bq, bk):
    qspec = lambda d: pl.BlockSpec((bq, d), lambda hh, t, tb: (tb[0, t], hh))
    kspec = lambda d: pl.BlockSpec((bk, d), lambda hh, t, tb: (tb[1, t], hh))
    return qspec, kspec


def _split3_cols(x):
    hi = x.astype(BF16).astype(F32)
    rest = x - hi
    mid = rest.astype(BF16).astype(F32)
    lo = (rest - mid).astype(BF16).astype(F32)
    return hi, mid, lo


def _place3(base, col, pieces, sign):
    lane = lax.broadcasted_iota(jnp.int32, base.shape, 1)
    out = base.astype(F32)
    for i, piece in enumerate(pieces):
        out = jnp.where(lane == col + i, sign * piece, out)
    return out.astype(BF16)


def _flash_fwd(qa, ka, va, heads, l_col, lse_col, sub_rows, name):
    s = qa.shape[0]
    da, dv = qa.shape[1] // heads, va.shape[1] // heads
    hps = max(n for n in range(1, ATTN_FWD_LANES // max(da, dv) + 1) if heads % n == 0)
    bq, bk = _blk(s, ATTN_BLOCK_Q), _blk(s, ATTN_BLOCK_K)
    tab = _causal_table(s, bq, bk, True)

    def body(tab_ref, q_ref, k_ref, v_ref, o_ref, qb_ref, m_sc, acc_sc):
        t = pl.program_id(1)
        qi, ki = tab_ref[0, t], tab_ref[1, t]

        @pl.when(tab_ref[3, t] == 1)
        def _():
            m_sc[...] = jnp.full_like(m_sc, NEG_BIG)
            acc_sc[...] = jnp.zeros_like(acc_sc)

        def step(masked):
            for hh in range(hps):
                qc, vc = slice(hh * da, (hh + 1) * da), slice(hh * dv, (hh + 1) * dv)
                for r0, nr, c0, nc in _sub_tiles(bq, bk, masked, bq == bk, False, sub_rows[int(masked)]):
                    sc = lax.dot_general(q_ref[r0:r0 + nr, qc], k_ref[c0:c0 + nc, qc], _NT,
                                         preferred_element_type=F32)
                    if masked:
                        sc = jnp.where(_causal_keep(qi * bq + r0, ki * bk + c0, nr, nc, False), sc, NEG_BIG)
                    m_prev = m_sc[hh, r0:r0 + nr, :]
                    m_new = jnp.maximum(m_prev, jnp.max(sc, axis=-1, keepdims=True))
                    p = jnp.exp(sc - m_new).astype(BF16)
                    acc_sc[r0:r0 + nr, vc] = jnp.exp(m_prev - m_new) * acc_sc[r0:r0 + nr, vc] + lax.dot_general(
                        p, v_ref[c0:c0 + nc, vc], _NN, preferred_element_type=F32)
                    m_sc[hh, r0:r0 + nr, :] = m_new

        @pl.when(tab_ref[2, t] == 1)
        def _():
            step(True)

        @pl.when(tab_ref[2, t] == 0)
        def _():
            step(False)

        @pl.when(tab_ref[4, t] == 1)
        def _():
            for hh in range(hps):
                qc, vc = slice(hh * da, (hh + 1) * da), slice(hh * dv, (hh + 1) * dv)
                acc = acc_sc[:, vc]
                lane = lax.broadcasted_iota(jnp.int32, acc.shape, 1)
                l = jnp.sum(jnp.where(lane == l_col, acc, 0.0), axis=-1, keepdims=True)
                o_ref[:, vc] = (acc / l).astype(o_ref.dtype)
                lse = m_sc[hh] + jnp.log(l)
                qb_ref[:, qc] = _place3(q_ref[:, qc], lse_col, _split3_cols(lse), -1.0)

    qspec, kspec = _attn_specs(bq, bk)
    return pl.pallas_call(
        body, name=name,
        grid_spec=pltpu.PrefetchScalarGridSpec(
            num_scalar_prefetch=1, grid=(heads // hps, tab.shape[1]),
            in_specs=[qspec(hps * da), kspec(hps * da), kspec(hps * dv)],
            out_specs=[qspec(hps * dv), qspec(hps * da)],
            scratch_shapes=[pltpu.VMEM((hps, bq, 1), F32), pltpu.VMEM((bq, hps * dv), F32)]),
        out_shape=[jax.ShapeDtypeStruct((s, heads * dv), BF16), jax.ShapeDtypeStruct((s, heads * da), BF16)],
        compiler_params=_params(("parallel", "arbitrary")),
    )(tab, qa, ka, va)


def _delta_epilogue(dv, delta_col):
    def epilogue(acc, o_tile):
        heads_out = []
        for hh in range(acc.shape[1] // dv):
            vc = slice(hh * dv, (hh + 1) * dv)
            dov = acc[:, vc].astype(BF16)
            delta = jnp.sum(dov.astype(F32) * o_tile[:, vc].astype(F32), axis=-1, keepdims=True)
            heads_out.append(_place3(dov, delta_col, _split3_cols(delta), 1.0))
        return (jnp.concatenate(heads_out, axis=1),)
    return epilogue


_TN =(((0,), (0,)), ((), ()))


def _flash_bwd(qa, ka, va, doa, heads, hps, name, sum_cols=None):
    s = qa.shape[0]
    da, dv = qa.shape[1] // heads, va.shape[1] // heads
    h = heads // hps
    bq, bk = _blk(s, ATTN_BLOCK_Q), _blk(s, ATTN_BLOCK_K)
    tab = _causal_table(s, bq, bk, False)
    n_tiles = tab.shape[1]
    n_sum = 0 if sum_cols is None else 2

    def head_column(acc, col):
        out = jnp.zeros((acc.shape[0], hps), F32)
        lane = lax.broadcasted_iota(jnp.int32, (acc.shape[0], da), 1)
        pick = lax.broadcasted_iota(jnp.int32, out.shape, 1)
        for hh in range(hps):
            val = jnp.sum(jnp.where(lane == col, acc[:, hh * da:(hh + 1) * da], 0.0), axis=-1, keepdims=True)
            out = jnp.where(pick == hh, val, out)
        return out

    def body(tab_ref, q_ref, k_ref, v_ref, do_ref, dq_ref, dk_ref, dv_ref, *rest):
        sum_refs, (dk_sc, dv_sc) = rest[:n_sum], rest[n_sum:]
        t = pl.program_id(1)
        qi, ki = tab_ref[0, t], tab_ref[1, t]

        @pl.when(t == 0)
        def _():
            dq_ref[...] = jnp.zeros_like(dq_ref)

        @pl.when(tab_ref[3, t] == 1)
        def _():
            dk_sc[...] = jnp.zeros_like(dk_sc)
            dv_sc[...] = jnp.zeros_like(dv_sc)

        def step(masked):
            for hh in range(hps):
                qc, vc = slice(hh * da, (hh + 1) * da), slice(hh * dv, (hh + 1) * dv)
                for r0, nr, c0, nc in _sub_tiles(bk, bq, masked, bq == bk, True, ATTN_SUB_ROWS):
                    qv, dov, kv = q_ref[c0:c0 + nc, qc], do_ref[c0:c0 + nc, vc], k_ref[r0:r0 + nr, qc]
                    st = lax.dot_general(kv, qv, _NT, preferred_element_type=F32)
                    if masked:
                        st = jnp.where(_causal_keep(qi * bq + c0, ki * bk + r0, nc, nr, True), st, NEG_BIG)
                    pt = jnp.exp(st)
                    dv_sc[r0:r0 + nr, vc] += lax.dot_general(pt.astype(BF16), dov, _NN, preferred_element_type=F32)
                    dpt = lax.dot_general(v_ref[r0:r0 + nr, vc], dov, _NT, preferred_element_type=F32)
                    dst = (pt * dpt).astype(BF16)
                    dk_sc[r0:r0 + nr, qc] += lax.dot_general(dst, qv, _NN, preferred_element_type=F32)
                    q_rows = pl.ds(pl.multiple_of(qi * bq + c0, ATTN_SUB_ROWS), nc)
                    dq_ref[q_rows, qc] += lax.dot_general(dst, kv, _TN, preferred_element_type=F32)

        @pl.when(tab_ref[2, t] == 1)
        def _():
            step(True)

        @pl.when(tab_ref[2, t] == 0)
        def _():
            step(False)

        @pl.when(tab_ref[4, t] == 1)
        def _():
            dk_ref[...] = dk_sc[...]
            dv_ref[...] = dv_sc[...]
            if n_sum:
                sum_refs[1][...] = head_column(dk_sc[...], sum_cols[1])

        if n_sum:
            @pl.when(t == n_tiles - 1)
            def _():
                sum_refs[0][...] = head_column(dq_ref[...], sum_cols[0])

    qspec, kspec = _attn_specs(bq, bk)
    out_specs = [pl.BlockSpec((s, hps * da), lambda hh, t, tb: (0, hh), pipeline_mode=pl.Buffered(1)),
                 kspec(hps * da), kspec(hps * dv)]
    out_shape = [jax.ShapeDtypeStruct((s, heads * da), F32), jax.ShapeDtypeStruct((s, heads * da), F32),
                 jax.ShapeDtypeStruct((s, heads * dv), F32)]
    if n_sum:
        out_specs += [pl.BlockSpec((None, s, hps), lambda hh, t, tb: (hh, 0, 0), pipeline_mode=pl.Buffered(1)),
                      pl.BlockSpec((None, bk, hps), lambda hh, t, tb: (hh, tb[1, t], 0))]
        out_shape += [jax.ShapeDtypeStruct((h, s, hps), F32)] * 2
    return pl.pallas_call(
        body, name=name,
        grid_spec=pltpu.PrefetchScalarGridSpec(
            num_scalar_prefetch=1, grid=(h, n_tiles),
            in_specs=[qspec(hps * da), kspec(hps * da), kspec(hps * dv), qspec(hps * dv)],
            out_specs=out_specs,
            scratch_shapes=[pltpu.VMEM((bk, hps * da), F32), pltpu.VMEM((bk, hps * dv), F32)]),
        out_shape=out_shape,
        compiler_params=_params(("parallel", "arbitrary")),
    )(tab, qa, ka, va, doa)


def _split3(x):
    hi = lax.reduce_precision(x, 8, 7)
    rest = x - hi
    mid = lax.reduce_precision(rest, 8, 7)
    lo = lax.reduce_precision(rest - mid, 8, 7)
    return jnp.stack([hi, mid, lo], axis=-1).astype(BF16)


def _pad_heads(w, heads, width, axis):
    shape = list(w.shape)
    d = shape[axis] // heads
    w = w.reshape(shape[:axis] + [heads, d] + shape[axis + 1:])
    pad = [(0, 0)] * w.ndim
    pad[axis + 1] = (0, width - d)
    return jnp.pad(w, pad).reshape(shape[:axis] + [heads * width] + shape[axis + 1:])


def _unpad_heads(w, heads, d, axis):
    shape = list(w.shape)
    width = shape[axis] // heads
    w = w.reshape(shape[:axis] + [heads, width] + shape[axis + 1:])
    w = lax.slice_in_dim(w, 0, d, axis=axis + 1)
    return w.reshape(shape[:axis] + [heads * d] + shape[axis + 1:])


def _placement(rows, heads, width, entries):
    e = np.zeros((rows, heads * width), np.float32)
    for row, col, val in entries:
        for hh in range(heads):
            e[row(hh) if callable(row) else row, hh * width + col] = val
    return jnp.asarray(e, BF16)


def _rope_mix(a, b, cos_t, sin_t, scale, heads, name):
    s = a.shape[0]
    d = a.shape[1] // heads
    bs = _blk(s, 1024, 8)

    def body(a_ref, b_ref, c_ref, s_ref, o_ref):
        o_ref[...] = ((a_ref[...] * c_ref[...] + b_ref[...] * s_ref[...]) * scale).astype(o_ref.dtype)

    blk = pl.BlockSpec((bs, d), lambda i, hh: (i, hh))
    tbl = pl.BlockSpec((bs, d), lambda i, hh: (i, 0))
    return pl.pallas_call(
        body, name=name, grid=(s // bs, heads), in_specs=[blk, blk, tbl, tbl], out_specs=blk,
        out_shape=jax.ShapeDtypeStruct(a.shape, BF16),
        compiler_params=_params(("parallel", "parallel")),
    )(a, b, cos_t, sin_t)


def _rope_proj(x, w_a, w_b, cos_t, sin_t, scale, heads, name):
    s, kdim = x.shape
    d = w_a.shape[1] // heads
    hpt = max(n for n in range(1, max(1, MATMUL_BLOCK // d) + 1) if heads % n == 0)
    bm = _blk(s, MATMUL_BLOCK)
    cos_w, sin_w = jnp.tile(cos_t, (1, hpt)), jnp.tile(sin_t, (1, hpt))

    def body(x_ref, wa_ref, wb_ref, c_ref, s_ref, o_ref):
        xv = x_ref[...]
        a = lax.dot_general(xv, wa_ref[...], _NN, preferred_element_type=F32)
        b = lax.dot_general(xv, wb_ref[...], _NN, preferred_element_type=F32)
        o_ref[...] = ((a * c_ref[...] + b * s_ref[...]) * scale).astype(o_ref.dtype)

    wide = pl.BlockSpec((bm, hpt * d), lambda i, j: (i, j))
    tbl = pl.BlockSpec((bm, hpt * d), lambda i, j: (i, 0))
    wgt = pl.BlockSpec((kdim, hpt * d), lambda i, j: (0, j))
    return pl.pallas_call(
        body, name=name, grid=(s // bm, heads // hpt),
        in_specs=[pl.BlockSpec((bm, kdim), lambda i, j: (i, 0)), wgt, wgt, tbl, tbl], out_specs=wide,
        out_shape=jax.ShapeDtypeStruct((s, heads * d), BF16),
        compiler_params=_params(("parallel", "parallel")),
    )(x, w_a, w_b, cos_w, sin_w)


def _rope_proj_bwd(x, g, w_a, w_b, cos_t, sin_t, scale, heads, name):
    s, kdim = x.shape
    d = w_a.shape[1] // heads
    hpt = max(n for n in range(1, max(1, MATMUL_BLOCK // d) + 1) if heads % n == 0)
    bs = _blk(s, MATMUL_BLOCK)
    n_tiles, n_rows = heads // hpt, s // bs
    cos_w, sin_w = jnp.tile(cos_t, (1, hpt)), jnp.tile(sin_t, (1, hpt))

    def halves(g_ref, c_ref, s_ref):
        gv = g_ref[...] * scale
        return (gv * c_ref[...]).astype(BF16), (gv * s_ref[...]).astype(BF16)

    def dx_body(g_ref, wa_ref, wb_ref, c_ref, s_ref, dx_ref, acc_ref):
        j = pl.program_id(1)
        ga, gb = halves(g_ref, c_ref, s_ref)
        part = (lax.dot_general(ga, wa_ref[...], _NT, preferred_element_type=F32)
                + lax.dot_general(gb, wb_ref[...], _NT, preferred_element_type=F32))

        @pl.when(j == 0)
        def _():
            acc_ref[...] = part

        @pl.when(j > 0)
        def _():
            acc_ref[...] += part

        @pl.when(j == n_tiles - 1)
        def _():
            dx_ref[...] = acc_ref[...]

    def dw_body(x_ref, g_ref, c_ref, s_ref, dwa_ref, dwb_ref, acc_a, acc_b):
        i = pl.program_id(1)
        ga, gb = halves(g_ref, c_ref, s_ref)
        xv = x_ref[...]
        pa = lax.dot_general(xv, ga, _TN, preferred_element_type=F32)
        pb = lax.dot_general(xv, gb, _TN, preferred_element_type=F32)

        @pl.when(i == 0)
        def _():
            acc_a[...] = pa
            acc_b[...] = pb

        @pl.when(i > 0)
        def _():
            acc_a[...] += pa
            acc_b[...] += pb

        @pl.when(i == n_rows - 1)
        def _():
            dwa_ref[...] = acc_a[...]
            dwb_ref[...] = acc_b[...]

    dx = pl.pallas_call(
        dx_body, name=name + "_dx", grid=(n_rows, n_tiles),
        in_specs=[pl.BlockSpec((bs, hpt * d), lambda i, j: (i, j)),
                  pl.BlockSpec((kdim, hpt * d), lambda i, j: (0, j)), pl.BlockSpec((kdim, hpt * d), lambda i, j: (0, j)),
                  pl.BlockSpec((bs, hpt * d), lambda i, j: (i, 0)), pl.BlockSpec((bs, hpt * d), lambda i, j: (i, 0))],
        out_specs=pl.BlockSpec((bs, kdim), lambda i, j: (i, 0)),
        out_shape=jax.ShapeDtypeStruct((s, kdim), F32),
        scratch_shapes=[pltpu.VMEM((bs, kdim), F32)],
        compiler_params=_params(("parallel", "arbitrary")),
    )(g, w_a, w_b, cos_w, sin_w)
    dwa, dwb = pl.pallas_call(
        dw_body, name=name + "_dw", grid=(n_tiles, n_rows),
        in_specs=[pl.BlockSpec((bs, kdim), lambda j, i: (i, 0)),
                  pl.BlockSpec((bs, hpt * d), lambda j, i: (i, j)),
                  pl.BlockSpec((bs, hpt * d), lambda j, i: (i, 0)), pl.BlockSpec((bs, hpt * d), lambda j, i: (i, 0))],
        out_specs=[pl.BlockSpec((kdim, hpt * d), lambda j, i: (0, j))] * 2,
        out_shape=[jax.ShapeDtypeStruct((kdim, heads * d), F32)] * 2,
        scratch_shapes=[pltpu.VMEM((kdim, hpt * d), F32)] * 2,
        compiler_params=_params(("parallel", "arbitrary")),
    )(x, g, cos_w, sin_w)
    return dx, dwa, dwb


def _adamw(w, g, m, v, name):
    r, wd = w.shape
    br = _blk(r, 512, 8)

    def body(w_ref, g_ref, m_ref, v_ref, d_ref, nm_ref, nv_ref):
        gv = g_ref[...]
        mn = ADAM_B1 * m_ref[...] + (1.0 - ADAM_B1) * gv
        vn = ADAM_B2 * v_ref[...] + (1.0 - ADAM_B2) * (gv * gv)
        m_hat = mn / (1.0 - ADAM_B1 ** ADAM_STEP)
        v_hat = vn / (1.0 - ADAM_B2 ** ADAM_STEP)
        d_ref[...] = -ADAM_LR * (m_hat / (jnp.sqrt(v_hat) + ADAM_EPS) + ADAM_WD * w_ref[...])
        nm_ref[...] = mn
        nv_ref[...] = vn

    row = pl.BlockSpec((br, wd), lambda i: (i, 0))
    return pl.pallas_call(
        body, name=name, grid=(r // br,), in_specs=[row] * 4, out_specs=[row] * 3,
        out_shape=[jax.ShapeDtypeStruct((r, wd), F32)] * 3,
        compiler_params=_params(("parallel",)),
    )(w, g, m, v)


_ANY = pl.BlockSpec(memory_space=pl.ANY)


def _place():
    x, y, c = lax.axis_index("x"), lax.axis_index("y"), lax.axis_index("c")
    chips = [(x, 1 - y), (1 - x, y), (1 - x, 1 - y)]
    return x, y, c, chips


def _all_gather_shards(shard, name):
    r, w = shard.shape
    hr = r // 2
    qr = hr // 2

    def body(x_ref, out_ref, send_sems, recv_sems):
        x, y, c, _ = _place()
        me, sibling, y_nbr, x_nbr = (x, y, c), (x, y, 1 - c), (x, 1 - y, c), (1 - x, y, c)

        def rows(j, half, piece=None):
            if piece is None:
                return out_ref.at[j, pl.ds(pl.multiple_of(half * hr, 16), hr), :]
            return out_ref.at[j, pl.ds(pl.multiple_of(half * hr + piece * qr, 16), qr), :]

        def mine(piece):
            return x_ref.at[pl.ds(pl.multiple_of(c * hr + piece * qr, 16), qr), :]

        def copy(sem, src, dst, to):
            return pltpu.make_async_remote_copy(src_ref=src, dst_ref=dst, send_sem=send_sems.at[sem],
                                                recv_sem=recv_sems.at[sem], device_id=to, device_id_type=MESH)

        sent = [copy(0, mine(0), rows(0, c, 0), y_nbr), copy(1, mine(1), rows(0, c, 1), y_nbr),
                copy(2, mine(0), rows(1, c, 0), x_nbr), copy(3, mine(1), rows(1, c, 1), x_nbr)]
        for cp in sent:
            cp.start()

        def landed(sem, ref):
            copy(sem, ref, ref, me).wait_recv()

        def pass_on(sem, src, dst, to):
            cp = copy(sem, src, dst, to)
            cp.start()
            sent.append(cp)

        landed(2, rows(1, c, 0))
        pass_on(4, rows(1, c, 0), rows(2, c, 0), y_nbr)
        landed(1, rows(0, c, 1))
        pass_on(5, rows(0, c, 1), rows(2, c, 1), x_nbr)
        landed(0, rows(0, c, 0))
        pass_on(6, rows(0, c), rows(0, c), sibling)
        landed(3, rows(1, c, 1))
        pass_on(7, rows(1, c), rows(1, c), sibling)
        landed(4, rows(2, c, 0))
        landed(5, rows(2, c, 1))
        pass_on(8, rows(2, c), rows(2, c), sibling)
        for j in range(3):
            landed(6 + j, rows(j, 1 - c))
        for cp in sent:
            cp.wait_send()

    return pl.pallas_call(
        body, name=name, in_specs=[_ANY], out_specs=_ANY,
        out_shape=jax.ShapeDtypeStruct((N_CHIPS - 1, r, w), shard.dtype),
        scratch_shapes=[pltpu.SemaphoreType.DMA((9,)), pltpu.SemaphoreType.DMA((9,))],
        compiler_params=pltpu.CompilerParams(vmem_limit_bytes=VMEM_LIMIT_BYTES),
    )(shard)


def _sibling_swap_halves(g, name):
    nq, r, w = g.shape
    hr = r // 2

    def body(g_ref, a_ref, send_sems, recv_sems):
        x, y, c, _ = _place()
        sibling = (x, y, 1 - c)
        cps = []
        for q in range(nq):
            cp = pltpu.make_async_remote_copy(
                src_ref=g_ref.at[q, pl.ds(pl.multiple_of((1 - c) * hr, 8), hr), :], dst_ref=a_ref.at[q],
                send_sem=send_sems.at[q], recv_sem=recv_sems.at[q], device_id=sibling, device_id_type=MESH)
            cp.start()
            cps.append(cp)
        for cp in cps:
            cp.wait()

    return pl.pallas_call(
        body, name=name, in_specs=[_ANY], out_specs=_ANY,
        out_shape=jax.ShapeDtypeStruct((nq, hr, w), g.dtype),
        scratch_shapes=[pltpu.SemaphoreType.DMA((nq,)), pltpu.SemaphoreType.DMA((nq,))],
        compiler_params=pltpu.CompilerParams(vmem_limit_bytes=VMEM_LIMIT_BYTES),
    )(g)


def _chip_sum(g, a, c_idx, name):
    nq, r, w = g.shape
    hr = r // 2
    br = _blk(hr, 512, 16)
    nb = hr // br

    def body(c_ref, g_ref, a_ref, o_ref):
        o_ref[...] = (g_ref[...] + a_ref[...]).astype(o_ref.dtype)

    return pl.pallas_call(
        body, name=name,
        grid_spec=pltpu.PrefetchScalarGridSpec(
            num_scalar_prefetch=1, grid=(nq, nb),
            in_specs=[pl.BlockSpec((None, br, w), lambda q, i, cr: (q, cr[0] * nb + i, 0)),
                      pl.BlockSpec((None, br, w), lambda q, i, cr: (q, i, 0))],
            out_specs=pl.BlockSpec((None, br, w), lambda q, i, cr: (q, i, 0))),
        out_shape=jax.ShapeDtypeStruct((nq, hr, w), BF16),
        compiler_params=_params(("parallel", "parallel")),
    )(c_idx, g, a)


def _chip_exchange(s4, name):
    nq, hr, w = s4.shape

    def body(s_ref, b_ref, send_sems, recv_sems):
        x, y, c, chips = _place()
        cps = []
        for j, (cx, cy) in enumerate(chips):
            cp = pltpu.make_async_remote_copy(
                src_ref=s_ref.at[2 * cx + cy], dst_ref=b_ref.at[j],
                send_sem=send_sems.at[j], recv_sem=recv_sems.at[j], device_id=(cx, cy, c), device_id_type=MESH)
            cp.start()
            cps.append(cp)
        for cp in cps:
            cp.wait()

    return pl.pallas_call(
        body, name=name, in_specs=[_ANY], out_specs=_ANY,
        out_shape=jax.ShapeDtypeStruct((nq - 1, hr, w), s4.dtype),
        scratch_shapes=[pltpu.SemaphoreType.DMA((3,)), pltpu.SemaphoreType.DMA((3,))],
        compiler_params=pltpu.CompilerParams(vmem_limit_bytes=VMEM_LIMIT_BYTES),
    )(s4)


def _sum_chips(s4, b3, p_idx, name):
    _, hr, w = s4.shape
    nb3 = b3.shape[0]
    br = _blk(hr, 512, 16)

    def body(p_ref, s_ref, b_ref, o_ref):
        acc = s_ref[...].astype(F32)
        for j in range(nb3):
            acc = acc + b_ref[j].astype(F32)
        o_ref[...] = acc

    return pl.pallas_call(
        body, name=name,
        grid_spec=pltpu.PrefetchScalarGridSpec(
            num_scalar_prefetch=1, grid=(hr // br,),
            in_specs=[pl.BlockSpec((None, br, w), lambda i, pr: (pr[0], i, 0)),
                      pl.BlockSpec((nb3, br, w), lambda i, pr: (0, i, 0))],
            out_specs=pl.BlockSpec((br, w), lambda i, pr: (i, 0))),
        out_shape=jax.ShapeDtypeStruct((hr, w), F32),
        compiler_params=_params(("parallel",)),
    )(p_idx, s4, b3)


def _sibling_swap(t, name):
    hr, w = t.shape

    def body(t_ref, o_ref, send_sem, recv_sem):
        x, y, c, _ = _place()
        cp = pltpu.make_async_remote_copy(src_ref=t_ref, dst_ref=o_ref, send_sem=send_sem, recv_sem=recv_sem,
                                          device_id=(x, y, 1 - c), device_id_type=MESH)
        cp.start()
        cp.wait()

    return pl.pallas_call(
        body, name=name, in_specs=[_ANY], out_specs=_ANY,
        out_shape=jax.ShapeDtypeStruct((hr, w), t.dtype),
        scratch_shapes=[pltpu.SemaphoreType.DMA, pltpu.SemaphoreType.DMA],
        compiler_params=pltpu.CompilerParams(vmem_limit_bytes=VMEM_LIMIT_BYTES),
    )(t)


def _all_reduce_small(v, name):
    r, w = v.shape

    def body(v_ref, o_ref, slots, send_sems, recv_sems):
        x, y, c, _ = _place()
        me = 4 * x + 2 * y + c
        slots[me] = v_ref[...]
        cps = []
        for k in range(1, N_DEV):
            fx, fy, fc = (k >> 2) & 1, (k >> 1) & 1, k & 1
            to = (x ^ fx, y ^ fy, c ^ fc)
            cp = pltpu.make_async_remote_copy(
                src_ref=v_ref, dst_ref=slots.at[me], send_sem=send_sems.at[k - 1], recv_sem=recv_sems.at[k - 1],
                device_id=to, device_id_type=MESH)
            cp.start()
            cps.append(cp)
        for k in range(1, N_DEV):
            fx, fy, fc = (k >> 2) & 1, (k >> 1) & 1, k & 1
            src_dev = 4 * (x ^ fx) + 2 * (y ^ fy) + (c ^ fc)
            pltpu.make_async_remote_copy(
                src_ref=v_ref, dst_ref=slots.at[src_dev], send_sem=send_sems.at[k - 1],
                recv_sem=recv_sems.at[k - 1], device_id=(x, y, c), device_id_type=MESH).wait_recv()
        for cp in cps:
            cp.wait_send()
        acc = slots[0]
        for d in range(1, N_DEV):
            acc = acc + slots[d]
        o_ref[...] = acc

    return pl.pallas_call(
        body, name=name,
        in_specs=[pl.BlockSpec(memory_space=pltpu.VMEM)], out_specs=pl.BlockSpec(memory_space=pltpu.VMEM),
        out_shape=jax.ShapeDtypeStruct((r, w), F32),
        scratch_shapes=[pltpu.VMEM((N_DEV, r, w), F32), pltpu.SemaphoreType.DMA((N_DEV - 1,)),
                        pltpu.SemaphoreType.DMA((N_DEV - 1,))],
        compiler_params=pltpu.CompilerParams(vmem_limit_bytes=VMEM_LIMIT_BYTES),
    )(v)


def _part_rows(shape, part_rows=PACK_PART_ROWS):
    assert shape[-1] <= PACK_LANES
    return _round_up(math.prod(shape[:-1]), part_rows)


def _packed_rows(shapes):
    return _round_up(sum(_part_rows(s) for s in shapes), PACK_ROWS_MULT)


def _pack(arrs, total_rows, dtype, part_rows=PACK_PART_ROWS):
    parts = []
    for a in arrs:
        a2 = a.reshape(-1, a.shape[-1]).astype(dtype)
        rows = _part_rows(a.shape, part_rows)
        parts.append(jnp.pad(a2, ((0, rows - a2.shape[0]), (0, PACK_LANES - a2.shape[1]))))
    used = sum(p.shape[0] for p in parts)
    if total_rows > used:
        parts.append(jnp.zeros((total_rows - used, PACK_LANES), dtype))
    return jnp.concatenate(parts, axis=0)


def _unpack(packed, shapes, part_rows=PACK_PART_ROWS):
    out, r0 = [], 0
    for s in shapes:
        out.append(packed[r0:r0 + math.prod(s[:-1]), :s[-1]].reshape(s))
        r0 += _part_rows(s, part_rows)
    return out


_BIG = (("fox_w_in", 2), ("fox_w_out", 1), ("mla_w_kv_a", 0), ("mla_w_kv_b", 1), ("mla_w_q_a", 1),
        ("mla_w_q_b", 2), ("mla_w_out", 1), ("ffn_w_up", 2), ("ffn_w_down", 1))
_SMALL = ("norm_mix_g", "norm_ffn_g", "fox_b_f", "kv_norm_g", "mla_kv_a_norm_g", "mla_q_a_norm_g", "final_norm_g")
_WEIGHTS = ("norm_mix_g", "norm_ffn_g", "fox_w_in", "fox_b_f", "fox_w_out", "kv_norm_g", "mla_w_kv_a",
            "mla_kv_a_norm_g", "mla_w_kv_b", "mla_w_q_a", "mla_q_a_norm_g", "mla_w_q_b", "mla_w_out",
            "ffn_w_up", "ffn_w_down", "final_norm_g")


def _ffn_fwd(x, h, w_up, w_down, tag):
    def relu_sq(acc):
        r = jnp.maximum(acc, 0.0)
        return r, r * r

    r, a = _matmul(h, w_up, mode="nn", out_dtypes=(BF16, BF16), epilogue=relu_sq, name=f"{tag}_up")
    x_out = _matmul(a, w_down, mode="nn", out_dtypes=(F32,), epilogue=lambda acc, res: (acc + res,),
                    extras=(x,), name=f"{tag}_down")
    return x_out, r, a


def _ffn_bwd(dx_out, x_in, h, r, a, g_norm, w_up, w_down, tag):
    d_u = _matmul(dx_out, w_down, mode="nt", out_dtypes=(BF16,), epilogue=lambda acc, rr: (acc * (2.0 * rr.astype(F32)),),
                  extras=(r,), name=f"{tag}_d_act")
    d_w_down = _matmul(a, dx_out, mode="tn", out_dtypes=(F32,), name=f"{tag}_d_w_down")
    d_w_up = _matmul(h, d_u, mode="tn", out_dtypes=(F32,), by_chip=True, name=f"{tag}_d_w_up")
    d_h = _matmul(d_u, w_up, mode="nt", out_dtypes=(F32,), name=f"{tag}_d_h")
    dx_in, (d_g,) = _rms_bwd(x_in, [(g_norm, d_h)], dx_out, name=f"{tag}_d_norm")
    return dx_in, d_w_up, d_w_down, d_g


def kernel(x, norm_mix_g, norm_ffn_g, fox_w_in, fox_b_f, fox_w_out, kv_norm_g, mla_w_kv_a, mla_kv_a_norm_g, mla_w_kv_b, mla_w_q_a, mla_q_a_norm_g, mla_w_q_b, mla_w_out, ffn_w_up, ffn_w_down, final_norm_g, loss_target, m_norm_mix_g, m_norm_ffn_g, m_fox_w_in, m_fox_b_f, m_fox_w_out, m_kv_norm_g, m_mla_w_kv_a, m_mla_kv_a_norm_g, m_mla_w_kv_b, m_mla_w_q_a, m_mla_q_a_norm_g, m_mla_w_q_b, m_mla_w_out, m_ffn_w_up, m_ffn_w_down, m_final_norm_g, v_norm_mix_g, v_norm_ffn_g, v_fox_w_in, v_fox_b_f, v_fox_w_out, v_kv_norm_g, v_mla_w_kv_a, v_mla_kv_a_norm_g, v_mla_w_kv_b, v_mla_w_q_a, v_mla_q_a_norm_g, v_mla_w_q_b, v_mla_w_out, v_ffn_w_up, v_ffn_w_down, v_final_norm_g):
    args = dict(locals())
    w_in = {n: args[n] for n in _WEIGHTS}
    m_in = {n: args["m_" + n] for n in _WEIGHTS}
    v_in = {n: args["v_" + n] for n in _WEIGHTS}

    xs = x[0]
    seq, d_model = xs.shape
    tgt = loss_target[0]
    fox_h, mla_h, nope = FOX_HEADS, MLA_HEADS, QK_NOPE_DIM
    kv_rank = mla_kv_a_norm_g.shape[0]
    rope = mla_w_kv_a.shape[1] - kv_rank
    half = rope // 2
    q_rank = mla_q_a_norm_g.shape[1]
    v_dim = mla_w_kv_b.shape[1] * N_CHIPS // mla_h - nope
    fox_w = fox_w_out.shape[1] * N_CHIPS
    fox_dh = fox_w // fox_h

    big_names = [n for n, _ in _BIG]
    shard_shapes = [w_in[n].shape for n in big_names]
    rows = _packed_rows(shard_shapes)
    my_shard = _pack([w_in[n] for n in big_names], rows, BF16)
    others = _all_gather_shards(my_shard, name="gather_weights")
    by_relation = jnp.concatenate([my_shard[None], others], axis=0)
    p_chip = 2 * lax.axis_index("x") + lax.axis_index("y")
    full = {}
    for q in range(N_CHIPS):
        shard_q = lax.dynamic_index_in_dim(by_relation, p_chip ^ q, axis=0, keepdims=False)
        for (n, ax), piece in zip(_BIG, _unpack(shard_q, shard_shapes)):
            full.setdefault(n, []).append(piece)
    full = {n: jnp.concatenate(full[n], axis=ax) for n, ax in _BIG}

    fox_scale = fox_dh ** -0.5
    fox_wd = _round_up(fox_dh + 9, LANE_TILE)
    fox_vwd = _round_up(fox_dh + 4, LANE_TILE)
    w_fox_in = full["fox_w_in"][0]
    w_fq = _pad_heads(w_fox_in[:, :fox_w] * fox_scale, fox_h, fox_wd, 1)
    w_fk = _pad_heads(w_fox_in[:, fox_w:2 * fox_w], fox_h, fox_wd, 1)
    w_fv = _pad_heads(w_fox_in[:, 2 * fox_w:3 * fox_w], fox_h, fox_vwd, 1)
    w_gate = w_fox_in[:, 3 * fox_w:]
    w_fox_out = _pad_heads(full["fox_w_out"][0], fox_h, fox_vwd, 0)
    n_cx = _round_up(3 * fox_h + 1, LANE_TILE)
    c_piece = lambda i: (lambda hh: 3 * hh + i)
    one_col = 3 * fox_h
    e_fq = _placement(n_cx, fox_h, fox_wd, [(c_piece(i), fox_dh + i, 1.0) for i in range(3)]
                      + [(one_col, fox_dh + 3 + i, 1.0) for i in range(3)])
    e_fk = _placement(n_cx, fox_h, fox_wd, [(one_col, fox_dh + i, 1.0) for i in range(3)]
                      + [(c_piece(i), fox_dh + 3 + i, -1.0) for i in range(3)]
                      + [(one_col, fox_dh + 6 + i, 1.0) for i in range(3)])
    e_fv = _placement(n_cx, fox_h, fox_vwd, [(one_col, fox_dh + i, -1.0) for i in range(3)]
                      + [(one_col, fox_dh + 3, 1.0)])

    mla_scale = (nope + rope) ** -0.5
    mla_dk = nope + rope
    mla_wd = _round_up(mla_dk + 3, LANE_TILE)
    mla_vwd = _round_up(v_dim + 4, LANE_TILE)
    w_kv_a = full["mla_w_kv_a"]
    w_kv_b3 = full["mla_w_kv_b"].reshape(kv_rank, mla_h, nope + v_dim)
    w_kn = _pad_heads(w_kv_b3[:, :, :nope].reshape(kv_rank, -1), mla_h, mla_wd, 1)
    w_mv = _pad_heads(w_kv_b3[:, :, nope:].reshape(kv_rank, -1), mla_h, mla_vwd, 1)
    w_q_a = full["mla_w_q_a"][0]
    w_q_b3 = full["mla_w_q_b"][0].reshape(q_rank, mla_h, nope + rope)
    w_qa_ = _pad_heads(w_q_b3.reshape(q_rank, -1), mla_h, mla_wd, 1)
    w_qb_ = _pad_heads(jnp.concatenate([jnp.zeros_like(w_q_b3[:, :, :nope]), -w_q_b3[:, :, nope + half:],
                                        w_q_b3[:, :, nope:nope + half]], axis=-1).reshape(q_rank, -1),
                       mla_h, mla_wd, 1)
    w_mla_out = _pad_heads(full["mla_w_out"][0], mla_h, mla_vwd, 0)
    w_up, w_down = full["ffn_w_up"], full["ffn_w_down"]
    n_kx = _round_up(rope + 1, LANE_TILE)
    e_mk = _placement(n_kx, mla_h, mla_wd, [(j, nope + j, 1.0) for j in range(rope)]
                      + [(rope, mla_dk + i, 1.0) for i in range(3)])
    e_mv = _placement(n_kx, mla_h, mla_vwd, [(rope, v_dim + i, -1.0) for i in range(3)] + [(rope, v_dim + 3, 1.0)])
    e_kr_u = _placement(n_kx, mla_h, mla_wd, [(j, nope + j, 1.0) for j in range(rope)]).T
    e_kr_v = _placement(n_kx, mla_h, mla_wd, [(j, nope + half + j, 1.0) for j in range(half)]
                        + [(half + j, nope + j, -1.0) for j in range(half)]).T

    inv = 1.0 / (ROPE_BASE ** (jnp.arange(0, rope, 2, dtype=F32) / rope))
    ang = jnp.arange(seq, dtype=F32)[:, None] * inv[None, :]
    cos, sin = jnp.cos(ang), jnp.sin(ang)
    pad_t = jnp.zeros((seq, mla_wd - mla_dk), F32)
    cos_t = jnp.concatenate([jnp.ones((seq, nope), F32), cos, cos, pad_t], axis=1)
    sin_t = jnp.concatenate([jnp.zeros((seq, nope), F32), sin, sin, pad_t], axis=1)
    pad_k = jnp.zeros((seq, n_kx - rope), F32)
    cos_k, sin_k = jnp.concatenate([cos, cos, pad_k], axis=1), jnp.concatenate([sin, sin, pad_k], axis=1)

    (h0,) = _rms_fwd(xs, norm_mix_g[0:1], name="l0_norm_mix")
    gate = _matmul(h0, w_gate, mode="nn", out_dtypes=(F32,), name="fox_gate")
    z = gate + fox_b_f[0][None, :]
    cum = jnp.cumsum(jax.nn.log_sigmoid(z), axis=0)
    cx = jnp.concatenate([_split3(cum).reshape(seq, 3 * fox_h), jnp.ones((seq, 1), BF16),
                          jnp.zeros((seq, n_cx - 3 * fox_h - 1), BF16)], axis=1)
    fqa = _matmul(h0, w_fq, mode="nn", out_dtypes=(BF16,), placed=(cx, e_fq), name="fox_q")
    fka = _matmul(h0, w_fk, mode="nn", out_dtypes=(BF16,), placed=(cx, e_fk), name="fox_k")
    fva = _matmul(h0, w_fv, mode="nn", out_dtypes=(BF16,), placed=(cx, e_fv), name="fox_v")
    foa, fqb = _flash_fwd(fqa, fka, fva, fox_h, fox_dh + 3, fox_dh + 6, FOX_FWD_SUB_ROWS, name="fox_attn")
    add_res = lambda acc, res: (acc + res,)
    x1 = _matmul(foa, w_fox_out, mode="nn", out_dtypes=(F32,), epilogue=add_res, extras=(xs,), name="fox_out")
    (h1,) = _rms_fwd(x1, norm_ffn_g[0:1], name="l0_norm_ffn")
    x2, r0, a0 = _ffn_fwd(x1, h1, w_up[0], w_down[0], "ffn0")

    src, h2 = _rms_fwd(x2, jnp.stack([kv_norm_g, norm_mix_g[1]]), name="l1_norm_kv_mix")
    kv_a = _matmul(src, w_kv_a, mode="nn", out_dtypes=(F32,), name="mla_kv_a")
    (c_kv,) = _rms_fwd(kv_a, mla_kv_a_norm_g[None, :], name="mla_norm_kv_a")
    kr1, kr2 = _rope(kv_a[None, :, kv_rank:kv_rank + half], kv_a[None, :, kv_rank + half:], cos, sin, 1.0,
                     name="mla_rope_k")
    krx = jnp.concatenate([kr1.astype(BF16), kr2.astype(BF16), jnp.ones((seq, 1), BF16),
                           jnp.zeros((seq, n_kx - rope - 1), BF16)], axis=1)
    mka = _matmul(c_kv, w_kn, mode="nn", out_dtypes=(BF16,), placed=(krx, e_mk), name="mla_k")
    mva = _matmul(c_kv, w_mv, mode="nn", out_dtypes=(BF16,), placed=(krx, e_mv), name="mla_v")
    cq_pre = _matmul(h2, w_q_a, mode="nn", out_dtypes=(F32,), name="mla_q_a")
    (c_q,) = _rms_fwd(cq_pre, mla_q_a_norm_g, name="mla_norm_q_a")
    mqa = _rope_proj(c_q, w_qa_, w_qb_, cos_t, sin_t, mla_scale, mla_h, name="mla_q_b_rope")
    moa, mqb = _flash_fwd(mqa, mka, mva, mla_h, v_dim + 3, mla_dk, MLA_FWD_SUB_ROWS, name="mla_attn")
    x3 = _matmul(moa, w_mla_out, mode="nn", out_dtypes=(F32,), epilogue=add_res, extras=(x2,), name="mla_out")
    (h3,) = _rms_fwd(x3, norm_ffn_g[1:2], name="l1_norm_ffn")
    x4, r1, a1 = _ffn_fwd(x3, h3, w_up[1], w_down[1], "ffn1")

    loss_tile, dx4, d_final_g = _loss_head(x4, final_norm_g[None, :], tgt, name="loss_head")
    loss = lax.psum(loss_tile[0, 0], ("x", "y", "c"))

    gw = {}
    dx3, d_up1, d_down1, d_nf1 = _ffn_bwd(dx4, x3, h3, r1, a1, norm_ffn_g[1:2], w_up[1], w_down[1], "ffn1")

    d_moa = _matmul(dx3, w_mla_out, mode="nt", out_dtypes=(BF16,), epilogue=_delta_epilogue(mla_vwd, v_dim),
                    extras=(moa,), name="mla_d_ctx")
    gw["mla_w_out"] = _unpad_heads(_matmul(moa, dx3, mode="tn", out_dtypes=(F32,), name="mla_d_w_out"),
                                   mla_h, v_dim, 0)[None]
    d_mqa, d_mka, d_mva = _flash_bwd(mqb, mka, mva, d_moa, mla_h, MLA_BWD_HEADS_PER_STEP,
                                     name="mla_attn_bwd")
    d_c_q, d_w_qa_, d_w_qb_ = _rope_proj_bwd(c_q, d_mqa, w_qa_, w_qb_, cos_t, sin_t, mla_scale, mla_h,
                                             name="mla_q_b_rope_bwd")
    d_w_qa_ = _unpad_heads(d_w_qa_, mla_h, mla_dk, 1).reshape(q_rank, mla_h, mla_dk)
    d_w_qb_ = _unpad_heads(d_w_qb_, mla_h, mla_dk, 1).reshape(q_rank, mla_h, mla_dk)
    gw["mla_w_q_b"] = jnp.concatenate(
        [d_w_qa_[:, :, :nope], d_w_qa_[:, :, nope:nope + half] + d_w_qb_[:, :, nope + half:],
         d_w_qa_[:, :, nope + half:] - d_w_qb_[:, :, nope:nope + half]], axis=-1).reshape(1, q_rank, mla_h * mla_dk)
    d_cq_pre, (d_q_a_g,) = _rms_bwd(cq_pre, [(mla_q_a_norm_g, d_c_q)], None, name="mla_d_norm_q_a")
    gw["mla_w_q_a"] = _matmul(h2, d_cq_pre, mode="tn", out_dtypes=(F32,), name="mla_d_w_q_a")[None]
    d_h2 = _matmul(d_cq_pre, w_q_a, mode="nt", out_dtypes=(F32,), name="mla_d_h")

    d_w_kn = _unpad_heads(_matmul(c_kv, d_mka, mode="tn", out_dtypes=(F32,), name="mla_d_w_k"), mla_h, nope, 1)
    d_w_mv = _unpad_heads(_matmul(c_kv, d_mva, mode="tn", out_dtypes=(F32,), name="mla_d_w_v"), mla_h, v_dim, 1)
    gw["mla_w_kv_b"] = jnp.concatenate([d_w_kn.reshape(kv_rank, mla_h, nope), d_w_mv.reshape(kv_rank, mla_h, v_dim)],
                                       axis=-1).reshape(kv_rank, mla_h * (nope + v_dim))
    d_c_kv_v = _matmul(d_mva, w_mv, mode="nt", out_dtypes=(F32,), name="mla_d_c_kv_v")
    d_c_kv = _matmul(d_mka, w_kn, mode="nt", out_dtypes=(F32,), epilogue=add_res, extras=(d_c_kv_v,),
                     name="mla_d_c_kv")
    d_ckv_pre, (d_kv_a_g,) = _rms_bwd(kv_a, [(mla_kv_a_norm_g[None, :], d_c_kv)], None, name="mla_d_norm_kv_a")
    d_kr_u = _matmul(d_mka, e_kr_u, mode="nn", out_dtypes=(F32,), name="mla_d_k_rope_u")
    d_kr_v = _matmul(d_mka, e_kr_v, mode="nn", out_dtypes=(F32,), name="mla_d_k_rope_v")
    d_kr = _rope_mix(d_kr_u, d_kr_v, cos_k, sin_k, 1.0, 1, name="mla_rope_dk")
    d_kv_a = jnp.concatenate([d_ckv_pre, d_kr[:, :rope].astype(F32)], axis=1)
    gw["mla_w_kv_a"] = _matmul(src, d_kv_a, mode="tn", out_dtypes=(F32,), name="mla_d_w_kv_a")
    d_src = _matmul(d_kv_a, w_kv_a, mode="nt", out_dtypes=(F32,), name="mla_d_src")
    dx2, (d_kv_g, d_nm1) = _rms_bwd(x2, [(kv_norm_g[None, :], d_src), (norm_mix_g[1:2], d_h2)], dx3,
                                    name="l1_d_norm_kv_mix")

    dx1, d_up0, d_down0, d_nf0 = _ffn_bwd(dx2, x1, h1, r0, a0, norm_ffn_g[0:1], w_up[0], w_down[0], "ffn0")
    by_rows = lambda g: g.reshape(N_CHIPS, g.shape[0] // N_CHIPS, g.shape[1])
    gw_by_chip = {"ffn_w_up": jnp.concatenate([d_up0, d_up1], axis=1),
                  "ffn_w_down": jnp.concatenate([by_rows(d_down0), by_rows(d_down1)], axis=1)}

    d_foa = _matmul(dx1, w_fox_out, mode="nt", out_dtypes=(BF16,), epilogue=_delta_epilogue(fox_vwd, fox_dh),
                    extras=(foa,), name="fox_d_ctx")
    gw["fox_w_out"] = _unpad_heads(_matmul(foa, dx1, mode="tn", out_dtypes=(F32,), name="fox_d_w_out"),
                                   fox_h, fox_dh, 0)[None]
    fox_hps = FOX_BWD_HEADS_PER_STEP if fox_h % FOX_BWD_HEADS_PER_STEP == 0 else 1
    d_fqa, d_fka, d_fva, ds_rows, ds_cols = _flash_bwd(fqb, fka, fva, d_foa, fox_h, fox_hps, name="fox_attn_bwd",
                                                       sum_cols=(fox_dh, fox_dh + 3))
    d_cum = jnp.transpose(ds_rows - ds_cols, (1, 0, 2)).reshape(seq, fox_h)
    d_z = lax.cumsum(d_cum, axis=0, reverse=True) * jax.nn.sigmoid(-z)
    d_b_f = jnp.sum(d_z, axis=0)
    d_w_in = [_unpad_heads(_matmul(h0, g, mode="tn", out_dtypes=(F32,), name=f"fox_d_w_{tag}"), fox_h, fox_dh, 1)
              for tag, g in (("q", d_fqa), ("k", d_fka), ("v", d_fva))]
    d_w_gate = _matmul(h0, d_z, mode="tn", out_dtypes=(F32,), name="fox_d_w_gate")
    gw["fox_w_in"] = jnp.concatenate([d_w_in[0] * fox_scale, d_w_in[1], d_w_in[2], d_w_gate], axis=1)[None]
    d_h0 = _matmul(d_z, w_gate, mode="nt", out_dtypes=(F32,), name="fox_d_h_gate")
    for tag, g, w in (("q", d_fqa, w_fq), ("k", d_fka, w_fk), ("v", d_fva, w_fv)):
        d_h0 = _matmul(g, w, mode="nt", out_dtypes=(F32,), epilogue=add_res, extras=(d_h0,), name=f"fox_d_h_{tag}")
    grad_x, (d_nm0,) = _rms_bwd(xs, [(norm_mix_g[0:1], d_h0)], dx1, name="l0_d_norm_mix")

    c_idx = lax.axis_index("c").astype(jnp.int32).reshape(1)
    parts = []
    for (n, ax), shape in zip(_BIG, shard_shapes):
        if n in gw_by_chip:
            g = gw_by_chip[n]
        else:
            g = gw[n]
            g = jnp.moveaxis(g.reshape(g.shape[:ax] + (N_CHIPS, shape[ax]) + g.shape[ax + 1:]), ax, 0)
            g = g.reshape(N_CHIPS, -1, shape[-1])
        parts.append(jnp.pad(g, ((0, 0), (0, _part_rows(shape) - g.shape[1]), (0, PACK_LANES - shape[-1]))))
    parts.append(jnp.zeros((N_CHIPS, rows - sum(p.shape[1] for p in parts), PACK_LANES), F32))
    g4 = jnp.concatenate(parts, axis=1)
    a4 = _sibling_swap_halves(g4, name="grads_to_sibling")
    s4 = _chip_sum(g4, a4, c_idx, name="grads_chip_sum")
    b3 = _chip_exchange(s4, name="grads_between_chips")
    t_mine = _sum_chips(s4, b3, p_chip.astype(jnp.int32).reshape(1), name="grads_sum_chips")
    t_theirs = _sibling_swap(t_mine, name="grads_join_halves")
    is_south = lax.axis_index("c") == 0
    g_big = jnp.concatenate([jnp.where(is_south, t_mine, t_theirs), jnp.where(is_south, t_theirs, t_mine)],
                            axis=0)

    small_local = {"norm_mix_g": jnp.concatenate([d_nm0, d_nm1], axis=0),
                   "norm_ffn_g": jnp.concatenate([d_nf0, d_nf1], axis=0),
                   "fox_b_f": d_b_f[None, :], "kv_norm_g": d_kv_g[0], "mla_kv_a_norm_g": d_kv_a_g[0],
                   "mla_q_a_norm_g": d_q_a_g, "final_norm_g": d_final_g[0]}
    small_shapes = [w_in[n].shape for n in _SMALL]
    small_rows = sum(_part_rows(s, SMALL_PART_ROWS) for s in small_shapes)
    pack_small = lambda arrs: _pack(arrs, small_rows, F32, SMALL_PART_ROWS)
    g_small = _all_reduce_small(pack_small([small_local[n] for n in _SMALL]), name="grads_small")

    grads = dict(zip(big_names, _unpack(g_big, shard_shapes)))
    delta, new_m, new_v = {}, {}, {}
    for n, shape in zip(big_names, shard_shapes):
        flat = lambda a: a.reshape(-1, shape[-1])
        outs = _adamw(flat(w_in[n]), flat(grads[n]), flat(m_in[n]), flat(v_in[n]), name=f"adamw_{n}")
        delta[n], new_m[n], new_v[n] = (o.reshape(shape) for o in outs)
    sm_outs = _adamw(pack_small([w_in[n] for n in _SMALL]), g_small, pack_small([m_in[n] for n in _SMALL]),
                     pack_small([v_in[n] for n in _SMALL]), name="adamw_small")
    for res, packed in zip((grads, delta, new_m, new_v), (g_small,) + tuple(sm_outs)):
        res.update(zip(_SMALL, _unpack(packed, small_shapes, SMALL_PART_ROWS)))

    return (loss, grad_x[None], *[grads[n] for n in _WEIGHTS], *[delta[n] for n in _WEIGHTS],
            *[new_m[n] for n in _WEIGHTS], *[new_v[n] for n in _WEIGHTS])
```

```python
import math

import numpy as np
import jax
import jax.numpy as jnp
from jax import lax
from jax.experimental import pallas as pl
from jax.experimental.pallas import tpu as pltpu

F32 = jnp.float32
BF16 = jnp.bfloat16

FOX_HEADS = 16
MLA_HEADS = 8
QK_NOPE_DIM = 128
ROPE_BASE = 10000.0
EPS = 1e-6

ADAM_LR = 0.001
ADAM_B1 = 0.9
ADAM_B2 = 0.999
ADAM_EPS = 1e-08
ADAM_WD = 0.01
ADAM_STEP = 10

N_CHIPS = 4
N_DEV = 8
PACK_LANES = 1024
PACK_PART_ROWS = 16
SMALL_PART_ROWS = 8
PACK_ROWS_MULT = 1024
VMEM_LIMIT_BYTES = 48 * 1024 * 1024
LANE_TILE = 128
MATMUL_BLOCK = 1024
MATMUL_WIDE_BLOCK = 2048
MATMUL_DEPTH = 2048
ATTN_BLOCK_Q = 1024
ATTN_BLOCK_K = 1024
ATTN_FWD_LANES = 1024
FOX_BWD_HEADS_PER_STEP = 4
MLA_BWD_HEADS_PER_STEP = 2
ATTN_SUB_ROWS = 256
FOX_FWD_SUB_ROWS = (1024, 512)
MLA_FWD_SUB_ROWS = (256, 256)
NEG_BIG = -1e30
MESH = pl.DeviceIdType.MESH


def _round_up(n, m):
    return -(-n // m) * m


def _blk(dim, pref, mult=128):
    if dim <= pref:
        return dim
    b = (pref // mult) * mult
    while b >= mult:
        if dim % b == 0:
            return b
        b -= mult
    return dim


def _params(sem=None):
    return pltpu.CompilerParams(dimension_semantics=sem, vmem_limit_bytes=VMEM_LIMIT_BYTES)


_DIMS = {"nn": (((1,), (0,)), ((), ())), "nt": (((1,), (1,)), ((), ())), "tn": (((0,), (0,)), ((), ()))}


def _matmul(a, b, *, mode, out_dtypes, name, epilogue=None, extras=(), placed=None, by_chip=False):
    if mode == "tn":
        kdim, m = a.shape
    else:
        m, kdim = a.shape
    n = b.shape[0] if mode == "nt" else b.shape[1]
    bm, bk = _blk(m, MATMUL_BLOCK), _blk(kdim, MATMUL_DEPTH)
    bn = _blk(n, MATMUL_WIDE_BLOCK if (mode != "tn" and kdim <= MATMUL_BLOCK) else MATMUL_BLOCK)
    if by_chip:
        bn = _blk(n // N_CHIPS, bn)
    nk = kdim // bk
    n_extra, n_out = len(extras), len(out_dtypes)
    n_placed = 0 if placed is None else 2
    dims = _DIMS[mode]

    def body(a_ref, b_ref, *rest):
        placed_refs = rest[:n_placed]
        rest = rest[n_placed:]
        extra_refs = rest[:n_extra]
        out_refs = rest[n_extra:n_extra + n_out]

        def finish(acc):
            if n_placed:
                acc = acc + lax.dot_general(placed_refs[0][...], placed_refs[1][...], _DIMS["nn"],
                                            preferred_element_type=F32)
            res = (acc,) if epilogue is None else epilogue(acc, *[r[...] for r in extra_refs])
            for o_ref, r in zip(out_refs, res):
                o_ref[...] = r.astype(o_ref.dtype)

        part = lax.dot_general(a_ref[...].astype(BF16), b_ref[...].astype(BF16), dims, preferred_element_type=F32)
        if nk == 1:
            finish(part)
            return
        acc_ref = rest[n_extra + n_out]
        k = pl.program_id(2)

        @pl.when(k == 0)
        def _():
            acc_ref[...] = part

        @pl.when((k > 0) & (k < nk - 1))
        def _():
            acc_ref[...] += part

        @pl.when(k == nk - 1)
        def _():
            finish(acc_ref[...] + part)

    if mode == "tn":
        a_spec = pl.BlockSpec((bk, bm), lambda i, j, k: (k, i))
    else:
        a_spec = pl.BlockSpec((bm, bk), lambda i, j, k: (i, k))
    if mode == "nt":
        b_spec = pl.BlockSpec((bn, bk), lambda i, j, k: (j, k))
    else:
        b_spec = pl.BlockSpec((bk, bn), lambda i, j, k: (k, j))
    tile = pl.BlockSpec((bm, bn), lambda i, j, k: (i, j))
    placed_specs = []
    if n_placed:
        k2 = placed[0].shape[1]
        placed_specs = [pl.BlockSpec((bm, k2), lambda i, j, k: (i, 0)), pl.BlockSpec((k2, bn), lambda i, j, k: (0, j))]
    out_tile, out_dims = tile, (m, n)
    if by_chip:
        per_chip = n // N_CHIPS // bn
        out_tile = pl.BlockSpec((None, bm, bn), lambda i, j, k: (j // per_chip, i, j % per_chip))
        out_dims = (N_CHIPS, m, n // N_CHIPS)
    outs = pl.pallas_call(
        body, name=name,
        grid=(m // bm, n // bn, nk),
        in_specs=[a_spec, b_spec] + placed_specs + [tile] * n_extra,
        out_specs=[out_tile] * n_out,
        out_shape=[jax.ShapeDtypeStruct(out_dims, dt) for dt in out_dtypes],
        scratch_shapes=[pltpu.VMEM((bm, bn), F32)] if nk > 1 else [],
        compiler_params=_params(("parallel", "parallel", "arbitrary")),
    )(a, b, *(placed or ()), *extras)
    return outs[0] if n_out == 1 else outs


def _rms_fwd(x, gains, name):
    s = x.shape[0]
    g, w = gains.shape
    bs = _blk(s, 512, 8)

    def body(x_ref, g_ref, *out_refs):
        xv = x_ref[...]
        y = xv * lax.rsqrt(jnp.mean(xv * xv, axis=-1, keepdims=True) + EPS)
        for i, o_ref in enumerate(out_refs):
            o_ref[...] = (y * g_ref[i:i + 1, :]).astype(o_ref.dtype)

    row = pl.BlockSpec((bs, w), lambda i: (i, 0))
    outs = pl.pallas_call(
        body, name=name, grid=(s // bs,),
        in_specs=[row, pl.BlockSpec((g, w), lambda i: (0, 0))],
        out_specs=[row] * g,
        out_shape=[jax.ShapeDtypeStruct((s, w), BF16)] * g,
        compiler_params=_params(("parallel",)),
    )(x, gains)
    return outs


def _rms_bwd(x, branches, resid, name):
    s = x.shape[0]
    w = branches[0][0].shape[1]
    nb = len(branches)
    bs = _blk(s, 512, 8)
    has_resid = resid is not None

    def body(x_ref, *rest):
        g_refs = rest[:nb]
        dy_refs = rest[nb:2 * nb]
        pos = 2 * nb
        r_ref = rest[pos] if has_resid else None
        pos += int(has_resid)
        dx_ref = rest[pos]
        dg_refs = rest[pos + 1:pos + 1 + nb]
        i = pl.program_id(0)

        @pl.when(i == 0)
        def _():
            for dg_ref in dg_refs:
                dg_ref[...] = jnp.zeros_like(dg_ref)

        xv = x_ref[...]
        rstd = lax.rsqrt(jnp.mean(xv * xv, axis=-1, keepdims=True) + EPS)
        xhat = xv * rstd
        dx = r_ref[...] if has_resid else jnp.zeros_like(xv)
        for g_ref, dy_ref, dg_ref in zip(g_refs, dy_refs, dg_refs):
            dy = dy_ref[...].astype(F32)
            dyg = dy * g_ref[...]
            dx = dx + rstd * (dyg - xhat * jnp.mean(dyg * xhat, axis=-1, keepdims=True))
            dg_ref[...] += jnp.sum(dy * xhat, axis=0, keepdims=True)
        dx_ref[...] = dx

    row = pl.BlockSpec((bs, w), lambda i: (i, 0))
    vec = pl.BlockSpec((1, w), lambda i: (0, 0))
    args = [x] + [g for g, _ in branches] + [dy for _, dy in branches] + ([resid] if has_resid else [])
    outs = pl.pallas_call(
        body, name=name, grid=(s // bs,),
        in_specs=[row] + [vec] * nb + [row] * nb + ([row] if has_resid else []),
        out_specs=[row] + [vec] * nb,
        out_shape=[jax.ShapeDtypeStruct((s, w), F32)] + [jax.ShapeDtypeStruct((1, w), F32)] * nb,
        compiler_params=_params(("arbitrary",)),
    )(*args)
    return outs[0], list(outs[1:])


def _loss_head(x, g, target, name):
    s, w = x.shape
    bs = _blk(s, 512, 8)

    def body(x_ref, g_ref, t_ref, loss_ref, dx_ref, dg_ref):
        i = pl.program_id(0)

        @pl.when(i == 0)
        def _():
            loss_ref[...] = jnp.zeros_like(loss_ref)
            dg_ref[...] = jnp.zeros_like(dg_ref)

        xv = x_ref[...]
        gv = g_ref[...]
        rstd = lax.rsqrt(jnp.mean(xv * xv, axis=-1, keepdims=True) + EPS)
        xhat = xv * rstd
        err = xhat * gv - t_ref[...]
        loss_ref[...] += 0.5 * jnp.sum(jnp.mean(err * err, axis=-1, keepdims=True))
        dy = err * (1.0 / w)
        dyg = dy * gv
        dx_ref[...] = rstd * (dyg - xhat * jnp.mean(dyg * xhat, axis=-1, keepdims=True))
        dg_ref[...] += jnp.sum(dy * xhat, axis=0, keepdims=True)

    row = pl.BlockSpec((bs, w), lambda i: (i, 0))
    vec = pl.BlockSpec((1, w), lambda i: (0, 0))
    return pl.pallas_call(
        body, name=name, grid=(s // bs,),
        in_specs=[row, vec, row],
        out_specs=[pl.BlockSpec((8, 128), lambda i: (0, 0)), row, vec],
        out_shape=[jax.ShapeDtypeStruct((8, 128), F32), jax.ShapeDtypeStruct((s, w), F32),
                   jax.ShapeDtypeStruct((1, w), F32)],
        compiler_params=_params(("arbitrary",)),
    )(x, g, target)


def _rope(a, b, cos, sin, sign, name):
    g, s, w = a.shape
    bs = _blk(s, 1024, 8)

    def body(a_ref, b_ref, c_ref, s_ref, o1_ref, o2_ref):
        av = jnp.sum(a_ref[...].astype(F32), axis=0)
        bv = jnp.sum(b_ref[...].astype(F32), axis=0)
        cv, sv = c_ref[...], s_ref[...] * sign
        o1_ref[...] = av * cv - bv * sv
        o2_ref[...] = bv * cv + av * sv

    grp = pl.BlockSpec((g, bs, w), lambda i: (0, i, 0))
    row = pl.BlockSpec((bs, w), lambda i: (i, 0))
    return pl.pallas_call(
        body, name=name, grid=(s // bs,),
        in_specs=[grp, grp, row, row], out_specs=[row, row],
        out_shape=[jax.ShapeDtypeStruct((s, w), F32)] * 2,
        compiler_params=_params(("parallel",)),
    )(a, b, cos, sin)


def _causal_table(s, bq, bk, q_major):
    nq, nk = s // bq, s // bk
    rows = []
    if q_major:
        for qi in range(nq):
            kmax = (qi * bq + bq - 1) // bk
            for ki in range(kmax + 1):
                rows.append((qi, ki, int(ki * bk + bk - 1 > qi * bq), int(ki == 0), int(ki == kmax)))
    else:
        for ki in range(nk):
            qmin = (ki * bk) // bq
            for qi in range(qmin, nq):
                rows.append((qi, ki, int(ki * bk + bk - 1 > qi * bq), int(qi == qmin), int(qi == nq - 1)))
    return jnp.asarray(np.array(rows, np.int32).T)


def _causal_keep(q0, k0, nq, nk, transposed):
    if transposed:
        kpos = k0 + lax.broadcasted_iota(jnp.int32, (nk, nq), 0)
        qpos = q0 + lax.broadcasted_iota(jnp.int32, (nk, nq), 1)
    else:
        qpos = q0 + lax.broadcasted_iota(jnp.int32, (nq, nk), 0)
        kpos = k0 + lax.broadcasted_iota(jnp.int32, (nq, nk), 1)
    return kpos <= qpos


def _sub_tiles(n_rows, n_cols, masked, square, rows_are_keys, sub_rows):
    sub = min(sub_rows, n_rows)
    out = []
    for r0 in range(0, n_rows, sub):
        if masked and square:
            c0, nc = (r0, n_cols - r0) if rows_are_keys else (0, r0 + sub)
        else:
            c0, nc = 0, n_cols
        out.append((r0, sub, c0, nc))
    return out


_NT = (((1,), (1,)), ((), ()))
_NN = (((1,), (0,)), ((), ()))


def _attn_specs(bq, bk):
    qspec = lambda d: pl.BlockSpec((bq, d), lambda hh, t, tb: (tb[0, t], hh))
    kspec = lambda d: pl.BlockSpec((bk, d), lambda hh, t, tb: (tb[1, t], hh))
    return qspec, kspec


def _split3_cols(x):
    hi = x.astype(BF16).astype(F32)
    rest = x - hi
    mid = rest.astype(BF16).astype(F32)
    lo = (rest - mid).astype(BF16).astype(F32)
    return hi, mid, lo


def _place3(base, col, pieces, sign):
    lane = lax.broadcasted_iota(jnp.int32, base.shape, 1)
    out = base.astype(F32)
    for i, piece in enumerate(pieces):
        out = jnp.where(lane == col + i, sign * piece, out)
    return out.astype(BF16)


def _flash_fwd(qa, ka, va, heads, l_col, lse_col, sub_rows, name):
    s = qa.shape[0]
    da, dv = qa.shape[1] // heads, va.shape[1] // heads
    hps = max(n for n in range(1, ATTN_FWD_LANES // max(da, dv) + 1) if heads % n == 0)
    bq, bk = _blk(s, ATTN_BLOCK_Q), _blk(s, ATTN_BLOCK_K)
    tab = _causal_table(s, bq, bk, True)

    def body(tab_ref, q_ref, k_ref, v_ref, o_ref, qb_ref, m_sc, acc_sc):
        t = pl.program_id(1)
        qi, ki = tab_ref[0, t], tab_ref[1, t]

        @pl.when(tab_ref[3, t] == 1)
        def _():
            m_sc[...] = jnp.full_like(m_sc, NEG_BIG)
            acc_sc[...] = jnp.zeros_like(acc_sc)

        def step(masked):
            for hh in range(hps):
                qc, vc = slice(hh * da, (hh + 1) * da), slice(hh * dv, (hh + 1) * dv)
                for r0, nr, c0, nc in _sub_tiles(bq, bk, masked, bq == bk, False, sub_rows[int(masked)]):
                    sc = lax.dot_general(q_ref[r0:r0 + nr, qc], k_ref[c0:c0 + nc, qc], _NT,
                                         preferred_element_type=F32)
                    if masked:
                        sc = jnp.where(_causal_keep(qi * bq + r0, ki * bk + c0, nr, nc, False), sc, NEG_BIG)
                    m_prev = m_sc[hh, r0:r0 + nr, :]
                    m_new = jnp.maximum(m_prev, jnp.max(sc, axis=-1, keepdims=True))
                    p = jnp.exp(sc - m_new).astype(BF16)
                    acc_sc[r0:r0 + nr, vc] = jnp.exp(m_prev - m_new) * acc_sc[r0:r0 + nr, vc] + lax.dot_general(
                        p, v_ref[c0:c0 + nc, vc], _NN, preferred_element_type=F32)
                    m_sc[hh, r0:r0 + nr, :] = m_new

        @pl.when(tab_ref[2, t] == 1)
        def _():
            step(True)

        @pl.when(tab_ref[2, t] == 0)
        def _():
            step(False)

        @pl.when(tab_ref[4, t] == 1)
        def _():
            for hh in range(hps):
                qc, vc = slice(hh * da, (hh + 1) * da), slice(hh * dv, (hh + 1) * dv)
                acc = acc_sc[:, vc]
                lane = lax.broadcasted_iota(jnp.int32, acc.shape, 1)
                l = jnp.sum(jnp.where(lane == l_col, acc, 0.0), axis=-1, keepdims=True)
                o_ref[:, vc] = (acc / l).astype(o_ref.dtype)
                lse = m_sc[hh] + jnp.log(l)
                qb_ref[:, qc] = _place3(q_ref[:, qc], lse_col, _split3_cols(lse), -1.0)

    qspec, kspec = _attn_specs(bq, bk)
    return pl.pallas_call(
        body, name=name,
        grid_spec=pltpu.PrefetchScalarGridSpec(
            num_scalar_prefetch=1, grid=(heads // hps, tab.shape[1]),
            in_specs=[qspec(hps * da), kspec(hps * da), kspec(hps * dv)],
            out_specs=[qspec(hps * dv), qspec(hps * da)],
            scratch_shapes=[pltpu.VMEM((hps, bq, 1), F32), pltpu.VMEM((bq, hps * dv), F32)]),
        out_shape=[jax.ShapeDtypeStruct((s, heads * dv), BF16), jax.ShapeDtypeStruct((s, heads * da), BF16)],
        compiler_params=_params(("parallel", "arbitrary")),
    )(tab, qa, ka, va)


def _delta_epilogue(dv, delta_col):
    def epilogue(acc, o_tile):
        heads_out = []
        for hh in range(acc.shape[1] // dv):
            vc = slice(hh * dv, (hh + 1) * dv)
            dov = acc[:, vc].astype(BF16)
            delta = jnp.sum(dov.astype(F32) * o_tile[:, vc].astype(F32), axis=-1, keepdims=True)
            heads_out.append(_place3(dov, delta_col, _split3_cols(delta), 1.0))
        return (jnp.concatenate(heads_out, axis=1),)
    return epilogue


_TN =(((0,), (0,)), ((), ()))


def _flash_bwd(qa, ka, va, doa, heads, hps, name, sum_cols=None):
    s = qa.shape[0]
    da, dv = qa.shape[1] // heads, va.shape[1] // heads
    h = heads // hps
    bq, bk = _blk(s, ATTN_BLOCK_Q), _blk(s, ATTN_BLOCK_K)
    tab = _causal_table(s, bq, bk, False)
    n_tiles = tab.shape[1]
    n_sum = 0 if sum_cols is None else 2

    def head_column(acc, col):
        out = jnp.zeros((acc.shape[0], hps), F32)
        lane = lax.broadcasted_iota(jnp.int32, (acc.shape[0], da), 1)
        pick = lax.broadcasted_iota(jnp.int32, out.shape, 1)
        for hh in range(hps):
            val = jnp.sum(jnp.where(lane == col, acc[:, hh * da:(hh + 1) * da], 0.0), axis=-1, keepdims=True)
            out = jnp.where(pick == hh, val, out)
        return out

    def body(tab_ref, q_ref, k_ref, v_ref, do_ref, dq_ref, dk_ref, dv_ref, *rest):
        sum_refs, (dk_sc, dv_sc) = rest[:n_sum], rest[n_sum:]
        t = pl.program_id(1)
        qi, ki = tab_ref[0, t], tab_ref[1, t]

        @pl.when(t == 0)
        def _():
            dq_ref[...] = jnp.zeros_like(dq_ref)

        @pl.when(tab_ref[3, t] == 1)
        def _():
            dk_sc[...] = jnp.zeros_like(dk_sc)
            dv_sc[...] = jnp.zeros_like(dv_sc)

        def step(masked):
            for hh in range(hps):
                qc, vc = slice(hh * da, (hh + 1) * da), slice(hh * dv, (hh + 1) * dv)
                for r0, nr, c0, nc in _sub_tiles(bk, bq, masked, bq == bk, True, ATTN_SUB_ROWS):
                    qv, dov, kv = q_ref[c0:c0 + nc, qc], do_ref[c0:c0 + nc, vc], k_ref[r0:r0 + nr, qc]
                    st = lax.dot_general(kv, qv, _NT, preferred_element_type=F32)
                    if masked:
                        st = jnp.where(_causal_keep(qi * bq + c0, ki * bk + r0, nc, nr, True), st, NEG_BIG)
                    pt = jnp.exp(st)
                    dv_sc[r0:r0 + nr, vc] += lax.dot_general(pt.astype(BF16), dov, _NN, preferred_element_type=F32)
                    dpt = lax.dot_general(v_ref[r0:r0 + nr, vc], dov, _NT, preferred_element_type=F32)
                    dst = (pt * dpt).astype(BF16)
                    dk_sc[r0:r0 + nr, qc] += lax.dot_general(dst, qv, _NN, preferred_element_type=F32)
                    q_rows = pl.ds(pl.multiple_of(qi * bq + c0, ATTN_SUB_ROWS), nc)
                    dq_ref[q_rows, qc] += lax.dot_general(dst, kv, _TN, preferred_element_type=F32)

        @pl.when(tab_ref[2, t] == 1)
        def _():
            step(True)

        @pl.when(tab_ref[2, t] == 0)
        def _():
            step(False)

        @pl.when(tab_ref[4, t] == 1)
        def _():
            dk_ref[...] = dk_sc[...]
            dv_ref[...] = dv_sc[...]
            if n_sum:
                sum_refs[1][...] = head_column(dk_sc[...], sum_cols[1])

        if n_sum:
            @pl.when(t == n_tiles - 1)
            def _():
                sum_refs[0][...] = head_column(dq_ref[...], sum_cols[0])

    qspec, kspec = _attn_specs(bq, bk)
    out_specs = [pl.BlockSpec((s, hps * da), lambda hh, t, tb: (0, hh), pipeline_mode=pl.Buffered(1)),
                 kspec(hps * da), kspec(hps * dv)]
    out_shape = [jax.ShapeDtypeStruct((s, heads * da), F32), jax.ShapeDtypeStruct((s, heads * da), F32),
                 jax.ShapeDtypeStruct((s, heads * dv), F32)]
    if n_sum:
        out_specs += [pl.BlockSpec((None, s, hps), lambda hh, t, tb: (hh, 0, 0), pipeline_mode=pl.Buffered(1)),
                      pl.BlockSpec((None, bk, hps), lambda hh, t, tb: (hh, tb[1, t], 0))]
        out_shape += [jax.ShapeDtypeStruct((h, s, hps), F32)] * 2
    return pl.pallas_call(
        body, name=name,
        grid_spec=pltpu.PrefetchScalarGridSpec(
            num_scalar_prefetch=1, grid=(h, n_tiles),
            in_specs=[qspec(hps * da), kspec(hps * da), kspec(hps * dv), qspec(hps * dv)],
            out_specs=out_specs,
            scratch_shapes=[pltpu.VMEM((bk, hps * da), F32), pltpu.VMEM((bk, hps * dv), F32)]),
        out_shape=out_shape,
        compiler_params=_params(("parallel", "arbitrary")),
    )(tab, qa, ka, va, doa)


def _split3(x):
    hi = lax.reduce_precision(x, 8, 7)
    rest = x - hi
    mid = lax.reduce_precision(rest, 8, 7)
    lo = lax.reduce_precision(rest - mid, 8, 7)
    return jnp.stack([hi, mid, lo], axis=-1).astype(BF16)


def _pad_heads(w, heads, width, axis):
    shape = list(w.shape)
    d = shape[axis] // heads
    w = w.reshape(shape[:axis] + [heads, d] + shape[axis + 1:])
    pad = [(0, 0)] * w.ndim
    pad[axis + 1] = (0, width - d)
    return jnp.pad(w, pad).reshape(shape[:axis] + [heads * width] + shape[axis + 1:])


def _unpad_heads(w, heads, d, axis):
    shape = list(w.shape)
    width = shape[axis] // heads
    w = w.reshape(shape[:axis] + [heads, width] + shape[axis + 1:])
    w = lax.slice_in_dim(w, 0, d, axis=axis + 1)
    return w.reshape(shape[:axis] + [heads * d] + shape[axis + 1:])


def _placement(rows, heads, width, entries):
    e = np.zeros((rows, heads * width), np.float32)
    for row, col, val in entries:
        for hh in range(heads):
            e[row(hh) if callable(row) else row, hh * width + col] = val
    return jnp.asarray(e, BF16)


def _rope_mix(a, b, cos_t, sin_t, scale, heads, name):
    s = a.shape[0]
    d = a.shape[1] // heads
    bs = _blk(s, 1024, 8)

    def body(a_ref, b_ref, c_ref, s_ref, o_ref):
        o_ref[...] = ((a_ref[...] * c_ref[...] + b_ref[...] * s_ref[...]) * scale).astype(o_ref.dtype)

    blk = pl.BlockSpec((bs, d), lambda i, hh: (i, hh))
    tbl = pl.BlockSpec((bs, d), lambda i, hh: (i, 0))
    return pl.pallas_call(
        body, name=name, grid=(s // bs, heads), in_specs=[blk, blk, tbl, tbl], out_specs=blk,
        out_shape=jax.ShapeDtypeStruct(a.shape, BF16),
        compiler_params=_params(("parallel", "parallel")),
    )(a, b, cos_t, sin_t)


def _rope_proj(x, w_a, w_b, cos_t, sin_t, scale, heads, name):
    s, kdim = x.shape
    d = w_a.shape[1] // heads
    hpt = max(n for n in range(1, max(1, MATMUL_BLOCK // d) + 1) if heads % n == 0)
    bm = _blk(s, MATMUL_BLOCK)
    cos_w, sin_w = jnp.tile(cos_t, (1, hpt)), jnp.tile(sin_t, (1, hpt))

    def body(x_ref, wa_ref, wb_ref, c_ref, s_ref, o_ref):
        xv = x_ref[...]
        a = lax.dot_general(xv, wa_ref[...], _NN, preferred_element_type=F32)
        b = lax.dot_general(xv, wb_ref[...], _NN, preferred_element_type=F32)
        o_ref[...] = ((a * c_ref[...] + b * s_ref[...]) * scale).astype(o_ref.dtype)

    wide = pl.BlockSpec((bm, hpt * d), lambda i, j: (i, j))
    tbl = pl.BlockSpec((bm, hpt * d), lambda i, j: (i, 0))
    wgt = pl.BlockSpec((kdim, hpt * d), lambda i, j: (0, j))
    return pl.pallas_call(
        body, name=name, grid=(s // bm, heads // hpt),
        in_specs=[pl.BlockSpec((bm, kdim), lambda i, j: (i, 0)), wgt, wgt, tbl, tbl], out_specs=wide,
        out_shape=jax.ShapeDtypeStruct((s, heads * d), BF16),
        compiler_params=_params(("parallel", "parallel")),
    )(x, w_a, w_b, cos_w, sin_w)


def _rope_proj_bwd(x, g, w_a, w_b, cos_t, sin_t, scale, heads, name):
    s, kdim = x.shape
    d = w_a.shape[1] // heads
    hpt = max(n for n in range(1, max(1, MATMUL_BLOCK // d) + 1) if heads % n == 0)
    bs = _blk(s, MATMUL_BLOCK)
    n_tiles, n_rows = heads // hpt, s // bs
    cos_w, sin_w = jnp.tile(cos_t, (1, hpt)), jnp.tile(sin_t, (1, hpt))

    def halves(g_ref, c_ref, s_ref):
        gv = g_ref[...] * scale
        return (gv * c_ref[...]).astype(BF16), (gv * s_ref[...]).astype(BF16)

    def dx_body(g_ref, wa_ref, wb_ref, c_ref, s_ref, dx_ref, acc_ref):
        j = pl.program_id(1)
        ga, gb = halves(g_ref, c_ref, s_ref)
        part = (lax.dot_general(ga, wa_ref[...], _NT, preferred_element_type=F32)
                + lax.dot_general(gb, wb_ref[...], _NT, preferred_element_type=F32))

        @pl.when(j == 0)
        def _():
            acc_ref[...] = part

        @pl.when(j > 0)
        def _():
            acc_ref[...] += part

        @pl.when(j == n_tiles - 1)
        def _():
            dx_ref[...] = acc_ref[...]

    def dw_body(x_ref, g_ref, c_ref, s_ref, dwa_ref, dwb_ref, acc_a, acc_b):
        i = pl.program_id(1)
        ga, gb = halves(g_ref, c_ref, s_ref)
        xv = x_ref[...]
        pa = lax.dot_general(xv, ga, _TN, preferred_element_type=F32)
        pb = lax.dot_general(xv, gb, _TN, preferred_element_type=F32)

        @pl.when(i == 0)
        def _():
            acc_a[...] = pa
            acc_b[...] = pb

        @pl.when(i > 0)
        def _():
            acc_a[...] += pa
            acc_b[...] += pb

        @pl.when(i == n_rows - 1)
        def _():
            dwa_ref[...] = acc_a[...]
            dwb_ref[...] = acc_b[...]

    dx = pl.pallas_call(
        dx_body, name=name + "_dx", grid=(n_rows, n_tiles),
        in_specs=[pl.BlockSpec((bs, hpt * d), lambda i, j: (i, j)),
                  pl.BlockSpec((kdim, hpt * d), lambda i, j: (0, j)), pl.BlockSpec((kdim, hpt * d), lambda i, j: (0, j)),
                  pl.BlockSpec((bs, hpt * d), lambda i, j: (i, 0)), pl.BlockSpec((bs, hpt * d), lambda i, j: (i, 0))],
        out_specs=pl.BlockSpec((bs, kdim), lambda i, j: (i, 0)),
        out_shape=jax.ShapeDtypeStruct((s, kdim), F32),
        scratch_shapes=[pltpu.VMEM((bs, kdim), F32)],
        compiler_params=_params(("parallel", "arbitrary")),
    )(g, w_a, w_b, cos_w, sin_w)
    dwa, dwb = pl.pallas_call(
        dw_body, name=name + "_dw", grid=(n_tiles, n_rows),
        in_specs=[pl.BlockSpec((bs, kdim), lambda j, i: (i, 0)),
                  pl.BlockSpec((bs, hpt * d), lambda j, i: (i, j)),
                  pl.BlockSpec((bs, hpt * d), lambda j, i: (i, 0)), pl.BlockSpec((bs, hpt * d), lambda j, i: (i, 0))],
        out_specs=[pl.BlockSpec((kdim, hpt * d), lambda j, i: (0, j))] * 2,
        out_shape=[jax.ShapeDtypeStruct((kdim, heads * d), F32)] * 2,
        scratch_shapes=[pltpu.VMEM((kdim, hpt * d), F32)] * 2,
        compiler_params=_params(("parallel", "arbitrary")),
    )(x, g, cos_w, sin_w)
    return dx, dwa, dwb


def _adamw(w, g, m, v, name):
    r, wd = w.shape
    br = _blk(r, 512, 8)

    def body(w_ref, g_ref, m_ref, v_ref, d_ref, nm_ref, nv_ref):
        gv = g_ref[...]
        mn = ADAM_B1 * m_ref[...] + (1.0 - ADAM_B1) * gv
        vn = ADAM_B2 * v_ref[...] + (1.0 - ADAM_B2) * (gv * gv)
        m_hat = mn / (1.0 - ADAM_B1 ** ADAM_STEP)
        v_hat = vn / (1.0 - ADAM_B2 ** ADAM_STEP)
        d_ref[...] = -ADAM_LR * (m_hat / (jnp.sqrt(v_hat) + ADAM_EPS) + ADAM_WD * w_ref[...])
        nm_ref[...] = mn
        nv_ref[...] = vn

    row = pl.BlockSpec((br, wd), lambda i: (i, 0))
    return pl.pallas_call(
        body, name=name, grid=(r // br,), in_specs=[row] * 4, out_specs=[row] * 3,
        out_shape=[jax.ShapeDtypeStruct((r, wd), F32)] * 3,
        compiler_params=_params(("parallel",)),
    )(w, g, m, v)


_ANY = pl.BlockSpec(memory_space=pl.ANY)


def _place():
    x, y, c = lax.axis_index("x"), lax.axis_index("y"), lax.axis_index("c")
    chips = [(x, 1 - y), (1 - x, y), (1 - x, 1 - y)]
    return x, y, c, chips


def _all_gather_shards(shard, name):
    r, w = shard.shape
    hr = r // 2
    qr = hr // 2

    def body(x_ref, out_ref, send_sems, recv_sems):
        x, y, c, _ = _place()
        me, sibling, y_nbr, x_nbr = (x, y, c), (x, y, 1 - c), (x, 1 - y, c), (1 - x, y, c)

        def rows(j, half, piece=None):
            if piece is None:
                return out_ref.at[j, pl.ds(pl.multiple_of(half * hr, 16), hr), :]
            return out_ref.at[j, pl.ds(pl.multiple_of(half * hr + piece * qr, 16), qr), :]

        def mine(piece):
            return x_ref.at[pl.ds(pl.multiple_of(c * hr + piece * qr, 16), qr), :]

        def copy(sem, src, dst, to):
            return pltpu.make_async_remote_copy(src_ref=src, dst_ref=dst, send_sem=send_sems.at[sem],
                                                recv_sem=recv_sems.at[sem], device_id=to, device_id_type=MESH)

        sent = [copy(0, mine(0), rows(0, c, 0), y_nbr), copy(1, mine(1), rows(0, c, 1), y_nbr),
                copy(2, mine(0), rows(1, c, 0), x_nbr), copy(3, mine(1), rows(1, c, 1), x_nbr)]
        for cp in sent:
            cp.start()

        def landed(sem, ref):
            copy(sem, ref, ref, me).wait_recv()

        def pass_on(sem, src, dst, to):
            cp = copy(sem, src, dst, to)
            cp.start()
            sent.append(cp)

        landed(2, rows(1, c, 0))
        pass_on(4, rows(1, c, 0), rows(2, c, 0), y_nbr)
        landed(1, rows(0, c, 1))
        pass_on(5, rows(0, c, 1), rows(2, c, 1), x_nbr)
        landed(0, rows(0, c, 0))
        pass_on(6, rows(0, c), rows(0, c), sibling)
        landed(3, rows(1, c, 1))
        pass_on(7, rows(1, c), rows(1, c), sibling)
        landed(4, rows(2, c, 0))
        landed(5, rows(2, c, 1))
        pass_on(8, rows(2, c), rows(2, c), sibling)
        for j in range(3):
            landed(6 + j, rows(j, 1 - c))
        for cp in sent:
            cp.wait_send()

    return pl.pallas_call(
        body, name=name, in_specs=[_ANY], out_specs=_ANY,
        out_shape=jax.ShapeDtypeStruct((N_CHIPS - 1, r, w), shard.dtype),
        scratch_shapes=[pltpu.SemaphoreType.DMA((9,)), pltpu.SemaphoreType.DMA((9,))],
        compiler_params=pltpu.CompilerParams(vmem_limit_bytes=VMEM_LIMIT_BYTES),
    )(shard)


def _sibling_swap_halves(g, name):
    nq, r, w = g.shape
    hr = r // 2

    def body(g_ref, a_ref, send_sems, recv_sems):
        x, y, c, _ = _place()
        sibling = (x, y, 1 - c)
        cps = []
        for q in range(nq):
            cp = pltpu.make_async_remote_copy(
                src_ref=g_ref.at[q, pl.ds(pl.multiple_of((1 - c) * hr, 8), hr), :], dst_ref=a_ref.at[q],
                send_sem=send_sems.at[q], recv_sem=recv_sems.at[q], device_id=sibling, device_id_type=MESH)
            cp.start()
            cps.append(cp)
        for cp in cps:
            cp.wait()

    return pl.pallas_call(
        body, name=name, in_specs=[_ANY], out_specs=_ANY,
        out_shape=jax.ShapeDtypeStruct((nq, hr, w), g.dtype),
        scratch_shapes=[pltpu.SemaphoreType.DMA((nq,)), pltpu.SemaphoreType.DMA((nq,))],
        compiler_params=pltpu.CompilerParams(vmem_limit_bytes=VMEM_LIMIT_BYTES),
    )(g)


def _chip_sum(g, a, c_idx, name):
    nq, r, w = g.shape
    hr = r // 2
    br = _blk(hr, 512, 16)
    nb = hr // br

    def body(c_ref, g_ref, a_ref, o_ref):
        o_ref[...] = (g_ref[...] + a_ref[...]).astype(o_ref.dtype)

    return pl.pallas_call(
        body, name=name,
        grid_spec=pltpu.PrefetchScalarGridSpec(
            num_scalar_prefetch=1, grid=(nq, nb),
            in_specs=[pl.BlockSpec((None, br, w), lambda q, i, cr: (q, cr[0] * nb + i, 0)),
                      pl.BlockSpec((None, br, w), lambda q, i, cr: (q, i, 0))],
            out_specs=pl.BlockSpec((None, br, w), lambda q, i, cr: (q, i, 0))),
        out_shape=jax.ShapeDtypeStruct((nq, hr, w), BF16),
        compiler_params=_params(("parallel", "parallel")),
    )(c_idx, g, a)


def _rs_first_hop(s4, name):
    nq, hr, w = s4.shape
    qr = hr // 2

    def body(s_ref, r_ref, send_sems, recv_sems):
        x, y, c, _ = _place()
        p = 2 * x + y
        x_nbr, y_nbr = (1 - x, y, c), (x, 1 - y, c)

        def piece(q, k):
            return s_ref.at[q, pl.ds(k * qr, qr), :]

        sends = [(piece(p ^ 2, 0), x_nbr), (piece(p ^ 3, 0), x_nbr), (piece(p ^ 1, 1), y_nbr), (piece(p ^ 3, 1), y_nbr)]
        cps = []
        for k, (src, to) in enumerate(sends):
            cp = pltpu.make_async_remote_copy(src_ref=src, dst_ref=r_ref.at[k], send_sem=send_sems.at[k],
                                              recv_sem=recv_sems.at[k], device_id=to, device_id_type=MESH)
            cp.start()
            cps.append(cp)
        for cp in cps:
            cp.wait()

    return pl.pallas_call(
        body, name=name, in_specs=[_ANY], out_specs=_ANY,
        out_shape=jax.ShapeDtypeStruct((nq, qr, w), s4.dtype),
        scratch_shapes=[pltpu.SemaphoreType.DMA((nq,)), pltpu.SemaphoreType.DMA((nq,))],
        compiler_params=pltpu.CompilerParams(vmem_limit_bytes=VMEM_LIMIT_BYTES),
    )(s4)


def _rs_middle(s4, r1, chip_idx, name):
    _, hr, w = s4.shape
    qr = hr // 2
    br = _blk(qr, 512, 16)
    nb = qr // br

    def body(idx_ref, mine_ref, theirs_ref, got_mine_ref, got_theirs_ref, own_ref, onward_ref):
        own_ref[...] = mine_ref[...].astype(F32) + got_mine_ref[...].astype(F32)
        onward_ref[...] = (theirs_ref[...].astype(F32) + got_theirs_ref[...].astype(F32)).astype(onward_ref.dtype)

    return pl.pallas_call(
        body, name=name,
        grid_spec=pltpu.PrefetchScalarGridSpec(
            num_scalar_prefetch=1, grid=(2, nb),
            in_specs=[pl.BlockSpec((None, br, w), lambda k, i, ix: (ix[0], k * nb + i, 0)),
                      pl.BlockSpec((None, br, w), lambda k, i, ix: (ix[1 + k], k * nb + i, 0)),
                      pl.BlockSpec((None, br, w), lambda k, i, ix: (2 * k, i, 0)),
                      pl.BlockSpec((None, br, w), lambda k, i, ix: (2 * k + 1, i, 0))],
            out_specs=[pl.BlockSpec((br, w), lambda k, i, ix: (k * nb + i, 0)),
                       pl.BlockSpec((None, br, w), lambda k, i, ix: (k, i, 0))]),
        out_shape=[jax.ShapeDtypeStruct((hr, w), F32), jax.ShapeDtypeStruct((2, qr, w), s4.dtype)],
        compiler_params=_params(("parallel", "parallel")),
    )(chip_idx, s4, s4, r1, r1)


def _rs_second_hop(onward, name):
    def body(u_ref, r_ref, send_sems, recv_sems):
        x, y, c, _ = _place()
        cps = []
        for k, to in enumerate([(x, 1 - y, c), (1 - x, y, c)]):
            cp = pltpu.make_async_remote_copy(src_ref=u_ref.at[k], dst_ref=r_ref.at[k], send_sem=send_sems.at[k],
                                              recv_sem=recv_sems.at[k], device_id=to, device_id_type=MESH)
            cp.start()
            cps.append(cp)
        for cp in cps:
            cp.wait()

    return pl.pallas_call(
        body, name=name, in_specs=[_ANY], out_specs=_ANY,
        out_shape=jax.ShapeDtypeStruct(onward.shape, onward.dtype),
        scratch_shapes=[pltpu.SemaphoreType.DMA((2,)), pltpu.SemaphoreType.DMA((2,))],
        compiler_params=pltpu.CompilerParams(vmem_limit_bytes=VMEM_LIMIT_BYTES),
    )(onward)


def _rs_last_add(own, r2, name):
    hr, w = own.shape
    br = _blk(hr // 2, 512, 16)

    def body(a_ref, b_ref, o_ref):
        o_ref[...] = a_ref[...] + b_ref[...].astype(F32)

    row = pl.BlockSpec((br, w), lambda i: (i, 0))
    return pl.pallas_call(
        body, name=name, grid=(hr // br,), in_specs=[row, row], out_specs=row,
        out_shape=jax.ShapeDtypeStruct((hr, w), F32),
        compiler_params=_params(("parallel",)),
    )(own, r2.reshape(hr, w))


def _sibling_swap(t, name):
    hr, w = t.shape

    def body(t_ref, o_ref, send_sem, recv_sem):
        x, y, c, _ = _place()
        cp = pltpu.make_async_remote_copy(src_ref=t_ref, dst_ref=o_ref, send_sem=send_sem, recv_sem=recv_sem,
                                          device_id=(x, y, 1 - c), device_id_type=MESH)
        cp.start()
        cp.wait()

    return pl.pallas_call(
        body, name=name, in_specs=[_ANY], out_specs=_ANY,
        out_shape=jax.ShapeDtypeStruct((hr, w), t.dtype),
        scratch_shapes=[pltpu.SemaphoreType.DMA, pltpu.SemaphoreType.DMA],
        compiler_params=pltpu.CompilerParams(vmem_limit_bytes=VMEM_LIMIT_BYTES),
    )(t)


def _all_reduce_small(v, name):
    r, w = v.shape

    def body(v_ref, o_ref, slots, send_sems, recv_sems):
        x, y, c, _ = _place()
        me = 4 * x + 2 * y + c
        slots[me] = v_ref[...]
        cps = []
        for k in range(1, N_DEV):
            fx, fy, fc = (k >> 2) & 1, (k >> 1) & 1, k & 1
            to = (x ^ fx, y ^ fy, c ^ fc)
            cp = pltpu.make_async_remote_copy(
                src_ref=v_ref, dst_ref=slots.at[me], send_sem=send_sems.at[k - 1], recv_sem=recv_sems.at[k - 1],
                device_id=to, device_id_type=MESH)
            cp.start()
            cps.append(cp)
        for k in range(1, N_DEV):
            fx, fy, fc = (k >> 2) & 1, (k >> 1) & 1, k & 1
            src_dev = 4 * (x ^ fx) + 2 * (y ^ fy) + (c ^ fc)
            pltpu.make_async_remote_copy(
                src_ref=v_ref, dst_ref=slots.at[src_dev], send_sem=send_sems.at[k - 1],
                recv_sem=recv_sems.at[k - 1], device_id=(x, y, c), device_id_type=MESH).wait_recv()
        for cp in cps:
            cp.wait_send()
        acc = slots[0]
        for d in range(1, N_DEV):
            acc = acc + slots[d]
        o_ref[...] = acc

    return pl.pallas_call(
        body, name=name,
        in_specs=[pl.BlockSpec(memory_space=pltpu.VMEM)], out_specs=pl.BlockSpec(memory_space=pltpu.VMEM),
        out_shape=jax.ShapeDtypeStruct((r, w), F32),
        scratch_shapes=[pltpu.VMEM((N_DEV, r, w), F32), pltpu.SemaphoreType.DMA((N_DEV - 1,)),
                        pltpu.SemaphoreType.DMA((N_DEV - 1,))],
        compiler_params=pltpu.CompilerParams(vmem_limit_bytes=VMEM_LIMIT_BYTES),
    )(v)


def _part_rows(shape, part_rows=PACK_PART_ROWS):
    assert shape[-1] <= PACK_LANES
    return _round_up(math.prod(shape[:-1]), part_rows)


def _packed_rows(shapes):
    return _round_up(sum(_part_rows(s) for s in shapes), PACK_ROWS_MULT)


def _pack(arrs, total_rows, dtype, part_rows=PACK_PART_ROWS):
    parts = []
    for a in arrs:
        a2 = a.reshape(-1, a.shape[-1]).astype(dtype)
        rows = _part_rows(a.shape, part_rows)
        parts.append(jnp.pad(a2, ((0, rows - a2.shape[0]), (0, PACK_LANES - a2.shape[1]))))
    used = sum(p.shape[0] for p in parts)
    if total_rows > used:
        parts.append(jnp.zeros((total_rows - used, PACK_LANES), dtype))
    return jnp.concatenate(parts, axis=0)


def _unpack(packed, shapes, part_rows=PACK_PART_ROWS):
    out, r0 = [], 0
    for s in shapes:
        out.append(packed[r0:r0 + math.prod(s[:-1]), :s[-1]].reshape(s))
        r0 += _part_rows(s, part_rows)
    return out


_BIG = (("fox_w_in", 2), ("fox_w_out", 1), ("mla_w_kv_a", 0), ("mla_w_kv_b", 1), ("mla_w_q_a", 1),
        ("mla_w_q_b", 2), ("mla_w_out", 1), ("ffn_w_up", 2), ("ffn_w_down", 1))
_SMALL = ("norm_mix_g", "norm_ffn_g", "fox_b_f", "kv_norm_g", "mla_kv_a_norm_g", "mla_q_a_norm_g", "final_norm_g")
_WEIGHTS = ("norm_mix_g", "norm_ffn_g", "fox_w_in", "fox_b_f", "fox_w_out", "kv_norm_g", "mla_w_kv_a",
            "mla_kv_a_norm_g", "mla_w_kv_b", "mla_w_q_a", "mla_q_a_norm_g", "mla_w_q_b", "mla_w_out",
            "ffn_w_up", "ffn_w_down", "final_norm_g")


def _ffn_fwd(x, h, w_up, w_down, tag):
    def relu_sq(acc):
        r = jnp.maximum(acc, 0.0)
        return r, r * r

    r, a = _matmul(h, w_up, mode="nn", out_dtypes=(BF16, BF16), epilogue=relu_sq, name=f"{tag}_up")
    x_out = _matmul(a, w_down, mode="nn", out_dtypes=(F32,), epilogue=lambda acc, res: (acc + res,),
                    extras=(x,), name=f"{tag}_down")
    return x_out, r, a


def _ffn_bwd(dx_out, x_in, h, r, a, g_norm, w_up, w_down, tag):
    d_u = _matmul(dx_out, w_down, mode="nt", out_dtypes=(BF16,), epilogue=lambda acc, rr: (acc * (2.0 * rr.astype(F32)),),
                  extras=(r,), name=f"{tag}_d_act")
    d_w_down = _matmul(a, dx_out, mode="tn", out_dtypes=(F32,), name=f"{tag}_d_w_down")
    d_w_up = _matmul(h, d_u, mode="tn", out_dtypes=(F32,), by_chip=True, name=f"{tag}_d_w_up")
    d_h = _matmul(d_u, w_up, mode="nt", out_dtypes=(F32,), name=f"{tag}_d_h")
    dx_in, (d_g,) = _rms_bwd(x_in, [(g_norm, d_h)], dx_out, name=f"{tag}_d_norm")
    return dx_in, d_w_up, d_w_down, d_g


def kernel(x, norm_mix_g, norm_ffn_g, fox_w_in, fox_b_f, fox_w_out, kv_norm_g, mla_w_kv_a, mla_kv_a_norm_g, mla_w_kv_b, mla_w_q_a, mla_q_a_norm_g, mla_w_q_b, mla_w_out, ffn_w_up, ffn_w_down, final_norm_g, loss_target, m_norm_mix_g, m_norm_ffn_g, m_fox_w_in, m_fox_b_f, m_fox_w_out, m_kv_norm_g, m_mla_w_kv_a, m_mla_kv_a_norm_g, m_mla_w_kv_b, m_mla_w_q_a, m_mla_q_a_norm_g, m_mla_w_q_b, m_mla_w_out, m_ffn_w_up, m_ffn_w_down, m_final_norm_g, v_norm_mix_g, v_norm_ffn_g, v_fox_w_in, v_fox_b_f, v_fox_w_out, v_kv_norm_g, v_mla_w_kv_a, v_mla_kv_a_norm_g, v_mla_w_kv_b, v_mla_w_q_a, v_mla_q_a_norm_g, v_mla_w_q_b, v_mla_w_out, v_ffn_w_up, v_ffn_w_down, v_final_norm_g):
    args = dict(locals())
    w_in = {n: args[n] for n in _WEIGHTS}
    m_in = {n: args["m_" + n] for n in _WEIGHTS}
    v_in = {n: args["v_" + n] for n in _WEIGHTS}

    xs = x[0]
    seq, d_model = xs.shape
    tgt = loss_target[0]
    fox_h, mla_h, nope = FOX_HEADS, MLA_HEADS, QK_NOPE_DIM
    kv_rank = mla_kv_a_norm_g.shape[0]
    rope = mla_w_kv_a.shape[1] - kv_rank
    half = rope // 2
    q_rank = mla_q_a_norm_g.shape[1]
    v_dim = mla_w_kv_b.shape[1] * N_CHIPS // mla_h - nope
    fox_w = fox_w_out.shape[1] * N_CHIPS
    fox_dh = fox_w // fox_h

    big_names = [n for n, _ in _BIG]
    shard_shapes = [w_in[n].shape for n in big_names]
    rows = _packed_rows(shard_shapes)
    my_shard = _pack([w_in[n] for n in big_names], rows, BF16)
    others = _all_gather_shards(my_shard, name="gather_weights")
    by_relation = jnp.concatenate([my_shard[None], others], axis=0)
    p_chip = 2 * lax.axis_index("x") + lax.axis_index("y")
    full = {}
    for q in range(N_CHIPS):
        shard_q = lax.dynamic_index_in_dim(by_relation, p_chip ^ q, axis=0, keepdims=False)
        for (n, ax), piece in zip(_BIG, _unpack(shard_q, shard_shapes)):
            full.setdefault(n, []).append(piece)
    full = {n: jnp.concatenate(full[n], axis=ax) for n, ax in _BIG}

    fox_scale = fox_dh ** -0.5
    fox_wd = _round_up(fox_dh + 9, LANE_TILE)
    fox_vwd = _round_up(fox_dh + 4, LANE_TILE)
    w_fox_in = full["fox_w_in"][0]
    w_fq = _pad_heads(w_fox_in[:, :fox_w] * fox_scale, fox_h, fox_wd, 1)
    w_fk = _pad_heads(w_fox_in[:, fox_w:2 * fox_w], fox_h, fox_wd, 1)
    w_fv = _pad_heads(w_fox_in[:, 2 * fox_w:3 * fox_w], fox_h, fox_vwd, 1)
    w_gate = w_fox_in[:, 3 * fox_w:]
    w_fox_out = _pad_heads(full["fox_w_out"][0], fox_h, fox_vwd, 0)
    n_cx = _round_up(3 * fox_h + 1, LANE_TILE)
    c_piece = lambda i: (lambda hh: 3 * hh + i)
    one_col = 3 * fox_h
    e_fq = _placement(n_cx, fox_h, fox_wd, [(c_piece(i), fox_dh + i, 1.0) for i in range(3)]
                      + [(one_col, fox_dh + 3 + i, 1.0) for i in range(3)])
    e_fk = _placement(n_cx, fox_h, fox_wd, [(one_col, fox_dh + i, 1.0) for i in range(3)]
                      + [(c_piece(i), fox_dh + 3 + i, -1.0) for i in range(3)]
                      + [(one_col, fox_dh + 6 + i, 1.0) for i in range(3)])
    e_fv = _placement(n_cx, fox_h, fox_vwd, [(one_col, fox_dh + i, -1.0) for i in range(3)]
                      + [(one_col, fox_dh + 3, 1.0)])

    mla_scale = (nope + rope) ** -0.5
    mla_dk = nope + rope
    mla_wd = _round_up(mla_dk + 3, LANE_TILE)
    mla_vwd = _round_up(v_dim + 4, LANE_TILE)
    w_kv_a = full["mla_w_kv_a"]
    w_kv_b3 = full["mla_w_kv_b"].reshape(kv_rank, mla_h, nope + v_dim)
    w_kn = _pad_heads(w_kv_b3[:, :, :nope].reshape(kv_rank, -1), mla_h, mla_wd, 1)
    w_mv = _pad_heads(w_kv_b3[:, :, nope:].reshape(kv_rank, -1), mla_h, mla_vwd, 1)
    w_q_a = full["mla_w_q_a"][0]
    w_q_b3 = full["mla_w_q_b"][0].reshape(q_rank, mla_h, nope + rope)
    w_qa_ = _pad_heads(w_q_b3.reshape(q_rank, -1), mla_h, mla_wd, 1)
    w_qb_ = _pad_heads(jnp.concatenate([jnp.zeros_like(w_q_b3[:, :, :nope]), -w_q_b3[:, :, nope + half:],
                                        w_q_b3[:, :, nope:nope + half]], axis=-1).reshape(q_rank, -1),
                       mla_h, mla_wd, 1)
    w_mla_out = _pad_heads(full["mla_w_out"][0], mla_h, mla_vwd, 0)
    w_up, w_down = full["ffn_w_up"], full["ffn_w_down"]
    n_kx = _round_up(rope + 1, LANE_TILE)
    e_mk = _placement(n_kx, mla_h, mla_wd, [(j, nope + j, 1.0) for j in range(rope)]
                      + [(rope, mla_dk + i, 1.0) for i in range(3)])
    e_mv = _placement(n_kx, mla_h, mla_vwd, [(rope, v_dim + i, -1.0) for i in range(3)] + [(rope, v_dim + 3, 1.0)])
    e_kr_u = _placement(n_kx, mla_h, mla_wd, [(j, nope + j, 1.0) for j in range(rope)]).T
    e_kr_v = _placement(n_kx, mla_h, mla_wd, [(j, nope + half + j, 1.0) for j in range(half)]
                        + [(half + j, nope + j, -1.0) for j in range(half)]).T

    inv = 1.0 / (ROPE_BASE ** (jnp.arange(0, rope, 2, dtype=F32) / rope))
    ang = jnp.arange(seq, dtype=F32)[:, None] * inv[None, :]
    cos, sin = jnp.cos(ang), jnp.sin(ang)
    pad_t = jnp.zeros((seq, mla_wd - mla_dk), F32)
    cos_t = jnp.concatenate([jnp.ones((seq, nope), F32), cos, cos, pad_t], axis=1)
    sin_t = jnp.concatenate([jnp.zeros((seq, nope), F32), sin, sin, pad_t], axis=1)
    pad_k = jnp.zeros((seq, n_kx - rope), F32)
    cos_k, sin_k = jnp.concatenate([cos, cos, pad_k], axis=1), jnp.concatenate([sin, sin, pad_k], axis=1)

    (h0,) = _rms_fwd(xs, norm_mix_g[0:1], name="l0_norm_mix")
    gate = _matmul(h0, w_gate, mode="nn", out_dtypes=(F32,), name="fox_gate")
    z = gate + fox_b_f[0][None, :]
    cum = jnp.cumsum(jax.nn.log_sigmoid(z), axis=0)
    cx = jnp.concatenate([_split3(cum).reshape(seq, 3 * fox_h), jnp.ones((seq, 1), BF16),
                          jnp.zeros((seq, n_cx - 3 * fox_h - 1), BF16)], axis=1)
    fqa = _matmul(h0, w_fq, mode="nn", out_dtypes=(BF16,), placed=(cx, e_fq), name="fox_q")
    fka = _matmul(h0, w_fk, mode="nn", out_dtypes=(BF16,), placed=(cx, e_fk), name="fox_k")
    fva = _matmul(h0, w_fv, mode="nn", out_dtypes=(BF16,), placed=(cx, e_fv), name="fox_v")
    foa, fqb = _flash_fwd(fqa, fka, fva, fox_h, fox_dh + 3, fox_dh + 6, FOX_FWD_SUB_ROWS, name="fox_attn")
    add_res = lambda acc, res: (acc + res,)
    x1 = _matmul(foa, w_fox_out, mode="nn", out_dtypes=(F32,), epilogue=add_res, extras=(xs,), name="fox_out")
    (h1,) = _rms_fwd(x1, norm_ffn_g[0:1], name="l0_norm_ffn")
    x2, r0, a0 = _ffn_fwd(x1, h1, w_up[0], w_down[0], "ffn0")

    src, h2 = _rms_fwd(x2, jnp.stack([kv_norm_g, norm_mix_g[1]]), name="l1_norm_kv_mix")
    kv_a = _matmul(src, w_kv_a, mode="nn", out_dtypes=(F32,), name="mla_kv_a")
    (c_kv,) = _rms_fwd(kv_a, mla_kv_a_norm_g[None, :], name="mla_norm_kv_a")
    kr1, kr2 = _rope(kv_a[None, :, kv_rank:kv_rank + half], kv_a[None, :, kv_rank + half:], cos, sin, 1.0,
                     name="mla_rope_k")
    krx = jnp.concatenate([kr1.astype(BF16), kr2.astype(BF16), jnp.ones((seq, 1), BF16),
                           jnp.zeros((seq, n_kx - rope - 1), BF16)], axis=1)
    mka = _matmul(c_kv, w_kn, mode="nn", out_dtypes=(BF16,), placed=(krx, e_mk), name="mla_k")
    mva = _matmul(c_kv, w_mv, mode="nn", out_dtypes=(BF16,), placed=(krx, e_mv), name="mla_v")
    cq_pre = _matmul(h2, w_q_a, mode="nn", out_dtypes=(F32,), name="mla_q_a")
    (c_q,) = _rms_fwd(cq_pre, mla_q_a_norm_g, name="mla_norm_q_a")
    mqa = _rope_proj(c_q, w_qa_, w_qb_, cos_t, sin_t, mla_scale, mla_h, name="mla_q_b_rope")
    moa, mqb = _flash_fwd(mqa, mka, mva, mla_h, v_dim + 3, mla_dk, MLA_FWD_SUB_ROWS, name="mla_attn")
    x3 = _matmul(moa, w_mla_out, mode="nn", out_dtypes=(F32,), epilogue=add_res, extras=(x2,), name="mla_out")
    (h3,) = _rms_fwd(x3, norm_ffn_g[1:2], name="l1_norm_ffn")
    x4, r1, a1 = _ffn_fwd(x3, h3, w_up[1], w_down[1], "ffn1")

    loss_tile, dx4, d_final_g = _loss_head(x4, final_norm_g[None, :], tgt, name="loss_head")
    loss = lax.psum(loss_tile[0, 0], ("x", "y", "c"))

    gw = {}
    dx3, d_up1, d_down1, d_nf1 = _ffn_bwd(dx4, x3, h3, r1, a1, norm_ffn_g[1:2], w_up[1], w_down[1], "ffn1")

    d_moa = _matmul(dx3, w_mla_out, mode="nt", out_dtypes=(BF16,), epilogue=_delta_epilogue(mla_vwd, v_dim),
                    extras=(moa,), name="mla_d_ctx")
    gw["mla_w_out"] = _unpad_heads(_matmul(moa, dx3, mode="tn", out_dtypes=(F32,), name="mla_d_w_out"),
                                   mla_h, v_dim, 0)[None]
    d_mqa, d_mka, d_mva = _flash_bwd(mqb, mka, mva, d_moa, mla_h, MLA_BWD_HEADS_PER_STEP,
                                     name="mla_attn_bwd")
    d_c_q, d_w_qa_, d_w_qb_ = _rope_proj_bwd(c_q, d_mqa, w_qa_, w_qb_, cos_t, sin_t, mla_scale, mla_h,
                                             name="mla_q_b_rope_bwd")
    d_w_qa_ = _unpad_heads(d_w_qa_, mla_h, mla_dk, 1).reshape(q_rank, mla_h, mla_dk)
    d_w_qb_ = _unpad_heads(d_w_qb_, mla_h, mla_dk, 1).reshape(q_rank, mla_h, mla_dk)
    gw["mla_w_q_b"] = jnp.concatenate(
        [d_w_qa_[:, :, :nope], d_w_qa_[:, :, nope:nope + half] + d_w_qb_[:, :, nope + half:],
         d_w_qa_[:, :, nope + half:] - d_w_qb_[:, :, nope:nope + half]], axis=-1).reshape(1, q_rank, mla_h * mla_dk)
    d_cq_pre, (d_q_a_g,) = _rms_bwd(cq_pre, [(mla_q_a_norm_g, d_c_q)], None, name="mla_d_norm_q_a")
    gw["mla_w_q_a"] = _matmul(h2, d_cq_pre, mode="tn", out_dtypes=(F32,), name="mla_d_w_q_a")[None]
    d_h2 = _matmul(d_cq_pre, w_q_a, mode="nt", out_dtypes=(F32,), name="mla_d_h")

    d_w_kn = _unpad_heads(_matmul(c_kv, d_mka, mode="tn", out_dtypes=(F32,), name="mla_d_w_k"), mla_h, nope, 1)
    d_w_mv = _unpad_heads(_matmul(c_kv, d_mva, mode="tn", out_dtypes=(F32,), name="mla_d_w_v"), mla_h, v_dim, 1)
    gw["mla_w_kv_b"] = jnp.concatenate([d_w_kn.reshape(kv_rank, mla_h, nope), d_w_mv.reshape(kv_rank, mla_h, v_dim)],
                                       axis=-1).reshape(kv_rank, mla_h * (nope + v_dim))
    d_c_kv_v = _matmul(d_mva, w_mv, mode="nt", out_dtypes=(F32,), name="mla_d_c_kv_v")
    d_c_kv = _matmul(d_mka, w_kn, mode="nt", out_dtypes=(F32,), epilogue=add_res, extras=(d_c_kv_v,),
                     name="mla_d_c_kv")
    d_ckv_pre, (d_kv_a_g,) = _rms_bwd(kv_a, [(mla_kv_a_norm_g[None, :], d_c_kv)], None, name="mla_d_norm_kv_a")
    d_kr_u = _matmul(d_mka, e_kr_u, mode="nn", out_dtypes=(F32,), name="mla_d_k_rope_u")
    d_kr_v = _matmul(d_mka, e_kr_v, mode="nn", out_dtypes=(F32,), name="mla_d_k_rope_v")
    d_kr = _rope_mix(d_kr_u, d_kr_v, cos_k, sin_k, 1.0, 1, name="mla_rope_dk")
    d_kv_a = jnp.concatenate([d_ckv_pre, d_kr[:, :rope].astype(F32)], axis=1)
    gw["mla_w_kv_a"] = _matmul(src, d_kv_a, mode="tn", out_dtypes=(F32,), name="mla_d_w_kv_a")
    d_src = _matmul(d_kv_a, w_kv_a, mode="nt", out_dtypes=(F32,), name="mla_d_src")
    dx2, (d_kv_g, d_nm1) = _rms_bwd(x2, [(kv_norm_g[None, :], d_src), (norm_mix_g[1:2], d_h2)], dx3,
                                    name="l1_d_norm_kv_mix")

    dx1, d_up0, d_down0, d_nf0 = _ffn_bwd(dx2, x1, h1, r0, a0, norm_ffn_g[0:1], w_up[0], w_down[0], "ffn0")
    by_rows = lambda g: g.reshape(N_CHIPS, g.shape[0] // N_CHIPS, g.shape[1])
    gw_by_chip = {"ffn_w_up": jnp.concatenate([d_up0, d_up1], axis=1),
                  "ffn_w_down": jnp.concatenate([by_rows(d_down0), by_rows(d_down1)], axis=1)}

    d_foa = _matmul(dx1, w_fox_out, mode="nt", out_dtypes=(BF16,), epilogue=_delta_epilogue(fox_vwd, fox_dh),
                    extras=(foa,), name="fox_d_ctx")
    gw["fox_w_out"] = _unpad_heads(_matmul(foa, dx1, mode="tn", out_dtypes=(F32,), name="fox_d_w_out"),
                                   fox_h, fox_dh, 0)[None]
    fox_hps = FOX_BWD_HEADS_PER_STEP if fox_h % FOX_BWD_HEADS_PER_STEP == 0 else 1
    d_fqa, d_fka, d_fva, ds_rows, ds_cols = _flash_bwd(fqb, fka, fva, d_foa, fox_h, fox_hps, name="fox_attn_bwd",
                                                       sum_cols=(fox_dh, fox_dh + 3))
    d_cum = jnp.transpose(ds_rows - ds_cols, (1, 0, 2)).reshape(seq, fox_h)
    d_z = lax.cumsum(d_cum, axis=0, reverse=True) * jax.nn.sigmoid(-z)
    d_b_f = jnp.sum(d_z, axis=0)
    d_w_in = [_unpad_heads(_matmul(h0, g, mode="tn", out_dtypes=(F32,), name=f"fox_d_w_{tag}"), fox_h, fox_dh, 1)
              for tag, g in (("q", d_fqa), ("k", d_fka), ("v", d_fva))]
    d_w_gate = _matmul(h0, d_z, mode="tn", out_dtypes=(F32,), name="fox_d_w_gate")
    gw["fox_w_in"] = jnp.concatenate([d_w_in[0] * fox_scale, d_w_in[1], d_w_in[2], d_w_gate], axis=1)[None]
    d_h0 = _matmul(d_z, w_gate, mode="nt", out_dtypes=(F32,), name="fox_d_h_gate")
    for tag, g, w in (("q", d_fqa, w_fq), ("k", d_fka, w_fk), ("v", d_fva, w_fv)):
        d_h0 = _matmul(g, w, mode="nt", out_dtypes=(F32,), epilogue=add_res, extras=(d_h0,), name=f"fox_d_h_{tag}")
    grad_x, (d_nm0,) = _rms_bwd(xs, [(norm_mix_g[0:1], d_h0)], dx1, name="l0_d_norm_mix")

    c_idx = lax.axis_index("c").astype(jnp.int32).reshape(1)
    parts = []
    for (n, ax), shape in zip(_BIG, shard_shapes):
        if n in gw_by_chip:
            g = gw_by_chip[n]
        else:
            g = gw[n]
            g = jnp.moveaxis(g.reshape(g.shape[:ax] + (N_CHIPS, shape[ax]) + g.shape[ax + 1:]), ax, 0)
            g = g.reshape(N_CHIPS, -1, shape[-1])
        parts.append(jnp.pad(g, ((0, 0), (0, _part_rows(shape) - g.shape[1]), (0, PACK_LANES - shape[-1]))))
    parts.append(jnp.zeros((N_CHIPS, rows - sum(p.shape[1] for p in parts), PACK_LANES), F32))
    g4 = jnp.concatenate(parts, axis=1)
    a4 = _sibling_swap_halves(g4, name="grads_to_sibling")
    s4 = _chip_sum(g4, a4, c_idx, name="grads_chip_sum")
    chip_idx = jnp.stack([p_chip, p_chip ^ 1, p_chip ^ 2]).astype(jnp.int32)
    r1 = _rs_first_hop(s4, name="grads_first_hop")
    t_own, onward = _rs_middle(s4, r1, chip_idx, name="grads_middle_sum")
    r2 = _rs_second_hop(onward, name="grads_second_hop")
    t_mine = _rs_last_add(t_own, r2, name="grads_last_sum")
    t_theirs = _sibling_swap(t_mine, name="grads_join_halves")
    is_south = lax.axis_index("c") == 0
    g_big = jnp.concatenate([jnp.where(is_south, t_mine, t_theirs), jnp.where(is_south, t_theirs, t_mine)],
                            axis=0)

    small_local = {"norm_mix_g": jnp.concatenate([d_nm0, d_nm1], axis=0),
                   "norm_ffn_g": jnp.concatenate([d_nf0, d_nf1], axis=0),
                   "fox_b_f": d_b_f[None, :], "kv_norm_g": d_kv_g[0], "mla_kv_a_norm_g": d_kv_a_g[0],
                   "mla_q_a_norm_g": d_q_a_g, "final_norm_g": d_final_g[0]}
    small_shapes = [w_in[n].shape for n in _SMALL]
    small_rows = sum(_part_rows(s, SMALL_PART_ROWS) for s in small_shapes)
    pack_small = lambda arrs: _pack(arrs, small_rows, F32, SMALL_PART_ROWS)
    g_small = _all_reduce_small(pack_small([small_local[n] for n in _SMALL]), name="grads_small")

    grads = dict(zip(big_names, _unpack(g_big, shard_shapes)))
    delta, new_m, new_v = {}, {}, {}
    for n, shape in zip(big_names, shard_shapes):
        flat = lambda a: a.reshape(-1, shape[-1])
        outs = _adamw(flat(w_in[n]), flat(grads[n]), flat(m_in[n]), flat(v_in[n]), name=f"adamw_{n}")
        delta[n], new_m[n], new_v[n] = (o.reshape(shape) for o in outs)
    sm_outs = _adamw(pack_small([w_in[n] for n in _SMALL]), g_small, pack_small([m_in[n] for n in _SMALL]),
                     pack_small([v_in[n] for n in _SMALL]), name="adamw_small")
    for res, packed in zip((grads, delta, new_m, new_v), (g_small,) + tuple(sm_outs)):
        res.update(zip(_SMALL, _unpack(packed, small_shapes, SMALL_PART_ROWS)))

    return (loss, grad_x[None], *[grads[n] for n in _WEIGHTS], *[delta[n] for n in _WEIGHTS],
            *[new_m[n] for n in _WEIGHTS], *[new_v[n] for n in _WEIGHTS])
```

```python
import math

import numpy as np
import jax
import jax.numpy as jnp
from jax import lax
from jax.experimental import pallas as pl
from jax.experimental.pallas import tpu as pltpu

F32 = jnp.float32
BF16 = jnp.bfloat16

FOX_HEADS = 16
MLA_HEADS = 8
QK_NOPE_DIM = 128
ROPE_BASE = 10000.0
EPS = 1e-6

ADAM_LR = 0.001
ADAM_B1 = 0.9
ADAM_B2 = 0.999
ADAM_EPS = 1e-08
ADAM_WD = 0.01
ADAM_STEP = 10

N_CHIPS = 4
N_DEV = 8
PACK_LANES = 1024
PACK_PART_ROWS = 16
SMALL_PART_ROWS = 8
PACK_ROWS_MULT = 1024
VMEM_LIMIT_BYTES = 48 * 1024 * 1024
LANE_TILE = 128
MATMUL_BLOCK = 1024
MATMUL_WIDE_BLOCK = 2048
MATMUL_DEPTH = 2048
ATTN_BLOCK_Q = 1024
ATTN_BLOCK_K = 1024
ATTN_FWD_LANES = 1024
FOX_BWD_HEADS_PER_STEP = 4
MLA_BWD_HEADS_PER_STEP = 2
ATTN_SUB_ROWS = 256
FOX_FWD_SUB_ROWS = (1024, 512)
MLA_FWD_SUB_ROWS = (256, 256)
NEG_BIG = -1e30
MESH = pl.DeviceIdType.MESH


def _round_up(n, m):
    return -(-n // m) * m


def _blk(dim, pref, mult=128):
    if dim <= pref:
        return dim
    b = (pref // mult) * mult
    while b >= mult:
        if dim % b == 0:
            return b
        b -= mult
    return dim


def _params(sem=None):
    return pltpu.CompilerParams(dimension_semantics=sem, vmem_limit_bytes=VMEM_LIMIT_BYTES)


_DIMS = {"nn": (((1,), (0,)), ((), ())), "nt": (((1,), (1,)), ((), ())), "tn": (((0,), (0,)), ((), ()))}


def _matmul(a, b, *, mode, out_dtypes, name, epilogue=None, extras=(), placed=None, by_chip=False):
    if mode == "tn":
        kdim, m = a.shape
    else:
        m, kdim = a.shape
    n = b.shape[0] if mode == "nt" else b.shape[1]
    bm, bk = _blk(m, MATMUL_BLOCK), _blk(kdim, MATMUL_DEPTH)
    bn = _blk(n, MATMUL_WIDE_BLOCK if (mode != "tn" and kdim <= MATMUL_BLOCK) else MATMUL_BLOCK)
    if by_chip:
        bn = _blk(n // N_CHIPS, bn)
    nk = kdim // bk
    n_extra, n_out = len(extras), len(out_dtypes)
    n_placed = 0 if placed is None else 2
    dims = _DIMS[mode]

    def body(a_ref, b_ref, *rest):
        placed_refs = rest[:n_placed]
        rest = rest[n_placed:]
        extra_refs = rest[:n_extra]
        out_refs = rest[n_extra:n_extra + n_out]

        def finish(acc):
            if n_placed:
                acc = acc + lax.dot_general(placed_refs[0][...], placed_refs[1][...], _DIMS["nn"],
                                            preferred_element_type=F32)
            res = (acc,) if epilogue is None else epilogue(acc, *[r[...] for r in extra_refs])
            for o_ref, r in zip(out_refs, res):
                o_ref[...] = r.astype(o_ref.dtype)

        part = lax.dot_general(a_ref[...].astype(BF16), b_ref[...].astype(BF16), dims, preferred_element_type=F32)
        if nk == 1:
            finish(part)
            return
        acc_ref = rest[n_extra + n_out]
        k = pl.program_id(2)

        @pl.when(k == 0)
        def _():
            acc_ref[...] = part

        @pl.when((k > 0) & (k < nk - 1))
        def _():
            acc_ref[...] += part

        @pl.when(k == nk - 1)
        def _():
            finish(acc_ref[...] + part)

    if mode == "tn":
        a_spec = pl.BlockSpec((bk, bm), lambda i, j, k: (k, i))
    else:
        a_spec = pl.BlockSpec((bm, bk), lambda i, j, k: (i, k))
    if mode == "nt":
        b_spec = pl.BlockSpec((bn, bk), lambda i, j, k: (j, k))
    else:
        b_spec = pl.BlockSpec((bk, bn), lambda i, j, k: (k, j))
    tile = pl.BlockSpec((bm, bn), lambda i, j, k: (i, j))
    placed_specs = []
    if n_placed:
        k2 = placed[0].shape[1]
        placed_specs = [pl.BlockSpec((bm, k2), lambda i, j, k: (i, 0)), pl.BlockSpec((k2, bn), lambda i, j, k: (0, j))]
    out_tile, out_dims = tile, (m, n)
    if by_chip:
        per_chip = n // N_CHIPS // bn
        out_tile = pl.BlockSpec((None, bm, bn), lambda i, j, k: (j // per_chip, i, j % per_chip))
        out_dims = (N_CHIPS, m, n // N_CHIPS)
    outs = pl.pallas_call(
        body, name=name,
        grid=(m // bm, n // bn, nk),
        in_specs=[a_spec, b_spec] + placed_specs + [tile] * n_extra,
        out_specs=[out_tile] * n_out,
        out_shape=[jax.ShapeDtypeStruct(out_dims, dt) for dt in out_dtypes],
        scratch_shapes=[pltpu.VMEM((bm, bn), F32)] if nk > 1 else [],
        compiler_params=_params(("parallel", "parallel", "arbitrary")),
    )(a, b, *(placed or ()), *extras)
    return outs[0] if n_out == 1 else outs


def _rms_fwd(x, gains, name):
    s = x.shape[0]
    g, w = gains.shape
    bs = _blk(s, 512, 8)

    def body(x_ref, g_ref, *out_refs):
        xv = x_ref[...]
        y = xv * lax.rsqrt(jnp.mean(xv * xv, axis=-1, keepdims=True) + EPS)
        for i, o_ref in enumerate(out_refs):
            o_ref[...] = (y * g_ref[i:i + 1, :]).astype(o_ref.dtype)

    row = pl.BlockSpec((bs, w), lambda i: (i, 0))
    outs = pl.pallas_call(
        body, name=name, grid=(s // bs,),
        in_specs=[row, pl.BlockSpec((g, w), lambda i: (0, 0))],
        out_specs=[row] * g,
        out_shape=[jax.ShapeDtypeStruct((s, w), BF16)] * g,
        compiler_params=_params(("parallel",)),
    )(x, gains)
    return outs


def _rms_bwd(x, branches, resid, name):
    s = x.shape[0]
    w = branches[0][0].shape[1]
    nb = len(branches)
    bs = _blk(s, 512, 8)
    has_resid = resid is not None

    def body(x_ref, *rest):
        g_refs = rest[:nb]
        dy_refs = rest[nb:2 * nb]
        pos = 2 * nb
        r_ref = rest[pos] if has_resid else None
        pos += int(has_resid)
        dx_ref = rest[pos]
        dg_refs = rest[pos + 1:pos + 1 + nb]
        i = pl.program_id(0)

        @pl.when(i == 0)
        def _():
            for dg_ref in dg_refs:
                dg_ref[...] = jnp.zeros_like(dg_ref)

        xv = x_ref[...]
        rstd = lax.rsqrt(jnp.mean(xv * xv, axis=-1, keepdims=True) + EPS)
        xhat = xv * rstd
        dx = r_ref[...] if has_resid else jnp.zeros_like(xv)
        for g_ref, dy_ref, dg_ref in zip(g_refs, dy_refs, dg_refs):
            dy = dy_ref[...].astype(F32)
            dyg = dy * g_ref[...]
            dx = dx + rstd * (dyg - xhat * jnp.mean(dyg * xhat, axis=-1, keepdims=True))
            dg_ref[...] += jnp.sum(dy * xhat, axis=0, keepdims=True)
        dx_ref[...] = dx

    row = pl.BlockSpec((bs, w), lambda i: (i, 0))
    vec = pl.BlockSpec((1, w), lambda i: (0, 0))
    args = [x] + [g for g, _ in branches] + [dy for _, dy in branches] + ([resid] if has_resid else [])
    outs = pl.pallas_call(
        body, name=name, grid=(s // bs,),
        in_specs=[row] + [vec] * nb + [row] * nb + ([row] if has_resid else []),
        out_specs=[row] + [vec] * nb,
        out_shape=[jax.ShapeDtypeStruct((s, w), F32)] + [jax.ShapeDtypeStruct((1, w), F32)] * nb,
        compiler_params=_params(("arbitrary",)),
    )(*args)
    return outs[0], list(outs[1:])


def _loss_head(x, g, target, name):
    s, w = x.shape
    bs = _blk(s, 512, 8)

    def body(x_ref, g_ref, t_ref, loss_ref, dx_ref, dg_ref):
        i = pl.program_id(0)

        @pl.when(i == 0)
        def _():
            loss_ref[...] = jnp.zeros_like(loss_ref)
            dg_ref[...] = jnp.zeros_like(dg_ref)

        xv = x_ref[...]
        gv = g_ref[...]
        rstd = lax.rsqrt(jnp.mean(xv * xv, axis=-1, keepdims=True) + EPS)
        xhat = xv * rstd
        err = xhat * gv - t_ref[...]
        loss_ref[...] += 0.5 * jnp.sum(jnp.mean(err * err, axis=-1, keepdims=True))
        dy = err * (1.0 / w)
        dyg = dy * gv
        dx_ref[...] = rstd * (dyg - xhat * jnp.mean(dyg * xhat, axis=-1, keepdims=True))
        dg_ref[...] += jnp.sum(dy * xhat, axis=0, keepdims=True)

    row = pl.BlockSpec((bs, w), lambda i: (i, 0))
    vec = pl.BlockSpec((1, w), lambda i: (0, 0))
    return pl.pallas_call(
        body, name=name, grid=(s // bs,),
        in_specs=[row, vec, row],
        out_specs=[pl.BlockSpec((8, 128), lambda i: (0, 0)), row, vec],
        out_shape=[jax.ShapeDtypeStruct((8, 128), F32), jax.ShapeDtypeStruct((s, w), F32),
                   jax.ShapeDtypeStruct((1, w), F32)],
        compiler_params=_params(("arbitrary",)),
    )(x, g, target)


def _rope(a, b, cos, sin, sign, name):
    g, s, w = a.shape
    bs = _blk(s, 1024, 8)

    def body(a_ref, b_ref, c_ref, s_ref, o1_ref, o2_ref):
        av = jnp.sum(a_ref[...].astype(F32), axis=0)
        bv = jnp.sum(b_ref[...].astype(F32), axis=0)
        cv, sv = c_ref[...], s_ref[...] * sign
        o1_ref[...] = av * cv - bv * sv
        o2_ref[...] = bv * cv + av * sv

    grp = pl.BlockSpec((g, bs, w), lambda i: (0, i, 0))
    row = pl.BlockSpec((bs, w), lambda i: (i, 0))
    return pl.pallas_call(
        body, name=name, grid=(s // bs,),
        in_specs=[grp, grp, row, row], out_specs=[row, row],
        out_shape=[jax.ShapeDtypeStruct((s, w), F32)] * 2,
        compiler_params=_params(("parallel",)),
    )(a, b, cos, sin)


def _causal_table(s, bq, bk, q_major):
    nq, nk = s // bq, s // bk
    rows = []
    if q_major:
        for qi in range(nq):
            kmax = (qi * bq + bq - 1) // bk
            for ki in range(kmax + 1):
                rows.append((qi, ki, int(ki * bk + bk - 1 > qi * bq), int(ki == 0), int(ki == kmax)))
    else:
        for ki in range(nk):
            qmin = (ki * bk) // bq
            for qi in range(qmin, nq):
                rows.append((qi, ki, int(ki * bk + bk - 1 > qi * bq), int(qi == qmin), int(qi == nq - 1)))
    return jnp.asarray(np.array(rows, np.int32).T)


def _causal_keep(q0, k0, nq, nk, transposed):
    if transposed:
        kpos = k0 + lax.broadcasted_iota(jnp.int32, (nk, nq), 0)
        qpos = q0 + lax.broadcasted_iota(jnp.int32, (nk, nq), 1)
    else:
        qpos = q0 + lax.broadcasted_iota(jnp.int32, (nq, nk), 0)
        kpos = k0 + lax.broadcasted_iota(jnp.int32, (nq, nk), 1)
    return kpos <= qpos


def _sub_tiles(n_rows, n_cols, masked, square, rows_are_keys, sub_rows):
    sub = min(sub_rows, n_rows)
    out = []
    for r0 in range(0, n_rows, sub):
        if masked and square:
            c0, nc = (r0, n_cols - r0) if rows_are_keys else (0, r0 + sub)
        else:
            c0, nc = 0, n_cols
        out.append((r0, sub, c0, nc))
    return out


_NT = (((1,), (1,)), ((), ()))
_NN = (((1,), (0,)), ((), ()))


def _attn_specs(bq, bk):
    qspec = lambda d: pl.BlockSpec((bq, d), lambda hh, t, tb: (tb[0, t], hh))
    kspec = lambda d: pl.BlockSpec((bk, d), lambda hh, t, tb: (tb[1, t], hh))
    return qspec, kspec


def _split3_cols(x):
    hi = x.astype(BF16).astype(F32)
    rest = x - hi
    mid = rest.astype(BF16).astype(F32)
    lo = (rest - mid).astype(BF16).astype(F32)
    return hi, mid, lo


def _place3(base, col, pieces, sign):
    lane = lax.broadcasted_iota(jnp.int32, base.shape, 1)
    out = base.astype(F32)
    for i, piece in enumerate(pieces):
        out = jnp.where(lane == col + i, sign * piece, out)
    return out.astype(BF16)


def _flash_fwd(qa, ka, va, heads, l_col, lse_col, sub_rows, name):
    s = qa.shape[0]
    da, dv = qa.shape[1] // heads, va.shape[1] // heads
    hps = max(n for n in range(1, ATTN_FWD_LANES // max(da, dv) + 1) if heads % n == 0)
    bq, bk = _blk(s, ATTN_BLOCK_Q), _blk(s, ATTN_BLOCK_K)
    tab = _causal_table(s, bq, bk, True)

    def body(tab_ref, q_ref, k_ref, v_ref, o_ref, qb_ref, m_sc, acc_sc):
        t = pl.program_id(1)
        qi, ki = tab_ref[0, t], tab_ref[1, t]

        @pl.when(tab_ref[3, t] == 1)
        def _():
            m_sc[...] = jnp.full_like(m_sc, NEG_BIG)
            acc_sc[...] = jnp.zeros_like(acc_sc)

        def step(masked):
            for hh in range(hps):
                qc, vc = slice(hh * da, (hh + 1) * da), slice(hh * dv, (hh + 1) * dv)
                for r0, nr, c0, nc in _sub_tiles(bq, bk, masked, bq == bk, False, sub_rows[int(masked)]):
                    sc = lax.dot_general(q_ref[r0:r0 + nr, qc], k_ref[c0:c0 + nc, qc], _NT,
                                         preferred_element_type=F32)
                    if masked:
                        sc = jnp.where(_causal_keep(qi * bq + r0, ki * bk + c0, nr, nc, False), sc, NEG_BIG)
                    m_prev = m_sc[hh, r0:r0 + nr, :]
                    m_new = jnp.maximum(m_prev, jnp.max(sc, axis=-1, keepdims=True))
                    p = jnp.exp(sc - m_new).astype(BF16)
                    acc_sc[r0:r0 + nr, vc] = jnp.exp(m_prev - m_new) * acc_sc[r0:r0 + nr, vc] + lax.dot_general(
                        p, v_ref[c0:c0 + nc, vc], _NN, preferred_element_type=F32)
                    m_sc[hh, r0:r0 + nr, :] = m_new

        @pl.when(tab_ref[2, t] == 1)
        def _():
            step(True)

        @pl.when(tab_ref[2, t] == 0)
        def _():
            step(False)

        @pl.when(tab_ref[4, t] == 1)
        def _():
            for hh in range(hps):
                qc, vc = slice(hh * da, (hh + 1) * da), slice(hh * dv, (hh + 1) * dv)
                acc = acc_sc[:, vc]
                lane = lax.broadcasted_iota(jnp.int32, acc.shape, 1)
                l = jnp.sum(jnp.where(lane == l_col, acc, 0.0), axis=-1, keepdims=True)
                o_ref[:, vc] = (acc / l).astype(o_ref.dtype)
                lse = m_sc[hh] + jnp.log(l)
                qb_ref[:, qc] = _place3(q_ref[:, qc], lse_col, _split3_cols(lse), -1.0)

    qspec, kspec = _attn_specs(bq, bk)
    return pl.pallas_call(
        body, name=name,
        grid_spec=pltpu.PrefetchScalarGridSpec(
            num_scalar_prefetch=1, grid=(heads // hps, tab.shape[1]),
            in_specs=[qspec(hps * da), kspec(hps * da), kspec(hps * dv)],
            out_specs=[qspec(hps * dv), qspec(hps * da)],
            scratch_shapes=[pltpu.VMEM((hps, bq, 1), F32), pltpu.VMEM((bq, hps * dv), F32)]),
        out_shape=[jax.ShapeDtypeStruct((s, heads * dv), BF16), jax.ShapeDtypeStruct((s, heads * da), BF16)],
        compiler_params=_params(("parallel", "arbitrary")),
    )(tab, qa, ka, va)


def _delta_epilogue(dv, delta_col):
    def epilogue(acc, o_tile):
        heads_out = []
        for hh in range(acc.shape[1] // dv):
            vc = slice(hh * dv, (hh + 1) * dv)
            dov = acc[:, vc].astype(BF16)
            delta = jnp.sum(dov.astype(F32) * o_tile[:, vc].astype(F32), axis=-1, keepdims=True)
            heads_out.append(_place3(dov, delta_col, _split3_cols(delta), 1.0))
        return (jnp.concatenate(heads_out, axis=1),)
    return epilogue


_TN =(((0,), (0,)), ((), ()))


def _flash_bwd(qa, ka, va, doa, heads, hps, name, sum_cols=None):
    s = qa.shape[0]
    da, dv = qa.shape[1] // heads, va.shape[1] // heads
    h = heads // hps
    bq, bk = _blk(s, ATTN_BLOCK_Q), _blk(s, ATTN_BLOCK_K)
    tab = _causal_table(s, bq, bk, False)
    n_tiles = tab.shape[1]
    n_sum = 0 if sum_cols is None else 2

    def head_column(acc, col):
        out = jnp.zeros((acc.shape[0], hps), F32)
        lane = lax.broadcasted_iota(jnp.int32, (acc.shape[0], da), 1)
        pick = lax.broadcasted_iota(jnp.int32, out.shape, 1)
        for hh in range(hps):
            val = jnp.sum(jnp.where(lane == col, acc[:, hh * da:(hh + 1) * da], 0.0), axis=-1, keepdims=True)
            out = jnp.where(pick == hh, val, out)
        return out

    def body(tab_ref, q_ref, k_ref, v_ref, do_ref, dq_ref, dk_ref, dv_ref, *rest):
        sum_refs, (dk_sc, dv_sc) = rest[:n_sum], rest[n_sum:]
        t = pl.program_id(1)
        qi, ki = tab_ref[0, t], tab_ref[1, t]

        @pl.when(t == 0)
        def _():
            dq_ref[...] = jnp.zeros_like(dq_ref)

        @pl.when(tab_ref[3, t] == 1)
        def _():
            dk_sc[...] = jnp.zeros_like(dk_sc)
            dv_sc[...] = jnp.zeros_like(dv_sc)

        def step(masked):
            for hh in range(hps):
                qc, vc = slice(hh * da, (hh + 1) * da), slice(hh * dv, (hh + 1) * dv)
                for r0, nr, c0, nc in _sub_tiles(bk, bq, masked, bq == bk, True, ATTN_SUB_ROWS):
                    qv, dov, kv = q_ref[c0:c0 + nc, qc], do_ref[c0:c0 + nc, vc], k_ref[r0:r0 + nr, qc]
                    st = lax.dot_general(kv, qv, _NT, preferred_element_type=F32)
                    if masked:
                        st = jnp.where(_causal_keep(qi * bq + c0, ki * bk + r0, nc, nr, True), st, NEG_BIG)
                    pt = jnp.exp(st)
                    dv_sc[r0:r0 + nr, vc] += lax.dot_general(pt.astype(BF16), dov, _NN, preferred_element_type=F32)
                    dpt = lax.dot_general(v_ref[r0:r0 + nr, vc], dov, _NT, preferred_element_type=F32)
                    dst = (pt * dpt).astype(BF16)
                    dk_sc[r0:r0 + nr, qc] += lax.dot_general(dst, qv, _NN, preferred_element_type=F32)
                    q_rows = pl.ds(pl.multiple_of(qi * bq + c0, ATTN_SUB_ROWS), nc)
                    dq_ref[q_rows, qc] += lax.dot_general(dst, kv, _TN, preferred_element_type=F32)

        @pl.when(tab_ref[2, t] == 1)
        def _():
            step(True)

        @pl.when(tab_ref[2, t] == 0)
        def _():
            step(False)

        @pl.when(tab_ref[4, t] == 1)
        def _():
            dk_ref[...] = dk_sc[...]
            dv_ref[...] = dv_sc[...]
            if n_sum:
                sum_refs[1][...] = head_column(dk_sc[...], sum_cols[1])

        if n_sum:
            @pl.when(t == n_tiles - 1)
            def _():
                sum_refs[0][...] = head_column(dq_ref[...], sum_cols[0])

    qspec, kspec = _attn_specs(bq, bk)
    out_specs = [pl.BlockSpec((s, hps * da), lambda hh, t, tb: (0, hh), pipeline_mode=pl.Buffered(1)),
                 kspec(hps * da), kspec(hps * dv)]
    out_shape = [jax.ShapeDtypeStruct((s, heads * da), F32), jax.ShapeDtypeStruct((s, heads * da), F32),
                 jax.ShapeDtypeStruct((s, heads * dv), F32)]
    if n_sum:
        out_specs += [pl.BlockSpec((None, s, hps), lambda hh, t, tb: (hh, 0, 0), pipeline_mode=pl.Buffered(1)),
                      pl.BlockSpec((None, bk, hps), lambda hh, t, tb: (hh, tb[1, t], 0))]
        out_shape += [jax.ShapeDtypeStruct((h, s, hps), F32)] * 2
    return pl.pallas_call(
        body, name=name,
        grid_spec=pltpu.PrefetchScalarGridSpec(
            num_scalar_prefetch=1, grid=(h, n_tiles),
            in_specs=[qspec(hps * da), kspec(hps * da), kspec(hps * dv), qspec(hps * dv)],
            out_specs=out_specs,
            scratch_shapes=[pltpu.VMEM((bk, hps * da), F32), pltpu.VMEM((bk, hps * dv), F32)]),
        out_shape=out_shape,
        compiler_params=_params(("parallel", "arbitrary")),
    )(tab, qa, ka, va, doa)


def _split3(x):
    hi = lax.reduce_precision(x, 8, 7)
    rest = x - hi
    mid = lax.reduce_precision(rest, 8, 7)
    lo = lax.reduce_precision(rest - mid, 8, 7)
    return jnp.stack([hi, mid, lo], axis=-1).astype(BF16)


def _pad_heads(w, heads, width, axis):
    shape = list(w.shape)
    d = shape[axis] // heads
    w = w.reshape(shape[:axis] + [heads, d] + shape[axis + 1:])
    pad = [(0, 0)] * w.ndim
    pad[axis + 1] = (0, width - d)
    return jnp.pad(w, pad).reshape(shape[:axis] + [heads * width] + shape[axis + 1:])


def _unpad_heads(w, heads, d, axis):
    shape = list(w.shape)
    width = shape[axis] // heads
    w = w.reshape(shape[:axis] + [heads, width] + shape[axis + 1:])
    w = lax.slice_in_dim(w, 0, d, axis=axis + 1)
    return w.reshape(shape[:axis] + [heads * d] + shape[axis + 1:])


def _placement(rows, heads, width, entries):
    e = np.zeros((rows, heads * width), np.float32)
    for row, col, val in entries:
        for hh in range(heads):
            e[row(hh) if callable(row) else row, hh * width + col] = val
    return jnp.asarray(e, BF16)


def _rope_mix(a, b, cos_t, sin_t, scale, heads, name):
    s = a.shape[0]
    d = a.shape[1] // heads
    bs = _blk(s, 1024, 8)

    def body(a_ref, b_ref, c_ref, s_ref, o_ref):
        o_ref[...] = ((a_ref[...] * c_ref[...] + b_ref[...] * s_ref[...]) * scale).astype(o_ref.dtype)

    blk = pl.BlockSpec((bs, d), lambda i, hh: (i, hh))
    tbl = pl.BlockSpec((bs, d), lambda i, hh: (i, 0))
    return pl.pallas_call(
        body, name=name, grid=(s // bs, heads), in_specs=[blk, blk, tbl, tbl], out_specs=blk,
        out_shape=jax.ShapeDtypeStruct(a.shape, BF16),
        compiler_params=_params(("parallel", "parallel")),
    )(a, b, cos_t, sin_t)


def _rope_proj(x, w_a, w_b, cos_t, sin_t, scale, heads, name):
    s, kdim = x.shape
    d = w_a.shape[1] // heads
    hpt = max(n for n in range(1, max(1, MATMUL_BLOCK // d) + 1) if heads % n == 0)
    bm = _blk(s, MATMUL_BLOCK)
    cos_w, sin_w = jnp.tile(cos_t, (1, hpt)), jnp.tile(sin_t, (1, hpt))

    def body(x_ref, wa_ref, wb_ref, c_ref, s_ref, o_ref):
        xv = x_ref[...]
        a = lax.dot_general(xv, wa_ref[...], _NN, preferred_element_type=F32)
        b = lax.dot_general(xv, wb_ref[...], _NN, preferred_element_type=F32)
        o_ref[...] = ((a * c_ref[...] + b * s_ref[...]) * scale).astype(o_ref.dtype)

    wide = pl.BlockSpec((bm, hpt * d), lambda i, j: (i, j))
    tbl = pl.BlockSpec((bm, hpt * d), lambda i, j: (i, 0))
    wgt = pl.BlockSpec((kdim, hpt * d), lambda i, j: (0, j))
    return pl.pallas_call(
        body, name=name, grid=(s // bm, heads // hpt),
        in_specs=[pl.BlockSpec((bm, kdim), lambda i, j: (i, 0)), wgt, wgt, tbl, tbl], out_specs=wide,
        out_shape=jax.ShapeDtypeStruct((s, heads * d), BF16),
        compiler_params=_params(("parallel", "parallel")),
    )(x, w_a, w_b, cos_w, sin_w)


def _rope_proj_bwd(x, g, w_a, w_b, cos_t, sin_t, scale, heads, name):
    s, kdim = x.shape
    d = w_a.shape[1] // heads
    hpt = max(n for n in range(1, max(1, MATMUL_BLOCK // d) + 1) if heads % n == 0)
    bs = _blk(s, MATMUL_BLOCK)
    n_tiles, n_rows = heads // hpt, s // bs
    cos_w, sin_w = jnp.tile(cos_t, (1, hpt)), jnp.tile(sin_t, (1, hpt))

    def halves(g_ref, c_ref, s_ref):
        gv = g_ref[...] * scale
        return (gv * c_ref[...]).astype(BF16), (gv * s_ref[...]).astype(BF16)

    def dx_body(g_ref, wa_ref, wb_ref, c_ref, s_ref, dx_ref, acc_ref):
        j = pl.program_id(1)
        ga, gb = halves(g_ref, c_ref, s_ref)
        part = (lax.dot_general(ga, wa_ref[...], _NT, preferred_element_type=F32)
                + lax.dot_general(gb, wb_ref[...], _NT, preferred_element_type=F32))

        @pl.when(j == 0)
        def _():
            acc_ref[...] = part

        @pl.when(j > 0)
        def _():
            acc_ref[...] += part

        @pl.when(j == n_tiles - 1)
        def _():
            dx_ref[...] = acc_ref[...]

    def dw_body(x_ref, g_ref, c_ref, s_ref, dwa_ref, dwb_ref, acc_a, acc_b):
        i = pl.program_id(1)
        ga, gb = halves(g_ref, c_ref, s_ref)
        xv = x_ref[...]
        pa = lax.dot_general(xv, ga, _TN, preferred_element_type=F32)
        pb = lax.dot_general(xv, gb, _TN, preferred_element_type=F32)

        @pl.when(i == 0)
        def _():
            acc_a[...] = pa
            acc_b[...] = pb

        @pl.when(i > 0)
        def _():
            acc_a[...] += pa
            acc_b[...] += pb

        @pl.when(i == n_rows - 1)
        def _():
            dwa_ref[...] = acc_a[...]
            dwb_ref[...] = acc_b[...]

    dx = pl.pallas_call(
        dx_body, name=name + "_dx", grid=(n_rows, n_tiles),
        in_specs=[pl.BlockSpec((bs, hpt * d), lambda i, j: (i, j)),
                  pl.BlockSpec((kdim, hpt * d), lambda i, j: (0, j)), pl.BlockSpec((kdim, hpt * d), lambda i, j: (0, j)),
                  pl.BlockSpec((bs, hpt * d), lambda i, j: (i, 0)), pl.BlockSpec((bs, hpt * d), lambda i, j: (i, 0))],
        out_specs=pl.BlockSpec((bs, kdim), lambda i, j: (i, 0)),
        out_shape=jax.ShapeDtypeStruct((s, kdim), F32),
        scratch_shapes=[pltpu.VMEM((bs, kdim), F32)],
        compiler_params=_params(("parallel", "arbitrary")),
    )(g, w_a, w_b, cos_w, sin_w)
    dwa, dwb = pl.pallas_call(
        dw_body, name=name + "_dw", grid=(n_tiles, n_rows),
        in_specs=[pl.BlockSpec((bs, kdim), lambda j, i: (i, 0)),
                  pl.BlockSpec((bs, hpt * d), lambda j, i: (i, j)),
                  pl.BlockSpec((bs, hpt * d), lambda j, i: (i, 0)), pl.BlockSpec((bs, hpt * d), lambda j, i: (i, 0))],
        out_specs=[pl.BlockSpec((kdim, hpt * d), lambda j, i: (0, j))] * 2,
        out_shape=[jax.ShapeDtypeStruct((kdim, heads * d), F32)] * 2,
        scratch_shapes=[pltpu.VMEM((kdim, hpt * d), F32)] * 2,
        compiler_params=_params(("parallel", "arbitrary")),
    )(x, g, cos_w, sin_w)
    return dx, dwa, dwb


def _adamw(w, g, m, v, name):
    r, wd = w.shape
    br = _blk(r, 512, 8)

    def body(w_ref, g_ref, m_ref, v_ref, d_ref, nm_ref, nv_ref):
        gv = g_ref[...]
        mn = ADAM_B1 * m_ref[...] + (1.0 - ADAM_B1) * gv
        vn = ADAM_B2 * v_ref[...] + (1.0 - ADAM_B2) * (gv * gv)
        m_hat = mn / (1.0 - ADAM_B1 ** ADAM_STEP)
        v_hat = vn / (1.0 - ADAM_B2 ** ADAM_STEP)
        d_ref[...] = -ADAM_LR * (m_hat / (jnp.sqrt(v_hat) + ADAM_EPS) + ADAM_WD * w_ref[...])
        nm_ref[...] = mn
        nv_ref[...] = vn

    row = pl.BlockSpec((br, wd), lambda i: (i, 0))
    return pl.pallas_call(
        body, name=name, grid=(r // br,), in_specs=[row] * 4, out_specs=[row] * 3,
        out_shape=[jax.ShapeDtypeStruct((r, wd), F32)] * 3,
        compiler_params=_params(("parallel",)),
    )(w, g, m, v)


_ANY = pl.BlockSpec(memory_space=pl.ANY)


def _place():
    return lax.axis_index("x"), lax.axis_index("y"), lax.axis_index("c"), None


def _all_gather_shards(shard, name):
    r, w = shard.shape
    hr = r // 2
    qr = hr // 2

    def body(x_ref, out_ref, send_sems, recv_sems):
        x, y, c, _ = _place()
        me, sibling, y_nbr, x_nbr = (x, y, c), (x, y, 1 - c), (x, 1 - y, c), (1 - x, y, c)

        def rows(j, half, piece=None):
            if piece is None:
                return out_ref.at[j, pl.ds(pl.multiple_of(half * hr, 16), hr), :]
            return out_ref.at[j, pl.ds(pl.multiple_of(half * hr + piece * qr, 16), qr), :]

        def mine(piece):
            return x_ref.at[pl.ds(pl.multiple_of(c * hr + piece * qr, 16), qr), :]

        def copy(sem, src, dst, to):
            return pltpu.make_async_remote_copy(src_ref=src, dst_ref=dst, send_sem=send_sems.at[sem],
                                                recv_sem=recv_sems.at[sem], device_id=to, device_id_type=MESH)

        sent = [copy(0, mine(0), rows(0, c, 0), y_nbr), copy(1, mine(1), rows(0, c, 1), y_nbr),
                copy(2, mine(0), rows(1, c, 0), x_nbr), copy(3, mine(1), rows(1, c, 1), x_nbr)]
        for cp in sent:
            cp.start()

        def landed(sem, ref):
            copy(sem, ref, ref, me).wait_recv()

        def pass_on(sem, src, dst, to):
            cp = copy(sem, src, dst, to)
            cp.start()
            sent.append(cp)

        landed(2, rows(1, c, 0))
        pass_on(4, rows(1, c, 0), rows(2, c, 0), y_nbr)
        landed(1, rows(0, c, 1))
        pass_on(5, rows(0, c, 1), rows(2, c, 1), x_nbr)
        landed(0, rows(0, c, 0))
        pass_on(6, rows(0, c), rows(0, c), sibling)
        landed(3, rows(1, c, 1))
        pass_on(7, rows(1, c), rows(1, c), sibling)
        landed(4, rows(2, c, 0))
        landed(5, rows(2, c, 1))
        pass_on(8, rows(2, c), rows(2, c), sibling)
        for j in range(3):
            landed(6 + j, rows(j, 1 - c))
        for cp in sent:
            cp.wait_send()

    return pl.pallas_call(
        body, name=name, in_specs=[_ANY], out_specs=_ANY,
        out_shape=jax.ShapeDtypeStruct((N_CHIPS - 1, r, w), shard.dtype),
        scratch_shapes=[pltpu.SemaphoreType.DMA((9,)), pltpu.SemaphoreType.DMA((9,))],
        compiler_params=pltpu.CompilerParams(vmem_limit_bytes=VMEM_LIMIT_BYTES),
    )(shard)


def _sibling_swap_halves(g, name):
    nq, r, w = g.shape
    hr = r // 2

    def body(g_ref, a_ref, send_sems, recv_sems):
        x, y, c, _ = _place()
        sibling = (x, y, 1 - c)
        cps = []
        for q in range(nq):
            cp = pltpu.make_async_remote_copy(
                src_ref=g_ref.at[q, pl.ds(pl.multiple_of((1 - c) * hr, 8), hr), :], dst_ref=a_ref.at[q],
                send_sem=send_sems.at[q], recv_sem=recv_sems.at[q], device_id=sibling, device_id_type=MESH)
            cp.start()
            cps.append(cp)
        for cp in cps:
            cp.wait()

    return pl.pallas_call(
        body, name=name, in_specs=[_ANY], out_specs=_ANY,
        out_shape=jax.ShapeDtypeStruct((nq, hr, w), g.dtype),
        scratch_shapes=[pltpu.SemaphoreType.DMA((nq,)), pltpu.SemaphoreType.DMA((nq,))],
        compiler_params=pltpu.CompilerParams(vmem_limit_bytes=VMEM_LIMIT_BYTES),
    )(g)


def _chip_sum(g, a, c_idx, name):
    nq, r, w = g.shape
    hr = r // 2
    br = _blk(hr, 512, 16)
    nb = hr // br

    def body(c_ref, g_ref, a_ref, o_ref):
        o_ref[...] = (g_ref[...] + a_ref[...]).astype(o_ref.dtype)

    return pl.pallas_call(
        body, name=name,
        grid_spec=pltpu.PrefetchScalarGridSpec(
            num_scalar_prefetch=1, grid=(nq, nb),
            in_specs=[pl.BlockSpec((None, br, w), lambda q, i, cr: (q, cr[0] * nb + i, 0)),
                      pl.BlockSpec((None, br, w), lambda q, i, cr: (q, i, 0))],
            out_specs=pl.BlockSpec((None, br, w), lambda q, i, cr: (q, i, 0))),
        out_shape=jax.ShapeDtypeStruct((nq, hr, w), BF16),
        compiler_params=_params(("parallel", "parallel")),
    )(c_idx, g, a)


def _rs_first_hop(s4, name):
    nq, hr, w = s4.shape
    qr = hr // 2

    def body(s_ref, r_ref, send_sems, recv_sems):
        x, y, c, _ = _place()
        p = 2 * x + y
        x_nbr, y_nbr = (1 - x, y, c), (x, 1 - y, c)

        def piece(q, k):
            return s_ref.at[q, pl.ds(k * qr, qr), :]

        sends = [(piece(p ^ 2, 0), x_nbr), (piece(p ^ 3, 0), x_nbr), (piece(p ^ 1, 1), y_nbr), (piece(p ^ 3, 1), y_nbr)]
        cps = []
        for k, (src, to) in enumerate(sends):
            cp = pltpu.make_async_remote_copy(src_ref=src, dst_ref=r_ref.at[k], send_sem=send_sems.at[k],
                                              recv_sem=recv_sems.at[k], device_id=to, device_id_type=MESH)
            cp.start()
            cps.append(cp)
        for cp in cps:
            cp.wait()

    return pl.pallas_call(
        body, name=name, in_specs=[_ANY], out_specs=_ANY,
        out_shape=jax.ShapeDtypeStruct((nq, qr, w), s4.dtype),
        scratch_shapes=[pltpu.SemaphoreType.DMA((nq,)), pltpu.SemaphoreType.DMA((nq,))],
        compiler_params=pltpu.CompilerParams(vmem_limit_bytes=VMEM_LIMIT_BYTES),
    )(s4)


def _rs_middle(s4, r1, chip_idx, name):
    _, hr, w = s4.shape
    qr = hr // 2
    br = _blk(qr, 512, 16)
    nb = qr // br

    def body(idx_ref, mine_ref, theirs_ref, got_mine_ref, got_theirs_ref, own_ref, onward_ref):
        own_ref[...] = mine_ref[...].astype(F32) + got_mine_ref[...].astype(F32)
        onward_ref[...] = (theirs_ref[...].astype(F32) + got_theirs_ref[...].astype(F32)).astype(onward_ref.dtype)

    return pl.pallas_call(
        body, name=name,
        grid_spec=pltpu.PrefetchScalarGridSpec(
            num_scalar_prefetch=1, grid=(2, nb),
            in_specs=[pl.BlockSpec((None, br, w), lambda k, i, ix: (ix[0], k * nb + i, 0)),
                      pl.BlockSpec((None, br, w), lambda k, i, ix: (ix[0] ^ (k + 1), k * nb + i, 0)),
                      pl.BlockSpec((None, br, w), lambda k, i, ix: (2 * k, i, 0)),
                      pl.BlockSpec((None, br, w), lambda k, i, ix: (2 * k + 1, i, 0))],
            out_specs=[pl.BlockSpec((br, w), lambda k, i, ix: (k * nb + i, 0)),
                       pl.BlockSpec((None, br, w), lambda k, i, ix: (k, i, 0))]),
        out_shape=[jax.ShapeDtypeStruct((hr, w), F32), jax.ShapeDtypeStruct((2, qr, w), s4.dtype)],
        compiler_params=_params(("parallel", "parallel")),
    )(chip_idx, s4, s4, r1, r1)


def _rs_second_hop(onward, name):
    def body(u_ref, r_ref, send_sems, recv_sems):
        x, y, c, _ = _place()
        cps = []
        for k, to in enumerate([(x, 1 - y, c), (1 - x, y, c)]):
            cp = pltpu.make_async_remote_copy(src_ref=u_ref.at[k], dst_ref=r_ref.at[k], send_sem=send_sems.at[k],
                                              recv_sem=recv_sems.at[k], device_id=to, device_id_type=MESH)
            cp.start()
            cps.append(cp)
        for cp in cps:
            cp.wait()

    return pl.pallas_call(
        body, name=name, in_specs=[_ANY], out_specs=_ANY,
        out_shape=jax.ShapeDtypeStruct(onward.shape, onward.dtype),
        scratch_shapes=[pltpu.SemaphoreType.DMA((2,)), pltpu.SemaphoreType.DMA((2,))],
        compiler_params=pltpu.CompilerParams(vmem_limit_bytes=VMEM_LIMIT_BYTES),
    )(onward)


def _rs_last_add(own, r2, name):
    hr, w = own.shape
    br = _blk(hr // 2, 512, 16)

    def body(a_ref, b_ref, o_ref):
        o_ref[...] = a_ref[...] + b_ref[...].astype(F32)

    row = pl.BlockSpec((br, w), lambda i: (i, 0))
    return pl.pallas_call(
        body, name=name, grid=(hr // br,), in_specs=[row, row], out_specs=row,
        out_shape=jax.ShapeDtypeStruct((hr, w), F32),
        compiler_params=_params(("parallel",)),
    )(own, r2.reshape(hr, w))


def _sibling_swap(t, name):
    hr, w = t.shape

    def body(t_ref, o_ref, send_sem, recv_sem):
        x, y, c, _ = _place()
        cp = pltpu.make_async_remote_copy(src_ref=t_ref, dst_ref=o_ref, send_sem=send_sem, recv_sem=recv_sem,
                                          device_id=(x, y, 1 - c), device_id_type=MESH)
        cp.start()
        cp.wait()

    return pl.pallas_call(
        body, name=name, in_specs=[_ANY], out_specs=_ANY,
        out_shape=jax.ShapeDtypeStruct((hr, w), t.dtype),
        scratch_shapes=[pltpu.SemaphoreType.DMA, pltpu.SemaphoreType.DMA],
        compiler_params=pltpu.CompilerParams(vmem_limit_bytes=VMEM_LIMIT_BYTES),
    )(t)


def _all_reduce_small(v, name):
    r, w = v.shape

    def body(v_ref, o_ref, slots, send_sems, recv_sems):
        x, y, c, _ = _place()
        me = 4 * x + 2 * y + c
        slots[me] = v_ref[...]
        cps = []
        for k in range(1, N_DEV):
            fx, fy, fc = (k >> 2) & 1, (k >> 1) & 1, k & 1
            to = (x ^ fx, y ^ fy, c ^ fc)
            cp = pltpu.make_async_remote_copy(
                src_ref=v_ref, dst_ref=slots.at[me], send_sem=send_sems.at[k - 1], recv_sem=recv_sems.at[k - 1],
                device_id=to, device_id_type=MESH)
            cp.start()
            cps.append(cp)
        for k in range(1, N_DEV):
            fx, fy, fc = (k >> 2) & 1, (k >> 1) & 1, k & 1
            src_dev = 4 * (x ^ fx) + 2 * (y ^ fy) + (c ^ fc)
            pltpu.make_async_remote_copy(
                src_ref=v_ref, dst_ref=slots.at[src_dev], send_sem=send_sems.at[k - 1],
                recv_sem=recv_sems.at[k - 1], device_id=(x, y, c), device_id_type=MESH).wait_recv()
        for cp in cps:
            cp.wait_send()
        acc = slots[0]
        for d in range(1, N_DEV):
            acc = acc + slots[d]
        o_ref[...] = acc

    return pl.pallas_call(
        body, name=name,
        in_specs=[pl.BlockSpec(memory_space=pltpu.VMEM)], out_specs=pl.BlockSpec(memory_space=pltpu.VMEM),
        out_shape=jax.ShapeDtypeStruct((r, w), F32),
        scratch_shapes=[pltpu.VMEM((N_DEV, r, w), F32), pltpu.SemaphoreType.DMA((N_DEV - 1,)),
                        pltpu.SemaphoreType.DMA((N_DEV - 1,))],
        compiler_params=pltpu.CompilerParams(vmem_limit_bytes=VMEM_LIMIT_BYTES),
    )(v)


def _part_rows(shape, part_rows=PACK_PART_ROWS):
    assert shape[-1] <= PACK_LANES
    return _round_up(math.prod(shape[:-1]), part_rows)


def _packed_rows(shapes):
    return _round_up(sum(_part_rows(s) for s in shapes), PACK_ROWS_MULT)


def _pack(arrs, total_rows, dtype, part_rows=PACK_PART_ROWS):
    parts = []
    for a in arrs:
        a2 = a.reshape(-1, a.shape[-1]).astype(dtype)
        rows = _part_rows(a.shape, part_rows)
        parts.append(jnp.pad(a2, ((0, rows - a2.shape[0]), (0, PACK_LANES - a2.shape[1]))))
    used = sum(p.shape[0] for p in parts)
    if total_rows > used:
        parts.append(jnp.zeros((total_rows - used, PACK_LANES), dtype))
    return jnp.concatenate(parts, axis=0)


def _unpack(packed, shapes, part_rows=PACK_PART_ROWS):
    out, r0 = [], 0
    for s in shapes:
        out.append(packed[r0:r0 + math.prod(s[:-1]), :s[-1]].reshape(s))
        r0 += _part_rows(s, part_rows)
    return out


_BIG = (("fox_w_in", 2), ("fox_w_out", 1), ("mla_w_kv_a", 0), ("mla_w_kv_b", 1), ("mla_w_q_a", 1),
        ("mla_w_q_b", 2), ("mla_w_out", 1), ("ffn_w_up", 2), ("ffn_w_down", 1))
_SMALL = ("norm_mix_g", "norm_ffn_g", "fox_b_f", "kv_norm_g", "mla_kv_a_norm_g", "mla_q_a_norm_g", "final_norm_g")
_WEIGHTS = ("norm_mix_g", "norm_ffn_g", "fox_w_in", "fox_b_f", "fox_w_out", "kv_norm_g", "mla_w_kv_a",
            "mla_kv_a_norm_g", "mla_w_kv_b", "mla_w_q_a", "mla_q_a_norm_g", "mla_w_q_b", "mla_w_out",
            "ffn_w_up", "ffn_w_down", "final_norm_g")


def _ffn_fwd(x, h, w_up, w_down, tag):
    def relu_sq(acc):
        r = jnp.maximum(acc, 0.0)
        return r, r * r

    r, a = _matmul(h, w_up, mode="nn", out_dtypes=(BF16, BF16), epilogue=relu_sq, name=f"{tag}_up")
    x_out = _matmul(a, w_down, mode="nn", out_dtypes=(F32,), epilogue=lambda acc, res: (acc + res,),
                    extras=(x,), name=f"{tag}_down")
    return x_out, r, a


def _ffn_bwd(dx_out, x_in, h, r, a, g_norm, w_up, w_down, tag):
    d_u = _matmul(dx_out, w_down, mode="nt", out_dtypes=(BF16,), epilogue=lambda acc, rr: (acc * (2.0 * rr.astype(F32)),),
                  extras=(r,), name=f"{tag}_d_act")
    d_w_down = _matmul(a, dx_out, mode="tn", out_dtypes=(F32,), name=f"{tag}_d_w_down")
    d_w_up = _matmul(h, d_u, mode="tn", out_dtypes=(F32,), by_chip=True, name=f"{tag}_d_w_up")
    d_h = _matmul(d_u, w_up, mode="nt", out_dtypes=(F32,), name=f"{tag}_d_h")
    dx_in, (d_g,) = _rms_bwd(x_in, [(g_norm, d_h)], dx_out, name=f"{tag}_d_norm")
    return dx_in, d_w_up, d_w_down, d_g


def kernel(x, norm_mix_g, norm_ffn_g, fox_w_in, fox_b_f, fox_w_out, kv_norm_g, mla_w_kv_a, mla_kv_a_norm_g, mla_w_kv_b, mla_w_q_a, mla_q_a_norm_g, mla_w_q_b, mla_w_out, ffn_w_up, ffn_w_down, final_norm_g, loss_target, m_norm_mix_g, m_norm_ffn_g, m_fox_w_in, m_fox_b_f, m_fox_w_out, m_kv_norm_g, m_mla_w_kv_a, m_mla_kv_a_norm_g, m_mla_w_kv_b, m_mla_w_q_a, m_mla_q_a_norm_g, m_mla_w_q_b, m_mla_w_out, m_ffn_w_up, m_ffn_w_down, m_final_norm_g, v_norm_mix_g, v_norm_ffn_g, v_fox_w_in, v_fox_b_f, v_fox_w_out, v_kv_norm_g, v_mla_w_kv_a, v_mla_kv_a_norm_g, v_mla_w_kv_b, v_mla_w_q_a, v_mla_q_a_norm_g, v_mla_w_q_b, v_mla_w_out, v_ffn_w_up, v_ffn_w_down, v_final_norm_g):
    args = dict(locals())
    w_in = {n: args[n] for n in _WEIGHTS}
    m_in = {n: args["m_" + n] for n in _WEIGHTS}
    v_in = {n: args["v_" + n] for n in _WEIGHTS}

    xs = x[0]
    seq, d_model = xs.shape
    tgt = loss_target[0]
    fox_h, mla_h, nope = FOX_HEADS, MLA_HEADS, QK_NOPE_DIM
    kv_rank = mla_kv_a_norm_g.shape[0]
    rope = mla_w_kv_a.shape[1] - kv_rank
    half = rope // 2
    q_rank = mla_q_a_norm_g.shape[1]
    v_dim = mla_w_kv_b.shape[1] * N_CHIPS // mla_h - nope
    fox_w = fox_w_out.shape[1] * N_CHIPS
    fox_dh = fox_w // fox_h

    big_names = [n for n, _ in _BIG]
    shard_shapes = [w_in[n].shape for n in big_names]
    rows = _packed_rows(shard_shapes)
    my_shard = _pack([w_in[n] for n in big_names], rows, BF16)
    others = _all_gather_shards(my_shard, name="gather_weights")
    by_relation = jnp.concatenate([my_shard[None], others], axis=0)
    p_chip = 2 * lax.axis_index("x") + lax.axis_index("y")
    full = {}
    for q in range(N_CHIPS):
        shard_q = lax.dynamic_index_in_dim(by_relation, p_chip ^ q, axis=0, keepdims=False)
        for (n, ax), piece in zip(_BIG, _unpack(shard_q, shard_shapes)):
            full.setdefault(n, []).append(piece)
    full = {n: jnp.concatenate(full[n], axis=ax) for n, ax in _BIG}

    fox_scale = fox_dh ** -0.5
    fox_wd = _round_up(fox_dh + 9, LANE_TILE)
    fox_vwd = _round_up(fox_dh + 4, LANE_TILE)
    w_fox_in = full["fox_w_in"][0]
    w_fq = _pad_heads(w_fox_in[:, :fox_w] * fox_scale, fox_h, fox_wd, 1)
    w_fk = _pad_heads(w_fox_in[:, fox_w:2 * fox_w], fox_h, fox_wd, 1)
    w_fv = _pad_heads(w_fox_in[:, 2 * fox_w:3 * fox_w], fox_h, fox_vwd, 1)
    w_gate = w_fox_in[:, 3 * fox_w:]
    w_fox_out = _pad_heads(full["fox_w_out"][0], fox_h, fox_vwd, 0)
    n_cx = _round_up(3 * fox_h + 1, LANE_TILE)
    c_piece = lambda i: (lambda hh: 3 * hh + i)
    one_col = 3 * fox_h
    e_fq = _placement(n_cx, fox_h, fox_wd, [(c_piece(i), fox_dh + i, 1.0) for i in range(3)]
                      + [(one_col, fox_dh + 3 + i, 1.0) for i in range(3)])
    e_fk = _placement(n_cx, fox_h, fox_wd, [(one_col, fox_dh + i, 1.0) for i in range(3)]
                      + [(c_piece(i), fox_dh + 3 + i, -1.0) for i in range(3)]
                      + [(one_col, fox_dh + 6 + i, 1.0) for i in range(3)])
    e_fv = _placement(n_cx, fox_h, fox_vwd, [(one_col, fox_dh + i, -1.0) for i in range(3)]
                      + [(one_col, fox_dh + 3, 1.0)])

    mla_scale = (nope + rope) ** -0.5
    mla_dk = nope + rope
    mla_wd = _round_up(mla_dk + 3, LANE_TILE)
    mla_vwd = _round_up(v_dim + 4, LANE_TILE)
    w_kv_a = full["mla_w_kv_a"]
    w_kv_b3 = full["mla_w_kv_b"].reshape(kv_rank, mla_h, nope + v_dim)
    w_kn = _pad_heads(w_kv_b3[:, :, :nope].reshape(kv_rank, -1), mla_h, mla_wd, 1)
    w_mv = _pad_heads(w_kv_b3[:, :, nope:].reshape(kv_rank, -1), mla_h, mla_vwd, 1)
    w_q_a = full["mla_w_q_a"][0]
    w_q_b3 = full["mla_w_q_b"][0].reshape(q_rank, mla_h, nope + rope)
    w_qa_ = _pad_heads(w_q_b3.reshape(q_rank, -1), mla_h, mla_wd, 1)
    w_qb_ = _pad_heads(jnp.concatenate([jnp.zeros_like(w_q_b3[:, :, :nope]), -w_q_b3[:, :, nope + half:],
                                        w_q_b3[:, :, nope:nope + half]], axis=-1).reshape(q_rank, -1),
                       mla_h, mla_wd, 1)
    w_mla_out = _pad_heads(full["mla_w_out"][0], mla_h, mla_vwd, 0)
    w_up, w_down = full["ffn_w_up"], full["ffn_w_down"]
    n_kx = _round_up(rope + 1, LANE_TILE)
    e_mk = _placement(n_kx, mla_h, mla_wd, [(j, nope + j, 1.0) for j in range(rope)]
                      + [(rope, mla_dk + i, 1.0) for i in range(3)])
    e_mv = _placement(n_kx, mla_h, mla_vwd, [(rope, v_dim + i, -1.0) for i in range(3)] + [(rope, v_dim + 3, 1.0)])
    e_kr_u = _placement(n_kx, mla_h, mla_wd, [(j, nope + j, 1.0) for j in range(rope)]).T
    e_kr_v = _placement(n_kx, mla_h, mla_wd, [(j, nope + half + j, 1.0) for j in range(half)]
                        + [(half + j, nope + j, -1.0) for j in range(half)]).T

    inv = 1.0 / (ROPE_BASE ** (jnp.arange(0, rope, 2, dtype=F32) / rope))
    ang = jnp.arange(seq, dtype=F32)[:, None] * inv[None, :]
    cos, sin = jnp.cos(ang), jnp.sin(ang)
    pad_t = jnp.zeros((seq, mla_wd - mla_dk), F32)
    cos_t = jnp.concatenate([jnp.ones((seq, nope), F32), cos, cos, pad_t], axis=1)
    sin_t = jnp.concatenate([jnp.zeros((seq, nope), F32), sin, sin, pad_t], axis=1)
    pad_k = jnp.zeros((seq, n_kx - rope), F32)
    cos_k, sin_k = jnp.concatenate([cos, cos, pad_k], axis=1), jnp.concatenate([sin, sin, pad_k], axis=1)

    (h0,) = _rms_fwd(xs, norm_mix_g[0:1], name="l0_norm_mix")
    gate = _matmul(h0, w_gate, mode="nn", out_dtypes=(F32,), name="fox_gate")
    z = gate + fox_b_f[0][None, :]
    cum = jnp.cumsum(jax.nn.log_sigmoid(z), axis=0)
    cx = jnp.concatenate([_split3(cum).reshape(seq, 3 * fox_h), jnp.ones((seq, 1), BF16),
                          jnp.zeros((seq, n_cx - 3 * fox_h - 1), BF16)], axis=1)
    fqa = _matmul(h0, w_fq, mode="nn", out_dtypes=(BF16,), placed=(cx, e_fq), name="fox_q")
    fka = _matmul(h0, w_fk, mode="nn", out_dtypes=(BF16,), placed=(cx, e_fk), name="fox_k")
    fva = _matmul(h0, w_fv, mode="nn", out_dtypes=(BF16,), placed=(cx, e_fv), name="fox_v")
    foa, fqb = _flash_fwd(fqa, fka, fva, fox_h, fox_dh + 3, fox_dh + 6, FOX_FWD_SUB_ROWS, name="fox_attn")
    add_res = lambda acc, res: (acc + res,)
    x1 = _matmul(foa, w_fox_out, mode="nn", out_dtypes=(F32,), epilogue=add_res, extras=(xs,), name="fox_out")
    (h1,) = _rms_fwd(x1, norm_ffn_g[0:1], name="l0_norm_ffn")
    x2, r0, a0 = _ffn_fwd(x1, h1, w_up[0], w_down[0], "ffn0")

    src, h2 = _rms_fwd(x2, jnp.stack([kv_norm_g, norm_mix_g[1]]), name="l1_norm_kv_mix")
    kv_a = _matmul(src, w_kv_a, mode="nn", out_dtypes=(F32,), name="mla_kv_a")
    (c_kv,) = _rms_fwd(kv_a, mla_kv_a_norm_g[None, :], name="mla_norm_kv_a")
    kr1, kr2 = _rope(kv_a[None, :, kv_rank:kv_rank + half], kv_a[None, :, kv_rank + half:], cos, sin, 1.0,
                     name="mla_rope_k")
    krx = jnp.concatenate([kr1.astype(BF16), kr2.astype(BF16), jnp.ones((seq, 1), BF16),
                           jnp.zeros((seq, n_kx - rope - 1), BF16)], axis=1)
    mka = _matmul(c_kv, w_kn, mode="nn", out_dtypes=(BF16,), placed=(krx, e_mk), name="mla_k")
    mva = _matmul(c_kv, w_mv, mode="nn", out_dtypes=(BF16,), placed=(krx, e_mv), name="mla_v")
    cq_pre = _matmul(h2, w_q_a, mode="nn", out_dtypes=(F32,), name="mla_q_a")
    (c_q,) = _rms_fwd(cq_pre, mla_q_a_norm_g, name="mla_norm_q_a")
    mqa = _rope_proj(c_q, w_qa_, w_qb_, cos_t, sin_t, mla_scale, mla_h, name="mla_q_b_rope")
    moa, mqb = _flash_fwd(mqa, mka, mva, mla_h, v_dim + 3, mla_dk, MLA_FWD_SUB_ROWS, name="mla_attn")
    x3 = _matmul(moa, w_mla_out, mode="nn", out_dtypes=(F32,), epilogue=add_res, extras=(x2,), name="mla_out")
    (h3,) = _rms_fwd(x3, norm_ffn_g[1:2], name="l1_norm_ffn")
    x4, r1, a1 = _ffn_fwd(x3, h3, w_up[1], w_down[1], "ffn1")

    loss_tile, dx4, d_final_g = _loss_head(x4, final_norm_g[None, :], tgt, name="loss_head")
    loss = lax.psum(loss_tile[0, 0], ("x", "y", "c"))

    gw = {}
    dx3, d_up1, d_down1, d_nf1 = _ffn_bwd(dx4, x3, h3, r1, a1, norm_ffn_g[1:2], w_up[1], w_down[1], "ffn1")

    d_moa = _matmul(dx3, w_mla_out, mode="nt", out_dtypes=(BF16,), epilogue=_delta_epilogue(mla_vwd, v_dim),
                    extras=(moa,), name="mla_d_ctx")
    gw["mla_w_out"] = _unpad_heads(_matmul(moa, dx3, mode="tn", out_dtypes=(F32,), name="mla_d_w_out"),
                                   mla_h, v_dim, 0)[None]
    d_mqa, d_mka, d_mva = _flash_bwd(mqb, mka, mva, d_moa, mla_h, MLA_BWD_HEADS_PER_STEP,
                                     name="mla_attn_bwd")
    d_c_q, d_w_qa_, d_w_qb_ = _rope_proj_bwd(c_q, d_mqa, w_qa_, w_qb_, cos_t, sin_t, mla_scale, mla_h,
                                             name="mla_q_b_rope_bwd")
    d_w_qa_ = _unpad_heads(d_w_qa_, mla_h, mla_dk, 1).reshape(q_rank, mla_h, mla_dk)
    d_w_qb_ = _unpad_heads(d_w_qb_, mla_h, mla_dk, 1).reshape(q_rank, mla_h, mla_dk)
    gw["mla_w_q_b"] = jnp.concatenate(
        [d_w_qa_[:, :, :nope], d_w_qa_[:, :, nope:nope + half] + d_w_qb_[:, :, nope + half:],
         d_w_qa_[:, :, nope + half:] - d_w_qb_[:, :, nope:nope + half]], axis=-1).reshape(1, q_rank, mla_h * mla_dk)
    d_cq_pre, (d_q_a_g,) = _rms_bwd(cq_pre, [(mla_q_a_norm_g, d_c_q)], None, name="mla_d_norm_q_a")
    gw["mla_w_q_a"] = _matmul(h2, d_cq_pre, mode="tn", out_dtypes=(F32,), name="mla_d_w_q_a")[None]
    d_h2 = _matmul(d_cq_pre, w_q_a, mode="nt", out_dtypes=(F32,), name="mla_d_h")

    d_w_kn = _unpad_heads(_matmul(c_kv, d_mka, mode="tn", out_dtypes=(F32,), name="mla_d_w_k"), mla_h, nope, 1)
    d_w_mv = _unpad_heads(_matmul(c_kv, d_mva, mode="tn", out_dtypes=(F32,), name="mla_d_w_v"), mla_h, v_dim, 1)
    gw["mla_w_kv_b"] = jnp.concatenate([d_w_kn.reshape(kv_rank, mla_h, nope), d_w_mv.reshape(kv_rank, mla_h, v_dim)],
                                       axis=-1).reshape(kv_rank, mla_h * (nope + v_dim))
    d_c_kv_v = _matmul(d_mva, w_mv, mode="nt", out_dtypes=(F32,), name="mla_d_c_kv_v")
    d_c_kv = _matmul(d_mka, w_kn, mode="nt", out_dtypes=(F32,), epilogue=add_res, extras=(d_c_kv_v,),
                     name="mla_d_c_kv")
    d_ckv_pre, (d_kv_a_g,) = _rms_bwd(kv_a, [(mla_kv_a_norm_g[None, :], d_c_kv)], None, name="mla_d_norm_kv_a")
    d_kr_u = _matmul(d_mka, e_kr_u, mode="nn", out_dtypes=(F32,), name="mla_d_k_rope_u")
    d_kr_v = _matmul(d_mka, e_kr_v, mode="nn", out_dtypes=(F32,), name="mla_d_k_rope_v")
    d_kr = _rope_mix(d_kr_u, d_kr_v, cos_k, sin_k, 1.0, 1, name="mla_rope_dk")
    d_kv_a = jnp.concatenate([d_ckv_pre, d_kr[:, :rope].astype(F32)], axis=1)
    gw["mla_w_kv_a"] = _matmul(src, d_kv_a, mode="tn", out_dtypes=(F32,), name="mla_d_w_kv_a")
    d_src = _matmul(d_kv_a, w_kv_a, mode="nt", out_dtypes=(F32,), name="mla_d_src")
    dx2, (d_kv_g, d_nm1) = _rms_bwd(x2, [(kv_norm_g[None, :], d_src), (norm_mix_g[1:2], d_h2)], dx3,
                                    name="l1_d_norm_kv_mix")

    dx1, d_up0, d_down0, d_nf0 = _ffn_bwd(dx2, x1, h1, r0, a0, norm_ffn_g[0:1], w_up[0], w_down[0], "ffn0")
    by_rows = lambda g: g.reshape(N_CHIPS, g.shape[0] // N_CHIPS, g.shape[1])
    gw_by_chip = {"ffn_w_up": jnp.concatenate([d_up0, d_up1], axis=1),
                  "ffn_w_down": jnp.concatenate([by_rows(d_down0), by_rows(d_down1)], axis=1)}

    d_foa = _matmul(dx1, w_fox_out, mode="nt", out_dtypes=(BF16,), epilogue=_delta_epilogue(fox_vwd, fox_dh),
                    extras=(foa,), name="fox_d_ctx")
    gw["fox_w_out"] = _unpad_heads(_matmul(foa, dx1, mode="tn", out_dtypes=(F32,), name="fox_d_w_out"),
                                   fox_h, fox_dh, 0)[None]
    fox_hps = FOX_BWD_HEADS_PER_STEP if fox_h % FOX_BWD_HEADS_PER_STEP == 0 else 1
    d_fqa, d_fka, d_fva, ds_rows, ds_cols = _flash_bwd(fqb, fka, fva, d_foa, fox_h, fox_hps, name="fox_attn_bwd",
                                                       sum_cols=(fox_dh, fox_dh + 3))
    d_cum = jnp.transpose(ds_rows - ds_cols, (1, 0, 2)).reshape(seq, fox_h)
    d_z = lax.cumsum(d_cum, axis=0, reverse=True) * jax.nn.sigmoid(-z)
    d_b_f = jnp.sum(d_z, axis=0)
    d_w_in = [_unpad_heads(_matmul(h0, g, mode="tn", out_dtypes=(F32,), name=f"fox_d_w_{tag}"), fox_h, fox_dh, 1)
              for tag, g in (("q", d_fqa), ("k", d_fka), ("v", d_fva))]
    d_w_gate = _matmul(h0, d_z, mode="tn", out_dtypes=(F32,), name="fox_d_w_gate")
    gw["fox_w_in"] = jnp.concatenate([d_w_in[0] * fox_scale, d_w_in[1], d_w_in[2], d_w_gate], axis=1)[None]
    d_h0 = _matmul(d_z, w_gate, mode="nt", out_dtypes=(F32,), name="fox_d_h_gate")
    for tag, g, w in (("q", d_fqa, w_fq), ("k", d_fka, w_fk), ("v", d_fva, w_fv)):
        d_h0 = _matmul(g, w, mode="nt", out_dtypes=(F32,), epilogue=add_res, extras=(d_h0,), name=f"fox_d_h_{tag}")
    grad_x, (d_nm0,) = _rms_bwd(xs, [(norm_mix_g[0:1], d_h0)], dx1, name="l0_d_norm_mix")

    c_idx = lax.axis_index("c").astype(jnp.int32).reshape(1)
    parts = []
    for (n, ax), shape in zip(_BIG, shard_shapes):
        if n in gw_by_chip:
            g = gw_by_chip[n]
        else:
            g = gw[n]
            g = jnp.moveaxis(g.reshape(g.shape[:ax] + (N_CHIPS, shape[ax]) + g.shape[ax + 1:]), ax, 0)
            g = g.reshape(N_CHIPS, -1, shape[-1])
        parts.append(jnp.pad(g, ((0, 0), (0, _part_rows(shape) - g.shape[1]), (0, PACK_LANES - shape[-1]))))
    parts.append(jnp.zeros((N_CHIPS, rows - sum(p.shape[1] for p in parts), PACK_LANES), F32))
    g4 = jnp.concatenate(parts, axis=1)
    a4 = _sibling_swap_halves(g4, name="grads_to_sibling")
    s4 = _chip_sum(g4, a4, c_idx, name="grads_chip_sum")
    chip_idx = p_chip.astype(jnp.int32).reshape(1)
    r1 = _rs_first_hop(s4, name="grads_first_hop")
    t_own, onward = _rs_middle(s4, r1, chip_idx, name="grads_middle_sum")
    r2 = _rs_second_hop(onward, name="grads_second_hop")
    t_mine = _rs_last_add(t_own, r2, name="grads_last_sum")
    t_theirs = _sibling_swap(t_mine, name="grads_join_halves")
    is_south = lax.axis_index("c") == 0
    g_big = jnp.concatenate([jnp.where(is_south, t_mine, t_theirs), jnp.where(is_south, t_theirs, t_mine)],
                            axis=0)

    small_local = {"norm_mix_g": jnp.concatenate([d_nm0, d_nm1], axis=0),
                   "norm_ffn_g": jnp.concatenate([d_nf0, d_nf1], axis=0),
                   "fox_b_f": d_b_f[None, :], "kv_norm_g": d_kv_g[0], "mla_kv_a_norm_g": d_kv_a_g[0],
                   "mla_q_a_norm_g": d_q_a_g, "final_norm_g": d_final_g[0]}
    small_shapes = [w_in[n].shape for n in _SMALL]
    small_rows = sum(_part_rows(s, SMALL_PART_ROWS) for s in small_shapes)
    pack_small = lambda arrs: _pack(arrs, small_rows, F32, SMALL_PART_ROWS)
    g_small = _all_reduce_small(pack_small([small_local[n] for n in _SMALL]), name="grads_small")

    grads = dict(zip(big_names, _unpack(g_big, shard_shapes)))
    delta, new_m, new_v = {}, {}, {}
    for n, shape in zip(big_names, shard_shapes):
        flat = lambda a: a.reshape(-1, shape[-1])
        outs = _adamw(flat(w_in[n]), flat(grads[n]), flat(m_in[n]), flat(v_in[n]), name=f"adamw_{n}")
        delta[n], new_m[n], new_v[n] = (o.reshape(shape) for o in outs)
    sm_outs = _adamw(pack_small([w_in[n] for n in _SMALL]), g_small, pack_small([m_in[n] for n in _SMALL]),
                     pack_small([v_in[n] for n in _SMALL]), name="adamw_small")
    for res, packed in zip((grads, delta, new_m, new_v), (g_small,) + tuple(sm_outs)):
        res.update(zip(_SMALL, _unpack(packed, small_shapes, SMALL_PART_ROWS)))

    return (loss, grad_x[None], *[grads[n] for n in _WEIGHTS], *[delta[n] for n in _WEIGHTS],
            *[new_m[n] for n in _WEIGHTS], *[new_v[n] for n in _WEIGHTS])
```

```python
import math

import numpy as np
import jax
import jax.numpy as jnp
from jax import lax
from jax.experimental import pallas as pl
from jax.experimental.pallas import tpu as pltpu

F32 = jnp.float32
BF16 = jnp.bfloat16

FOX_HEADS = 16
MLA_HEADS = 8
QK_NOPE_DIM = 128
ROPE_BASE = 10000.0
EPS = 1e-6

ADAM_LR = 0.001
ADAM_B1 = 0.9
ADAM_B2 = 0.999
ADAM_EPS = 1e-08
ADAM_WD = 0.01
ADAM_STEP = 10

N_CHIPS = 4
N_DEV = 8
PACK_LANES = 1024
PACK_PART_ROWS = 16
SMALL_PART_ROWS = 8
PACK_ROWS_MULT = 1024
VMEM_LIMIT_BYTES = 48 * 1024 * 1024
LANE_TILE = 128
MATMUL_BLOCK = 1024
MATMUL_WIDE_BLOCK = 2048
MATMUL_DEPTH = 2048
ATTN_BLOCK_Q = 1024
ATTN_BLOCK_K = 1024
ATTN_FWD_LANES = 1024
FOX_BWD_HEADS_PER_STEP = 4
MLA_BWD_HEADS_PER_STEP = 2
ATTN_SUB_ROWS = 256
FOX_FWD_SUB_ROWS = (1024, 512)
MLA_FWD_SUB_ROWS = (256, 256)
NEG_BIG = -1e30
MESH = pl.DeviceIdType.MESH


def _round_up(n, m):
    return -(-n // m) * m


def _blk(dim, pref, mult=128):
    if dim <= pref:
        return dim
    b = (pref // mult) * mult
    while b >= mult:
        if dim % b == 0:
            return b
        b -= mult
    return dim


def _params(sem=None):
    return pltpu.CompilerParams(dimension_semantics=sem, vmem_limit_bytes=VMEM_LIMIT_BYTES)


_DIMS = {"nn": (((1,), (0,)), ((), ())), "nt": (((1,), (1,)), ((), ())), "tn": (((0,), (0,)), ((), ()))}


def _matmul(a, b, *, mode, out_dtypes, name, epilogue=None, extras=(), placed=None, by_chip=False):
    if mode == "tn":
        kdim, m = a.shape
    else:
        m, kdim = a.shape
    n = b.shape[0] if mode == "nt" else b.shape[1]
    bm, bk = _blk(m, MATMUL_BLOCK), _blk(kdim, MATMUL_DEPTH)
    bn = _blk(n, MATMUL_WIDE_BLOCK if (mode != "tn" and kdim <= MATMUL_BLOCK) else MATMUL_BLOCK)
    if by_chip:
        bn = _blk(n // N_CHIPS, bn)
    nk = kdim // bk
    n_extra, n_out = len(extras), len(out_dtypes)
    n_placed = 0 if placed is None else 2
    dims = _DIMS[mode]

    def body(a_ref, b_ref, *rest):
        placed_refs = rest[:n_placed]
        rest = rest[n_placed:]
        extra_refs = rest[:n_extra]
        out_refs = rest[n_extra:n_extra + n_out]

        def finish(acc):
            if n_placed:
                acc = acc + lax.dot_general(placed_refs[0][...], placed_refs[1][...], _DIMS["nn"],
                                            preferred_element_type=F32)
            res = (acc,) if epilogue is None else epilogue(acc, *[r[...] for r in extra_refs])
            for o_ref, r in zip(out_refs, res):
                o_ref[...] = r.astype(o_ref.dtype)

        part = lax.dot_general(a_ref[...].astype(BF16), b_ref[...].astype(BF16), dims, preferred_element_type=F32)
        if nk == 1:
            finish(part)
            return
        acc_ref = rest[n_extra + n_out]
        k = pl.program_id(2)

        @pl.when(k == 0)
        def _():
            acc_ref[...] = part

        @pl.when((k > 0) & (k < nk - 1))
        def _():
            acc_ref[...] += part

        @pl.when(k == nk - 1)
        def _():
            finish(acc_ref[...] + part)

    if mode == "tn":
        a_spec = pl.BlockSpec((bk, bm), lambda i, j, k: (k, i))
    else:
        a_spec = pl.BlockSpec((bm, bk), lambda i, j, k: (i, k))
    if mode == "nt":
        b_spec = pl.BlockSpec((bn, bk), lambda i, j, k: (j, k))
    else:
        b_spec = pl.BlockSpec((bk, bn), lambda i, j, k: (k, j))
    tile = pl.BlockSpec((bm, bn), lambda i, j, k: (i, j))
    placed_specs = []
    if n_placed:
        k2 = placed[0].shape[1]
        placed_specs = [pl.BlockSpec((bm, k2), lambda i, j, k: (i, 0)), pl.BlockSpec((k2, bn), lambda i, j, k: (0, j))]
    out_tile, out_dims = tile, (m, n)
    if by_chip:
        per_chip = n // N_CHIPS // bn
        out_tile = pl.BlockSpec((None, bm, bn), lambda i, j, k: (j // per_chip, i, j % per_chip))
        out_dims = (N_CHIPS, m, n // N_CHIPS)
    outs = pl.pallas_call(
        body, name=name,
        grid=(m // bm, n // bn, nk),
        in_specs=[a_spec, b_spec] + placed_specs + [tile] * n_extra,
        out_specs=[out_tile] * n_out,
        out_shape=[jax.ShapeDtypeStruct(out_dims, dt) for dt in out_dtypes],
        scratch_shapes=[pltpu.VMEM((bm, bn), F32)] if nk > 1 else [],
        compiler_params=_params(("parallel", "parallel", "arbitrary")),
    )(a, b, *(placed or ()), *extras)
    return outs[0] if n_out == 1 else outs


def _rms_fwd(x, gains, name):
    s = x.shape[0]
    g, w = gains.shape
    bs = _blk(s, 512, 8)

    def body(x_ref, g_ref, *out_refs):
        xv = x_ref[...]
        y = xv * lax.rsqrt(jnp.mean(xv * xv, axis=-1, keepdims=True) + EPS)
        for i, o_ref in enumerate(out_refs):
            o_ref[...] = (y * g_ref[i:i + 1, :]).astype(o_ref.dtype)

    row = pl.BlockSpec((bs, w), lambda i: (i, 0))
    outs = pl.pallas_call(
        body, name=name, grid=(s // bs,),
        in_specs=[row, pl.BlockSpec((g, w), lambda i: (0, 0))],
        out_specs=[row] * g,
        out_shape=[jax.ShapeDtypeStruct((s, w), BF16)] * g,
        compiler_params=_params(("parallel",)),
    )(x, gains)
    return outs


def _rms_bwd(x, branches, resid, name):
    s = x.shape[0]
    w = branches[0][0].shape[1]
    nb = len(branches)
    bs = _blk(s, 512, 8)
    has_resid = resid is not None

    def body(x_ref, *rest):
        g_refs = rest[:nb]
        dy_refs = rest[nb:2 * nb]
        pos = 2 * nb
        r_ref = rest[pos] if has_resid else None
        pos += int(has_resid)
        dx_ref = rest[pos]
        dg_refs = rest[pos + 1:pos + 1 + nb]
        i = pl.program_id(0)

        @pl.when(i == 0)
        def _():
            for dg_ref in dg_refs:
                dg_ref[...] = jnp.zeros_like(dg_ref)

        xv = x_ref[...]
        rstd = lax.rsqrt(jnp.mean(xv * xv, axis=-1, keepdims=True) + EPS)
        xhat = xv * rstd
        dx = r_ref[...] if has_resid else jnp.zeros_like(xv)
        for g_ref, dy_ref, dg_ref in zip(g_refs, dy_refs, dg_refs):
            dy = dy_ref[...].astype(F32)
            dyg = dy * g_ref[...]
            dx = dx + rstd * (dyg - xhat * jnp.mean(dyg * xhat, axis=-1, keepdims=True))
            dg_ref[...] += jnp.sum(dy * xhat, axis=0, keepdims=True)
        dx_ref[...] = dx

    row = pl.BlockSpec((bs, w), lambda i: (i, 0))
    vec = pl.BlockSpec((1, w), lambda i: (0, 0))
    args = [x] + [g for g, _ in branches] + [dy for _, dy in branches] + ([resid] if has_resid else [])
    outs = pl.pallas_call(
        body, name=name, grid=(s // bs,),
        in_specs=[row] + [vec] * nb + [row] * nb + ([row] if has_resid else []),
        out_specs=[row] + [vec] * nb,
        out_shape=[jax.ShapeDtypeStruct((s, w), F32)] + [jax.ShapeDtypeStruct((1, w), F32)] * nb,
        compiler_params=_params(("arbitrary",)),
    )(*args)
    return outs[0], list(outs[1:])


def _loss_head(x, g, target, name):
    s, w = x.shape
    bs = _blk(s, 512, 8)

    def body(x_ref, g_ref, t_ref, loss_ref, dx_ref, dg_ref):
        i = pl.program_id(0)

        @pl.when(i == 0)
        def _():
            loss_ref[...] = jnp.zeros_like(loss_ref)
            dg_ref[...] = jnp.zeros_like(dg_ref)

        xv = x_ref[...]
        gv = g_ref[...]
        rstd = lax.rsqrt(jnp.mean(xv * xv, axis=-1, keepdims=True) + EPS)
        xhat = xv * rstd
        err = xhat * gv - t_ref[...]
        loss_ref[...] += 0.5 * jnp.sum(jnp.mean(err * err, axis=-1, keepdims=True))
        dy = err * (1.0 / w)
        dyg = dy * gv
        dx_ref[...] = rstd * (dyg - xhat * jnp.mean(dyg * xhat, axis=-1, keepdims=True))
        dg_ref[...] += jnp.sum(dy * xhat, axis=0, keepdims=True)

    row = pl.BlockSpec((bs, w), lambda i: (i, 0))
    vec = pl.BlockSpec((1, w), lambda i: (0, 0))
    return pl.pallas_call(
        body, name=name, grid=(s // bs,),
        in_specs=[row, vec, row],
        out_specs=[pl.BlockSpec((8, 128), lambda i: (0, 0)), row, vec],
        out_shape=[jax.ShapeDtypeStruct((8, 128), F32), jax.ShapeDtypeStruct((s, w), F32),
                   jax.ShapeDtypeStruct((1, w), F32)],
        compiler_params=_params(("arbitrary",)),
    )(x, g, target)


def _rope(a, b, cos, sin, sign, name):
    g, s, w = a.shape
    bs = _blk(s, 1024, 8)

    def body(a_ref, b_ref, c_ref, s_ref, o1_ref, o2_ref):
        av = jnp.sum(a_ref[...].astype(F32), axis=0)
        bv = jnp.sum(b_ref[...].astype(F32), axis=0)
        cv, sv = c_ref[...], s_ref[...] * sign
        o1_ref[...] = av * cv - bv * sv
        o2_ref[...] = bv * cv + av * sv

    grp = pl.BlockSpec((g, bs, w), lambda i: (0, i, 0))
    row = pl.BlockSpec((bs, w), lambda i: (i, 0))
    return pl.pallas_call(
        body, name=name, grid=(s // bs,),
        in_specs=[grp, grp, row, row], out_specs=[row, row],
        out_shape=[jax.ShapeDtypeStruct((s, w), F32)] * 2,
        compiler_params=_params(("parallel",)),
    )(a, b, cos, sin)


def _causal_table(s, bq, bk, q_major):
    nq, nk = s // bq, s // bk
    rows = []
    if q_major:
        for qi in range(nq):
            kmax = (qi * bq + bq - 1) // bk
            for ki in range(kmax + 1):
                rows.append((qi, ki, int(ki * bk + bk - 1 > qi * bq), int(ki == 0), int(ki == kmax)))
    else:
        for ki in range(nk):
            qmin = (ki * bk) // bq
            for qi in range(qmin, nq):
                rows.append((qi, ki, int(ki * bk + bk - 1 > qi * bq), int(qi == qmin), int(qi == nq - 1)))
    return jnp.asarray(np.array(rows, np.int32).T)


def _causal_keep(q0, k0, nq, nk, transposed):
    if transposed:
        kpos = k0 + lax.broadcasted_iota(jnp.int32, (nk, nq), 0)
        qpos = q0 + lax.broadcasted_iota(jnp.int32, (nk, nq), 1)
    else:
        qpos = q0 + lax.broadcasted_iota(jnp.int32, (nq, nk), 0)
        kpos = k0 + lax.broadcasted_iota(jnp.int32, (nq, nk), 1)
    return kpos <= qpos


def _sub_tiles(n_rows, n_cols, masked, square, rows_are_keys, sub_rows):
    sub = min(sub_rows, n_rows)
    out = []
    for r0 in range(0, n_rows, sub):
        if masked and square:
            c0, nc = (r0, n_cols - r0) if rows_are_keys else (0, r0 + sub)
        else:
            c0, nc = 0, n_cols
        out.append((r0, sub, c0, nc))
    return out


_NT = (((1,), (1,)), ((), ()))
_NN = (((1,), (0,)), ((), ()))


def _attn_specs(bq, bk):
    qspec = lambda d: pl.BlockSpec((bq, d), lambda hh, t, tb: (tb[0, t], hh))
    kspec = lambda d: pl.BlockSpec((bk, d), lambda hh, t, tb: (tb[1, t], hh))
    return qspec, kspec


def _split3_cols(x):
    hi = x.astype(BF16).astype(F32)
    rest = x - hi
    mid = rest.astype(BF16).astype(F32)
    lo = (rest - mid).astype(BF16).astype(F32)
    return hi, mid, lo


def _place3(base, col, pieces, sign):
    lane = lax.broadcasted_iota(jnp.int32, base.shape, 1)
    out = base.astype(F32)
    for i, piece in enumerate(pieces):
        out = jnp.where(lane == col + i, sign * piece, out)
    return out.astype(BF16)


def _flash_fwd(qa, ka, va, heads, l_col, lse_col, sub_rows, name):
    s = qa.shape[0]
    da, dv = qa.shape[1] // heads, va.shape[1] // heads
    hps = max(n for n in range(1, ATTN_FWD_LANES // max(da, dv) + 1) if heads % n == 0)
    bq, bk = _blk(s, ATTN_BLOCK_Q), _blk(s, ATTN_BLOCK_K)
    tab = _causal_table(s, bq, bk, True)

    def body(tab_ref, q_ref, k_ref, v_ref, o_ref, qb_ref, m_sc, acc_sc):
        t = pl.program_id(1)
        qi, ki = tab_ref[0, t], tab_ref[1, t]

        @pl.when(tab_ref[3, t] == 1)
        def _():
            m_sc[...] = jnp.full_like(m_sc, NEG_BIG)
            acc_sc[...] = jnp.zeros_like(acc_sc)

        def step(masked):
            for hh in range(hps):
                qc, vc = slice(hh * da, (hh + 1) * da), slice(hh * dv, (hh + 1) * dv)
                for r0, nr, c0, nc in _sub_tiles(bq, bk, masked, bq == bk, False, sub_rows[int(masked)]):
                    sc = lax.dot_general(q_ref[r0:r0 + nr, qc], k_ref[c0:c0 + nc, qc], _NT,
                                         preferred_element_type=F32)
                    if masked:
                        sc = jnp.where(_causal_keep(qi * bq + r0, ki * bk + c0, nr, nc, False), sc, NEG_BIG)
                    m_prev = m_sc[hh, r0:r0 + nr, :]
                    m_new = jnp.maximum(m_prev, jnp.max(sc, axis=-1, keepdims=True))
                    p = jnp.exp(sc - m_new).astype(BF16)
                    acc_sc[r0:r0 + nr, vc] = jnp.exp(m_prev - m_new) * acc_sc[r0:r0 + nr, vc] + lax.dot_general(
                        p, v_ref[c0:c0 + nc, vc], _NN, preferred_element_type=F32)
                    m_sc[hh, r0:r0 + nr, :] = m_new

        @pl.when(tab_ref[2, t] == 1)
        def _():
            step(True)

        @pl.when(tab_ref[2, t] == 0)
        def _():
            step(False)

        @pl.when(tab_ref[4, t] == 1)
        def _():
            for hh in range(hps):
                qc, vc = slice(hh * da, (hh + 1) * da), slice(hh * dv, (hh + 1) * dv)
                acc = acc_sc[:, vc]
                lane = lax.broadcasted_iota(jnp.int32, acc.shape, 1)
                l = jnp.sum(jnp.where(lane == l_col, acc, 0.0), axis=-1, keepdims=True)
                o_ref[:, vc] = (acc / l).astype(o_ref.dtype)
                lse = m_sc[hh] + jnp.log(l)
                qb_ref[:, qc] = _place3(q_ref[:, qc], lse_col, _split3_cols(lse), -1.0)

    qspec, kspec = _attn_specs(bq, bk)
    return pl.pallas_call(
        body, name=name,
        grid_spec=pltpu.PrefetchScalarGridSpec(
            num_scalar_prefetch=1, grid=(heads // hps, tab.shape[1]),
            in_specs=[qspec(hps * da), kspec(hps * da), kspec(hps * dv)],
            out_specs=[qspec(hps * dv), qspec(hps * da)],
            scratch_shapes=[pltpu.VMEM((hps, bq, 1), F32), pltpu.VMEM((bq, hps * dv), F32)]),
        out_shape=[jax.ShapeDtypeStruct((s, heads * dv), BF16), jax.ShapeDtypeStruct((s, heads * da), BF16)],
        compiler_params=_params(("parallel", "arbitrary")),
    )(tab, qa, ka, va)


def _delta_epilogue(dv, delta_col):
    def epilogue(acc, o_tile):
        heads_out = []
        for hh in range(acc.shape[1] // dv):
            vc = slice(hh * dv, (hh + 1) * dv)
            dov = acc[:, vc].astype(BF16)
            delta = jnp.sum(dov.astype(F32) * o_tile[:, vc].astype(F32), axis=-1, keepdims=True)
            heads_out.append(_place3(dov, delta_col, _split3_cols(delta), 1.0))
        return (jnp.concatenate(heads_out, axis=1),)
    return epilogue


_TN =(((0,), (0,)), ((), ()))


def _flash_bwd(qa, ka, va, doa, heads, hps, name, sum_cols=None):
    s = qa.shape[0]
    da, dv = qa.shape[1] // heads, va.shape[1] // heads
    h = heads // hps
    bq, bk = _blk(s, ATTN_BLOCK_Q), _blk(s, ATTN_BLOCK_K)
    tab = _causal_table(s, bq, bk, False)
    n_tiles = tab.shape[1]
    n_sum = 0 if sum_cols is None else 2

    def head_column(acc, col):
        out = jnp.zeros((acc.shape[0], hps), F32)
        lane = lax.broadcasted_iota(jnp.int32, (acc.shape[0], da), 1)
        pick = lax.broadcasted_iota(jnp.int32, out.shape, 1)
        for hh in range(hps):
            val = jnp.sum(jnp.where(lane == col, acc[:, hh * da:(hh + 1) * da], 0.0), axis=-1, keepdims=True)
            out = jnp.where(pick == hh, val, out)
        return out

    def body(tab_ref, q_ref, k_ref, v_ref, do_ref, dq_ref, dk_ref, dv_ref, *rest):
        sum_refs, (dk_sc, dv_sc) = rest[:n_sum], rest[n_sum:]
        t = pl.program_id(1)
        qi, ki = tab_ref[0, t], tab_ref[1, t]

        @pl.when(t == 0)
        def _():
            dq_ref[...] = jnp.zeros_like(dq_ref)

        @pl.when(tab_ref[3, t] == 1)
        def _():
            dk_sc[...] = jnp.zeros_like(dk_sc)
            dv_sc[...] = jnp.zeros_like(dv_sc)

        def step(masked):
            for hh in range(hps):
                qc, vc = slice(hh * da, (hh + 1) * da), slice(hh * dv, (hh + 1) * dv)
                for r0, nr, c0, nc in _sub_tiles(bk, bq, masked, bq == bk, True, ATTN_SUB_ROWS):
                    qv, dov, kv = q_ref[c0:c0 + nc, qc], do_ref[c0:c0 + nc, vc], k_ref[r0:r0 + nr, qc]
                    st = lax.dot_general(kv, qv, _NT, preferred_element_type=F32)
                    if masked:
                        st = jnp.where(_causal_keep(qi * bq + c0, ki * bk + r0, nc, nr, True), st, NEG_BIG)
                    pt = jnp.exp(st)
                    dv_sc[r0:r0 + nr, vc] += lax.dot_general(pt.astype(BF16), dov, _NN, preferred_element_type=F32)
                    dpt = lax.dot_general(v_ref[r0:r0 + nr, vc], dov, _NT, preferred_element_type=F32)
                    dst = (pt * dpt).astype(BF16)
                    dk_sc[r0:r0 + nr, qc] += lax.dot_general(dst, qv, _NN, preferred_element_type=F32)
                    q_rows = pl.ds(pl.multiple_of(qi * bq + c0, ATTN_SUB_ROWS), nc)
                    dq_ref[q_rows, qc] += lax.dot_general(dst, kv, _TN, preferred_element_type=F32)

        @pl.when(tab_ref[2, t] == 1)
        def _():
            step(True)

        @pl.when(tab_ref[2, t] == 0)
        def _():
            step(False)

        @pl.when(tab_ref[4, t] == 1)
        def _():
            dk_ref[...] = dk_sc[...].astype(dk_ref.dtype)
            dv_ref[...] = dv_sc[...].astype(dv_ref.dtype)
            if n_sum:
                sum_refs[1][...] = head_column(dk_sc[...], sum_cols[1])

        if n_sum:
            @pl.when(t == n_tiles - 1)
            def _():
                sum_refs[0][...] = head_column(dq_ref[...], sum_cols[0])

    qspec, kspec = _attn_specs(bq, bk)
    out_specs = [pl.BlockSpec((s, hps * da), lambda hh, t, tb: (0, hh), pipeline_mode=pl.Buffered(1)),
                 kspec(hps * da), kspec(hps * dv)]
    out_shape = [jax.ShapeDtypeStruct((s, heads * da), F32), jax.ShapeDtypeStruct((s, heads * da), BF16),
                 jax.ShapeDtypeStruct((s, heads * dv), BF16)]
    if n_sum:
        out_specs += [pl.BlockSpec((None, s, hps), lambda hh, t, tb: (hh, 0, 0), pipeline_mode=pl.Buffered(1)),
                      pl.BlockSpec((None, bk, hps), lambda hh, t, tb: (hh, tb[1, t], 0))]
        out_shape += [jax.ShapeDtypeStruct((h, s, hps), F32)] * 2
    return pl.pallas_call(
        body, name=name,
        grid_spec=pltpu.PrefetchScalarGridSpec(
            num_scalar_prefetch=1, grid=(h, n_tiles),
            in_specs=[qspec(hps * da), kspec(hps * da), kspec(hps * dv), qspec(hps * dv)],
            out_specs=out_specs,
            scratch_shapes=[pltpu.VMEM((bk, hps * da), F32), pltpu.VMEM((bk, hps * dv), F32)]),
        out_shape=out_shape,
        compiler_params=_params(("parallel", "arbitrary")),
    )(tab, qa, ka, va, doa)


def _split3(x):
    hi = lax.reduce_precision(x, 8, 7)
    rest = x - hi
    mid = lax.reduce_precision(rest, 8, 7)
    lo = lax.reduce_precision(rest - mid, 8, 7)
    return jnp.stack([hi, mid, lo], axis=-1).astype(BF16)


def _pad_heads(w, heads, width, axis):
    shape = list(w.shape)
    d = shape[axis] // heads
    w = w.reshape(shape[:axis] + [heads, d] + shape[axis + 1:])
    pad = [(0, 0)] * w.ndim
    pad[axis + 1] = (0, width - d)
    return jnp.pad(w, pad).reshape(shape[:axis] + [heads * width] + shape[axis + 1:])


def _unpad_heads(w, heads, d, axis):
    shape = list(w.shape)
    width = shape[axis] // heads
    w = w.reshape(shape[:axis] + [heads, width] + shape[axis + 1:])
    w = lax.slice_in_dim(w, 0, d, axis=axis + 1)
    return w.reshape(shape[:axis] + [heads * d] + shape[axis + 1:])


def _placement(rows, heads, width, entries):
    e = np.zeros((rows, heads * width), np.float32)
    for row, col, val in entries:
        for hh in range(heads):
            e[row(hh) if callable(row) else row, hh * width + col] = val
    return jnp.asarray(e, BF16)


def _rope_mix(a, b, cos_t, sin_t, scale, heads, name):
    s = a.shape[0]
    d = a.shape[1] // heads
    bs = _blk(s, 1024, 8)

    def body(a_ref, b_ref, c_ref, s_ref, o_ref):
        o_ref[...] = ((a_ref[...] * c_ref[...] + b_ref[...] * s_ref[...]) * scale).astype(o_ref.dtype)

    blk = pl.BlockSpec((bs, d), lambda i, hh: (i, hh))
    tbl = pl.BlockSpec((bs, d), lambda i, hh: (i, 0))
    return pl.pallas_call(
        body, name=name, grid=(s // bs, heads), in_specs=[blk, blk, tbl, tbl], out_specs=blk,
        out_shape=jax.ShapeDtypeStruct(a.shape, BF16),
        compiler_params=_params(("parallel", "parallel")),
    )(a, b, cos_t, sin_t)


def _rope_proj(x, w_a, w_b, cos_t, sin_t, scale, heads, name):
    s, kdim = x.shape
    d = w_a.shape[1] // heads
    hpt = max(n for n in range(1, max(1, MATMUL_BLOCK // d) + 1) if heads % n == 0)
    bm = _blk(s, MATMUL_BLOCK)
    cos_w, sin_w = jnp.tile(cos_t, (1, hpt)), jnp.tile(sin_t, (1, hpt))

    def body(x_ref, wa_ref, wb_ref, c_ref, s_ref, o_ref):
        xv = x_ref[...]
        a = lax.dot_general(xv, wa_ref[...], _NN, preferred_element_type=F32)
        b = lax.dot_general(xv, wb_ref[...], _NN, preferred_element_type=F32)
        o_ref[...] = ((a * c_ref[...] + b * s_ref[...]) * scale).astype(o_ref.dtype)

    wide = pl.BlockSpec((bm, hpt * d), lambda i, j: (i, j))
    tbl = pl.BlockSpec((bm, hpt * d), lambda i, j: (i, 0))
    wgt = pl.BlockSpec((kdim, hpt * d), lambda i, j: (0, j))
    return pl.pallas_call(
        body, name=name, grid=(s // bm, heads // hpt),
        in_specs=[pl.BlockSpec((bm, kdim), lambda i, j: (i, 0)), wgt, wgt, tbl, tbl], out_specs=wide,
        out_shape=jax.ShapeDtypeStruct((s, heads * d), BF16),
        compiler_params=_params(("parallel", "parallel")),
    )(x, w_a, w_b, cos_w, sin_w)


def _rope_proj_bwd(x, g, w_a, w_b, cos_t, sin_t, scale, heads, name):
    s, kdim = x.shape
    d = w_a.shape[1] // heads
    hpt = max(n for n in range(1, max(1, MATMUL_BLOCK // d) + 1) if heads % n == 0)
    bs = _blk(s, MATMUL_BLOCK)
    n_tiles, n_rows = heads // hpt, s // bs
    cos_w, sin_w = jnp.tile(cos_t, (1, hpt)), jnp.tile(sin_t, (1, hpt))

    def halves(g_ref, c_ref, s_ref):
        gv = g_ref[...] * scale
        return (gv * c_ref[...]).astype(BF16), (gv * s_ref[...]).astype(BF16)

    def dx_body(g_ref, wa_ref, wb_ref, c_ref, s_ref, dx_ref, acc_ref):
        j = pl.program_id(1)
        ga, gb = halves(g_ref, c_ref, s_ref)
        part = (lax.dot_general(ga, wa_ref[...], _NT, preferred_element_type=F32)
                + lax.dot_general(gb, wb_ref[...], _NT, preferred_element_type=F32))

        @pl.when(j == 0)
        def _():
            acc_ref[...] = part

        @pl.when(j > 0)
        def _():
            acc_ref[...] += part

        @pl.when(j == n_tiles - 1)
        def _():
            dx_ref[...] = acc_ref[...]

    def dw_body(x_ref, g_ref, c_ref, s_ref, dwa_ref, dwb_ref, acc_a, acc_b):
        i = pl.program_id(1)
        ga, gb = halves(g_ref, c_ref, s_ref)
        xv = x_ref[...]
        pa = lax.dot_general(xv, ga, _TN, preferred_element_type=F32)
        pb = lax.dot_general(xv, gb, _TN, preferred_element_type=F32)

        @pl.when(i == 0)
        def _():
            acc_a[...] = pa
            acc_b[...] = pb

        @pl.when(i > 0)
        def _():
            acc_a[...] += pa
            acc_b[...] += pb

        @pl.when(i == n_rows - 1)
        def _():
            dwa_ref[...] = acc_a[...]
            dwb_ref[...] = acc_b[...]

    dx = pl.pallas_call(
        dx_body, name=name + "_dx", grid=(n_rows, n_tiles),
        in_specs=[pl.BlockSpec((bs, hpt * d), lambda i, j: (i, j)),
                  pl.BlockSpec((kdim, hpt * d), lambda i, j: (0, j)), pl.BlockSpec((kdim, hpt * d), lambda i, j: (0, j)),
                  pl.BlockSpec((bs, hpt * d), lambda i, j: (i, 0)), pl.BlockSpec((bs, hpt * d), lambda i, j: (i, 0))],
        out_specs=pl.BlockSpec((bs, kdim), lambda i, j: (i, 0)),
        out_shape=jax.ShapeDtypeStruct((s, kdim), F32),
        scratch_shapes=[pltpu.VMEM((bs, kdim), F32)],
        compiler_params=_params(("parallel", "arbitrary")),
    )(g, w_a, w_b, cos_w, sin_w)
    dwa, dwb = pl.pallas_call(
        dw_body, name=name + "_dw", grid=(n_tiles, n_rows),
        in_specs=[pl.BlockSpec((bs, kdim), lambda j, i: (i, 0)),
                  pl.BlockSpec((bs, hpt * d), lambda j, i: (i, j)),
                  pl.BlockSpec((bs, hpt * d), lambda j, i: (i, 0)), pl.BlockSpec((bs, hpt * d), lambda j, i: (i, 0))],
        out_specs=[pl.BlockSpec((kdim, hpt * d), lambda j, i: (0, j))] * 2,
        out_shape=[jax.ShapeDtypeStruct((kdim, heads * d), F32)] * 2,
        scratch_shapes=[pltpu.VMEM((kdim, hpt * d), F32)] * 2,
        compiler_params=_params(("parallel", "arbitrary")),
    )(x, g, cos_w, sin_w)
    return dx, dwa, dwb


def _adamw(w, g, m, v, name):
    r, wd = w.shape
    br = _blk(r, 512, 8)

    def body(w_ref, g_ref, m_ref, v_ref, d_ref, nm_ref, nv_ref):
        gv = g_ref[...]
        mn = ADAM_B1 * m_ref[...] + (1.0 - ADAM_B1) * gv
        vn = ADAM_B2 * v_ref[...] + (1.0 - ADAM_B2) * (gv * gv)
        m_hat = mn / (1.0 - ADAM_B1 ** ADAM_STEP)
        v_hat = vn / (1.0 - ADAM_B2 ** ADAM_STEP)
        d_ref[...] = -ADAM_LR * (m_hat / (jnp.sqrt(v_hat) + ADAM_EPS) + ADAM_WD * w_ref[...])
        nm_ref[...] = mn
        nv_ref[...] = vn

    row = pl.BlockSpec((br, wd), lambda i: (i, 0))
    return pl.pallas_call(
        body, name=name, grid=(r // br,), in_specs=[row] * 4, out_specs=[row] * 3,
        out_shape=[jax.ShapeDtypeStruct((r, wd), F32)] * 3,
        compiler_params=_params(("parallel",)),
    )(w, g, m, v)


_ANY = pl.BlockSpec(memory_space=pl.ANY)


def _place():
    return lax.axis_index("x"), lax.axis_index("y"), lax.axis_index("c"), None


def _all_gather_shards(shard, name):
    r, w = shard.shape
    hr = r // 2
    qr = hr // 2

    def body(x_ref, out_ref, send_sems, recv_sems):
        x, y, c, _ = _place()
        me, sibling, y_nbr, x_nbr = (x, y, c), (x, y, 1 - c), (x, 1 - y, c), (1 - x, y, c)

        def rows(j, half, piece=None):
            if piece is None:
                return out_ref.at[j, pl.ds(pl.multiple_of(half * hr, 16), hr), :]
            return out_ref.at[j, pl.ds(pl.multiple_of(half * hr + piece * qr, 16), qr), :]

        def mine(piece):
            return x_ref.at[pl.ds(pl.multiple_of(c * hr + piece * qr, 16), qr), :]

        def copy(sem, src, dst, to):
            return pltpu.make_async_remote_copy(src_ref=src, dst_ref=dst, send_sem=send_sems.at[sem],
                                                recv_sem=recv_sems.at[sem], device_id=to, device_id_type=MESH)

        sent = [copy(0, mine(0), rows(0, c, 0), y_nbr), copy(1, mine(1), rows(0, c, 1), y_nbr),
                copy(2, mine(0), rows(1, c, 0), x_nbr), copy(3, mine(1), rows(1, c, 1), x_nbr)]
        for cp in sent:
            cp.start()

        def landed(sem, ref):
            copy(sem, ref, ref, me).wait_recv()

        def pass_on(sem, src, dst, to):
            cp = copy(sem, src, dst, to)
            cp.start()
            sent.append(cp)

        landed(2, rows(1, c, 0))
        pass_on(4, rows(1, c, 0), rows(2, c, 0), y_nbr)
        landed(1, rows(0, c, 1))
        pass_on(5, rows(0, c, 1), rows(2, c, 1), x_nbr)
        landed(0, rows(0, c, 0))
        pass_on(6, rows(0, c), rows(0, c), sibling)
        landed(3, rows(1, c, 1))
        pass_on(7, rows(1, c), rows(1, c), sibling)
        landed(4, rows(2, c, 0))
        landed(5, rows(2, c, 1))
        pass_on(8, rows(2, c), rows(2, c), sibling)
        for j in range(3):
            landed(6 + j, rows(j, 1 - c))
        for cp in sent:
            cp.wait_send()

    return pl.pallas_call(
        body, name=name, in_specs=[_ANY], out_specs=_ANY,
        out_shape=jax.ShapeDtypeStruct((N_CHIPS - 1, r, w), shard.dtype),
        scratch_shapes=[pltpu.SemaphoreType.DMA((9,)), pltpu.SemaphoreType.DMA((9,))],
        compiler_params=pltpu.CompilerParams(vmem_limit_bytes=VMEM_LIMIT_BYTES),
    )(shard)


def _sibling_swap_halves(g, name):
    nq, r, w = g.shape
    hr = r // 2

    def body(g_ref, a_ref, send_sems, recv_sems):
        x, y, c, _ = _place()
        sibling = (x, y, 1 - c)
        cps = []
        for q in range(nq):
            cp = pltpu.make_async_remote_copy(
                src_ref=g_ref.at[q, pl.ds(pl.multiple_of((1 - c) * hr, 8), hr), :], dst_ref=a_ref.at[q],
                send_sem=send_sems.at[q], recv_sem=recv_sems.at[q], device_id=sibling, device_id_type=MESH)
            cp.start()
            cps.append(cp)
        for cp in cps:
            cp.wait()

    return pl.pallas_call(
        body, name=name, in_specs=[_ANY], out_specs=_ANY,
        out_shape=jax.ShapeDtypeStruct((nq, hr, w), g.dtype),
        scratch_shapes=[pltpu.SemaphoreType.DMA((nq,)), pltpu.SemaphoreType.DMA((nq,))],
        compiler_params=pltpu.CompilerParams(vmem_limit_bytes=VMEM_LIMIT_BYTES),
    )(g)


def _chip_sum(g, a, c_idx, name):
    nq, r, w = g.shape
    hr = r // 2
    br = _blk(hr, 512, 16)
    nb = hr // br

    def body(c_ref, g_ref, a_ref, o_ref):
        o_ref[...] = (g_ref[...] + a_ref[...]).astype(o_ref.dtype)

    return pl.pallas_call(
        body, name=name,
        grid_spec=pltpu.PrefetchScalarGridSpec(
            num_scalar_prefetch=1, grid=(nq, nb),
            in_specs=[pl.BlockSpec((None, br, w), lambda q, i, cr: (q, cr[0] * nb + i, 0)),
                      pl.BlockSpec((None, br, w), lambda q, i, cr: (q, i, 0))],
            out_specs=pl.BlockSpec((None, br, w), lambda q, i, cr: (q, i, 0))),
        out_shape=jax.ShapeDtypeStruct((nq, hr, w), BF16),
        compiler_params=_params(("parallel", "parallel")),
    )(c_idx, g, a)


def _rs_first_hop(s4, name):
    nq, hr, w = s4.shape
    qr = hr // 2

    def body(s_ref, r_ref, send_sems, recv_sems):
        x, y, c, _ = _place()
        p = 2 * x + y
        x_nbr, y_nbr = (1 - x, y, c), (x, 1 - y, c)

        def piece(q, k):
            return s_ref.at[q, pl.ds(k * qr, qr), :]

        sends = [(piece(p ^ 2, 0), x_nbr), (piece(p ^ 3, 0), x_nbr), (piece(p ^ 1, 1), y_nbr), (piece(p ^ 3, 1), y_nbr)]
        cps = []
        for k, (src, to) in enumerate(sends):
            cp = pltpu.make_async_remote_copy(src_ref=src, dst_ref=r_ref.at[k], send_sem=send_sems.at[k],
                                              recv_sem=recv_sems.at[k], device_id=to, device_id_type=MESH)
            cp.start()
            cps.append(cp)
        for cp in cps:
            cp.wait()

    return pl.pallas_call(
        body, name=name, in_specs=[_ANY], out_specs=_ANY,
        out_shape=jax.ShapeDtypeStruct((nq, qr, w), s4.dtype),
        scratch_shapes=[pltpu.SemaphoreType.DMA((nq,)), pltpu.SemaphoreType.DMA((nq,))],
        compiler_params=pltpu.CompilerParams(vmem_limit_bytes=VMEM_LIMIT_BYTES),
    )(s4)


def _rs_middle(s4, r1, chip_idx, name):
    _, hr, w = s4.shape
    qr = hr // 2
    br = _blk(qr, 512, 16)
    nb = qr // br

    def body(idx_ref, mine_ref, theirs_ref, got_mine_ref, got_theirs_ref, own_ref, onward_ref):
        own_ref[...] = mine_ref[...].astype(F32) + got_mine_ref[...].astype(F32)
        onward_ref[...] = (theirs_ref[...].astype(F32) + got_theirs_ref[...].astype(F32)).astype(onward_ref.dtype)

    return pl.pallas_call(
        body, name=name,
        grid_spec=pltpu.PrefetchScalarGridSpec(
            num_scalar_prefetch=1, grid=(2, nb),
            in_specs=[pl.BlockSpec((None, br, w), lambda k, i, ix: (ix[0], k * nb + i, 0)),
                      pl.BlockSpec((None, br, w), lambda k, i, ix: (ix[0] ^ (k + 1), k * nb + i, 0)),
                      pl.BlockSpec((None, br, w), lambda k, i, ix: (2 * k, i, 0)),
                      pl.BlockSpec((None, br, w), lambda k, i, ix: (2 * k + 1, i, 0))],
            out_specs=[pl.BlockSpec((br, w), lambda k, i, ix: (k * nb + i, 0)),
                       pl.BlockSpec((None, br, w), lambda k, i, ix: (k, i, 0))]),
        out_shape=[jax.ShapeDtypeStruct((hr, w), F32), jax.ShapeDtypeStruct((2, qr, w), s4.dtype)],
        compiler_params=_params(("parallel", "parallel")),
    )(chip_idx, s4, s4, r1, r1)


def _rs_second_hop(onward, name):
    def body(u_ref, r_ref, send_sems, recv_sems):
        x, y, c, _ = _place()
        cps = []
        for k, to in enumerate([(x, 1 - y, c), (1 - x, y, c)]):
            cp = pltpu.make_async_remote_copy(src_ref=u_ref.at[k], dst_ref=r_ref.at[k], send_sem=send_sems.at[k],
                                              recv_sem=recv_sems.at[k], device_id=to, device_id_type=MESH)
            cp.start()
            cps.append(cp)
        for cp in cps:
            cp.wait()

    return pl.pallas_call(
        body, name=name, in_specs=[_ANY], out_specs=_ANY,
        out_shape=jax.ShapeDtypeStruct(onward.shape, onward.dtype),
        scratch_shapes=[pltpu.SemaphoreType.DMA((2,)), pltpu.SemaphoreType.DMA((2,))],
        compiler_params=pltpu.CompilerParams(vmem_limit_bytes=VMEM_LIMIT_BYTES),
    )(onward)


def _rs_last_add(own, r2, name):
    hr, w = own.shape
    br = _blk(hr // 2, 512, 16)

    def body(a_ref, b_ref, o_ref):
        o_ref[...] = a_ref[...] + b_ref[...].astype(F32)

    row = pl.BlockSpec((br, w), lambda i: (i, 0))
    return pl.pallas_call(
        body, name=name, grid=(hr // br,), in_specs=[row, row], out_specs=row,
        out_shape=jax.ShapeDtypeStruct((hr, w), F32),
        compiler_params=_params(("parallel",)),
    )(own, r2.reshape(hr, w))


def _sibling_swap(t, name):
    hr, w = t.shape

    def body(t_ref, o_ref, send_sem, recv_sem):
        x, y, c, _ = _place()
        cp = pltpu.make_async_remote_copy(src_ref=t_ref, dst_ref=o_ref, send_sem=send_sem, recv_sem=recv_sem,
                                          device_id=(x, y, 1 - c), device_id_type=MESH)
        cp.start()
        cp.wait()

    return pl.pallas_call(
        body, name=name, in_specs=[_ANY], out_specs=_ANY,
        out_shape=jax.ShapeDtypeStruct((hr, w), t.dtype),
        scratch_shapes=[pltpu.SemaphoreType.DMA, pltpu.SemaphoreType.DMA],
        compiler_params=pltpu.CompilerParams(vmem_limit_bytes=VMEM_LIMIT_BYTES),
    )(t)


def _all_reduce_small(v, name):
    r, w = v.shape

    def body(v_ref, o_ref, slots, send_sems, recv_sems):
        x, y, c, _ = _place()
        me = 4 * x + 2 * y + c
        slots[me] = v_ref[...]
        cps = []
        for k in range(1, N_DEV):
            fx, fy, fc = (k >> 2) & 1, (k >> 1) & 1, k & 1
            to = (x ^ fx, y ^ fy, c ^ fc)
            cp = pltpu.make_async_remote_copy(
                src_ref=v_ref, dst_ref=slots.at[me], send_sem=send_sems.at[k - 1], recv_sem=recv_sems.at[k - 1],
                device_id=to, device_id_type=MESH)
            cp.start()
            cps.append(cp)
        for k in range(1, N_DEV):
            fx, fy, fc = (k >> 2) & 1, (k >> 1) & 1, k & 1
            src_dev = 4 * (x ^ fx) + 2 * (y ^ fy) + (c ^ fc)
            pltpu.make_async_remote_copy(
                src_ref=v_ref, dst_ref=slots.at[src_dev], send_sem=send_sems.at[k - 1],
                recv_sem=recv_sems.at[k - 1], device_id=(x, y, c), device_id_type=MESH).wait_recv()
        for cp in cps:
            cp.wait_send()
        acc = slots[0]
        for d in range(1, N_DEV):
            acc = acc + slots[d]
        o_ref[...] = acc

    return pl.pallas_call(
        body, name=name,
        in_specs=[pl.BlockSpec(memory_space=pltpu.VMEM)], out_specs=pl.BlockSpec(memory_space=pltpu.VMEM),
        out_shape=jax.ShapeDtypeStruct((r, w), F32),
        scratch_shapes=[pltpu.VMEM((N_DEV, r, w), F32), pltpu.SemaphoreType.DMA((N_DEV - 1,)),
                        pltpu.SemaphoreType.DMA((N_DEV - 1,))],
        compiler_params=pltpu.CompilerParams(vmem_limit_bytes=VMEM_LIMIT_BYTES),
    )(v)


def _part_rows(shape, part_rows=PACK_PART_ROWS):
    assert shape[-1] <= PACK_LANES
    return _round_up(math.prod(shape[:-1]), part_rows)


def _packed_rows(shapes):
    return _round_up(sum(_part_rows(s) for s in shapes), PACK_ROWS_MULT)


def _pack(arrs, total_rows, dtype, part_rows=PACK_PART_ROWS):
    parts = []
    for a in arrs:
        a2 = a.reshape(-1, a.shape[-1]).astype(dtype)
        rows = _part_rows(a.shape, part_rows)
        parts.append(jnp.pad(a2, ((0, rows - a2.shape[0]), (0, PACK_LANES - a2.shape[1]))))
    used = sum(p.shape[0] for p in parts)
    if total_rows > used:
        parts.append(jnp.zeros((total_rows - used, PACK_LANES), dtype))
    return jnp.concatenate(parts, axis=0)


def _unpack(packed, shapes, part_rows=PACK_PART_ROWS):
    out, r0 = [], 0
    for s in shapes:
        out.append(packed[r0:r0 + math.prod(s[:-1]), :s[-1]].reshape(s))
        r0 += _part_rows(s, part_rows)
    return out


_BIG = (("fox_w_in", 2), ("fox_w_out", 1), ("mla_w_kv_a", 0), ("mla_w_kv_b", 1), ("mla_w_q_a", 1),
        ("mla_w_q_b", 2), ("mla_w_out", 1), ("ffn_w_up", 2), ("ffn_w_down", 1))
_SMALL = ("norm_mix_g", "norm_ffn_g", "fox_b_f", "kv_norm_g", "mla_kv_a_norm_g", "mla_q_a_norm_g", "final_norm_g")
_WEIGHTS = ("norm_mix_g", "norm_ffn_g", "fox_w_in", "fox_b_f", "fox_w_out", "kv_norm_g", "mla_w_kv_a",
            "mla_kv_a_norm_g", "mla_w_kv_b", "mla_w_q_a", "mla_q_a_norm_g", "mla_w_q_b", "mla_w_out",
            "ffn_w_up", "ffn_w_down", "final_norm_g")


def _ffn_fwd(x, h, w_up, w_down, tag):
    def relu_sq(acc):
        r = jnp.maximum(acc, 0.0)
        return r, r * r

    r, a = _matmul(h, w_up, mode="nn", out_dtypes=(BF16, BF16), epilogue=relu_sq, name=f"{tag}_up")
    x_out = _matmul(a, w_down, mode="nn", out_dtypes=(F32,), epilogue=lambda acc, res: (acc + res,),
                    extras=(x,), name=f"{tag}_down")
    return x_out, r, a


def _ffn_bwd(dx_out, x_in, h, r, a, g_norm, w_up, w_down, tag):
    d_u = _matmul(dx_out, w_down, mode="nt", out_dtypes=(BF16,), epilogue=lambda acc, rr: (acc * (2.0 * rr.astype(F32)),),
                  extras=(r,), name=f"{tag}_d_act")
    d_w_down = _matmul(a, dx_out, mode="tn", out_dtypes=(F32,), name=f"{tag}_d_w_down")
    d_w_up = _matmul(h, d_u, mode="tn", out_dtypes=(F32,), by_chip=True, name=f"{tag}_d_w_up")
    d_h = _matmul(d_u, w_up, mode="nt", out_dtypes=(F32,), name=f"{tag}_d_h")
    dx_in, (d_g,) = _rms_bwd(x_in, [(g_norm, d_h)], dx_out, name=f"{tag}_d_norm")
    return dx_in, d_w_up, d_w_down, d_g


def kernel(x, norm_mix_g, norm_ffn_g, fox_w_in, fox_b_f, fox_w_out, kv_norm_g, mla_w_kv_a, mla_kv_a_norm_g, mla_w_kv_b, mla_w_q_a, mla_q_a_norm_g, mla_w_q_b, mla_w_out, ffn_w_up, ffn_w_down, final_norm_g, loss_target, m_norm_mix_g, m_norm_ffn_g, m_fox_w_in, m_fox_b_f, m_fox_w_out, m_kv_norm_g, m_mla_w_kv_a, m_mla_kv_a_norm_g, m_mla_w_kv_b, m_mla_w_q_a, m_mla_q_a_norm_g, m_mla_w_q_b, m_mla_w_out, m_ffn_w_up, m_ffn_w_down, m_final_norm_g, v_norm_mix_g, v_norm_ffn_g, v_fox_w_in, v_fox_b_f, v_fox_w_out, v_kv_norm_g, v_mla_w_kv_a, v_mla_kv_a_norm_g, v_mla_w_kv_b, v_mla_w_q_a, v_mla_q_a_norm_g, v_mla_w_q_b, v_mla_w_out, v_ffn_w_up, v_ffn_w_down, v_final_norm_g):
    args = dict(locals())
    w_in = {n: args[n] for n in _WEIGHTS}
    m_in = {n: args["m_" + n] for n in _WEIGHTS}
    v_in = {n: args["v_" + n] for n in _WEIGHTS}

    xs = x[0]
    seq, d_model = xs.shape
    tgt = loss_target[0]
    fox_h, mla_h, nope = FOX_HEADS, MLA_HEADS, QK_NOPE_DIM
    kv_rank = mla_kv_a_norm_g.shape[0]
    rope = mla_w_kv_a.shape[1] - kv_rank
    half = rope // 2
    q_rank = mla_q_a_norm_g.shape[1]
    v_dim = mla_w_kv_b.shape[1] * N_CHIPS // mla_h - nope
    fox_w = fox_w_out.shape[1] * N_CHIPS
    fox_dh = fox_w // fox_h

    big_names = [n for n, _ in _BIG]
    shard_shapes = [w_in[n].shape for n in big_names]
    rows = _packed_rows(shard_shapes)
    my_shard = _pack([w_in[n] for n in big_names], rows, BF16)
    others = _all_gather_shards(my_shard, name="gather_weights")
    by_relation = jnp.concatenate([my_shard[None], others], axis=0)
    p_chip = 2 * lax.axis_index("x") + lax.axis_index("y")
    full = {}
    for q in range(N_CHIPS):
        shard_q = lax.dynamic_index_in_dim(by_relation, p_chip ^ q, axis=0, keepdims=False)
        for (n, ax), piece in zip(_BIG, _unpack(shard_q, shard_shapes)):
            full.setdefault(n, []).append(piece)
    full = {n: jnp.concatenate(full[n], axis=ax) for n, ax in _BIG}

    fox_scale = fox_dh ** -0.5
    fox_wd = _round_up(fox_dh + 9, LANE_TILE)
    fox_vwd = _round_up(fox_dh + 4, LANE_TILE)
    w_fox_in = full["fox_w_in"][0]
    w_fq = _pad_heads(w_fox_in[:, :fox_w] * fox_scale, fox_h, fox_wd, 1)
    w_fk = _pad_heads(w_fox_in[:, fox_w:2 * fox_w], fox_h, fox_wd, 1)
    w_fv = _pad_heads(w_fox_in[:, 2 * fox_w:3 * fox_w], fox_h, fox_vwd, 1)
    w_gate = w_fox_in[:, 3 * fox_w:]
    w_fox_out = _pad_heads(full["fox_w_out"][0], fox_h, fox_vwd, 0)
    n_cx = _round_up(3 * fox_h + 1, LANE_TILE)
    c_piece = lambda i: (lambda hh: 3 * hh + i)
    one_col = 3 * fox_h
    e_fq = _placement(n_cx, fox_h, fox_wd, [(c_piece(i), fox_dh + i, 1.0) for i in range(3)]
                      + [(one_col, fox_dh + 3 + i, 1.0) for i in range(3)])
    e_fk = _placement(n_cx, fox_h, fox_wd, [(one_col, fox_dh + i, 1.0) for i in range(3)]
                      + [(c_piece(i), fox_dh + 3 + i, -1.0) for i in range(3)]
                      + [(one_col, fox_dh + 6 + i, 1.0) for i in range(3)])
    e_fv = _placement(n_cx, fox_h, fox_vwd, [(one_col, fox_dh + i, -1.0) for i in range(3)]
                      + [(one_col, fox_dh + 3, 1.0)])

    mla_scale = (nope + rope) ** -0.5
    mla_dk = nope + rope
    mla_wd = _round_up(mla_dk + 3, LANE_TILE)
    mla_vwd = _round_up(v_dim + 4, LANE_TILE)
    w_kv_a = full["mla_w_kv_a"]
    w_kv_b3 = full["mla_w_kv_b"].reshape(kv_rank, mla_h, nope + v_dim)
    w_kn = _pad_heads(w_kv_b3[:, :, :nope].reshape(kv_rank, -1), mla_h, mla_wd, 1)
    w_mv = _pad_heads(w_kv_b3[:, :, nope:].reshape(kv_rank, -1), mla_h, mla_vwd, 1)
    w_q_a = full["mla_w_q_a"][0]
    w_q_b3 = full["mla_w_q_b"][0].reshape(q_rank, mla_h, nope + rope)
    w_qa_ = _pad_heads(w_q_b3.reshape(q_rank, -1), mla_h, mla_wd, 1)
    w_qb_ = _pad_heads(jnp.concatenate([jnp.zeros_like(w_q_b3[:, :, :nope]), -w_q_b3[:, :, nope + half:],
                                        w_q_b3[:, :, nope:nope + half]], axis=-1).reshape(q_rank, -1),
                       mla_h, mla_wd, 1)
    w_mla_out = _pad_heads(full["mla_w_out"][0], mla_h, mla_vwd, 0)
    w_up, w_down = full["ffn_w_up"], full["ffn_w_down"]
    n_kx = _round_up(rope + 1, LANE_TILE)
    e_mk = _placement(n_kx, mla_h, mla_wd, [(j, nope + j, 1.0) for j in range(rope)]
                      + [(rope, mla_dk + i, 1.0) for i in range(3)])
    e_mv = _placement(n_kx, mla_h, mla_vwd, [(rope, v_dim + i, -1.0) for i in range(3)] + [(rope, v_dim + 3, 1.0)])
    e_kr_u = _placement(n_kx, mla_h, mla_wd, [(j, nope + j, 1.0) for j in range(rope)]).T
    e_kr_v = _placement(n_kx, mla_h, mla_wd, [(j, nope + half + j, 1.0) for j in range(half)]
                        + [(half + j, nope + j, -1.0) for j in range(half)]).T

    inv = 1.0 / (ROPE_BASE ** (jnp.arange(0, rope, 2, dtype=F32) / rope))
    ang = jnp.arange(seq, dtype=F32)[:, None] * inv[None, :]
    cos, sin = jnp.cos(ang), jnp.sin(ang)
    pad_t = jnp.zeros((seq, mla_wd - mla_dk), F32)
    cos_t = jnp.concatenate([jnp.ones((seq, nope), F32), cos, cos, pad_t], axis=1)
    sin_t = jnp.concatenate([jnp.zeros((seq, nope), F32), sin, sin, pad_t], axis=1)
    pad_k = jnp.zeros((seq, n_kx - rope), F32)
    cos_k, sin_k = jnp.concatenate([cos, cos, pad_k], axis=1), jnp.concatenate([sin, sin, pad_k], axis=1)

    (h0,) = _rms_fwd(xs, norm_mix_g[0:1], name="l0_norm_mix")
    gate = _matmul(h0, w_gate, mode="nn", out_dtypes=(F32,), name="fox_gate")
    z = gate + fox_b_f[0][None, :]
    cum = jnp.cumsum(jax.nn.log_sigmoid(z), axis=0)
    cx = jnp.concatenate([_split3(cum).reshape(seq, 3 * fox_h), jnp.ones((seq, 1), BF16),
                          jnp.zeros((seq, n_cx - 3 * fox_h - 1), BF16)], axis=1)
    fqa = _matmul(h0, w_fq, mode="nn", out_dtypes=(BF16,), placed=(cx, e_fq), name="fox_q")
    fka = _matmul(h0, w_fk, mode="nn", out_dtypes=(BF16,), placed=(cx, e_fk), name="fox_k")
    fva = _matmul(h0, w_fv, mode="nn", out_dtypes=(BF16,), placed=(cx, e_fv), name="fox_v")
    foa, fqb = _flash_fwd(fqa, fka, fva, fox_h, fox_dh + 3, fox_dh + 6, FOX_FWD_SUB_ROWS, name="fox_attn")
    add_res = lambda acc, res: (acc + res,)
    x1 = _matmul(foa, w_fox_out, mode="nn", out_dtypes=(F32,), epilogue=add_res, extras=(xs,), name="fox_out")
    (h1,) = _rms_fwd(x1, norm_ffn_g[0:1], name="l0_norm_ffn")
    x2, r0, a0 = _ffn_fwd(x1, h1, w_up[0], w_down[0], "ffn0")

    src, h2 = _rms_fwd(x2, jnp.stack([kv_norm_g, norm_mix_g[1]]), name="l1_norm_kv_mix")
    kv_a = _matmul(src, w_kv_a, mode="nn", out_dtypes=(F32,), name="mla_kv_a")
    (c_kv,) = _rms_fwd(kv_a, mla_kv_a_norm_g[None, :], name="mla_norm_kv_a")
    kr1, kr2 = _rope(kv_a[None, :, kv_rank:kv_rank + half], kv_a[None, :, kv_rank + half:], cos, sin, 1.0,
                     name="mla_rope_k")
    krx = jnp.concatenate([kr1.astype(BF16), kr2.astype(BF16), jnp.ones((seq, 1), BF16),
                           jnp.zeros((seq, n_kx - rope - 1), BF16)], axis=1)
    mka = _matmul(c_kv, w_kn, mode="nn", out_dtypes=(BF16,), placed=(krx, e_mk), name="mla_k")
    mva = _matmul(c_kv, w_mv, mode="nn", out_dtypes=(BF16,), placed=(krx, e_mv), name="mla_v")
    cq_pre = _matmul(h2, w_q_a, mode="nn", out_dtypes=(F32,), name="mla_q_a")
    (c_q,) = _rms_fwd(cq_pre, mla_q_a_norm_g, name="mla_norm_q_a")
    mqa = _rope_proj(c_q, w_qa_, w_qb_, cos_t, sin_t, mla_scale, mla_h, name="mla_q_b_rope")
    moa, mqb = _flash_fwd(mqa, mka, mva, mla_h, v_dim + 3, mla_dk, MLA_FWD_SUB_ROWS, name="mla_attn")
    x3 = _matmul(moa, w_mla_out, mode="nn", out_dtypes=(F32,), epilogue=add_res, extras=(x2,), name="mla_out")
    (h3,) = _rms_fwd(x3, norm_ffn_g[1:2], name="l1_norm_ffn")
    x4, r1, a1 = _ffn_fwd(x3, h3, w_up[1], w_down[1], "ffn1")

    loss_tile, dx4, d_final_g = _loss_head(x4, final_norm_g[None, :], tgt, name="loss_head")
    loss = lax.psum(loss_tile[0, 0], ("x", "y", "c"))

    gw = {}
    dx3, d_up1, d_down1, d_nf1 = _ffn_bwd(dx4, x3, h3, r1, a1, norm_ffn_g[1:2], w_up[1], w_down[1], "ffn1")

    d_moa = _matmul(dx3, w_mla_out, mode="nt", out_dtypes=(BF16,), epilogue=_delta_epilogue(mla_vwd, v_dim),
                    extras=(moa,), name="mla_d_ctx")
    gw["mla_w_out"] = _unpad_heads(_matmul(moa, dx3, mode="tn", out_dtypes=(F32,), name="mla_d_w_out"),
                                   mla_h, v_dim, 0)[None]
    d_mqa, d_mka, d_mva = _flash_bwd(mqb, mka, mva, d_moa, mla_h, MLA_BWD_HEADS_PER_STEP,
                                     name="mla_attn_bwd")
    d_c_q, d_w_qa_, d_w_qb_ = _rope_proj_bwd(c_q, d_mqa, w_qa_, w_qb_, cos_t, sin_t, mla_scale, mla_h,
                                             name="mla_q_b_rope_bwd")
    d_w_qa_ = _unpad_heads(d_w_qa_, mla_h, mla_dk, 1).reshape(q_rank, mla_h, mla_dk)
    d_w_qb_ = _unpad_heads(d_w_qb_, mla_h, mla_dk, 1).reshape(q_rank, mla_h, mla_dk)
    gw["mla_w_q_b"] = jnp.concatenate(
        [d_w_qa_[:, :, :nope], d_w_qa_[:, :, nope:nope + half] + d_w_qb_[:, :, nope + half:],
         d_w_qa_[:, :, nope + half:] - d_w_qb_[:, :, nope:nope + half]], axis=-1).reshape(1, q_rank, mla_h * mla_dk)
    d_cq_pre, (d_q_a_g,) = _rms_bwd(cq_pre, [(mla_q_a_norm_g, d_c_q)], None, name="mla_d_norm_q_a")
    gw["mla_w_q_a"] = _matmul(h2, d_cq_pre, mode="tn", out_dtypes=(F32,), name="mla_d_w_q_a")[None]
    d_h2 = _matmul(d_cq_pre, w_q_a, mode="nt", out_dtypes=(F32,), name="mla_d_h")

    d_w_kn = _unpad_heads(_matmul(c_kv, d_mka, mode="tn", out_dtypes=(F32,), name="mla_d_w_k"), mla_h, nope, 1)
    d_w_mv = _unpad_heads(_matmul(c_kv, d_mva, mode="tn", out_dtypes=(F32,), name="mla_d_w_v"), mla_h, v_dim, 1)
    gw["mla_w_kv_b"] = jnp.concatenate([d_w_kn.reshape(kv_rank, mla_h, nope), d_w_mv.reshape(kv_rank, mla_h, v_dim)],
                                       axis=-1).reshape(kv_rank, mla_h * (nope + v_dim))
    d_c_kv_v = _matmul(d_mva, w_mv, mode="nt", out_dtypes=(F32,), name="mla_d_c_kv_v")
    d_c_kv = _matmul(d_mka, w_kn, mode="nt", out_dtypes=(F32,), epilogue=add_res, extras=(d_c_kv_v,),
                     name="mla_d_c_kv")
    d_ckv_pre, (d_kv_a_g,) = _rms_bwd(kv_a, [(mla_kv_a_norm_g[None, :], d_c_kv)], None, name="mla_d_norm_kv_a")
    d_kr_u = _matmul(d_mka, e_kr_u, mode="nn", out_dtypes=(F32,), name="mla_d_k_rope_u")
    d_kr_v = _matmul(d_mka, e_kr_v, mode="nn", out_dtypes=(F32,), name="mla_d_k_rope_v")
    d_kr = _rope_mix(d_kr_u, d_kr_v, cos_k, sin_k, 1.0, 1, name="mla_rope_dk")
    d_kv_a = jnp.concatenate([d_ckv_pre, d_kr[:, :rope].astype(F32)], axis=1)
    gw["mla_w_kv_a"] = _matmul(src, d_kv_a, mode="tn", out_dtypes=(F32,), name="mla_d_w_kv_a")
    d_src = _matmul(d_kv_a, w_kv_a, mode="nt", out_dtypes=(F32,), name="mla_d_src")
    dx2, (d_kv_g, d_nm1) = _rms_bwd(x2, [(kv_norm_g[None, :], d_src), (norm_mix_g[1:2], d_h2)], dx3,
                                    name="l1_d_norm_kv_mix")

    dx1, d_up0, d_down0, d_nf0 = _ffn_bwd(dx2, x1, h1, r0, a0, norm_ffn_g[0:1], w_up[0], w_down[0], "ffn0")
    by_rows = lambda g: g.reshape(N_CHIPS, g.shape[0] // N_CHIPS, g.shape[1])
    gw_by_chip = {"ffn_w_up": jnp.concatenate([d_up0, d_up1], axis=1),
                  "ffn_w_down": jnp.concatenate([by_rows(d_down0), by_rows(d_down1)], axis=1)}

    d_foa = _matmul(dx1, w_fox_out, mode="nt", out_dtypes=(BF16,), epilogue=_delta_epilogue(fox_vwd, fox_dh),
                    extras=(foa,), name="fox_d_ctx")
    gw["fox_w_out"] = _unpad_heads(_matmul(foa, dx1, mode="tn", out_dtypes=(F32,), name="fox_d_w_out"),
                                   fox_h, fox_dh, 0)[None]
    fox_hps = FOX_BWD_HEADS_PER_STEP if fox_h % FOX_BWD_HEADS_PER_STEP == 0 else 1
    d_fqa, d_fka, d_fva, ds_rows, ds_cols = _flash_bwd(fqb, fka, fva, d_foa, fox_h, fox_hps, name="fox_attn_bwd",
                                                       sum_cols=(fox_dh, fox_dh + 3))
    d_cum = jnp.transpose(ds_rows - ds_cols, (1, 0, 2)).reshape(seq, fox_h)
    d_z = lax.cumsum(d_cum, axis=0, reverse=True) * jax.nn.sigmoid(-z)
    d_b_f = jnp.sum(d_z, axis=0)
    d_w_in = [_unpad_heads(_matmul(h0, g, mode="tn", out_dtypes=(F32,), name=f"fox_d_w_{tag}"), fox_h, fox_dh, 1)
              for tag, g in (("q", d_fqa), ("k", d_fka), ("v", d_fva))]
    d_w_gate = _matmul(h0, d_z, mode="tn", out_dtypes=(F32,), name="fox_d_w_gate")
    gw["fox_w_in"] = jnp.concatenate([d_w_in[0] * fox_scale, d_w_in[1], d_w_in[2], d_w_gate], axis=1)[None]
    d_h0 = _matmul(d_z, w_gate, mode="nt", out_dtypes=(F32,), name="fox_d_h_gate")
    for tag, g, w in (("q", d_fqa, w_fq), ("k", d_fka, w_fk), ("v", d_fva, w_fv)):
        d_h0 = _matmul(g, w, mode="nt", out_dtypes=(F32,), epilogue=add_res, extras=(d_h0,), name=f"fox_d_h_{tag}")
    grad_x, (d_nm0,) = _rms_bwd(xs, [(norm_mix_g[0:1], d_h0)], dx1, name="l0_d_norm_mix")

    c_idx = lax.axis_index("c").astype(jnp.int32).reshape(1)
    parts = []
    for (n, ax), shape in zip(_BIG, shard_shapes):
        if n in gw_by_chip:
            g = gw_by_chip[n]
        else:
            g = gw[n]
            g = jnp.moveaxis(g.reshape(g.shape[:ax] + (N_CHIPS, shape[ax]) + g.shape[ax + 1:]), ax, 0)
            g = g.reshape(N_CHIPS, -1, shape[-1])
        parts.append(jnp.pad(g, ((0, 0), (0, _part_rows(shape) - g.shape[1]), (0, PACK_LANES - shape[-1]))))
    parts.append(jnp.zeros((N_CHIPS, rows - sum(p.shape[1] for p in parts), PACK_LANES), F32))
    g4 = jnp.concatenate(parts, axis=1)
    a4 = _sibling_swap_halves(g4, name="grads_to_sibling")
    s4 = _chip_sum(g4, a4, c_idx, name="grads_chip_sum")
    chip_idx = p_chip.astype(jnp.int32).reshape(1)
    r1 = _rs_first_hop(s4, name="grads_first_hop")
    t_own, onward = _rs_middle(s4, r1, chip_idx, name="grads_middle_sum")
    r2 = _rs_second_hop(onward, name="grads_second_hop")
    t_mine = _rs_last_add(t_own, r2, name="grads_last_sum")
    t_theirs = _sibling_swap(t_mine, name="grads_join_halves")
    is_south = lax.axis_index("c") == 0
    g_big = jnp.concatenate([jnp.where(is_south, t_mine, t_theirs), jnp.where(is_south, t_theirs, t_mine)],
                            axis=0)

    small_local = {"norm_mix_g": jnp.concatenate([d_nm0, d_nm1], axis=0),
                   "norm_ffn_g": jnp.concatenate([d_nf0, d_nf1], axis=0),
                   "fox_b_f": d_b_f[None, :], "kv_norm_g": d_kv_g[0], "mla_kv_a_norm_g": d_kv_a_g[0],
                   "mla_q_a_norm_g": d_q_a_g, "final_norm_g": d_final_g[0]}
    small_shapes = [w_in[n].shape for n in _SMALL]
    small_rows = sum(_part_rows(s, SMALL_PART_ROWS) for s in small_shapes)
    pack_small = lambda arrs: _pack(arrs, small_rows, F32, SMALL_PART_ROWS)
    g_small = _all_reduce_small(pack_small([small_local[n] for n in _SMALL]), name="grads_small")

    grads = dict(zip(big_names, _unpack(g_big, shard_shapes)))
    delta, new_m, new_v = {}, {}, {}
    for n, shape in zip(big_names, shard_shapes):
        flat = lambda a: a.reshape(-1, shape[-1])
        outs = _adamw(flat(w_in[n]), flat(grads[n]), flat(m_in[n]), flat(v_in[n]), name=f"adamw_{n}")
        delta[n], new_m[n], new_v[n] = (o.reshape(shape) for o in outs)
    sm_outs = _adamw(pack_small([w_in[n] for n in _SMALL]), g_small, pack_small([m_in[n] for n in _SMALL]),
                     pack_small([v_in[n] for n in _SMALL]), name="adamw_small")
    for res, packed in zip((grads, delta, new_m, new_v), (g_small,) + tuple(sm_outs)):
        res.update(zip(_SMALL, _unpack(packed, small_shapes, SMALL_PART_ROWS)))

    return (loss, grad_x[None], *[grads[n] for n in _WEIGHTS], *[delta[n] for n in _WEIGHTS],
            *[new_m[n] for n in _WEIGHTS], *[new_v[n] for n in _WEIGHTS])
```

```python
import math

import numpy as np
import jax
import jax.numpy as jnp
from jax import lax
from jax.experimental import pallas as pl
from jax.experimental.pallas import tpu as pltpu

F32 = jnp.float32
BF16 = jnp.bfloat16

FOX_HEADS = 16
MLA_HEADS = 8
QK_NOPE_DIM = 128
ROPE_BASE = 10000.0
EPS = 1e-6

ADAM_LR = 0.001
ADAM_B1 = 0.9
ADAM_B2 = 0.999
ADAM_EPS = 1e-08
ADAM_WD = 0.01
ADAM_STEP = 10

N_CHIPS = 4
N_DEV = 8
PACK_LANES = 1024
PACK_PART_ROWS = 16
SMALL_PART_ROWS = 8
PACK_ROWS_MULT = 1024
VMEM_LIMIT_BYTES = 48 * 1024 * 1024
LANE_TILE = 128
MATMUL_BLOCK = 1024
MATMUL_WIDE_BLOCK = 2048
MATMUL_DEPTH = 2048
ATTN_BLOCK_Q = 1024
ATTN_BLOCK_K = 1024
ATTN_FWD_LANES = 1024
FOX_BWD_HEADS_PER_STEP = 4
MLA_BWD_HEADS_PER_STEP = 2
ATTN_SUB_ROWS = 256
FOX_FWD_SUB_ROWS = (1024, 512)
MLA_FWD_SUB_ROWS = (256, 256)
NEG_BIG = -1e30
MESH = pl.DeviceIdType.MESH


def _round_up(n, m):
    return -(-n // m) * m


def _blk(dim, pref, mult=128):
    if dim <= pref:
        return dim
    b = (pref // mult) * mult
    while b >= mult:
        if dim % b == 0:
            return b
        b -= mult
    return dim


def _params(sem=None):
    return pltpu.CompilerParams(dimension_semantics=sem, vmem_limit_bytes=VMEM_LIMIT_BYTES)


_DIMS = {"nn": (((1,), (0,)), ((), ())), "nt": (((1,), (1,)), ((), ())), "tn": (((0,), (0,)), ((), ()))}


def _matmul(a, b, *, mode, out_dtypes, name, epilogue=None, extras=(), placed=None, by_chip=False):
    if mode == "tn":
        kdim, m = a.shape
    else:
        m, kdim = a.shape
    n = b.shape[0] if mode == "nt" else b.shape[1]
    bm, bk = _blk(m, MATMUL_BLOCK), _blk(kdim, MATMUL_DEPTH)
    bn = _blk(n, MATMUL_WIDE_BLOCK if (mode != "tn" and kdim <= MATMUL_BLOCK) else MATMUL_BLOCK)
    if by_chip:
        bn = _blk(n // N_CHIPS, bn)
    nk = kdim // bk
    n_extra, n_out = len(extras), len(out_dtypes)
    n_placed = 0 if placed is None else 2
    dims = _DIMS[mode]

    def body(a_ref, b_ref, *rest):
        placed_refs = rest[:n_placed]
        rest = rest[n_placed:]
        extra_refs = rest[:n_extra]
        out_refs = rest[n_extra:n_extra + n_out]

        def finish(acc):
            if n_placed:
                acc = acc + lax.dot_general(placed_refs[0][...], placed_refs[1][...], _DIMS["nn"],
                                            preferred_element_type=F32)
            res = (acc,) if epilogue is None else epilogue(acc, *[r[...] for r in extra_refs])
            for o_ref, r in zip(out_refs, res):
                o_ref[...] = r.astype(o_ref.dtype)

        part = lax.dot_general(a_ref[...].astype(BF16), b_ref[...].astype(BF16), dims, preferred_element_type=F32)
        if nk == 1:
            finish(part)
            return
        acc_ref = rest[n_extra + n_out]
        k = pl.program_id(2)

        @pl.when(k == 0)
        def _():
            acc_ref[...] = part

        @pl.when((k > 0) & (k < nk - 1))
        def _():
            acc_ref[...] += part

        @pl.when(k == nk - 1)
        def _():
            finish(acc_ref[...] + part)

    if mode == "tn":
        a_spec = pl.BlockSpec((bk, bm), lambda i, j, k: (k, i))
    else:
        a_spec = pl.BlockSpec((bm, bk), lambda i, j, k: (i, k))
    if mode == "nt":
        b_spec = pl.BlockSpec((bn, bk), lambda i, j, k: (j, k))
    else:
        b_spec = pl.BlockSpec((bk, bn), lambda i, j, k: (k, j))
    tile = pl.BlockSpec((bm, bn), lambda i, j, k: (i, j))
    placed_specs = []
    if n_placed:
        k2 = placed[0].shape[1]
        placed_specs = [pl.BlockSpec((bm, k2), lambda i, j, k: (i, 0)), pl.BlockSpec((k2, bn), lambda i, j, k: (0, j))]
    out_tile, out_dims = tile, (m, n)
    if by_chip:
        per_chip = n // N_CHIPS // bn
        out_tile = pl.BlockSpec((None, bm, bn), lambda i, j, k: (j // per_chip, i, j % per_chip))
        out_dims = (N_CHIPS, m, n // N_CHIPS)
    outs = pl.pallas_call(
        body, name=name,
        grid=(m // bm, n // bn, nk),
        in_specs=[a_spec, b_spec] + placed_specs + [tile] * n_extra,
        out_specs=[out_tile] * n_out,
        out_shape=[jax.ShapeDtypeStruct(out_dims, dt) for dt in out_dtypes],
        scratch_shapes=[pltpu.VMEM((bm, bn), F32)] if nk > 1 else [],
        compiler_params=_params(("parallel", "parallel", "arbitrary")),
    )(a, b, *(placed or ()), *extras)
    return outs[0] if n_out == 1 else outs


def _rms_fwd(x, gains, name):
    s = x.shape[0]
    g, w = gains.shape
    bs = _blk(s, 512, 8)

    def body(x_ref, g_ref, *out_refs):
        xv = x_ref[...]
        y = xv * lax.rsqrt(jnp.mean(xv * xv, axis=-1, keepdims=True) + EPS)
        for i, o_ref in enumerate(out_refs):
            o_ref[...] = (y * g_ref[i:i + 1, :]).astype(o_ref.dtype)

    row = pl.BlockSpec((bs, w), lambda i: (i, 0))
    outs = pl.pallas_call(
        body, name=name, grid=(s // bs,),
        in_specs=[row, pl.BlockSpec((g, w), lambda i: (0, 0))],
        out_specs=[row] * g,
        out_shape=[jax.ShapeDtypeStruct((s, w), BF16)] * g,
        compiler_params=_params(("parallel",)),
    )(x, gains)
    return outs


def _rms_bwd(x, branches, resid, name):
    s = x.shape[0]
    w = branches[0][0].shape[1]
    nb = len(branches)
    bs = _blk(s, 512, 8)
    has_resid = resid is not None

    def body(x_ref, *rest):
        g_refs = rest[:nb]
        dy_refs = rest[nb:2 * nb]
        pos = 2 * nb
        r_ref = rest[pos] if has_resid else None
        pos += int(has_resid)
        dx_ref, lo_ref = rest[pos], rest[pos + 1]
        dg_refs = rest[pos + 2:pos + 2 + nb]
        i = pl.program_id(0)

        @pl.when(i == 0)
        def _():
            for dg_ref in dg_refs:
                dg_ref[...] = jnp.zeros_like(dg_ref)

        xv = x_ref[...]
        rstd = lax.rsqrt(jnp.mean(xv * xv, axis=-1, keepdims=True) + EPS)
        xhat = xv * rstd
        dx = r_ref[...] if has_resid else jnp.zeros_like(xv)
        for g_ref, dy_ref, dg_ref in zip(g_refs, dy_refs, dg_refs):
            dy = dy_ref[...].astype(F32)
            dyg = dy * g_ref[...]
            dx = dx + rstd * (dyg - xhat * jnp.mean(dyg * xhat, axis=-1, keepdims=True))
            dg_ref[...] += jnp.sum(dy * xhat, axis=0, keepdims=True)
        dx_ref[...] = dx
        lo_ref[...] = dx.astype(lo_ref.dtype)

    row = pl.BlockSpec((bs, w), lambda i: (i, 0))
    vec = pl.BlockSpec((1, w), lambda i: (0, 0))
    args = [x] + [g for g, _ in branches] + [dy for _, dy in branches] + ([resid] if has_resid else [])
    outs = pl.pallas_call(
        body, name=name, grid=(s // bs,),
        in_specs=[row] + [vec] * nb + [row] * nb + ([row] if has_resid else []),
        out_specs=[row, row] + [vec] * nb,
        out_shape=[jax.ShapeDtypeStruct((s, w), F32), jax.ShapeDtypeStruct((s, w), BF16)]
        + [jax.ShapeDtypeStruct((1, w), F32)] * nb,
        compiler_params=_params(("arbitrary",)),
    )(*args)
    return (outs[0], outs[1]), list(outs[2:])


def _loss_head(x, g, target, name):
    s, w = x.shape
    bs = _blk(s, 512, 8)

    def body(x_ref, g_ref, t_ref, loss_ref, dx_ref, dg_ref, lo_ref):
        i = pl.program_id(0)

        @pl.when(i == 0)
        def _():
            loss_ref[...] = jnp.zeros_like(loss_ref)
            dg_ref[...] = jnp.zeros_like(dg_ref)

        xv = x_ref[...]
        gv = g_ref[...]
        rstd = lax.rsqrt(jnp.mean(xv * xv, axis=-1, keepdims=True) + EPS)
        xhat = xv * rstd
        err = xhat * gv - t_ref[...]
        loss_ref[...] += 0.5 * jnp.sum(jnp.mean(err * err, axis=-1, keepdims=True))
        dy = err * (1.0 / w)
        dyg = dy * gv
        dx = rstd * (dyg - xhat * jnp.mean(dyg * xhat, axis=-1, keepdims=True))
        dx_ref[...] = dx
        lo_ref[...] = dx.astype(lo_ref.dtype)
        dg_ref[...] += jnp.sum(dy * xhat, axis=0, keepdims=True)

    row = pl.BlockSpec((bs, w), lambda i: (i, 0))
    vec = pl.BlockSpec((1, w), lambda i: (0, 0))
    loss_tile, dx, dg, dx_lo = pl.pallas_call(
        body, name=name, grid=(s // bs,),
        in_specs=[row, vec, row],
        out_specs=[pl.BlockSpec((8, 128), lambda i: (0, 0)), row, vec, row],
        out_shape=[jax.ShapeDtypeStruct((8, 128), F32), jax.ShapeDtypeStruct((s, w), F32),
                   jax.ShapeDtypeStruct((1, w), F32), jax.ShapeDtypeStruct((s, w), BF16)],
        compiler_params=_params(("arbitrary",)),
    )(x, g, target)
    return loss_tile, (dx, dx_lo), dg


def _rope(a, b, cos, sin, sign, name):
    g, s, w = a.shape
    bs = _blk(s, 1024, 8)

    def body(a_ref, b_ref, c_ref, s_ref, o1_ref, o2_ref):
        av = jnp.sum(a_ref[...].astype(F32), axis=0)
        bv = jnp.sum(b_ref[...].astype(F32), axis=0)
        cv, sv = c_ref[...], s_ref[...] * sign
        o1_ref[...] = av * cv - bv * sv
        o2_ref[...] = bv * cv + av * sv

    grp = pl.BlockSpec((g, bs, w), lambda i: (0, i, 0))
    row = pl.BlockSpec((bs, w), lambda i: (i, 0))
    return pl.pallas_call(
        body, name=name, grid=(s // bs,),
        in_specs=[grp, grp, row, row], out_specs=[row, row],
        out_shape=[jax.ShapeDtypeStruct((s, w), F32)] * 2,
        compiler_params=_params(("parallel",)),
    )(a, b, cos, sin)


def _causal_table(s, bq, bk, q_major):
    nq, nk = s // bq, s // bk
    rows = []
    if q_major:
        for qi in range(nq):
            kmax = (qi * bq + bq - 1) // bk
            for ki in range(kmax + 1):
                rows.append((qi, ki, int(ki * bk + bk - 1 > qi * bq), int(ki == 0), int(ki == kmax)))
    else:
        for ki in range(nk):
            qmin = (ki * bk) // bq
            for qi in range(qmin, nq):
                rows.append((qi, ki, int(ki * bk + bk - 1 > qi * bq), int(qi == qmin), int(qi == nq - 1)))
    return jnp.asarray(np.array(rows, np.int32).T)


def _causal_keep(q0, k0, nq, nk, transposed):
    if transposed:
        kpos = k0 + lax.broadcasted_iota(jnp.int32, (nk, nq), 0)
        qpos = q0 + lax.broadcasted_iota(jnp.int32, (nk, nq), 1)
    else:
        qpos = q0 + lax.broadcasted_iota(jnp.int32, (nq, nk), 0)
        kpos = k0 + lax.broadcasted_iota(jnp.int32, (nq, nk), 1)
    return kpos <= qpos


def _sub_tiles(n_rows, n_cols, masked, square, rows_are_keys, sub_rows):
    sub = min(sub_rows, n_rows)
    out = []
    for r0 in range(0, n_rows, sub):
        if masked and square:
            c0, nc = (r0, n_cols - r0) if rows_are_keys else (0, r0 + sub)
        else:
            c0, nc = 0, n_cols
        out.append((r0, sub, c0, nc))
    return out


_NT = (((1,), (1,)), ((), ()))
_NN = (((1,), (0,)), ((), ()))


def _attn_specs(bq, bk):
    qspec = lambda d: pl.BlockSpec((bq, d), lambda hh, t, tb: (tb[0, t], hh))
    kspec = lambda d: pl.BlockSpec((bk, d), lambda hh, t, tb: (tb[1, t], hh))
    return qspec, kspec


def _split3_cols(x):
    hi = x.astype(BF16).astype(F32)
    rest = x - hi
    mid = rest.astype(BF16).astype(F32)
    lo = (rest - mid).astype(BF16).astype(F32)
    return hi, mid, lo


def _place3(base, col, pieces, sign):
    lane = lax.broadcasted_iota(jnp.int32, base.shape, 1)
    out = base.astype(F32)
    for i, piece in enumerate(pieces):
        out = jnp.where(lane == col + i, sign * piece, out)
    return out.astype(BF16)


def _flash_fwd(qa, ka, va, heads, l_col, lse_col, sub_rows, name):
    s = qa.shape[0]
    da, dv = qa.shape[1] // heads, va.shape[1] // heads
    hps = max(n for n in range(1, ATTN_FWD_LANES // max(da, dv) + 1) if heads % n == 0)
    bq, bk = _blk(s, ATTN_BLOCK_Q), _blk(s, ATTN_BLOCK_K)
    tab = _causal_table(s, bq, bk, True)

    def body(tab_ref, q_ref, k_ref, v_ref, o_ref, qb_ref, m_sc, acc_sc):
        t = pl.program_id(1)
        qi, ki = tab_ref[0, t], tab_ref[1, t]

        @pl.when(tab_ref[3, t] == 1)
        def _():
            m_sc[...] = jnp.full_like(m_sc, NEG_BIG)
            acc_sc[...] = jnp.zeros_like(acc_sc)

        def step(masked):
            for hh in range(hps):
                qc, vc = slice(hh * da, (hh + 1) * da), slice(hh * dv, (hh + 1) * dv)
                for r0, nr, c0, nc in _sub_tiles(bq, bk, masked, bq == bk, False, sub_rows[int(masked)]):
                    sc = lax.dot_general(q_ref[r0:r0 + nr, qc], k_ref[c0:c0 + nc, qc], _NT,
                                         preferred_element_type=F32)
                    if masked:
                        sc = jnp.where(_causal_keep(qi * bq + r0, ki * bk + c0, nr, nc, False), sc, NEG_BIG)
                    m_prev = m_sc[hh, r0:r0 + nr, :]
                    m_new = jnp.maximum(m_prev, jnp.max(sc, axis=-1, keepdims=True))
                    p = jnp.exp(sc - m_new).astype(BF16)
                    acc_sc[r0:r0 + nr, vc] = jnp.exp(m_prev - m_new) * acc_sc[r0:r0 + nr, vc] + lax.dot_general(
                        p, v_ref[c0:c0 + nc, vc], _NN, preferred_element_type=F32)
                    m_sc[hh, r0:r0 + nr, :] = m_new

        @pl.when(tab_ref[2, t] == 1)
        def _():
            step(True)

        @pl.when(tab_ref[2, t] == 0)
        def _():
            step(False)

        @pl.when(tab_ref[4, t] == 1)
        def _():
            for hh in range(hps):
                qc, vc = slice(hh * da, (hh + 1) * da), slice(hh * dv, (hh + 1) * dv)
                acc = acc_sc[:, vc]
                lane = lax.broadcasted_iota(jnp.int32, acc.shape, 1)
                l = jnp.sum(jnp.where(lane == l_col, acc, 0.0), axis=-1, keepdims=True)
                o_ref[:, vc] = (acc / l).astype(o_ref.dtype)
                lse = m_sc[hh] + jnp.log(l)
                qb_ref[:, qc] = _place3(q_ref[:, qc], lse_col, _split3_cols(lse), -1.0)

    qspec, kspec = _attn_specs(bq, bk)
    return pl.pallas_call(
        body, name=name,
        grid_spec=pltpu.PrefetchScalarGridSpec(
            num_scalar_prefetch=1, grid=(heads // hps, tab.shape[1]),
            in_specs=[qspec(hps * da), kspec(hps * da), kspec(hps * dv)],
            out_specs=[qspec(hps * dv), qspec(hps * da)],
            scratch_shapes=[pltpu.VMEM((hps, bq, 1), F32), pltpu.VMEM((bq, hps * dv), F32)]),
        out_shape=[jax.ShapeDtypeStruct((s, heads * dv), BF16), jax.ShapeDtypeStruct((s, heads * da), BF16)],
        compiler_params=_params(("parallel", "arbitrary")),
    )(tab, qa, ka, va)


def _delta_epilogue(dv, delta_col):
    def epilogue(acc, o_tile):
        heads_out = []
        for hh in range(acc.shape[1] // dv):
            vc = slice(hh * dv, (hh + 1) * dv)
            dov = acc[:, vc].astype(BF16)
            delta = jnp.sum(dov.astype(F32) * o_tile[:, vc].astype(F32), axis=-1, keepdims=True)
            heads_out.append(_place3(dov, delta_col, _split3_cols(delta), 1.0))
        return (jnp.concatenate(heads_out, axis=1),)
    return epilogue


_TN =(((0,), (0,)), ((), ()))


def _flash_bwd(qa, ka, va, doa, heads, hps, name, sum_cols=None):
    s = qa.shape[0]
    da, dv = qa.shape[1] // heads, va.shape[1] // heads
    h = heads // hps
    bq, bk = _blk(s, ATTN_BLOCK_Q), _blk(s, ATTN_BLOCK_K)
    tab = _causal_table(s, bq, bk, False)
    n_tiles = tab.shape[1]
    n_sum = 0 if sum_cols is None else 2

    def head_column(acc, col):
        out = jnp.zeros((acc.shape[0], hps), F32)
        lane = lax.broadcasted_iota(jnp.int32, (acc.shape[0], da), 1)
        pick = lax.broadcasted_iota(jnp.int32, out.shape, 1)
        for hh in range(hps):
            val = jnp.sum(jnp.where(lane == col, acc[:, hh * da:(hh + 1) * da], 0.0), axis=-1, keepdims=True)
            out = jnp.where(pick == hh, val, out)
        return out

    def body(tab_ref, q_ref, k_ref, v_ref, do_ref, dq_ref, dk_ref, dv_ref, *rest):
        sum_refs, (dk_sc, dv_sc) = rest[:n_sum], rest[n_sum:]
        t = pl.program_id(1)
        qi, ki = tab_ref[0, t], tab_ref[1, t]

        @pl.when(t == 0)
        def _():
            dq_ref[...] = jnp.zeros_like(dq_ref)

        @pl.when(tab_ref[3, t] == 1)
        def _():
            dk_sc[...] = jnp.zeros_like(dk_sc)
            dv_sc[...] = jnp.zeros_like(dv_sc)

        def step(masked):
            for hh in range(hps):
                qc, vc = slice(hh * da, (hh + 1) * da), slice(hh * dv, (hh + 1) * dv)
                for r0, nr, c0, nc in _sub_tiles(bk, bq, masked, bq == bk, True, ATTN_SUB_ROWS):
                    qv, dov, kv = q_ref[c0:c0 + nc, qc], do_ref[c0:c0 + nc, vc], k_ref[r0:r0 + nr, qc]
                    st = lax.dot_general(kv, qv, _NT, preferred_element_type=F32)
                    if masked:
                        st = jnp.where(_causal_keep(qi * bq + c0, ki * bk + r0, nc, nr, True), st, NEG_BIG)
                    pt = jnp.exp(st)
                    dv_sc[r0:r0 + nr, vc] += lax.dot_general(pt.astype(BF16), dov, _NN, preferred_element_type=F32)
                    dpt = lax.dot_general(v_ref[r0:r0 + nr, vc], dov, _NT, preferred_element_type=F32)
                    dst = (pt * dpt).astype(BF16)
                    dk_sc[r0:r0 + nr, qc] += lax.dot_general(dst, qv, _NN, preferred_element_type=F32)
                    q_rows = pl.ds(pl.multiple_of(qi * bq + c0, ATTN_SUB_ROWS), nc)
                    dq_ref[q_rows, qc] += lax.dot_general(dst, kv, _TN, preferred_element_type=F32)

        @pl.when(tab_ref[2, t] == 1)
        def _():
            step(True)

        @pl.when(tab_ref[2, t] == 0)
        def _():
            step(False)

        @pl.when(tab_ref[4, t] == 1)
        def _():
            dk_ref[...] = dk_sc[...].astype(dk_ref.dtype)
            dv_ref[...] = dv_sc[...].astype(dv_ref.dtype)
            if n_sum:
                sum_refs[1][...] = head_column(dk_sc[...], sum_cols[1])

        if n_sum:
            @pl.when(t == n_tiles - 1)
            def _():
                sum_refs[0][...] = head_column(dq_ref[...], sum_cols[0])

    qspec, kspec = _attn_specs(bq, bk)
    out_specs = [pl.BlockSpec((s, hps * da), lambda hh, t, tb: (0, hh), pipeline_mode=pl.Buffered(1)),
                 kspec(hps * da), kspec(hps * dv)]
    out_shape = [jax.ShapeDtypeStruct((s, heads * da), F32), jax.ShapeDtypeStruct((s, heads * da), BF16),
                 jax.ShapeDtypeStruct((s, heads * dv), BF16)]
    if n_sum:
        out_specs += [pl.BlockSpec((None, s, hps), lambda hh, t, tb: (hh, 0, 0), pipeline_mode=pl.Buffered(1)),
                      pl.BlockSpec((None, bk, hps), lambda hh, t, tb: (hh, tb[1, t], 0))]
        out_shape += [jax.ShapeDtypeStruct((h, s, hps), F32)] * 2
    return pl.pallas_call(
        body, name=name,
        grid_spec=pltpu.PrefetchScalarGridSpec(
            num_scalar_prefetch=1, grid=(h, n_tiles),
            in_specs=[qspec(hps * da), kspec(hps * da), kspec(hps * dv), qspec(hps * dv)],
            out_specs=out_specs,
            scratch_shapes=[pltpu.VMEM((bk, hps * da), F32), pltpu.VMEM((bk, hps * dv), F32)]),
        out_shape=out_shape,
        compiler_params=_params(("parallel", "arbitrary")),
    )(tab, qa, ka, va, doa)


def _split3(x):
    hi = lax.reduce_precision(x, 8, 7)
    rest = x - hi
    mid = lax.reduce_precision(rest, 8, 7)
    lo = lax.reduce_precision(rest - mid, 8, 7)
    return jnp.stack([hi, mid, lo], axis=-1).astype(BF16)


def _pad_heads(w, heads, width, axis):
    shape = list(w.shape)
    d = shape[axis] // heads
    w = w.reshape(shape[:axis] + [heads, d] + shape[axis + 1:])
    pad = [(0, 0)] * w.ndim
    pad[axis + 1] = (0, width - d)
    return jnp.pad(w, pad).reshape(shape[:axis] + [heads * width] + shape[axis + 1:])


def _unpad_heads(w, heads, d, axis):
    shape = list(w.shape)
    width = shape[axis] // heads
    w = w.reshape(shape[:axis] + [heads, width] + shape[axis + 1:])
    w = lax.slice_in_dim(w, 0, d, axis=axis + 1)
    return w.reshape(shape[:axis] + [heads * d] + shape[axis + 1:])


def _placement(rows, heads, width, entries):
    e = np.zeros((rows, heads * width), np.float32)
    for row, col, val in entries:
        for hh in range(heads):
            e[row(hh) if callable(row) else row, hh * width + col] = val
    return jnp.asarray(e, BF16)


def _rope_mix(a, b, cos_t, sin_t, scale, heads, name):
    s = a.shape[0]
    d = a.shape[1] // heads
    bs = _blk(s, 1024, 8)

    def body(a_ref, b_ref, c_ref, s_ref, o_ref):
        o_ref[...] = ((a_ref[...] * c_ref[...] + b_ref[...] * s_ref[...]) * scale).astype(o_ref.dtype)

    blk = pl.BlockSpec((bs, d), lambda i, hh: (i, hh))
    tbl = pl.BlockSpec((bs, d), lambda i, hh: (i, 0))
    return pl.pallas_call(
        body, name=name, grid=(s // bs, heads), in_specs=[blk, blk, tbl, tbl], out_specs=blk,
        out_shape=jax.ShapeDtypeStruct(a.shape, BF16),
        compiler_params=_params(("parallel", "parallel")),
    )(a, b, cos_t, sin_t)


def _rope_proj(x, w_a, w_b, cos_t, sin_t, scale, heads, name):
    s, kdim = x.shape
    d = w_a.shape[1] // heads
    hpt = max(n for n in range(1, max(1, MATMUL_BLOCK // d) + 1) if heads % n == 0)
    bm = _blk(s, MATMUL_BLOCK)
    cos_w, sin_w = jnp.tile(cos_t, (1, hpt)), jnp.tile(sin_t, (1, hpt))

    def body(x_ref, wa_ref, wb_ref, c_ref, s_ref, o_ref):
        xv = x_ref[...]
        a = lax.dot_general(xv, wa_ref[...], _NN, preferred_element_type=F32)
        b = lax.dot_general(xv, wb_ref[...], _NN, preferred_element_type=F32)
        o_ref[...] = ((a * c_ref[...] + b * s_ref[...]) * scale).astype(o_ref.dtype)

    wide = pl.BlockSpec((bm, hpt * d), lambda i, j: (i, j))
    tbl = pl.BlockSpec((bm, hpt * d), lambda i, j: (i, 0))
    wgt = pl.BlockSpec((kdim, hpt * d), lambda i, j: (0, j))
    return pl.pallas_call(
        body, name=name, grid=(s // bm, heads // hpt),
        in_specs=[pl.BlockSpec((bm, kdim), lambda i, j: (i, 0)), wgt, wgt, tbl, tbl], out_specs=wide,
        out_shape=jax.ShapeDtypeStruct((s, heads * d), BF16),
        compiler_params=_params(("parallel", "parallel")),
    )(x, w_a, w_b, cos_w, sin_w)


def _rope_proj_bwd(x, g, w_a, w_b, cos_t, sin_t, scale, heads, name):
    s, kdim = x.shape
    d = w_a.shape[1] // heads
    hpt = max(n for n in range(1, max(1, MATMUL_BLOCK // d) + 1) if heads % n == 0)
    bs = _blk(s, MATMUL_BLOCK)
    n_tiles, n_rows = heads // hpt, s // bs
    cos_w, sin_w = jnp.tile(cos_t, (1, hpt)), jnp.tile(sin_t, (1, hpt))

    def halves(g_ref, c_ref, s_ref):
        gv = g_ref[...] * scale
        return (gv * c_ref[...]).astype(BF16), (gv * s_ref[...]).astype(BF16)

    def dx_body(g_ref, wa_ref, wb_ref, c_ref, s_ref, dx_ref, acc_ref):
        j = pl.program_id(1)
        ga, gb = halves(g_ref, c_ref, s_ref)
        part = (lax.dot_general(ga, wa_ref[...], _NT, preferred_element_type=F32)
                + lax.dot_general(gb, wb_ref[...], _NT, preferred_element_type=F32))

        @pl.when(j == 0)
        def _():
            acc_ref[...] = part

        @pl.when(j > 0)
        def _():
            acc_ref[...] += part

        @pl.when(j == n_tiles - 1)
        def _():
            dx_ref[...] = acc_ref[...]

    def dw_body(x_ref, g_ref, c_ref, s_ref, dwa_ref, dwb_ref, acc_a, acc_b):
        i = pl.program_id(1)
        ga, gb = halves(g_ref, c_ref, s_ref)
        xv = x_ref[...]
        pa = lax.dot_general(xv, ga, _TN, preferred_element_type=F32)
        pb = lax.dot_general(xv, gb, _TN, preferred_element_type=F32)

        @pl.when(i == 0)
        def _():
            acc_a[...] = pa
            acc_b[...] = pb

        @pl.when(i > 0)
        def _():
            acc_a[...] += pa
            acc_b[...] += pb

        @pl.when(i == n_rows - 1)
        def _():
            dwa_ref[...] = acc_a[...]
            dwb_ref[...] = acc_b[...]

    dx = pl.pallas_call(
        dx_body, name=name + "_dx", grid=(n_rows, n_tiles),
        in_specs=[pl.BlockSpec((bs, hpt * d), lambda i, j: (i, j)),
                  pl.BlockSpec((kdim, hpt * d), lambda i, j: (0, j)), pl.BlockSpec((kdim, hpt * d), lambda i, j: (0, j)),
                  pl.BlockSpec((bs, hpt * d), lambda i, j: (i, 0)), pl.BlockSpec((bs, hpt * d), lambda i, j: (i, 0))],
        out_specs=pl.BlockSpec((bs, kdim), lambda i, j: (i, 0)),
        out_shape=jax.ShapeDtypeStruct((s, kdim), F32),
        scratch_shapes=[pltpu.VMEM((bs, kdim), F32)],
        compiler_params=_params(("parallel", "arbitrary")),
    )(g, w_a, w_b, cos_w, sin_w)
    dwa, dwb = pl.pallas_call(
        dw_body, name=name + "_dw", grid=(n_tiles, n_rows),
        in_specs=[pl.BlockSpec((bs, kdim), lambda j, i: (i, 0)),
                  pl.BlockSpec((bs, hpt * d), lambda j, i: (i, j)),
                  pl.BlockSpec((bs, hpt * d), lambda j, i: (i, 0)), pl.BlockSpec((bs, hpt * d), lambda j, i: (i, 0))],
        out_specs=[pl.BlockSpec((kdim, hpt * d), lambda j, i: (0, j))] * 2,
        out_shape=[jax.ShapeDtypeStruct((kdim, heads * d), F32)] * 2,
        scratch_shapes=[pltpu.VMEM((kdim, hpt * d), F32)] * 2,
        compiler_params=_params(("parallel", "arbitrary")),
    )(x, g, cos_w, sin_w)
    return dx, dwa, dwb


def _adamw(w, g, m, v, name):
    r, wd = w.shape
    br = _blk(r, 512, 8)

    def body(w_ref, g_ref, m_ref, v_ref, d_ref, nm_ref, nv_ref):
        gv = g_ref[...]
        mn = ADAM_B1 * m_ref[...] + (1.0 - ADAM_B1) * gv
        vn = ADAM_B2 * v_ref[...] + (1.0 - ADAM_B2) * (gv * gv)
        m_hat = mn / (1.0 - ADAM_B1 ** ADAM_STEP)
        v_hat = vn / (1.0 - ADAM_B2 ** ADAM_STEP)
        d_ref[...] = -ADAM_LR * (m_hat / (jnp.sqrt(v_hat) + ADAM_EPS) + ADAM_WD * w_ref[...])
        nm_ref[...] = mn
        nv_ref[...] = vn

    row = pl.BlockSpec((br, wd), lambda i: (i, 0))
    return pl.pallas_call(
        body, name=name, grid=(r // br,), in_specs=[row] * 4, out_specs=[row] * 3,
        out_shape=[jax.ShapeDtypeStruct((r, wd), F32)] * 3,
        compiler_params=_params(("parallel",)),
    )(w, g, m, v)


_ANY = pl.BlockSpec(memory_space=pl.ANY)


def _place():
    return lax.axis_index("x"), lax.axis_index("y"), lax.axis_index("c"), None


def _all_gather_shards(shard, name):
    r, w = shard.shape
    hr = r // 2
    qr = hr // 2

    def body(x_ref, out_ref, send_sems, recv_sems):
        x, y, c, _ = _place()
        me, sibling, y_nbr, x_nbr = (x, y, c), (x, y, 1 - c), (x, 1 - y, c), (1 - x, y, c)

        def rows(j, half, piece=None):
            if piece is None:
                return out_ref.at[j, pl.ds(pl.multiple_of(half * hr, 16), hr), :]
            return out_ref.at[j, pl.ds(pl.multiple_of(half * hr + piece * qr, 16), qr), :]

        def mine(piece):
            return x_ref.at[pl.ds(pl.multiple_of(c * hr + piece * qr, 16), qr), :]

        def copy(sem, src, dst, to):
            return pltpu.make_async_remote_copy(src_ref=src, dst_ref=dst, send_sem=send_sems.at[sem],
                                                recv_sem=recv_sems.at[sem], device_id=to, device_id_type=MESH)

        sent = [copy(0, mine(0), rows(0, c, 0), y_nbr), copy(1, mine(1), rows(0, c, 1), y_nbr),
                copy(2, mine(0), rows(1, c, 0), x_nbr), copy(3, mine(1), rows(1, c, 1), x_nbr)]
        for cp in sent:
            cp.start()

        def landed(sem, ref):
            copy(sem, ref, ref, me).wait_recv()

        def pass_on(sem, src, dst, to):
            cp = copy(sem, src, dst, to)
            cp.start()
            sent.append(cp)

        landed(2, rows(1, c, 0))
        pass_on(4, rows(1, c, 0), rows(2, c, 0), y_nbr)
        landed(1, rows(0, c, 1))
        pass_on(5, rows(0, c, 1), rows(2, c, 1), x_nbr)
        landed(0, rows(0, c, 0))
        pass_on(6, rows(0, c), rows(0, c), sibling)
        landed(3, rows(1, c, 1))
        pass_on(7, rows(1, c), rows(1, c), sibling)
        landed(4, rows(2, c, 0))
        landed(5, rows(2, c, 1))
        pass_on(8, rows(2, c), rows(2, c), sibling)
        for j in range(3):
            landed(6 + j, rows(j, 1 - c))
        for cp in sent:
            cp.wait_send()

    return pl.pallas_call(
        body, name=name, in_specs=[_ANY], out_specs=_ANY,
        out_shape=jax.ShapeDtypeStruct((N_CHIPS - 1, r, w), shard.dtype),
        scratch_shapes=[pltpu.SemaphoreType.DMA((9,)), pltpu.SemaphoreType.DMA((9,))],
        compiler_params=pltpu.CompilerParams(vmem_limit_bytes=VMEM_LIMIT_BYTES),
    )(shard)


def _sibling_swap_halves(g, name):
    nq, r, w = g.shape
    hr = r // 2

    def body(g_ref, a_ref, send_sems, recv_sems):
        x, y, c, _ = _place()
        sibling = (x, y, 1 - c)
        cps = []
        for q in range(nq):
            cp = pltpu.make_async_remote_copy(
                src_ref=g_ref.at[q, pl.ds(pl.multiple_of((1 - c) * hr, 8), hr), :], dst_ref=a_ref.at[q],
                send_sem=send_sems.at[q], recv_sem=recv_sems.at[q], device_id=sibling, device_id_type=MESH)
            cp.start()
            cps.append(cp)
        for cp in cps:
            cp.wait()

    return pl.pallas_call(
        body, name=name, in_specs=[_ANY], out_specs=_ANY,
        out_shape=jax.ShapeDtypeStruct((nq, hr, w), g.dtype),
        scratch_shapes=[pltpu.SemaphoreType.DMA((nq,)), pltpu.SemaphoreType.DMA((nq,))],
        compiler_params=pltpu.CompilerParams(vmem_limit_bytes=VMEM_LIMIT_BYTES),
    )(g)


def _chip_sum(g, a, c_idx, name):
    nq, r, w = g.shape
    hr = r // 2
    br = _blk(hr, 512, 16)
    nb = hr // br

    def body(c_ref, g_ref, a_ref, o_ref):
        o_ref[...] = (g_ref[...] + a_ref[...]).astype(o_ref.dtype)

    return pl.pallas_call(
        body, name=name,
        grid_spec=pltpu.PrefetchScalarGridSpec(
            num_scalar_prefetch=1, grid=(nq, nb),
            in_specs=[pl.BlockSpec((None, br, w), lambda q, i, cr: (q, cr[0] * nb + i, 0)),
                      pl.BlockSpec((None, br, w), lambda q, i, cr: (q, i, 0))],
            out_specs=pl.BlockSpec((None, br, w), lambda q, i, cr: (q, i, 0))),
        out_shape=jax.ShapeDtypeStruct((nq, hr, w), BF16),
        compiler_params=_params(("parallel", "parallel")),
    )(c_idx, g, a)


def _rs_first_hop(s4, name):
    nq, hr, w = s4.shape
    qr = hr // 2

    def body(s_ref, r_ref, send_sems, recv_sems):
        x, y, c, _ = _place()
        p = 2 * x + y
        x_nbr, y_nbr = (1 - x, y, c), (x, 1 - y, c)

        def piece(q, k):
            return s_ref.at[q, pl.ds(k * qr, qr), :]

        sends = [(piece(p ^ 2, 0), x_nbr), (piece(p ^ 3, 0), x_nbr), (piece(p ^ 1, 1), y_nbr), (piece(p ^ 3, 1), y_nbr)]
        cps = []
        for k, (src, to) in enumerate(sends):
            cp = pltpu.make_async_remote_copy(src_ref=src, dst_ref=r_ref.at[k], send_sem=send_sems.at[k],
                                              recv_sem=recv_sems.at[k], device_id=to, device_id_type=MESH)
            cp.start()
            cps.append(cp)
        for cp in cps:
            cp.wait()

    return pl.pallas_call(
        body, name=name, in_specs=[_ANY], out_specs=_ANY,
        out_shape=jax.ShapeDtypeStruct((nq, qr, w), s4.dtype),
        scratch_shapes=[pltpu.SemaphoreType.DMA((nq,)), pltpu.SemaphoreType.DMA((nq,))],
        compiler_params=pltpu.CompilerParams(vmem_limit_bytes=VMEM_LIMIT_BYTES),
    )(s4)


def _rs_middle(s4, r1, chip_idx, name):
    _, hr, w = s4.shape
    qr = hr // 2
    br = _blk(qr, 512, 16)
    nb = qr // br

    def body(idx_ref, mine_ref, theirs_ref, got_mine_ref, got_theirs_ref, own_ref, onward_ref):
        own_ref[...] = mine_ref[...].astype(F32) + got_mine_ref[...].astype(F32)
        onward_ref[...] = (theirs_ref[...].astype(F32) + got_theirs_ref[...].astype(F32)).astype(onward_ref.dtype)

    return pl.pallas_call(
        body, name=name,
        grid_spec=pltpu.PrefetchScalarGridSpec(
            num_scalar_prefetch=1, grid=(2, nb),
            in_specs=[pl.BlockSpec((None, br, w), lambda k, i, ix: (ix[0], k * nb + i, 0)),
                      pl.BlockSpec((None, br, w), lambda k, i, ix: (ix[0] ^ (k + 1), k * nb + i, 0)),
                      pl.BlockSpec((None, br, w), lambda k, i, ix: (2 * k, i, 0)),
                      pl.BlockSpec((None, br, w), lambda k, i, ix: (2 * k + 1, i, 0))],
            out_specs=[pl.BlockSpec((br, w), lambda k, i, ix: (k * nb + i, 0)),
                       pl.BlockSpec((None, br, w), lambda k, i, ix: (k, i, 0))]),
        out_shape=[jax.ShapeDtypeStruct((hr, w), F32), jax.ShapeDtypeStruct((2, qr, w), s4.dtype)],
        compiler_params=_params(("parallel", "parallel")),
    )(chip_idx, s4, s4, r1, r1)


def _rs_second_hop(onward, name):
    def body(u_ref, r_ref, send_sems, recv_sems):
        x, y, c, _ = _place()
        cps = []
        for k, to in enumerate([(x, 1 - y, c), (1 - x, y, c)]):
            cp = pltpu.make_async_remote_copy(src_ref=u_ref.at[k], dst_ref=r_ref.at[k], send_sem=send_sems.at[k],
                                              recv_sem=recv_sems.at[k], device_id=to, device_id_type=MESH)
            cp.start()
            cps.append(cp)
        for cp in cps:
            cp.wait()

    return pl.pallas_call(
        body, name=name, in_specs=[_ANY], out_specs=_ANY,
        out_shape=jax.ShapeDtypeStruct(onward.shape, onward.dtype),
        scratch_shapes=[pltpu.SemaphoreType.DMA((2,)), pltpu.SemaphoreType.DMA((2,))],
        compiler_params=pltpu.CompilerParams(vmem_limit_bytes=VMEM_LIMIT_BYTES),
    )(onward)


def _rs_last_add(own, r2, name):
    hr, w = own.shape
    br = _blk(hr // 2, 512, 16)

    def body(a_ref, b_ref, o_ref):
        o_ref[...] = a_ref[...] + b_ref[...].astype(F32)

    row = pl.BlockSpec((br, w), lambda i: (i, 0))
    return pl.pallas_call(
        body, name=name, grid=(hr // br,), in_specs=[row, row], out_specs=row,
        out_shape=jax.ShapeDtypeStruct((hr, w), F32),
        compiler_params=_params(("parallel",)),
    )(own, r2.reshape(hr, w))


def _sibling_swap(t, name):
    hr, w = t.shape

    def body(t_ref, o_ref, send_sem, recv_sem):
        x, y, c, _ = _place()
        cp = pltpu.make_async_remote_copy(src_ref=t_ref, dst_ref=o_ref, send_sem=send_sem, recv_sem=recv_sem,
                                          device_id=(x, y, 1 - c), device_id_type=MESH)
        cp.start()
        cp.wait()

    return pl.pallas_call(
        body, name=name, in_specs=[_ANY], out_specs=_ANY,
        out_shape=jax.ShapeDtypeStruct((hr, w), t.dtype),
        scratch_shapes=[pltpu.SemaphoreType.DMA, pltpu.SemaphoreType.DMA],
        compiler_params=pltpu.CompilerParams(vmem_limit_bytes=VMEM_LIMIT_BYTES),
    )(t)


def _all_reduce_small(v, name):
    r, w = v.shape

    def body(v_ref, o_ref, slots, send_sems, recv_sems):
        x, y, c, _ = _place()
        me = 4 * x + 2 * y + c
        slots[me] = v_ref[...]
        cps = []
        for k in range(1, N_DEV):
            fx, fy, fc = (k >> 2) & 1, (k >> 1) & 1, k & 1
            to = (x ^ fx, y ^ fy, c ^ fc)
            cp = pltpu.make_async_remote_copy(
                src_ref=v_ref, dst_ref=slots.at[me], send_sem=send_sems.at[k - 1], recv_sem=recv_sems.at[k - 1],
                device_id=to, device_id_type=MESH)
            cp.start()
            cps.append(cp)
        for k in range(1, N_DEV):
            fx, fy, fc = (k >> 2) & 1, (k >> 1) & 1, k & 1
            src_dev = 4 * (x ^ fx) + 2 * (y ^ fy) + (c ^ fc)
            pltpu.make_async_remote_copy(
                src_ref=v_ref, dst_ref=slots.at[src_dev], send_sem=send_sems.at[k - 1],
                recv_sem=recv_sems.at[k - 1], device_id=(x, y, c), device_id_type=MESH).wait_recv()
        for cp in cps:
            cp.wait_send()
        acc = slots[0]
        for d in range(1, N_DEV):
            acc = acc + slots[d]
        o_ref[...] = acc

    return pl.pallas_call(
        body, name=name,
        in_specs=[pl.BlockSpec(memory_space=pltpu.VMEM)], out_specs=pl.BlockSpec(memory_space=pltpu.VMEM),
        out_shape=jax.ShapeDtypeStruct((r, w), F32),
        scratch_shapes=[pltpu.VMEM((N_DEV, r, w), F32), pltpu.SemaphoreType.DMA((N_DEV - 1,)),
                        pltpu.SemaphoreType.DMA((N_DEV - 1,))],
        compiler_params=pltpu.CompilerParams(vmem_limit_bytes=VMEM_LIMIT_BYTES),
    )(v)


def _part_rows(shape, part_rows=PACK_PART_ROWS):
    assert shape[-1] <= PACK_LANES
    return _round_up(math.prod(shape[:-1]), part_rows)


def _packed_rows(shapes):
    return _round_up(sum(_part_rows(s) for s in shapes), PACK_ROWS_MULT)


def _pack(arrs, total_rows, dtype, part_rows=PACK_PART_ROWS):
    parts = []
    for a in arrs:
        a2 = a.reshape(-1, a.shape[-1]).astype(dtype)
        rows = _part_rows(a.shape, part_rows)
        parts.append(jnp.pad(a2, ((0, rows - a2.shape[0]), (0, PACK_LANES - a2.shape[1]))))
    used = sum(p.shape[0] for p in parts)
    if total_rows > used:
        parts.append(jnp.zeros((total_rows - used, PACK_LANES), dtype))
    return jnp.concatenate(parts, axis=0)


def _unpack(packed, shapes, part_rows=PACK_PART_ROWS):
    out, r0 = [], 0
    for s in shapes:
        out.append(packed[r0:r0 + math.prod(s[:-1]), :s[-1]].reshape(s))
        r0 += _part_rows(s, part_rows)
    return out


_BIG = (("fox_w_in", 2), ("fox_w_out", 1), ("mla_w_kv_a", 0), ("mla_w_kv_b", 1), ("mla_w_q_a", 1),
        ("mla_w_q_b", 2), ("mla_w_out", 1), ("ffn_w_up", 2), ("ffn_w_down", 1))
_SMALL = ("norm_mix_g", "norm_ffn_g", "fox_b_f", "kv_norm_g", "mla_kv_a_norm_g", "mla_q_a_norm_g", "final_norm_g")
_WEIGHTS = ("norm_mix_g", "norm_ffn_g", "fox_w_in", "fox_b_f", "fox_w_out", "kv_norm_g", "mla_w_kv_a",
            "mla_kv_a_norm_g", "mla_w_kv_b", "mla_w_q_a", "mla_q_a_norm_g", "mla_w_q_b", "mla_w_out",
            "ffn_w_up", "ffn_w_down", "final_norm_g")


def _ffn_fwd(x, h, w_up, w_down, tag):
    def relu_sq(acc):
        r = jnp.maximum(acc, 0.0)
        return r, r * r

    r, a = _matmul(h, w_up, mode="nn", out_dtypes=(BF16, BF16), epilogue=relu_sq, name=f"{tag}_up")
    x_out = _matmul(a, w_down, mode="nn", out_dtypes=(F32,), epilogue=lambda acc, res: (acc + res,),
                    extras=(x,), name=f"{tag}_down")
    return x_out, r, a


def _ffn_bwd(dx_pair, x_in, h, r, a, g_norm, w_up, w_down, tag):
    dx_out, dx_lo = dx_pair
    d_u = _matmul(dx_lo, w_down, mode="nt", out_dtypes=(BF16,), epilogue=lambda acc, rr: (acc * (2.0 * rr.astype(F32)),),
                  extras=(r,), name=f"{tag}_d_act")
    d_w_down = _matmul(a, dx_lo, mode="tn", out_dtypes=(F32,), name=f"{tag}_d_w_down")
    d_w_up = _matmul(h, d_u, mode="tn", out_dtypes=(F32,), by_chip=True, name=f"{tag}_d_w_up")
    d_h = _matmul(d_u, w_up, mode="nt", out_dtypes=(F32,), name=f"{tag}_d_h")
    dx_in, (d_g,) = _rms_bwd(x_in, [(g_norm, d_h)], dx_out, name=f"{tag}_d_norm")
    return dx_in, d_w_up, d_w_down, d_g


def kernel(x, norm_mix_g, norm_ffn_g, fox_w_in, fox_b_f, fox_w_out, kv_norm_g, mla_w_kv_a, mla_kv_a_norm_g, mla_w_kv_b, mla_w_q_a, mla_q_a_norm_g, mla_w_q_b, mla_w_out, ffn_w_up, ffn_w_down, final_norm_g, loss_target, m_norm_mix_g, m_norm_ffn_g, m_fox_w_in, m_fox_b_f, m_fox_w_out, m_kv_norm_g, m_mla_w_kv_a, m_mla_kv_a_norm_g, m_mla_w_kv_b, m_mla_w_q_a, m_mla_q_a_norm_g, m_mla_w_q_b, m_mla_w_out, m_ffn_w_up, m_ffn_w_down, m_final_norm_g, v_norm_mix_g, v_norm_ffn_g, v_fox_w_in, v_fox_b_f, v_fox_w_out, v_kv_norm_g, v_mla_w_kv_a, v_mla_kv_a_norm_g, v_mla_w_kv_b, v_mla_w_q_a, v_mla_q_a_norm_g, v_mla_w_q_b, v_mla_w_out, v_ffn_w_up, v_ffn_w_down, v_final_norm_g):
    args = dict(locals())
    w_in = {n: args[n] for n in _WEIGHTS}
    m_in = {n: args["m_" + n] for n in _WEIGHTS}
    v_in = {n: args["v_" + n] for n in _WEIGHTS}

    xs = x[0]
    seq, d_model = xs.shape
    tgt = loss_target[0]
    fox_h, mla_h, nope = FOX_HEADS, MLA_HEADS, QK_NOPE_DIM
    kv_rank = mla_kv_a_norm_g.shape[0]
    rope = mla_w_kv_a.shape[1] - kv_rank
    half = rope // 2
    q_rank = mla_q_a_norm_g.shape[1]
    v_dim = mla_w_kv_b.shape[1] * N_CHIPS // mla_h - nope
    fox_w = fox_w_out.shape[1] * N_CHIPS
    fox_dh = fox_w // fox_h

    big_names = [n for n, _ in _BIG]
    shard_shapes = [w_in[n].shape for n in big_names]
    rows = _packed_rows(shard_shapes)
    my_shard = _pack([w_in[n] for n in big_names], rows, BF16)
    others = _all_gather_shards(my_shard, name="gather_weights")
    by_relation = jnp.concatenate([my_shard[None], others], axis=0)
    p_chip = 2 * lax.axis_index("x") + lax.axis_index("y")
    full = {}
    for q in range(N_CHIPS):
        shard_q = lax.dynamic_index_in_dim(by_relation, p_chip ^ q, axis=0, keepdims=False)
        for (n, ax), piece in zip(_BIG, _unpack(shard_q, shard_shapes)):
            full.setdefault(n, []).append(piece)
    full = {n: jnp.concatenate(full[n], axis=ax) for n, ax in _BIG}

    fox_scale = fox_dh ** -0.5
    fox_wd = _round_up(fox_dh + 9, LANE_TILE)
    fox_vwd = _round_up(fox_dh + 4, LANE_TILE)
    w_fox_in = full["fox_w_in"][0]
    w_fq = _pad_heads(w_fox_in[:, :fox_w] * fox_scale, fox_h, fox_wd, 1)
    w_fk = _pad_heads(w_fox_in[:, fox_w:2 * fox_w], fox_h, fox_wd, 1)
    w_fv = _pad_heads(w_fox_in[:, 2 * fox_w:3 * fox_w], fox_h, fox_vwd, 1)
    w_gate = w_fox_in[:, 3 * fox_w:]
    w_fox_out = _pad_heads(full["fox_w_out"][0], fox_h, fox_vwd, 0)
    n_cx = _round_up(3 * fox_h + 1, LANE_TILE)
    c_piece = lambda i: (lambda hh: 3 * hh + i)
    one_col = 3 * fox_h
    e_fq = _placement(n_cx, fox_h, fox_wd, [(c_piece(i), fox_dh + i, 1.0) for i in range(3)]
                      + [(one_col, fox_dh + 3 + i, 1.0) for i in range(3)])
    e_fk = _placement(n_cx, fox_h, fox_wd, [(one_col, fox_dh + i, 1.0) for i in range(3)]
                      + [(c_piece(i), fox_dh + 3 + i, -1.0) for i in range(3)]
                      + [(one_col, fox_dh + 6 + i, 1.0) for i in range(3)])
    e_fv = _placement(n_cx, fox_h, fox_vwd, [(one_col, fox_dh + i, -1.0) for i in range(3)]
                      + [(one_col, fox_dh + 3, 1.0)])

    mla_scale = (nope + rope) ** -0.5
    mla_dk = nope + rope
    mla_wd = _round_up(mla_dk + 3, LANE_TILE)
    mla_vwd = _round_up(v_dim + 4, LANE_TILE)
    w_kv_a = full["mla_w_kv_a"]
    w_kv_b3 = full["mla_w_kv_b"].reshape(kv_rank, mla_h, nope + v_dim)
    w_kn = _pad_heads(w_kv_b3[:, :, :nope].reshape(kv_rank, -1), mla_h, mla_wd, 1)
    w_mv = _pad_heads(w_kv_b3[:, :, nope:].reshape(kv_rank, -1), mla_h, mla_vwd, 1)
    w_q_a = full["mla_w_q_a"][0]
    w_q_b3 = full["mla_w_q_b"][0].reshape(q_rank, mla_h, nope + rope)
    w_qa_ = _pad_heads(w_q_b3.reshape(q_rank, -1), mla_h, mla_wd, 1)
    w_qb_ = _pad_heads(jnp.concatenate([jnp.zeros_like(w_q_b3[:, :, :nope]), -w_q_b3[:, :, nope + half:],
                                        w_q_b3[:, :, nope:nope + half]], axis=-1).reshape(q_rank, -1),
                       mla_h, mla_wd, 1)
    w_mla_out = _pad_heads(full["mla_w_out"][0], mla_h, mla_vwd, 0)
    w_up, w_down = full["ffn_w_up"], full["ffn_w_down"]
    n_kx = _round_up(rope + 1, LANE_TILE)
    e_mk = _placement(n_kx, mla_h, mla_wd, [(j, nope + j, 1.0) for j in range(rope)]
                      + [(rope, mla_dk + i, 1.0) for i in range(3)])
    e_mv = _placement(n_kx, mla_h, mla_vwd, [(rope, v_dim + i, -1.0) for i in range(3)] + [(rope, v_dim + 3, 1.0)])
    e_kr_u = _placement(n_kx, mla_h, mla_wd, [(j, nope + j, 1.0) for j in range(rope)]).T
    e_kr_v = _placement(n_kx, mla_h, mla_wd, [(j, nope + half + j, 1.0) for j in range(half)]
                        + [(half + j, nope + j, -1.0) for j in range(half)]).T

    inv = 1.0 / (ROPE_BASE ** (jnp.arange(0, rope, 2, dtype=F32) / rope))
    ang = jnp.arange(seq, dtype=F32)[:, None] * inv[None, :]
    cos, sin = jnp.cos(ang), jnp.sin(ang)
    pad_t = jnp.zeros((seq, mla_wd - mla_dk), F32)
    cos_t = jnp.concatenate([jnp.ones((seq, nope), F32), cos, cos, pad_t], axis=1)
    sin_t = jnp.concatenate([jnp.zeros((seq, nope), F32), sin, sin, pad_t], axis=1)
    pad_k = jnp.zeros((seq, n_kx - rope), F32)
    cos_k, sin_k = jnp.concatenate([cos, cos, pad_k], axis=1), jnp.concatenate([sin, sin, pad_k], axis=1)

    (h0,) = _rms_fwd(xs, norm_mix_g[0:1], name="l0_norm_mix")
    gate = _matmul(h0, w_gate, mode="nn", out_dtypes=(F32,), name="fox_gate")
    z = gate + fox_b_f[0][None, :]
    cum = jnp.cumsum(jax.nn.log_sigmoid(z), axis=0)
    cx = jnp.concatenate([_split3(cum).reshape(seq, 3 * fox_h), jnp.ones((seq, 1), BF16),
                          jnp.zeros((seq, n_cx - 3 * fox_h - 1), BF16)], axis=1)
    fqa = _matmul(h0, w_fq, mode="nn", out_dtypes=(BF16,), placed=(cx, e_fq), name="fox_q")
    fka = _matmul(h0, w_fk, mode="nn", out_dtypes=(BF16,), placed=(cx, e_fk), name="fox_k")
    fva = _matmul(h0, w_fv, mode="nn", out_dtypes=(BF16,), placed=(cx, e_fv), name="fox_v")
    foa, fqb = _flash_fwd(fqa, fka, fva, fox_h, fox_dh + 3, fox_dh + 6, FOX_FWD_SUB_ROWS, name="fox_attn")
    add_res = lambda acc, res: (acc + res,)
    x1 = _matmul(foa, w_fox_out, mode="nn", out_dtypes=(F32,), epilogue=add_res, extras=(xs,), name="fox_out")
    (h1,) = _rms_fwd(x1, norm_ffn_g[0:1], name="l0_norm_ffn")
    x2, r0, a0 = _ffn_fwd(x1, h1, w_up[0], w_down[0], "ffn0")

    src, h2 = _rms_fwd(x2, jnp.stack([kv_norm_g, norm_mix_g[1]]), name="l1_norm_kv_mix")
    kv_a = _matmul(src, w_kv_a, mode="nn", out_dtypes=(F32,), name="mla_kv_a")
    (c_kv,) = _rms_fwd(kv_a, mla_kv_a_norm_g[None, :], name="mla_norm_kv_a")
    kr1, kr2 = _rope(kv_a[None, :, kv_rank:kv_rank + half], kv_a[None, :, kv_rank + half:], cos, sin, 1.0,
                     name="mla_rope_k")
    krx = jnp.concatenate([kr1.astype(BF16), kr2.astype(BF16), jnp.ones((seq, 1), BF16),
                           jnp.zeros((seq, n_kx - rope - 1), BF16)], axis=1)
    mka = _matmul(c_kv, w_kn, mode="nn", out_dtypes=(BF16,), placed=(krx, e_mk), name="mla_k")
    mva = _matmul(c_kv, w_mv, mode="nn", out_dtypes=(BF16,), placed=(krx, e_mv), name="mla_v")
    cq_pre = _matmul(h2, w_q_a, mode="nn", out_dtypes=(F32,), name="mla_q_a")
    (c_q,) = _rms_fwd(cq_pre, mla_q_a_norm_g, name="mla_norm_q_a")
    mqa = _rope_proj(c_q, w_qa_, w_qb_, cos_t, sin_t, mla_scale, mla_h, name="mla_q_b_rope")
    moa, mqb = _flash_fwd(mqa, mka, mva, mla_h, v_dim + 3, mla_dk, MLA_FWD_SUB_ROWS, name="mla_attn")
    x3 = _matmul(moa, w_mla_out, mode="nn", out_dtypes=(F32,), epilogue=add_res, extras=(x2,), name="mla_out")
    (h3,) = _rms_fwd(x3, norm_ffn_g[1:2], name="l1_norm_ffn")
    x4, r1, a1 = _ffn_fwd(x3, h3, w_up[1], w_down[1], "ffn1")

    loss_tile, dx4, d_final_g = _loss_head(x4, final_norm_g[None, :], tgt, name="loss_head")
    loss = lax.psum(loss_tile[0, 0], ("x", "y", "c"))

    gw = {}
    dx3, d_up1, d_down1, d_nf1 = _ffn_bwd(dx4, x3, h3, r1, a1, norm_ffn_g[1:2], w_up[1], w_down[1], "ffn1")

    d_moa = _matmul(dx3[1], w_mla_out, mode="nt", out_dtypes=(BF16,), epilogue=_delta_epilogue(mla_vwd, v_dim),
                    extras=(moa,), name="mla_d_ctx")
    gw["mla_w_out"] = _unpad_heads(_matmul(moa, dx3[1], mode="tn", out_dtypes=(F32,), name="mla_d_w_out"),
                                   mla_h, v_dim, 0)[None]
    d_mqa, d_mka, d_mva = _flash_bwd(mqb, mka, mva, d_moa, mla_h, MLA_BWD_HEADS_PER_STEP,
                                     name="mla_attn_bwd")
    d_c_q, d_w_qa_, d_w_qb_ = _rope_proj_bwd(c_q, d_mqa, w_qa_, w_qb_, cos_t, sin_t, mla_scale, mla_h,
                                             name="mla_q_b_rope_bwd")
    d_w_qa_ = _unpad_heads(d_w_qa_, mla_h, mla_dk, 1).reshape(q_rank, mla_h, mla_dk)
    d_w_qb_ = _unpad_heads(d_w_qb_, mla_h, mla_dk, 1).reshape(q_rank, mla_h, mla_dk)
    gw["mla_w_q_b"] = jnp.concatenate(
        [d_w_qa_[:, :, :nope], d_w_qa_[:, :, nope:nope + half] + d_w_qb_[:, :, nope + half:],
         d_w_qa_[:, :, nope + half:] - d_w_qb_[:, :, nope:nope + half]], axis=-1).reshape(1, q_rank, mla_h * mla_dk)
    (_, d_cq_pre), (d_q_a_g,) = _rms_bwd(cq_pre, [(mla_q_a_norm_g, d_c_q)], None, name="mla_d_norm_q_a")
    gw["mla_w_q_a"] = _matmul(h2, d_cq_pre, mode="tn", out_dtypes=(F32,), name="mla_d_w_q_a")[None]
    d_h2 = _matmul(d_cq_pre, w_q_a, mode="nt", out_dtypes=(F32,), name="mla_d_h")

    d_w_kn = _unpad_heads(_matmul(c_kv, d_mka, mode="tn", out_dtypes=(F32,), name="mla_d_w_k"), mla_h, nope, 1)
    d_w_mv = _unpad_heads(_matmul(c_kv, d_mva, mode="tn", out_dtypes=(F32,), name="mla_d_w_v"), mla_h, v_dim, 1)
    gw["mla_w_kv_b"] = jnp.concatenate([d_w_kn.reshape(kv_rank, mla_h, nope), d_w_mv.reshape(kv_rank, mla_h, v_dim)],
                                       axis=-1).reshape(kv_rank, mla_h * (nope + v_dim))
    d_c_kv_v = _matmul(d_mva, w_mv, mode="nt", out_dtypes=(F32,), name="mla_d_c_kv_v")
    d_c_kv = _matmul(d_mka, w_kn, mode="nt", out_dtypes=(F32,), epilogue=add_res, extras=(d_c_kv_v,),
                     name="mla_d_c_kv")
    (_, d_ckv_pre), (d_kv_a_g,) = _rms_bwd(kv_a, [(mla_kv_a_norm_g[None, :], d_c_kv)], None,
                                           name="mla_d_norm_kv_a")
    d_kr_u = _matmul(d_mka, e_kr_u, mode="nn", out_dtypes=(F32,), name="mla_d_k_rope_u")
    d_kr_v = _matmul(d_mka, e_kr_v, mode="nn", out_dtypes=(F32,), name="mla_d_k_rope_v")
    d_kr = _rope_mix(d_kr_u, d_kr_v, cos_k, sin_k, 1.0, 1, name="mla_rope_dk")
    d_kv_a = jnp.concatenate([d_ckv_pre, d_kr[:, :rope]], axis=1)
    gw["mla_w_kv_a"] = _matmul(src, d_kv_a, mode="tn", out_dtypes=(F32,), name="mla_d_w_kv_a")
    d_src = _matmul(d_kv_a, w_kv_a, mode="nt", out_dtypes=(F32,), name="mla_d_src")
    dx2, (d_kv_g, d_nm1) = _rms_bwd(x2, [(kv_norm_g[None, :], d_src), (norm_mix_g[1:2], d_h2)], dx3[0],
                                    name="l1_d_norm_kv_mix")

    dx1, d_up0, d_down0, d_nf0 = _ffn_bwd(dx2, x1, h1, r0, a0, norm_ffn_g[0:1], w_up[0], w_down[0], "ffn0")
    by_rows = lambda g: g.reshape(N_CHIPS, g.shape[0] // N_CHIPS, g.shape[1])
    gw_by_chip = {"ffn_w_up": jnp.concatenate([d_up0, d_up1], axis=1),
                  "ffn_w_down": jnp.concatenate([by_rows(d_down0), by_rows(d_down1)], axis=1)}

    d_foa = _matmul(dx1[1], w_fox_out, mode="nt", out_dtypes=(BF16,), epilogue=_delta_epilogue(fox_vwd, fox_dh),
                    extras=(foa,), name="fox_d_ctx")
    gw["fox_w_out"] = _unpad_heads(_matmul(foa, dx1[1], mode="tn", out_dtypes=(F32,), name="fox_d_w_out"),
                                   fox_h, fox_dh, 0)[None]
    fox_hps = FOX_BWD_HEADS_PER_STEP if fox_h % FOX_BWD_HEADS_PER_STEP == 0 else 1
    d_fqa, d_fka, d_fva, ds_rows, ds_cols = _flash_bwd(fqb, fka, fva, d_foa, fox_h, fox_hps, name="fox_attn_bwd",
                                                       sum_cols=(fox_dh, fox_dh + 3))
    d_cum = jnp.transpose(ds_rows - ds_cols, (1, 0, 2)).reshape(seq, fox_h)
    d_z = lax.cumsum(d_cum, axis=0, reverse=True) * jax.nn.sigmoid(-z)
    d_b_f = jnp.sum(d_z, axis=0)
    d_w_in = [_unpad_heads(_matmul(h0, g, mode="tn", out_dtypes=(F32,), name=f"fox_d_w_{tag}"), fox_h, fox_dh, 1)
              for tag, g in (("q", d_fqa), ("k", d_fka), ("v", d_fva))]
    d_w_gate = _matmul(h0, d_z, mode="tn", out_dtypes=(F32,), name="fox_d_w_gate")
    gw["fox_w_in"] = jnp.concatenate([d_w_in[0] * fox_scale, d_w_in[1], d_w_in[2], d_w_gate], axis=1)[None]
    d_h0 = _matmul(d_z, w_gate, mode="nt", out_dtypes=(F32,), name="fox_d_h_gate")
    for tag, g, w in (("q", d_fqa, w_fq), ("k", d_fka, w_fk), ("v", d_fva, w_fv)):
        d_h0 = _matmul(g, w, mode="nt", out_dtypes=(F32,), epilogue=add_res, extras=(d_h0,), name=f"fox_d_h_{tag}")
    (grad_x, _), (d_nm0,) = _rms_bwd(xs, [(norm_mix_g[0:1], d_h0)], dx1[0], name="l0_d_norm_mix")

    c_idx = lax.axis_index("c").astype(jnp.int32).reshape(1)
    parts = []
    for (n, ax), shape in zip(_BIG, shard_shapes):
        if n in gw_by_chip:
            g = gw_by_chip[n]
        else:
            g = gw[n]
            g = jnp.moveaxis(g.reshape(g.shape[:ax] + (N_CHIPS, shape[ax]) + g.shape[ax + 1:]), ax, 0)
            g = g.reshape(N_CHIPS, -1, shape[-1])
        parts.append(jnp.pad(g, ((0, 0), (0, _part_rows(shape) - g.shape[1]), (0, PACK_LANES - shape[-1]))))
    parts.append(jnp.zeros((N_CHIPS, rows - sum(p.shape[1] for p in parts), PACK_LANES), F32))
    g4 = jnp.concatenate(parts, axis=1)
    a4 = _sibling_swap_halves(g4, name="grads_to_sibling")
    s4 = _chip_sum(g4, a4, c_idx, name="grads_chip_sum")
    chip_idx = p_chip.astype(jnp.int32).reshape(1)
    r1 = _rs_first_hop(s4, name="grads_first_hop")
    t_own, onward = _rs_middle(s4, r1, chip_idx, name="grads_middle_sum")
    r2 = _rs_second_hop(onward, name="grads_second_hop")
    t_mine = _rs_last_add(t_own, r2, name="grads_last_sum")
    t_theirs = _sibling_swap(t_mine, name="grads_join_halves")
    is_south = lax.axis_index("c") == 0
    g_big = jnp.concatenate([jnp.where(is_south, t_mine, t_theirs), jnp.where(is_south, t_theirs, t_mine)],
                            axis=0)

    small_local = {"norm_mix_g": jnp.concatenate([d_nm0, d_nm1], axis=0),
                   "norm_ffn_g": jnp.concatenate([d_nf0, d_nf1], axis=0),
                   "fox_b_f": d_b_f[None, :], "kv_norm_g": d_kv_g[0], "mla_kv_a_norm_g": d_kv_a_g[0],
                   "mla_q_a_norm_g": d_q_a_g, "final_norm_g": d_final_g[0]}
    small_shapes = [w_in[n].shape for n in _SMALL]
    small_rows = sum(_part_rows(s, SMALL_PART_ROWS) for s in small_shapes)
    pack_small = lambda arrs: _pack(arrs, small_rows, F32, SMALL_PART_ROWS)
    g_small = _all_reduce_small(pack_small([small_local[n] for n in _SMALL]), name="grads_small")

    grads = dict(zip(big_names, _unpack(g_big, shard_shapes)))
    delta, new_m, new_v = {}, {}, {}
    for n, shape in zip(big_names, shard_shapes):
        flat = lambda a: a.reshape(-1, shape[-1])
        outs = _adamw(flat(w_in[n]), flat(grads[n]), flat(m_in[n]), flat(v_in[n]), name=f"adamw_{n}")
        delta[n], new_m[n], new_v[n] = (o.reshape(shape) for o in outs)
    sm_outs = _adamw(pack_small([w_in[n] for n in _SMALL]), g_small, pack_small([m_in[n] for n in _SMALL]),
                     pack_small([v_in[n] for n in _SMALL]), name="adamw_small")
    for res, packed in zip((grads, delta, new_m, new_v), (g_small,) + tuple(sm_outs)):
        res.update(zip(_SMALL, _unpack(packed, small_shapes, SMALL_PART_ROWS)))

    return (loss, grad_x[None], *[grads[n] for n in _WEIGHTS], *[delta[n] for n in _WEIGHTS],
            *[new_m[n] for n in _WEIGHTS], *[new_v[n] for n in _WEIGHTS])
```
